```python
import jax, jax.numpy as jnp
from jax import lax
import numpy as np

D_MODEL = 1024
BATCH = 8
SEQ = 8192
DEPTH = 4

HEAD_DIM = 64
EPS = 1e-6
A_Q_HEADS = 12
A_KV_HEADS = 2
A_GROUP = A_Q_HEADS // A_KV_HEADS
WINDOW = 128
A_Q_W = A_Q_HEADS * HEAD_DIM
A_KV_W = A_KV_HEADS * HEAD_DIM
B_GROUPS = 6
B_GROUP_DIM = 128
B_WIDTH = B_GROUPS * B_GROUP_DIM
CHUNK = 128
N_MEM = 256
MEM_HEADS = 4
MEM_WIDTH = MEM_HEADS * HEAD_DIM
MIX_WIDTH = A_Q_W + MEM_WIDTH
A_IN = A_Q_W + 2 * A_KV_W + MEM_WIDTH
B_IN = 2 * B_WIDTH + MEM_WIDTH
D_FF = -(-8 * D_MODEL // (3 * 256)) * 256
N_A = (DEPTH + 1) // 2
N_B = DEPTH // 2

kernel_name = "hybrid_swa_sink_gmlp_memxattn_trunk"


def rms_norm(x, g):
    xf = x.astype(jnp.float32)
    y = xf * lax.rsqrt(jnp.mean(xf * xf, axis=-1, keepdims=True) + EPS)
    return (y * g.astype(jnp.float32)).astype(x.dtype)


def sliding_window_attention(q, k, v, sinks):
    b, s, _, hd = q.shape
    nb = s // WINDOW
    qb = q.reshape(b, nb, WINDOW, A_KV_HEADS, A_GROUP, hd)

    def with_prev(t):
        tb = t.reshape(b, nb, WINDOW, A_KV_HEADS, hd)
        prev = jnp.pad(tb[:, :-1], ((0, 0), (1, 0), (0, 0), (0, 0), (0, 0)))
        return jnp.concatenate([prev, tb], axis=2)

    kb, vb = with_prev(k), with_prev(v)
    scores = jnp.einsum('bnqhgd,bnkhd->bnhgqk', qb, kb).astype(jnp.float32) * (hd ** -0.5)
    qi = jnp.arange(WINDOW)[:, None]
    kj = jnp.arange(2 * WINDOW)[None, :]
    rel = qi + WINDOW - kj
    band = (rel >= 0) & (rel < WINDOW)
    not_pad = (jnp.arange(nb)[:, None, None] > 0) | (kj[None] >= WINDOW)
    mask = band[None] & not_pad
    scores = jnp.where(mask[None, :, None, None], scores, -jnp.inf)
    sink = sinks.astype(jnp.float32).reshape(A_KV_HEADS, A_GROUP)[None, None, :, :, None, None]
    m = jnp.maximum(jnp.max(scores, axis=-1, keepdims=True), sink)
    e = jnp.exp(scores - m)
    p = e / (jnp.sum(e, axis=-1, keepdims=True) + jnp.exp(sink - m))
    o = jnp.einsum('bnhgqk,bnkhd->bnqhgd', p.astype(v.dtype), vb)
    return o.reshape(b, s, A_Q_HEADS * hd)


def chunked_spatial_gating(z, w_s, b_s, ln_g, ln_b):
    b, s, _ = z.shape
    u, v = jnp.split(z, 2, axis=-1)
    v = v.reshape(b, s // CHUNK, CHUNK, B_GROUPS, B_GROUP_DIM)
    vf = v.astype(jnp.float32)
    mu = jnp.mean(vf, axis=-1, keepdims=True)
    var = jnp.mean(jnp.square(vf - mu), axis=-1, keepdims=True)
    vn = ((vf - mu) * lax.rsqrt(var + EPS) * ln_g.astype(jnp.float32) + ln_b.astype(jnp.float32)).astype(z.dtype)
    causal = jnp.tril(jnp.ones((CHUNK, CHUNK), dtype=bool))
    w = jnp.where(causal[None], w_s, jnp.zeros_like(w_s)).astype(vn.dtype)
    sv = jnp.einsum('gts,bnsgd->bntgd', w, vn) + b_s.T.astype(vn.dtype)[None, None, :, :, None]
    return u * sv.reshape(b, s, B_WIDTH).astype(z.dtype)


def memory_attention(q_mem, mem_n, w_kv):
    b, nm, _ = mem_n.shape
    kv = mem_n @ w_kv
    k, v = jnp.split(kv, 2, axis=-1)
    k = k.reshape(b, nm, MEM_HEADS, HEAD_DIM)
    v = v.reshape(b, nm, MEM_HEADS, HEAD_DIM)
    s = jnp.einsum('bshd,bmhd->bhsm', q_mem, k).astype(jnp.float32) * (HEAD_DIM ** -0.5)
    p = jax.nn.softmax(s, axis=-1).astype(v.dtype)
    o = jnp.einsum('bhsm,bmhd->bshd', p, v)
    return o.reshape(q_mem.shape[0], q_mem.shape[1], MEM_WIDTH)


def _fwd_setup_inputs(seed: int = 0) -> dict:
    key = jax.random.key(seed)
    ks = jax.random.split(key, 20)
    f32 = jnp.float32

    def nrm(k, shape, fan_in):
        return jax.random.normal(k, shape, f32) * (fan_in ** -0.5)

    def gain(k, shape):
        return 1.0 + 0.05 * jax.random.normal(k, shape, f32)

    return {
        "x": jax.random.normal(ks[0], (BATCH, SEQ, D_MODEL), f32),
        "mem": jax.random.normal(ks[1], (BATCH, N_MEM, D_MODEL), f32),
        "mem_norm_g": gain(ks[2], (D_MODEL,)),
        "mix_norm_g": gain(ks[3], (DEPTH, D_MODEL)),
        "ffn_norm_g": gain(ks[4], (DEPTH, D_MODEL)),
        "final_norm_g": gain(ks[5], (D_MODEL,)),
        "a_w_in": nrm(ks[6], (N_A, D_MODEL, A_IN), D_MODEL),
        "a_sinks": 0.5 * jax.random.normal(ks[7], (N_A, A_Q_HEADS), f32),
        "a_w_out": nrm(ks[8], (N_A, MIX_WIDTH, D_MODEL), MIX_WIDTH),
        "b_w_in": nrm(ks[9], (N_B, D_MODEL, B_IN), D_MODEL),
        "b_w_s": nrm(ks[10], (N_B, B_GROUPS, CHUNK, CHUNK), CHUNK),
        "b_bias_s": 1.0 + 0.05 * jax.random.normal(ks[11], (N_B, B_GROUPS, CHUNK), f32),
        "b_ln_g": gain(ks[12], (N_B, B_GROUPS, B_GROUP_DIM)),
        "b_ln_b": 0.02 * jax.random.normal(ks[13], (N_B, B_GROUPS, B_GROUP_DIM), f32),
        "b_w_out": nrm(ks[14], (N_B, MIX_WIDTH, D_MODEL), MIX_WIDTH),
        "w_mem_kv": nrm(ks[15], (DEPTH, D_MODEL, 2 * MEM_WIDTH), D_MODEL),
        "w_gate_up": nrm(ks[16], (DEPTH, D_MODEL, 2 * D_FF), D_MODEL),
        "w_down": nrm(ks[17], (DEPTH, D_FF, D_MODEL), D_FF),
    }


def _fwd_reference(x, mem, mem_norm_g, mix_norm_g, ffn_norm_g, final_norm_g,
              a_w_in, a_sinks, a_w_out,
              b_w_in, b_w_s, b_bias_s, b_ln_g, b_ln_b, b_w_out,
              w_mem_kv, w_gate_up, w_down):
    b, s, _ = x.shape
    mem_n = rms_norm(mem, mem_norm_g)
    h = x
    for i in range(DEPTH):
        j = i // 2
        xn = rms_norm(h, mix_norm_g[i])
        if i % 2 == 0:
            proj = xn @ a_w_in[j]
            q, k, v, q_mem = jnp.split(proj, [A_Q_W, A_Q_W + A_KV_W, A_Q_W + 2 * A_KV_W], axis=-1)
            mix = sliding_window_attention(
                q.reshape(b, s, A_Q_HEADS, HEAD_DIM),
                k.reshape(b, s, A_KV_HEADS, HEAD_DIM),
                v.reshape(b, s, A_KV_HEADS, HEAD_DIM),
                a_sinks[j])
            w_out = a_w_out[j]
        else:
            proj = xn @ b_w_in[j]
            z, q_mem = jnp.split(proj, [2 * B_WIDTH], axis=-1)
            mix = chunked_spatial_gating(jax.nn.gelu(z), b_w_s[j], b_bias_s[j], b_ln_g[j], b_ln_b[j])
            w_out = b_w_out[j]
        mem_out = memory_attention(q_mem.reshape(b, s, MEM_HEADS, HEAD_DIM), mem_n, w_mem_kv[i])
        h = h + jnp.concatenate([mix, mem_out.astype(mix.dtype)], axis=-1) @ w_out
        hn = rms_norm(h, ffn_norm_g[i])
        gate, up = jnp.split(hn @ w_gate_up[i], 2, axis=-1)
        h = h + (jax.nn.silu(gate) * up) @ w_down[i]
    return rms_norm(h, final_norm_g)


import jax as _jax
import jax.numpy as _jnp

TWIN_FORMAT = 'train_step'
FWD_PARAMS = ['x', 'mem', 'mem_norm_g', 'mix_norm_g', 'ffn_norm_g', 'final_norm_g', 'a_w_in', 'a_sinks', 'a_w_out', 'b_w_in', 'b_w_s', 'b_bias_s', 'b_ln_g', 'b_ln_b', 'b_w_out', 'w_mem_kv', 'w_gate_up', 'w_down']
TWIN_WEIGHTS = ['mem_norm_g', 'mix_norm_g', 'ffn_norm_g', 'final_norm_g', 'a_w_in', 'a_sinks', 'a_w_out', 'b_w_in', 'b_w_s', 'b_bias_s', 'b_ln_g', 'b_ln_b', 'b_w_out', 'w_mem_kv', 'w_gate_up', 'w_down']
TWIN_DIFF_INPUT = 'x'
TWIN_INPUTS = ['x', 'mem', 'mem_norm_g', 'mix_norm_g', 'ffn_norm_g', 'final_norm_g', 'a_w_in', 'a_sinks', 'a_w_out', 'b_w_in', 'b_w_s', 'b_bias_s', 'b_ln_g', 'b_ln_b', 'b_w_out', 'w_mem_kv', 'w_gate_up', 'w_down', 'loss_target', 'm_mem_norm_g', 'm_mix_norm_g', 'm_ffn_norm_g', 'm_final_norm_g', 'm_a_w_in', 'm_a_sinks', 'm_a_w_out', 'm_b_w_in', 'm_b_w_s', 'm_b_bias_s', 'm_b_ln_g', 'm_b_ln_b', 'm_b_w_out', 'm_w_mem_kv', 'm_w_gate_up', 'm_w_down', 'v_mem_norm_g', 'v_mix_norm_g', 'v_ffn_norm_g', 'v_final_norm_g', 'v_a_w_in', 'v_a_sinks', 'v_a_w_out', 'v_b_w_in', 'v_b_w_s', 'v_b_bias_s', 'v_b_ln_g', 'v_b_ln_b', 'v_b_w_out', 'v_w_mem_kv', 'v_w_gate_up', 'v_w_down']
TWIN_OUTPUTS = ['loss', 'grad_x', 'grad_mem_norm_g', 'grad_mix_norm_g', 'grad_ffn_norm_g', 'grad_final_norm_g', 'grad_a_w_in', 'grad_a_sinks', 'grad_a_w_out', 'grad_b_w_in', 'grad_b_w_s', 'grad_b_bias_s', 'grad_b_ln_g', 'grad_b_ln_b', 'grad_b_w_out', 'grad_w_mem_kv', 'grad_w_gate_up', 'grad_w_down', 'delta_mem_norm_g', 'delta_mix_norm_g', 'delta_ffn_norm_g', 'delta_final_norm_g', 'delta_a_w_in', 'delta_a_sinks', 'delta_a_w_out', 'delta_b_w_in', 'delta_b_w_s', 'delta_b_bias_s', 'delta_b_ln_g', 'delta_b_ln_b', 'delta_b_w_out', 'delta_w_mem_kv', 'delta_w_gate_up', 'delta_w_down', 'new_m_mem_norm_g', 'new_m_mix_norm_g', 'new_m_ffn_norm_g', 'new_m_final_norm_g', 'new_m_a_w_in', 'new_m_a_sinks', 'new_m_a_w_out', 'new_m_b_w_in', 'new_m_b_w_s', 'new_m_b_bias_s', 'new_m_b_ln_g', 'new_m_b_ln_b', 'new_m_b_w_out', 'new_m_w_mem_kv', 'new_m_w_gate_up', 'new_m_w_down', 'new_v_mem_norm_g', 'new_v_mix_norm_g', 'new_v_ffn_norm_g', 'new_v_final_norm_g', 'new_v_a_w_in', 'new_v_a_sinks', 'new_v_a_w_out', 'new_v_b_w_in', 'new_v_b_w_s', 'new_v_b_bias_s', 'new_v_b_ln_g', 'new_v_b_ln_b', 'new_v_b_w_out', 'new_v_w_mem_kv', 'new_v_w_gate_up', 'new_v_w_down']
TWIN_LEAF_KINDS = {'loss': 'loss', 'grad_x': 'grad_x', 'grad_mem_norm_g': 'grad_w', 'grad_mix_norm_g': 'grad_w', 'grad_ffn_norm_g': 'grad_w', 'grad_final_norm_g': 'grad_w', 'grad_a_w_in': 'grad_w', 'grad_a_sinks': 'grad_w', 'grad_a_w_out': 'grad_w', 'grad_b_w_in': 'grad_w', 'grad_b_w_s': 'grad_w', 'grad_b_bias_s': 'grad_w', 'grad_b_ln_g': 'grad_w', 'grad_b_ln_b': 'grad_w', 'grad_b_w_out': 'grad_w', 'grad_w_mem_kv': 'grad_w', 'grad_w_gate_up': 'grad_w', 'grad_w_down': 'grad_w', 'delta_mem_norm_g': 'delta_w', 'delta_mix_norm_g': 'delta_w', 'delta_ffn_norm_g': 'delta_w', 'delta_final_norm_g': 'delta_w', 'delta_a_w_in': 'delta_w', 'delta_a_sinks': 'delta_w', 'delta_a_w_out': 'delta_w', 'delta_b_w_in': 'delta_w', 'delta_b_w_s': 'delta_w', 'delta_b_bias_s': 'delta_w', 'delta_b_ln_g': 'delta_w', 'delta_b_ln_b': 'delta_w', 'delta_b_w_out': 'delta_w', 'delta_w_mem_kv': 'delta_w', 'delta_w_gate_up': 'delta_w', 'delta_w_down': 'delta_w', 'new_m_mem_norm_g': 'new_m', 'new_m_mix_norm_g': 'new_m', 'new_m_ffn_norm_g': 'new_m', 'new_m_final_norm_g': 'new_m', 'new_m_a_w_in': 'new_m', 'new_m_a_sinks': 'new_m', 'new_m_a_w_out': 'new_m', 'new_m_b_w_in': 'new_m', 'new_m_b_w_s': 'new_m', 'new_m_b_bias_s': 'new_m', 'new_m_b_ln_g': 'new_m', 'new_m_b_ln_b': 'new_m', 'new_m_b_w_out': 'new_m', 'new_m_w_mem_kv': 'new_m', 'new_m_w_gate_up': 'new_m', 'new_m_w_down': 'new_m', 'new_v_mem_norm_g': 'new_v', 'new_v_mix_norm_g': 'new_v', 'new_v_ffn_norm_g': 'new_v', 'new_v_final_norm_g': 'new_v', 'new_v_a_w_in': 'new_v', 'new_v_a_sinks': 'new_v', 'new_v_a_w_out': 'new_v', 'new_v_b_w_in': 'new_v', 'new_v_b_w_s': 'new_v', 'new_v_b_bias_s': 'new_v', 'new_v_b_ln_g': 'new_v', 'new_v_b_ln_b': 'new_v', 'new_v_b_w_out': 'new_v', 'new_v_w_mem_kv': 'new_v', 'new_v_w_gate_up': 'new_v', 'new_v_w_down': 'new_v'}


def _forward(args):
    return _fwd_reference(*[args[k] for k in FWD_PARAMS])


def _output_shape():
    out = _jax.eval_shape(lambda: _forward(_fwd_setup_inputs(0)))
    return out.shape, out.dtype

N_MICROBATCH = 1
ADAM_LR = 0.001
ADAM_B1 = 0.9
ADAM_B2 = 0.999
ADAM_EPS = 1e-08
ADAM_WD = 0.01
ADAM_STEP = 10
PER_EXAMPLE_BATCH_AXIS = {'x': 0, 'mem': 0, 'loss_target': 0}
SHARED_INPUTS = []
_WEIGHT_DTYPES = {'mem_norm_g': _jnp.float32, 'mix_norm_g': _jnp.float32, 'ffn_norm_g': _jnp.float32, 'final_norm_g': _jnp.float32, 'a_w_in': _jnp.float32, 'a_sinks': _jnp.float32, 'a_w_out': _jnp.float32, 'b_w_in': _jnp.float32, 'b_w_s': _jnp.float32, 'b_bias_s': _jnp.float32, 'b_ln_g': _jnp.float32, 'b_ln_b': _jnp.float32, 'b_w_out': _jnp.float32, 'w_mem_kv': _jnp.float32, 'w_gate_up': _jnp.float32, 'w_down': _jnp.float32}
MOMENT_SCALE = {'mem_norm_g': 4.307418e-02, 'mix_norm_g': 1.354355e-01, 'ffn_norm_g': 1.861548e-01, 'final_norm_g': 6.411014e+01, 'a_w_in': 8.132044e-02, 'a_sinks': 4.269605e-02, 'a_w_out': 6.262191e-02, 'b_w_in': 1.283075e-01, 'b_w_s': 8.925429e-02, 'b_bias_s': 1.349599e-01, 'b_ln_g': 9.694306e-02, 'b_ln_b': 9.029160e-02, 'b_w_out': 1.679573e-01, 'w_mem_kv': 2.932876e-02, 'w_gate_up': 7.686508e-02, 'w_down': 1.260717e-01}


def _to_microbatches(a, axis):
    t = _jnp.moveaxis(a, axis, 0)
    t = t.reshape((N_MICROBATCH, t.shape[0] // N_MICROBATCH) + t.shape[1:])
    return _jnp.moveaxis(t, 1, axis + 1)


def setup_inputs(seed: int = 0) -> dict:
    inp = _fwd_setup_inputs(seed)
    key = _jax.random.fold_in(_jax.random.key(seed), 7919)
    shape, _ = _output_shape()
    out = dict(inp)
    out["loss_target"] = _jax.random.normal(_jax.random.fold_in(key, 0), shape, _jnp.float32)
    for i, name in enumerate(TWIN_WEIGHTS):
        w = inp[name].astype(_jnp.float32)
        if MOMENT_SCALE is None:
            s = _jnp.sqrt(_jnp.mean(_jnp.square(w)) + 1e-30)
        else:
            s = MOMENT_SCALE[name]
        km, kv = _jax.random.split(_jax.random.fold_in(key, i + 1))
        out[name] = w
        out["m_" + name] = s * _jax.random.normal(km, w.shape, _jnp.float32)
        out["v_" + name] = (s * s) * _jax.random.uniform(kv, w.shape, _jnp.float32, 0.5, 1.5)
    if N_MICROBATCH > 1:
        for name, axis in PER_EXAMPLE_BATCH_AXIS.items():
            out[name] = _to_microbatches(out[name], axis)
    return {'x': out['x'], 'mem': out['mem'], 'mem_norm_g': out['mem_norm_g'], 'mix_norm_g': out['mix_norm_g'], 'ffn_norm_g': out['ffn_norm_g'], 'final_norm_g': out['final_norm_g'], 'a_w_in': out['a_w_in'], 'a_sinks': out['a_sinks'], 'a_w_out': out['a_w_out'], 'b_w_in': out['b_w_in'], 'b_w_s': out['b_w_s'], 'b_bias_s': out['b_bias_s'], 'b_ln_g': out['b_ln_g'], 'b_ln_b': out['b_ln_b'], 'b_w_out': out['b_w_out'], 'w_mem_kv': out['w_mem_kv'], 'w_gate_up': out['w_gate_up'], 'w_down': out['w_down'], 'loss_target': out['loss_target'], 'm_mem_norm_g': out['m_mem_norm_g'], 'm_mix_norm_g': out['m_mix_norm_g'], 'm_ffn_norm_g': out['m_ffn_norm_g'], 'm_final_norm_g': out['m_final_norm_g'], 'm_a_w_in': out['m_a_w_in'], 'm_a_sinks': out['m_a_sinks'], 'm_a_w_out': out['m_a_w_out'], 'm_b_w_in': out['m_b_w_in'], 'm_b_w_s': out['m_b_w_s'], 'm_b_bias_s': out['m_b_bias_s'], 'm_b_ln_g': out['m_b_ln_g'], 'm_b_ln_b': out['m_b_ln_b'], 'm_b_w_out': out['m_b_w_out'], 'm_w_mem_kv': out['m_w_mem_kv'], 'm_w_gate_up': out['m_w_gate_up'], 'm_w_down': out['m_w_down'], 'v_mem_norm_g': out['v_mem_norm_g'], 'v_mix_norm_g': out['v_mix_norm_g'], 'v_ffn_norm_g': out['v_ffn_norm_g'], 'v_final_norm_g': out['v_final_norm_g'], 'v_a_w_in': out['v_a_w_in'], 'v_a_sinks': out['v_a_sinks'], 'v_a_w_out': out['v_a_w_out'], 'v_b_w_in': out['v_b_w_in'], 'v_b_w_s': out['v_b_w_s'], 'v_b_bias_s': out['v_b_bias_s'], 'v_b_ln_g': out['v_b_ln_g'], 'v_b_ln_b': out['v_b_ln_b'], 'v_b_w_out': out['v_b_w_out'], 'v_w_mem_kv': out['v_w_mem_kv'], 'v_w_gate_up': out['v_w_gate_up'], 'v_w_down': out['v_w_down']}


def _loss(weights, diff, rest, loss_target):
    with _jax.named_scope("forward"):
        args = {**rest, TWIN_DIFF_INPUT: diff, **{k: w.astype(_WEIGHT_DTYPES[k]) for k, w in weights.items()}}
        y = _forward(args)
    with _jax.named_scope("loss_head"):
        err = _jnp.square(y.astype(_jnp.float32) - loss_target)
        return 0.5 * _jnp.sum(_jnp.mean(err, axis=-1)) if err.ndim else 0.5 * err


def _adamw(w, g, m, v):
    m = ADAM_B1 * m + (1.0 - ADAM_B1) * g
    v = ADAM_B2 * v + (1.0 - ADAM_B2) * _jnp.square(g)
    m_hat = m / (1.0 - ADAM_B1 ** ADAM_STEP)
    v_hat = v / (1.0 - ADAM_B2 ** ADAM_STEP)
    delta = -ADAM_LR * (m_hat / (_jnp.sqrt(v_hat) + ADAM_EPS) + ADAM_WD * w)
    return delta, m, v


def reference(x, mem, mem_norm_g, mix_norm_g, ffn_norm_g, final_norm_g, a_w_in, a_sinks, a_w_out, b_w_in, b_w_s, b_bias_s, b_ln_g, b_ln_b, b_w_out, w_mem_kv, w_gate_up, w_down, loss_target, m_mem_norm_g, m_mix_norm_g, m_ffn_norm_g, m_final_norm_g, m_a_w_in, m_a_sinks, m_a_w_out, m_b_w_in, m_b_w_s, m_b_bias_s, m_b_ln_g, m_b_ln_b, m_b_w_out, m_w_mem_kv, m_w_gate_up, m_w_down, v_mem_norm_g, v_mix_norm_g, v_ffn_norm_g, v_final_norm_g, v_a_w_in, v_a_sinks, v_a_w_out, v_b_w_in, v_b_w_s, v_b_bias_s, v_b_ln_g, v_b_ln_b, v_b_w_out, v_w_mem_kv, v_w_gate_up, v_w_down):
    given = dict(x=x, mem=mem, mem_norm_g=mem_norm_g, mix_norm_g=mix_norm_g, ffn_norm_g=ffn_norm_g, final_norm_g=final_norm_g, a_w_in=a_w_in, a_sinks=a_sinks, a_w_out=a_w_out, b_w_in=b_w_in, b_w_s=b_w_s, b_bias_s=b_bias_s, b_ln_g=b_ln_g, b_ln_b=b_ln_b, b_w_out=b_w_out, w_mem_kv=w_mem_kv, w_gate_up=w_gate_up, w_down=w_down, loss_target=loss_target, m_mem_norm_g=m_mem_norm_g, m_mix_norm_g=m_mix_norm_g, m_ffn_norm_g=m_ffn_norm_g, m_final_norm_g=m_final_norm_g, m_a_w_in=m_a_w_in, m_a_sinks=m_a_sinks, m_a_w_out=m_a_w_out, m_b_w_in=m_b_w_in, m_b_w_s=m_b_w_s, m_b_bias_s=m_b_bias_s, m_b_ln_g=m_b_ln_g, m_b_ln_b=m_b_ln_b, m_b_w_out=m_b_w_out, m_w_mem_kv=m_w_mem_kv, m_w_gate_up=m_w_gate_up, m_w_down=m_w_down, v_mem_norm_g=v_mem_norm_g, v_mix_norm_g=v_mix_norm_g, v_ffn_norm_g=v_ffn_norm_g, v_final_norm_g=v_final_norm_g, v_a_w_in=v_a_w_in, v_a_sinks=v_a_sinks, v_a_w_out=v_a_w_out, v_b_w_in=v_b_w_in, v_b_w_s=v_b_w_s, v_b_bias_s=v_b_bias_s, v_b_ln_g=v_b_ln_g, v_b_ln_b=v_b_ln_b, v_b_w_out=v_b_w_out, v_w_mem_kv=v_w_mem_kv, v_w_gate_up=v_w_gate_up, v_w_down=v_w_down)
    weights = {n: given[n] for n in TWIN_WEIGHTS}
    shared = {n: given[n] for n in SHARED_INPUTS}
    per_example = {n: given[n] for n in ['x', 'mem']}
    grad_fn = _jax.value_and_grad(_loss, argnums=(0, 1))

    def one_microbatch(ex, loss_target):
        ex = dict(ex)
        diff = ex.pop(TWIN_DIFF_INPUT)
        return grad_fn(weights, diff, {**shared, **ex}, loss_target)

    if N_MICROBATCH == 1:
        loss, (grad_w, grad_x) = one_microbatch(per_example, given["loss_target"])
    else:
        def body(carry, xs):
            loss_sum, grad_sum = carry
            l_k, (gw_k, gx_k) = one_microbatch(xs[0], xs[1])
            with _jax.named_scope("update"):
                return (loss_sum + l_k, _jax.tree.map(_jnp.add, grad_sum, gw_k)), gx_k

        init = (_jnp.zeros((), _jnp.float32), _jax.tree.map(_jnp.zeros_like, weights))
        (loss, grad_w), grad_x = _jax.lax.scan(body, init, (per_example, given["loss_target"]))
    with _jax.named_scope("update"):
        delta_w, new_m, new_v = {}, {}, {}
        for n in TWIN_WEIGHTS:
            delta_w[n], new_m[n], new_v[n] = _adamw(weights[n], grad_w[n], given["m_" + n], given["v_" + n])
    return (loss, grad_x, *[grad_w[n] for n in TWIN_WEIGHTS], *[delta_w[n] for n in TWIN_WEIGHTS],
            *[new_m[n] for n in TWIN_WEIGHTS], *[new_v[n] for n in TWIN_WEIGHTS])
```

```python
import jax
import jax.numpy as jnp
from jax import lax
from jax.experimental import pallas as pl
from jax.experimental.pallas import tpu as pltpu

F32, BF16 = jnp.float32, jnp.bfloat16
EPS = 1e-6
HEAD_DIM = 64
Q_HEADS, KV_HEADS, GROUP = 12, 2, 6
WINDOW = 128
MEM_HEADS = 4
B_GROUPS = 6
Q_W, KV_W, MEM_W, B_W = 768, 128, 256, 768
SCALE = HEAD_DIM ** -0.5
NEG = -1e30
ADAM_LR, ADAM_B1, ADAM_B2, ADAM_EPS, ADAM_WD, ADAM_STEP = 0.001, 0.9, 0.999, 1e-08, 0.01, 10
V7X_VMEM_LIMIT_BYTES = 48 * 1024 * 1024
MESH = pl.DeviceIdType.MESH
HBM_SPEC = pl.BlockSpec(memory_space=pltpu.HBM)
VMEM_SPEC = pl.BlockSpec(memory_space=pltpu.VMEM)


def _cp(*sem):
    return pltpu.CompilerParams(dimension_semantics=sem or None, vmem_limit_bytes=V7X_VMEM_LIMIT_BYTES)


def _tile(n, cands):
    for t in cands:
        if n % t == 0:
            return t
    return n


def _sds(shape, dtype):
    return jax.ShapeDtypeStruct(tuple(shape), dtype)


def _dot(a, b, ca, cb):
    return lax.dot_general(a, b, (((ca,), (cb,)), ((), ())), preferred_element_type=F32)


def _rms(x, g):
    return x * lax.rsqrt(jnp.mean(x * x, axis=-1, keepdims=True) + EPS) * g


def rmsnorm_fwd(h, g, name):
    t, d = h.shape
    tm = _tile(t, (512, 256, 128))

    def body(h_ref, g_ref, o_ref):
        o_ref[...] = _rms(h_ref[...], g_ref[...]).astype(o_ref.dtype)

    return pl.pallas_call(
        body, name=name, grid=(t // tm,),
        in_specs=[pl.BlockSpec((tm, d), lambda i: (i, 0)), pl.BlockSpec((1, d), lambda i: (0, 0))],
        out_specs=pl.BlockSpec((tm, d), lambda i: (i, 0)),
        out_shape=_sds((t, d), BF16), compiler_params=_cp("parallel"))(h, g)


def rmsnorm_bwd(h, g, dxn, dres, name):
    t, d = h.shape
    tm = _tile(t, (512, 256, 128))

    def body(h_ref, g_ref, dxn_ref, dres_ref, dh_ref, dg_ref):
        _, vjp = jax.vjp(_rms, h_ref[...], g_ref[...])
        dh, dg = vjp(dxn_ref[...].astype(F32))
        dh_ref[...] = dres_ref[...] + dh

        @pl.when(pl.program_id(0) == 0)
        def _():
            dg_ref[...] = jnp.zeros_like(dg_ref)

        dg_ref[...] += dg

    row = pl.BlockSpec((tm, d), lambda i: (i, 0))
    vec = pl.BlockSpec((1, d), lambda i: (0, 0))
    return pl.pallas_call(
        body, name=name, grid=(t // tm,), in_specs=[row, vec, row, row], out_specs=[row, vec],
        out_shape=[_sds((t, d), F32), _sds((1, d), F32)], compiler_params=_cp("arbitrary"))(h, g, dxn, dres)


def loss_head(h, g, tgt, name):
    t, d = h.shape
    tm = _tile(t, (512, 256, 128))

    def body(h_ref, g_ref, t_ref, l_ref, dh_ref, dg_ref):
        y, vjp = jax.vjp(_rms, h_ref[...], g_ref[...])
        err = y - t_ref[...]
        dh, dg = vjp(err * (1.0 / d))
        dh_ref[...] = dh
        part = 0.5 * jnp.sum(jnp.mean(err * err, axis=-1, keepdims=True), axis=0, keepdims=True)

        @pl.when(pl.program_id(0) == 0)
        def _():
            dg_ref[...] = jnp.zeros_like(dg_ref)
            l_ref[...] = jnp.zeros_like(l_ref)

        dg_ref[...] += dg
        l_ref[...] += part

    row = pl.BlockSpec((tm, d), lambda i: (i, 0))
    vec = pl.BlockSpec((1, d), lambda i: (0, 0))
    one = pl.BlockSpec((1, 1), lambda i: (0, 0))
    return pl.pallas_call(
        body, name=name, grid=(t // tm,), in_specs=[row, vec, row], out_specs=[one, row, vec],
        out_shape=[_sds((1, 1), F32), _sds((t, d), F32), _sds((1, d), F32)], compiler_params=_cp("arbitrary"))(h, g, tgt)


def matmul(a, b, mode, out_dtype, name, res=None, tm=1024, tn=512, tk=2816):
    if mode == "nn":
        (m, k), n = a.shape, b.shape[1]
    elif mode == "nt":
        (m, k), n = a.shape, b.shape[0]
    else:
        (k, m), n = a.shape, b.shape[1]
    tm = _tile(m, (tm, 1408, 1024, 512, 256, 128))
    tn = _tile(n, (tn, 1408, 896, 640, 512, 256, 128))
    tk = _tile(k, (tk, 2816, 1792, 1280, 1024, 512, 256, 128))
    nk = k // tk
    if mode == "nn":
        a_spec = pl.BlockSpec((tm, tk), lambda i, j, kk: (i, kk))
        b_spec = pl.BlockSpec((tk, tn), lambda i, j, kk: (kk, j))
        ca, cb = 1, 0
    elif mode == "nt":
        a_spec = pl.BlockSpec((tm, tk), lambda i, j, kk: (i, kk))
        b_spec = pl.BlockSpec((tn, tk), lambda i, j, kk: (j, kk))
        ca, cb = 1, 1
    else:
        a_spec = pl.BlockSpec((tk, tm), lambda i, j, kk: (kk, i))
        b_spec = pl.BlockSpec((tk, tn), lambda i, j, kk: (kk, j))
        ca, cb = 0, 0
    o_spec = pl.BlockSpec((tm, tn), lambda i, j, kk: (i, j))
    has_res = res is not None

    def body(a_ref, b_ref, *rest):
        o_ref = rest[1] if has_res else rest[0]
        p = _dot(a_ref[...].astype(BF16), b_ref[...].astype(BF16), ca, cb)
        if nk == 1:
            if has_res:
                p = p + rest[0][...]
            o_ref[...] = p.astype(o_ref.dtype)
        else:
            acc_ref = rest[-1]
            kk = pl.program_id(2)

            @pl.when(kk == 0)
            def _():
                acc_ref[...] = p

            @pl.when(kk > 0)
            def _():
                acc_ref[...] += p

            @pl.when(kk == nk - 1)
            def _():
                r = acc_ref[...]
                if has_res:
                    r = r + rest[0][...]
                o_ref[...] = r.astype(o_ref.dtype)

    return pl.pallas_call(
        body, name=name, grid=(m // tm, n // tn, nk),
        in_specs=[a_spec, b_spec] + ([o_spec] if has_res else []), out_specs=o_spec,
        out_shape=_sds((m, n), out_dtype), scratch_shapes=[pltpu.VMEM((tm, tn), F32)] if nk > 1 else [],
        compiler_params=_cp("parallel", "parallel", "arbitrary"))(*((a, b, res) if has_res else (a, b)))


def _swiglu(gate, up):
    return gate / (1.0 + jnp.exp(-gate)) * up


def swiglu_fwd(gu, name):
    t, f2 = gu.shape
    f = f2 // 2
    tm = _tile(t, (512, 256, 128))

    def body(g_ref, u_ref, o_ref):
        o_ref[...] = _swiglu(g_ref[...].astype(F32), u_ref[...].astype(F32)).astype(o_ref.dtype)

    return pl.pallas_call(
        body, name=name, grid=(t // tm,),
        in_specs=[pl.BlockSpec((tm, f), lambda i: (i, 0)), pl.BlockSpec((tm, f), lambda i: (i, 1))],
        out_specs=pl.BlockSpec((tm, f), lambda i: (i, 0)),
        out_shape=_sds((t, f), BF16), compiler_params=_cp("parallel"))(gu, gu)


def swiglu_bwd(gu, dact, name):
    t, f2 = gu.shape
    f = f2 // 2
    tm = _tile(t, (256, 128))

    def body(g_ref, u_ref, d_ref, o_ref):
        _, vjp = jax.vjp(_swiglu, g_ref[...].astype(F32), u_ref[...].astype(F32))
        dg, du = vjp(d_ref[...].astype(F32))
        o_ref[:, :f] = dg.astype(o_ref.dtype)
        o_ref[:, f:] = du.astype(o_ref.dtype)

    return pl.pallas_call(
        body, name=name, grid=(t // tm,),
        in_specs=[pl.BlockSpec((tm, f), lambda i: (i, 0)), pl.BlockSpec((tm, f), lambda i: (i, 1)),
                  pl.BlockSpec((tm, f), lambda i: (i, 0))],
        out_specs=pl.BlockSpec((tm, f2), lambda i: (i, 0)),
        out_shape=_sds((t, f2), BF16), compiler_params=_cp("parallel"))(gu, gu, dact)


def _softmax_den(parts, extra=None):
    m = parts[0].max(axis=-1, keepdims=True)
    for s in parts[1:]:
        m = jnp.maximum(m, s.max(axis=-1, keepdims=True))
    if extra is not None:
        m = jnp.maximum(m, extra)
    m = lax.stop_gradient(m)
    es = [jnp.exp(s - m) for s in parts]
    den = es[0].sum(axis=-1, keepdims=True)
    for e in es[1:]:
        den = den + e.sum(axis=-1, keepdims=True)
    if extra is not None:
        den = den + jnp.exp(extra - m)
    return es, den


def _swa_head(q, kp, kc, vp, vc, sink, mask_prev, mask_cur):
    qb = q.astype(BF16)
    sp = jnp.where(mask_prev, _dot(qb, kp.astype(BF16), 1, 1) * SCALE, NEG)
    sc = jnp.where(mask_cur, _dot(qb, kc.astype(BF16), 1, 1) * SCALE, NEG)
    (ep, ec), den = _softmax_den([sp, sc], sink)
    return _dot((ep / den).astype(BF16), vp.astype(BF16), 1, 0) + _dot((ec / den).astype(BF16), vc.astype(BF16), 1, 0)


def _mem_head(q, k, v):
    s = _dot(q.astype(BF16), k.astype(BF16), 1, 1) * SCALE
    (e,), den = _softmax_den([s])
    return _dot((e / den).astype(BF16), v.astype(BF16), 1, 0)


def _gelu(x):
    return 0.5 * x * (1.0 + jnp.tanh(0.7978845608028654 * (x + 0.044715 * (x * x * x))))


def _gmlp_group(zu, zv, w, bcol, lg, lb, tri):
    u, v = _gelu(zu), _gelu(zv)
    mu = jnp.mean(v, axis=-1, keepdims=True)
    var = jnp.mean(jnp.square(v - mu), axis=-1, keepdims=True)
    vn = (v - mu) * lax.rsqrt(var + EPS) * lg + lb
    sv = _dot(jnp.where(tri, w, 0.0).astype(BF16), vn.astype(BF16), 1, 0) + bcol
    return u * sv


def _cols(x, width):
    return [x[:, j * width:(j + 1) * width] for j in range(x.shape[1] // width)]


def _swa_masks(has_prev):
    qi = lax.broadcasted_iota(jnp.int32, (WINDOW, WINDOW), 0)
    kj = lax.broadcasted_iota(jnp.int32, (WINDOW, WINDOW), 1)
    return jnp.logical_and(kj > qi, has_prev), kj <= qi


def _mix_a(qs, kps, kcs, vps, vcs, sinks, qms, kms, vms, masks):
    outs = [_swa_head(qs[h], kps[h // GROUP], kcs[h // GROUP], vps[h // GROUP], vcs[h // GROUP], sinks[h], *masks)
            for h in range(Q_HEADS)]
    return outs + [_mem_head(qms[h], kms[h], vms[h]) for h in range(MEM_HEADS)]


def _mix_a_specs(nm, blk):
    prev = lambda n: jnp.maximum(blk(n) - 1, 0)
    return [pl.BlockSpec((WINDOW, Q_W), lambda n: (blk(n), 0)),
            pl.BlockSpec((WINDOW, KV_W), lambda n: (prev(n), Q_W // KV_W)),
            pl.BlockSpec((WINDOW, KV_W), lambda n: (blk(n), Q_W // KV_W)),
            pl.BlockSpec((WINDOW, KV_W), lambda n: (prev(n), Q_W // KV_W + 1)),
            pl.BlockSpec((WINDOW, KV_W), lambda n: (blk(n), Q_W // KV_W + 1)),
            pl.BlockSpec((WINDOW, MEM_W), lambda n: (blk(n), (Q_W + 2 * KV_W) // MEM_W)),
            pl.BlockSpec((16, 128), lambda n: (0, 0)),
            pl.BlockSpec((nm, MEM_W), lambda n: (0, 0)),
            pl.BlockSpec((nm, MEM_W), lambda n: (0, 1))]


def _mix_a_args(refs, has_prev):
    q, kp, kc, vp, vc, qm, sk, km, vm = [r[...].astype(F32) for r in refs]
    return (_cols(q, HEAD_DIM), _cols(kp, HEAD_DIM), _cols(kc, HEAD_DIM), _cols(vp, HEAD_DIM), _cols(vc, HEAD_DIM),
            [sk[h:h + 1, 0:1] for h in range(Q_HEADS)], _cols(qm, HEAD_DIM), _cols(km, HEAD_DIM), _cols(vm, HEAD_DIM))


def mixer_a_fwd(proj, sk, kv, name):
    t, nm = proj.shape[0], kv.shape[0]

    def body(*refs):
        o_ref = refs[-1]
        args = _mix_a_args(refs[:-1], pl.program_id(0) > 0)
        o_ref[...] = jnp.concatenate(_mix_a(*args, _swa_masks(pl.program_id(0) > 0)), axis=1).astype(o_ref.dtype)

    return pl.pallas_call(
        body, name=name, grid=(t // WINDOW,), in_specs=_mix_a_specs(nm, lambda n: n),
        out_specs=pl.BlockSpec((WINDOW, Q_W + MEM_W), lambda n: (n, 0)),
        out_shape=_sds((t, Q_W + MEM_W), BF16), compiler_params=_cp("parallel"))(proj, proj, proj, proj, proj, proj, sk, kv, kv)


def _onehot_rows(vals, shape):
    rows = lax.broadcasted_iota(jnp.int32, shape, 0)
    out = jnp.zeros(shape, F32)
    for h, v in enumerate(vals):
        out = out + jnp.where(rows == h, jnp.broadcast_to(v, shape), 0.0)
    return out


def mixer_a_bwd(proj, dcat, sk, kv, name):
    t, nm = proj.shape[0], kv.shape[0]
    nb = t // WINDOW
    blk = lambda i: nb - 1 - i

    def body(*refs):
        dcat_ref, dproj_ref, dsk_ref, dkv_ref, carry_ref = refs[9:]
        i = pl.program_id(0)

        @pl.when(i == 0)
        def _():
            carry_ref[...] = jnp.zeros_like(carry_ref)
            dsk_ref[...] = jnp.zeros_like(dsk_ref)
            dkv_ref[...] = jnp.zeros_like(dkv_ref)

        masks = _swa_masks(blk(i) > 0)
        args = _mix_a_args(refs[:9], None)
        _, vjp = jax.vjp(lambda *a: _mix_a(*a, masks), *args)
        dqs, dkps, dkcs, dvps, dvcs, dsinks, dqms, dkms, dvms = vjp(_cols(dcat_ref[...].astype(F32), HEAD_DIM))
        dkv_cur = jnp.concatenate(dkcs + dvcs, axis=1) + carry_ref[...]
        carry_ref[...] = jnp.concatenate(dkps + dvps, axis=1)
        dproj_ref[...] = jnp.concatenate(dqs + [dkv_cur] + dqms, axis=1).astype(dproj_ref.dtype)
        dsk_ref[...] += _onehot_rows(dsinks, (16, 128))
        dkv_ref[...] += jnp.concatenate(dkms + dvms, axis=1)

    width = Q_W + 2 * KV_W + MEM_W
    return pl.pallas_call(
        body, name=name, grid=(nb,),
        in_specs=_mix_a_specs(nm, blk) + [pl.BlockSpec((WINDOW, Q_W + MEM_W), lambda i: (blk(i), 0))],
        out_specs=[pl.BlockSpec((WINDOW, width), lambda i: (blk(i), 0)), pl.BlockSpec((16, 128), lambda i: (0, 0)),
                   pl.BlockSpec((nm, 2 * MEM_W), lambda i: (0, 0))],
        out_shape=[_sds((t, width), BF16), _sds((16, 128), F32), _sds((nm, 2 * MEM_W), F32)],
        scratch_shapes=[pltpu.VMEM((WINDOW, 2 * KV_W), F32)],
        compiler_params=_cp("arbitrary"))(proj, proj, proj, proj, proj, proj, sk, kv, kv, dcat)


def _mix_b(zus, zvs, ws, bcols, lgs, lbs, qms, kms, vms, tri):
    outs = [_gmlp_group(zus[g], zvs[g], ws[g], bcols[g], lgs[g], lbs[g], tri) for g in range(B_GROUPS)]
    return outs + [_mem_head(qms[h], kms[h], vms[h]) for h in range(MEM_HEADS)]


def _mix_b_specs(nm):
    return [pl.BlockSpec((WINDOW, 2 * B_W), lambda n: (n, 0)),
            pl.BlockSpec((WINDOW, MEM_W), lambda n: (n, 2 * B_W // MEM_W)),
            pl.BlockSpec((B_GROUPS, WINDOW, WINDOW), lambda n: (0, 0, 0)),
            pl.BlockSpec((WINDOW, 128), lambda n: (0, 0)),
            pl.BlockSpec((8, 128), lambda n: (0, 0)),
            pl.BlockSpec((8, 128), lambda n: (0, 0)),
            pl.BlockSpec((nm, MEM_W), lambda n: (0, 0)),
            pl.BlockSpec((nm, MEM_W), lambda n: (0, 1))]


def _mix_b_args(refs):
    z, qm, ws, bt, lg, lb, km, vm = [r[...].astype(F32) for r in refs]
    zs = _cols(z, 128)
    return (zs[:B_GROUPS], zs[B_GROUPS:], [ws[g] for g in range(B_GROUPS)], [bt[:, g:g + 1] for g in range(B_GROUPS)],
            [lg[g:g + 1, :] for g in range(B_GROUPS)], [lb[g:g + 1, :] for g in range(B_GROUPS)],
            _cols(qm, HEAD_DIM), _cols(km, HEAD_DIM), _cols(vm, HEAD_DIM))


def _tri():
    return lax.broadcasted_iota(jnp.int32, (WINDOW, WINDOW), 0) >= lax.broadcasted_iota(jnp.int32, (WINDOW, WINDOW), 1)


def mixer_b_fwd(proj, ws, bt, lg, lb, kv, name):
    t, nm = proj.shape[0], kv.shape[0]

    def body(*refs):
        o_ref = refs[-1]
        o_ref[...] = jnp.concatenate(_mix_b(*_mix_b_args(refs[:-1]), _tri()), axis=1).astype(o_ref.dtype)

    return pl.pallas_call(
        body, name=name, grid=(t // WINDOW,), in_specs=_mix_b_specs(nm),
        out_specs=pl.BlockSpec((WINDOW, B_W + MEM_W), lambda n: (n, 0)),
        out_shape=_sds((t, B_W + MEM_W), BF16), compiler_params=_cp("parallel"))(proj, proj, ws, bt, lg, lb, kv, kv)


def mixer_b_bwd(proj, dcat, ws, bt, lg, lb, kv, name):
    t, nm = proj.shape[0], kv.shape[0]

    def body(*refs):
        dcat_ref, dproj_ref, dws_ref, dbt_ref, dlg_ref, dlb_ref, dkv_ref = refs[8:]

        @pl.when(pl.program_id(0) == 0)
        def _():
            for r in (dws_ref, dbt_ref, dlg_ref, dlb_ref, dkv_ref):
                r[...] = jnp.zeros_like(r)

        tri = _tri()
        _, vjp = jax.vjp(lambda *a: _mix_b(*a, tri), *_mix_b_args(refs[:8]))
        dzus, dzvs, dws, dbcols, dlgs, dlbs, dqms, dkms, dvms = vjp(
            _cols(dcat_ref[:, :B_W].astype(F32), 128) + _cols(dcat_ref[:, B_W:].astype(F32), HEAD_DIM))
        dproj_ref[...] = jnp.concatenate(dzus + dzvs + dqms, axis=1).astype(dproj_ref.dtype)
        for g in range(B_GROUPS):
            dws_ref[g] += dws[g]
        lanes = lax.broadcasted_iota(jnp.int32, (WINDOW, 128), 1)
        dbt = jnp.zeros((WINDOW, 128), F32)
        for g in range(B_GROUPS):
            dbt = dbt + jnp.where(lanes == g, jnp.broadcast_to(dbcols[g], (WINDOW, 128)), 0.0)
        dbt_ref[...] += dbt
        dlg_ref[...] += _onehot_rows(dlgs, (8, 128))
        dlb_ref[...] += _onehot_rows(dlbs, (8, 128))
        dkv_ref[...] += jnp.concatenate(dkms + dvms, axis=1)

    width = 2 * B_W + MEM_W
    const2 = lambda n: (0, 0)
    return pl.pallas_call(
        body, name=name, grid=(t // WINDOW,),
        in_specs=_mix_b_specs(nm) + [pl.BlockSpec((WINDOW, B_W + MEM_W), lambda n: (n, 0))],
        out_specs=[pl.BlockSpec((WINDOW, width), lambda n: (n, 0)),
                   pl.BlockSpec((B_GROUPS, WINDOW, WINDOW), lambda n: (0, 0, 0)),
                   pl.BlockSpec((WINDOW, 128), const2), pl.BlockSpec((8, 128), const2), pl.BlockSpec((8, 128), const2),
                   pl.BlockSpec((nm, 2 * MEM_W), const2)],
        out_shape=[_sds((t, width), BF16), _sds((B_GROUPS, WINDOW, WINDOW), F32), _sds((WINDOW, 128), F32),
                   _sds((8, 128), F32), _sds((8, 128), F32), _sds((nm, 2 * MEM_W), F32)],
        compiler_params=_cp("arbitrary"))(proj, proj, ws, bt, lg, lb, kv, kv, dcat)


def adamw(w, g, m, v, name):
    r, c = w.shape
    tr = _tile(r, (512, 352, 256, 128, 64, 32, 16, 8))

    def body(w_ref, g_ref, m_ref, v_ref, d_ref, nm_ref, nv_ref):
        g = g_ref[...]
        m2 = ADAM_B1 * m_ref[...] + (1.0 - ADAM_B1) * g
        v2 = ADAM_B2 * v_ref[...] + (1.0 - ADAM_B2) * jnp.square(g)
        m_hat = m2 / (1.0 - ADAM_B1 ** ADAM_STEP)
        v_hat = v2 / (1.0 - ADAM_B2 ** ADAM_STEP)
        d_ref[...] = -ADAM_LR * (m_hat / (jnp.sqrt(v_hat) + ADAM_EPS) + ADAM_WD * w_ref[...])
        nm_ref[...] = m2
        nv_ref[...] = v2

    spec = pl.BlockSpec((tr, c), lambda i: (i, 0))
    return pl.pallas_call(
        body, name=name, grid=(r // tr,), in_specs=[spec] * 4, out_specs=[spec] * 3,
        out_shape=[_sds((r, c), F32)] * 3, compiler_params=_cp("parallel"))(w, g, m, v)


def _place():
    return lax.axis_index("x"), lax.axis_index("y"), lax.axis_index("c")


def _other_chips(x, y):
    return [(1 - x, y), (x, 1 - y), (1 - x, 1 - y)]


def _remote(src, dst, send_sems, recv_sems, k, dev):
    return pltpu.make_async_remote_copy(src_ref=src, dst_ref=dst, send_sem=send_sems.at[k], recv_sem=recv_sems.at[k],
                                        device_id=dev, device_id_type=MESH)


def allgather_weights(shards, name):
    nw = len(shards)

    def body(*refs):
        ins, outs = refs[:nw], refs[nw:2 * nw]
        send_sems, recv_sems, local_sems = refs[2 * nw:]
        x, y, c = _place()
        chips = _other_chips(x, y)
        halves = [(pl.ds(c * (r.shape[0] // 2), r.shape[0] // 2), pl.ds((1 - c) * (r.shape[0] // 2), r.shape[0] // 2)) for r in ins]
        local = [pltpu.make_async_copy(ins[w], outs[w].at[:, 2 * x + y], local_sems.at[w]) for w in range(nw)]
        for cp in local:
            cp.start()
        first = [_remote(ins[w].at[halves[w][0]], outs[w].at[halves[w][0], 2 * x + y], send_sems, recv_sems, 6 * w + j, (*chip, c))
                 for j, chip in enumerate(chips) for w in range(nw)]
        for cp in first:
            cp.start()
        passed = []
        for j, chip in enumerate(chips):
            for w in range(nw):
                blk = outs[w].at[halves[w][0], 2 * chip[0] + chip[1]]
                _remote(blk, blk, send_sems, recv_sems, 6 * w + j, (x, y, c)).wait_recv()
                cp = _remote(blk, blk, send_sems, recv_sems, 6 * w + 3 + j, (x, y, 1 - c))
                cp.start()
                passed.append(cp)
        for j, chip in enumerate(chips):
            for w in range(nw):
                blk = outs[w].at[halves[w][1], 2 * chip[0] + chip[1]]
                _remote(blk, blk, send_sems, recv_sems, 6 * w + 3 + j, (x, y, c)).wait_recv()
        for cp in first + passed:
            cp.wait_send()
        for cp in local:
            cp.wait()

    return pl.pallas_call(
        body, name=name, in_specs=[HBM_SPEC] * nw, out_specs=[HBM_SPEC] * nw,
        out_shape=[_sds((s.shape[0], 4) + s.shape[1:], s.dtype) for s in shards],
        scratch_shapes=[pltpu.SemaphoreType.DMA((6 * nw,)), pltpu.SemaphoreType.DMA((6 * nw,)), pltpu.SemaphoreType.DMA((nw,))],
        compiler_params=pltpu.CompilerParams(has_side_effects=True))(*shards)


def sibling_swap_halves(gs, name):
    nw = len(gs)

    def body(*refs):
        ins, outs = refs[:nw], refs[nw:2 * nw]
        send_sems, recv_sems = refs[2 * nw:]
        x, y, c = _place()
        copies = [_remote(ins[w].at[pl.ds((1 - c) * outs[w].shape[0], outs[w].shape[0])], outs[w], send_sems, recv_sems, w, (x, y, 1 - c))
                  for w in range(nw)]
        for cp in copies:
            cp.start()
        for cp in copies:
            cp.wait()

    return pl.pallas_call(
        body, name=name, in_specs=[HBM_SPEC] * nw, out_specs=[HBM_SPEC] * nw,
        out_shape=[_sds((g.shape[0] // 2,) + g.shape[1:], g.dtype) for g in gs],
        scratch_shapes=[pltpu.SemaphoreType.DMA((nw,)), pltpu.SemaphoreType.DMA((nw,))],
        compiler_params=pltpu.CompilerParams(has_side_effects=True))(*gs)


def chips_exchange(sbs, name):
    nw = len(sbs)

    def body(*refs):
        ins, outs = refs[:nw], refs[nw:2 * nw]
        send_sems, recv_sems = refs[2 * nw:]
        x, y, c = _place()
        copies = [_remote(ins[w].at[:, 2 * chip[0] + chip[1]], outs[w].at[j], send_sems, recv_sems, 3 * w + j, (*chip, c))
                  for j, chip in enumerate(_other_chips(x, y)) for w in range(nw)]
        for cp in copies:
            cp.start()
        for cp in copies:
            cp.wait()

    return pl.pallas_call(
        body, name=name, in_specs=[HBM_SPEC] * nw, out_specs=[HBM_SPEC] * nw,
        out_shape=[_sds((3, s.shape[0]) + s.shape[2:], s.dtype) for s in sbs],
        scratch_shapes=[pltpu.SemaphoreType.DMA((3 * nw,)), pltpu.SemaphoreType.DMA((3 * nw,))],
        compiler_params=pltpu.CompilerParams(has_side_effects=True))(*sbs)


def sibling_allgather_halves(fs, name):
    nw = len(fs)

    def body(*refs):
        ins, outs = refs[:nw], refs[nw:2 * nw]
        send_sems, recv_sems, local_sems = refs[2 * nw:]
        x, y, c = _place()
        mine = [outs[w].at[pl.ds(c * ins[w].shape[0], ins[w].shape[0])] for w in range(nw)]
        local = [pltpu.make_async_copy(ins[w], mine[w], local_sems.at[w]) for w in range(nw)]
        copies = [_remote(ins[w], mine[w], send_sems, recv_sems, w, (x, y, 1 - c)) for w in range(nw)]
        for cp in local + copies:
            cp.start()
        for w in range(nw):
            theirs = outs[w].at[pl.ds((1 - c) * ins[w].shape[0], ins[w].shape[0])]
            _remote(theirs, theirs, send_sems, recv_sems, w, (x, y, c)).wait_recv()
        for cp in copies:
            cp.wait_send()
        for cp in local:
            cp.wait()

    return pl.pallas_call(
        body, name=name, in_specs=[HBM_SPEC] * nw, out_specs=[HBM_SPEC] * nw,
        out_shape=[_sds((2 * f.shape[0],) + f.shape[1:], f.dtype) for f in fs],
        scratch_shapes=[pltpu.SemaphoreType.DMA((nw,)), pltpu.SemaphoreType.DMA((nw,)), pltpu.SemaphoreType.DMA((nw,))],
        compiler_params=pltpu.CompilerParams(has_side_effects=True))(*fs)


def _half_tile(a):
    return _tile(a, (256, 352, 128, 64, 32, 16))


def chip_partial_sums(g, r1, c_arr, name):
    hl, _, a, b = r1.shape
    ta = _half_tile(a)

    def body(c_ref, g_ref, r_ref, o_ref):
        o_ref[...] = (g_ref[...] + r_ref[...]).astype(o_ref.dtype)

    blk = (None, None, ta, b)
    return pl.pallas_call(
        body, name=name,
        grid_spec=pltpu.PrefetchScalarGridSpec(
            num_scalar_prefetch=1, grid=(hl, 4, a // ta),
            in_specs=[pl.BlockSpec(blk, lambda l, s, i, c: (c[0] * hl + l, s, i, 0)), pl.BlockSpec(blk, lambda l, s, i, c: (l, s, i, 0))],
            out_specs=pl.BlockSpec(blk, lambda l, s, i, c: (l, s, i, 0))),
        out_shape=_sds(r1.shape, BF16), compiler_params=_cp("parallel", "parallel", "parallel"))(c_arr, g, r1)


def shard_total(g, r1, r2, cs_arr, name):
    hl, _, a, b = r1.shape
    ta = _half_tile(a)

    def body(cs_ref, g_ref, r1_ref, p0_ref, p1_ref, p2_ref, o_ref):
        o_ref[...] = (((g_ref[...] + r1_ref[...]) + p0_ref[...].astype(F32)) + p1_ref[...].astype(F32)) + p2_ref[...].astype(F32)

    blk4, blk3 = (None, None, ta, b), (None, ta, b)
    peer = lambda k: pl.BlockSpec((None, None, ta, b), lambda l, i, cs: (k, l, i, 0))
    return pl.pallas_call(
        body, name=name,
        grid_spec=pltpu.PrefetchScalarGridSpec(
            num_scalar_prefetch=1, grid=(hl, a // ta),
            in_specs=[pl.BlockSpec(blk4, lambda l, i, cs: (cs[0] * hl + l, cs[1], i, 0)),
                      pl.BlockSpec(blk4, lambda l, i, cs: (l, cs[1], i, 0)), peer(0), peer(1), peer(2)],
            out_specs=pl.BlockSpec(blk3, lambda l, i, cs: (l, i, 0))),
        out_shape=_sds((hl, a, b), F32), compiler_params=_cp("parallel", "parallel"))(cs_arr, g, r1, r2, r2, r2)


def allgather_small(v, name):
    r, n = v.shape

    def body(x_ref, out_ref, send_sems, recv_sems, local_sem):
        x, y, c = _place()
        me, sibling = (x, y, c), (x, y, 1 - c)
        chips = _other_chips(x, y)

        def rows(px, py, pc):
            return out_ref.at[pl.ds((4 * px + 2 * py + pc) * r, r), :]

        def copy(k, block, to, src=None):
            return _remote(rows(*block) if src is None else src, rows(*block), send_sems, recv_sems, k, to)

        mine = pltpu.make_async_copy(x_ref, rows(*me), local_sem)
        mine.start()
        first = [copy(0, me, sibling, src=x_ref)] + [copy(1 + j, me, (*chip, c), src=x_ref) for j, chip in enumerate(chips)]
        for cp in first:
            cp.start()
        passed = [copy(4 + j, (*chip, c), sibling) for j, chip in enumerate(chips)]
        for j, chip in enumerate(chips):
            copy(1 + j, (*chip, c), me).wait_recv()
            passed[j].start()
        copy(0, sibling, me).wait_recv()
        for j, chip in enumerate(chips):
            copy(4 + j, (*chip, 1 - c), me).wait_recv()
        for cp in first + passed:
            cp.wait_send()
        mine.wait()

    return pl.pallas_call(
        body, name=name, in_specs=[VMEM_SPEC], out_specs=VMEM_SPEC, out_shape=_sds((8 * r, n), v.dtype),
        scratch_shapes=[pltpu.SemaphoreType.DMA((7,)), pltpu.SemaphoreType.DMA((7,)), pltpu.SemaphoreType.DMA],
        compiler_params=pltpu.CompilerParams(has_side_effects=True, vmem_limit_bytes=V7X_VMEM_LIMIT_BYTES))(v)


def sum_devices(v8, name):
    _, r, n = v8.shape
    tr = _tile(r, (88, 64, 32, 16, 8))

    def body(v_ref, o_ref):
        acc = v_ref[0]
        for d in range(1, 8):
            acc = acc + v_ref[d]
        o_ref[...] = acc

    return pl.pallas_call(
        body, name=name, grid=(r // tr,), in_specs=[pl.BlockSpec((8, tr, n), lambda i: (0, i, 0))],
        out_specs=pl.BlockSpec((tr, n), lambda i: (i, 0)), out_shape=_sds((r, n), F32), compiler_params=_cp("parallel"))(v8)


SHARDED = (("a_w_in", 2), ("a_w_out", 1), ("b_w_in", 2), ("b_w_out", 1), ("w_mem_kv", 1), ("w_gate_up", 2), ("w_down", 1))


def _full_from_gathered(wg, axis):
    l, _, a, b = wg.shape
    if axis == 1:
        return wg.reshape(l, 4 * a, b)
    return wg.transpose(0, 2, 1, 3).reshape(l, a, 4 * b)


def _by_shard(dw, axis):
    l, k, n = dw.shape
    if axis == 1:
        return dw.reshape(l, 4, k // 4, n)
    return dw.reshape(l, k, 4, n // 4).transpose(0, 2, 1, 3)


def _pack(arrs):
    parts = []
    for a in arrs:
        flat = a.reshape(-1)
        flat = jnp.pad(flat, (0, -flat.shape[0] % 1024))
        parts.append(flat.reshape(-1, 128))
    return jnp.concatenate(parts, axis=0)


def _unpack(buf, like):
    out, row = [], 0
    for a in like:
        size = 1
        for s in a.shape:
            size *= s
        rows = -(-size // 1024) * 8
        out.append(buf[row:row + rows].reshape(-1)[:size].reshape(a.shape))
        row += rows
    return out


def kernel(x, mem, mem_norm_g, mix_norm_g, ffn_norm_g, final_norm_g, a_w_in, a_sinks, a_w_out, b_w_in, b_w_s, b_bias_s, b_ln_g, b_ln_b, b_w_out, w_mem_kv, w_gate_up, w_down, loss_target, m_mem_norm_g, m_mix_norm_g, m_ffn_norm_g, m_final_norm_g, m_a_w_in, m_a_sinks, m_a_w_out, m_b_w_in, m_b_w_s, m_b_bias_s, m_b_ln_g, m_b_ln_b, m_b_w_out, m_w_mem_kv, m_w_gate_up, m_w_down, v_mem_norm_g, v_mix_norm_g, v_ffn_norm_g, v_final_norm_g, v_a_w_in, v_a_sinks, v_a_w_out, v_b_w_in, v_b_w_s, v_b_bias_s, v_b_ln_g, v_b_ln_b, v_b_w_out, v_w_mem_kv, v_w_gate_up, v_w_down):
    given = dict(locals())
    depth = mix_norm_g.shape[0]
    d = x.shape[-1]
    xi, yi, ci = _place()
    c_arr = jnp.stack([ci]).astype(jnp.int32)
    cs_arr = jnp.stack([ci, 2 * xi + yi]).astype(jnp.int32)

    gathered = allgather_weights([given[n].astype(BF16) for n, _ in SHARDED], "allgather_weights")
    full = {n: _full_from_gathered(wg, ax) for (n, ax), wg in zip(SHARDED, gathered)}

    h = x.reshape(-1, d)
    tgt = loss_target.reshape(-1, d)
    mem2 = mem.reshape(-1, d)
    row = lambda v: v.reshape(1, -1)

    mem_n = rmsnorm_fwd(mem2, row(mem_norm_g), "mem_norm")
    saved = []
    for i in range(depth):
        j = i // 2
        kv = matmul(mem_n, full["w_mem_kv"][i], "nn", BF16, "mem_kv")
        xn = rmsnorm_fwd(h, row(mix_norm_g[i]), "mix_norm")
        if i % 2 == 0:
            w_in, w_out = full["a_w_in"][j], full["a_w_out"][j]
            sk = jnp.pad(jnp.broadcast_to(a_sinks[j][:, None], (Q_HEADS, 128)), ((0, 16 - Q_HEADS), (0, 0)))
            proj = matmul(xn, w_in, "nn", BF16, "a_in")
            cat = mixer_a_fwd(proj, sk, kv, "mixer_a")
            extra = (sk,)
        else:
            w_in, w_out = full["b_w_in"][j], full["b_w_out"][j]
            bt = jnp.pad(b_bias_s[j].T, ((0, 0), (0, 128 - B_GROUPS)))
            lg = jnp.pad(b_ln_g[j], ((0, 8 - B_GROUPS), (0, 0)))
            lb = jnp.pad(b_ln_b[j], ((0, 8 - B_GROUPS), (0, 0)))
            proj = matmul(xn, w_in, "nn", BF16, "b_in")
            cat = mixer_b_fwd(proj, b_w_s[j], bt, lg, lb, kv, "mixer_b")
            extra = (b_w_s[j], bt, lg, lb)
        h_mid = matmul(cat, w_out, "nn", F32, "mix_out", res=h)
        hn = rmsnorm_fwd(h_mid, row(ffn_norm_g[i]), "ffn_norm")
        gu = matmul(hn, full["w_gate_up"][i], "nn", BF16, "gate_up")
        act = swiglu_fwd(gu, "swiglu")
        h_out = matmul(act, full["w_down"][i], "nn", F32, "down", res=h_mid)
        saved.append((h, xn, proj, cat, h_mid, hn, gu, act, kv, w_in, w_out, extra))
        h = h_out

    loss_part, dh, d_final_g = loss_head(h, row(final_norm_g), tgt, "loss_head")
    loss = lax.psum(loss_part[0, 0], ("x", "y", "c"))

    dw = {n: [None] * given[n].shape[0] for n, _ in SHARDED}
    d_mix_g, d_ffn_g = [None] * depth, [None] * depth
    d_sinks, d_ws, d_bias, d_lg, d_lb = [], [], [], [], []
    d_mem_n = jnp.zeros(mem2.shape, F32)
    for i in reversed(range(depth)):
        j = i // 2
        h_in, xn, proj, cat, h_mid, hn, gu, act, kv, w_in, w_out, extra = saved[i]
        dact = matmul(dh, full["w_down"][i], "nt", F32, "down_dx", tn=1408)
        dw["w_down"][i] = matmul(act, dh, "tn", F32, "down_dw", tm=1408)
        dgu = swiglu_bwd(gu, dact, "swiglu_bwd")
        dw["w_gate_up"][i] = matmul(hn, dgu, "tn", F32, "gate_up_dw")
        dhn = matmul(dgu, full["w_gate_up"][i], "nt", F32, "gate_up_dx")
        dh, d_ffn_g[i] = rmsnorm_bwd(h_mid, row(ffn_norm_g[i]), dhn, dh, "ffn_norm_bwd")
        dcat = matmul(dh, w_out, "nt", F32, "mix_out_dx")
        if i % 2 == 0:
            dw["a_w_out"][j] = matmul(cat, dh, "tn", F32, "mix_out_dw")
            dproj, dsk, dkv = mixer_a_bwd(proj, dcat, extra[0], kv, "mixer_a_bwd")
            d_sinks.insert(0, dsk[:Q_HEADS, 0])
            dw["a_w_in"][j] = matmul(xn, dproj, "tn", F32, "a_in_dw", tn=640)
            dxn = matmul(dproj, w_in, "nt", F32, "a_in_dx")
        else:
            dw["b_w_out"][j] = matmul(cat, dh, "tn", F32, "mix_out_dw")
            dproj, dws, dbt, dlg, dlb, dkv = mixer_b_bwd(proj, dcat, *extra, kv, "mixer_b_bwd")
            d_ws.insert(0, dws)
            d_bias.insert(0, dbt[:, :B_GROUPS].T)
            d_lg.insert(0, dlg[:B_GROUPS])
            d_lb.insert(0, dlb[:B_GROUPS])
            dw["b_w_in"][j] = matmul(xn, dproj, "tn", F32, "b_in_dw", tn=896)
            dxn = matmul(dproj, w_in, "nt", F32, "b_in_dx")
        dw["w_mem_kv"][i] = matmul(mem_n, dkv, "tn", F32, "mem_kv_dw")
        d_mem_n = matmul(dkv, full["w_mem_kv"][i], "nt", F32, "mem_kv_dx", res=d_mem_n)
        dh, d_mix_g[i] = rmsnorm_bwd(h_in, row(mix_norm_g[i]), dxn, dh, "mix_norm_bwd")
    grad_x = dh.reshape(x.shape)
    _, d_mem_g = rmsnorm_bwd(mem2, row(mem_norm_g), d_mem_n, jnp.zeros(mem2.shape, F32), "mem_norm_bwd")

    by_shard = [_by_shard(jnp.stack(dw[n]), ax) for n, ax in SHARDED]
    from_sibling = sibling_swap_halves(by_shard, "grads_sibling_swap")
    partial = [chip_partial_sums(g, r1, c_arr, "grads_chip_sum") for g, r1 in zip(by_shard, from_sibling)]
    from_chips = chips_exchange(partial, "grads_chips_exchange")
    totals = [shard_total(g, r1, r2, cs_arr, "grads_shard_total") for g, r1, r2 in zip(by_shard, from_sibling, from_chips)]
    grads_sharded = sibling_allgather_halves(totals, "grads_sibling_allgather")

    out = {}
    for (n, _), g in zip(SHARDED, grads_sharded):
        shape = given[n].shape
        two_d = lambda a: a.reshape(-1, shape[-1])
        delta, new_m, new_v = adamw(two_d(given[n]), two_d(g), two_d(given["m_" + n]), two_d(given["v_" + n]), "adamw")
        out[n] = (g.reshape(shape), delta.reshape(shape), new_m.reshape(shape), new_v.reshape(shape))

    small = ("mem_norm_g", "mix_norm_g", "ffn_norm_g", "final_norm_g", "a_sinks", "b_w_s", "b_bias_s", "b_ln_g", "b_ln_b")
    small_g = [d_mem_g[0], jnp.concatenate(d_mix_g, axis=0), jnp.concatenate(d_ffn_g, axis=0), d_final_g[0],
               jnp.stack(d_sinks), jnp.stack(d_ws), jnp.stack(d_bias), jnp.stack(d_lg), jnp.stack(d_lb)]
    packed = _pack(small_g)
    g_small = sum_devices(allgather_small(packed, "small_allgather").reshape(8, *packed.shape), "small_sum")
    like = [given[n] for n in small]
    delta_s, new_m_s, new_v_s = adamw(_pack(like), g_small, _pack([given["m_" + n] for n in small]),
                                      _pack([given["v_" + n] for n in small]), "adamw_small")
    for n, g, dl, nm_, nv_ in zip(small, _unpack(g_small, like), _unpack(delta_s, like), _unpack(new_m_s, like), _unpack(new_v_s, like)):
        out[n] = (g, dl, nm_, nv_)

    order = ("mem_norm_g", "mix_norm_g", "ffn_norm_g", "final_norm_g", "a_w_in", "a_sinks", "a_w_out", "b_w_in", "b_w_s",
             "b_bias_s", "b_ln_g", "b_ln_b", "b_w_out", "w_mem_kv", "w_gate_up", "w_down")
    return (loss, grad_x, *[out[n][0] for n in order], *[out[n][1] for n in order],
            *[out[n][2] for n in order], *[out[n][3] for n in order])
```

```python
import jax
import jax.numpy as jnp
from jax import lax
from jax.experimental import pallas as pl
from jax.experimental.pallas import tpu as pltpu

F32, BF16 = jnp.float32, jnp.bfloat16
EPS = 1e-6
HEAD_DIM = 64
Q_HEADS, KV_HEADS, GROUP = 12, 2, 6
WINDOW = 128
MEM_HEADS = 4
B_GROUPS = 6
Q_W, KV_W, MEM_W, B_W = 768, 128, 256, 768
SCALE = HEAD_DIM ** -0.5
NEG = -1e30
ADAM_LR, ADAM_B1, ADAM_B2, ADAM_EPS, ADAM_WD, ADAM_STEP = 0.001, 0.9, 0.999, 1e-08, 0.01, 10
V7X_VMEM_LIMIT_BYTES = 48 * 1024 * 1024
MESH = pl.DeviceIdType.MESH
HBM_SPEC = pl.BlockSpec(memory_space=pltpu.HBM)
VMEM_SPEC = pl.BlockSpec(memory_space=pltpu.VMEM)


def _cp(*sem):
    return pltpu.CompilerParams(dimension_semantics=sem or None, vmem_limit_bytes=V7X_VMEM_LIMIT_BYTES)


def _tile(n, cands):
    for t in cands:
        if n % t == 0:
            return t
    return n


def _sds(shape, dtype):
    return jax.ShapeDtypeStruct(tuple(shape), dtype)


def _dot(a, b, ca, cb):
    return lax.dot_general(a, b, (((ca,), (cb,)), ((), ())), preferred_element_type=F32)


def _rms(x, g):
    return x * lax.rsqrt(jnp.mean(x * x, axis=-1, keepdims=True) + EPS) * g


def rmsnorm_fwd(h, g, name):
    t, d = h.shape
    tm = _tile(t, (512, 256, 128))

    def body(h_ref, g_ref, o_ref):
        o_ref[...] = _rms(h_ref[...], g_ref[...]).astype(o_ref.dtype)

    return pl.pallas_call(
        body, name=name, grid=(t // tm,),
        in_specs=[pl.BlockSpec((tm, d), lambda i: (i, 0)), pl.BlockSpec((1, d), lambda i: (0, 0))],
        out_specs=pl.BlockSpec((tm, d), lambda i: (i, 0)),
        out_shape=_sds((t, d), BF16), compiler_params=_cp("parallel"))(h, g)


def rmsnorm_bwd(h, g, dxn, dres, name):
    t, d = h.shape
    tm = _tile(t, (512, 256, 128))

    def body(h_ref, g_ref, dxn_ref, dres_ref, dh_ref, dg_ref):
        _, vjp = jax.vjp(_rms, h_ref[...], g_ref[...])
        dh, dg = vjp(dxn_ref[...].astype(F32))
        dh_ref[...] = dres_ref[...] + dh

        @pl.when(pl.program_id(0) == 0)
        def _():
            dg_ref[...] = jnp.zeros_like(dg_ref)

        dg_ref[...] += dg

    row = pl.BlockSpec((tm, d), lambda i: (i, 0))
    vec = pl.BlockSpec((1, d), lambda i: (0, 0))
    return pl.pallas_call(
        body, name=name, grid=(t // tm,), in_specs=[row, vec, row, row], out_specs=[row, vec],
        out_shape=[_sds((t, d), F32), _sds((1, d), F32)], compiler_params=_cp("arbitrary"))(h, g, dxn, dres)


def loss_head(h, g, tgt, name):
    t, d = h.shape
    tm = _tile(t, (512, 256, 128))

    def body(h_ref, g_ref, t_ref, l_ref, dh_ref, dg_ref):
        y, vjp = jax.vjp(_rms, h_ref[...], g_ref[...])
        err = y - t_ref[...]
        dh, dg = vjp(err * (1.0 / d))
        dh_ref[...] = dh
        part = 0.5 * jnp.sum(jnp.mean(err * err, axis=-1, keepdims=True), axis=0, keepdims=True)

        @pl.when(pl.program_id(0) == 0)
        def _():
            dg_ref[...] = jnp.zeros_like(dg_ref)
            l_ref[...] = jnp.zeros_like(l_ref)

        dg_ref[...] += dg
        l_ref[...] += part

    row = pl.BlockSpec((tm, d), lambda i: (i, 0))
    vec = pl.BlockSpec((1, d), lambda i: (0, 0))
    one = pl.BlockSpec((1, 1), lambda i: (0, 0))
    return pl.pallas_call(
        body, name=name, grid=(t // tm,), in_specs=[row, vec, row], out_specs=[one, row, vec],
        out_shape=[_sds((1, 1), F32), _sds((t, d), F32), _sds((1, d), F32)], compiler_params=_cp("arbitrary"))(h, g, tgt)


def matmul(a, b, mode, out_dtype, name, res=None, tm=1024, tn=512, tk=2816):
    if mode == "nn":
        (m, k), n = a.shape, b.shape[1]
    elif mode == "nt":
        (m, k), n = a.shape, b.shape[0]
    else:
        (k, m), n = a.shape, b.shape[1]
    tm = _tile(m, (tm, 1408, 1024, 512, 256, 128))
    tn = _tile(n, (tn, 1408, 896, 640, 512, 256, 128))
    tk = _tile(k, (tk, 2816, 1792, 1280, 1024, 512, 256, 128))
    nk = k // tk
    if mode == "nn":
        a_spec = pl.BlockSpec((tm, tk), lambda i, j, kk: (i, kk))
        b_spec = pl.BlockSpec((tk, tn), lambda i, j, kk: (kk, j))
        ca, cb = 1, 0
    elif mode == "nt":
        a_spec = pl.BlockSpec((tm, tk), lambda i, j, kk: (i, kk))
        b_spec = pl.BlockSpec((tn, tk), lambda i, j, kk: (j, kk))
        ca, cb = 1, 1
    else:
        a_spec = pl.BlockSpec((tk, tm), lambda i, j, kk: (kk, i))
        b_spec = pl.BlockSpec((tk, tn), lambda i, j, kk: (kk, j))
        ca, cb = 0, 0
    o_spec = pl.BlockSpec((tm, tn), lambda i, j, kk: (i, j))
    has_res = res is not None

    def body(a_ref, b_ref, *rest):
        o_ref = rest[1] if has_res else rest[0]
        p = _dot(a_ref[...].astype(BF16), b_ref[...].astype(BF16), ca, cb)
        if nk == 1:
            if has_res:
                p = p + rest[0][...]
            o_ref[...] = p.astype(o_ref.dtype)
        else:
            acc_ref = rest[-1]
            kk = pl.program_id(2)

            @pl.when(kk == 0)
            def _():
                acc_ref[...] = p

            @pl.when(kk > 0)
            def _():
                acc_ref[...] += p

            @pl.when(kk == nk - 1)
            def _():
                r = acc_ref[...]
                if has_res:
                    r = r + rest[0][...]
                o_ref[...] = r.astype(o_ref.dtype)

    return pl.pallas_call(
        body, name=name, grid=(m // tm, n // tn, nk),
        in_specs=[a_spec, b_spec] + ([o_spec] if has_res else []), out_specs=o_spec,
        out_shape=_sds((m, n), out_dtype), scratch_shapes=[pltpu.VMEM((tm, tn), F32)] if nk > 1 else [],
        compiler_params=_cp("parallel", "parallel", "arbitrary"))(*((a, b, res) if has_res else (a, b)))


def _swiglu(gate, up):
    return gate / (1.0 + jnp.exp(-gate)) * up


def swiglu_fwd(gu, name):
    t, f2 = gu.shape
    f = f2 // 2
    tm = _tile(t, (512, 256, 128))

    def body(g_ref, u_ref, o_ref):
        o_ref[...] = _swiglu(g_ref[...].astype(F32), u_ref[...].astype(F32)).astype(o_ref.dtype)

    return pl.pallas_call(
        body, name=name, grid=(t // tm,),
        in_specs=[pl.BlockSpec((tm, f), lambda i: (i, 0)), pl.BlockSpec((tm, f), lambda i: (i, 1))],
        out_specs=pl.BlockSpec((tm, f), lambda i: (i, 0)),
        out_shape=_sds((t, f), BF16), compiler_params=_cp("parallel"))(gu, gu)


def swiglu_bwd(gu, dact, name):
    t, f2 = gu.shape
    f = f2 // 2
    tm = _tile(t, (256, 128))

    def body(g_ref, u_ref, d_ref, o_ref):
        _, vjp = jax.vjp(_swiglu, g_ref[...].astype(F32), u_ref[...].astype(F32))
        dg, du = vjp(d_ref[...].astype(F32))
        o_ref[:, :f] = dg.astype(o_ref.dtype)
        o_ref[:, f:] = du.astype(o_ref.dtype)

    return pl.pallas_call(
        body, name=name, grid=(t // tm,),
        in_specs=[pl.BlockSpec((tm, f), lambda i: (i, 0)), pl.BlockSpec((tm, f), lambda i: (i, 1)),
                  pl.BlockSpec((tm, f), lambda i: (i, 0))],
        out_specs=pl.BlockSpec((tm, f2), lambda i: (i, 0)),
        out_shape=_sds((t, f2), BF16), compiler_params=_cp("parallel"))(gu, gu, dact)


def _softmax(s, sink=None):
    m = s.max(axis=-1, keepdims=True)
    if sink is not None:
        m = jnp.maximum(m, sink)
    m = lax.stop_gradient(m)
    e = jnp.exp(s - m)
    den = e.sum(axis=-1, keepdims=True)
    if sink is not None:
        den = den + jnp.exp(sink - m)
    return e * (1.0 / den)


def _low_lanes():
    return lax.broadcasted_iota(jnp.int32, (1, 128), 1) < HEAD_DIM


def _stack_heads(slabs):
    low = _low_lanes()
    return jnp.concatenate([p for s in slabs for p in (jnp.where(low, s, 0.0), jnp.where(low, 0.0, s))], axis=0)


def _unstack_heads(o, n_slabs):
    low = _low_lanes()
    return [jnp.where(low, o[2 * j * WINDOW:(2 * j + 1) * WINDOW], o[(2 * j + 1) * WINDOW:(2 * j + 2) * WINDOW])
            for j in range(n_slabs)]


def _swa_group(q_slabs, k_both, v_both, sinks, mask):
    qs = _stack_heads(q_slabs).astype(BF16)
    s = jnp.where(mask, _dot(qs, k_both.astype(BF16), 1, 1) * SCALE, NEG)
    sink = jnp.concatenate([jnp.broadcast_to(v, (WINDOW, 1)) for v in sinks], axis=0)
    return _unstack_heads(_dot(_softmax(s, sink).astype(BF16), v_both.astype(BF16), 1, 0), len(q_slabs))


def _mem_pair(q_slab, k_slab, v_slab):
    s = _dot(_stack_heads([q_slab]).astype(BF16), k_slab.astype(BF16), 1, 1) * SCALE
    return _unstack_heads(_dot(_softmax(s).astype(BF16), v_slab.astype(BF16), 1, 0), 1)[0]


def _gelu(x):
    return 0.5 * x * (1.0 + jnp.tanh(0.7978845608028654 * (x + 0.044715 * (x * x * x))))


def _gmlp_group(zu, zv, w, bcol, lg, lb, tri):
    u, v = _gelu(zu), _gelu(zv)
    mu = jnp.mean(v, axis=-1, keepdims=True)
    var = jnp.mean(jnp.square(v - mu), axis=-1, keepdims=True)
    vn = (v - mu) * lax.rsqrt(var + EPS) * lg + lb
    sv = _dot(jnp.where(tri, w, 0.0).astype(BF16), vn.astype(BF16), 1, 0) + bcol
    return u * sv


def _cols(x, width):
    return [x[:, j * width:(j + 1) * width] for j in range(x.shape[1] // width)]


def _swa_mask(has_prev):
    qi = lax.broadcasted_iota(jnp.int32, (GROUP * WINDOW, 2 * WINDOW), 0) & (WINDOW - 1)
    kj = lax.broadcasted_iota(jnp.int32, (GROUP * WINDOW, 2 * WINDOW), 1)
    in_prev = jnp.logical_and(jnp.logical_and(kj < WINDOW, kj > qi), has_prev)
    return jnp.logical_or(in_prev, jnp.logical_and(kj >= WINDOW, kj - WINDOW <= qi))


def _mix_a(q_slabs, k_boths, v_boths, sinks, qm_slabs, km_slabs, vm_slabs, mask):
    per = GROUP // 2
    outs = []
    for g in range(KV_HEADS):
        outs += _swa_group(q_slabs[per * g:per * (g + 1)], k_boths[g], v_boths[g], sinks[GROUP * g:GROUP * (g + 1)], mask)
    return outs + [_mem_pair(qm_slabs[j], km_slabs[j], vm_slabs[j]) for j in range(MEM_HEADS // 2)]


def _in_both_halves(prev, cur):
    cat = jnp.concatenate([prev, cur], axis=0)
    rolled = pltpu.roll(cat, HEAD_DIM, axis=1)
    low = _low_lanes()
    return [jnp.where(low, cat, rolled), jnp.where(low, rolled, cat)]


def _from_both_halves(d_boths):
    t = [d + pltpu.roll(d, HEAD_DIM, axis=1) for d in d_boths]
    return jnp.where(_low_lanes(), t[0], t[1])


def _mix_a_specs(nm, blk):
    prev = lambda n: jnp.maximum(blk(n) - 1, 0)
    return [pl.BlockSpec((WINDOW, Q_W), lambda n: (blk(n), 0)),
            pl.BlockSpec((WINDOW, KV_W), lambda n: (prev(n), Q_W // KV_W)),
            pl.BlockSpec((WINDOW, KV_W), lambda n: (blk(n), Q_W // KV_W)),
            pl.BlockSpec((WINDOW, KV_W), lambda n: (prev(n), Q_W // KV_W + 1)),
            pl.BlockSpec((WINDOW, KV_W), lambda n: (blk(n), Q_W // KV_W + 1)),
            pl.BlockSpec((WINDOW, MEM_W), lambda n: (blk(n), (Q_W + 2 * KV_W) // MEM_W)),
            pl.BlockSpec((16, 128), lambda n: (0, 0)),
            pl.BlockSpec((nm, MEM_W), lambda n: (0, 0)),
            pl.BlockSpec((nm, MEM_W), lambda n: (0, 1))]


def _mix_a_args(refs):
    q, kp, kc, vp, vc, qm, sk, km, vm = [r[...].astype(F32) for r in refs]
    return (_cols(q, 128), _in_both_halves(kp, kc), _in_both_halves(vp, vc), [sk[h:h + 1, 0:1] for h in range(Q_HEADS)],
            _cols(qm, 128), _cols(km, 128), _cols(vm, 128))


def mixer_a_fwd(proj, sk, kv, name):
    t, nm = proj.shape[0], kv.shape[0]

    def body(*refs):
        o_ref = refs[-1]
        slabs = _mix_a(*_mix_a_args(refs[:-1]), _swa_mask(pl.program_id(0) > 0))
        o_ref[...] = jnp.concatenate(slabs, axis=1).astype(o_ref.dtype)

    return pl.pallas_call(
        body, name=name, grid=(t // WINDOW,), in_specs=_mix_a_specs(nm, lambda n: n),
        out_specs=pl.BlockSpec((WINDOW, Q_W + MEM_W), lambda n: (n, 0)),
        out_shape=_sds((t, Q_W + MEM_W), BF16), compiler_params=_cp("parallel"))(proj, proj, proj, proj, proj, proj, sk, kv, kv)


def _onehot_rows(vals, shape):
    rows = lax.broadcasted_iota(jnp.int32, shape, 0)
    out = jnp.zeros(shape, F32)
    for h, v in enumerate(vals):
        out = out + jnp.where(rows == h, jnp.broadcast_to(v, shape), 0.0)
    return out


def mixer_a_bwd(proj, dcat, sk, kv, name):
    t, nm = proj.shape[0], kv.shape[0]
    nb = t // WINDOW
    blk = lambda i: nb - 1 - i

    def body(*refs):
        dcat_ref, dproj_ref, dsk_ref, dkv_ref, carry_ref = refs[9:]
        i = pl.program_id(0)

        @pl.when(i == 0)
        def _():
            carry_ref[...] = jnp.zeros_like(carry_ref)
            dsk_ref[...] = jnp.zeros_like(dsk_ref)
            dkv_ref[...] = jnp.zeros_like(dkv_ref)

        mask = _swa_mask(blk(i) > 0)
        _, vjp = jax.vjp(lambda *a: _mix_a(*a, mask), *_mix_a_args(refs[:9]))
        dqs, dk_boths, dv_boths, dsinks, dqms, dkms, dvms = vjp(_cols(dcat_ref[...].astype(F32), 128))
        dkv = jnp.concatenate([_from_both_halves(dk_boths), _from_both_halves(dv_boths)], axis=1)
        dkv_cur = dkv[WINDOW:] + carry_ref[...]
        carry_ref[...] = dkv[:WINDOW]
        dproj_ref[...] = jnp.concatenate(dqs + [dkv_cur] + dqms, axis=1).astype(dproj_ref.dtype)
        dsk_ref[...] += _onehot_rows(dsinks, (16, 128))
        dkv_ref[...] += jnp.concatenate(dkms + dvms, axis=1)

    width = Q_W + 2 * KV_W + MEM_W
    return pl.pallas_call(
        body, name=name, grid=(nb,),
        in_specs=_mix_a_specs(nm, blk) + [pl.BlockSpec((WINDOW, Q_W + MEM_W), lambda i: (blk(i), 0))],
        out_specs=[pl.BlockSpec((WINDOW, width), lambda i: (blk(i), 0)), pl.BlockSpec((16, 128), lambda i: (0, 0)),
                   pl.BlockSpec((nm, 2 * MEM_W), lambda i: (0, 0))],
        out_shape=[_sds((t, width), BF16), _sds((16, 128), F32), _sds((nm, 2 * MEM_W), F32)],
        scratch_shapes=[pltpu.VMEM((WINDOW, 2 * KV_W), F32)],
        compiler_params=_cp("arbitrary"))(proj, proj, proj, proj, proj, proj, sk, kv, kv, dcat)


def _mix_b(zus, zvs, ws, bcols, lgs, lbs, qms, kms, vms, tri):
    outs = [_gmlp_group(zus[g], zvs[g], ws[g], bcols[g], lgs[g], lbs[g], tri) for g in range(B_GROUPS)]
    return outs + [_mem_pair(qms[j], kms[j], vms[j]) for j in range(MEM_HEADS // 2)]


def _mix_b_specs(nm):
    return [pl.BlockSpec((WINDOW, 2 * B_W), lambda n: (n, 0)),
            pl.BlockSpec((WINDOW, MEM_W), lambda n: (n, 2 * B_W // MEM_W)),
            pl.BlockSpec((B_GROUPS, WINDOW, WINDOW), lambda n: (0, 0, 0)),
            pl.BlockSpec((WINDOW, 128), lambda n: (0, 0)),
            pl.BlockSpec((8, 128), lambda n: (0, 0)),
            pl.BlockSpec((8, 128), lambda n: (0, 0)),
            pl.BlockSpec((nm, MEM_W), lambda n: (0, 0)),
            pl.BlockSpec((nm, MEM_W), lambda n: (0, 1))]


def _mix_b_args(refs):
    z, qm, ws, bt, lg, lb, km, vm = [r[...].astype(F32) for r in refs]
    zs = _cols(z, 128)
    return (zs[:B_GROUPS], zs[B_GROUPS:], [ws[g] for g in range(B_GROUPS)], [bt[:, g:g + 1] for g in range(B_GROUPS)],
            [lg[g:g + 1, :] for g in range(B_GROUPS)], [lb[g:g + 1, :] for g in range(B_GROUPS)],
            _cols(qm, 128), _cols(km, 128), _cols(vm, 128))


def _tri():
    return lax.broadcasted_iota(jnp.int32, (WINDOW, WINDOW), 0) >= lax.broadcasted_iota(jnp.int32, (WINDOW, WINDOW), 1)


def mixer_b_fwd(proj, ws, bt, lg, lb, kv, name):
    t, nm = proj.shape[0], kv.shape[0]

    def body(*refs):
        o_ref = refs[-1]
        o_ref[...] = jnp.concatenate(_mix_b(*_mix_b_args(refs[:-1]), _tri()), axis=1).astype(o_ref.dtype)

    return pl.pallas_call(
        body, name=name, grid=(t // WINDOW,), in_specs=_mix_b_specs(nm),
        out_specs=pl.BlockSpec((WINDOW, B_W + MEM_W), lambda n: (n, 0)),
        out_shape=_sds((t, B_W + MEM_W), BF16), compiler_params=_cp("parallel"))(proj, proj, ws, bt, lg, lb, kv, kv)


def mixer_b_bwd(proj, dcat, ws, bt, lg, lb, kv, name):
    t, nm = proj.shape[0], kv.shape[0]

    def body(*refs):
        dcat_ref, dproj_ref, dws_ref, dbt_ref, dlg_ref, dlb_ref, dkv_ref = refs[8:]

        @pl.when(pl.program_id(0) == 0)
        def _():
            for r in (dws_ref, dbt_ref, dlg_ref, dlb_ref, dkv_ref):
                r[...] = jnp.zeros_like(r)

        tri = _tri()
        _, vjp = jax.vjp(lambda *a: _mix_b(*a, tri), *_mix_b_args(refs[:8]))
        dzus, dzvs, dws, dbcols, dlgs, dlbs, dqms, dkms, dvms = vjp(_cols(dcat_ref[...].astype(F32), 128))
        dproj_ref[...] = jnp.concatenate(dzus + dzvs + dqms, axis=1).astype(dproj_ref.dtype)
        for g in range(B_GROUPS):
            dws_ref[g] += dws[g]
        lanes = lax.broadcasted_iota(jnp.int32, (WINDOW, 128), 1)
        dbt = jnp.zeros((WINDOW, 128), F32)
        for g in range(B_GROUPS):
            dbt = dbt + jnp.where(lanes == g, jnp.broadcast_to(dbcols[g], (WINDOW, 128)), 0.0)
        dbt_ref[...] += dbt
        dlg_ref[...] += _onehot_rows(dlgs, (8, 128))
        dlb_ref[...] += _onehot_rows(dlbs, (8, 128))
        dkv_ref[...] += jnp.concatenate(dkms + dvms, axis=1)

    width = 2 * B_W + MEM_W
    const2 = lambda n: (0, 0)
    return pl.pallas_call(
        body, name=name, grid=(t // WINDOW,),
        in_specs=_mix_b_specs(nm) + [pl.BlockSpec((WINDOW, B_W + MEM_W), lambda n: (n, 0))],
        out_specs=[pl.BlockSpec((WINDOW, width), lambda n: (n, 0)),
                   pl.BlockSpec((B_GROUPS, WINDOW, WINDOW), lambda n: (0, 0, 0)),
                   pl.BlockSpec((WINDOW, 128), const2), pl.BlockSpec((8, 128), const2), pl.BlockSpec((8, 128), const2),
                   pl.BlockSpec((nm, 2 * MEM_W), const2)],
        out_shape=[_sds((t, width), BF16), _sds((B_GROUPS, WINDOW, WINDOW), F32), _sds((WINDOW, 128), F32),
                   _sds((8, 128), F32), _sds((8, 128), F32), _sds((nm, 2 * MEM_W), F32)],
        compiler_params=_cp("arbitrary"))(proj, proj, ws, bt, lg, lb, kv, kv, dcat)


def _adamw_update(w, g, m, v):
    m2 = ADAM_B1 * m + (1.0 - ADAM_B1) * g
    v2 = ADAM_B2 * v + (1.0 - ADAM_B2) * jnp.square(g)
    m_hat = m2 / (1.0 - ADAM_B1 ** ADAM_STEP)
    v_hat = v2 / (1.0 - ADAM_B2 ** ADAM_STEP)
    return -ADAM_LR * (m_hat / (jnp.sqrt(v_hat) + ADAM_EPS) + ADAM_WD * w), m2, v2


def adamw(w, g, m, v, name):
    r, c = w.shape
    tr = _tile(r, (512, 352, 256, 128, 64, 32, 16, 8))

    def body(w_ref, g_ref, m_ref, v_ref, d_ref, nm_ref, nv_ref):
        d_ref[...], nm_ref[...], nv_ref[...] = _adamw_update(w_ref[...], g_ref[...], m_ref[...], v_ref[...])

    spec = pl.BlockSpec((tr, c), lambda i: (i, 0))
    return pl.pallas_call(
        body, name=name, grid=(r // tr,), in_specs=[spec] * 4, out_specs=[spec] * 3,
        out_shape=[_sds((r, c), F32)] * 3, compiler_params=_cp("parallel"))(w, g, m, v)


def adamw_halves(w, g_mine, g_theirs, m, v, c_arr, name):
    r, c = w.shape
    tr = _tile(r // 2, (256, 352, 128, 64, 32, 16, 8))
    per_half = r // 2 // tr

    def body(c_ref, w_ref, gm_ref, gt_ref, m_ref, v_ref, g_ref, d_ref, nm_ref, nv_ref):
        g = jnp.where(pl.program_id(0) // per_half == c_ref[0], gm_ref[...], gt_ref[...])
        g_ref[...] = g
        d_ref[...], nm_ref[...], nv_ref[...] = _adamw_update(w_ref[...], g, m_ref[...], v_ref[...])

    spec = pl.BlockSpec((tr, c), lambda i, cr: (i, 0))
    half = pl.BlockSpec((tr, c), lambda i, cr: (i % per_half, 0))
    return pl.pallas_call(
        body, name=name,
        grid_spec=pltpu.PrefetchScalarGridSpec(num_scalar_prefetch=1, grid=(r // tr,), in_specs=[spec, half, half, spec, spec],
                                               out_specs=[spec] * 4),
        out_shape=[_sds((r, c), F32)] * 4, compiler_params=_cp("parallel"))(c_arr, w, g_mine, g_theirs, m, v)


def _place():
    return lax.axis_index("x"), lax.axis_index("y"), lax.axis_index("c")


def _other_chips(x, y):
    return [(1 - x, y), (x, 1 - y), (1 - x, 1 - y)]


def _remote(src, dst, send_sems, recv_sems, k, dev):
    return pltpu.make_async_remote_copy(src_ref=src, dst_ref=dst, send_sem=send_sems.at[k], recv_sem=recv_sems.at[k],
                                        device_id=dev, device_id_type=MESH)


def allgather_weights(shards, name):
    nw = len(shards)

    def body(*refs):
        ins, outs = refs[:nw], refs[nw:2 * nw]
        send_sems, recv_sems = refs[2 * nw:]
        x, y, c = _place()
        chips = _other_chips(x, y)
        halves = [(pl.ds(c * (r.shape[0] // 2), r.shape[0] // 2), pl.ds((1 - c) * (r.shape[0] // 2), r.shape[0] // 2)) for r in ins]
        first = [_remote(ins[w].at[halves[w][0]], outs[w].at[halves[w][0], 2 * x + y], send_sems, recv_sems, 6 * w + j, (*chip, c))
                 for j, chip in enumerate(chips) for w in range(nw)]
        for cp in first:
            cp.start()
        passed = []
        for j, chip in enumerate(chips):
            for w in range(nw):
                blk = outs[w].at[halves[w][0], 2 * chip[0] + chip[1]]
                _remote(blk, blk, send_sems, recv_sems, 6 * w + j, (x, y, c)).wait_recv()
                cp = _remote(blk, blk, send_sems, recv_sems, 6 * w + 3 + j, (x, y, 1 - c))
                cp.start()
                passed.append(cp)
        for j, chip in enumerate(chips):
            for w in range(nw):
                blk = outs[w].at[halves[w][1], 2 * chip[0] + chip[1]]
                _remote(blk, blk, send_sems, recv_sems, 6 * w + 3 + j, (x, y, c)).wait_recv()
        for cp in first + passed:
            cp.wait_send()

    return pl.pallas_call(
        body, name=name, in_specs=[HBM_SPEC] * nw, out_specs=[HBM_SPEC] * nw,
        out_shape=[_sds((s.shape[0], 4) + s.shape[1:], s.dtype) for s in shards],
        scratch_shapes=[pltpu.SemaphoreType.DMA((6 * nw,)), pltpu.SemaphoreType.DMA((6 * nw,))],
        compiler_params=pltpu.CompilerParams(has_side_effects=True))(*shards)


def sibling_swap_halves(gs, name):
    nw = len(gs)

    def body(*refs):
        ins, outs = refs[:nw], refs[nw:2 * nw]
        send_sems, recv_sems = refs[2 * nw:]
        x, y, c = _place()
        copies = [_remote(ins[w].at[pl.ds((1 - c) * outs[w].shape[0], outs[w].shape[0])], outs[w], send_sems, recv_sems, w, (x, y, 1 - c))
                  for w in range(nw)]
        for cp in copies:
            cp.start()
        for cp in copies:
            cp.wait()

    return pl.pallas_call(
        body, name=name, in_specs=[HBM_SPEC] * nw, out_specs=[HBM_SPEC] * nw,
        out_shape=[_sds((g.shape[0] // 2,) + g.shape[1:], g.dtype) for g in gs],
        scratch_shapes=[pltpu.SemaphoreType.DMA((nw,)), pltpu.SemaphoreType.DMA((nw,))],
        compiler_params=pltpu.CompilerParams(has_side_effects=True))(*gs)


def chips_exchange(sbs, name):
    nw = len(sbs)

    def body(*refs):
        ins, outs = refs[:nw], refs[nw:2 * nw]
        send_sems, recv_sems = refs[2 * nw:]
        x, y, c = _place()
        copies = [_remote(ins[w].at[:, 2 * chip[0] + chip[1]], outs[w].at[j], send_sems, recv_sems, 3 * w + j, (*chip, c))
                  for j, chip in enumerate(_other_chips(x, y)) for w in range(nw)]
        for cp in copies:
            cp.start()
        for cp in copies:
            cp.wait()

    return pl.pallas_call(
        body, name=name, in_specs=[HBM_SPEC] * nw, out_specs=[HBM_SPEC] * nw,
        out_shape=[_sds((3, s.shape[0]) + s.shape[2:], s.dtype) for s in sbs],
        scratch_shapes=[pltpu.SemaphoreType.DMA((3 * nw,)), pltpu.SemaphoreType.DMA((3 * nw,))],
        compiler_params=pltpu.CompilerParams(has_side_effects=True))(*sbs)


def sibling_swap(fs, name):
    nw = len(fs)

    def body(*refs):
        ins, outs = refs[:nw], refs[nw:2 * nw]
        send_sems, recv_sems = refs[2 * nw:]
        x, y, c = _place()
        copies = [_remote(ins[w], outs[w], send_sems, recv_sems, w, (x, y, 1 - c)) for w in range(nw)]
        for cp in copies:
            cp.start()
        for cp in copies:
            cp.wait()

    return pl.pallas_call(
        body, name=name, in_specs=[HBM_SPEC] * nw, out_specs=[HBM_SPEC] * nw,
        out_shape=[_sds(f.shape, f.dtype) for f in fs],
        scratch_shapes=[pltpu.SemaphoreType.DMA((nw,)), pltpu.SemaphoreType.DMA((nw,))],
        compiler_params=pltpu.CompilerParams(has_side_effects=True))(*fs)


def _half_tile(a):
    return _tile(a, (256, 352, 128, 64, 32, 16))


def chip_partial_sums(g, r1, c_arr, name):
    hl, _, a, b = r1.shape
    ta = _half_tile(a)

    def body(c_ref, g_ref, r_ref, o_ref):
        o_ref[...] = (g_ref[...] + r_ref[...]).astype(o_ref.dtype)

    blk = (None, None, ta, b)
    return pl.pallas_call(
        body, name=name,
        grid_spec=pltpu.PrefetchScalarGridSpec(
            num_scalar_prefetch=1, grid=(hl, 4, a // ta),
            in_specs=[pl.BlockSpec(blk, lambda l, s, i, c: (c[0] * hl + l, s, i, 0)), pl.BlockSpec(blk, lambda l, s, i, c: (l, s, i, 0))],
            out_specs=pl.BlockSpec(blk, lambda l, s, i, c: (l, s, i, 0))),
        out_shape=_sds(r1.shape, BF16), compiler_params=_cp("parallel", "parallel", "parallel"))(c_arr, g, r1)


def shard_total(g, r1, r2, cs_arr, name):
    hl, _, a, b = r1.shape
    ta = _half_tile(a)

    def body(cs_ref, g_ref, r1_ref, p0_ref, p1_ref, p2_ref, o_ref):
        o_ref[...] = (((g_ref[...] + r1_ref[...]) + p0_ref[...].astype(F32)) + p1_ref[...].astype(F32)) + p2_ref[...].astype(F32)

    blk4, blk3 = (None, None, ta, b), (None, ta, b)
    peer = lambda k: pl.BlockSpec((None, None, ta, b), lambda l, i, cs: (k, l, i, 0))
    return pl.pallas_call(
        body, name=name,
        grid_spec=pltpu.PrefetchScalarGridSpec(
            num_scalar_prefetch=1, grid=(hl, a // ta),
            in_specs=[pl.BlockSpec(blk4, lambda l, i, cs: (cs[0] * hl + l, cs[1], i, 0)),
                      pl.BlockSpec(blk4, lambda l, i, cs: (l, cs[1], i, 0)), peer(0), peer(1), peer(2)],
            out_specs=pl.BlockSpec(blk3, lambda l, i, cs: (l, i, 0))),
        out_shape=_sds((hl, a, b), F32), compiler_params=_cp("parallel", "parallel"))(cs_arr, g, r1, r2, r2, r2)


def allgather_small(v, name):
    r, n = v.shape

    def body(x_ref, out_ref, send_sems, recv_sems, local_sem):
        x, y, c = _place()
        me, sibling = (x, y, c), (x, y, 1 - c)
        chips = _other_chips(x, y)

        def rows(px, py, pc):
            return out_ref.at[pl.ds((4 * px + 2 * py + pc) * r, r), :]

        def copy(k, block, to, src=None):
            return _remote(rows(*block) if src is None else src, rows(*block), send_sems, recv_sems, k, to)

        mine = pltpu.make_async_copy(x_ref, rows(*me), local_sem)
        mine.start()
        first = [copy(0, me, sibling, src=x_ref)] + [copy(1 + j, me, (*chip, c), src=x_ref) for j, chip in enumerate(chips)]
        for cp in first:
            cp.start()
        passed = [copy(4 + j, (*chip, c), sibling) for j, chip in enumerate(chips)]
        for j, chip in enumerate(chips):
            copy(1 + j, (*chip, c), me).wait_recv()
            passed[j].start()
        copy(0, sibling, me).wait_recv()
        for j, chip in enumerate(chips):
            copy(4 + j, (*chip, 1 - c), me).wait_recv()
        for cp in first + passed:
            cp.wait_send()
        mine.wait()

    return pl.pallas_call(
        body, name=name, in_specs=[VMEM_SPEC], out_specs=VMEM_SPEC, out_shape=_sds((8 * r, n), v.dtype),
        scratch_shapes=[pltpu.SemaphoreType.DMA((7,)), pltpu.SemaphoreType.DMA((7,)), pltpu.SemaphoreType.DMA],
        compiler_params=pltpu.CompilerParams(has_side_effects=True, vmem_limit_bytes=V7X_VMEM_LIMIT_BYTES))(v)


def sum_devices(v8, name):
    _, r, n = v8.shape
    tr = _tile(r, (88, 64, 32, 16, 8))

    def body(v_ref, o_ref):
        acc = v_ref[0]
        for d in range(1, 8):
            acc = acc + v_ref[d]
        o_ref[...] = acc

    return pl.pallas_call(
        body, name=name, grid=(r // tr,), in_specs=[pl.BlockSpec((8, tr, n), lambda i: (0, i, 0))],
        out_specs=pl.BlockSpec((tr, n), lambda i: (i, 0)), out_shape=_sds((r, n), F32), compiler_params=_cp("parallel"))(v8)


SHARDED = (("a_w_in", 2), ("a_w_out", 1), ("b_w_in", 2), ("b_w_out", 1), ("w_mem_kv", 1), ("w_gate_up", 2), ("w_down", 1))


def _full_from_gathered(wg, axis):
    l, _, a, b = wg.shape
    if axis == 1:
        return wg.reshape(l, 4 * a, b)
    return wg.transpose(0, 2, 1, 3).reshape(l, a, 4 * b)


def _by_shard(dw, axis):
    l, k, n = dw.shape
    if axis == 1:
        return dw.reshape(l, 4, k // 4, n)
    return dw.reshape(l, k, 4, n // 4).transpose(0, 2, 1, 3)


def _pack(arrs):
    parts = []
    for a in arrs:
        flat = a.reshape(-1)
        flat = jnp.pad(flat, (0, -flat.shape[0] % 1024))
        parts.append(flat.reshape(-1, 128))
    return jnp.concatenate(parts, axis=0)


def _unpack(buf, like):
    out, row = [], 0
    for a in like:
        size = 1
        for s in a.shape:
            size *= s
        rows = -(-size // 1024) * 8
        out.append(buf[row:row + rows].reshape(-1)[:size].reshape(a.shape))
        row += rows
    return out


def kernel(x, mem, mem_norm_g, mix_norm_g, ffn_norm_g, final_norm_g, a_w_in, a_sinks, a_w_out, b_w_in, b_w_s, b_bias_s, b_ln_g, b_ln_b, b_w_out, w_mem_kv, w_gate_up, w_down, loss_target, m_mem_norm_g, m_mix_norm_g, m_ffn_norm_g, m_final_norm_g, m_a_w_in, m_a_sinks, m_a_w_out, m_b_w_in, m_b_w_s, m_b_bias_s, m_b_ln_g, m_b_ln_b, m_b_w_out, m_w_mem_kv, m_w_gate_up, m_w_down, v_mem_norm_g, v_mix_norm_g, v_ffn_norm_g, v_final_norm_g, v_a_w_in, v_a_sinks, v_a_w_out, v_b_w_in, v_b_w_s, v_b_bias_s, v_b_ln_g, v_b_ln_b, v_b_w_out, v_w_mem_kv, v_w_gate_up, v_w_down):
    given = dict(locals())
    depth = mix_norm_g.shape[0]
    d = x.shape[-1]
    xi, yi, ci = _place()
    c_arr = jnp.stack([ci]).astype(jnp.int32)
    cs_arr = jnp.stack([ci, 2 * xi + yi]).astype(jnp.int32)

    own = [given[n].astype(BF16) for n, _ in SHARDED]
    gathered = allgather_weights(own, "allgather_weights")
    gathered = [lax.dynamic_update_slice(wg, w[:, None], (0, 2 * xi + yi, 0, 0)) for wg, w in zip(gathered, own)]
    full = {n: _full_from_gathered(wg, ax) for (n, ax), wg in zip(SHARDED, gathered)}

    h = x.reshape(-1, d)
    tgt = loss_target.reshape(-1, d)
    mem2 = mem.reshape(-1, d)
    row = lambda v: v.reshape(1, -1)

    mem_n = rmsnorm_fwd(mem2, row(mem_norm_g), "mem_norm")
    saved = []
    for i in range(depth):
        j = i // 2
        kv = matmul(mem_n, full["w_mem_kv"][i], "nn", BF16, "mem_kv")
        xn = rmsnorm_fwd(h, row(mix_norm_g[i]), "mix_norm")
        if i % 2 == 0:
            w_in, w_out = full["a_w_in"][j], full["a_w_out"][j]
            sk = jnp.pad(jnp.broadcast_to(a_sinks[j][:, None], (Q_HEADS, 128)), ((0, 16 - Q_HEADS), (0, 0)))
            proj = matmul(xn, w_in, "nn", BF16, "a_in")
            cat = mixer_a_fwd(proj, sk, kv, "mixer_a")
            extra = (sk,)
        else:
            w_in, w_out = full["b_w_in"][j], full["b_w_out"][j]
            bt = jnp.pad(b_bias_s[j].T, ((0, 0), (0, 128 - B_GROUPS)))
            lg = jnp.pad(b_ln_g[j], ((0, 8 - B_GROUPS), (0, 0)))
            lb = jnp.pad(b_ln_b[j], ((0, 8 - B_GROUPS), (0, 0)))
            proj = matmul(xn, w_in, "nn", BF16, "b_in")
            cat = mixer_b_fwd(proj, b_w_s[j], bt, lg, lb, kv, "mixer_b")
            extra = (b_w_s[j], bt, lg, lb)
        h_mid = matmul(cat, w_out, "nn", F32, "mix_out", res=h)
        hn = rmsnorm_fwd(h_mid, row(ffn_norm_g[i]), "ffn_norm")
        gu = matmul(hn, full["w_gate_up"][i], "nn", BF16, "gate_up")
        act = swiglu_fwd(gu, "swiglu")
        h_out = matmul(act, full["w_down"][i], "nn", F32, "down", res=h_mid)
        saved.append((h, xn, proj, cat, h_mid, hn, gu, act, kv, w_in, w_out, extra))
        h = h_out

    loss_part, dh, d_final_g = loss_head(h, row(final_norm_g), tgt, "loss_head")
    loss = lax.psum(loss_part[0, 0], ("x", "y", "c"))

    dw = {n: [None] * given[n].shape[0] for n, _ in SHARDED}
    d_mix_g, d_ffn_g = [None] * depth, [None] * depth
    d_sinks, d_ws, d_bias, d_lg, d_lb = [], [], [], [], []
    d_mem_n = jnp.zeros(mem2.shape, F32)
    for i in reversed(range(depth)):
        j = i // 2
        h_in, xn, proj, cat, h_mid, hn, gu, act, kv, w_in, w_out, extra = saved[i]
        dact = matmul(dh, full["w_down"][i], "nt", F32, "down_dx", tn=1408)
        dw["w_down"][i] = matmul(act, dh, "tn", F32, "down_dw", tm=1408)
        dgu = swiglu_bwd(gu, dact, "swiglu_bwd")
        dw["w_gate_up"][i] = matmul(hn, dgu, "tn", F32, "gate_up_dw")
        dhn = matmul(dgu, full["w_gate_up"][i], "nt", F32, "gate_up_dx")
        dh, d_ffn_g[i] = rmsnorm_bwd(h_mid, row(ffn_norm_g[i]), dhn, dh, "ffn_norm_bwd")
        dcat = matmul(dh, w_out, "nt", F32, "mix_out_dx")
        if i % 2 == 0:
            dw["a_w_out"][j] = matmul(cat, dh, "tn", F32, "mix_out_dw")
            dproj, dsk, dkv = mixer_a_bwd(proj, dcat, extra[0], kv, "mixer_a_bwd")
            d_sinks.insert(0, dsk[:Q_HEADS, 0])
            dw["a_w_in"][j] = matmul(xn, dproj, "tn", F32, "a_in_dw", tn=640)
            dxn = matmul(dproj, w_in, "nt", F32, "a_in_dx")
        else:
            dw["b_w_out"][j] = matmul(cat, dh, "tn", F32, "mix_out_dw")
            dproj, dws, dbt, dlg, dlb, dkv = mixer_b_bwd(proj, dcat, *extra, kv, "mixer_b_bwd")
            d_ws.insert(0, dws)
            d_bias.insert(0, dbt[:, :B_GROUPS].T)
            d_lg.insert(0, dlg[:B_GROUPS])
            d_lb.insert(0, dlb[:B_GROUPS])
            dw["b_w_in"][j] = matmul(xn, dproj, "tn", F32, "b_in_dw", tn=896)
            dxn = matmul(dproj, w_in, "nt", F32, "b_in_dx")
        dw["w_mem_kv"][i] = matmul(mem_n, dkv, "tn", F32, "mem_kv_dw")
        d_mem_n = matmul(dkv, full["w_mem_kv"][i], "nt", F32, "mem_kv_dx", res=d_mem_n)
        dh, d_mix_g[i] = rmsnorm_bwd(h_in, row(mix_norm_g[i]), dxn, dh, "mix_norm_bwd")
    grad_x = dh.reshape(x.shape)
    _, d_mem_g = rmsnorm_bwd(mem2, row(mem_norm_g), d_mem_n, jnp.zeros(mem2.shape, F32), "mem_norm_bwd")

    by_shard = [_by_shard(jnp.stack(dw[n]), ax) for n, ax in SHARDED]
    from_sibling = sibling_swap_halves(by_shard, "grads_sibling_swap")
    partial = [chip_partial_sums(g, r1, c_arr, "grads_chip_sum") for g, r1 in zip(by_shard, from_sibling)]
    from_chips = chips_exchange(partial, "grads_chips_exchange")
    totals = [shard_total(g, r1, r2, cs_arr, "grads_shard_total") for g, r1, r2 in zip(by_shard, from_sibling, from_chips)]
    totals_sibling = sibling_swap(totals, "grads_sibling_totals")

    out = {}
    for (n, _), g_mine, g_theirs in zip(SHARDED, totals, totals_sibling):
        shape = given[n].shape
        two_d = lambda a: a.reshape(-1, shape[-1])
        res = adamw_halves(two_d(given[n]), two_d(g_mine), two_d(g_theirs), two_d(given["m_" + n]), two_d(given["v_" + n]),
                           c_arr, "adamw")
        out[n] = tuple(r.reshape(shape) for r in res)

    small = ("mem_norm_g", "mix_norm_g", "ffn_norm_g", "final_norm_g", "a_sinks", "b_w_s", "b_bias_s", "b_ln_g", "b_ln_b")
    small_g = [d_mem_g[0], jnp.concatenate(d_mix_g, axis=0), jnp.concatenate(d_ffn_g, axis=0), d_final_g[0],
               jnp.stack(d_sinks), jnp.stack(d_ws), jnp.stack(d_bias), jnp.stack(d_lg), jnp.stack(d_lb)]
    packed = _pack(small_g)
    g_small = sum_devices(allgather_small(packed, "small_allgather").reshape(8, *packed.shape), "small_sum")
    like = [given[n] for n in small]
    delta_s, new_m_s, new_v_s = adamw(_pack(like), g_small, _pack([given["m_" + n] for n in small]),
                                      _pack([given["v_" + n] for n in small]), "adamw_small")
    for n, g, dl, nm_, nv_ in zip(small, _unpack(g_small, like), _unpack(delta_s, like), _unpack(new_m_s, like), _unpack(new_v_s, like)):
        out[n] = (g, dl, nm_, nv_)

    order = ("mem_norm_g", "mix_norm_g", "ffn_norm_g", "final_norm_g", "a_w_in", "a_sinks", "a_w_out", "b_w_in", "b_w_s",
             "b_bias_s", "b_ln_g", "b_ln_b", "b_w_out", "w_mem_kv", "w_gate_up", "w_down")
    return (loss, grad_x, *[out[n][0] for n in order], *[out[n][1] for n in order],
            *[out[n][2] for n in order], *[out[n][3] for n in order])
```

```python
import jax
import jax.numpy as jnp
from jax import lax
from jax.experimental import pallas as pl
from jax.experimental.pallas import tpu as pltpu

F32, BF16 = jnp.float32, jnp.bfloat16
EPS = 1e-6
HEAD_DIM = 64
Q_HEADS, KV_HEADS, GROUP = 12, 2, 6
WINDOW = 128
MEM_HEADS = 4
B_GROUPS = 6
Q_W, KV_W, MEM_W, B_W = 768, 128, 256, 768
SCALE = HEAD_DIM ** -0.5
NEG = -1e30
ADAM_LR, ADAM_B1, ADAM_B2, ADAM_EPS, ADAM_WD, ADAM_STEP = 0.001, 0.9, 0.999, 1e-08, 0.01, 10
V7X_VMEM_LIMIT_BYTES = 48 * 1024 * 1024
MESH = pl.DeviceIdType.MESH
HBM_SPEC = pl.BlockSpec(memory_space=pltpu.HBM)
VMEM_SPEC = pl.BlockSpec(memory_space=pltpu.VMEM)


def _cp(*sem):
    return pltpu.CompilerParams(dimension_semantics=sem or None, vmem_limit_bytes=V7X_VMEM_LIMIT_BYTES)


def _tile(n, cands):
    for t in cands:
        if n % t == 0:
            return t
    return n


def _sds(shape, dtype):
    return jax.ShapeDtypeStruct(tuple(shape), dtype)


def _dot(a, b, ca, cb):
    return lax.dot_general(a, b, (((ca,), (cb,)), ((), ())), preferred_element_type=F32)


def _rms(x, g):
    return x * lax.rsqrt(jnp.mean(x * x, axis=-1, keepdims=True) + EPS) * g


def rmsnorm_fwd(h, g, name):
    t, d = h.shape
    tm = _tile(t, (512, 256, 128))

    def body(h_ref, g_ref, o_ref):
        o_ref[...] = _rms(h_ref[...], g_ref[...]).astype(o_ref.dtype)

    return pl.pallas_call(
        body, name=name, grid=(t // tm,),
        in_specs=[pl.BlockSpec((tm, d), lambda i: (i, 0)), pl.BlockSpec((1, d), lambda i: (0, 0))],
        out_specs=pl.BlockSpec((tm, d), lambda i: (i, 0)),
        out_shape=_sds((t, d), BF16), compiler_params=_cp("parallel"))(h, g)


def rmsnorm_bwd(h, g, dxn, dres, name):
    t, d = h.shape
    tm = _tile(t, (512, 256, 128))

    def body(h_ref, g_ref, dxn_ref, dres_ref, dh_ref, dg_ref):
        _, vjp = jax.vjp(_rms, h_ref[...], g_ref[...])
        dh, dg = vjp(dxn_ref[...].astype(F32))
        dh_ref[...] = dres_ref[...] + dh

        @pl.when(pl.program_id(0) == 0)
        def _():
            dg_ref[...] = jnp.zeros_like(dg_ref)

        dg_ref[...] += dg

    row = pl.BlockSpec((tm, d), lambda i: (i, 0))
    vec = pl.BlockSpec((1, d), lambda i: (0, 0))
    return pl.pallas_call(
        body, name=name, grid=(t // tm,), in_specs=[row, vec, row, row], out_specs=[row, vec],
        out_shape=[_sds((t, d), F32), _sds((1, d), F32)], compiler_params=_cp("arbitrary"))(h, g, dxn, dres)


def loss_head(h, g, tgt, name):
    t, d = h.shape
    tm = _tile(t, (512, 256, 128))

    def body(h_ref, g_ref, t_ref, l_ref, dh_ref, dg_ref):
        y, vjp = jax.vjp(_rms, h_ref[...], g_ref[...])
        err = y - t_ref[...]
        dh, dg = vjp(err * (1.0 / d))
        dh_ref[...] = dh
        part = 0.5 * jnp.sum(jnp.mean(err * err, axis=-1, keepdims=True), axis=0, keepdims=True)

        @pl.when(pl.program_id(0) == 0)
        def _():
            dg_ref[...] = jnp.zeros_like(dg_ref)
            l_ref[...] = jnp.zeros_like(l_ref)

        dg_ref[...] += dg
        l_ref[...] += part

    row = pl.BlockSpec((tm, d), lambda i: (i, 0))
    vec = pl.BlockSpec((1, d), lambda i: (0, 0))
    one = pl.BlockSpec((1, 1), lambda i: (0, 0))
    return pl.pallas_call(
        body, name=name, grid=(t // tm,), in_specs=[row, vec, row], out_specs=[one, row, vec],
        out_shape=[_sds((1, 1), F32), _sds((t, d), F32), _sds((1, d), F32)], compiler_params=_cp("arbitrary"))(h, g, tgt)


def _logical(op):
    arr, lead = op if isinstance(op, tuple) else (op, None)
    planes = arr.shape[-3] if arr.ndim - (lead is not None) == 3 else 1
    return arr, lead, arr.shape[-2], arr.shape[-1], planes


def _spec(op, rows_t, cols_t, row_of, col_of):
    arr, lead, _, cols, _ = _logical(op)
    per = cols // cols_t
    lead = () if lead is None else (lead,)
    if arr.ndim - len(lead) == 2:
        return pl.BlockSpec((None,) * len(lead) + (rows_t, cols_t), lambda *g: lead + (row_of(*g), col_of(*g)))
    return pl.BlockSpec((None,) * len(lead) + (None, rows_t, cols_t),
                        lambda *g: lead + (col_of(*g) // per, row_of(*g), col_of(*g) % per))


def _arr(op):
    return op[0] if isinstance(op, tuple) else op


def matmul(a, b, mode, out_dtype, name, res=None, tm=1024, tn=512, tk=2816, out_planes=None, out_into=None):
    _, _, ar, ac, ap = _logical(a)
    _, _, br, bc, bp = _logical(b)
    if mode == "nn":
        m, ka, kb, n = ar, ac * ap, br, bc * bp
        n_plane, ka_plane, kb_plane = bc, ac, br
    elif mode == "nt":
        m, ka, n, kb = ar, ac * ap, br, bc * bp
        n_plane, ka_plane, kb_plane = br, ac, bc
    else:
        ka, m, kb, n = ar, ac * ap, br, bc * bp
        n_plane, ka_plane, kb_plane = bc, ar, br
    m_plane = ac if mode == "tn" else ar
    assert ka == kb, name
    k = ka
    kind, planes = out_planes or ("cols", 1)
    if kind == "cols":
        n_plane = min(n_plane, n // planes)
    tm = _tile(m_plane, (tm, 1408, 1024, 512, 256, 128))
    if kind == "rows" and tm % (m // planes):
        tm = m_plane
    tn = _tile(n_plane, (tn, 1408, 896, 640, 512, 256, 128))
    tk = _tile(min(ka_plane, kb_plane), (tk, 2816, 1792, 1408, 1280, 1024, 512, 256, 128))
    nk = k // tk
    row_i, col_j, red = (lambda i, j, kk: i), (lambda i, j, kk: j), (lambda i, j, kk: kk)
    if mode == "nn":
        a_spec, b_spec, ca, cb = _spec(a, tm, tk, row_i, red), _spec(b, tk, tn, red, col_j), 1, 0
    elif mode == "nt":
        a_spec, b_spec, ca, cb = _spec(a, tm, tk, row_i, red), _spec(b, tn, tk, col_j, red), 1, 1
    else:
        a_spec, b_spec, ca, cb = _spec(a, tk, tm, red, row_i), _spec(b, tk, tn, red, col_j), 0, 0
    lead = () if out_into is None else (out_into[1],)
    if planes == 1:
        o_shape, o_block = (m, n), (tm, tn)
        o_index = lambda i, j, kk: lead + (i, j)
    elif kind == "cols":
        per = n // planes // tn
        o_shape, o_block = (planes, m, n // planes), (None, tm, tn)
        o_index = lambda i, j, kk: lead + (j // per, i, j % per)
    else:
        o_shape, o_block = (planes, m // planes, n), (tm // (m // planes), m // planes, tn)
        o_index = lambda i, j, kk: lead + (i, 0, j)
    o_spec = pl.BlockSpec((None,) * len(lead) + o_block, o_index)
    if out_into is not None:
        assert out_into[0].shape[1:] == o_shape and out_into[0].dtype == out_dtype, name
        o_shape = out_into[0].shape
    has_res = res is not None
    n_in = 2 + has_res + (out_into is not None)

    def body(*refs):
        a_ref, b_ref = refs[:2]
        rest = refs[2:2 + has_res] + refs[n_in:]
        o_ref = rest[1] if has_res else rest[0]
        p = _dot(a_ref[...].astype(BF16), b_ref[...].astype(BF16), ca, cb)
        if nk == 1:
            if has_res:
                p = p + rest[0][...]
            o_ref[...] = p.astype(o_ref.dtype).reshape(o_ref.shape)
        else:
            acc_ref = rest[-1]
            kk = pl.program_id(2)

            @pl.when(kk == 0)
            def _():
                acc_ref[...] = p

            @pl.when(kk > 0)
            def _():
                acc_ref[...] += p

            @pl.when(kk == nk - 1)
            def _():
                r = acc_ref[...]
                if has_res:
                    r = r + rest[0][...]
                o_ref[...] = r.astype(o_ref.dtype).reshape(o_ref.shape)

    operands = [_arr(a), _arr(b)] + ([res] if has_res else []) + ([out_into[0]] if out_into is not None else [])
    return pl.pallas_call(
        body, name=name, grid=(m // tm, n // tn, nk),
        in_specs=[a_spec, b_spec] + ([pl.BlockSpec((tm, tn), lambda i, j, kk: (i, j))] if has_res else [])
        + ([pl.BlockSpec(memory_space=pl.ANY)] if out_into is not None else []),
        out_specs=o_spec, out_shape=_sds(o_shape, out_dtype),
        input_output_aliases={n_in - 1: 0} if out_into is not None else {},
        scratch_shapes=[pltpu.VMEM((tm, tn), F32)] if nk > 1 else [],
        compiler_params=_cp("parallel", "parallel", "arbitrary"))(*operands)


def _swiglu(gate, up):
    return gate / (1.0 + jnp.exp(-gate)) * up


def gate_up_fwd(hn, w, layer, name):
    t, d = hn.shape
    half = w.shape[-1]
    tm = _tile(t, (512, 256, 128))

    def body(a_ref, wg_ref, wu_ref, gu_ref, act_ref):
        a = a_ref[...]
        g, u = _dot(a, wg_ref[...], 1, 0), _dot(a, wu_ref[...], 1, 0)
        gu_ref[0] = g.astype(gu_ref.dtype)
        gu_ref[1] = u.astype(gu_ref.dtype)
        act_ref[...] = _swiglu(g, u).astype(act_ref.dtype)

    return pl.pallas_call(
        body, name=name, grid=(2, t // tm),
        in_specs=[pl.BlockSpec((tm, d), lambda j, i: (i, 0)),
                  pl.BlockSpec((None, None, d, half), lambda j, i: (layer, j, 0, 0)),
                  pl.BlockSpec((None, None, d, half), lambda j, i: (layer, 2 + j, 0, 0))],
        out_specs=[pl.BlockSpec((2, tm, half), lambda j, i: (0, i, j)), pl.BlockSpec((tm, half), lambda j, i: (i, j))],
        out_shape=[_sds((2, t, 2 * half), BF16), _sds((t, 2 * half), BF16)],
        compiler_params=_cp("parallel", "parallel"))(hn, w, w)


def down_dx_swiglu_bwd(dh, wd, gu, name):
    t, d = dh.shape
    w, layer = wd
    f = w.shape[-2]
    tm = _tile(t, (512, 256, 128))
    tn = _tile(f, (1408, 512, 256, 128))

    def body(dh_ref, w_ref, gu_ref, o_ref):
        dact = _dot(dh_ref[...].astype(BF16), w_ref[...], 1, 1)
        _, vjp = jax.vjp(_swiglu, gu_ref[0].astype(F32), gu_ref[1].astype(F32))
        dg, du = vjp(dact)
        o_ref[0] = dg.astype(o_ref.dtype)
        o_ref[1] = du.astype(o_ref.dtype)

    planes = pl.BlockSpec((2, tm, tn), lambda j, i: (0, i, j))
    return pl.pallas_call(
        body, name=name, grid=(f // tn, t // tm),
        in_specs=[pl.BlockSpec((tm, d), lambda j, i: (i, 0)), pl.BlockSpec((None, tn, d), lambda j, i: (layer, j, 0)), planes],
        out_specs=planes, out_shape=_sds((2, t, f), BF16), compiler_params=_cp("parallel", "parallel"))(dh, w, gu)


def _softmax(s, sink=None):
    m = s.max(axis=-1, keepdims=True)
    if sink is not None:
        m = jnp.maximum(m, sink)
    m = lax.stop_gradient(m)
    e = jnp.exp(s - m)
    den = e.sum(axis=-1, keepdims=True)
    if sink is not None:
        den = den + jnp.exp(sink - m)
    return e * (1.0 / den)


def _low_lanes():
    return lax.broadcasted_iota(jnp.int32, (1, 128), 1) < HEAD_DIM


def _stack_heads(slabs):
    low = _low_lanes()
    return jnp.concatenate([p for s in slabs for p in (jnp.where(low, s, 0.0), jnp.where(low, 0.0, s))], axis=0)


def _unstack_heads(o, n_slabs):
    low = _low_lanes()
    return [jnp.where(low, o[2 * j * WINDOW:(2 * j + 1) * WINDOW], o[(2 * j + 1) * WINDOW:(2 * j + 2) * WINDOW])
            for j in range(n_slabs)]


def _swa_group(q_slabs, k_both, v_both, sinks, mask):
    qs = _stack_heads(q_slabs).astype(BF16)
    s = jnp.where(mask, _dot(qs, k_both.astype(BF16), 1, 1) * SCALE, NEG)
    sink = jnp.concatenate([jnp.broadcast_to(v, (WINDOW, 1)) for v in sinks], axis=0)
    return _unstack_heads(_dot(_softmax(s, sink).astype(BF16), v_both.astype(BF16), 1, 0), len(q_slabs))


def _mem_pair(q_slab, k_slab, v_slab):
    s = _dot(_stack_heads([q_slab]).astype(BF16), k_slab.astype(BF16), 1, 1) * SCALE
    return _unstack_heads(_dot(_softmax(s).astype(BF16), v_slab.astype(BF16), 1, 0), 1)[0]


def _gelu(x):
    return 0.5 * x * (1.0 + jnp.tanh(0.7978845608028654 * (x + 0.044715 * (x * x * x))))


def _gmlp_group(zu, zv, w, bcol, lg, lb, tri):
    u, v = _gelu(zu), _gelu(zv)
    mu = jnp.mean(v, axis=-1, keepdims=True)
    var = jnp.mean(jnp.square(v - mu), axis=-1, keepdims=True)
    vn = (v - mu) * lax.rsqrt(var + EPS) * lg + lb
    sv = _dot(jnp.where(tri, w, 0.0).astype(BF16), vn.astype(BF16), 1, 0) + bcol
    return u * sv


def _cols(x, width):
    return [x[:, j * width:(j + 1) * width] for j in range(x.shape[1] // width)]


def _swa_mask(has_prev):
    qi = lax.broadcasted_iota(jnp.int32, (GROUP * WINDOW, 2 * WINDOW), 0) & (WINDOW - 1)
    kj = lax.broadcasted_iota(jnp.int32, (GROUP * WINDOW, 2 * WINDOW), 1)
    in_prev = jnp.logical_and(jnp.logical_and(kj < WINDOW, kj > qi), has_prev)
    return jnp.logical_or(in_prev, jnp.logical_and(kj >= WINDOW, kj - WINDOW <= qi))


def _mix_a(q_slabs, k_boths, v_boths, sinks, qm_slabs, km_slabs, vm_slabs, mask):
    per = GROUP // 2
    outs = []
    for g in range(KV_HEADS):
        outs += _swa_group(q_slabs[per * g:per * (g + 1)], k_boths[g], v_boths[g], sinks[GROUP * g:GROUP * (g + 1)], mask)
    return outs + [_mem_pair(qm_slabs[j], km_slabs[j], vm_slabs[j]) for j in range(MEM_HEADS // 2)]


def _in_both_halves(prev, cur):
    cat = jnp.concatenate([prev, cur], axis=0)
    rolled = pltpu.roll(cat, HEAD_DIM, axis=1)
    low = _low_lanes()
    return [jnp.where(low, cat, rolled), jnp.where(low, rolled, cat)]


def _from_both_halves(d_boths):
    t = [d + pltpu.roll(d, HEAD_DIM, axis=1) for d in d_boths]
    return jnp.where(_low_lanes(), t[0], t[1])


def _mix_a_specs(nm, blk):
    prev = lambda n: jnp.maximum(blk(n) - 1, 0)
    return [pl.BlockSpec((WINDOW, Q_W), lambda n: (blk(n), 0)),
            pl.BlockSpec((WINDOW, KV_W), lambda n: (prev(n), Q_W // KV_W)),
            pl.BlockSpec((WINDOW, KV_W), lambda n: (blk(n), Q_W // KV_W)),
            pl.BlockSpec((WINDOW, KV_W), lambda n: (prev(n), Q_W // KV_W + 1)),
            pl.BlockSpec((WINDOW, KV_W), lambda n: (blk(n), Q_W // KV_W + 1)),
            pl.BlockSpec((WINDOW, MEM_W), lambda n: (blk(n), (Q_W + 2 * KV_W) // MEM_W)),
            pl.BlockSpec((16, 128), lambda n: (0, 0)),
            pl.BlockSpec((nm, MEM_W), lambda n: (0, 0)),
            pl.BlockSpec((nm, MEM_W), lambda n: (0, 1))]


def _mix_a_args(refs):
    q, kp, kc, vp, vc, qm, sk, km, vm = [r[...].astype(F32) for r in refs]
    return (_cols(q, 128), _in_both_halves(kp, kc), _in_both_halves(vp, vc), [sk[h:h + 1, 0:1] for h in range(Q_HEADS)],
            _cols(qm, 128), _cols(km, 128), _cols(vm, 128))


def mixer_a_fwd(proj, sk, kv, name):
    t, nm = proj.shape[0], kv.shape[0]

    def body(*refs):
        o_ref = refs[-1]
        slabs = _mix_a(*_mix_a_args(refs[:-1]), _swa_mask(pl.program_id(0) > 0))
        o_ref[...] = jnp.concatenate(slabs, axis=1).astype(o_ref.dtype)

    return pl.pallas_call(
        body, name=name, grid=(t // WINDOW,), in_specs=_mix_a_specs(nm, lambda n: n),
        out_specs=pl.BlockSpec((WINDOW, Q_W + MEM_W), lambda n: (n, 0)),
        out_shape=_sds((t, Q_W + MEM_W), BF16), compiler_params=_cp("parallel"))(proj, proj, proj, proj, proj, proj, sk, kv, kv)


def _onehot_rows(vals, shape):
    rows = lax.broadcasted_iota(jnp.int32, shape, 0)
    out = jnp.zeros(shape, F32)
    for h, v in enumerate(vals):
        out = out + jnp.where(rows == h, jnp.broadcast_to(v, shape), 0.0)
    return out


def mixer_a_bwd(proj, dcat, sk, kv, name):
    t, nm = proj.shape[0], kv.shape[0]
    nb = t // WINDOW
    blk = lambda i: nb - 1 - i

    def body(*refs):
        dcat_ref, dproj_ref, dsk_ref, dkv_ref, carry_ref = refs[9:]
        i = pl.program_id(0)

        @pl.when(i == 0)
        def _():
            carry_ref[...] = jnp.zeros_like(carry_ref)
            dsk_ref[...] = jnp.zeros_like(dsk_ref)
            dkv_ref[...] = jnp.zeros_like(dkv_ref)

        mask = _swa_mask(blk(i) > 0)
        _, vjp = jax.vjp(lambda *a: _mix_a(*a, mask), *_mix_a_args(refs[:9]))
        dqs, dk_boths, dv_boths, dsinks, dqms, dkms, dvms = vjp(_cols(dcat_ref[...].astype(F32), 128))
        dkv = jnp.concatenate([_from_both_halves(dk_boths), _from_both_halves(dv_boths)], axis=1)
        dkv_cur = dkv[WINDOW:] + carry_ref[...]
        carry_ref[...] = dkv[:WINDOW]
        dproj_ref[...] = jnp.concatenate(dqs + [dkv_cur] + dqms, axis=1).astype(dproj_ref.dtype)
        dsk_ref[...] += _onehot_rows(dsinks, (16, 128))
        dkv_ref[...] += jnp.concatenate(dkms + dvms, axis=1)

    width = Q_W + 2 * KV_W + MEM_W
    return pl.pallas_call(
        body, name=name, grid=(nb,),
        in_specs=_mix_a_specs(nm, blk) + [pl.BlockSpec((WINDOW, Q_W + MEM_W), lambda i: (blk(i), 0))],
        out_specs=[pl.BlockSpec((WINDOW, width), lambda i: (blk(i), 0)), pl.BlockSpec((16, 128), lambda i: (0, 0)),
                   pl.BlockSpec((nm, 2 * MEM_W), lambda i: (0, 0))],
        out_shape=[_sds((t, width), BF16), _sds((16, 128), F32), _sds((nm, 2 * MEM_W), F32)],
        scratch_shapes=[pltpu.VMEM((WINDOW, 2 * KV_W), F32)],
        compiler_params=_cp("arbitrary"))(proj, proj, proj, proj, proj, proj, sk, kv, kv, dcat)


def _mix_b(zus, zvs, ws, bcols, lgs, lbs, qms, kms, vms, tri):
    outs = [_gmlp_group(zus[g], zvs[g], ws[g], bcols[g], lgs[g], lbs[g], tri) for g in range(B_GROUPS)]
    return outs + [_mem_pair(qms[j], kms[j], vms[j]) for j in range(MEM_HEADS // 2)]


def _mix_b_specs(nm):
    return [pl.BlockSpec((WINDOW, 2 * B_W), lambda n: (n, 0)),
            pl.BlockSpec((WINDOW, MEM_W), lambda n: (n, 2 * B_W // MEM_W)),
            pl.BlockSpec((B_GROUPS, WINDOW, WINDOW), lambda n: (0, 0, 0)),
            pl.BlockSpec((WINDOW, 128), lambda n: (0, 0)),
            pl.BlockSpec((8, 128), lambda n: (0, 0)),
            pl.BlockSpec((8, 128), lambda n: (0, 0)),
            pl.BlockSpec((nm, MEM_W), lambda n: (0, 0)),
            pl.BlockSpec((nm, MEM_W), lambda n: (0, 1))]


def _mix_b_args(refs):
    z, qm, ws, bt, lg, lb, km, vm = [r[...].astype(F32) for r in refs]
    zs = _cols(z, 128)
    return (zs[:B_GROUPS], zs[B_GROUPS:], [ws[g] for g in range(B_GROUPS)], [bt[:, g:g + 1] for g in range(B_GROUPS)],
            [lg[g:g + 1, :] for g in range(B_GROUPS)], [lb[g:g + 1, :] for g in range(B_GROUPS)],
            _cols(qm, 128), _cols(km, 128), _cols(vm, 128))


def _tri():
    return lax.broadcasted_iota(jnp.int32, (WINDOW, WINDOW), 0) >= lax.broadcasted_iota(jnp.int32, (WINDOW, WINDOW), 1)


def mixer_b_fwd(proj, ws, bt, lg, lb, kv, name):
    t, nm = proj.shape[0], kv.shape[0]

    def body(*refs):
        o_ref = refs[-1]
        o_ref[...] = jnp.concatenate(_mix_b(*_mix_b_args(refs[:-1]), _tri()), axis=1).astype(o_ref.dtype)

    return pl.pallas_call(
        body, name=name, grid=(t // WINDOW,), in_specs=_mix_b_specs(nm),
        out_specs=pl.BlockSpec((WINDOW, B_W + MEM_W), lambda n: (n, 0)),
        out_shape=_sds((t, B_W + MEM_W), BF16), compiler_params=_cp("parallel"))(proj, proj, ws, bt, lg, lb, kv, kv)


def mixer_b_bwd(proj, dcat, ws, bt, lg, lb, kv, name):
    t, nm = proj.shape[0], kv.shape[0]

    def body(*refs):
        dcat_ref, dproj_ref, dws_ref, dbt_ref, dlg_ref, dlb_ref, dkv_ref = refs[8:]

        @pl.when(pl.program_id(0) == 0)
        def _():
            for r in (dws_ref, dbt_ref, dlg_ref, dlb_ref, dkv_ref):
                r[...] = jnp.zeros_like(r)

        tri = _tri()
        _, vjp = jax.vjp(lambda *a: _mix_b(*a, tri), *_mix_b_args(refs[:8]))
        dzus, dzvs, dws, dbcols, dlgs, dlbs, dqms, dkms, dvms = vjp(_cols(dcat_ref[...].astype(F32), 128))
        dproj_ref[...] = jnp.concatenate(dzus + dzvs + dqms, axis=1).astype(dproj_ref.dtype)
        for g in range(B_GROUPS):
            dws_ref[g] += dws[g]
        lanes = lax.broadcasted_iota(jnp.int32, (WINDOW, 128), 1)
        dbt = jnp.zeros((WINDOW, 128), F32)
        for g in range(B_GROUPS):
            dbt = dbt + jnp.where(lanes == g, jnp.broadcast_to(dbcols[g], (WINDOW, 128)), 0.0)
        dbt_ref[...] += dbt
        dlg_ref[...] += _onehot_rows(dlgs, (8, 128))
        dlb_ref[...] += _onehot_rows(dlbs, (8, 128))
        dkv_ref[...] += jnp.concatenate(dkms + dvms, axis=1)

    width = 2 * B_W + MEM_W
    const2 = lambda n: (0, 0)
    return pl.pallas_call(
        body, name=name, grid=(t // WINDOW,),
        in_specs=_mix_b_specs(nm) + [pl.BlockSpec((WINDOW, B_W + MEM_W), lambda n: (n, 0))],
        out_specs=[pl.BlockSpec((WINDOW, width), lambda n: (n, 0)),
                   pl.BlockSpec((B_GROUPS, WINDOW, WINDOW), lambda n: (0, 0, 0)),
                   pl.BlockSpec((WINDOW, 128), const2), pl.BlockSpec((8, 128), const2), pl.BlockSpec((8, 128), const2),
                   pl.BlockSpec((nm, 2 * MEM_W), const2)],
        out_shape=[_sds((t, width), BF16), _sds((B_GROUPS, WINDOW, WINDOW), F32), _sds((WINDOW, 128), F32),
                   _sds((8, 128), F32), _sds((8, 128), F32), _sds((nm, 2 * MEM_W), F32)],
        compiler_params=_cp("arbitrary"))(proj, proj, ws, bt, lg, lb, kv, kv, dcat)


def _adamw_update(w, g, m, v):
    m2 = ADAM_B1 * m + (1.0 - ADAM_B1) * g
    v2 = ADAM_B2 * v + (1.0 - ADAM_B2) * jnp.square(g)
    m_hat = m2 / (1.0 - ADAM_B1 ** ADAM_STEP)
    v_hat = v2 / (1.0 - ADAM_B2 ** ADAM_STEP)
    return -ADAM_LR * (m_hat / (jnp.sqrt(v_hat) + ADAM_EPS) + ADAM_WD * w), m2, v2


def adamw(w, g, m, v, name):
    r, c = w.shape
    tr = _tile(r, (512, 352, 256, 128, 64, 32, 16, 8))

    def body(w_ref, g_ref, m_ref, v_ref, d_ref, nm_ref, nv_ref):
        d_ref[...], nm_ref[...], nv_ref[...] = _adamw_update(w_ref[...], g_ref[...], m_ref[...], v_ref[...])

    spec = pl.BlockSpec((tr, c), lambda i: (i, 0))
    return pl.pallas_call(
        body, name=name, grid=(r // tr,), in_specs=[spec] * 4, out_specs=[spec] * 3,
        out_shape=[_sds((r, c), F32)] * 3, compiler_params=_cp("parallel"))(w, g, m, v)


def adamw_halves(w, g_mine, g_theirs, m, v, c_arr, name):
    r, c = w.shape
    tr = _tile(r // 2, (256, 352, 128, 64, 32, 16, 8))
    per_half = r // 2 // tr

    def body(c_ref, w_ref, gm_ref, gt_ref, m_ref, v_ref, g_ref, d_ref, nm_ref, nv_ref):
        g = jnp.where(pl.program_id(0) // per_half == c_ref[0], gm_ref[...], gt_ref[...])
        g_ref[...] = g
        d_ref[...], nm_ref[...], nv_ref[...] = _adamw_update(w_ref[...], g, m_ref[...], v_ref[...])

    spec = pl.BlockSpec((tr, c), lambda i, cr: (i, 0))
    half = pl.BlockSpec((tr, c), lambda i, cr: (i % per_half, 0))
    return pl.pallas_call(
        body, name=name,
        grid_spec=pltpu.PrefetchScalarGridSpec(num_scalar_prefetch=1, grid=(r // tr,), in_specs=[spec, half, half, spec, spec],
                                               out_specs=[spec] * 4),
        out_shape=[_sds((r, c), F32)] * 4, compiler_params=_cp("parallel"))(c_arr, w, g_mine, g_theirs, m, v)


def _place():
    return lax.axis_index("x"), lax.axis_index("y"), lax.axis_index("c")


def _other_chips(x, y):
    return [(1 - x, y), (x, 1 - y), (1 - x, 1 - y)]


def _remote(src, dst, send_sems, recv_sems, k, dev):
    return pltpu.make_async_remote_copy(src_ref=src, dst_ref=dst, send_sem=send_sems.at[k], recv_sem=recv_sems.at[k],
                                        device_id=dev, device_id_type=MESH)


def allgather_weights(shards, name):
    nw = len(shards)

    def body(*refs):
        ins, outs = refs[:nw], refs[nw:2 * nw]
        send_sems, recv_sems = refs[2 * nw:]
        x, y, c = _place()
        chips = _other_chips(x, y)
        halves = [(pl.ds(c * (r.shape[0] // 2), r.shape[0] // 2), pl.ds((1 - c) * (r.shape[0] // 2), r.shape[0] // 2)) for r in ins]
        first = [_remote(ins[w].at[halves[w][0]], outs[w].at[halves[w][0], 2 * x + y], send_sems, recv_sems, 6 * w + j, (*chip, c))
                 for j, chip in enumerate(chips) for w in range(nw)]
        for cp in first:
            cp.start()
        passed = []
        for j, chip in enumerate(chips):
            for w in range(nw):
                blk = outs[w].at[halves[w][0], 2 * chip[0] + chip[1]]
                _remote(blk, blk, send_sems, recv_sems, 6 * w + j, (x, y, c)).wait_recv()
                cp = _remote(blk, blk, send_sems, recv_sems, 6 * w + 3 + j, (x, y, 1 - c))
                cp.start()
                passed.append(cp)
        for j, chip in enumerate(chips):
            for w in range(nw):
                blk = outs[w].at[halves[w][1], 2 * chip[0] + chip[1]]
                _remote(blk, blk, send_sems, recv_sems, 6 * w + 3 + j, (x, y, c)).wait_recv()
        for cp in first + passed:
            cp.wait_send()

    return pl.pallas_call(
        body, name=name, in_specs=[HBM_SPEC] * nw, out_specs=[HBM_SPEC] * nw,
        out_shape=[_sds((s.shape[0], 4) + s.shape[1:], s.dtype) for s in shards],
        scratch_shapes=[pltpu.SemaphoreType.DMA((6 * nw,)), pltpu.SemaphoreType.DMA((6 * nw,))],
        compiler_params=pltpu.CompilerParams(has_side_effects=True))(*shards)


def sibling_swap_halves(gs, name):
    nw = len(gs)

    def body(*refs):
        ins, outs = refs[:nw], refs[nw:2 * nw]
        send_sems, recv_sems = refs[2 * nw:]
        x, y, c = _place()
        copies = [_remote(ins[w].at[pl.ds((1 - c) * outs[w].shape[0], outs[w].shape[0])], outs[w], send_sems, recv_sems, w, (x, y, 1 - c))
                  for w in range(nw)]
        for cp in copies:
            cp.start()
        for cp in copies:
            cp.wait()

    return pl.pallas_call(
        body, name=name, in_specs=[HBM_SPEC] * nw, out_specs=[HBM_SPEC] * nw,
        out_shape=[_sds((g.shape[0] // 2,) + g.shape[1:], g.dtype) for g in gs],
        scratch_shapes=[pltpu.SemaphoreType.DMA((nw,)), pltpu.SemaphoreType.DMA((nw,))],
        compiler_params=pltpu.CompilerParams(has_side_effects=True))(*gs)


def chips_exchange(sbs, name):
    nw = len(sbs)

    def body(*refs):
        ins, outs = refs[:nw], refs[nw:2 * nw]
        send_sems, recv_sems = refs[2 * nw:]
        x, y, c = _place()
        copies = [_remote(ins[w].at[:, 2 * chip[0] + chip[1]], outs[w].at[j], send_sems, recv_sems, 3 * w + j, (*chip, c))
                  for j, chip in enumerate(_other_chips(x, y)) for w in range(nw)]
        for cp in copies:
            cp.start()
        for cp in copies:
            cp.wait()

    return pl.pallas_call(
        body, name=name, in_specs=[HBM_SPEC] * nw, out_specs=[HBM_SPEC] * nw,
        out_shape=[_sds((3, s.shape[0]) + s.shape[2:], s.dtype) for s in sbs],
        scratch_shapes=[pltpu.SemaphoreType.DMA((3 * nw,)), pltpu.SemaphoreType.DMA((3 * nw,))],
        compiler_params=pltpu.CompilerParams(has_side_effects=True))(*sbs)


def sibling_swap(fs, name):
    nw = len(fs)

    def body(*refs):
        ins, outs = refs[:nw], refs[nw:2 * nw]
        send_sems, recv_sems = refs[2 * nw:]
        x, y, c = _place()
        copies = [_remote(ins[w], outs[w], send_sems, recv_sems, w, (x, y, 1 - c)) for w in range(nw)]
        for cp in copies:
            cp.start()
        for cp in copies:
            cp.wait()

    return pl.pallas_call(
        body, name=name, in_specs=[HBM_SPEC] * nw, out_specs=[HBM_SPEC] * nw,
        out_shape=[_sds(f.shape, f.dtype) for f in fs],
        scratch_shapes=[pltpu.SemaphoreType.DMA((nw,)), pltpu.SemaphoreType.DMA((nw,))],
        compiler_params=pltpu.CompilerParams(has_side_effects=True))(*fs)


def _half_tile(a):
    return _tile(a, (256, 352, 128, 64, 32, 16))


def chip_partial_sums(g, r1, c_arr, name):
    hl, _, a, b = r1.shape
    ta = _half_tile(a)

    def body(c_ref, g_ref, r_ref, o_ref):
        o_ref[...] = (g_ref[...] + r_ref[...]).astype(o_ref.dtype)

    blk = (None, None, ta, b)
    return pl.pallas_call(
        body, name=name,
        grid_spec=pltpu.PrefetchScalarGridSpec(
            num_scalar_prefetch=1, grid=(hl, 4, a // ta),
            in_specs=[pl.BlockSpec(blk, lambda l, s, i, c: (c[0] * hl + l, s, i, 0)), pl.BlockSpec(blk, lambda l, s, i, c: (l, s, i, 0))],
            out_specs=pl.BlockSpec(blk, lambda l, s, i, c: (l, s, i, 0))),
        out_shape=_sds(r1.shape, BF16), compiler_params=_cp("parallel", "parallel", "parallel"))(c_arr, g, r1)


def shard_total(g, r1, r2, cs_arr, name):
    hl, _, a, b = r1.shape
    ta = _half_tile(a)

    def body(cs_ref, g_ref, r1_ref, p0_ref, p1_ref, p2_ref, o_ref):
        o_ref[...] = (((g_ref[...] + r1_ref[...]) + p0_ref[...].astype(F32)) + p1_ref[...].astype(F32)) + p2_ref[...].astype(F32)

    blk4, blk3 = (None, None, ta, b), (None, ta, b)
    peer = lambda k: pl.BlockSpec((None, None, ta, b), lambda l, i, cs: (k, l, i, 0))
    return pl.pallas_call(
        body, name=name,
        grid_spec=pltpu.PrefetchScalarGridSpec(
            num_scalar_prefetch=1, grid=(hl, a // ta),
            in_specs=[pl.BlockSpec(blk4, lambda l, i, cs: (cs[0] * hl + l, cs[1], i, 0)),
                      pl.BlockSpec(blk4, lambda l, i, cs: (l, cs[1], i, 0)), peer(0), peer(1), peer(2)],
            out_specs=pl.BlockSpec(blk3, lambda l, i, cs: (l, i, 0))),
        out_shape=_sds((hl, a, b), F32), compiler_params=_cp("parallel", "parallel"))(cs_arr, g, r1, r2, r2, r2)


def allgather_small(v, name):
    r, n = v.shape

    def body(x_ref, out_ref, send_sems, recv_sems, local_sem):
        x, y, c = _place()
        me, sibling = (x, y, c), (x, y, 1 - c)
        chips = _other_chips(x, y)

        def rows(px, py, pc):
            return out_ref.at[pl.ds((4 * px + 2 * py + pc) * r, r), :]

        def copy(k, block, to, src=None):
            return _remote(rows(*block) if src is None else src, rows(*block), send_sems, recv_sems, k, to)

        mine = pltpu.make_async_copy(x_ref, rows(*me), local_sem)
        mine.start()
        first = [copy(0, me, sibling, src=x_ref)] + [copy(1 + j, me, (*chip, c), src=x_ref) for j, chip in enumerate(chips)]
        for cp in first:
            cp.start()
        passed = [copy(4 + j, (*chip, c), sibling) for j, chip in enumerate(chips)]
        for j, chip in enumerate(chips):
            copy(1 + j, (*chip, c), me).wait_recv()
            passed[j].start()
        copy(0, sibling, me).wait_recv()
        for j, chip in enumerate(chips):
            copy(4 + j, (*chip, 1 - c), me).wait_recv()
        for cp in first + passed:
            cp.wait_send()
        mine.wait()

    return pl.pallas_call(
        body, name=name, in_specs=[VMEM_SPEC], out_specs=VMEM_SPEC, out_shape=_sds((8 * r, n), v.dtype),
        scratch_shapes=[pltpu.SemaphoreType.DMA((7,)), pltpu.SemaphoreType.DMA((7,)), pltpu.SemaphoreType.DMA],
        compiler_params=pltpu.CompilerParams(has_side_effects=True, vmem_limit_bytes=V7X_VMEM_LIMIT_BYTES))(v)


def sum_devices(v8, name):
    _, r, n = v8.shape
    tr = _tile(r, (88, 64, 32, 16, 8))

    def body(v_ref, o_ref):
        acc = v_ref[0]
        for d in range(1, 8):
            acc = acc + v_ref[d]
        o_ref[...] = acc

    return pl.pallas_call(
        body, name=name, grid=(r // tr,), in_specs=[pl.BlockSpec((8, tr, n), lambda i: (0, i, 0))],
        out_specs=pl.BlockSpec((tr, n), lambda i: (i, 0)), out_shape=_sds((r, n), F32), compiler_params=_cp("parallel"))(v8)


SHARDED = (("a_w_in", 2), ("a_w_out", 1), ("b_w_in", 2), ("b_w_out", 1), ("w_mem_kv", 1), ("w_gate_up", 2), ("w_down", 1))


def _full_from_gathered(wg, axis):
    l, _, a, b = wg.shape
    if axis == 1:
        return wg.reshape(l, 4 * a, b)
    return wg.transpose(0, 2, 1, 3).reshape(l, a, 4 * b)


def _by_shard(dw, axis):
    l, k, n = dw.shape
    if axis == 1:
        return dw.reshape(l, 4, k // 4, n)
    return dw.reshape(l, k, 4, n // 4).transpose(0, 2, 1, 3)


def _pack(arrs):
    parts = []
    for a in arrs:
        flat = a.reshape(-1)
        flat = jnp.pad(flat, (0, -flat.shape[0] % 1024))
        parts.append(flat.reshape(-1, 128))
    return jnp.concatenate(parts, axis=0)


def _unpack(buf, like):
    out, row = [], 0
    for a in like:
        size = 1
        for s in a.shape:
            size *= s
        rows = -(-size // 1024) * 8
        out.append(buf[row:row + rows].reshape(-1)[:size].reshape(a.shape))
        row += rows
    return out


def kernel(x, mem, mem_norm_g, mix_norm_g, ffn_norm_g, final_norm_g, a_w_in, a_sinks, a_w_out, b_w_in, b_w_s, b_bias_s, b_ln_g, b_ln_b, b_w_out, w_mem_kv, w_gate_up, w_down, loss_target, m_mem_norm_g, m_mix_norm_g, m_ffn_norm_g, m_final_norm_g, m_a_w_in, m_a_sinks, m_a_w_out, m_b_w_in, m_b_w_s, m_b_bias_s, m_b_ln_g, m_b_ln_b, m_b_w_out, m_w_mem_kv, m_w_gate_up, m_w_down, v_mem_norm_g, v_mix_norm_g, v_ffn_norm_g, v_final_norm_g, v_a_w_in, v_a_sinks, v_a_w_out, v_b_w_in, v_b_w_s, v_b_bias_s, v_b_ln_g, v_b_ln_b, v_b_w_out, v_w_mem_kv, v_w_gate_up, v_w_down):
    given = dict(locals())
    depth = mix_norm_g.shape[0]
    d = x.shape[-1]
    xi, yi, ci = _place()
    c_arr = jnp.stack([ci]).astype(jnp.int32)
    cs_arr = jnp.stack([ci, 2 * xi + yi]).astype(jnp.int32)

    own = [given[n].astype(BF16) for n, _ in SHARDED]
    gathered = allgather_weights(own, "allgather_weights")
    gathered = [lax.dynamic_update_slice(wg, w[:, None], (0, 2 * xi + yi, 0, 0)) for wg, w in zip(gathered, own)]
    gathered = dict(zip([n for n, _ in SHARDED], gathered))
    w_gate_up4 = gathered["w_gate_up"]
    full = {n: _full_from_gathered(gathered[n], ax) for n, ax in SHARDED if n != "w_gate_up"}
    weight = lambda n, l: (full[n], l)

    h = x.reshape(-1, d)
    tgt = loss_target.reshape(-1, d)
    mem2 = mem.reshape(-1, d)
    row = lambda v: v.reshape(1, -1)

    mem_n = rmsnorm_fwd(mem2, row(mem_norm_g), "mem_norm")
    saved = []
    for i in range(depth):
        j = i // 2
        kv = matmul(mem_n, weight("w_mem_kv", i), "nn", BF16, "mem_kv")
        xn = rmsnorm_fwd(h, row(mix_norm_g[i]), "mix_norm")
        if i % 2 == 0:
            w_in, w_out = weight("a_w_in", j), weight("a_w_out", j)
            sk = jnp.pad(jnp.broadcast_to(a_sinks[j][:, None], (Q_HEADS, 128)), ((0, 16 - Q_HEADS), (0, 0)))
            proj = matmul(xn, w_in, "nn", BF16, "a_in")
            cat = mixer_a_fwd(proj, sk, kv, "mixer_a")
            extra = (sk,)
        else:
            w_in, w_out = weight("b_w_in", j), weight("b_w_out", j)
            bt = jnp.pad(b_bias_s[j].T, ((0, 0), (0, 128 - B_GROUPS)))
            lg = jnp.pad(b_ln_g[j], ((0, 8 - B_GROUPS), (0, 0)))
            lb = jnp.pad(b_ln_b[j], ((0, 8 - B_GROUPS), (0, 0)))
            proj = matmul(xn, w_in, "nn", BF16, "b_in")
            cat = mixer_b_fwd(proj, b_w_s[j], bt, lg, lb, kv, "mixer_b")
            extra = (b_w_s[j], bt, lg, lb)
        h_mid = matmul(cat, w_out, "nn", F32, "mix_out", res=h)
        hn = rmsnorm_fwd(h_mid, row(ffn_norm_g[i]), "ffn_norm")
        gu, act = gate_up_fwd(hn, w_gate_up4, i, "gate_up")
        h_out = matmul(act, weight("w_down", i), "nn", F32, "down", res=h_mid)
        saved.append((h, xn, proj, cat, h_mid, hn, gu, act, kv, w_in, w_out, extra))
        h = h_out

    loss_part, dh, d_final_g = loss_head(h, row(final_norm_g), tgt, "loss_head")
    loss = lax.psum(loss_part[0, 0], ("x", "y", "c"))

    dw = {n: lax.empty((given[n].shape[0], 4) + given[n].shape[1:], F32) for n, _ in SHARDED if n not in ("a_w_in", "b_w_in")}
    dw_in = {"a_w_in": [None] * a_w_in.shape[0], "b_w_in": [None] * b_w_in.shape[0]}
    d_mix_g, d_ffn_g = [None] * depth, [None] * depth
    d_sinks, d_ws, d_bias, d_lg, d_lb = [], [], [], [], []
    d_mem_n = jnp.zeros(mem2.shape, F32)
    for i in reversed(range(depth)):
        j = i // 2
        h_in, xn, proj, cat, h_mid, hn, gu, act, kv, w_in, w_out, extra = saved[i]
        dgu = down_dx_swiglu_bwd(dh, weight("w_down", i), gu, "down_dx")
        dw["w_down"] = matmul(act, dh, "tn", F32, "down_dw", tm=1408, out_planes=("rows", 4), out_into=(dw["w_down"], i))
        dw["w_gate_up"] = matmul(hn, dgu, "tn", F32, "gate_up_dw", tn=1408, out_planes=("cols", 4), out_into=(dw["w_gate_up"], i))
        dhn = matmul(dgu, (w_gate_up4, i), "nt", F32, "gate_up_dx", tk=1408)
        dh, d_ffn_g[i] = rmsnorm_bwd(h_mid, row(ffn_norm_g[i]), dhn, dh, "ffn_norm_bwd")
        dcat = matmul(dh, w_out, "nt", F32, "mix_out_dx")
        w_out_name = "a_w_out" if i % 2 == 0 else "b_w_out"
        dw[w_out_name] = matmul(cat, dh, "tn", F32, "mix_out_dw", out_planes=("rows", 4), out_into=(dw[w_out_name], j))
        if i % 2 == 0:
            dproj, dsk, dkv = mixer_a_bwd(proj, dcat, extra[0], kv, "mixer_a_bwd")
            d_sinks.insert(0, dsk[:Q_HEADS, 0])
            dw_in["a_w_in"][j] = matmul(xn, dproj, "tn", F32, "a_in_dw", tn=640)
            dxn = matmul(dproj, w_in, "nt", F32, "a_in_dx")
        else:
            dproj, dws, dbt, dlg, dlb, dkv = mixer_b_bwd(proj, dcat, *extra, kv, "mixer_b_bwd")
            d_ws.insert(0, dws)
            d_bias.insert(0, dbt[:, :B_GROUPS].T)
            d_lg.insert(0, dlg[:B_GROUPS])
            d_lb.insert(0, dlb[:B_GROUPS])
            dw_in["b_w_in"][j] = matmul(xn, dproj, "tn", F32, "b_in_dw", tn=896)
            dxn = matmul(dproj, w_in, "nt", F32, "b_in_dx")
        dw["w_mem_kv"] = matmul(mem_n, dkv, "tn", F32, "mem_kv_dw", out_planes=("rows", 4), out_into=(dw["w_mem_kv"], i))
        d_mem_n = matmul(dkv, weight("w_mem_kv", i), "nt", F32, "mem_kv_dx", res=d_mem_n)
        dh, d_mix_g[i] = rmsnorm_bwd(h_in, row(mix_norm_g[i]), dxn, dh, "mix_norm_bwd")
    grad_x = dh.reshape(x.shape)
    _, d_mem_g = rmsnorm_bwd(mem2, row(mem_norm_g), d_mem_n, jnp.zeros(mem2.shape, F32), "mem_norm_bwd")

    by_shard = [dw[n] if n in dw else _by_shard(jnp.stack(dw_in[n]), ax) for n, ax in SHARDED]
    from_sibling = sibling_swap_halves(by_shard, "grads_sibling_swap")
    partial = [chip_partial_sums(g, r1, c_arr, "grads_chip_sum") for g, r1 in zip(by_shard, from_sibling)]
    from_chips = chips_exchange(partial, "grads_chips_exchange")
    totals = [shard_total(g, r1, r2, cs_arr, "grads_shard_total") for g, r1, r2 in zip(by_shard, from_sibling, from_chips)]
    totals_sibling = sibling_swap(totals, "grads_sibling_totals")

    out = {}
    for (n, _), g_mine, g_theirs in zip(SHARDED, totals, totals_sibling):
        shape = given[n].shape
        two_d = lambda a: a.reshape(-1, shape[-1])
        res = adamw_halves(two_d(given[n]), two_d(g_mine), two_d(g_theirs), two_d(given["m_" + n]), two_d(given["v_" + n]),
                           c_arr, "adamw")
        out[n] = tuple(r.reshape(shape) for r in res)

    small = ("mem_norm_g", "mix_norm_g", "ffn_norm_g", "final_norm_g", "a_sinks", "b_w_s", "b_bias_s", "b_ln_g", "b_ln_b")
    small_g = [d_mem_g[0], jnp.concatenate(d_mix_g, axis=0), jnp.concatenate(d_ffn_g, axis=0), d_final_g[0],
               jnp.stack(d_sinks), jnp.stack(d_ws), jnp.stack(d_bias), jnp.stack(d_lg), jnp.stack(d_lb)]
    packed = _pack(small_g)
    g_small = sum_devices(allgather_small(packed, "small_allgather").reshape(8, *packed.shape), "small_sum")
    like = [given[n] for n in small]
    delta_s, new_m_s, new_v_s = adamw(_pack(like), g_small, _pack([given["m_" + n] for n in small]),
                                      _pack([given["v_" + n] for n in small]), "adamw_small")
    for n, g, dl, nm_, nv_ in zip(small, _unpack(g_small, like), _unpack(delta_s, like), _unpack(new_m_s, like), _unpack(new_v_s, like)):
        out[n] = (g, dl, nm_, nv_)

    order = ("mem_norm_g", "mix_norm_g", "ffn_norm_g", "final_norm_g", "a_w_in", "a_sinks", "a_w_out", "b_w_in", "b_w_s",
             "b_bias_s", "b_ln_g", "b_ln_b", "b_w_out", "w_mem_kv", "w_gate_up", "w_down")
    return (loss, grad_x, *[out[n][0] for n in order], *[out[n][1] for n in order],
            *[out[n][2] for n in order], *[out[n][3] for n in order])
```

```python
import jax
import jax.numpy as jnp
from jax import lax
from jax.experimental import pallas as pl
from jax.experimental.pallas import tpu as pltpu

F32, BF16 = jnp.float32, jnp.bfloat16
EPS = 1e-6
HEAD_DIM = 64
Q_HEADS, KV_HEADS, GROUP = 12, 2, 6
WINDOW = 128
MEM_HEADS = 4
B_GROUPS = 6
Q_W, KV_W, MEM_W, B_W = 768, 128, 256, 768
SCALE = HEAD_DIM ** -0.5
NEG = -1e30
ADAM_LR, ADAM_B1, ADAM_B2, ADAM_EPS, ADAM_WD, ADAM_STEP = 0.001, 0.9, 0.999, 1e-08, 0.01, 10
V7X_VMEM_LIMIT_BYTES = 48 * 1024 * 1024
MESH = pl.DeviceIdType.MESH
HBM_SPEC = pl.BlockSpec(memory_space=pltpu.HBM)
VMEM_SPEC = pl.BlockSpec(memory_space=pltpu.VMEM)


def _cp(*sem):
    return pltpu.CompilerParams(dimension_semantics=sem or None, vmem_limit_bytes=V7X_VMEM_LIMIT_BYTES)


def _tile(n, cands):
    for t in cands:
        if n % t == 0:
            return t
    return n


def _sds(shape, dtype):
    return jax.ShapeDtypeStruct(tuple(shape), dtype)


def _dot(a, b, ca, cb):
    return lax.dot_general(a, b, (((ca,), (cb,)), ((), ())), preferred_element_type=F32)


def _rms(x, g):
    return x * lax.rsqrt(jnp.mean(x * x, axis=-1, keepdims=True) + EPS) * g


def rmsnorm_fwd(h, g, name):
    t, d = h.shape
    tm = _tile(t, (512, 256, 128))

    def body(h_ref, g_ref, o_ref):
        o_ref[...] = _rms(h_ref[...], g_ref[...]).astype(o_ref.dtype)

    return pl.pallas_call(
        body, name=name, grid=(t // tm,),
        in_specs=[pl.BlockSpec((tm, d), lambda i: (i, 0)), pl.BlockSpec((1, d), lambda i: (0, 0))],
        out_specs=pl.BlockSpec((tm, d), lambda i: (i, 0)),
        out_shape=_sds((t, d), BF16), compiler_params=_cp("parallel"))(h, g)


def rmsnorm_bwd(h, g, dxn, dres, name):
    t, d = h.shape
    tm = _tile(t, (512, 256, 128))

    def body(h_ref, g_ref, dxn_ref, dres_ref, dh_ref, dg_ref):
        _, vjp = jax.vjp(_rms, h_ref[...], g_ref[...])
        dh, dg = vjp(dxn_ref[...].astype(F32))
        dh_ref[...] = dres_ref[...] + dh

        @pl.when(pl.program_id(0) == 0)
        def _():
            dg_ref[...] = jnp.zeros_like(dg_ref)

        dg_ref[...] += dg

    row = pl.BlockSpec((tm, d), lambda i: (i, 0))
    vec = pl.BlockSpec((1, d), lambda i: (0, 0))
    return pl.pallas_call(
        body, name=name, grid=(t // tm,), in_specs=[row, vec, row, row], out_specs=[row, vec],
        out_shape=[_sds((t, d), F32), _sds((1, d), F32)], compiler_params=_cp("arbitrary"))(h, g, dxn, dres)


def loss_head(h, g, tgt, name):
    t, d = h.shape
    tm = _tile(t, (512, 256, 128))

    def body(h_ref, g_ref, t_ref, l_ref, dh_ref, dg_ref):
        y, vjp = jax.vjp(_rms, h_ref[...], g_ref[...])
        err = y - t_ref[...]
        dh, dg = vjp(err * (1.0 / d))
        dh_ref[...] = dh
        part = 0.5 * jnp.sum(jnp.mean(err * err, axis=-1, keepdims=True), axis=0, keepdims=True)

        @pl.when(pl.program_id(0) == 0)
        def _():
            dg_ref[...] = jnp.zeros_like(dg_ref)
            l_ref[...] = jnp.zeros_like(l_ref)

        dg_ref[...] += dg
        l_ref[...] += part

    row = pl.BlockSpec((tm, d), lambda i: (i, 0))
    vec = pl.BlockSpec((1, d), lambda i: (0, 0))
    one = pl.BlockSpec((1, 1), lambda i: (0, 0))
    return pl.pallas_call(
        body, name=name, grid=(t // tm,), in_specs=[row, vec, row], out_specs=[one, row, vec],
        out_shape=[_sds((1, 1), F32), _sds((t, d), F32), _sds((1, d), F32)], compiler_params=_cp("arbitrary"))(h, g, tgt)


def _logical(op):
    arr, lead = op if isinstance(op, tuple) else (op, None)
    planes = arr.shape[-3] if arr.ndim - (lead is not None) == 3 else 1
    return arr, lead, arr.shape[-2], arr.shape[-1], planes


def _spec(op, rows_t, cols_t, row_of, col_of):
    arr, lead, _, cols, _ = _logical(op)
    per = cols // cols_t
    lead = () if lead is None else (lead,)
    if arr.ndim - len(lead) == 2:
        return pl.BlockSpec((None,) * len(lead) + (rows_t, cols_t), lambda *g: lead + (row_of(*g), col_of(*g)))
    return pl.BlockSpec((None,) * len(lead) + (None, rows_t, cols_t),
                        lambda *g: lead + (col_of(*g) // per, row_of(*g), col_of(*g) % per))


def _arr(op):
    return op[0] if isinstance(op, tuple) else op


def norm_matmul(h, g, w, out_dtype, name):
    t, d = h.shape
    wa, layer = w
    n = wa.shape[-1]
    tm = _tile(t, (512, 256, 128))

    def body(h_ref, g_ref, w_ref, xn_ref, o_ref):
        xn = _rms(h_ref[...], g_ref[...]).astype(BF16)
        xn_ref[...] = xn
        o_ref[...] = _dot(xn, w_ref[...], 1, 0).astype(o_ref.dtype)

    return pl.pallas_call(
        body, name=name, grid=(t // tm,),
        in_specs=[pl.BlockSpec((tm, d), lambda i: (i, 0)), pl.BlockSpec((1, d), lambda i: (0, 0)),
                  pl.BlockSpec((None, d, n), lambda i: (layer, 0, 0))],
        out_specs=[pl.BlockSpec((tm, d), lambda i: (i, 0)), pl.BlockSpec((tm, n), lambda i: (i, 0))],
        out_shape=[_sds((t, d), BF16), _sds((t, n), out_dtype)], compiler_params=_cp("parallel"))(h, g, wa)


def dx_norm_bwd(dy, w, h, g, dres, name):
    t, d = h.shape
    _, _, _, kc, kp = _logical(dy)
    _, _, _, wc, wp = _logical(w)
    tk = _tile(min(kc, wc), (1792, 1408, 1280, 1024, 512, 256, 128))
    nk = kc * kp // tk
    assert kc * kp == wc * wp, name
    tm = _tile(t, (512, 256, 128))

    def body(dy_ref, w_ref, h_ref, g_ref, dres_ref, dh_ref, dg_ref, *acc):
        i, kk = pl.program_id(0), pl.program_id(1)
        p = _dot(dy_ref[...].astype(BF16), w_ref[...], 1, 1)
        if nk > 1:
            @pl.when(kk == 0)
            def _():
                acc[0][...] = p

            @pl.when(kk > 0)
            def _():
                acc[0][...] += p

        @pl.when(kk == nk - 1)
        def _():
            dxn = acc[0][...] if nk > 1 else p
            _, vjp = jax.vjp(_rms, h_ref[...], g_ref[...])
            dh, dg = vjp(dxn)
            dh_ref[...] = dres_ref[...] + dh

            @pl.when(i == 0)
            def _():
                dg_ref[...] = jnp.zeros_like(dg_ref)

            dg_ref[...] += dg

    row = pl.BlockSpec((tm, d), lambda i, kk: (i, 0))
    vec = pl.BlockSpec((1, d), lambda i, kk: (0, 0))
    return pl.pallas_call(
        body, name=name, grid=(t // tm, nk),
        in_specs=[_spec(dy, tm, tk, lambda i, kk: i, lambda i, kk: kk), _spec(w, d, tk, lambda i, kk: 0, lambda i, kk: kk), row, vec, row],
        out_specs=[row, vec], out_shape=[_sds((t, d), F32), _sds((1, d), F32)],
        scratch_shapes=[pltpu.VMEM((tm, d), F32)] if nk > 1 else [],
        compiler_params=_cp("arbitrary", "arbitrary"))(_arr(dy), _arr(w), h, g, dres)


def matmul(a, b, mode, out_dtype, name, res=None, tm=None, tn=1792, tk=2816, out_planes=None, out_into=None):
    _, _, ar, ac, ap = _logical(a)
    _, _, br, bc, bp = _logical(b)
    if mode == "nn":
        m, ka, kb, n = ar, ac * ap, br, bc * bp
        n_plane, ka_plane, kb_plane = bc, ac, br
    elif mode == "nt":
        m, ka, n, kb = ar, ac * ap, br, bc * bp
        n_plane, ka_plane, kb_plane = br, ac, bc
    else:
        ka, m, kb, n = ar, ac * ap, br, bc * bp
        n_plane, ka_plane, kb_plane = bc, ar, br
    m_plane = ac if mode == "tn" else ar
    assert ka == kb, name
    k = ka
    kind, planes = out_planes or ("cols", 1)
    if kind == "cols":
        n_plane = min(n_plane, n // planes)
    tm = _tile(m_plane, ((1024, 1408, 512, 256, 128) if mode == "tn" else (512, 256, 128)) if tm is None else (tm, 1024, 512, 256, 128))
    if kind == "rows" and tm % (m // planes):
        tm = m_plane
    tn = _tile(n_plane, (tn, 1792, 1408, 1280, 1024, 896, 640, 512, 256, 128))
    tk = _tile(min(ka_plane, kb_plane), (tk, 2816, 1792, 1408, 1280, 1024, 512, 256, 128))
    nk = k // tk
    row_i, col_j, red = (lambda i, j, kk: i), (lambda i, j, kk: j), (lambda i, j, kk: kk)
    if mode == "nn":
        a_spec, b_spec, ca, cb = _spec(a, tm, tk, row_i, red), _spec(b, tk, tn, red, col_j), 1, 0
    elif mode == "nt":
        a_spec, b_spec, ca, cb = _spec(a, tm, tk, row_i, red), _spec(b, tn, tk, col_j, red), 1, 1
    else:
        a_spec, b_spec, ca, cb = _spec(a, tk, tm, red, row_i), _spec(b, tk, tn, red, col_j), 0, 0
    lead = () if out_into is None else (out_into[1],)
    if planes == 1:
        o_shape, o_block = (m, n), (tm, tn)
        o_index = lambda i, j, kk: lead + (i, j)
    elif kind == "cols":
        per = n // planes // tn
        o_shape, o_block = (planes, m, n // planes), (None, tm, tn)
        o_index = lambda i, j, kk: lead + (j // per, i, j % per)
    else:
        o_shape, o_block = (planes, m // planes, n), (tm // (m // planes), m // planes, tn)
        o_index = lambda i, j, kk: lead + (i, 0, j)
    o_spec = pl.BlockSpec((None,) * len(lead) + o_block, o_index)
    if out_into is not None:
        assert out_into[0].shape[1:] == o_shape and out_into[0].dtype == out_dtype, name
        o_shape = out_into[0].shape
    has_res = res is not None
    n_in = 2 + has_res + (out_into is not None)

    def body(*refs):
        a_ref, b_ref = refs[:2]
        rest = refs[2:2 + has_res] + refs[n_in:]
        o_ref = rest[1] if has_res else rest[0]
        p = _dot(a_ref[...].astype(BF16), b_ref[...].astype(BF16), ca, cb)
        if nk == 1:
            if has_res:
                p = p + rest[0][...]
            o_ref[...] = p.astype(o_ref.dtype).reshape(o_ref.shape)
        else:
            acc_ref = rest[-1]
            kk = pl.program_id(2)

            @pl.when(kk == 0)
            def _():
                acc_ref[...] = p

            @pl.when(kk > 0)
            def _():
                acc_ref[...] += p

            @pl.when(kk == nk - 1)
            def _():
                r = acc_ref[...]
                if has_res:
                    r = r + rest[0][...]
                o_ref[...] = r.astype(o_ref.dtype).reshape(o_ref.shape)

    operands = [_arr(a), _arr(b)] + ([res] if has_res else []) + ([out_into[0]] if out_into is not None else [])
    return pl.pallas_call(
        body, name=name, grid=(m // tm, n // tn, nk),
        in_specs=[a_spec, b_spec] + ([pl.BlockSpec((tm, tn), lambda i, j, kk: (i, j))] if has_res else [])
        + ([pl.BlockSpec(memory_space=pl.ANY)] if out_into is not None else []),
        out_specs=o_spec, out_shape=_sds(o_shape, out_dtype),
        input_output_aliases={n_in - 1: 0} if out_into is not None else {},
        scratch_shapes=[pltpu.VMEM((tm, tn), F32)] if nk > 1 else [],
        compiler_params=_cp("parallel", "parallel", "arbitrary"))(*operands)


def _swiglu(gate, up):
    return gate / (1.0 + jnp.exp(-gate)) * up


def gate_up_fwd(h, g, w, layer, name):
    t, d = h.shape
    half = w.shape[-1]
    tm = _tile(t, (512, 256, 128))

    def body(h_ref, g_ref, wg_ref, wu_ref, hn_ref, gu_ref, act_ref):
        a = _rms(h_ref[...], g_ref[...]).astype(BF16)
        hn_ref[...] = a
        gate, up = _dot(a, wg_ref[...], 1, 0), _dot(a, wu_ref[...], 1, 0)
        gu_ref[0] = gate.astype(gu_ref.dtype)
        gu_ref[1] = up.astype(gu_ref.dtype)
        act_ref[...] = _swiglu(gate, up).astype(act_ref.dtype)

    return pl.pallas_call(
        body, name=name, grid=(2, t // tm),
        in_specs=[pl.BlockSpec((tm, d), lambda j, i: (i, 0)), pl.BlockSpec((1, d), lambda j, i: (0, 0)),
                  pl.BlockSpec((None, None, d, half), lambda j, i: (layer, j, 0, 0)),
                  pl.BlockSpec((None, None, d, half), lambda j, i: (layer, 2 + j, 0, 0))],
        out_specs=[pl.BlockSpec((None, tm, d), lambda j, i: (j, i, 0)), pl.BlockSpec((2, tm, half), lambda j, i: (0, i, j)),
                   pl.BlockSpec((tm, half), lambda j, i: (i, j))],
        out_shape=[_sds((2, t, d), BF16), _sds((2, t, 2 * half), BF16), _sds((t, 2 * half), BF16)],
        compiler_params=_cp("parallel", "parallel"))(h, g, w, w)


def down_dx_swiglu_bwd(dh, wd, gu, name):
    t, d = dh.shape
    w, layer = wd
    f = w.shape[-2]
    tm = _tile(t, (512, 256, 128))
    tn = _tile(f, (1408, 512, 256, 128))

    def body(dh_ref, w_ref, gu_ref, o_ref):
        dact = _dot(dh_ref[...].astype(BF16), w_ref[...], 1, 1)
        _, vjp = jax.vjp(_swiglu, gu_ref[0].astype(F32), gu_ref[1].astype(F32))
        dg, du = vjp(dact)
        o_ref[0] = dg.astype(o_ref.dtype)
        o_ref[1] = du.astype(o_ref.dtype)

    planes = pl.BlockSpec((2, tm, tn), lambda j, i: (0, i, j))
    return pl.pallas_call(
        body, name=name, grid=(f // tn, t // tm),
        in_specs=[pl.BlockSpec((tm, d), lambda j, i: (i, 0)), pl.BlockSpec((None, tn, d), lambda j, i: (layer, j, 0)), planes],
        out_specs=planes, out_shape=_sds((2, t, f), BF16), compiler_params=_cp("parallel", "parallel"))(dh, w, gu)


def _softmax(s, sink=None):
    m = s.max(axis=-1, keepdims=True)
    if sink is not None:
        m = jnp.maximum(m, sink)
    m = lax.stop_gradient(m)
    e = jnp.exp(s - m)
    den = e.sum(axis=-1, keepdims=True)
    if sink is not None:
        den = den + jnp.exp(sink - m)
    return e * (1.0 / den)


def _low_lanes():
    return lax.broadcasted_iota(jnp.int32, (1, 128), 1) < HEAD_DIM


def _stack_heads(slabs):
    low = _low_lanes()
    return jnp.concatenate([p for s in slabs for p in (jnp.where(low, s, 0.0), jnp.where(low, 0.0, s))], axis=0)


def _unstack_heads(o, n_slabs):
    low = _low_lanes()
    return [jnp.where(low, o[2 * j * WINDOW:(2 * j + 1) * WINDOW], o[(2 * j + 1) * WINDOW:(2 * j + 2) * WINDOW])
            for j in range(n_slabs)]


def _swa_group(q_slabs, k_both, v_both, sinks, mask):
    qs = _stack_heads(q_slabs).astype(BF16)
    s = jnp.where(mask, _dot(qs, k_both.astype(BF16), 1, 1) * SCALE, NEG)
    sink = jnp.concatenate([jnp.broadcast_to(v, (WINDOW, 1)) for v in sinks], axis=0)
    return _unstack_heads(_dot(_softmax(s, sink).astype(BF16), v_both.astype(BF16), 1, 0), len(q_slabs))


def _mem_pair(q_slab, k_slab, v_slab):
    s = _dot(_stack_heads([q_slab]).astype(BF16), k_slab.astype(BF16), 1, 1) * SCALE
    return _unstack_heads(_dot(_softmax(s).astype(BF16), v_slab.astype(BF16), 1, 0), 1)[0]


def _gelu(x):
    return 0.5 * x * (1.0 + jnp.tanh(0.7978845608028654 * (x + 0.044715 * (x * x * x))))


def _gmlp_group(zu, zv, w, bcol, lg, lb, tri):
    u, v = _gelu(zu), _gelu(zv)
    mu = jnp.mean(v, axis=-1, keepdims=True)
    var = jnp.mean(jnp.square(v - mu), axis=-1, keepdims=True)
    vn = (v - mu) * lax.rsqrt(var + EPS) * lg + lb
    sv = _dot(jnp.where(tri, w, 0.0).astype(BF16), vn.astype(BF16), 1, 0) + bcol
    return u * sv


def _cols(x, width):
    return [x[:, j * width:(j + 1) * width] for j in range(x.shape[1] // width)]


def _swa_mask(has_prev):
    qi = lax.broadcasted_iota(jnp.int32, (GROUP * WINDOW, 2 * WINDOW), 0) & (WINDOW - 1)
    kj = lax.broadcasted_iota(jnp.int32, (GROUP * WINDOW, 2 * WINDOW), 1)
    in_prev = jnp.logical_and(jnp.logical_and(kj < WINDOW, kj > qi), has_prev)
    return jnp.logical_or(in_prev, jnp.logical_and(kj >= WINDOW, kj - WINDOW <= qi))


def _mix_a(q_slabs, k_boths, v_boths, sinks, qm_slabs, km_slabs, vm_slabs, mask):
    per = GROUP // 2
    outs = []
    for g in range(KV_HEADS):
        outs += _swa_group(q_slabs[per * g:per * (g + 1)], k_boths[g], v_boths[g], sinks[GROUP * g:GROUP * (g + 1)], mask)
    return outs + [_mem_pair(qm_slabs[j], km_slabs[j], vm_slabs[j]) for j in range(MEM_HEADS // 2)]


def _in_both_halves(prev, cur):
    cat = jnp.concatenate([prev, cur], axis=0)
    rolled = pltpu.roll(cat, HEAD_DIM, axis=1)
    low = _low_lanes()
    return [jnp.where(low, cat, rolled), jnp.where(low, rolled, cat)]


def _from_both_halves(d_boths):
    t = [d + pltpu.roll(d, HEAD_DIM, axis=1) for d in d_boths]
    return jnp.where(_low_lanes(), t[0], t[1])


def _mix_a_specs(nm, blk):
    prev = lambda n: jnp.maximum(blk(n) - 1, 0)
    return [pl.BlockSpec((WINDOW, Q_W), lambda n: (blk(n), 0)),
            pl.BlockSpec((WINDOW, KV_W), lambda n: (prev(n), Q_W // KV_W)),
            pl.BlockSpec((WINDOW, KV_W), lambda n: (blk(n), Q_W // KV_W)),
            pl.BlockSpec((WINDOW, KV_W), lambda n: (prev(n), Q_W // KV_W + 1)),
            pl.BlockSpec((WINDOW, KV_W), lambda n: (blk(n), Q_W // KV_W + 1)),
            pl.BlockSpec((WINDOW, MEM_W), lambda n: (blk(n), (Q_W + 2 * KV_W) // MEM_W)),
            pl.BlockSpec((16, 128), lambda n: (0, 0)),
            pl.BlockSpec((nm, MEM_W), lambda n: (0, 0)),
            pl.BlockSpec((nm, MEM_W), lambda n: (0, 1))]


def _mix_a_args(refs):
    q, kp, kc, vp, vc, qm, sk, km, vm = [r[...].astype(F32) for r in refs]
    return (_cols(q, 128), _in_both_halves(kp, kc), _in_both_halves(vp, vc), [sk[h:h + 1, 0:1] for h in range(Q_HEADS)],
            _cols(qm, 128), _cols(km, 128), _cols(vm, 128))


def mixer_a_fwd(proj, sk, kv, name):
    t, nm = proj.shape[0], kv.shape[0]

    def body(*refs):
        o_ref = refs[-1]
        slabs = _mix_a(*_mix_a_args(refs[:-1]), _swa_mask(pl.program_id(0) > 0))
        o_ref[...] = jnp.concatenate(slabs, axis=1).astype(o_ref.dtype)

    return pl.pallas_call(
        body, name=name, grid=(t // WINDOW,), in_specs=_mix_a_specs(nm, lambda n: n),
        out_specs=pl.BlockSpec((WINDOW, Q_W + MEM_W), lambda n: (n, 0)),
        out_shape=_sds((t, Q_W + MEM_W), BF16), compiler_params=_cp("parallel"))(proj, proj, proj, proj, proj, proj, sk, kv, kv)


def _onehot_rows(vals, shape):
    rows = lax.broadcasted_iota(jnp.int32, shape, 0)
    out = jnp.zeros(shape, F32)
    for h, v in enumerate(vals):
        out = out + jnp.where(rows == h, jnp.broadcast_to(v, shape), 0.0)
    return out


def mixer_a_bwd(proj, dcat, sk, kv, name):
    t, nm = proj.shape[0], kv.shape[0]
    nb = t // WINDOW
    blk = lambda i: nb - 1 - i

    def body(*refs):
        dcat_ref, dproj_ref, dsk_ref, dkv_ref, carry_ref = refs[9:]
        i = pl.program_id(0)

        @pl.when(i == 0)
        def _():
            carry_ref[...] = jnp.zeros_like(carry_ref)
            dsk_ref[...] = jnp.zeros_like(dsk_ref)
            dkv_ref[...] = jnp.zeros_like(dkv_ref)

        mask = _swa_mask(blk(i) > 0)
        _, vjp = jax.vjp(lambda *a: _mix_a(*a, mask), *_mix_a_args(refs[:9]))
        dqs, dk_boths, dv_boths, dsinks, dqms, dkms, dvms = vjp(_cols(dcat_ref[...].astype(F32), 128))
        dkv = jnp.concatenate([_from_both_halves(dk_boths), _from_both_halves(dv_boths)], axis=1)
        dkv_cur = dkv[WINDOW:] + carry_ref[...]
        carry_ref[...] = dkv[:WINDOW]
        dproj_ref[...] = jnp.concatenate(dqs + [dkv_cur] + dqms, axis=1).astype(dproj_ref.dtype)
        dsk_ref[...] += _onehot_rows(dsinks, (16, 128))
        dkv_ref[...] += jnp.concatenate(dkms + dvms, axis=1)

    width = Q_W + 2 * KV_W + MEM_W
    return pl.pallas_call(
        body, name=name, grid=(nb,),
        in_specs=_mix_a_specs(nm, blk) + [pl.BlockSpec((WINDOW, Q_W + MEM_W), lambda i: (blk(i), 0))],
        out_specs=[pl.BlockSpec((WINDOW, width), lambda i: (blk(i), 0)), pl.BlockSpec((16, 128), lambda i: (0, 0)),
                   pl.BlockSpec((nm, 2 * MEM_W), lambda i: (0, 0))],
        out_shape=[_sds((t, width), BF16), _sds((16, 128), F32), _sds((nm, 2 * MEM_W), F32)],
        scratch_shapes=[pltpu.VMEM((WINDOW, 2 * KV_W), F32)],
        compiler_params=_cp("arbitrary"))(proj, proj, proj, proj, proj, proj, sk, kv, kv, dcat)


def _mix_b(zus, zvs, ws, bcols, lgs, lbs, qms, kms, vms, tri):
    outs = [_gmlp_group(zus[g], zvs[g], ws[g], bcols[g], lgs[g], lbs[g], tri) for g in range(B_GROUPS)]
    return outs + [_mem_pair(qms[j], kms[j], vms[j]) for j in range(MEM_HEADS // 2)]


def _mix_b_specs(nm):
    return [pl.BlockSpec((WINDOW, 2 * B_W), lambda n: (n, 0)),
            pl.BlockSpec((WINDOW, MEM_W), lambda n: (n, 2 * B_W // MEM_W)),
            pl.BlockSpec((B_GROUPS, WINDOW, WINDOW), lambda n: (0, 0, 0)),
            pl.BlockSpec((WINDOW, 128), lambda n: (0, 0)),
            pl.BlockSpec((8, 128), lambda n: (0, 0)),
            pl.BlockSpec((8, 128), lambda n: (0, 0)),
            pl.BlockSpec((nm, MEM_W), lambda n: (0, 0)),
            pl.BlockSpec((nm, MEM_W), lambda n: (0, 1))]


def _mix_b_args(refs):
    z, qm, ws, bt, lg, lb, km, vm = [r[...].astype(F32) for r in refs]
    zs = _cols(z, 128)
    return (zs[:B_GROUPS], zs[B_GROUPS:], [ws[g] for g in range(B_GROUPS)], [bt[:, g:g + 1] for g in range(B_GROUPS)],
            [lg[g:g + 1, :] for g in range(B_GROUPS)], [lb[g:g + 1, :] for g in range(B_GROUPS)],
            _cols(qm, 128), _cols(km, 128), _cols(vm, 128))


def _tri():
    return lax.broadcasted_iota(jnp.int32, (WINDOW, WINDOW), 0) >= lax.broadcasted_iota(jnp.int32, (WINDOW, WINDOW), 1)


def mixer_b_fwd(proj, ws, bt, lg, lb, kv, name):
    t, nm = proj.shape[0], kv.shape[0]

    def body(*refs):
        o_ref = refs[-1]
        o_ref[...] = jnp.concatenate(_mix_b(*_mix_b_args(refs[:-1]), _tri()), axis=1).astype(o_ref.dtype)

    return pl.pallas_call(
        body, name=name, grid=(t // WINDOW,), in_specs=_mix_b_specs(nm),
        out_specs=pl.BlockSpec((WINDOW, B_W + MEM_W), lambda n: (n, 0)),
        out_shape=_sds((t, B_W + MEM_W), BF16), compiler_params=_cp("parallel"))(proj, proj, ws, bt, lg, lb, kv, kv)


def mixer_b_bwd(proj, dcat, ws, bt, lg, lb, kv, name):
    t, nm = proj.shape[0], kv.shape[0]

    def body(*refs):
        dcat_ref, dproj_ref, dws_ref, dbt_ref, dlg_ref, dlb_ref, dkv_ref = refs[8:]

        @pl.when(pl.program_id(0) == 0)
        def _():
            for r in (dws_ref, dbt_ref, dlg_ref, dlb_ref, dkv_ref):
                r[...] = jnp.zeros_like(r)

        tri = _tri()
        _, vjp = jax.vjp(lambda *a: _mix_b(*a, tri), *_mix_b_args(refs[:8]))
        dzus, dzvs, dws, dbcols, dlgs, dlbs, dqms, dkms, dvms = vjp(_cols(dcat_ref[...].astype(F32), 128))
        dproj_ref[...] = jnp.concatenate(dzus + dzvs + dqms, axis=1).astype(dproj_ref.dtype)
        for g in range(B_GROUPS):
            dws_ref[g] += dws[g]
        lanes = lax.broadcasted_iota(jnp.int32, (WINDOW, 128), 1)
        dbt = jnp.zeros((WINDOW, 128), F32)
        for g in range(B_GROUPS):
            dbt = dbt + jnp.where(lanes == g, jnp.broadcast_to(dbcols[g], (WINDOW, 128)), 0.0)
        dbt_ref[...] += dbt
        dlg_ref[...] += _onehot_rows(dlgs, (8, 128))
        dlb_ref[...] += _onehot_rows(dlbs, (8, 128))
        dkv_ref[...] += jnp.concatenate(dkms + dvms, axis=1)

    width = 2 * B_W + MEM_W
    const2 = lambda n: (0, 0)
    return pl.pallas_call(
        body, name=name, grid=(t // WINDOW,),
        in_specs=_mix_b_specs(nm) + [pl.BlockSpec((WINDOW, B_W + MEM_W), lambda n: (n, 0))],
        out_specs=[pl.BlockSpec((WINDOW, width), lambda n: (n, 0)),
                   pl.BlockSpec((B_GROUPS, WINDOW, WINDOW), lambda n: (0, 0, 0)),
                   pl.BlockSpec((WINDOW, 128), const2), pl.BlockSpec((8, 128), const2), pl.BlockSpec((8, 128), const2),
                   pl.BlockSpec((nm, 2 * MEM_W), const2)],
        out_shape=[_sds((t, width), BF16), _sds((B_GROUPS, WINDOW, WINDOW), F32), _sds((WINDOW, 128), F32),
                   _sds((8, 128), F32), _sds((8, 128), F32), _sds((nm, 2 * MEM_W), F32)],
        compiler_params=_cp("arbitrary"))(proj, proj, ws, bt, lg, lb, kv, kv, dcat)


def _adamw_update(w, g, m, v):
    m2 = ADAM_B1 * m + (1.0 - ADAM_B1) * g
    v2 = ADAM_B2 * v + (1.0 - ADAM_B2) * jnp.square(g)
    m_hat = m2 / (1.0 - ADAM_B1 ** ADAM_STEP)
    v_hat = v2 / (1.0 - ADAM_B2 ** ADAM_STEP)
    return -ADAM_LR * (m_hat / (jnp.sqrt(v_hat) + ADAM_EPS) + ADAM_WD * w), m2, v2


def adamw(w, g, m, v, name):
    r, c = w.shape
    tr = _tile(r, (512, 352, 256, 128, 64, 32, 16, 8))

    def body(w_ref, g_ref, m_ref, v_ref, d_ref, nm_ref, nv_ref):
        d_ref[...], nm_ref[...], nv_ref[...] = _adamw_update(w_ref[...], g_ref[...], m_ref[...], v_ref[...])

    spec = pl.BlockSpec((tr, c), lambda i: (i, 0))
    return pl.pallas_call(
        body, name=name, grid=(r // tr,), in_specs=[spec] * 4, out_specs=[spec] * 3,
        out_shape=[_sds((r, c), F32)] * 3, compiler_params=_cp("parallel"))(w, g, m, v)


def adamw_halves(w, g_mine, g_theirs, m, v, c_arr, name):
    r, c = w.shape
    tr = _tile(r // 2, (256, 352, 128, 64, 32, 16, 8))
    per_half = r // 2 // tr

    def body(c_ref, w_ref, gm_ref, gt_ref, m_ref, v_ref, g_ref, d_ref, nm_ref, nv_ref):
        g = jnp.where(pl.program_id(0) // per_half == c_ref[0], gm_ref[...], gt_ref[...])
        g_ref[...] = g
        d_ref[...], nm_ref[...], nv_ref[...] = _adamw_update(w_ref[...], g, m_ref[...], v_ref[...])

    spec = pl.BlockSpec((tr, c), lambda i, cr: (i, 0))
    half = pl.BlockSpec((tr, c), lambda i, cr: (i % per_half, 0))
    return pl.pallas_call(
        body, name=name,
        grid_spec=pltpu.PrefetchScalarGridSpec(num_scalar_prefetch=1, grid=(r // tr,), in_specs=[spec, half, half, spec, spec],
                                               out_specs=[spec] * 4),
        out_shape=[_sds((r, c), F32)] * 4, compiler_params=_cp("parallel"))(c_arr, w, g_mine, g_theirs, m, v)


def _place():
    return lax.axis_index("x"), lax.axis_index("y"), lax.axis_index("c")


def _other_chips(x, y):
    return [(1 - x, y), (x, 1 - y), (1 - x, 1 - y)]


def _remote(src, dst, send_sems, recv_sems, k, dev):
    return pltpu.make_async_remote_copy(src_ref=src, dst_ref=dst, send_sem=send_sems.at[k], recv_sem=recv_sems.at[k],
                                        device_id=dev, device_id_type=MESH)


def allgather_weights(shards, name):
    nw = len(shards)

    def body(*refs):
        ins, outs = refs[:nw], refs[nw:2 * nw]
        send_sems, recv_sems = refs[2 * nw:]
        x, y, c = _place()
        chips = _other_chips(x, y)
        halves = [(pl.ds(c * (r.shape[0] // 2), r.shape[0] // 2), pl.ds((1 - c) * (r.shape[0] // 2), r.shape[0] // 2)) for r in ins]
        first = [_remote(ins[w].at[halves[w][0]], outs[w].at[halves[w][0], 2 * x + y], send_sems, recv_sems, 6 * w + j, (*chip, c))
                 for j, chip in enumerate(chips) for w in range(nw)]
        for cp in first:
            cp.start()
        passed = []
        for j, chip in enumerate(chips):
            for w in range(nw):
                blk = outs[w].at[halves[w][0], 2 * chip[0] + chip[1]]
                _remote(blk, blk, send_sems, recv_sems, 6 * w + j, (x, y, c)).wait_recv()
                cp = _remote(blk, blk, send_sems, recv_sems, 6 * w + 3 + j, (x, y, 1 - c))
                cp.start()
                passed.append(cp)
        for j, chip in enumerate(chips):
            for w in range(nw):
                blk = outs[w].at[halves[w][1], 2 * chip[0] + chip[1]]
                _remote(blk, blk, send_sems, recv_sems, 6 * w + 3 + j, (x, y, c)).wait_recv()
        for cp in first + passed:
            cp.wait_send()

    return pl.pallas_call(
        body, name=name, in_specs=[HBM_SPEC] * nw, out_specs=[HBM_SPEC] * nw,
        out_shape=[_sds((s.shape[0], 4) + s.shape[1:], s.dtype) for s in shards],
        scratch_shapes=[pltpu.SemaphoreType.DMA((6 * nw,)), pltpu.SemaphoreType.DMA((6 * nw,))],
        compiler_params=pltpu.CompilerParams(has_side_effects=True))(*shards)


def sibling_swap_halves(gs, name):
    nw = len(gs)

    def body(*refs):
        ins, outs = refs[:nw], refs[nw:2 * nw]
        send_sems, recv_sems = refs[2 * nw:]
        x, y, c = _place()
        copies = [_remote(ins[w].at[pl.ds((1 - c) * outs[w].shape[0], outs[w].shape[0])], outs[w], send_sems, recv_sems, w, (x, y, 1 - c))
                  for w in range(nw)]
        for cp in copies:
            cp.start()
        for cp in copies:
            cp.wait()

    return pl.pallas_call(
        body, name=name, in_specs=[HBM_SPEC] * nw, out_specs=[HBM_SPEC] * nw,
        out_shape=[_sds((g.shape[0] // 2,) + g.shape[1:], g.dtype) for g in gs],
        scratch_shapes=[pltpu.SemaphoreType.DMA((nw,)), pltpu.SemaphoreType.DMA((nw,))],
        compiler_params=pltpu.CompilerParams(has_side_effects=True))(*gs)


def chips_exchange(sbs, name):
    nw = len(sbs)

    def body(*refs):
        ins, outs = refs[:nw], refs[nw:2 * nw]
        send_sems, recv_sems = refs[2 * nw:]
        x, y, c = _place()
        copies = [_remote(ins[w].at[:, 2 * chip[0] + chip[1]], outs[w].at[j], send_sems, recv_sems, 3 * w + j, (*chip, c))
                  for j, chip in enumerate(_other_chips(x, y)) for w in range(nw)]
        for cp in copies:
            cp.start()
        for cp in copies:
            cp.wait()

    return pl.pallas_call(
        body, name=name, in_specs=[HBM_SPEC] * nw, out_specs=[HBM_SPEC] * nw,
        out_shape=[_sds((3, s.shape[0]) + s.shape[2:], s.dtype) for s in sbs],
        scratch_shapes=[pltpu.SemaphoreType.DMA((3 * nw,)), pltpu.SemaphoreType.DMA((3 * nw,))],
        compiler_params=pltpu.CompilerParams(has_side_effects=True))(*sbs)


def sibling_swap(fs, name):
    nw = len(fs)

    def body(*refs):
        ins, outs = refs[:nw], refs[nw:2 * nw]
        send_sems, recv_sems = refs[2 * nw:]
        x, y, c = _place()
        copies = [_remote(ins[w], outs[w], send_sems, recv_sems, w, (x, y, 1 - c)) for w in range(nw)]
        for cp in copies:
            cp.start()
        for cp in copies:
            cp.wait()

    return pl.pallas_call(
        body, name=name, in_specs=[HBM_SPEC] * nw, out_specs=[HBM_SPEC] * nw,
        out_shape=[_sds(f.shape, f.dtype) for f in fs],
        scratch_shapes=[pltpu.SemaphoreType.DMA((nw,)), pltpu.SemaphoreType.DMA((nw,))],
        compiler_params=pltpu.CompilerParams(has_side_effects=True))(*fs)


def _half_tile(a):
    return _tile(a, (256, 352, 128, 64, 32, 16))


def chip_partial_sums(g, r1, c_arr, name):
    hl, _, a, b = r1.shape
    ta = _half_tile(a)

    def body(c_ref, g_ref, r_ref, o_ref):
        o_ref[...] = (g_ref[...] + r_ref[...]).astype(o_ref.dtype)

    blk = (None, None, ta, b)
    return pl.pallas_call(
        body, name=name,
        grid_spec=pltpu.PrefetchScalarGridSpec(
            num_scalar_prefetch=1, grid=(hl, 4, a // ta),
            in_specs=[pl.BlockSpec(blk, lambda l, s, i, c: (c[0] * hl + l, s, i, 0)), pl.BlockSpec(blk, lambda l, s, i, c: (l, s, i, 0))],
            out_specs=pl.BlockSpec(blk, lambda l, s, i, c: (l, s, i, 0))),
        out_shape=_sds(r1.shape, BF16), compiler_params=_cp("parallel", "parallel", "parallel"))(c_arr, g, r1)


def shard_total(g, r1, r2, cs_arr, name):
    hl, _, a, b = r1.shape
    ta = _half_tile(a)

    def body(cs_ref, g_ref, r1_ref, p0_ref, p1_ref, p2_ref, o_ref):
        o_ref[...] = (((g_ref[...] + r1_ref[...]) + p0_ref[...].astype(F32)) + p1_ref[...].astype(F32)) + p2_ref[...].astype(F32)

    blk4, blk3 = (None, None, ta, b), (None, ta, b)
    peer = lambda k: pl.BlockSpec((None, None, ta, b), lambda l, i, cs: (k, l, i, 0))
    return pl.pallas_call(
        body, name=name,
        grid_spec=pltpu.PrefetchScalarGridSpec(
            num_scalar_prefetch=1, grid=(hl, a // ta),
            in_specs=[pl.BlockSpec(blk4, lambda l, i, cs: (cs[0] * hl + l, cs[1], i, 0)),
                      pl.BlockSpec(blk4, lambda l, i, cs: (l, cs[1], i, 0)), peer(0), peer(1), peer(2)],
            out_specs=pl.BlockSpec(blk3, lambda l, i, cs: (l, i, 0))),
        out_shape=_sds((hl, a, b), F32), compiler_params=_cp("parallel", "parallel"))(cs_arr, g, r1, r2, r2, r2)


def allgather_small(v, name):
    r, n = v.shape

    def body(x_ref, out_ref, send_sems, recv_sems, local_sem):
        x, y, c = _place()
        me, sibling = (x, y, c), (x, y, 1 - c)
        chips = _other_chips(x, y)

        def rows(px, py, pc):
            return out_ref.at[pl.ds((4 * px + 2 * py + pc) * r, r), :]

        def copy(k, block, to, src=None):
            return _remote(rows(*block) if src is None else src, rows(*block), send_sems, recv_sems, k, to)

        mine = pltpu.make_async_copy(x_ref, rows(*me), local_sem)
        mine.start()
        first = [copy(0, me, sibling, src=x_ref)] + [copy(1 + j, me, (*chip, c), src=x_ref) for j, chip in enumerate(chips)]
        for cp in first:
            cp.start()
        passed = [copy(4 + j, (*chip, c), sibling) for j, chip in enumerate(chips)]
        for j, chip in enumerate(chips):
            copy(1 + j, (*chip, c), me).wait_recv()
            passed[j].start()
        copy(0, sibling, me).wait_recv()
        for j, chip in enumerate(chips):
            copy(4 + j, (*chip, 1 - c), me).wait_recv()
        for cp in first + passed:
            cp.wait_send()
        mine.wait()

    return pl.pallas_call(
        body, name=name, in_specs=[VMEM_SPEC], out_specs=VMEM_SPEC, out_shape=_sds((8 * r, n), v.dtype),
        scratch_shapes=[pltpu.SemaphoreType.DMA((7,)), pltpu.SemaphoreType.DMA((7,)), pltpu.SemaphoreType.DMA],
        compiler_params=pltpu.CompilerParams(has_side_effects=True, vmem_limit_bytes=V7X_VMEM_LIMIT_BYTES))(v)


def sum_devices(v8, name):
    _, r, n = v8.shape
    tr = _tile(r, (88, 64, 32, 16, 8))

    def body(v_ref, o_ref):
        acc = v_ref[0]
        for d in range(1, 8):
            acc = acc + v_ref[d]
        o_ref[...] = acc

    return pl.pallas_call(
        body, name=name, grid=(r // tr,), in_specs=[pl.BlockSpec((8, tr, n), lambda i: (0, i, 0))],
        out_specs=pl.BlockSpec((tr, n), lambda i: (i, 0)), out_shape=_sds((r, n), F32), compiler_params=_cp("parallel"))(v8)


SHARDED = (("a_w_in", 2), ("a_w_out", 1), ("b_w_in", 2), ("b_w_out", 1), ("w_mem_kv", 1), ("w_gate_up", 2), ("w_down", 1))


def _full_from_gathered(wg, axis):
    l, _, a, b = wg.shape
    if axis == 1:
        return wg.reshape(l, 4 * a, b)
    return wg.transpose(0, 2, 1, 3).reshape(l, a, 4 * b)


def _by_shard(dw, axis):
    l, k, n = dw.shape
    if axis == 1:
        return dw.reshape(l, 4, k // 4, n)
    return dw.reshape(l, k, 4, n // 4).transpose(0, 2, 1, 3)


def _pack(arrs):
    parts = []
    for a in arrs:
        flat = a.reshape(-1)
        flat = jnp.pad(flat, (0, -flat.shape[0] % 1024))
        parts.append(flat.reshape(-1, 128))
    return jnp.concatenate(parts, axis=0)


def _unpack(buf, like):
    out, row = [], 0
    for a in like:
        size = 1
        for s in a.shape:
            size *= s
        rows = -(-size // 1024) * 8
        out.append(buf[row:row + rows].reshape(-1)[:size].reshape(a.shape))
        row += rows
    return out


def kernel(x, mem, mem_norm_g, mix_norm_g, ffn_norm_g, final_norm_g, a_w_in, a_sinks, a_w_out, b_w_in, b_w_s, b_bias_s, b_ln_g, b_ln_b, b_w_out, w_mem_kv, w_gate_up, w_down, loss_target, m_mem_norm_g, m_mix_norm_g, m_ffn_norm_g, m_final_norm_g, m_a_w_in, m_a_sinks, m_a_w_out, m_b_w_in, m_b_w_s, m_b_bias_s, m_b_ln_g, m_b_ln_b, m_b_w_out, m_w_mem_kv, m_w_gate_up, m_w_down, v_mem_norm_g, v_mix_norm_g, v_ffn_norm_g, v_final_norm_g, v_a_w_in, v_a_sinks, v_a_w_out, v_b_w_in, v_b_w_s, v_b_bias_s, v_b_ln_g, v_b_ln_b, v_b_w_out, v_w_mem_kv, v_w_gate_up, v_w_down):
    given = dict(locals())
    depth = mix_norm_g.shape[0]
    d = x.shape[-1]
    xi, yi, ci = _place()
    c_arr = jnp.stack([ci]).astype(jnp.int32)
    cs_arr = jnp.stack([ci, 2 * xi + yi]).astype(jnp.int32)

    own = [given[n].astype(BF16) for n, _ in SHARDED]
    gathered = allgather_weights(own, "allgather_weights")
    gathered = [lax.dynamic_update_slice(wg, w[:, None], (0, 2 * xi + yi, 0, 0)) for wg, w in zip(gathered, own)]
    gathered = dict(zip([n for n, _ in SHARDED], gathered))
    w_gate_up4 = gathered["w_gate_up"]
    full = {n: _full_from_gathered(gathered[n], ax) for n, ax in SHARDED if n != "w_gate_up"}
    weight = lambda n, l: (full[n], l)

    h = x.reshape(-1, d)
    tgt = loss_target.reshape(-1, d)
    mem2 = mem.reshape(-1, d)
    row = lambda v: v.reshape(1, -1)

    mem_n = rmsnorm_fwd(mem2, row(mem_norm_g), "mem_norm")
    saved = []
    for i in range(depth):
        j = i // 2
        kv = matmul(mem_n, weight("w_mem_kv", i), "nn", BF16, "mem_kv")
        if i % 2 == 0:
            w_in, w_out = weight("a_w_in", j), weight("a_w_out", j)
            sk = jnp.pad(jnp.broadcast_to(a_sinks[j][:, None], (Q_HEADS, 128)), ((0, 16 - Q_HEADS), (0, 0)))
            xn, proj = norm_matmul(h, row(mix_norm_g[i]), w_in, BF16, "a_in")
            cat = mixer_a_fwd(proj, sk, kv, "mixer_a")
            extra = (sk,)
        else:
            w_in, w_out = weight("b_w_in", j), weight("b_w_out", j)
            bt = jnp.pad(b_bias_s[j].T, ((0, 0), (0, 128 - B_GROUPS)))
            lg = jnp.pad(b_ln_g[j], ((0, 8 - B_GROUPS), (0, 0)))
            lb = jnp.pad(b_ln_b[j], ((0, 8 - B_GROUPS), (0, 0)))
            xn, proj = norm_matmul(h, row(mix_norm_g[i]), w_in, BF16, "b_in")
            cat = mixer_b_fwd(proj, b_w_s[j], bt, lg, lb, kv, "mixer_b")
            extra = (b_w_s[j], bt, lg, lb)
        h_mid = matmul(cat, w_out, "nn", F32, "mix_out", res=h)
        hn, gu, act = gate_up_fwd(h_mid, row(ffn_norm_g[i]), w_gate_up4, i, "gate_up")
        h_out = matmul(act, weight("w_down", i), "nn", F32, "down", res=h_mid)
        saved.append((h, xn, proj, cat, h_mid, hn, gu, act, kv, w_in, w_out, extra))
        h = h_out

    loss_part, dh, d_final_g = loss_head(h, row(final_norm_g), tgt, "loss_head")
    loss = lax.psum(loss_part[0, 0], ("x", "y", "c"))

    dw = {n: lax.empty((given[n].shape[0], 4) + given[n].shape[1:], F32) for n, _ in SHARDED if n not in ("a_w_in", "b_w_in")}
    dw_in = {"a_w_in": [None] * a_w_in.shape[0], "b_w_in": [None] * b_w_in.shape[0]}
    d_mix_g, d_ffn_g = [None] * depth, [None] * depth
    d_sinks, d_ws, d_bias, d_lg, d_lb = [], [], [], [], []
    d_mem_n = jnp.zeros(mem2.shape, F32)
    for i in reversed(range(depth)):
        j = i // 2
        h_in, xn, proj, cat, h_mid, hn, gu, act, kv, w_in, w_out, extra = saved[i]
        dgu = down_dx_swiglu_bwd(dh, weight("w_down", i), gu, "down_dx")
        dw["w_down"] = matmul(act, dh, "tn", F32, "down_dw", tm=1408, out_planes=("rows", 4), out_into=(dw["w_down"], i))
        dw["w_gate_up"] = matmul((hn, 0), dgu, "tn", F32, "gate_up_dw", tn=1408, out_planes=("cols", 4), out_into=(dw["w_gate_up"], i))
        dh, d_ffn_g[i] = dx_norm_bwd(dgu, (w_gate_up4, i), h_mid, row(ffn_norm_g[i]), dh, "gate_up_dx")
        dcat = matmul(dh, w_out, "nt", F32, "mix_out_dx")
        w_out_name = "a_w_out" if i % 2 == 0 else "b_w_out"
        dw[w_out_name] = matmul(cat, dh, "tn", F32, "mix_out_dw", out_planes=("rows", 4), out_into=(dw[w_out_name], j))
        if i % 2 == 0:
            dproj, dsk, dkv = mixer_a_bwd(proj, dcat, extra[0], kv, "mixer_a_bwd")
            d_sinks.insert(0, dsk[:Q_HEADS, 0])
            dw_in["a_w_in"][j] = matmul(xn, dproj, "tn", F32, "a_in_dw")
        else:
            dproj, dws, dbt, dlg, dlb, dkv = mixer_b_bwd(proj, dcat, *extra, kv, "mixer_b_bwd")
            d_ws.insert(0, dws)
            d_bias.insert(0, dbt[:, :B_GROUPS].T)
            d_lg.insert(0, dlg[:B_GROUPS])
            d_lb.insert(0, dlb[:B_GROUPS])
            dw_in["b_w_in"][j] = matmul(xn, dproj, "tn", F32, "b_in_dw")
        dw["w_mem_kv"] = matmul(mem_n, dkv, "tn", F32, "mem_kv_dw", out_planes=("rows", 4), out_into=(dw["w_mem_kv"], i))
        d_mem_n = matmul(dkv, weight("w_mem_kv", i), "nt", F32, "mem_kv_dx", res=d_mem_n)
        dh, d_mix_g[i] = dx_norm_bwd(dproj, w_in, h_in, row(mix_norm_g[i]), dh, "in_dx")
    grad_x = dh.reshape(x.shape)
    _, d_mem_g = rmsnorm_bwd(mem2, row(mem_norm_g), d_mem_n, jnp.zeros(mem2.shape, F32), "mem_norm_bwd")

    by_shard = [dw[n] if n in dw else _by_shard(jnp.stack(dw_in[n]), ax) for n, ax in SHARDED]
    from_sibling = sibling_swap_halves(by_shard, "grads_sibling_swap")
    partial = [chip_partial_sums(g, r1, c_arr, "grads_chip_sum") for g, r1 in zip(by_shard, from_sibling)]
    from_chips = chips_exchange(partial, "grads_chips_exchange")
    totals = [shard_total(g, r1, r2, cs_arr, "grads_shard_total") for g, r1, r2 in zip(by_shard, from_sibling, from_chips)]
    totals_sibling = sibling_swap(totals, "grads_sibling_totals")

    out = {}
    for (n, _), g_mine, g_theirs in zip(SHARDED, totals, totals_sibling):
        shape = given[n].shape
        two_d = lambda a: a.reshape(-1, shape[-1])
        res = adamw_halves(two_d(given[n]), two_d(g_mine), two_d(g_theirs), two_d(given["m_" + n]), two_d(given["v_" + n]),
                           c_arr, "adamw")
        out[n] = tuple(r.reshape(shape) for r in res)

    small = ("mem_norm_g", "mix_norm_g", "ffn_norm_g", "final_norm_g", "a_sinks", "b_w_s", "b_bias_s", "b_ln_g", "b_ln_b")
    small_g = [d_mem_g[0], jnp.concatenate(d_mix_g, axis=0), jnp.concatenate(d_ffn_g, axis=0), d_final_g[0],
               jnp.stack(d_sinks), jnp.stack(d_ws), jnp.stack(d_bias), jnp.stack(d_lg), jnp.stack(d_lb)]
    packed = _pack(small_g)
    g_small = sum_devices(allgather_small(packed, "small_allgather").reshape(8, *packed.shape), "small_sum")
    like = [given[n] for n in small]
    delta_s, new_m_s, new_v_s = adamw(_pack(like), g_small, _pack([given["m_" + n] for n in small]),
                                      _pack([given["v_" + n] for n in small]), "adamw_small")
    for n, g, dl, nm_, nv_ in zip(small, _unpack(g_small, like), _unpack(delta_s, like), _unpack(new_m_s, like), _unpack(new_v_s, like)):
        out[n] = (g, dl, nm_, nv_)

    order = ("mem_norm_g", "mix_norm_g", "ffn_norm_g", "final_norm_g", "a_w_in", "a_sinks", "a_w_out", "b_w_in", "b_w_s",
             "b_bias_s", "b_ln_g", "b_ln_b", "b_w_out", "w_mem_kv", "w_gate_up", "w_down")
    return (loss, grad_x, *[out[n][0] for n in order], *[out[n][1] for n in order],
            *[out[n][2] for n in order], *[out[n][3] for n in order])
```

```python
import jax
import jax.numpy as jnp
from jax import lax
from jax.experimental import pallas as pl
from jax.experimental.pallas import tpu as pltpu

F32, BF16 = jnp.float32, jnp.bfloat16
EPS = 1e-6
HEAD_DIM = 64
Q_HEADS, KV_HEADS, GROUP = 12, 2, 6
WINDOW = 128
MEM_HEADS = 4
B_GROUPS = 6
Q_W, KV_W, MEM_W, B_W = 768, 128, 256, 768
SCALE = HEAD_DIM ** -0.5
NEG = -1e30
ADAM_LR, ADAM_B1, ADAM_B2, ADAM_EPS, ADAM_WD, ADAM_STEP = 0.001, 0.9, 0.999, 1e-08, 0.01, 10
V7X_VMEM_LIMIT_BYTES = 48 * 1024 * 1024
MESH = pl.DeviceIdType.MESH
HBM_SPEC = pl.BlockSpec(memory_space=pltpu.HBM)
VMEM_SPEC = pl.BlockSpec(memory_space=pltpu.VMEM)


def _cp(*sem):
    return pltpu.CompilerParams(dimension_semantics=sem or None, vmem_limit_bytes=V7X_VMEM_LIMIT_BYTES)


def _tile(n, cands):
    for t in cands:
        if n % t == 0:
            return t
    return n


def _sds(shape, dtype):
    return jax.ShapeDtypeStruct(tuple(shape), dtype)


def _dot(a, b, ca, cb):
    return lax.dot_general(a, b, (((ca,), (cb,)), ((), ())), preferred_element_type=F32)


def _rms(x, g):
    return x * lax.rsqrt(jnp.mean(x * x, axis=-1, keepdims=True) + EPS) * g


def rmsnorm_fwd(h, g, name):
    t, d = h.shape
    tm = _tile(t, (512, 256, 128))

    def body(h_ref, g_ref, o_ref):
        o_ref[...] = _rms(h_ref[...], g_ref[...]).astype(o_ref.dtype)

    return pl.pallas_call(
        body, name=name, grid=(t // tm,),
        in_specs=[pl.BlockSpec((tm, d), lambda i: (i, 0)), pl.BlockSpec((1, d), lambda i: (0, 0))],
        out_specs=pl.BlockSpec((tm, d), lambda i: (i, 0)),
        out_shape=_sds((t, d), BF16), compiler_params=_cp("parallel"))(h, g)


def rmsnorm_bwd(h, g, dxn, dres, name):
    t, d = h.shape
    tm = _tile(t, (512, 256, 128))

    def body(h_ref, g_ref, dxn_ref, dres_ref, dh_ref, dg_ref):
        _, vjp = jax.vjp(_rms, h_ref[...], g_ref[...])
        dh, dg = vjp(dxn_ref[...].astype(F32))
        dh_ref[...] = dres_ref[...] + dh

        @pl.when(pl.program_id(0) == 0)
        def _():
            dg_ref[...] = jnp.zeros_like(dg_ref)

        dg_ref[...] += dg

    row = pl.BlockSpec((tm, d), lambda i: (i, 0))
    vec = pl.BlockSpec((1, d), lambda i: (0, 0))
    return pl.pallas_call(
        body, name=name, grid=(t // tm,), in_specs=[row, vec, row, row], out_specs=[row, vec],
        out_shape=[_sds((t, d), F32), _sds((1, d), F32)], compiler_params=_cp("arbitrary"))(h, g, dxn, dres)


def loss_head(h, g, tgt, name):
    t, d = h.shape
    tm = _tile(t, (512, 256, 128))

    def body(h_ref, g_ref, t_ref, l_ref, dh_ref, dg_ref):
        y, vjp = jax.vjp(_rms, h_ref[...], g_ref[...])
        err = y - t_ref[...]
        dh, dg = vjp(err * (1.0 / d))
        dh_ref[...] = dh
        part = 0.5 * jnp.sum(jnp.mean(err * err, axis=-1, keepdims=True), axis=0, keepdims=True)

        @pl.when(pl.program_id(0) == 0)
        def _():
            dg_ref[...] = jnp.zeros_like(dg_ref)
            l_ref[...] = jnp.zeros_like(l_ref)

        dg_ref[...] += dg
        l_ref[...] += part

    row = pl.BlockSpec((tm, d), lambda i: (i, 0))
    vec = pl.BlockSpec((1, d), lambda i: (0, 0))
    one = pl.BlockSpec((1, 1), lambda i: (0, 0))
    return pl.pallas_call(
        body, name=name, grid=(t // tm,), in_specs=[row, vec, row], out_specs=[one, row, vec],
        out_shape=[_sds((1, 1), F32), _sds((t, d), F32), _sds((1, d), F32)], compiler_params=_cp("arbitrary"))(h, g, tgt)


def _logical(op):
    arr, lead = op if isinstance(op, tuple) else (op, None)
    planes = arr.shape[-3] if arr.ndim - (lead is not None) == 3 else 1
    return arr, lead, arr.shape[-2], arr.shape[-1], planes


def _spec(op, rows_t, cols_t, row_of, col_of):
    arr, lead, _, cols, _ = _logical(op)
    per = cols // cols_t
    lead = () if lead is None else (lead,)
    if arr.ndim - len(lead) == 2:
        return pl.BlockSpec((None,) * len(lead) + (rows_t, cols_t), lambda *g: lead + (row_of(*g), col_of(*g)))
    return pl.BlockSpec((None,) * len(lead) + (None, rows_t, cols_t),
                        lambda *g: lead + (col_of(*g) // per, row_of(*g), col_of(*g) % per))


def _arr(op):
    return op[0] if isinstance(op, tuple) else op


def norm_matmul(h, g, w, out_dtype, name):
    t, d = h.shape
    wa, layer = w
    n = wa.shape[-1]
    tm = _tile(t, (512, 256, 128))

    def body(h_ref, g_ref, w_ref, xn_ref, o_ref):
        xn = _rms(h_ref[...], g_ref[...]).astype(BF16)
        xn_ref[...] = xn
        o_ref[...] = _dot(xn, w_ref[...], 1, 0).astype(o_ref.dtype)

    return pl.pallas_call(
        body, name=name, grid=(t // tm,),
        in_specs=[pl.BlockSpec((tm, d), lambda i: (i, 0)), pl.BlockSpec((1, d), lambda i: (0, 0)),
                  pl.BlockSpec((None, d, n), lambda i: (layer, 0, 0))],
        out_specs=[pl.BlockSpec((tm, d), lambda i: (i, 0)), pl.BlockSpec((tm, n), lambda i: (i, 0))],
        out_shape=[_sds((t, d), BF16), _sds((t, n), out_dtype)], compiler_params=_cp("parallel"))(h, g, wa)


def dx_norm_bwd(dy, w, h, g, dres, name, rider=None):
    t, d = h.shape
    dy_arr, dy_lead, _, kc, kp = _logical(dy)
    w_arr, w_lead, _, wc, wp = _logical(w)
    assert kc * kp == wc * wp and dy_lead is None, name
    chunk = min(kc, wc)
    tm = _tile(t, (512, 256, 128))

    def piece(ref, planes, cols, q):
        off = q * chunk % cols
        return ref[q * chunk // cols, :, off:off + chunk] if planes > 1 else ref[:, off:off + chunk]

    def body(dy_ref, w_ref, h_ref, g_ref, dres_ref, dh_ref, dg_ref):
        dxn = None
        for q in range(kc * kp // chunk):
            p = _dot(piece(dy_ref, kp, kc, q).astype(BF16), piece(w_ref, wp, wc, q), 1, 1)
            dxn = p if dxn is None else dxn + p
        _, vjp = jax.vjp(_rms, h_ref[...], g_ref[...])
        dh, dg = vjp(dxn)
        dh_ref[...] = dres_ref[...] + dh

        @pl.when(pl.program_id(0) == 0)
        def _():
            dg_ref[...] = jnp.zeros_like(dg_ref)

        dg_ref[...] += dg

    w_lead = () if w_lead is None else (w_lead,)
    w_block = ((wp,) if wp > 1 else ()) + (d, wc)
    w_spec = pl.BlockSpec((None,) * len(w_lead) + w_block, lambda i: w_lead + (0,) * len(w_block), pipeline_mode=pl.Buffered(1))
    dy_spec = pl.BlockSpec((kp, tm, kc), lambda i: (0, i, 0)) if kp > 1 else pl.BlockSpec((tm, kc), lambda i: (i, 0))
    row = pl.BlockSpec((tm, d), lambda i: (i, 0))
    vec = pl.BlockSpec((1, d), lambda i: (0, 0))
    grid = (t // tm,)
    body, r_ops, r_in, r_shapes, r_out, r_scratch = with_rider(body, 5, 2, grid, rider)
    return pl.pallas_call(
        body, name=name, grid=grid, in_specs=[dy_spec, w_spec, row, vec, row] + r_in,
        out_specs=[row, vec] + r_out, out_shape=[_sds((t, d), F32), _sds((1, d), F32)] + r_shapes, scratch_shapes=r_scratch,
        compiler_params=_cp("arbitrary"))(dy_arr, w_arr, h, g, dres, *r_ops)


def matmul(a, b, mode, out_dtype, name, res=None, tm=None, tn=1792, tk=2816, out_planes=None, out_into=None):
    _, _, ar, ac, ap = _logical(a)
    _, _, br, bc, bp = _logical(b)
    if mode == "nn":
        m, ka, kb, n = ar, ac * ap, br, bc * bp
        n_plane, ka_plane, kb_plane = bc, ac, br
    elif mode == "nt":
        m, ka, n, kb = ar, ac * ap, br, bc * bp
        n_plane, ka_plane, kb_plane = br, ac, bc
    else:
        ka, m, kb, n = ar, ac * ap, br, bc * bp
        n_plane, ka_plane, kb_plane = bc, ar, br
    m_plane = ac if mode == "tn" else ar
    assert ka == kb, name
    k = ka
    kind, planes = out_planes or ("cols", 1)
    if kind == "cols":
        n_plane = min(n_plane, n // planes)
    tm = _tile(m_plane, ((1024, 1408, 512, 256, 128) if mode == "tn" else (512, 256, 128)) if tm is None else (tm, 1024, 512, 256, 128))
    if kind == "rows" and tm % (m // planes):
        tm = m_plane
    tn = _tile(n_plane, (tn, 1792, 1408, 1280, 1024, 896, 640, 512, 256, 128))
    tk = _tile(min(ka_plane, kb_plane), (tk, 2816, 1792, 1408, 1280, 1024, 512, 256, 128))
    nk = k // tk
    row_i, col_j, red = (lambda i, j, kk: i), (lambda i, j, kk: j), (lambda i, j, kk: kk)
    if mode == "nn":
        a_spec, b_spec, ca, cb = _spec(a, tm, tk, row_i, red), _spec(b, tk, tn, red, col_j), 1, 0
    elif mode == "nt":
        a_spec, b_spec, ca, cb = _spec(a, tm, tk, row_i, red), _spec(b, tn, tk, col_j, red), 1, 1
    else:
        a_spec, b_spec, ca, cb = _spec(a, tk, tm, red, row_i), _spec(b, tk, tn, red, col_j), 0, 0
    lead = () if out_into is None else (out_into[1],)
    if planes == 1:
        o_shape, o_block = (m, n), (tm, tn)
        o_index = lambda i, j, kk: lead + (i, j)
    elif kind == "cols":
        per = n // planes // tn
        o_shape, o_block = (planes, m, n // planes), (None, tm, tn)
        o_index = lambda i, j, kk: lead + (j // per, i, j % per)
    else:
        o_shape, o_block = (planes, m // planes, n), (tm // (m // planes), m // planes, tn)
        o_index = lambda i, j, kk: lead + (i, 0, j)
    o_spec = pl.BlockSpec((None,) * len(lead) + o_block, o_index)
    if out_into is not None:
        assert out_into[0].shape[1:] == o_shape and out_into[0].dtype == out_dtype, name
        o_shape = out_into[0].shape
    has_res = res is not None
    n_in = 2 + has_res + (out_into is not None)

    def body(*refs):
        a_ref, b_ref = refs[:2]
        rest = refs[2:2 + has_res] + refs[n_in:]
        o_ref = rest[1] if has_res else rest[0]
        p = _dot(a_ref[...].astype(BF16), b_ref[...].astype(BF16), ca, cb)
        if nk == 1:
            if has_res:
                p = p + rest[0][...]
            o_ref[...] = p.astype(o_ref.dtype).reshape(o_ref.shape)
        else:
            acc_ref = rest[-1]
            kk = pl.program_id(2)

            @pl.when(kk == 0)
            def _():
                acc_ref[...] = p

            @pl.when(kk > 0)
            def _():
                acc_ref[...] += p

            @pl.when(kk == nk - 1)
            def _():
                r = acc_ref[...]
                if has_res:
                    r = r + rest[0][...]
                o_ref[...] = r.astype(o_ref.dtype).reshape(o_ref.shape)

    operands = [_arr(a), _arr(b)] + ([res] if has_res else []) + ([out_into[0]] if out_into is not None else [])
    return pl.pallas_call(
        body, name=name, grid=(m // tm, n // tn, nk),
        in_specs=[a_spec, b_spec] + ([pl.BlockSpec((tm, tn), lambda i, j, kk: (i, j))] if has_res else [])
        + ([pl.BlockSpec(memory_space=pl.ANY)] if out_into is not None else []),
        out_specs=o_spec, out_shape=_sds(o_shape, out_dtype),
        input_output_aliases={n_in - 1: 0} if out_into is not None else {},
        scratch_shapes=[pltpu.VMEM((tm, tn), F32)] if nk > 1 else [],
        compiler_params=_cp("parallel", "parallel", "arbitrary"))(*operands)


def _swiglu(gate, up):
    return gate / (1.0 + jnp.exp(-gate)) * up


def gate_up_fwd(h, g, w, layer, name, rider=None):
    t, d = h.shape
    half = w.shape[-1]
    tm = _tile(t, (512, 256, 128))

    def body(h_ref, g_ref, wg_ref, wu_ref, hn_ref, gu_ref, act_ref):
        a = _rms(h_ref[...], g_ref[...]).astype(BF16)
        hn_ref[...] = a
        gate, up = _dot(a, wg_ref[...], 1, 0), _dot(a, wu_ref[...], 1, 0)
        gu_ref[0] = gate.astype(gu_ref.dtype)
        gu_ref[1] = up.astype(gu_ref.dtype)
        act_ref[...] = _swiglu(gate, up).astype(act_ref.dtype)

    grid = (2, t // tm)
    body, r_ops, r_in, r_shapes, r_out, r_scratch = with_rider(body, 4, 3, grid, rider)
    return pl.pallas_call(
        body, name=name, grid=grid,
        in_specs=[pl.BlockSpec((tm, d), lambda j, i: (i, 0)), pl.BlockSpec((1, d), lambda j, i: (0, 0)),
                  pl.BlockSpec((None, None, d, half), lambda j, i: (layer, j, 0, 0)),
                  pl.BlockSpec((None, None, d, half), lambda j, i: (layer, 2 + j, 0, 0))] + r_in,
        out_specs=[pl.BlockSpec((None, tm, d), lambda j, i: (j, i, 0)), pl.BlockSpec((2, tm, half), lambda j, i: (0, i, j)),
                   pl.BlockSpec((tm, half), lambda j, i: (i, j))] + r_out,
        out_shape=[_sds((2, t, d), BF16), _sds((2, t, 2 * half), BF16), _sds((t, 2 * half), BF16)] + r_shapes,
        scratch_shapes=r_scratch, compiler_params=_cp("arbitrary", "arbitrary"))(h, g, w, w, *r_ops)


def down_dx_swiglu_bwd(dh, wd, gu, name, rider=None):
    t, d = dh.shape
    w, layer = wd
    f = w.shape[-2]
    tm = _tile(t, (512, 256, 128))
    tn = _tile(f, (1408, 512, 256, 128))

    def body(dh_ref, w_ref, gu_ref, o_ref):
        dact = _dot(dh_ref[...].astype(BF16), w_ref[...], 1, 1)
        gate, up = gu_ref[0].astype(F32), gu_ref[1].astype(F32)
        sig = 1.0 / (1.0 + jnp.exp(-gate))
        silu = gate * sig
        o_ref[0] = (dact * up * (sig + silu * (1.0 - sig))).astype(o_ref.dtype)
        o_ref[1] = (dact * silu).astype(o_ref.dtype)

    planes = pl.BlockSpec((2, tm, tn), lambda j, i: (0, i, j))
    grid = (f // tn, t // tm)
    body, r_ops, r_in, r_shapes, r_out, r_scratch = with_rider(body, 3, 1, grid, rider)
    return pl.pallas_call(
        body, name=name, grid=grid,
        in_specs=[pl.BlockSpec((tm, d), lambda j, i: (i, 0)), pl.BlockSpec((None, tn, d), lambda j, i: (layer, j, 0)), planes] + r_in,
        out_specs=[planes] + r_out, out_shape=[_sds((2, t, f), BF16)] + r_shapes, scratch_shapes=r_scratch,
        compiler_params=_cp("arbitrary", "arbitrary"))(dh, w, gu, *r_ops)


def _softmax(s, sink=None):
    m = s.max(axis=-1, keepdims=True)
    if sink is not None:
        m = jnp.maximum(m, sink)
    m = lax.stop_gradient(m)
    e = jnp.exp(s - m)
    den = e.sum(axis=-1, keepdims=True)
    if sink is not None:
        den = den + jnp.exp(sink - m)
    return e * (1.0 / den)


def _low_lanes():
    return lax.broadcasted_iota(jnp.int32, (1, 128), 1) < HEAD_DIM


def _stack_heads(slabs):
    low = _low_lanes()
    return jnp.concatenate([p for s in slabs for p in (jnp.where(low, s, 0.0), jnp.where(low, 0.0, s))], axis=0)


def _unstack_heads(o, n_slabs):
    low = _low_lanes()
    return [jnp.where(low, o[2 * j * WINDOW:(2 * j + 1) * WINDOW], o[(2 * j + 1) * WINDOW:(2 * j + 2) * WINDOW])
            for j in range(n_slabs)]


def _swa_group(q_slabs, k_both, v_both, sinks, mask):
    qs = _stack_heads(q_slabs).astype(BF16)
    s = jnp.where(mask, _dot(qs, k_both.astype(BF16), 1, 1) * SCALE, NEG)
    sink = jnp.concatenate([jnp.broadcast_to(v, (WINDOW, 1)) for v in sinks], axis=0)
    return _unstack_heads(_dot(_softmax(s, sink).astype(BF16), v_both.astype(BF16), 1, 0), len(q_slabs))


def _mem_pair(q_slab, k_slab, v_slab):
    s = _dot(_stack_heads([q_slab]).astype(BF16), k_slab.astype(BF16), 1, 1) * SCALE
    return _unstack_heads(_dot(_softmax(s).astype(BF16), v_slab.astype(BF16), 1, 0), 1)[0]


def _gelu(x):
    return 0.5 * x * (1.0 + jnp.tanh(0.7978845608028654 * (x + 0.044715 * (x * x * x))))


def _gmlp_group(zu, zv, w, bcol, lg, lb, tri):
    u, v = _gelu(zu), _gelu(zv)
    mu = jnp.mean(v, axis=-1, keepdims=True)
    var = jnp.mean(jnp.square(v - mu), axis=-1, keepdims=True)
    vn = (v - mu) * lax.rsqrt(var + EPS) * lg + lb
    sv = _dot(jnp.where(tri, w, 0.0).astype(BF16), vn.astype(BF16), 1, 0) + bcol
    return u * sv


def _cols(x, width):
    return [x[:, j * width:(j + 1) * width] for j in range(x.shape[1] // width)]


def _swa_mask(has_prev):
    qi = lax.broadcasted_iota(jnp.int32, (GROUP * WINDOW, 2 * WINDOW), 0) & (WINDOW - 1)
    kj = lax.broadcasted_iota(jnp.int32, (GROUP * WINDOW, 2 * WINDOW), 1)
    in_prev = jnp.logical_and(jnp.logical_and(kj < WINDOW, kj > qi), has_prev)
    return jnp.logical_or(in_prev, jnp.logical_and(kj >= WINDOW, kj - WINDOW <= qi))


def _mix_a(q_slabs, k_boths, v_boths, sinks, qm_slabs, km_slabs, vm_slabs, mask):
    per = GROUP // 2
    outs = []
    for g in range(KV_HEADS):
        outs += _swa_group(q_slabs[per * g:per * (g + 1)], k_boths[g], v_boths[g], sinks[GROUP * g:GROUP * (g + 1)], mask)
    return outs + [_mem_pair(qm_slabs[j], km_slabs[j], vm_slabs[j]) for j in range(MEM_HEADS // 2)]


def _in_both_halves(prev, cur):
    cat = jnp.concatenate([prev, cur], axis=0)
    rolled = pltpu.roll(cat, HEAD_DIM, axis=1)
    low = _low_lanes()
    return [jnp.where(low, cat, rolled), jnp.where(low, rolled, cat)]


def _from_both_halves(d_boths):
    t = [d + pltpu.roll(d, HEAD_DIM, axis=1) for d in d_boths]
    return jnp.where(_low_lanes(), t[0], t[1])


def _mix_a_specs(nm, blk):
    prev = lambda n: jnp.maximum(blk(n) - 1, 0)
    return [pl.BlockSpec((WINDOW, Q_W), lambda n: (blk(n), 0)),
            pl.BlockSpec((WINDOW, KV_W), lambda n: (prev(n), Q_W // KV_W)),
            pl.BlockSpec((WINDOW, KV_W), lambda n: (blk(n), Q_W // KV_W)),
            pl.BlockSpec((WINDOW, KV_W), lambda n: (prev(n), Q_W // KV_W + 1)),
            pl.BlockSpec((WINDOW, KV_W), lambda n: (blk(n), Q_W // KV_W + 1)),
            pl.BlockSpec((WINDOW, MEM_W), lambda n: (blk(n), (Q_W + 2 * KV_W) // MEM_W)),
            pl.BlockSpec((16, 128), lambda n: (0, 0)),
            pl.BlockSpec((nm, MEM_W), lambda n: (0, 0)),
            pl.BlockSpec((nm, MEM_W), lambda n: (0, 1))]


def _mix_a_args(refs):
    q, kp, kc, vp, vc, qm, sk, km, vm = [r[...].astype(F32) for r in refs]
    return (_cols(q, 128), _in_both_halves(kp, kc), _in_both_halves(vp, vc), [sk[h:h + 1, 0:1] for h in range(Q_HEADS)],
            _cols(qm, 128), _cols(km, 128), _cols(vm, 128))


def mixer_a_fwd(proj, sk, kv, name):
    t, nm = proj.shape[0], kv.shape[0]

    def body(*refs):
        o_ref = refs[-1]
        slabs = _mix_a(*_mix_a_args(refs[:-1]), _swa_mask(pl.program_id(0) > 0))
        o_ref[...] = jnp.concatenate(slabs, axis=1).astype(o_ref.dtype)

    return pl.pallas_call(
        body, name=name, grid=(t // WINDOW,), in_specs=_mix_a_specs(nm, lambda n: n),
        out_specs=pl.BlockSpec((WINDOW, Q_W + MEM_W), lambda n: (n, 0)),
        out_shape=_sds((t, Q_W + MEM_W), BF16), compiler_params=_cp("parallel"))(proj, proj, proj, proj, proj, proj, sk, kv, kv)


def _onehot_rows(vals, shape):
    rows = lax.broadcasted_iota(jnp.int32, shape, 0)
    out = jnp.zeros(shape, F32)
    for h, v in enumerate(vals):
        out = out + jnp.where(rows == h, jnp.broadcast_to(v, shape), 0.0)
    return out


def mixer_a_bwd(proj, dcat, sk, kv, name):
    t, nm = proj.shape[0], kv.shape[0]
    nb = t // WINDOW
    blk = lambda i: nb - 1 - i

    def body(*refs):
        dcat_ref, dproj_ref, dsk_ref, dkv_ref, carry_ref = refs[9:]
        i = pl.program_id(0)

        @pl.when(i == 0)
        def _():
            carry_ref[...] = jnp.zeros_like(carry_ref)
            dsk_ref[...] = jnp.zeros_like(dsk_ref)
            dkv_ref[...] = jnp.zeros_like(dkv_ref)

        mask = _swa_mask(blk(i) > 0)
        _, vjp = jax.vjp(lambda *a: _mix_a(*a, mask), *_mix_a_args(refs[:9]))
        dqs, dk_boths, dv_boths, dsinks, dqms, dkms, dvms = vjp(_cols(dcat_ref[...].astype(F32), 128))
        dkv = jnp.concatenate([_from_both_halves(dk_boths), _from_both_halves(dv_boths)], axis=1)
        dkv_cur = dkv[WINDOW:] + carry_ref[...]
        carry_ref[...] = dkv[:WINDOW]
        dproj_ref[...] = jnp.concatenate(dqs + [dkv_cur] + dqms, axis=1).astype(dproj_ref.dtype)
        dsk_ref[...] += _onehot_rows(dsinks, (16, 128))
        dkv_ref[...] += jnp.concatenate(dkms + dvms, axis=1)

    width = Q_W + 2 * KV_W + MEM_W
    return pl.pallas_call(
        body, name=name, grid=(nb,),
        in_specs=_mix_a_specs(nm, blk) + [pl.BlockSpec((WINDOW, Q_W + MEM_W), lambda i: (blk(i), 0))],
        out_specs=[pl.BlockSpec((WINDOW, width), lambda i: (blk(i), 0)), pl.BlockSpec((16, 128), lambda i: (0, 0)),
                   pl.BlockSpec((nm, 2 * MEM_W), lambda i: (0, 0))],
        out_shape=[_sds((t, width), BF16), _sds((16, 128), F32), _sds((nm, 2 * MEM_W), F32)],
        scratch_shapes=[pltpu.VMEM((WINDOW, 2 * KV_W), F32)],
        compiler_params=_cp("arbitrary"))(proj, proj, proj, proj, proj, proj, sk, kv, kv, dcat)


def _mix_b(zus, zvs, ws, bcols, lgs, lbs, qms, kms, vms, tri):
    outs = [_gmlp_group(zus[g], zvs[g], ws[g], bcols[g], lgs[g], lbs[g], tri) for g in range(B_GROUPS)]
    return outs + [_mem_pair(qms[j], kms[j], vms[j]) for j in range(MEM_HEADS // 2)]


def _mix_b_specs(nm):
    return [pl.BlockSpec((WINDOW, 2 * B_W), lambda n: (n, 0)),
            pl.BlockSpec((WINDOW, MEM_W), lambda n: (n, 2 * B_W // MEM_W)),
            pl.BlockSpec((B_GROUPS, WINDOW, WINDOW), lambda n: (0, 0, 0)),
            pl.BlockSpec((WINDOW, 128), lambda n: (0, 0)),
            pl.BlockSpec((8, 128), lambda n: (0, 0)),
            pl.BlockSpec((8, 128), lambda n: (0, 0)),
            pl.BlockSpec((nm, MEM_W), lambda n: (0, 0)),
            pl.BlockSpec((nm, MEM_W), lambda n: (0, 1))]


def _mix_b_args(refs):
    z, qm, ws, bt, lg, lb, km, vm = [r[...].astype(F32) for r in refs]
    zs = _cols(z, 128)
    return (zs[:B_GROUPS], zs[B_GROUPS:], [ws[g] for g in range(B_GROUPS)], [bt[:, g:g + 1] for g in range(B_GROUPS)],
            [lg[g:g + 1, :] for g in range(B_GROUPS)], [lb[g:g + 1, :] for g in range(B_GROUPS)],
            _cols(qm, 128), _cols(km, 128), _cols(vm, 128))


def _tri():
    return lax.broadcasted_iota(jnp.int32, (WINDOW, WINDOW), 0) >= lax.broadcasted_iota(jnp.int32, (WINDOW, WINDOW), 1)


def mixer_b_fwd(proj, ws, bt, lg, lb, kv, name):
    t, nm = proj.shape[0], kv.shape[0]

    def body(*refs):
        o_ref = refs[-1]
        o_ref[...] = jnp.concatenate(_mix_b(*_mix_b_args(refs[:-1]), _tri()), axis=1).astype(o_ref.dtype)

    return pl.pallas_call(
        body, name=name, grid=(t // WINDOW,), in_specs=_mix_b_specs(nm),
        out_specs=pl.BlockSpec((WINDOW, B_W + MEM_W), lambda n: (n, 0)),
        out_shape=_sds((t, B_W + MEM_W), BF16), compiler_params=_cp("parallel"))(proj, proj, ws, bt, lg, lb, kv, kv)


def mixer_b_bwd(proj, dcat, ws, bt, lg, lb, kv, name):
    t, nm = proj.shape[0], kv.shape[0]

    def body(*refs):
        dcat_ref, dproj_ref, dws_ref, dbt_ref, dlg_ref, dlb_ref, dkv_ref = refs[8:]

        @pl.when(pl.program_id(0) == 0)
        def _():
            for r in (dws_ref, dbt_ref, dlg_ref, dlb_ref, dkv_ref):
                r[...] = jnp.zeros_like(r)

        tri = _tri()
        _, vjp = jax.vjp(lambda *a: _mix_b(*a, tri), *_mix_b_args(refs[:8]))
        dzus, dzvs, dws, dbcols, dlgs, dlbs, dqms, dkms, dvms = vjp(_cols(dcat_ref[...].astype(F32), 128))
        dproj_ref[...] = jnp.concatenate(dzus + dzvs + dqms, axis=1).astype(dproj_ref.dtype)
        for g in range(B_GROUPS):
            dws_ref[g] += dws[g]
        lanes = lax.broadcasted_iota(jnp.int32, (WINDOW, 128), 1)
        dbt = jnp.zeros((WINDOW, 128), F32)
        for g in range(B_GROUPS):
            dbt = dbt + jnp.where(lanes == g, jnp.broadcast_to(dbcols[g], (WINDOW, 128)), 0.0)
        dbt_ref[...] += dbt
        dlg_ref[...] += _onehot_rows(dlgs, (8, 128))
        dlb_ref[...] += _onehot_rows(dlbs, (8, 128))
        dkv_ref[...] += jnp.concatenate(dkms + dvms, axis=1)

    width = 2 * B_W + MEM_W
    const2 = lambda n: (0, 0)
    return pl.pallas_call(
        body, name=name, grid=(t // WINDOW,),
        in_specs=_mix_b_specs(nm) + [pl.BlockSpec((WINDOW, B_W + MEM_W), lambda n: (n, 0))],
        out_specs=[pl.BlockSpec((WINDOW, width), lambda n: (n, 0)),
                   pl.BlockSpec((B_GROUPS, WINDOW, WINDOW), lambda n: (0, 0, 0)),
                   pl.BlockSpec((WINDOW, 128), const2), pl.BlockSpec((8, 128), const2), pl.BlockSpec((8, 128), const2),
                   pl.BlockSpec((nm, 2 * MEM_W), const2)],
        out_shape=[_sds((t, width), BF16), _sds((B_GROUPS, WINDOW, WINDOW), F32), _sds((WINDOW, 128), F32),
                   _sds((8, 128), F32), _sds((8, 128), F32), _sds((nm, 2 * MEM_W), F32)],
        compiler_params=_cp("arbitrary"))(proj, proj, ws, bt, lg, lb, kv, kv, dcat)


def _adamw_update(w, g, m, v):
    m2 = ADAM_B1 * m + (1.0 - ADAM_B1) * g
    v2 = ADAM_B2 * v + (1.0 - ADAM_B2) * jnp.square(g)
    m_hat = m2 / (1.0 - ADAM_B1 ** ADAM_STEP)
    v_hat = v2 / (1.0 - ADAM_B2 ** ADAM_STEP)
    return -ADAM_LR * (m_hat / (jnp.sqrt(v_hat) + ADAM_EPS) + ADAM_WD * w), m2, v2


def adamw(w, g, m, v, name):
    r, c = w.shape
    tr = _tile(r, (512, 352, 256, 128, 64, 32, 16, 8))

    def body(w_ref, g_ref, m_ref, v_ref, d_ref, nm_ref, nv_ref):
        d_ref[...], nm_ref[...], nv_ref[...] = _adamw_update(w_ref[...], g_ref[...], m_ref[...], v_ref[...])

    spec = pl.BlockSpec((tr, c), lambda i: (i, 0))
    return pl.pallas_call(
        body, name=name, grid=(r // tr,), in_specs=[spec] * 4, out_specs=[spec] * 3,
        out_shape=[_sds((r, c), F32)] * 3, compiler_params=_cp("parallel"))(w, g, m, v)


def adamw_halves(w, g_mine, g_theirs, m, v, c_arr, rows, name):
    r, c = w.shape
    tr = _tile(rows // 2, (256, 352, 128, 64, 32, 16, 8))
    per_half = rows // 2 // tr

    def body(c_ref, w_ref, gm_ref, gt_ref, m_ref, v_ref, g_ref, d_ref, nm_ref, nv_ref):
        g = jnp.where(pl.program_id(0) // per_half % 2 == c_ref[0], gm_ref[...], gt_ref[...])
        g_ref[...] = g
        d_ref[...], nm_ref[...], nv_ref[...] = _adamw_update(w_ref[...], g, m_ref[...], v_ref[...])

    spec = pl.BlockSpec((tr, c), lambda i, cr: (i, 0))
    half = pl.BlockSpec((tr, c), lambda i, cr: (i // (2 * per_half) * per_half + i % per_half, 0))
    return pl.pallas_call(
        body, name=name,
        grid_spec=pltpu.PrefetchScalarGridSpec(num_scalar_prefetch=1, grid=(r // tr,), in_specs=[spec, half, half, spec, spec],
                                               out_specs=[spec] * 4),
        out_shape=[_sds((r, c), F32)] * 4, compiler_params=_cp("parallel"))(c_arr, w, g_mine, g_theirs, m, v)


def _place():
    return lax.axis_index("x"), lax.axis_index("y"), lax.axis_index("c")


def _other_chips(x, y):
    return [(1 - x, y), (x, 1 - y), (1 - x, 1 - y)]


def _remote(src, dst, send_sems, recv_sems, k, dev):
    return pltpu.make_async_remote_copy(src_ref=src, dst_ref=dst, send_sem=send_sems.at[k], recv_sem=recv_sems.at[k],
                                        device_id=dev, device_id_type=MESH)


class Exchange:
    def __init__(self, ins, out_shapes, n_sems, start, finish):
        self.ins, self.out_shapes, self.n_sems, self.start, self.finish = list(ins), list(out_shapes), n_sems, start, finish

    def scratch(self):
        return [pltpu.SemaphoreType.DMA((self.n_sems,)), pltpu.SemaphoreType.DMA((self.n_sems,))]


def run_exchange(ex, name):
    ni, no = len(ex.ins), len(ex.out_shapes)

    def body(*refs):
        ex.start(refs[:ni], refs[ni:ni + no], *refs[ni + no:])
        ex.finish(refs[:ni], refs[ni:ni + no], *refs[ni + no:])

    return pl.pallas_call(
        body, name=name, in_specs=[HBM_SPEC] * ni, out_specs=[HBM_SPEC] * no, out_shape=ex.out_shapes, scratch_shapes=ex.scratch(),
        compiler_params=pltpu.CompilerParams(has_side_effects=True))(*ex.ins)


def with_rider(body, n_in, n_out, grid, ex):
    if ex is None:
        return body, [], [], [], [], []
    ni, no = len(ex.ins), len(ex.out_shapes)

    def riding(*refs):
        r_in, r_out, sems = refs[n_in:n_in + ni], refs[n_in + ni + n_out:n_in + ni + n_out + no], refs[-2:]
        first = last = None
        for axis, size in enumerate(grid):
            at_first, at_last = pl.program_id(axis) == 0, pl.program_id(axis) == size - 1
            first = at_first if first is None else jnp.logical_and(first, at_first)
            last = at_last if last is None else jnp.logical_and(last, at_last)

        @pl.when(first)
        def _():
            ex.start(r_in, r_out, *sems)

        body(*refs[:n_in], *refs[n_in + ni:n_in + ni + n_out], *refs[n_in + ni + n_out + no:-2])

        @pl.when(last)
        def _():
            ex.finish(r_in, r_out, *sems)

    return riding, ex.ins, [HBM_SPEC] * ni, ex.out_shapes, [HBM_SPEC] * no, ex.scratch()


def gather_exchange(shards):
    nw = len(shards)

    def rows(ref, cc):
        return pl.ds(cc * (ref.shape[1] // 2), ref.shape[1] // 2)

    def sent(ins, outs, send_sems, recv_sems, w, j):
        x, y, c = _place()
        return _remote(ins[w].at[:, rows(ins[w], c)], outs[w].at[:, 2 * x + y, rows(ins[w], c)], send_sems, recv_sems, 6 * w + j,
                       (*_other_chips(x, y)[j], c))

    def landed(ins, outs, send_sems, recv_sems, w, j, cc, to):
        x, y, c = _place()
        chip = _other_chips(x, y)[j]
        blk = outs[w].at[:, 2 * chip[0] + chip[1], rows(ins[w], cc)]
        return _remote(blk, blk, send_sems, recv_sems, 6 * w + (j if to is None else 3 + j), (x, y, c) if to is None else to)

    def start(ins, outs, send_sems, recv_sems):
        for j in range(3):
            for w in range(nw):
                sent(ins, outs, send_sems, recv_sems, w, j).start()

    def finish(ins, outs, send_sems, recv_sems):
        x, y, c = _place()
        for j in range(3):
            for w in range(nw):
                landed(ins, outs, send_sems, recv_sems, w, j, c, None).wait_recv()
                landed(ins, outs, send_sems, recv_sems, w, j, c, (x, y, 1 - c)).start()
        for j in range(3):
            for w in range(nw):
                landed(ins, outs, send_sems, recv_sems, w, j, 1 - c, (x, y, c)).wait_recv()
        for j in range(3):
            for w in range(nw):
                sent(ins, outs, send_sems, recv_sems, w, j).wait_send()
                landed(ins, outs, send_sems, recv_sems, w, j, c, (x, y, 1 - c)).wait_send()

    return Exchange(shards, [_sds((s.shape[0], 4) + s.shape[1:], s.dtype) for s in shards], 6 * nw, start, finish)


def copies_exchange(ins, out_shapes, n_sems, copies):
    def start(*refs):
        for cp in copies(*refs):
            cp.start()

    def finish(*refs):
        for cp in copies(*refs):
            cp.wait()

    return Exchange(ins, out_shapes, n_sems, start, finish)


def sibling_halves_exchange(gs):
    def copies(ins, outs, send_sems, recv_sems):
        x, y, c = _place()
        return [_remote(g.at[:, :, pl.ds((1 - c) * (g.shape[2] // 2), g.shape[2] // 2)], o, send_sems, recv_sems, w, (x, y, 1 - c))
                for w, (g, o) in enumerate(zip(ins, outs))]

    return copies_exchange(gs, [_sds(g.shape[:2] + (g.shape[2] // 2, g.shape[3]), g.dtype) for g in gs], len(gs), copies)


def chips_exchange(sbs):
    def copies(ins, outs, send_sems, recv_sems):
        x, y, c = _place()
        return [_remote(s.at[:, 2 * chip[0] + chip[1]], o.at[j], send_sems, recv_sems, 3 * w + j, (*chip, c))
                for j, chip in enumerate(_other_chips(x, y)) for w, (s, o) in enumerate(zip(ins, outs))]

    return copies_exchange(sbs, [_sds((3, s.shape[0]) + s.shape[2:], s.dtype) for s in sbs], 3 * len(sbs), copies)


def sibling_exchange(fs):
    def copies(ins, outs, send_sems, recv_sems):
        x, y, c = _place()
        return [_remote(f, o, send_sems, recv_sems, w, (x, y, 1 - c)) for w, (f, o) in enumerate(zip(ins, outs))]

    return copies_exchange(fs, [_sds(f.shape, f.dtype) for f in fs], len(fs), copies)


def _half_tile(a):
    return _tile(a, (256, 352, 176, 128, 64, 32, 16))


def chip_partial_sums(g, r1, c_arr, name):
    nl, _, a2, b = r1.shape
    ta = _half_tile(a2)
    per = a2 // ta

    def body(c_ref, g_ref, r_ref, o_ref):
        o_ref[...] = (g_ref[...] + r_ref[...]).astype(o_ref.dtype)

    blk = (None, None, ta, b)
    return pl.pallas_call(
        body, name=name,
        grid_spec=pltpu.PrefetchScalarGridSpec(
            num_scalar_prefetch=1, grid=(nl, 4, per),
            in_specs=[pl.BlockSpec(blk, lambda l, s, i, c: (l, s, c[0] * per + i, 0)), pl.BlockSpec(blk, lambda l, s, i, c: (l, s, i, 0))],
            out_specs=pl.BlockSpec(blk, lambda l, s, i, c: (l, s, i, 0))),
        out_shape=_sds(r1.shape, BF16), compiler_params=_cp("parallel", "parallel", "parallel"))(c_arr, g, r1)


def shard_total(g, r1, r2, cs_arr, name):
    nl, _, a2, b = r1.shape
    ta = _half_tile(a2)
    per = a2 // ta

    def body(cs_ref, g_ref, r1_ref, p0_ref, p1_ref, p2_ref, o_ref):
        o_ref[...] = (((g_ref[...] + r1_ref[...]) + p0_ref[...].astype(F32)) + p1_ref[...].astype(F32)) + p2_ref[...].astype(F32)

    blk4, blk3 = (None, None, ta, b), (None, ta, b)
    peer = lambda k: pl.BlockSpec((None, None, ta, b), lambda l, i, cs: (k, l, i, 0))
    return pl.pallas_call(
        body, name=name,
        grid_spec=pltpu.PrefetchScalarGridSpec(
            num_scalar_prefetch=1, grid=(nl, per),
            in_specs=[pl.BlockSpec(blk4, lambda l, i, cs: (l, cs[1], cs[0] * per + i, 0)),
                      pl.BlockSpec(blk4, lambda l, i, cs: (l, cs[1], i, 0)), peer(0), peer(1), peer(2)],
            out_specs=pl.BlockSpec(blk3, lambda l, i, cs: (l, i, 0))),
        out_shape=_sds((nl, a2, b), F32), compiler_params=_cp("parallel", "parallel"))(cs_arr, g, r1, r2, r2, r2)


def allgather_small(v, name):
    r, n = v.shape

    def body(x_ref, out_ref, send_sems, recv_sems, local_sem):
        x, y, c = _place()
        me, sibling = (x, y, c), (x, y, 1 - c)
        chips = _other_chips(x, y)

        def rows(px, py, pc):
            return out_ref.at[pl.ds((4 * px + 2 * py + pc) * r, r), :]

        def copy(k, block, to, src=None):
            return _remote(rows(*block) if src is None else src, rows(*block), send_sems, recv_sems, k, to)

        mine = pltpu.make_async_copy(x_ref, rows(*me), local_sem)
        mine.start()
        first = [copy(0, me, sibling, src=x_ref)] + [copy(1 + j, me, (*chip, c), src=x_ref) for j, chip in enumerate(chips)]
        for cp in first:
            cp.start()
        passed = [copy(4 + j, (*chip, c), sibling) for j, chip in enumerate(chips)]
        for j, chip in enumerate(chips):
            copy(1 + j, (*chip, c), me).wait_recv()
            passed[j].start()
        copy(0, sibling, me).wait_recv()
        for j, chip in enumerate(chips):
            copy(4 + j, (*chip, 1 - c), me).wait_recv()
        for cp in first + passed:
            cp.wait_send()
        mine.wait()

    return pl.pallas_call(
        body, name=name, in_specs=[VMEM_SPEC], out_specs=VMEM_SPEC, out_shape=_sds((8 * r, n), v.dtype),
        scratch_shapes=[pltpu.SemaphoreType.DMA((7,)), pltpu.SemaphoreType.DMA((7,)), pltpu.SemaphoreType.DMA],
        compiler_params=pltpu.CompilerParams(has_side_effects=True, vmem_limit_bytes=V7X_VMEM_LIMIT_BYTES))(v)


def sum_devices(v8, name):
    _, r, n = v8.shape
    tr = _tile(r, (88, 64, 32, 16, 8))

    def body(v_ref, o_ref):
        acc = v_ref[0]
        for d in range(1, 8):
            acc = acc + v_ref[d]
        o_ref[...] = acc

    return pl.pallas_call(
        body, name=name, grid=(r // tr,), in_specs=[pl.BlockSpec((8, tr, n), lambda i: (0, i, 0))],
        out_specs=pl.BlockSpec((tr, n), lambda i: (i, 0)), out_shape=_sds((r, n), F32), compiler_params=_cp("parallel"))(v8)


SHARDED = (("a_w_in", 2), ("a_w_out", 1), ("b_w_in", 2), ("b_w_out", 1), ("w_mem_kv", 1), ("w_gate_up", 2), ("w_down", 1))


def _full_from_gathered(wg, axis):
    l, _, a, b = wg.shape
    if axis == 1:
        return wg.reshape(l, 4 * a, b)
    return wg.transpose(0, 2, 1, 3).reshape(l, a, 4 * b)


def _by_shard(dw, axis):
    l, k, n = dw.shape
    if axis == 1:
        return dw.reshape(l, 4, k // 4, n)
    return dw.reshape(l, k, 4, n // 4).transpose(0, 2, 1, 3)


def _pack(arrs):
    parts = []
    for a in arrs:
        flat = a.reshape(-1)
        flat = jnp.pad(flat, (0, -flat.shape[0] % 1024))
        parts.append(flat.reshape(-1, 128))
    return jnp.concatenate(parts, axis=0)


def _unpack(buf, like):
    out, row = [], 0
    for a in like:
        size = 1
        for s in a.shape:
            size *= s
        rows = -(-size // 1024) * 8
        out.append(buf[row:row + rows].reshape(-1)[:size].reshape(a.shape))
        row += rows
    return out


def kernel(x, mem, mem_norm_g, mix_norm_g, ffn_norm_g, final_norm_g, a_w_in, a_sinks, a_w_out, b_w_in, b_w_s, b_bias_s, b_ln_g, b_ln_b, b_w_out, w_mem_kv, w_gate_up, w_down, loss_target, m_mem_norm_g, m_mix_norm_g, m_ffn_norm_g, m_final_norm_g, m_a_w_in, m_a_sinks, m_a_w_out, m_b_w_in, m_b_w_s, m_b_bias_s, m_b_ln_g, m_b_ln_b, m_b_w_out, m_w_mem_kv, m_w_gate_up, m_w_down, v_mem_norm_g, v_mix_norm_g, v_ffn_norm_g, v_final_norm_g, v_a_w_in, v_a_sinks, v_a_w_out, v_b_w_in, v_b_w_s, v_b_bias_s, v_b_ln_g, v_b_ln_b, v_b_w_out, v_w_mem_kv, v_w_gate_up, v_w_down):
    given = dict(locals())
    depth = mix_norm_g.shape[0]
    d = x.shape[-1]
    xi, yi, ci = _place()
    c_arr = jnp.stack([ci]).astype(jnp.int32)
    cs_arr = jnp.stack([ci, 2 * xi + yi]).astype(jnp.int32)

    axis_of = dict(SHARDED)
    own = {n: given[n].astype(BF16) for n, _ in SHARDED}

    def layer_weights(l):
        mix = "a" if l % 2 == 0 else "b"
        return [(mix + "_w_in", l // 2), (mix + "_w_out", l // 2), ("w_mem_kv", l), ("w_gate_up", l), ("w_down", l)]

    def gather_of(l):
        return gather_exchange([own[n][k:k + 1] for n, k in layer_weights(l)])

    def usable(l, gathered):
        ops = {}
        for (n, k), wg in zip(layer_weights(l), gathered):
            wg = lax.dynamic_update_slice(wg, own[n][k:k + 1, None], (0, 2 * xi + yi, 0, 0))
            ops[n[2:] if n[0] in "ab" else n] = (wg if n == "w_gate_up" else _full_from_gathered(wg, axis_of[n]), 0)
        return ops

    weights = {0: usable(0, run_exchange(gather_of(0), "gather_weights"))}

    h = x.reshape(-1, d)
    tgt = loss_target.reshape(-1, d)
    mem2 = mem.reshape(-1, d)
    row = lambda v: v.reshape(1, -1)

    mem_n = rmsnorm_fwd(mem2, row(mem_norm_g), "mem_norm")
    saved = []
    for i in range(depth):
        j = i // 2
        wl = weights[i]
        w_in, w_out = wl["w_in"], wl["w_out"]
        kv = matmul(mem_n, wl["w_mem_kv"], "nn", BF16, "mem_kv")
        if i % 2 == 0:
            sk = jnp.pad(jnp.broadcast_to(a_sinks[j][:, None], (Q_HEADS, 128)), ((0, 16 - Q_HEADS), (0, 0)))
            xn, proj = norm_matmul(h, row(mix_norm_g[i]), w_in, BF16, "a_in")
            cat = mixer_a_fwd(proj, sk, kv, "mixer_a")
            extra = (sk,)
        else:
            bt = jnp.pad(b_bias_s[j].T, ((0, 0), (0, 128 - B_GROUPS)))
            lg = jnp.pad(b_ln_g[j], ((0, 8 - B_GROUPS), (0, 0)))
            lb = jnp.pad(b_ln_b[j], ((0, 8 - B_GROUPS), (0, 0)))
            xn, proj = norm_matmul(h, row(mix_norm_g[i]), w_in, BF16, "b_in")
            cat = mixer_b_fwd(proj, b_w_s[j], bt, lg, lb, kv, "mixer_b")
            extra = (b_w_s[j], bt, lg, lb)
        h_mid = matmul(cat, w_out, "nn", F32, "mix_out", res=h)
        hn, gu, act, *gathered = gate_up_fwd(h_mid, row(ffn_norm_g[i]), *wl["w_gate_up"], "gate_up",
                                             rider=gather_of(i + 1) if i + 1 < depth else None)
        if gathered:
            weights[i + 1] = usable(i + 1, gathered)
        h_out = matmul(act, wl["w_down"], "nn", F32, "down", res=h_mid)
        saved.append((h, xn, proj, cat, h_mid, hn, gu, act, kv, extra))
        h = h_out

    loss_part, dh, d_final_g = loss_head(h, row(final_norm_g), tgt, "loss_head")
    loss = lax.psum(loss_part[0, 0], ("x", "y", "c"))

    d_mix_g, d_ffn_g = [None] * depth, [None] * depth
    d_sinks, d_ws, d_bias, d_lg, d_lb = [], [], [], [], []
    d_mem_n = jnp.zeros(mem2.shape, F32)
    totals = [None] * depth
    pending = None
    for i in reversed(range(depth)):
        h_in, xn, proj, cat, h_mid, hn, gu, act, kv, extra = saved[i]
        wl = weights[i]
        dgu, *from_sibling = down_dx_swiglu_bwd(dh, wl["w_down"], gu, "down_dx",
                                                rider=sibling_halves_exchange(pending) if pending else None)
        if pending:
            partial = [chip_partial_sums(g, r1, c_arr, "grads_chip_sum") for g, r1 in zip(pending, from_sibling)]
        dw_down = matmul(act, dh, "tn", F32, "down_dw", tm=1408, out_planes=("rows", 4))
        dw_gate_up = matmul((hn, 0), dgu, "tn", F32, "gate_up_dw", tn=1408, out_planes=("cols", 4))
        dh, d_ffn_g[i], *from_chips = dx_norm_bwd(dgu, wl["w_gate_up"], h_mid, row(ffn_norm_g[i]), dh, "gate_up_dx",
                                                  rider=chips_exchange(partial) if pending else None)
        if pending:
            totals[i + 1] = [shard_total(g, r1, r2, cs_arr, "grads_shard_total") for g, r1, r2 in zip(pending, from_sibling, from_chips)]
        dcat = matmul(dh, wl["w_out"], "nt", F32, "mix_out_dx")
        dw_out = matmul(cat, dh, "tn", F32, "mix_out_dw", out_planes=("rows", 4))
        if i % 2 == 0:
            dproj, dsk, dkv = mixer_a_bwd(proj, dcat, extra[0], kv, "mixer_a_bwd")
            d_sinks.insert(0, dsk[:Q_HEADS, 0])
            dw_in = matmul(xn, dproj, "tn", F32, "a_in_dw")
        else:
            dproj, dws, dbt, dlg, dlb, dkv = mixer_b_bwd(proj, dcat, *extra, kv, "mixer_b_bwd")
            d_ws.insert(0, dws)
            d_bias.insert(0, dbt[:, :B_GROUPS].T)
            d_lg.insert(0, dlg[:B_GROUPS])
            d_lb.insert(0, dlb[:B_GROUPS])
            dw_in = matmul(xn, dproj, "tn", F32, "b_in_dw")
        dw_kv = matmul(mem_n, dkv, "tn", F32, "mem_kv_dw", out_planes=("rows", 4))
        d_mem_n = matmul(dkv, wl["w_mem_kv"], "nt", F32, "mem_kv_dx", res=d_mem_n)
        dh, d_mix_g[i] = dx_norm_bwd(dproj, wl["w_in"], h_in, row(mix_norm_g[i]), dh, "in_dx")
        pending = [_by_shard(dw_in[None], 2), dw_out[None], dw_kv[None], dw_gate_up[None], dw_down[None]]
    grad_x = dh.reshape(x.shape)
    _, d_mem_g = rmsnorm_bwd(mem2, row(mem_norm_g), d_mem_n, jnp.zeros(mem2.shape, F32), "mem_norm_bwd")

    from_sibling = run_exchange(sibling_halves_exchange(pending), "grads_sibling_swap")
    partial = [chip_partial_sums(g, r1, c_arr, "grads_chip_sum") for g, r1 in zip(pending, from_sibling)]
    from_chips = run_exchange(chips_exchange(partial), "grads_chips_exchange")
    totals[0] = [shard_total(g, r1, r2, cs_arr, "grads_shard_total") for g, r1, r2 in zip(pending, from_sibling, from_chips)]

    mine = {n: [None] * given[n].shape[0] for n, _ in SHARDED}
    for l in range(depth):
        for (n, k), tot in zip(layer_weights(l), totals[l]):
            mine[n][k] = tot
    mine = [jnp.concatenate(mine[n], axis=0) for n, _ in SHARDED]
    theirs = run_exchange(sibling_exchange(mine), "grads_sibling_totals")
    out = {}
    for (n, _), g_mine, g_theirs in zip(SHARDED, mine, theirs):
        shape = given[n].shape
        two_d = lambda a: a.reshape(-1, shape[-1])
        res = adamw_halves(two_d(given[n]), two_d(g_mine), two_d(g_theirs), two_d(given["m_" + n]), two_d(given["v_" + n]),
                           c_arr, shape[1], "adamw")
        out[n] = tuple(r.reshape(shape) for r in res)

    small = ("mem_norm_g", "mix_norm_g", "ffn_norm_g", "final_norm_g", "a_sinks", "b_w_s", "b_bias_s", "b_ln_g", "b_ln_b")
    small_g = [d_mem_g[0], jnp.concatenate(d_mix_g, axis=0), jnp.concatenate(d_ffn_g, axis=0), d_final_g[0],
               jnp.stack(d_sinks), jnp.stack(d_ws), jnp.stack(d_bias), jnp.stack(d_lg), jnp.stack(d_lb)]
    packed = _pack(small_g)
    g_small = sum_devices(allgather_small(packed, "small_allgather").reshape(8, *packed.shape), "small_sum")
    like = [given[n] for n in small]
    delta_s, new_m_s, new_v_s = adamw(_pack(like), g_small, _pack([given["m_" + n] for n in small]),
                                      _pack([given["v_" + n] for n in small]), "adamw_small")
    for n, g, dl, nm_, nv_ in zip(small, _unpack(g_small, like), _unpack(delta_s, like), _unpack(new_m_s, like), _unpack(new_v_s, like)):
        out[n] = (g, dl, nm_, nv_)

    order = ("mem_norm_g", "mix_norm_g", "ffn_norm_g", "final_norm_g", "a_w_in", "a_sinks", "a_w_out", "b_w_in", "b_w_s",
             "b_bias_s", "b_ln_g", "b_ln_b", "b_w_out", "w_mem_kv", "w_gate_up", "w_down")
    return (loss, grad_x, *[out[n][0] for n in order], *[out[n][1] for n in order],
            *[out[n][2] for n in order], *[out[n][3] for n in order])
```

```python
import jax
import jax.numpy as jnp
from jax import lax
from jax.experimental import pallas as pl
from jax.experimental.pallas import tpu as pltpu

F32, BF16 = jnp.float32, jnp.bfloat16
EPS = 1e-6
HEAD_DIM = 64
Q_HEADS, KV_HEADS, GROUP = 12, 2, 6
WINDOW = 128
MEM_HEADS = 4
B_GROUPS = 6
Q_W, KV_W, MEM_W, B_W = 768, 128, 256, 768
SCALE = HEAD_DIM ** -0.5
NEG = -1e30
ADAM_LR, ADAM_B1, ADAM_B2, ADAM_EPS, ADAM_WD, ADAM_STEP = 0.001, 0.9, 0.999, 1e-08, 0.01, 10
V7X_VMEM_LIMIT_BYTES = 48 * 1024 * 1024
MESH = pl.DeviceIdType.MESH
HBM_SPEC = pl.BlockSpec(memory_space=pltpu.HBM)
VMEM_SPEC = pl.BlockSpec(memory_space=pltpu.VMEM)


def _cp(*sem):
    return pltpu.CompilerParams(dimension_semantics=sem or None, vmem_limit_bytes=V7X_VMEM_LIMIT_BYTES)


def _tile(n, cands):
    for t in cands:
        if n % t == 0:
            return t
    return n


def _sds(shape, dtype):
    return jax.ShapeDtypeStruct(tuple(shape), dtype)


def _dot(a, b, ca, cb):
    return lax.dot_general(a, b, (((ca,), (cb,)), ((), ())), preferred_element_type=F32)


def _rms(x, g):
    return x * lax.rsqrt(jnp.mean(x * x, axis=-1, keepdims=True) + EPS) * g


def rmsnorm_fwd(h, g, name):
    t, d = h.shape
    tm = _tile(t, (512, 256, 128))

    def body(h_ref, g_ref, o_ref):
        o_ref[...] = _rms(h_ref[...], g_ref[...]).astype(o_ref.dtype)

    return pl.pallas_call(
        body, name=name, grid=(t // tm,),
        in_specs=[pl.BlockSpec((tm, d), lambda i: (i, 0)), pl.BlockSpec((1, d), lambda i: (0, 0))],
        out_specs=pl.BlockSpec((tm, d), lambda i: (i, 0)),
        out_shape=_sds((t, d), BF16), compiler_params=_cp("parallel"))(h, g)


def rmsnorm_bwd(h, g, dxn, dres, name):
    t, d = h.shape
    tm = _tile(t, (512, 256, 128))

    def body(h_ref, g_ref, dxn_ref, dres_ref, dh_ref, dg_ref):
        _, vjp = jax.vjp(_rms, h_ref[...], g_ref[...])
        dh, dg = vjp(dxn_ref[...].astype(F32))
        dh_ref[...] = dres_ref[...] + dh

        @pl.when(pl.program_id(0) == 0)
        def _():
            dg_ref[...] = jnp.zeros_like(dg_ref)

        dg_ref[...] += dg

    row = pl.BlockSpec((tm, d), lambda i: (i, 0))
    vec = pl.BlockSpec((1, d), lambda i: (0, 0))
    return pl.pallas_call(
        body, name=name, grid=(t // tm,), in_specs=[row, vec, row, row], out_specs=[row, vec],
        out_shape=[_sds((t, d), F32), _sds((1, d), F32)], compiler_params=_cp("arbitrary"))(h, g, dxn, dres)


def loss_head(h, g, tgt, name):
    t, d = h.shape
    tm = _tile(t, (512, 256, 128))

    def body(h_ref, g_ref, t_ref, l_ref, dh_ref, dg_ref):
        y, vjp = jax.vjp(_rms, h_ref[...], g_ref[...])
        err = y - t_ref[...]
        dh, dg = vjp(err * (1.0 / d))
        dh_ref[...] = dh
        part = 0.5 * jnp.sum(jnp.mean(err * err, axis=-1, keepdims=True), axis=0, keepdims=True)

        @pl.when(pl.program_id(0) == 0)
        def _():
            dg_ref[...] = jnp.zeros_like(dg_ref)
            l_ref[...] = jnp.zeros_like(l_ref)

        dg_ref[...] += dg
        l_ref[...] += part

    row = pl.BlockSpec((tm, d), lambda i: (i, 0))
    vec = pl.BlockSpec((1, d), lambda i: (0, 0))
    one = pl.BlockSpec((1, 1), lambda i: (0, 0))
    return pl.pallas_call(
        body, name=name, grid=(t // tm,), in_specs=[row, vec, row], out_specs=[one, row, vec],
        out_shape=[_sds((1, 1), F32), _sds((t, d), F32), _sds((1, d), F32)], compiler_params=_cp("arbitrary"))(h, g, tgt)


def _logical(op):
    arr, lead = op if isinstance(op, tuple) else (op, None)
    planes = arr.shape[-3] if arr.ndim - (lead is not None) == 3 else 1
    return arr, lead, arr.shape[-2], arr.shape[-1], planes


def _spec(op, rows_t, cols_t, row_of, col_of):
    arr, lead, _, cols, _ = _logical(op)
    per = cols // cols_t
    lead = () if lead is None else (lead,)
    if arr.ndim - len(lead) == 2:
        return pl.BlockSpec((None,) * len(lead) + (rows_t, cols_t), lambda *g: lead + (row_of(*g), col_of(*g)))
    return pl.BlockSpec((None,) * len(lead) + (None, rows_t, cols_t),
                        lambda *g: lead + (col_of(*g) // per, row_of(*g), col_of(*g) % per))


def _arr(op):
    return op[0] if isinstance(op, tuple) else op


def norm_matmul(h, g, w, out_dtype, name):
    t, d = h.shape
    wa, layer = w
    n = wa.shape[-1]
    tm = _tile(t, (512, 256, 128))

    def body(h_ref, g_ref, w_ref, xn_ref, o_ref):
        xn = _rms(h_ref[...], g_ref[...]).astype(BF16)
        xn_ref[...] = xn
        o_ref[...] = _dot(xn, w_ref[...], 1, 0).astype(o_ref.dtype)

    return pl.pallas_call(
        body, name=name, grid=(t // tm,),
        in_specs=[pl.BlockSpec((tm, d), lambda i: (i, 0)), pl.BlockSpec((1, d), lambda i: (0, 0)),
                  pl.BlockSpec((None, d, n), lambda i: (layer, 0, 0))],
        out_specs=[pl.BlockSpec((tm, d), lambda i: (i, 0)), pl.BlockSpec((tm, n), lambda i: (i, 0))],
        out_shape=[_sds((t, d), BF16), _sds((t, n), out_dtype)], compiler_params=_cp("parallel"))(h, g, wa)


def dx_norm_bwd(dy, w, h, g, dres, name, rider=None):
    t, d = h.shape
    dy_arr, dy_lead, _, kc, kp = _logical(dy)
    w_arr, w_lead, _, wc, wp = _logical(w)
    assert kc * kp == wc * wp and dy_lead is None, name
    chunk = min(kc, wc)
    tm = _tile(t, (512, 256, 128))

    def piece(ref, planes, cols, q):
        off = q * chunk % cols
        return ref[q * chunk // cols, :, off:off + chunk] if planes > 1 else ref[:, off:off + chunk]

    def body(dy_ref, w_ref, h_ref, g_ref, dres_ref, dh_ref, dg_ref):
        dxn = None
        for q in range(kc * kp // chunk):
            p = _dot(piece(dy_ref, kp, kc, q).astype(BF16), piece(w_ref, wp, wc, q), 1, 1)
            dxn = p if dxn is None else dxn + p
        _, vjp = jax.vjp(_rms, h_ref[...], g_ref[...])
        dh, dg = vjp(dxn)
        dh_ref[...] = dres_ref[...] + dh

        @pl.when(pl.program_id(0) == 0)
        def _():
            dg_ref[...] = jnp.zeros_like(dg_ref)

        dg_ref[...] += dg

    w_lead = () if w_lead is None else (w_lead,)
    w_block = ((wp,) if wp > 1 else ()) + (d, wc)
    w_spec = pl.BlockSpec((None,) * len(w_lead) + w_block, lambda i: w_lead + (0,) * len(w_block), pipeline_mode=pl.Buffered(1))
    dy_spec = pl.BlockSpec((kp, tm, kc), lambda i: (0, i, 0)) if kp > 1 else pl.BlockSpec((tm, kc), lambda i: (i, 0))
    row = pl.BlockSpec((tm, d), lambda i: (i, 0))
    vec = pl.BlockSpec((1, d), lambda i: (0, 0))
    grid = (t // tm,)
    body, r_ops, r_in, r_shapes, r_out, r_scratch = with_rider(body, 5, 2, grid, rider)
    return pl.pallas_call(
        body, name=name, grid=grid, in_specs=[dy_spec, w_spec, row, vec, row] + r_in,
        out_specs=[row, vec] + r_out, out_shape=[_sds((t, d), F32), _sds((1, d), F32)] + r_shapes, scratch_shapes=r_scratch,
        compiler_params=_cp("arbitrary"))(dy_arr, w_arr, h, g, dres, *r_ops)


def matmul(a, b, mode, out_dtype, name, res=None, tm=None, tn=1792, tk=2816, out_planes=None, out_into=None):
    _, _, ar, ac, ap = _logical(a)
    _, _, br, bc, bp = _logical(b)
    if mode == "nn":
        m, ka, kb, n = ar, ac * ap, br, bc * bp
        n_plane, ka_plane, kb_plane = bc, ac, br
    elif mode == "nt":
        m, ka, n, kb = ar, ac * ap, br, bc * bp
        n_plane, ka_plane, kb_plane = br, ac, bc
    else:
        ka, m, kb, n = ar, ac * ap, br, bc * bp
        n_plane, ka_plane, kb_plane = bc, ar, br
    m_plane = ac if mode == "tn" else ar
    assert ka == kb, name
    k = ka
    kind, planes = out_planes or ("cols", 1)
    if kind == "cols":
        n_plane = min(n_plane, n // planes)
    tm = _tile(m_plane, ((1024, 1408, 512, 256, 128) if mode == "tn" else (512, 256, 128)) if tm is None else (tm, 1024, 512, 256, 128))
    if kind == "rows" and tm % (m // planes):
        tm = m_plane
    tn = _tile(n_plane, (tn, 1792, 1408, 1280, 1024, 896, 640, 512, 256, 128))
    tk = _tile(min(ka_plane, kb_plane), (tk, 2816, 1792, 1408, 1280, 1024, 512, 256, 128))
    nk = k // tk
    row_i, col_j, red = (lambda i, j, kk: i), (lambda i, j, kk: j), (lambda i, j, kk: kk)
    if mode == "nn":
        a_spec, b_spec, ca, cb = _spec(a, tm, tk, row_i, red), _spec(b, tk, tn, red, col_j), 1, 0
    elif mode == "nt":
        a_spec, b_spec, ca, cb = _spec(a, tm, tk, row_i, red), _spec(b, tn, tk, col_j, red), 1, 1
    else:
        a_spec, b_spec, ca, cb = _spec(a, tk, tm, red, row_i), _spec(b, tk, tn, red, col_j), 0, 0
    lead = () if out_into is None else (out_into[1],)
    if planes == 1:
        o_shape, o_block = (m, n), (tm, tn)
        o_index = lambda i, j, kk: lead + (i, j)
    elif kind == "cols":
        per = n // planes // tn
        o_shape, o_block = (planes, m, n // planes), (None, tm, tn)
        o_index = lambda i, j, kk: lead + (j // per, i, j % per)
    else:
        o_shape, o_block = (planes, m // planes, n), (tm // (m // planes), m // planes, tn)
        o_index = lambda i, j, kk: lead + (i, 0, j)
    o_spec = pl.BlockSpec((None,) * len(lead) + o_block, o_index)
    if out_into is not None:
        assert out_into[0].shape[1:] == o_shape and out_into[0].dtype == out_dtype, name
        o_shape = out_into[0].shape
    has_res = res is not None
    n_in = 2 + has_res + (out_into is not None)

    def body(*refs):
        a_ref, b_ref = refs[:2]
        rest = refs[2:2 + has_res] + refs[n_in:]
        o_ref = rest[1] if has_res else rest[0]
        p = _dot(a_ref[...].astype(BF16), b_ref[...].astype(BF16), ca, cb)
        if nk == 1:
            if has_res:
                p = p + rest[0][...]
            o_ref[...] = p.astype(o_ref.dtype).reshape(o_ref.shape)
        else:
            acc_ref = rest[-1]
            kk = pl.program_id(2)

            @pl.when(kk == 0)
            def _():
                acc_ref[...] = p

            @pl.when(kk > 0)
            def _():
                acc_ref[...] += p

            @pl.when(kk == nk - 1)
            def _():
                r = acc_ref[...]
                if has_res:
                    r = r + rest[0][...]
                o_ref[...] = r.astype(o_ref.dtype).reshape(o_ref.shape)

    operands = [_arr(a), _arr(b)] + ([res] if has_res else []) + ([out_into[0]] if out_into is not None else [])
    return pl.pallas_call(
        body, name=name, grid=(m // tm, n // tn, nk),
        in_specs=[a_spec, b_spec] + ([pl.BlockSpec((tm, tn), lambda i, j, kk: (i, j))] if has_res else [])
        + ([pl.BlockSpec(memory_space=pl.ANY)] if out_into is not None else []),
        out_specs=o_spec, out_shape=_sds(o_shape, out_dtype),
        input_output_aliases={n_in - 1: 0} if out_into is not None else {},
        scratch_shapes=[pltpu.VMEM((tm, tn), F32)] if nk > 1 else [],
        compiler_params=_cp("parallel", "parallel", "arbitrary"))(*operands)


def _swiglu(gate, up):
    return gate / (1.0 + jnp.exp(-gate)) * up


def gate_up_fwd(h, g, w, layer, name, rider=None):
    t, d = h.shape
    half = w.shape[-1]
    tm = _tile(t, (512, 256, 128))

    def body(h_ref, g_ref, wg_ref, wu_ref, hn_ref, gu_ref, act_ref):
        a = _rms(h_ref[...], g_ref[...]).astype(BF16)
        hn_ref[...] = a
        gate, up = _dot(a, wg_ref[...], 1, 0), _dot(a, wu_ref[...], 1, 0)
        gu_ref[0] = gate.astype(gu_ref.dtype)
        gu_ref[1] = up.astype(gu_ref.dtype)
        act_ref[...] = _swiglu(gate, up).astype(act_ref.dtype)

    grid = (2, t // tm)
    body, r_ops, r_in, r_shapes, r_out, r_scratch = with_rider(body, 4, 3, grid, rider)
    return pl.pallas_call(
        body, name=name, grid=grid,
        in_specs=[pl.BlockSpec((tm, d), lambda j, i: (i, 0)), pl.BlockSpec((1, d), lambda j, i: (0, 0)),
                  pl.BlockSpec((None, None, d, half), lambda j, i: (layer, j, 0, 0)),
                  pl.BlockSpec((None, None, d, half), lambda j, i: (layer, 2 + j, 0, 0))] + r_in,
        out_specs=[pl.BlockSpec((None, tm, d), lambda j, i: (j, i, 0)), pl.BlockSpec((2, tm, half), lambda j, i: (0, i, j)),
                   pl.BlockSpec((tm, half), lambda j, i: (i, j))] + r_out,
        out_shape=[_sds((2, t, d), BF16), _sds((2, t, 2 * half), BF16), _sds((t, 2 * half), BF16)] + r_shapes,
        scratch_shapes=r_scratch, compiler_params=_cp("arbitrary", "arbitrary"))(h, g, w, w, *r_ops)


def down_dx_swiglu_bwd(dh, wd, gu, name, rider=None):
    t, d = dh.shape
    w, layer = wd
    f = w.shape[-2]
    tm = _tile(t, (512, 256, 128))
    tn = _tile(f, (1408, 512, 256, 128))

    def body(dh_ref, w_ref, gu_ref, o_ref):
        dact = _dot(dh_ref[...].astype(BF16), w_ref[...], 1, 1)
        gate, up = gu_ref[0].astype(F32), gu_ref[1].astype(F32)
        sig = 1.0 / (1.0 + jnp.exp(-gate))
        silu = gate * sig
        o_ref[0] = (dact * up * (sig + silu * (1.0 - sig))).astype(o_ref.dtype)
        o_ref[1] = (dact * silu).astype(o_ref.dtype)

    planes = pl.BlockSpec((2, tm, tn), lambda j, i: (0, i, j))
    grid = (f // tn, t // tm)
    body, r_ops, r_in, r_shapes, r_out, r_scratch = with_rider(body, 3, 1, grid, rider)
    return pl.pallas_call(
        body, name=name, grid=grid,
        in_specs=[pl.BlockSpec((tm, d), lambda j, i: (i, 0)), pl.BlockSpec((None, tn, d), lambda j, i: (layer, j, 0)), planes] + r_in,
        out_specs=[planes] + r_out, out_shape=[_sds((2, t, f), BF16)] + r_shapes, scratch_shapes=r_scratch,
        compiler_params=_cp("arbitrary", "arbitrary"))(dh, w, gu, *r_ops)


def _softmax_over_keys(s, sink=None):
    m = s.max(axis=0, keepdims=True)
    if sink is not None:
        m = jnp.maximum(m, sink)
    m = lax.stop_gradient(m)
    e = jnp.exp(s - m)
    den = e.sum(axis=0, keepdims=True)
    if sink is not None:
        den = den + jnp.exp(sink - m)
    return e * (1.0 / den)


def _low_lanes():
    return lax.broadcasted_iota(jnp.int32, (1, 128), 1) < HEAD_DIM


def _stack_heads(slabs):
    low = _low_lanes()
    return jnp.concatenate([p for s in slabs for p in (jnp.where(low, s, 0.0), jnp.where(low, 0.0, s))], axis=0)


def _unstack_heads(o, n_slabs):
    low = _low_lanes()
    return [jnp.where(low, o[2 * j * WINDOW:(2 * j + 1) * WINDOW], o[(2 * j + 1) * WINDOW:(2 * j + 2) * WINDOW])
            for j in range(n_slabs)]


def _swa_group(q_slabs, k_both, v_both, sinks, mask):
    qs = _stack_heads(q_slabs).astype(BF16)
    s = jnp.where(mask, _dot(k_both.astype(BF16), qs, 1, 1) * SCALE, NEG)
    sink = jnp.concatenate([jnp.broadcast_to(v, (1, WINDOW)) for v in sinks], axis=1)
    return _unstack_heads(_dot(_softmax_over_keys(s, sink).astype(BF16), v_both.astype(BF16), 0, 0), len(q_slabs))


def _mem_pair(q_slab, k_slab, v_slab):
    s = _dot(k_slab.astype(BF16), _stack_heads([q_slab]).astype(BF16), 1, 1) * SCALE
    return _unstack_heads(_dot(_softmax_over_keys(s).astype(BF16), v_slab.astype(BF16), 0, 0), 1)[0]


def _gelu(x):
    return 0.5 * x * (1.0 + jnp.tanh(0.7978845608028654 * (x + 0.044715 * (x * x * x))))


def _gmlp_group(zu, zv, w, bcol, lg, lb, tri):
    u, v = _gelu(zu), _gelu(zv)
    mu = jnp.mean(v, axis=-1, keepdims=True)
    var = jnp.mean(jnp.square(v - mu), axis=-1, keepdims=True)
    vn = (v - mu) * lax.rsqrt(var + EPS) * lg + lb
    sv = _dot(jnp.where(tri, w, 0.0).astype(BF16), vn.astype(BF16), 1, 0) + bcol
    return u * sv


def _cols(x, width):
    return [x[:, j * width:(j + 1) * width] for j in range(x.shape[1] // width)]


def _swa_mask(has_prev):
    qi = lax.broadcasted_iota(jnp.int32, (2 * WINDOW, GROUP * WINDOW), 1) & (WINDOW - 1)
    kj = lax.broadcasted_iota(jnp.int32, (2 * WINDOW, GROUP * WINDOW), 0)
    in_prev = jnp.logical_and(jnp.logical_and(kj < WINDOW, kj > qi), has_prev)
    return jnp.logical_or(in_prev, jnp.logical_and(kj >= WINDOW, kj - WINDOW <= qi))


def _mix_a(q_slabs, k_boths, v_boths, sinks, qm_slabs, km_slabs, vm_slabs, mask):
    per = GROUP // 2
    outs = []
    for g in range(KV_HEADS):
        outs += _swa_group(q_slabs[per * g:per * (g + 1)], k_boths[g], v_boths[g], sinks[GROUP * g:GROUP * (g + 1)], mask)
    return outs + [_mem_pair(qm_slabs[j], km_slabs[j], vm_slabs[j]) for j in range(MEM_HEADS // 2)]


def _in_both_halves(prev, cur):
    cat = jnp.concatenate([prev, cur], axis=0)
    rolled = pltpu.roll(cat, HEAD_DIM, axis=1)
    low = _low_lanes()
    return [jnp.where(low, cat, rolled), jnp.where(low, rolled, cat)]


def _from_both_halves(d_boths):
    t = [d + pltpu.roll(d, HEAD_DIM, axis=1) for d in d_boths]
    return jnp.where(_low_lanes(), t[0], t[1])


def _mix_a_specs(nm, blk):
    prev = lambda n: jnp.maximum(blk(n) - 1, 0)
    return [pl.BlockSpec((WINDOW, Q_W), lambda n: (blk(n), 0)),
            pl.BlockSpec((WINDOW, KV_W), lambda n: (prev(n), Q_W // KV_W)),
            pl.BlockSpec((WINDOW, KV_W), lambda n: (blk(n), Q_W // KV_W)),
            pl.BlockSpec((WINDOW, KV_W), lambda n: (prev(n), Q_W // KV_W + 1)),
            pl.BlockSpec((WINDOW, KV_W), lambda n: (blk(n), Q_W // KV_W + 1)),
            pl.BlockSpec((WINDOW, MEM_W), lambda n: (blk(n), (Q_W + 2 * KV_W) // MEM_W)),
            pl.BlockSpec((16, 128), lambda n: (0, 0)),
            pl.BlockSpec((nm, MEM_W), lambda n: (0, 0)),
            pl.BlockSpec((nm, MEM_W), lambda n: (0, 1))]


def _mix_a_args(refs):
    q, kp, kc, vp, vc, qm, sk, km, vm = [r[...].astype(F32) for r in refs]
    return (_cols(q, 128), _in_both_halves(kp, kc), _in_both_halves(vp, vc), [sk[h:h + 1, 0:1] for h in range(Q_HEADS)],
            _cols(qm, 128), _cols(km, 128), _cols(vm, 128))


def mixer_a_fwd(proj, sk, kv, name):
    t, nm = proj.shape[0], kv.shape[0]

    def body(*refs):
        o_ref = refs[-1]
        slabs = _mix_a(*_mix_a_args(refs[:-1]), _swa_mask(pl.program_id(0) > 0))
        o_ref[...] = jnp.concatenate(slabs, axis=1).astype(o_ref.dtype)

    return pl.pallas_call(
        body, name=name, grid=(t // WINDOW,), in_specs=_mix_a_specs(nm, lambda n: n),
        out_specs=pl.BlockSpec((WINDOW, Q_W + MEM_W), lambda n: (n, 0)),
        out_shape=_sds((t, Q_W + MEM_W), BF16), compiler_params=_cp("parallel"))(proj, proj, proj, proj, proj, proj, sk, kv, kv)


def _onehot_rows(vals, shape):
    rows = lax.broadcasted_iota(jnp.int32, shape, 0)
    out = jnp.zeros(shape, F32)
    for h, v in enumerate(vals):
        out = out + jnp.where(rows == h, jnp.broadcast_to(v, shape), 0.0)
    return out


def mixer_a_bwd(proj, dcat, sk, kv, name):
    t, nm = proj.shape[0], kv.shape[0]
    nb = t // WINDOW
    blk = lambda i: nb - 1 - i

    def body(*refs):
        dcat_ref, dproj_ref, dsk_ref, dkv_ref, carry_ref = refs[9:]
        i = pl.program_id(0)

        @pl.when(i == 0)
        def _():
            carry_ref[...] = jnp.zeros_like(carry_ref)
            dsk_ref[...] = jnp.zeros_like(dsk_ref)
            dkv_ref[...] = jnp.zeros_like(dkv_ref)

        mask = _swa_mask(blk(i) > 0)
        _, vjp = jax.vjp(lambda *a: _mix_a(*a, mask), *_mix_a_args(refs[:9]))
        dqs, dk_boths, dv_boths, dsinks, dqms, dkms, dvms = vjp(_cols(dcat_ref[...].astype(F32), 128))
        dkv = jnp.concatenate([_from_both_halves(dk_boths), _from_both_halves(dv_boths)], axis=1)
        dkv_cur = dkv[WINDOW:] + carry_ref[...]
        carry_ref[...] = dkv[:WINDOW]
        dproj_ref[...] = jnp.concatenate(dqs + [dkv_cur] + dqms, axis=1).astype(dproj_ref.dtype)
        dsk_ref[...] += _onehot_rows(dsinks, (16, 128))
        dkv_ref[...] += jnp.concatenate(dkms + dvms, axis=1)

    width = Q_W + 2 * KV_W + MEM_W
    return pl.pallas_call(
        body, name=name, grid=(nb,),
        in_specs=_mix_a_specs(nm, blk) + [pl.BlockSpec((WINDOW, Q_W + MEM_W), lambda i: (blk(i), 0))],
        out_specs=[pl.BlockSpec((WINDOW, width), lambda i: (blk(i), 0)), pl.BlockSpec((16, 128), lambda i: (0, 0)),
                   pl.BlockSpec((nm, 2 * MEM_W), lambda i: (0, 0))],
        out_shape=[_sds((t, width), BF16), _sds((16, 128), F32), _sds((nm, 2 * MEM_W), F32)],
        scratch_shapes=[pltpu.VMEM((WINDOW, 2 * KV_W), F32)],
        compiler_params=_cp("arbitrary"))(proj, proj, proj, proj, proj, proj, sk, kv, kv, dcat)


def _mix_b(zus, zvs, ws, bcols, lgs, lbs, qms, kms, vms, tri):
    outs = [_gmlp_group(zus[g], zvs[g], ws[g], bcols[g], lgs[g], lbs[g], tri) for g in range(B_GROUPS)]
    return outs + [_mem_pair(qms[j], kms[j], vms[j]) for j in range(MEM_HEADS // 2)]


def _mix_b_specs(nm):
    return [pl.BlockSpec((WINDOW, 2 * B_W), lambda n: (n, 0)),
            pl.BlockSpec((WINDOW, MEM_W), lambda n: (n, 2 * B_W // MEM_W)),
            pl.BlockSpec((B_GROUPS, WINDOW, WINDOW), lambda n: (0, 0, 0)),
            pl.BlockSpec((WINDOW, 128), lambda n: (0, 0)),
            pl.BlockSpec((8, 128), lambda n: (0, 0)),
            pl.BlockSpec((8, 128), lambda n: (0, 0)),
            pl.BlockSpec((nm, MEM_W), lambda n: (0, 0)),
            pl.BlockSpec((nm, MEM_W), lambda n: (0, 1))]


def _mix_b_args(refs):
    z, qm, ws, bt, lg, lb, km, vm = [r[...].astype(F32) for r in refs]
    zs = _cols(z, 128)
    return (zs[:B_GROUPS], zs[B_GROUPS:], [ws[g] for g in range(B_GROUPS)], [bt[:, g:g + 1] for g in range(B_GROUPS)],
            [lg[g:g + 1, :] for g in range(B_GROUPS)], [lb[g:g + 1, :] for g in range(B_GROUPS)],
            _cols(qm, 128), _cols(km, 128), _cols(vm, 128))


def _tri():
    return lax.broadcasted_iota(jnp.int32, (WINDOW, WINDOW), 0) >= lax.broadcasted_iota(jnp.int32, (WINDOW, WINDOW), 1)


def mixer_b_fwd(proj, ws, bt, lg, lb, kv, name):
    t, nm = proj.shape[0], kv.shape[0]

    def body(*refs):
        o_ref = refs[-1]
        o_ref[...] = jnp.concatenate(_mix_b(*_mix_b_args(refs[:-1]), _tri()), axis=1).astype(o_ref.dtype)

    return pl.pallas_call(
        body, name=name, grid=(t // WINDOW,), in_specs=_mix_b_specs(nm),
        out_specs=pl.BlockSpec((WINDOW, B_W + MEM_W), lambda n: (n, 0)),
        out_shape=_sds((t, B_W + MEM_W), BF16), compiler_params=_cp("parallel"))(proj, proj, ws, bt, lg, lb, kv, kv)


def mixer_b_bwd(proj, dcat, ws, bt, lg, lb, kv, name):
    t, nm = proj.shape[0], kv.shape[0]

    def body(*refs):
        dcat_ref, dproj_ref, dws_ref, dbt_ref, dlg_ref, dlb_ref, dkv_ref = refs[8:]

        @pl.when(pl.program_id(0) == 0)
        def _():
            for r in (dws_ref, dbt_ref, dlg_ref, dlb_ref, dkv_ref):
                r[...] = jnp.zeros_like(r)

        tri = _tri()
        _, vjp = jax.vjp(lambda *a: _mix_b(*a, tri), *_mix_b_args(refs[:8]))
        dzus, dzvs, dws, dbcols, dlgs, dlbs, dqms, dkms, dvms = vjp(_cols(dcat_ref[...].astype(F32), 128))
        dproj_ref[...] = jnp.concatenate(dzus + dzvs + dqms, axis=1).astype(dproj_ref.dtype)
        for g in range(B_GROUPS):
            dws_ref[g] += dws[g]
        lanes = lax.broadcasted_iota(jnp.int32, (WINDOW, 128), 1)
        dbt = jnp.zeros((WINDOW, 128), F32)
        for g in range(B_GROUPS):
            dbt = dbt + jnp.where(lanes == g, jnp.broadcast_to(dbcols[g], (WINDOW, 128)), 0.0)
        dbt_ref[...] += dbt
        dlg_ref[...] += _onehot_rows(dlgs, (8, 128))
        dlb_ref[...] += _onehot_rows(dlbs, (8, 128))
        dkv_ref[...] += jnp.concatenate(dkms + dvms, axis=1)

    width = 2 * B_W + MEM_W
    const2 = lambda n: (0, 0)
    return pl.pallas_call(
        body, name=name, grid=(t // WINDOW,),
        in_specs=_mix_b_specs(nm) + [pl.BlockSpec((WINDOW, B_W + MEM_W), lambda n: (n, 0))],
        out_specs=[pl.BlockSpec((WINDOW, width), lambda n: (n, 0)),
                   pl.BlockSpec((B_GROUPS, WINDOW, WINDOW), lambda n: (0, 0, 0)),
                   pl.BlockSpec((WINDOW, 128), const2), pl.BlockSpec((8, 128), const2), pl.BlockSpec((8, 128), const2),
                   pl.BlockSpec((nm, 2 * MEM_W), const2)],
        out_shape=[_sds((t, width), BF16), _sds((B_GROUPS, WINDOW, WINDOW), F32), _sds((WINDOW, 128), F32),
                   _sds((8, 128), F32), _sds((8, 128), F32), _sds((nm, 2 * MEM_W), F32)],
        compiler_params=_cp("arbitrary"))(proj, proj, ws, bt, lg, lb, kv, kv, dcat)


def _adamw_update(w, g, m, v):
    m2 = ADAM_B1 * m + (1.0 - ADAM_B1) * g
    v2 = ADAM_B2 * v + (1.0 - ADAM_B2) * jnp.square(g)
    m_hat = m2 / (1.0 - ADAM_B1 ** ADAM_STEP)
    v_hat = v2 / (1.0 - ADAM_B2 ** ADAM_STEP)
    return -ADAM_LR * (m_hat / (jnp.sqrt(v_hat) + ADAM_EPS) + ADAM_WD * w), m2, v2


def adamw(w, g, m, v, name):
    r, c = w.shape
    tr = _tile(r, (512, 352, 256, 128, 64, 32, 16, 8))

    def body(w_ref, g_ref, m_ref, v_ref, d_ref, nm_ref, nv_ref):
        d_ref[...], nm_ref[...], nv_ref[...] = _adamw_update(w_ref[...], g_ref[...], m_ref[...], v_ref[...])

    spec = pl.BlockSpec((tr, c), lambda i: (i, 0))
    return pl.pallas_call(
        body, name=name, grid=(r // tr,), in_specs=[spec] * 4, out_specs=[spec] * 3,
        out_shape=[_sds((r, c), F32)] * 3, compiler_params=_cp("parallel"))(w, g, m, v)


def adamw_halves(w, g_mine, g_theirs, m, v, c_arr, rows, name):
    r, c = w.shape
    tr = _tile(rows // 2, (256, 352, 128, 64, 32, 16, 8))
    per_half = rows // 2 // tr

    def body(c_ref, w_ref, gm_ref, gt_ref, m_ref, v_ref, g_ref, d_ref, nm_ref, nv_ref):
        g = jnp.where(pl.program_id(0) // per_half % 2 == c_ref[0], gm_ref[...], gt_ref[...])
        g_ref[...] = g
        d_ref[...], nm_ref[...], nv_ref[...] = _adamw_update(w_ref[...], g, m_ref[...], v_ref[...])

    spec = pl.BlockSpec((tr, c), lambda i, cr: (i, 0))
    half = pl.BlockSpec((tr, c), lambda i, cr: (i // (2 * per_half) * per_half + i % per_half, 0))
    return pl.pallas_call(
        body, name=name,
        grid_spec=pltpu.PrefetchScalarGridSpec(num_scalar_prefetch=1, grid=(r // tr,), in_specs=[spec, half, half, spec, spec],
                                               out_specs=[spec] * 4),
        out_shape=[_sds((r, c), F32)] * 4, compiler_params=_cp("parallel"))(c_arr, w, g_mine, g_theirs, m, v)


def _place():
    return lax.axis_index("x"), lax.axis_index("y"), lax.axis_index("c")


def _other_chips(x, y):
    return [(1 - x, y), (x, 1 - y), (1 - x, 1 - y)]


def _remote(src, dst, send_sems, recv_sems, k, dev):
    return pltpu.make_async_remote_copy(src_ref=src, dst_ref=dst, send_sem=send_sems.at[k], recv_sem=recv_sems.at[k],
                                        device_id=dev, device_id_type=MESH)


class Exchange:
    def __init__(self, ins, out_shapes, n_sems, start, finish):
        self.ins, self.out_shapes, self.n_sems, self.start, self.finish = list(ins), list(out_shapes), n_sems, start, finish

    def scratch(self):
        return [pltpu.SemaphoreType.DMA((self.n_sems,)), pltpu.SemaphoreType.DMA((self.n_sems,))]


def run_exchange(ex, name):
    ni, no = len(ex.ins), len(ex.out_shapes)

    def body(*refs):
        ex.start(refs[:ni], refs[ni:ni + no], *refs[ni + no:])
        ex.finish(refs[:ni], refs[ni:ni + no], *refs[ni + no:])

    return pl.pallas_call(
        body, name=name, in_specs=[HBM_SPEC] * ni, out_specs=[HBM_SPEC] * no, out_shape=ex.out_shapes, scratch_shapes=ex.scratch(),
        compiler_params=pltpu.CompilerParams(has_side_effects=True))(*ex.ins)


def with_rider(body, n_in, n_out, grid, ex):
    if ex is None:
        return body, [], [], [], [], []
    ni, no = len(ex.ins), len(ex.out_shapes)

    def riding(*refs):
        r_in, r_out, sems = refs[n_in:n_in + ni], refs[n_in + ni + n_out:n_in + ni + n_out + no], refs[-2:]
        first = last = None
        for axis, size in enumerate(grid):
            at_first, at_last = pl.program_id(axis) == 0, pl.program_id(axis) == size - 1
            first = at_first if first is None else jnp.logical_and(first, at_first)
            last = at_last if last is None else jnp.logical_and(last, at_last)

        @pl.when(first)
        def _():
            ex.start(r_in, r_out, *sems)

        body(*refs[:n_in], *refs[n_in + ni:n_in + ni + n_out], *refs[n_in + ni + n_out + no:-2])

        @pl.when(last)
        def _():
            ex.finish(r_in, r_out, *sems)

    return riding, ex.ins, [HBM_SPEC] * ni, ex.out_shapes, [HBM_SPEC] * no, ex.scratch()


def gather_exchange(shards):
    nw = len(shards)

    def rows(ref, cc):
        return pl.ds(cc * (ref.shape[1] // 2), ref.shape[1] // 2)

    def sent(ins, outs, send_sems, recv_sems, w, j):
        x, y, c = _place()
        return _remote(ins[w].at[:, rows(ins[w], c)], outs[w].at[:, 2 * x + y, rows(ins[w], c)], send_sems, recv_sems, 6 * w + j,
                       (*_other_chips(x, y)[j], c))

    def landed(ins, outs, send_sems, recv_sems, w, j, cc, to):
        x, y, c = _place()
        chip = _other_chips(x, y)[j]
        blk = outs[w].at[:, 2 * chip[0] + chip[1], rows(ins[w], cc)]
        return _remote(blk, blk, send_sems, recv_sems, 6 * w + (j if to is None else 3 + j), (x, y, c) if to is None else to)

    def start(ins, outs, send_sems, recv_sems):
        for j in range(3):
            for w in range(nw):
                sent(ins, outs, send_sems, recv_sems, w, j).start()

    def finish(ins, outs, send_sems, recv_sems):
        x, y, c = _place()
        for j in range(3):
            for w in range(nw):
                landed(ins, outs, send_sems, recv_sems, w, j, c, None).wait_recv()
                landed(ins, outs, send_sems, recv_sems, w, j, c, (x, y, 1 - c)).start()
        for j in range(3):
            for w in range(nw):
                landed(ins, outs, send_sems, recv_sems, w, j, 1 - c, (x, y, c)).wait_recv()
        for j in range(3):
            for w in range(nw):
                sent(ins, outs, send_sems, recv_sems, w, j).wait_send()
                landed(ins, outs, send_sems, recv_sems, w, j, c, (x, y, 1 - c)).wait_send()

    return Exchange(shards, [_sds((s.shape[0], 4) + s.shape[1:], s.dtype) for s in shards], 6 * nw, start, finish)


def copies_exchange(ins, out_shapes, n_sems, copies):
    def start(*refs):
        for cp in copies(*refs):
            cp.start()

    def finish(*refs):
        for cp in copies(*refs):
            cp.wait()

    return Exchange(ins, out_shapes, n_sems, start, finish)


def sibling_halves_exchange(gs):
    def copies(ins, outs, send_sems, recv_sems):
        x, y, c = _place()
        return [_remote(g.at[:, :, pl.ds((1 - c) * (g.shape[2] // 2), g.shape[2] // 2)], o, send_sems, recv_sems, w, (x, y, 1 - c))
                for w, (g, o) in enumerate(zip(ins, outs))]

    return copies_exchange(gs, [_sds(g.shape[:2] + (g.shape[2] // 2, g.shape[3]), g.dtype) for g in gs], len(gs), copies)


def chips_exchange(sbs):
    def copies(ins, outs, send_sems, recv_sems):
        x, y, c = _place()
        return [_remote(s.at[:, 2 * chip[0] + chip[1]], o.at[j], send_sems, recv_sems, 3 * w + j, (*chip, c))
                for j, chip in enumerate(_other_chips(x, y)) for w, (s, o) in enumerate(zip(ins, outs))]

    return copies_exchange(sbs, [_sds((3, s.shape[0]) + s.shape[2:], s.dtype) for s in sbs], 3 * len(sbs), copies)


def sibling_exchange(fs):
    def copies(ins, outs, send_sems, recv_sems):
        x, y, c = _place()
        return [_remote(f, o, send_sems, recv_sems, w, (x, y, 1 - c)) for w, (f, o) in enumerate(zip(ins, outs))]

    return copies_exchange(fs, [_sds(f.shape, f.dtype) for f in fs], len(fs), copies)


def _half_tile(a):
    return _tile(a, (256, 352, 176, 128, 64, 32, 16))


def chip_partial_sums(g, r1, c_arr, name):
    nl, _, a2, b = r1.shape
    ta = _half_tile(a2)
    per = a2 // ta

    def body(c_ref, g_ref, r_ref, o_ref):
        o_ref[...] = (g_ref[...] + r_ref[...]).astype(o_ref.dtype)

    blk = (None, None, ta, b)
    return pl.pallas_call(
        body, name=name,
        grid_spec=pltpu.PrefetchScalarGridSpec(
            num_scalar_prefetch=1, grid=(nl, 4, per),
            in_specs=[pl.BlockSpec(blk, lambda l, s, i, c: (l, s, c[0] * per + i, 0)), pl.BlockSpec(blk, lambda l, s, i, c: (l, s, i, 0))],
            out_specs=pl.BlockSpec(blk, lambda l, s, i, c: (l, s, i, 0))),
        out_shape=_sds(r1.shape, BF16), compiler_params=_cp("parallel", "parallel", "parallel"))(c_arr, g, r1)


def shard_total(g, r1, r2, cs_arr, name):
    nl, _, a2, b = r1.shape
    ta = _half_tile(a2)
    per = a2 // ta

    def body(cs_ref, g_ref, r1_ref, p0_ref, p1_ref, p2_ref, o_ref):
        o_ref[...] = (((g_ref[...] + r1_ref[...]) + p0_ref[...].astype(F32)) + p1_ref[...].astype(F32)) + p2_ref[...].astype(F32)

    blk4, blk3 = (None, None, ta, b), (None, ta, b)
    peer = lambda k: pl.BlockSpec((None, None, ta, b), lambda l, i, cs: (k, l, i, 0))
    return pl.pallas_call(
        body, name=name,
        grid_spec=pltpu.PrefetchScalarGridSpec(
            num_scalar_prefetch=1, grid=(nl, per),
            in_specs=[pl.BlockSpec(blk4, lambda l, i, cs: (l, cs[1], cs[0] * per + i, 0)),
                      pl.BlockSpec(blk4, lambda l, i, cs: (l, cs[1], i, 0)), peer(0), peer(1), peer(2)],
            out_specs=pl.BlockSpec(blk3, lambda l, i, cs: (l, i, 0))),
        out_shape=_sds((nl, a2, b), F32), compiler_params=_cp("parallel", "parallel"))(cs_arr, g, r1, r2, r2, r2)


def allgather_small(v, name):
    r, n = v.shape

    def body(x_ref, out_ref, send_sems, recv_sems, local_sem):
        x, y, c = _place()
        me, sibling = (x, y, c), (x, y, 1 - c)
        chips = _other_chips(x, y)

        def rows(px, py, pc):
            return out_ref.at[pl.ds((4 * px + 2 * py + pc) * r, r), :]

        def copy(k, block, to, src=None):
            return _remote(rows(*block) if src is None else src, rows(*block), send_sems, recv_sems, k, to)

        mine = pltpu.make_async_copy(x_ref, rows(*me), local_sem)
        mine.start()
        first = [copy(0, me, sibling, src=x_ref)] + [copy(1 + j, me, (*chip, c), src=x_ref) for j, chip in enumerate(chips)]
        for cp in first:
            cp.start()
        passed = [copy(4 + j, (*chip, c), sibling) for j, chip in enumerate(chips)]
        for j, chip in enumerate(chips):
            copy(1 + j, (*chip, c), me).wait_recv()
            passed[j].start()
        copy(0, sibling, me).wait_recv()
        for j, chip in enumerate(chips):
            copy(4 + j, (*chip, 1 - c), me).wait_recv()
        for cp in first + passed:
            cp.wait_send()
        mine.wait()

    return pl.pallas_call(
        body, name=name, in_specs=[VMEM_SPEC], out_specs=VMEM_SPEC, out_shape=_sds((8 * r, n), v.dtype),
        scratch_shapes=[pltpu.SemaphoreType.DMA((7,)), pltpu.SemaphoreType.DMA((7,)), pltpu.SemaphoreType.DMA],
        compiler_params=pltpu.CompilerParams(has_side_effects=True, vmem_limit_bytes=V7X_VMEM_LIMIT_BYTES))(v)


def sum_devices(v8, name):
    _, r, n = v8.shape
    tr = _tile(r, (88, 64, 32, 16, 8))

    def body(v_ref, o_ref):
        acc = v_ref[0]
        for d in range(1, 8):
            acc = acc + v_ref[d]
        o_ref[...] = acc

    return pl.pallas_call(
        body, name=name, grid=(r // tr,), in_specs=[pl.BlockSpec((8, tr, n), lambda i: (0, i, 0))],
        out_specs=pl.BlockSpec((tr, n), lambda i: (i, 0)), out_shape=_sds((r, n), F32), compiler_params=_cp("parallel"))(v8)


SHARDED = (("a_w_in", 2), ("a_w_out", 1), ("b_w_in", 2), ("b_w_out", 1), ("w_mem_kv", 1), ("w_gate_up", 2), ("w_down", 1))


def _full_from_gathered(wg, axis):
    l, _, a, b = wg.shape
    if axis == 1:
        return wg.reshape(l, 4 * a, b)
    return wg.transpose(0, 2, 1, 3).reshape(l, a, 4 * b)


def _by_shard(dw, axis):
    l, k, n = dw.shape
    if axis == 1:
        return dw.reshape(l, 4, k // 4, n)
    return dw.reshape(l, k, 4, n // 4).transpose(0, 2, 1, 3)


def _pack(arrs):
    parts = []
    for a in arrs:
        flat = a.reshape(-1)
        flat = jnp.pad(flat, (0, -flat.shape[0] % 1024))
        parts.append(flat.reshape(-1, 128))
    return jnp.concatenate(parts, axis=0)


def _unpack(buf, like):
    out, row = [], 0
    for a in like:
        size = 1
        for s in a.shape:
            size *= s
        rows = -(-size // 1024) * 8
        out.append(buf[row:row + rows].reshape(-1)[:size].reshape(a.shape))
        row += rows
    return out


def kernel(x, mem, mem_norm_g, mix_norm_g, ffn_norm_g, final_norm_g, a_w_in, a_sinks, a_w_out, b_w_in, b_w_s, b_bias_s, b_ln_g, b_ln_b, b_w_out, w_mem_kv, w_gate_up, w_down, loss_target, m_mem_norm_g, m_mix_norm_g, m_ffn_norm_g, m_final_norm_g, m_a_w_in, m_a_sinks, m_a_w_out, m_b_w_in, m_b_w_s, m_b_bias_s, m_b_ln_g, m_b_ln_b, m_b_w_out, m_w_mem_kv, m_w_gate_up, m_w_down, v_mem_norm_g, v_mix_norm_g, v_ffn_norm_g, v_final_norm_g, v_a_w_in, v_a_sinks, v_a_w_out, v_b_w_in, v_b_w_s, v_b_bias_s, v_b_ln_g, v_b_ln_b, v_b_w_out, v_w_mem_kv, v_w_gate_up, v_w_down):
    given = dict(locals())
    depth = mix_norm_g.shape[0]
    d = x.shape[-1]
    xi, yi, ci = _place()
    c_arr = jnp.stack([ci]).astype(jnp.int32)
    cs_arr = jnp.stack([ci, 2 * xi + yi]).astype(jnp.int32)

    axis_of = dict(SHARDED)
    own = {n: given[n].astype(BF16) for n, _ in SHARDED}

    def layer_weights(l):
        mix = "a" if l % 2 == 0 else "b"
        return [(mix + "_w_in", l // 2), (mix + "_w_out", l // 2), ("w_mem_kv", l), ("w_gate_up", l), ("w_down", l)]

    def gather_of(l):
        return gather_exchange([own[n][k:k + 1] for n, k in layer_weights(l)])

    def usable(l, gathered):
        ops = {}
        for (n, k), wg in zip(layer_weights(l), gathered):
            wg = lax.dynamic_update_slice(wg, own[n][k:k + 1, None], (0, 2 * xi + yi, 0, 0))
            ops[n[2:] if n[0] in "ab" else n] = (wg if n == "w_gate_up" else _full_from_gathered(wg, axis_of[n]), 0)
        return ops

    weights = {0: usable(0, run_exchange(gather_of(0), "gather_weights"))}

    h = x.reshape(-1, d)
    tgt = loss_target.reshape(-1, d)
    mem2 = mem.reshape(-1, d)
    row = lambda v: v.reshape(1, -1)

    mem_n = rmsnorm_fwd(mem2, row(mem_norm_g), "mem_norm")
    saved = []
    for i in range(depth):
        j = i // 2
        wl = weights[i]
        w_in, w_out = wl["w_in"], wl["w_out"]
        kv = matmul(mem_n, wl["w_mem_kv"], "nn", BF16, "mem_kv")
        if i % 2 == 0:
            sk = jnp.pad(jnp.broadcast_to(a_sinks[j][:, None], (Q_HEADS, 128)), ((0, 16 - Q_HEADS), (0, 0)))
            xn, proj = norm_matmul(h, row(mix_norm_g[i]), w_in, BF16, "a_in")
            cat = mixer_a_fwd(proj, sk, kv, "mixer_a")
            extra = (sk,)
        else:
            bt = jnp.pad(b_bias_s[j].T, ((0, 0), (0, 128 - B_GROUPS)))
            lg = jnp.pad(b_ln_g[j], ((0, 8 - B_GROUPS), (0, 0)))
            lb = jnp.pad(b_ln_b[j], ((0, 8 - B_GROUPS), (0, 0)))
            xn, proj = norm_matmul(h, row(mix_norm_g[i]), w_in, BF16, "b_in")
            cat = mixer_b_fwd(proj, b_w_s[j], bt, lg, lb, kv, "mixer_b")
            extra = (b_w_s[j], bt, lg, lb)
        h_mid = matmul(cat, w_out, "nn", F32, "mix_out", res=h)
        hn, gu, act, *gathered = gate_up_fwd(h_mid, row(ffn_norm_g[i]), *wl["w_gate_up"], "gate_up",
                                             rider=gather_of(i + 1) if i + 1 < depth else None)
        if gathered:
            weights[i + 1] = usable(i + 1, gathered)
        h_out = matmul(act, wl["w_down"], "nn", F32, "down", res=h_mid)
        saved.append((h, xn, proj, cat, h_mid, hn, gu, act, kv, extra))
        h = h_out

    loss_part, dh, d_final_g = loss_head(h, row(final_norm_g), tgt, "loss_head")
    loss = lax.psum(loss_part[0, 0], ("x", "y", "c"))

    d_mix_g, d_ffn_g = [None] * depth, [None] * depth
    d_sinks, d_ws, d_bias, d_lg, d_lb = [], [], [], [], []
    d_mem_n = jnp.zeros(mem2.shape, F32)
    totals = [None] * depth
    pending = None
    for i in reversed(range(depth)):
        h_in, xn, proj, cat, h_mid, hn, gu, act, kv, extra = saved[i]
        wl = weights[i]
        dgu, *from_sibling = down_dx_swiglu_bwd(dh, wl["w_down"], gu, "down_dx",
                                                rider=sibling_halves_exchange(pending) if pending else None)
        if pending:
            partial = [chip_partial_sums(g, r1, c_arr, "grads_chip_sum") for g, r1 in zip(pending, from_sibling)]
        dw_down = matmul(act, dh, "tn", F32, "down_dw", tm=1408, out_planes=("rows", 4))
        dw_gate_up = matmul((hn, 0), dgu, "tn", F32, "gate_up_dw", tn=1408, out_planes=("cols", 4))
        dh, d_ffn_g[i], *from_chips = dx_norm_bwd(dgu, wl["w_gate_up"], h_mid, row(ffn_norm_g[i]), dh, "gate_up_dx",
                                                  rider=chips_exchange(partial) if pending else None)
        if pending:
            totals[i + 1] = [shard_total(g, r1, r2, cs_arr, "grads_shard_total") for g, r1, r2 in zip(pending, from_sibling, from_chips)]
        dcat = matmul(dh, wl["w_out"], "nt", F32, "mix_out_dx")
        dw_out = matmul(cat, dh, "tn", F32, "mix_out_dw", out_planes=("rows", 4))
        if i % 2 == 0:
            dproj, dsk, dkv = mixer_a_bwd(proj, dcat, extra[0], kv, "mixer_a_bwd")
            d_sinks.insert(0, dsk[:Q_HEADS, 0])
            dw_in = matmul(xn, dproj, "tn", F32, "a_in_dw")
        else:
            dproj, dws, dbt, dlg, dlb, dkv = mixer_b_bwd(proj, dcat, *extra, kv, "mixer_b_bwd")
            d_ws.insert(0, dws)
            d_bias.insert(0, dbt[:, :B_GROUPS].T)
            d_lg.insert(0, dlg[:B_GROUPS])
            d_lb.insert(0, dlb[:B_GROUPS])
            dw_in = matmul(xn, dproj, "tn", F32, "b_in_dw")
        dw_kv = matmul(mem_n, dkv, "tn", F32, "mem_kv_dw", out_planes=("rows", 4))
        d_mem_n = matmul(dkv, wl["w_mem_kv"], "nt", F32, "mem_kv_dx", res=d_mem_n)
        dh, d_mix_g[i] = dx_norm_bwd(dproj, wl["w_in"], h_in, row(mix_norm_g[i]), dh, "in_dx")
        pending = [_by_shard(dw_in[None], 2), dw_out[None], dw_kv[None], dw_gate_up[None], dw_down[None]]
    grad_x = dh.reshape(x.shape)
    _, d_mem_g = rmsnorm_bwd(mem2, row(mem_norm_g), d_mem_n, jnp.zeros(mem2.shape, F32), "mem_norm_bwd")

    from_sibling = run_exchange(sibling_halves_exchange(pending), "grads_sibling_swap")
    partial = [chip_partial_sums(g, r1, c_arr, "grads_chip_sum") for g, r1 in zip(pending, from_sibling)]
    from_chips = run_exchange(chips_exchange(partial), "grads_chips_exchange")
    totals[0] = [shard_total(g, r1, r2, cs_arr, "grads_shard_total") for g, r1, r2 in zip(pending, from_sibling, from_chips)]

    mine = {n: [None] * given[n].shape[0] for n, _ in SHARDED}
    for l in range(depth):
        for (n, k), tot in zip(layer_weights(l), totals[l]):
            mine[n][k] = tot
    mine = [jnp.concatenate(mine[n], axis=0) for n, _ in SHARDED]
    theirs = run_exchange(sibling_exchange(mine), "grads_sibling_totals")
    out = {}
    for (n, _), g_mine, g_theirs in zip(SHARDED, mine, theirs):
        shape = given[n].shape
        two_d = lambda a: a.reshape(-1, shape[-1])
        res = adamw_halves(two_d(given[n]), two_d(g_mine), two_d(g_theirs), two_d(given["m_" + n]), two_d(given["v_" + n]),
                           c_arr, shape[1], "adamw")
        out[n] = tuple(r.reshape(shape) for r in res)

    small = ("mem_norm_g", "mix_norm_g", "ffn_norm_g", "final_norm_g", "a_sinks", "b_w_s", "b_bias_s", "b_ln_g", "b_ln_b")
    small_g = [d_mem_g[0], jnp.concatenate(d_mix_g, axis=0), jnp.concatenate(d_ffn_g, axis=0), d_final_g[0],
               jnp.stack(d_sinks), jnp.stack(d_ws), jnp.stack(d_bias), jnp.stack(d_lg), jnp.stack(d_lb)]
    packed = _pack(small_g)
    g_small = sum_devices(allgather_small(packed, "small_allgather").reshape(8, *packed.shape), "small_sum")
    like = [given[n] for n in small]
    delta_s, new_m_s, new_v_s = adamw(_pack(like), g_small, _pack([given["m_" + n] for n in small]),
                                      _pack([given["v_" + n] for n in small]), "adamw_small")
    for n, g, dl, nm_, nv_ in zip(small, _unpack(g_small, like), _unpack(delta_s, like), _unpack(new_m_s, like), _unpack(new_v_s, like)):
        out[n] = (g, dl, nm_, nv_)

    order = ("mem_norm_g", "mix_norm_g", "ffn_norm_g", "final_norm_g", "a_w_in", "a_sinks", "a_w_out", "b_w_in", "b_w_s",
             "b_bias_s", "b_ln_g", "b_ln_b", "b_w_out", "w_mem_kv", "w_gate_up", "w_down")
    return (loss, grad_x, *[out[n][0] for n in order], *[out[n][1] for n in order],
            *[out[n][2] for n in order], *[out[n][3] for n in order])
```

```python
import jax
import jax.numpy as jnp
from jax import lax
from jax.experimental import pallas as pl
from jax.experimental.pallas import tpu as pltpu

F32, BF16 = jnp.float32, jnp.bfloat16
EPS = 1e-6
HEAD_DIM = 64
Q_HEADS, KV_HEADS, GROUP = 12, 2, 6
WINDOW = 128
MEM_HEADS = 4
B_GROUPS = 6
Q_W, KV_W, MEM_W, B_W = 768, 128, 256, 768
SCALE = HEAD_DIM ** -0.5
NEG = -1e30
ADAM_LR, ADAM_B1, ADAM_B2, ADAM_EPS, ADAM_WD, ADAM_STEP = 0.001, 0.9, 0.999, 1e-08, 0.01, 10
V7X_VMEM_LIMIT_BYTES = 48 * 1024 * 1024
MESH = pl.DeviceIdType.MESH
HBM_SPEC = pl.BlockSpec(memory_space=pltpu.HBM)
VMEM_SPEC = pl.BlockSpec(memory_space=pltpu.VMEM)


def _cp(*sem):
    return pltpu.CompilerParams(dimension_semantics=sem or None, vmem_limit_bytes=V7X_VMEM_LIMIT_BYTES)


def _tile(n, cands):
    for t in cands:
        if n % t == 0:
            return t
    return n


def _sds(shape, dtype):
    return jax.ShapeDtypeStruct(tuple(shape), dtype)


def _dot(a, b, ca, cb):
    return lax.dot_general(a, b, (((ca,), (cb,)), ((), ())), preferred_element_type=F32)


def _rms(x, g):
    return x * lax.rsqrt(jnp.mean(x * x, axis=-1, keepdims=True) + EPS) * g


def rmsnorm_fwd(h, g, name):
    t, d = h.shape
    tm = _tile(t, (512, 256, 128))

    def body(h_ref, g_ref, o_ref):
        o_ref[...] = _rms(h_ref[...], g_ref[...]).astype(o_ref.dtype)

    return pl.pallas_call(
        body, name=name, grid=(t // tm,),
        in_specs=[pl.BlockSpec((tm, d), lambda i: (i, 0)), pl.BlockSpec((1, d), lambda i: (0, 0))],
        out_specs=pl.BlockSpec((tm, d), lambda i: (i, 0)),
        out_shape=_sds((t, d), BF16), compiler_params=_cp("parallel"))(h, g)


def rmsnorm_bwd(h, g, dxn, dres, name):
    t, d = h.shape
    tm = _tile(t, (512, 256, 128))

    def body(h_ref, g_ref, dxn_ref, dres_ref, dh_ref, dg_ref):
        _, vjp = jax.vjp(_rms, h_ref[...], g_ref[...])
        dh, dg = vjp(dxn_ref[...].astype(F32))
        dh_ref[...] = dres_ref[...] + dh

        @pl.when(pl.program_id(0) == 0)
        def _():
            dg_ref[...] = jnp.zeros_like(dg_ref)

        dg_ref[...] += dg

    row = pl.BlockSpec((tm, d), lambda i: (i, 0))
    vec = pl.BlockSpec((1, d), lambda i: (0, 0))
    return pl.pallas_call(
        body, name=name, grid=(t // tm,), in_specs=[row, vec, row, row], out_specs=[row, vec],
        out_shape=[_sds((t, d), F32), _sds((1, d), F32)], compiler_params=_cp("arbitrary"))(h, g, dxn, dres)


def loss_head(h, g, tgt, name):
    t, d = h.shape
    tm = _tile(t, (512, 256, 128))

    def body(h_ref, g_ref, t_ref, l_ref, dh_ref, dg_ref):
        y, vjp = jax.vjp(_rms, h_ref[...], g_ref[...])
        err = y - t_ref[...]
        dh, dg = vjp(err * (1.0 / d))
        dh_ref[...] = dh
        part = 0.5 * jnp.sum(jnp.mean(err * err, axis=-1, keepdims=True), axis=0, keepdims=True)

        @pl.when(pl.program_id(0) == 0)
        def _():
            dg_ref[...] = jnp.zeros_like(dg_ref)
            l_ref[...] = jnp.zeros_like(l_ref)

        dg_ref[...] += dg
        l_ref[...] += part

    row = pl.BlockSpec((tm, d), lambda i: (i, 0))
    vec = pl.BlockSpec((1, d), lambda i: (0, 0))
    one = pl.BlockSpec((1, 1), lambda i: (0, 0))
    return pl.pallas_call(
        body, name=name, grid=(t // tm,), in_specs=[row, vec, row], out_specs=[one, row, vec],
        out_shape=[_sds((1, 1), F32), _sds((t, d), F32), _sds((1, d), F32)], compiler_params=_cp("arbitrary"))(h, g, tgt)


def _logical(op):
    arr, lead = op if isinstance(op, tuple) else (op, None)
    planes = arr.shape[-3] if arr.ndim - (lead is not None) == 3 else 1
    return arr, lead, arr.shape[-2], arr.shape[-1], planes


def _spec(op, rows_t, cols_t, row_of, col_of):
    arr, lead, _, cols, _ = _logical(op)
    per = cols // cols_t
    lead = () if lead is None else (lead,)
    if arr.ndim - len(lead) == 2:
        return pl.BlockSpec((None,) * len(lead) + (rows_t, cols_t), lambda *g: lead + (row_of(*g), col_of(*g)))
    return pl.BlockSpec((None,) * len(lead) + (None, rows_t, cols_t),
                        lambda *g: lead + (col_of(*g) // per, row_of(*g), col_of(*g) % per))


def _arr(op):
    return op[0] if isinstance(op, tuple) else op


def norm_matmul(h, g, w, out_dtype, name):
    t, d = h.shape
    wa, layer = w
    n = wa.shape[-1]
    tm = _tile(t, (512, 256, 128))

    def body(h_ref, g_ref, w_ref, xn_ref, o_ref):
        xn = _rms(h_ref[...], g_ref[...]).astype(BF16)
        xn_ref[...] = xn
        o_ref[...] = _dot(xn, w_ref[...], 1, 0).astype(o_ref.dtype)

    return pl.pallas_call(
        body, name=name, grid=(t // tm,),
        in_specs=[pl.BlockSpec((tm, d), lambda i: (i, 0)), pl.BlockSpec((1, d), lambda i: (0, 0)),
                  pl.BlockSpec((None, d, n), lambda i: (layer, 0, 0))],
        out_specs=[pl.BlockSpec((tm, d), lambda i: (i, 0)), pl.BlockSpec((tm, n), lambda i: (i, 0))],
        out_shape=[_sds((t, d), BF16), _sds((t, n), out_dtype)], compiler_params=_cp("parallel"))(h, g, wa)


def dx_norm_bwd(dy, w, h, g, dres, name, rider=None):
    t, d = h.shape
    dy_arr, dy_lead, _, kc, kp = _logical(dy)
    w_arr, w_lead, _, wc, wp = _logical(w)
    assert kc * kp == wc * wp and dy_lead is None, name
    chunk = min(kc, wc)
    tm = _tile(t, (512, 256, 128))

    def piece(ref, planes, cols, q):
        off = q * chunk % cols
        return ref[q * chunk // cols, :, off:off + chunk] if planes > 1 else ref[:, off:off + chunk]

    def body(dy_ref, w_ref, h_ref, g_ref, dres_ref, dh_ref, dg_ref):
        dxn = None
        for q in range(kc * kp // chunk):
            p = _dot(piece(dy_ref, kp, kc, q).astype(BF16), piece(w_ref, wp, wc, q), 1, 1)
            dxn = p if dxn is None else dxn + p
        _, vjp = jax.vjp(_rms, h_ref[...], g_ref[...])
        dh, dg = vjp(dxn)
        dh_ref[...] = dres_ref[...] + dh

        @pl.when(pl.program_id(0) == 0)
        def _():
            dg_ref[...] = jnp.zeros_like(dg_ref)

        dg_ref[...] += dg

    w_lead = () if w_lead is None else (w_lead,)
    w_block = ((wp,) if wp > 1 else ()) + (d, wc)
    w_spec = pl.BlockSpec((None,) * len(w_lead) + w_block, lambda i: w_lead + (0,) * len(w_block), pipeline_mode=pl.Buffered(1))
    dy_spec = pl.BlockSpec((kp, tm, kc), lambda i: (0, i, 0)) if kp > 1 else pl.BlockSpec((tm, kc), lambda i: (i, 0))
    row = pl.BlockSpec((tm, d), lambda i: (i, 0))
    vec = pl.BlockSpec((1, d), lambda i: (0, 0))
    grid = (t // tm,)
    body, r_ops, r_in, r_shapes, r_out, r_scratch = with_rider(body, 5, 2, grid, rider)
    return pl.pallas_call(
        body, name=name, grid=grid, in_specs=[dy_spec, w_spec, row, vec, row] + r_in,
        out_specs=[row, vec] + r_out, out_shape=[_sds((t, d), F32), _sds((1, d), F32)] + r_shapes, scratch_shapes=r_scratch,
        compiler_params=_cp("arbitrary"))(dy_arr, w_arr, h, g, dres, *r_ops)


def matmul(a, b, mode, out_dtype, name, res=None, tm=None, tn=1792, tk=2816, out_planes=None, out_into=None):
    _, _, ar, ac, ap = _logical(a)
    _, _, br, bc, bp = _logical(b)
    if mode == "nn":
        m, ka, kb, n = ar, ac * ap, br, bc * bp
        n_plane, ka_plane, kb_plane = bc, ac, br
    elif mode == "nt":
        m, ka, n, kb = ar, ac * ap, br, bc * bp
        n_plane, ka_plane, kb_plane = br, ac, bc
    else:
        ka, m, kb, n = ar, ac * ap, br, bc * bp
        n_plane, ka_plane, kb_plane = bc, ar, br
    m_plane = ac if mode == "tn" else ar
    assert ka == kb, name
    k = ka
    kind, planes = out_planes or ("cols", 1)
    if kind == "cols":
        n_plane = min(n_plane, n // planes)
    tm = _tile(m_plane, ((1024, 1408, 512, 256, 128) if mode == "tn" else (512, 256, 128)) if tm is None else (tm, 1024, 512, 256, 128))
    if kind == "rows" and tm % (m // planes):
        tm = m_plane
    tn = _tile(n_plane, (tn, 1792, 1408, 1280, 1024, 896, 640, 512, 256, 128))
    tk = _tile(min(ka_plane, kb_plane), (tk, 2816, 1792, 1408, 1280, 1024, 512, 256, 128))
    nk = k // tk
    row_i, col_j, red = (lambda i, j, kk: i), (lambda i, j, kk: j), (lambda i, j, kk: kk)
    if mode == "nn":
        a_spec, b_spec, ca, cb = _spec(a, tm, tk, row_i, red), _spec(b, tk, tn, red, col_j), 1, 0
    elif mode == "nt":
        a_spec, b_spec, ca, cb = _spec(a, tm, tk, row_i, red), _spec(b, tn, tk, col_j, red), 1, 1
    else:
        a_spec, b_spec, ca, cb = _spec(a, tk, tm, red, row_i), _spec(b, tk, tn, red, col_j), 0, 0
    lead = () if out_into is None else (out_into[1],)
    if planes == 1:
        o_shape, o_block = (m, n), (tm, tn)
        o_index = lambda i, j, kk: lead + (i, j)
    elif kind == "cols":
        per = n // planes // tn
        o_shape, o_block = (planes, m, n // planes), (None, tm, tn)
        o_index = lambda i, j, kk: lead + (j // per, i, j % per)
    else:
        o_shape, o_block = (planes, m // planes, n), (tm // (m // planes), m // planes, tn)
        o_index = lambda i, j, kk: lead + (i, 0, j)
    o_spec = pl.BlockSpec((None,) * len(lead) + o_block, o_index)
    if out_into is not None:
        assert out_into[0].shape[1:] == o_shape and out_into[0].dtype == out_dtype, name
        o_shape = out_into[0].shape
    has_res = res is not None
    n_in = 2 + has_res + (out_into is not None)

    def body(*refs):
        a_ref, b_ref = refs[:2]
        rest = refs[2:2 + has_res] + refs[n_in:]
        o_ref = rest[1] if has_res else rest[0]
        p = _dot(a_ref[...].astype(BF16), b_ref[...].astype(BF16), ca, cb)
        if nk == 1:
            if has_res:
                p = p + rest[0][...]
            o_ref[...] = p.astype(o_ref.dtype).reshape(o_ref.shape)
        else:
            acc_ref = rest[-1]
            kk = pl.program_id(2)

            @pl.when(kk == 0)
            def _():
                acc_ref[...] = p

            @pl.when(kk > 0)
            def _():
                acc_ref[...] += p

            @pl.when(kk == nk - 1)
            def _():
                r = acc_ref[...]
                if has_res:
                    r = r + rest[0][...]
                o_ref[...] = r.astype(o_ref.dtype).reshape(o_ref.shape)

    operands = [_arr(a), _arr(b)] + ([res] if has_res else []) + ([out_into[0]] if out_into is not None else [])
    return pl.pallas_call(
        body, name=name, grid=(m // tm, n // tn, nk),
        in_specs=[a_spec, b_spec] + ([pl.BlockSpec((tm, tn), lambda i, j, kk: (i, j))] if has_res else [])
        + ([pl.BlockSpec(memory_space=pl.ANY)] if out_into is not None else []),
        out_specs=o_spec, out_shape=_sds(o_shape, out_dtype),
        input_output_aliases={n_in - 1: 0} if out_into is not None else {},
        scratch_shapes=[pltpu.VMEM((tm, tn), F32)] if nk > 1 else [],
        compiler_params=_cp("parallel", "parallel", "arbitrary"))(*operands)


def _swiglu(gate, up):
    return gate / (1.0 + jnp.exp(-gate)) * up


def gate_up_fwd(h, g, w, layer, name, rider=None):
    t, d = h.shape
    half = w.shape[-1]
    tm = _tile(t, (512, 256, 128))

    def body(h_ref, g_ref, wg_ref, wu_ref, hn_ref, gu_ref, act_ref):
        a = _rms(h_ref[...], g_ref[...]).astype(BF16)
        hn_ref[...] = a
        gate, up = _dot(a, wg_ref[...], 1, 0), _dot(a, wu_ref[...], 1, 0)
        gu_ref[0] = gate.astype(gu_ref.dtype)
        gu_ref[1] = up.astype(gu_ref.dtype)
        act_ref[...] = _swiglu(gate, up).astype(act_ref.dtype)

    grid = (2, t // tm)
    body, r_ops, r_in, r_shapes, r_out, r_scratch = with_rider(body, 4, 3, grid, rider)
    return pl.pallas_call(
        body, name=name, grid=grid,
        in_specs=[pl.BlockSpec((tm, d), lambda j, i: (i, 0)), pl.BlockSpec((1, d), lambda j, i: (0, 0)),
                  pl.BlockSpec((None, None, d, half), lambda j, i: (layer, j, 0, 0)),
                  pl.BlockSpec((None, None, d, half), lambda j, i: (layer, 2 + j, 0, 0))] + r_in,
        out_specs=[pl.BlockSpec((None, tm, d), lambda j, i: (j, i, 0)), pl.BlockSpec((2, tm, half), lambda j, i: (0, i, j)),
                   pl.BlockSpec((tm, half), lambda j, i: (i, j))] + r_out,
        out_shape=[_sds((2, t, d), BF16), _sds((2, t, 2 * half), BF16), _sds((t, 2 * half), BF16)] + r_shapes,
        scratch_shapes=r_scratch, compiler_params=_cp("arbitrary", "arbitrary"))(h, g, w, w, *r_ops)


def down_dx_swiglu_bwd(dh, wd, gu, name, rider=None):
    t, d = dh.shape
    w, layer = wd
    f = w.shape[-2]
    tm = _tile(t, (512, 256, 128))
    tn = _tile(f, (1408, 512, 256, 128))

    def body(dh_ref, w_ref, gu_ref, o_ref):
        dact = _dot(dh_ref[...].astype(BF16), w_ref[...], 1, 1)
        gate, up = gu_ref[0].astype(F32), gu_ref[1].astype(F32)
        sig = 1.0 / (1.0 + jnp.exp(-gate))
        silu = gate * sig
        o_ref[0] = (dact * up * (sig + silu * (1.0 - sig))).astype(o_ref.dtype)
        o_ref[1] = (dact * silu).astype(o_ref.dtype)

    planes = pl.BlockSpec((2, tm, tn), lambda j, i: (0, i, j))
    grid = (f // tn, t // tm)
    body, r_ops, r_in, r_shapes, r_out, r_scratch = with_rider(body, 3, 1, grid, rider)
    return pl.pallas_call(
        body, name=name, grid=grid,
        in_specs=[pl.BlockSpec((tm, d), lambda j, i: (i, 0)), pl.BlockSpec((None, tn, d), lambda j, i: (layer, j, 0)), planes] + r_in,
        out_specs=[planes] + r_out, out_shape=[_sds((2, t, f), BF16)] + r_shapes, scratch_shapes=r_scratch,
        compiler_params=_cp("arbitrary", "arbitrary"))(dh, w, gu, *r_ops)


def _softmax_over_keys(s, sink=None):
    m = s.max(axis=0, keepdims=True)
    if sink is not None:
        m = jnp.maximum(m, sink)
    m = lax.stop_gradient(m)
    e = jnp.exp(s - m)
    den = e.sum(axis=0, keepdims=True)
    if sink is not None:
        den = den + jnp.exp(sink - m)
    return e * (1.0 / den)


def _low_lanes():
    return lax.broadcasted_iota(jnp.int32, (1, 128), 1) < HEAD_DIM


def _stack_heads(slabs):
    low = _low_lanes()
    return jnp.concatenate([p for s in slabs for p in (jnp.where(low, s, 0.0), jnp.where(low, 0.0, s))], axis=0)


def _unstack_heads(o, n_slabs):
    low = _low_lanes()
    return [jnp.where(low, o[2 * j * WINDOW:(2 * j + 1) * WINDOW], o[(2 * j + 1) * WINDOW:(2 * j + 2) * WINDOW])
            for j in range(n_slabs)]


def _swa_group(q_slabs, k_both, v_both, sinks, mask):
    qs = _stack_heads(q_slabs).astype(BF16)
    s = jnp.where(mask, _dot(k_both.astype(BF16), qs, 1, 1) * SCALE, NEG)
    sink = jnp.concatenate([jnp.broadcast_to(v, (1, WINDOW)) for v in sinks], axis=1)
    return _unstack_heads(_dot(_softmax_over_keys(s, sink).astype(BF16), v_both.astype(BF16), 0, 0), len(q_slabs))


def _mem_pair(q_slab, k_slab, v_slab):
    s = _dot(k_slab.astype(BF16), _stack_heads([q_slab]).astype(BF16), 1, 1) * SCALE
    return _unstack_heads(_dot(_softmax_over_keys(s).astype(BF16), v_slab.astype(BF16), 0, 0), 1)[0]


def _gelu(x):
    return 0.5 * x * (1.0 + jnp.tanh(0.7978845608028654 * (x + 0.044715 * (x * x * x))))


def _gmlp_group(zu, zv, w, bcol, lg, lb, tri):
    u, v = _gelu(zu), _gelu(zv)
    mu = jnp.mean(v, axis=-1, keepdims=True)
    var = jnp.mean(jnp.square(v - mu), axis=-1, keepdims=True)
    vn = (v - mu) * lax.rsqrt(var + EPS) * lg + lb
    sv = _dot(jnp.where(tri, w, 0.0).astype(BF16), vn.astype(BF16), 1, 0) + bcol
    return u * sv


def _cols(x, width):
    return [x[:, j * width:(j + 1) * width] for j in range(x.shape[1] // width)]


def _swa_mask(has_prev):
    qi = lax.broadcasted_iota(jnp.int32, (2 * WINDOW, GROUP * WINDOW), 1) & (WINDOW - 1)
    kj = lax.broadcasted_iota(jnp.int32, (2 * WINDOW, GROUP * WINDOW), 0)
    in_prev = jnp.logical_and(jnp.logical_and(kj < WINDOW, kj > qi), has_prev)
    return jnp.logical_or(in_prev, jnp.logical_and(kj >= WINDOW, kj - WINDOW <= qi))


def _mix_a(q_slabs, k_boths, v_boths, sinks, qm_slabs, km_slabs, vm_slabs, mask):
    per = GROUP // 2
    outs = []
    for g in range(KV_HEADS):
        outs += _swa_group(q_slabs[per * g:per * (g + 1)], k_boths[g], v_boths[g], sinks[GROUP * g:GROUP * (g + 1)], mask)
    return outs + [_mem_pair(qm_slabs[j], km_slabs[j], vm_slabs[j]) for j in range(MEM_HEADS // 2)]


def _in_both_halves(prev, cur):
    cat = jnp.concatenate([prev, cur], axis=0)
    rolled = pltpu.roll(cat, HEAD_DIM, axis=1)
    low = _low_lanes()
    return [jnp.where(low, cat, rolled), jnp.where(low, rolled, cat)]


def _from_both_halves(d_boths):
    t = [d + pltpu.roll(d, HEAD_DIM, axis=1) for d in d_boths]
    return jnp.where(_low_lanes(), t[0], t[1])


def _mix_a_specs(nm, blk):
    prev = lambda n: jnp.maximum(blk(n) - 1, 0)
    return [pl.BlockSpec((WINDOW, Q_W), lambda n: (blk(n), 0)),
            pl.BlockSpec((WINDOW, KV_W), lambda n: (prev(n), Q_W // KV_W)),
            pl.BlockSpec((WINDOW, KV_W), lambda n: (blk(n), Q_W // KV_W)),
            pl.BlockSpec((WINDOW, KV_W), lambda n: (prev(n), Q_W // KV_W + 1)),
            pl.BlockSpec((WINDOW, KV_W), lambda n: (blk(n), Q_W // KV_W + 1)),
            pl.BlockSpec((WINDOW, MEM_W), lambda n: (blk(n), (Q_W + 2 * KV_W) // MEM_W)),
            pl.BlockSpec((16, 128), lambda n: (0, 0)),
            pl.BlockSpec((nm, MEM_W), lambda n: (0, 0)),
            pl.BlockSpec((nm, MEM_W), lambda n: (0, 1))]


def _mix_a_args(refs):
    q, kp, kc, vp, vc, qm, sk, km, vm = [r[...].astype(F32) for r in refs]
    return (_cols(q, 128), _in_both_halves(kp, kc), _in_both_halves(vp, vc), [sk[h:h + 1, 0:1] for h in range(Q_HEADS)],
            _cols(qm, 128), _cols(km, 128), _cols(vm, 128))


def mixer_a_fwd(proj, sk, kv, name, rider=None):
    t, nm = proj.shape[0], kv.shape[0]

    def body(*refs):
        o_ref = refs[-1]
        slabs = _mix_a(*_mix_a_args(refs[:-1]), _swa_mask(pl.program_id(0) > 0))
        o_ref[...] = jnp.concatenate(slabs, axis=1).astype(o_ref.dtype)

    grid = (t // WINDOW,)
    body, r_ops, r_in, r_shapes, r_out, r_scratch = with_rider(body, 9, 1, grid, rider)
    return pl.pallas_call(
        body, name=name, grid=grid, in_specs=_mix_a_specs(nm, lambda n: n) + r_in,
        out_specs=[pl.BlockSpec((WINDOW, Q_W + MEM_W), lambda n: (n, 0))] + r_out,
        out_shape=[_sds((t, Q_W + MEM_W), BF16)] + r_shapes, scratch_shapes=r_scratch,
        compiler_params=_cp("arbitrary"))(proj, proj, proj, proj, proj, proj, sk, kv, kv, *r_ops)


def _onehot_rows(vals, shape):
    rows = lax.broadcasted_iota(jnp.int32, shape, 0)
    out = jnp.zeros(shape, F32)
    for h, v in enumerate(vals):
        out = out + jnp.where(rows == h, jnp.broadcast_to(v, shape), 0.0)
    return out


def mixer_a_bwd(proj, dcat, sk, kv, name, rider=None):
    t, nm = proj.shape[0], kv.shape[0]
    nb = t // WINDOW
    blk = lambda i: nb - 1 - i

    def body(*refs):
        dcat_ref, dproj_ref, dsk_ref, dkv_ref, carry_ref = refs[9:]
        i = pl.program_id(0)

        @pl.when(i == 0)
        def _():
            carry_ref[...] = jnp.zeros_like(carry_ref)
            dsk_ref[...] = jnp.zeros_like(dsk_ref)
            dkv_ref[...] = jnp.zeros_like(dkv_ref)

        mask = _swa_mask(blk(i) > 0)
        _, vjp = jax.vjp(lambda *a: _mix_a(*a, mask), *_mix_a_args(refs[:9]))
        dqs, dk_boths, dv_boths, dsinks, dqms, dkms, dvms = vjp(_cols(dcat_ref[...].astype(F32), 128))
        dkv = jnp.concatenate([_from_both_halves(dk_boths), _from_both_halves(dv_boths)], axis=1)
        dkv_cur = dkv[WINDOW:] + carry_ref[...]
        carry_ref[...] = dkv[:WINDOW]
        dproj_ref[...] = jnp.concatenate(dqs + [dkv_cur] + dqms, axis=1).astype(dproj_ref.dtype)
        dsk_ref[...] += _onehot_rows(dsinks, (16, 128))
        dkv_ref[...] += jnp.concatenate(dkms + dvms, axis=1)

    width = Q_W + 2 * KV_W + MEM_W
    body, r_ops, r_in, r_shapes, r_out, r_scratch = with_rider(body, 10, 3, (nb,), rider)
    return pl.pallas_call(
        body, name=name, grid=(nb,),
        in_specs=_mix_a_specs(nm, blk) + [pl.BlockSpec((WINDOW, Q_W + MEM_W), lambda i: (blk(i), 0))] + r_in,
        out_specs=[pl.BlockSpec((WINDOW, width), lambda i: (blk(i), 0)), pl.BlockSpec((16, 128), lambda i: (0, 0)),
                   pl.BlockSpec((nm, 2 * MEM_W), lambda i: (0, 0))] + r_out,
        out_shape=[_sds((t, width), BF16), _sds((16, 128), F32), _sds((nm, 2 * MEM_W), F32)] + r_shapes,
        scratch_shapes=[pltpu.VMEM((WINDOW, 2 * KV_W), F32)] + r_scratch,
        compiler_params=_cp("arbitrary"))(proj, proj, proj, proj, proj, proj, sk, kv, kv, dcat, *r_ops)


def _mix_b(zus, zvs, ws, bcols, lgs, lbs, qms, kms, vms, tri):
    outs = [_gmlp_group(zus[g], zvs[g], ws[g], bcols[g], lgs[g], lbs[g], tri) for g in range(B_GROUPS)]
    return outs + [_mem_pair(qms[j], kms[j], vms[j]) for j in range(MEM_HEADS // 2)]


def _mix_b_specs(nm):
    return [pl.BlockSpec((WINDOW, 2 * B_W), lambda n: (n, 0)),
            pl.BlockSpec((WINDOW, MEM_W), lambda n: (n, 2 * B_W // MEM_W)),
            pl.BlockSpec((B_GROUPS, WINDOW, WINDOW), lambda n: (0, 0, 0)),
            pl.BlockSpec((WINDOW, 128), lambda n: (0, 0)),
            pl.BlockSpec((8, 128), lambda n: (0, 0)),
            pl.BlockSpec((8, 128), lambda n: (0, 0)),
            pl.BlockSpec((nm, MEM_W), lambda n: (0, 0)),
            pl.BlockSpec((nm, MEM_W), lambda n: (0, 1))]


def _mix_b_args(refs):
    z, qm, ws, bt, lg, lb, km, vm = [r[...].astype(F32) for r in refs]
    zs = _cols(z, 128)
    return (zs[:B_GROUPS], zs[B_GROUPS:], [ws[g] for g in range(B_GROUPS)], [bt[:, g:g + 1] for g in range(B_GROUPS)],
            [lg[g:g + 1, :] for g in range(B_GROUPS)], [lb[g:g + 1, :] for g in range(B_GROUPS)],
            _cols(qm, 128), _cols(km, 128), _cols(vm, 128))


def _tri():
    return lax.broadcasted_iota(jnp.int32, (WINDOW, WINDOW), 0) >= lax.broadcasted_iota(jnp.int32, (WINDOW, WINDOW), 1)


def mixer_b_fwd(proj, ws, bt, lg, lb, kv, name):
    t, nm = proj.shape[0], kv.shape[0]

    def body(*refs):
        o_ref = refs[-1]
        o_ref[...] = jnp.concatenate(_mix_b(*_mix_b_args(refs[:-1]), _tri()), axis=1).astype(o_ref.dtype)

    return pl.pallas_call(
        body, name=name, grid=(t // WINDOW,), in_specs=_mix_b_specs(nm),
        out_specs=pl.BlockSpec((WINDOW, B_W + MEM_W), lambda n: (n, 0)),
        out_shape=_sds((t, B_W + MEM_W), BF16), compiler_params=_cp("parallel"))(proj, proj, ws, bt, lg, lb, kv, kv)


def mixer_b_bwd(proj, dcat, ws, bt, lg, lb, kv, name):
    t, nm = proj.shape[0], kv.shape[0]

    def body(*refs):
        dcat_ref, dproj_ref, dws_ref, dbt_ref, dlg_ref, dlb_ref, dkv_ref = refs[8:]

        @pl.when(pl.program_id(0) == 0)
        def _():
            for r in (dws_ref, dbt_ref, dlg_ref, dlb_ref, dkv_ref):
                r[...] = jnp.zeros_like(r)

        tri = _tri()
        _, vjp = jax.vjp(lambda *a: _mix_b(*a, tri), *_mix_b_args(refs[:8]))
        dzus, dzvs, dws, dbcols, dlgs, dlbs, dqms, dkms, dvms = vjp(_cols(dcat_ref[...].astype(F32), 128))
        dproj_ref[...] = jnp.concatenate(dzus + dzvs + dqms, axis=1).astype(dproj_ref.dtype)
        for g in range(B_GROUPS):
            dws_ref[g] += dws[g]
        lanes = lax.broadcasted_iota(jnp.int32, (WINDOW, 128), 1)
        dbt = jnp.zeros((WINDOW, 128), F32)
        for g in range(B_GROUPS):
            dbt = dbt + jnp.where(lanes == g, jnp.broadcast_to(dbcols[g], (WINDOW, 128)), 0.0)
        dbt_ref[...] += dbt
        dlg_ref[...] += _onehot_rows(dlgs, (8, 128))
        dlb_ref[...] += _onehot_rows(dlbs, (8, 128))
        dkv_ref[...] += jnp.concatenate(dkms + dvms, axis=1)

    width = 2 * B_W + MEM_W
    const2 = lambda n: (0, 0)
    return pl.pallas_call(
        body, name=name, grid=(t // WINDOW,),
        in_specs=_mix_b_specs(nm) + [pl.BlockSpec((WINDOW, B_W + MEM_W), lambda n: (n, 0))],
        out_specs=[pl.BlockSpec((WINDOW, width), lambda n: (n, 0)),
                   pl.BlockSpec((B_GROUPS, WINDOW, WINDOW), lambda n: (0, 0, 0)),
                   pl.BlockSpec((WINDOW, 128), const2), pl.BlockSpec((8, 128), const2), pl.BlockSpec((8, 128), const2),
                   pl.BlockSpec((nm, 2 * MEM_W), const2)],
        out_shape=[_sds((t, width), BF16), _sds((B_GROUPS, WINDOW, WINDOW), F32), _sds((WINDOW, 128), F32),
                   _sds((8, 128), F32), _sds((8, 128), F32), _sds((nm, 2 * MEM_W), F32)],
        compiler_params=_cp("arbitrary"))(proj, proj, ws, bt, lg, lb, kv, kv, dcat)


def _adamw_update(w, g, m, v):
    m2 = ADAM_B1 * m + (1.0 - ADAM_B1) * g
    v2 = ADAM_B2 * v + (1.0 - ADAM_B2) * jnp.square(g)
    m_hat = m2 / (1.0 - ADAM_B1 ** ADAM_STEP)
    v_hat = v2 / (1.0 - ADAM_B2 ** ADAM_STEP)
    return -ADAM_LR * (m_hat / (jnp.sqrt(v_hat) + ADAM_EPS) + ADAM_WD * w), m2, v2


def adamw(w, g, m, v, name):
    r, c = w.shape
    tr = _tile(r, (512, 352, 256, 128, 64, 32, 16, 8))

    def body(w_ref, g_ref, m_ref, v_ref, d_ref, nm_ref, nv_ref):
        d_ref[...], nm_ref[...], nv_ref[...] = _adamw_update(w_ref[...], g_ref[...], m_ref[...], v_ref[...])

    spec = pl.BlockSpec((tr, c), lambda i: (i, 0))
    return pl.pallas_call(
        body, name=name, grid=(r // tr,), in_specs=[spec] * 4, out_specs=[spec] * 3,
        out_shape=[_sds((r, c), F32)] * 3, compiler_params=_cp("parallel"))(w, g, m, v)


def adamw_halves(w, g_mine, g_theirs, m, v, c_arr, rows, name):
    r, c = w.shape
    tr = _tile(rows // 2, (256, 352, 128, 64, 32, 16, 8))
    per_half = rows // 2 // tr

    def body(c_ref, w_ref, gm_ref, gt_ref, m_ref, v_ref, g_ref, d_ref, nm_ref, nv_ref):
        g = jnp.where(pl.program_id(0) // per_half % 2 == c_ref[0], gm_ref[...], gt_ref[...])
        g_ref[...] = g
        d_ref[...], nm_ref[...], nv_ref[...] = _adamw_update(w_ref[...], g, m_ref[...], v_ref[...])

    spec = pl.BlockSpec((tr, c), lambda i, cr: (i, 0))
    half = pl.BlockSpec((tr, c), lambda i, cr: (i // (2 * per_half) * per_half + i % per_half, 0))
    return pl.pallas_call(
        body, name=name,
        grid_spec=pltpu.PrefetchScalarGridSpec(num_scalar_prefetch=1, grid=(r // tr,), in_specs=[spec, half, half, spec, spec],
                                               out_specs=[spec] * 4),
        out_shape=[_sds((r, c), F32)] * 4, compiler_params=_cp("parallel"))(c_arr, w, g_mine, g_theirs, m, v)


def _place():
    return lax.axis_index("x"), lax.axis_index("y"), lax.axis_index("c")


def _other_chips(x, y):
    return [(1 - x, y), (x, 1 - y), (1 - x, 1 - y)]


def _remote(src, dst, send_sems, recv_sems, k, dev):
    return pltpu.make_async_remote_copy(src_ref=src, dst_ref=dst, send_sem=send_sems.at[k], recv_sem=recv_sems.at[k],
                                        device_id=dev, device_id_type=MESH)


class Exchange:
    def __init__(self, ins, out_shapes, n_sems, start, finish):
        self.ins, self.out_shapes, self.start, self.finish = list(ins), list(out_shapes), start, finish
        self.sems = [n_sems, n_sems] if isinstance(n_sems, int) else list(n_sems)

    def scratch(self):
        return [pltpu.SemaphoreType.DMA((n,)) for n in self.sems]


def both_exchanges(a, b):
    ni, no, ns = len(a.ins), len(a.out_shapes), len(a.sems)

    def start(ins, outs, *sems):
        a.start(ins[:ni], outs[:no], *sems[:ns])
        b.start(ins[ni:], outs[no:], *sems[ns:])

    def finish(ins, outs, *sems):
        a.finish(ins[:ni], outs[:no], *sems[:ns])
        b.finish(ins[ni:], outs[no:], *sems[ns:])

    return Exchange(a.ins + b.ins, a.out_shapes + b.out_shapes, a.sems + b.sems, start, finish)


def run_exchange(ex, name):
    ni, no = len(ex.ins), len(ex.out_shapes)

    def body(*refs):
        ex.start(refs[:ni], refs[ni:ni + no], *refs[ni + no:])
        ex.finish(refs[:ni], refs[ni:ni + no], *refs[ni + no:])

    return pl.pallas_call(
        body, name=name, in_specs=[HBM_SPEC] * ni, out_specs=[HBM_SPEC] * no, out_shape=ex.out_shapes, scratch_shapes=ex.scratch(),
        compiler_params=pltpu.CompilerParams(has_side_effects=True))(*ex.ins)


def with_rider(body, n_in, n_out, grid, ex):
    if ex is None:
        return body, [], [], [], [], []
    ni, no, ns = len(ex.ins), len(ex.out_shapes), len(ex.sems)

    def riding(*refs):
        r_in, r_out, sems = refs[n_in:n_in + ni], refs[n_in + ni + n_out:n_in + ni + n_out + no], refs[-ns:]
        first = last = None
        for axis, size in enumerate(grid):
            at_first, at_last = pl.program_id(axis) == 0, pl.program_id(axis) == size - 1
            first = at_first if first is None else jnp.logical_and(first, at_first)
            last = at_last if last is None else jnp.logical_and(last, at_last)

        @pl.when(first)
        def _():
            ex.start(r_in, r_out, *sems)

        body(*refs[:n_in], *refs[n_in + ni:n_in + ni + n_out], *refs[n_in + ni + n_out + no:-ns])

        @pl.when(last)
        def _():
            ex.finish(r_in, r_out, *sems)

    return riding, ex.ins, [HBM_SPEC] * ni, ex.out_shapes, [HBM_SPEC] * no, ex.scratch()


def gather_exchange(shards):
    nw = len(shards)
    entry = [k for _, k in shards]

    def rows(ref, cc):
        return pl.ds(cc * (ref.shape[1] // 2), ref.shape[1] // 2)

    def sent(ins, outs, send_sems, recv_sems, w, j):
        x, y, c = _place()
        return _remote(ins[w].at[pl.ds(entry[w], 1), rows(ins[w], c)], outs[w].at[:, 2 * x + y, rows(ins[w], c)], send_sems, recv_sems,
                       7 * w + j, (*_other_chips(x, y)[j], c))

    def landed(ins, outs, send_sems, recv_sems, w, j, cc, to):
        x, y, c = _place()
        chip = _other_chips(x, y)[j]
        blk = outs[w].at[:, 2 * chip[0] + chip[1], rows(ins[w], cc)]
        return _remote(blk, blk, send_sems, recv_sems, 7 * w + (j if to is None else 3 + j), (x, y, c) if to is None else to)

    def own(ins, outs, send_sems, recv_sems, w):
        x, y, c = _place()
        return _remote(ins[w].at[pl.ds(entry[w], 1)], outs[w].at[:, 2 * x + y], send_sems, recv_sems, 7 * w + 6, (x, y, 1 - c))

    def start(ins, outs, send_sems, recv_sems):
        for j in range(3):
            for w in range(nw):
                sent(ins, outs, send_sems, recv_sems, w, j).start()
        for w in range(nw):
            own(ins, outs, send_sems, recv_sems, w).start()

    def finish(ins, outs, send_sems, recv_sems):
        x, y, c = _place()
        for j in range(3):
            for w in range(nw):
                landed(ins, outs, send_sems, recv_sems, w, j, c, None).wait_recv()
                landed(ins, outs, send_sems, recv_sems, w, j, c, (x, y, 1 - c)).start()
        for w in range(nw):
            own(ins, outs, send_sems, recv_sems, w).wait()
        for j in range(3):
            for w in range(nw):
                landed(ins, outs, send_sems, recv_sems, w, j, 1 - c, (x, y, c)).wait_recv()
        for j in range(3):
            for w in range(nw):
                sent(ins, outs, send_sems, recv_sems, w, j).wait_send()
                landed(ins, outs, send_sems, recv_sems, w, j, c, (x, y, 1 - c)).wait_send()

    return Exchange([s for s, _ in shards], [_sds((1, 4) + s.shape[1:], s.dtype) for s, _ in shards], 7 * nw, start, finish)


def copies_exchange(ins, out_shapes, n_sems, copies):
    def start(*refs):
        for cp in copies(*refs):
            cp.start()

    def finish(*refs):
        for cp in copies(*refs):
            cp.wait()

    return Exchange(ins, out_shapes, n_sems, start, finish)


def sibling_halves_exchange(gs):
    def copies(ins, outs, send_sems, recv_sems):
        x, y, c = _place()
        return [_remote(g.at[:, :, pl.ds((1 - c) * (g.shape[2] // 2), g.shape[2] // 2)], o, send_sems, recv_sems, w, (x, y, 1 - c))
                for w, (g, o) in enumerate(zip(ins, outs))]

    return copies_exchange(gs, [_sds(g.shape[:2] + (g.shape[2] // 2, g.shape[3]), g.dtype) for g in gs], len(gs), copies)


def chips_exchange(sbs):
    def copies(ins, outs, send_sems, recv_sems):
        x, y, c = _place()
        return [_remote(s.at[:, 2 * chip[0] + chip[1]], o.at[j], send_sems, recv_sems, 3 * w + j, (*chip, c))
                for j, chip in enumerate(_other_chips(x, y)) for w, (s, o) in enumerate(zip(ins, outs))]

    return copies_exchange(sbs, [_sds((3, s.shape[0]) + s.shape[2:], s.dtype) for s in sbs], 3 * len(sbs), copies)


def sibling_exchange(fs):
    def copies(ins, outs, send_sems, recv_sems):
        x, y, c = _place()
        return [_remote(f, o, send_sems, recv_sems, w, (x, y, 1 - c)) for w, (f, o) in enumerate(zip(ins, outs))]

    return copies_exchange(fs, [_sds(f.shape, f.dtype) for f in fs], len(fs), copies)


def _half_tile(a):
    return _tile(a, (256, 352, 176, 128, 64, 32, 16))


def chip_partial_sums(g, r1, c_arr, name):
    nl, _, a2, b = r1.shape
    ta = _half_tile(a2)
    per = a2 // ta

    def body(c_ref, g_ref, r_ref, o_ref):
        o_ref[...] = (g_ref[...] + r_ref[...]).astype(o_ref.dtype)

    blk = (None, None, ta, b)
    return pl.pallas_call(
        body, name=name,
        grid_spec=pltpu.PrefetchScalarGridSpec(
            num_scalar_prefetch=1, grid=(nl, 4, per),
            in_specs=[pl.BlockSpec(blk, lambda l, s, i, c: (l, s, c[0] * per + i, 0)), pl.BlockSpec(blk, lambda l, s, i, c: (l, s, i, 0))],
            out_specs=pl.BlockSpec(blk, lambda l, s, i, c: (l, s, i, 0))),
        out_shape=_sds(r1.shape, BF16), compiler_params=_cp("parallel", "parallel", "parallel"))(c_arr, g, r1)


def shard_total(g, r1, r2, cs_arr, name):
    nl, _, a2, b = r1.shape
    ta = _half_tile(a2)
    per = a2 // ta

    def body(cs_ref, g_ref, r1_ref, p0_ref, p1_ref, p2_ref, o_ref):
        o_ref[...] = (((g_ref[...] + r1_ref[...]) + p0_ref[...].astype(F32)) + p1_ref[...].astype(F32)) + p2_ref[...].astype(F32)

    blk4, blk3 = (None, None, ta, b), (None, ta, b)
    peer = lambda k: pl.BlockSpec((None, None, ta, b), lambda l, i, cs: (k, l, i, 0))
    return pl.pallas_call(
        body, name=name,
        grid_spec=pltpu.PrefetchScalarGridSpec(
            num_scalar_prefetch=1, grid=(nl, per),
            in_specs=[pl.BlockSpec(blk4, lambda l, i, cs: (l, cs[1], cs[0] * per + i, 0)),
                      pl.BlockSpec(blk4, lambda l, i, cs: (l, cs[1], i, 0)), peer(0), peer(1), peer(2)],
            out_specs=pl.BlockSpec(blk3, lambda l, i, cs: (l, i, 0))),
        out_shape=_sds((nl, a2, b), F32), compiler_params=_cp("parallel", "parallel"))(cs_arr, g, r1, r2, r2, r2)


def allgather_small(v, name):
    r, n = v.shape

    def body(x_ref, out_ref, send_sems, recv_sems, local_sem):
        x, y, c = _place()
        me, sibling = (x, y, c), (x, y, 1 - c)
        chips = _other_chips(x, y)

        def rows(px, py, pc):
            return out_ref.at[pl.ds((4 * px + 2 * py + pc) * r, r), :]

        def copy(k, block, to, src=None):
            return _remote(rows(*block) if src is None else src, rows(*block), send_sems, recv_sems, k, to)

        mine = pltpu.make_async_copy(x_ref, rows(*me), local_sem)
        mine.start()
        first = [copy(0, me, sibling, src=x_ref)] + [copy(1 + j, me, (*chip, c), src=x_ref) for j, chip in enumerate(chips)]
        for cp in first:
            cp.start()
        passed = [copy(4 + j, (*chip, c), sibling) for j, chip in enumerate(chips)]
        for j, chip in enumerate(chips):
            copy(1 + j, (*chip, c), me).wait_recv()
            passed[j].start()
        copy(0, sibling, me).wait_recv()
        for j, chip in enumerate(chips):
            copy(4 + j, (*chip, 1 - c), me).wait_recv()
        for cp in first + passed:
            cp.wait_send()
        mine.wait()

    return pl.pallas_call(
        body, name=name, in_specs=[VMEM_SPEC], out_specs=VMEM_SPEC, out_shape=_sds((8 * r, n), v.dtype),
        scratch_shapes=[pltpu.SemaphoreType.DMA((7,)), pltpu.SemaphoreType.DMA((7,)), pltpu.SemaphoreType.DMA],
        compiler_params=pltpu.CompilerParams(has_side_effects=True, vmem_limit_bytes=V7X_VMEM_LIMIT_BYTES))(v)


def sum_devices(v8, name):
    _, r, n = v8.shape
    tr = _tile(r, (88, 64, 32, 16, 8))

    def body(v_ref, o_ref):
        acc = v_ref[0]
        for d in range(1, 8):
            acc = acc + v_ref[d]
        o_ref[...] = acc

    return pl.pallas_call(
        body, name=name, grid=(r // tr,), in_specs=[pl.BlockSpec((8, tr, n), lambda i: (0, i, 0))],
        out_specs=pl.BlockSpec((tr, n), lambda i: (i, 0)), out_shape=_sds((r, n), F32), compiler_params=_cp("parallel"))(v8)


SHARDED = (("a_w_in", 2), ("a_w_out", 1), ("b_w_in", 2), ("b_w_out", 1), ("w_mem_kv", 1), ("w_gate_up", 2), ("w_down", 1))


def _full_from_gathered(wg, axis):
    l, _, a, b = wg.shape
    if axis == 1:
        return wg.reshape(l, 4 * a, b)
    return wg.transpose(0, 2, 1, 3).reshape(l, a, 4 * b)


def _by_shard(dw, axis):
    l, k, n = dw.shape
    if axis == 1:
        return dw.reshape(l, 4, k // 4, n)
    return dw.reshape(l, k, 4, n // 4).transpose(0, 2, 1, 3)


def _pack(arrs):
    parts = []
    for a in arrs:
        flat = a.reshape(-1)
        flat = jnp.pad(flat, (0, -flat.shape[0] % 1024))
        parts.append(flat.reshape(-1, 128))
    return jnp.concatenate(parts, axis=0)


def _unpack(buf, like):
    out, row = [], 0
    for a in like:
        size = 1
        for s in a.shape:
            size *= s
        rows = -(-size // 1024) * 8
        out.append(buf[row:row + rows].reshape(-1)[:size].reshape(a.shape))
        row += rows
    return out


def kernel(x, mem, mem_norm_g, mix_norm_g, ffn_norm_g, final_norm_g, a_w_in, a_sinks, a_w_out, b_w_in, b_w_s, b_bias_s, b_ln_g, b_ln_b, b_w_out, w_mem_kv, w_gate_up, w_down, loss_target, m_mem_norm_g, m_mix_norm_g, m_ffn_norm_g, m_final_norm_g, m_a_w_in, m_a_sinks, m_a_w_out, m_b_w_in, m_b_w_s, m_b_bias_s, m_b_ln_g, m_b_ln_b, m_b_w_out, m_w_mem_kv, m_w_gate_up, m_w_down, v_mem_norm_g, v_mix_norm_g, v_ffn_norm_g, v_final_norm_g, v_a_w_in, v_a_sinks, v_a_w_out, v_b_w_in, v_b_w_s, v_b_bias_s, v_b_ln_g, v_b_ln_b, v_b_w_out, v_w_mem_kv, v_w_gate_up, v_w_down):
    given = dict(locals())
    depth = mix_norm_g.shape[0]
    d = x.shape[-1]
    xi, yi, ci = _place()
    c_arr = jnp.stack([ci]).astype(jnp.int32)
    cs_arr = jnp.stack([ci, 2 * xi + yi]).astype(jnp.int32)

    axis_of = dict(SHARDED)
    own = {n: given[n].astype(BF16) for n, _ in SHARDED}
    MIXER, FFN = slice(0, 3), slice(3, 5)

    def layer_weights(l):
        mix = "a" if l % 2 == 0 else "b"
        return [(mix + "_w_in", l // 2), (mix + "_w_out", l // 2), ("w_mem_kv", l), ("w_gate_up", l), ("w_down", l)]

    def gather_of(l, part=slice(0, 5)):
        return gather_exchange([(own[n], k) for n, k in layer_weights(l)[part]])

    def usable(l, gathered, part=slice(0, 5)):
        return {n[2:] if n[0] in "ab" else n: (wg if n == "w_gate_up" else _full_from_gathered(wg, axis_of[n]), 0)
                for (n, _), wg in zip(layer_weights(l)[part], gathered)}

    weights = {0: usable(0, run_exchange(gather_of(0, MIXER), "gather_weights"), MIXER)}

    h = x.reshape(-1, d)
    tgt = loss_target.reshape(-1, d)
    mem2 = mem.reshape(-1, d)
    row = lambda v: v.reshape(1, -1)

    mem_n = rmsnorm_fwd(mem2, row(mem_norm_g), "mem_norm")
    saved = []
    for i in range(depth):
        j = i // 2
        wl = weights[i]
        w_in, w_out = wl["w_in"], wl["w_out"]
        kv = matmul(mem_n, wl["w_mem_kv"], "nn", BF16, "mem_kv")
        if i % 2 == 0:
            sk = jnp.pad(jnp.broadcast_to(a_sinks[j][:, None], (Q_HEADS, 128)), ((0, 16 - Q_HEADS), (0, 0)))
            xn, proj = norm_matmul(h, row(mix_norm_g[i]), w_in, BF16, "a_in")
            cat, *gathered = mixer_a_fwd(proj, sk, kv, "mixer_a", rider=gather_of(0, FFN) if i == 0 else None)
            if gathered:
                wl.update(usable(0, gathered, FFN))
            extra = (sk,)
        else:
            bt = jnp.pad(b_bias_s[j].T, ((0, 0), (0, 128 - B_GROUPS)))
            lg = jnp.pad(b_ln_g[j], ((0, 8 - B_GROUPS), (0, 0)))
            lb = jnp.pad(b_ln_b[j], ((0, 8 - B_GROUPS), (0, 0)))
            xn, proj = norm_matmul(h, row(mix_norm_g[i]), w_in, BF16, "b_in")
            cat = mixer_b_fwd(proj, b_w_s[j], bt, lg, lb, kv, "mixer_b")
            extra = (b_w_s[j], bt, lg, lb)
        h_mid = matmul(cat, w_out, "nn", F32, "mix_out", res=h)
        hn, gu, act, *gathered = gate_up_fwd(h_mid, row(ffn_norm_g[i]), *wl["w_gate_up"], "gate_up",
                                             rider=gather_of(i + 1) if i + 1 < depth else None)
        if gathered:
            weights[i + 1] = usable(i + 1, gathered)
        h_out = matmul(act, wl["w_down"], "nn", F32, "down", res=h_mid)
        saved.append((h, xn, proj, cat, h_mid, hn, gu, act, kv, extra))
        h = h_out

    loss_part, dh, d_final_g = loss_head(h, row(final_norm_g), tgt, "loss_head")
    loss = lax.psum(loss_part[0, 0], ("x", "y", "c"))

    d_mix_g, d_ffn_g = [None] * depth, [None] * depth
    d_sinks, d_ws, d_bias, d_lg, d_lb = [], [], [], [], []
    d_mem_n = jnp.zeros(mem2.shape, F32)
    totals = [None] * depth
    pending = None
    for i in reversed(range(depth)):
        h_in, xn, proj, cat, h_mid, hn, gu, act, kv, extra = saved[i]
        wl = weights[i]
        dgu, *from_sibling = down_dx_swiglu_bwd(dh, wl["w_down"], gu, "down_dx",
                                                rider=sibling_halves_exchange(pending) if pending else None)
        if pending:
            partial = [chip_partial_sums(g, r1, c_arr, "grads_chip_sum") for g, r1 in zip(pending, from_sibling)]
        dw_down = matmul(act, dh, "tn", F32, "down_dw", tm=1408, out_planes=("rows", 4))
        dw_gate_up = matmul((hn, 0), dgu, "tn", F32, "gate_up_dw", tn=1408, out_planes=("cols", 4))
        ffn = [dw_gate_up[None], dw_down[None]] if i == 0 else []
        riders = ([chips_exchange(partial)] if pending else []) + ([sibling_halves_exchange(ffn)] if ffn else [])
        rider = None if not riders else riders[0] if len(riders) == 1 else both_exchanges(*riders)
        dh, d_ffn_g[i], *landed = dx_norm_bwd(dgu, wl["w_gate_up"], h_mid, row(ffn_norm_g[i]), dh, "gate_up_dx", rider=rider)
        if pending:
            from_chips = landed[:len(pending)]
            totals[i + 1] = [shard_total(g, r1, r2, cs_arr, "grads_shard_total") for g, r1, r2 in zip(pending, from_sibling, from_chips)]
        if ffn:
            ffn_sibling = landed[-len(ffn):]
            ffn_partial = [chip_partial_sums(g, r1, c_arr, "grads_chip_sum") for g, r1 in zip(ffn, ffn_sibling)]
        dcat = matmul(dh, wl["w_out"], "nt", F32, "mix_out_dx")
        dw_out = matmul(cat, dh, "tn", F32, "mix_out_dw", out_planes=("rows", 4))
        if i % 2 == 0:
            dproj, dsk, dkv, *ffn_chips = mixer_a_bwd(proj, dcat, extra[0], kv, "mixer_a_bwd",
                                                      rider=chips_exchange(ffn_partial) if ffn else None)
            if ffn:
                ffn_totals = [shard_total(g, r1, r2, cs_arr, "grads_shard_total") for g, r1, r2 in zip(ffn, ffn_sibling, ffn_chips)]
            d_sinks.insert(0, dsk[:Q_HEADS, 0])
            dw_in = matmul(xn, dproj, "tn", F32, "a_in_dw")
        else:
            dproj, dws, dbt, dlg, dlb, dkv = mixer_b_bwd(proj, dcat, *extra, kv, "mixer_b_bwd")
            d_ws.insert(0, dws)
            d_bias.insert(0, dbt[:, :B_GROUPS].T)
            d_lg.insert(0, dlg[:B_GROUPS])
            d_lb.insert(0, dlb[:B_GROUPS])
            dw_in = matmul(xn, dproj, "tn", F32, "b_in_dw")
        dw_kv = matmul(mem_n, dkv, "tn", F32, "mem_kv_dw", out_planes=("rows", 4))
        d_mem_n = matmul(dkv, wl["w_mem_kv"], "nt", F32, "mem_kv_dx", res=d_mem_n)
        dh, d_mix_g[i] = dx_norm_bwd(dproj, wl["w_in"], h_in, row(mix_norm_g[i]), dh, "in_dx")
        pending = [_by_shard(dw_in[None], 2), dw_out[None], dw_kv[None]] + ([] if ffn else [dw_gate_up[None], dw_down[None]])
    grad_x = dh.reshape(x.shape)
    _, d_mem_g = rmsnorm_bwd(mem2, row(mem_norm_g), d_mem_n, jnp.zeros(mem2.shape, F32), "mem_norm_bwd")

    from_sibling = run_exchange(sibling_halves_exchange(pending), "grads_sibling_swap")
    partial = [chip_partial_sums(g, r1, c_arr, "grads_chip_sum") for g, r1 in zip(pending, from_sibling)]
    from_chips = run_exchange(chips_exchange(partial), "grads_chips_exchange")
    totals[0] = [shard_total(g, r1, r2, cs_arr, "grads_shard_total") for g, r1, r2 in zip(pending, from_sibling, from_chips)] + ffn_totals

    mine = {n: [None] * given[n].shape[0] for n, _ in SHARDED}
    for l in range(depth):
        for (n, k), tot in zip(layer_weights(l), totals[l]):
            mine[n][k] = tot
    mine = [jnp.concatenate(mine[n], axis=0) for n, _ in SHARDED]
    theirs = run_exchange(sibling_exchange(mine), "grads_sibling_totals")
    out = {}
    for (n, _), g_mine, g_theirs in zip(SHARDED, mine, theirs):
        shape = given[n].shape
        two_d = lambda a: a.reshape(-1, shape[-1])
        res = adamw_halves(two_d(given[n]), two_d(g_mine), two_d(g_theirs), two_d(given["m_" + n]), two_d(given["v_" + n]),
                           c_arr, shape[1], "adamw")
        out[n] = tuple(r.reshape(shape) for r in res)

    small = ("mem_norm_g", "mix_norm_g", "ffn_norm_g", "final_norm_g", "a_sinks", "b_w_s", "b_bias_s", "b_ln_g", "b_ln_b")
    small_g = [d_mem_g[0], jnp.concatenate(d_mix_g, axis=0), jnp.concatenate(d_ffn_g, axis=0), d_final_g[0],
               jnp.stack(d_sinks), jnp.stack(d_ws), jnp.stack(d_bias), jnp.stack(d_lg), jnp.stack(d_lb)]
    packed = _pack(small_g)
    g_small = sum_devices(allgather_small(packed, "small_allgather").reshape(8, *packed.shape), "small_sum")
    like = [given[n] for n in small]
    delta_s, new_m_s, new_v_s = adamw(_pack(like), g_small, _pack([given["m_" + n] for n in small]),
                                      _pack([given["v_" + n] for n in small]), "adamw_small")
    for n, g, dl, nm_, nv_ in zip(small, _unpack(g_small, like), _unpack(delta_s, like), _unpack(new_m_s, like), _unpack(new_v_s, like)):
        out[n] = (g, dl, nm_, nv_)

    order = ("mem_norm_g", "mix_norm_g", "ffn_norm_g", "final_norm_g", "a_w_in", "a_sinks", "a_w_out", "b_w_in", "b_w_s",
             "b_bias_s", "b_ln_g", "b_ln_b", "b_w_out", "w_mem_kv", "w_gate_up", "w_down")
    return (loss, grad_x, *[out[n][0] for n in order], *[out[n][1] for n in order],
            *[out[n][2] for n in order], *[out[n][3] for n in order])
```

```python
import jax
import jax.numpy as jnp
from jax import lax
from jax.experimental import pallas as pl
from jax.experimental.pallas import tpu as pltpu

F32, BF16 = jnp.float32, jnp.bfloat16
EPS = 1e-6
HEAD_DIM = 64
Q_HEADS, KV_HEADS, GROUP = 12, 2, 6
WINDOW = 128
MEM_HEADS = 4
B_GROUPS = 6
Q_W, KV_W, MEM_W, B_W = 768, 128, 256, 768
SCALE = HEAD_DIM ** -0.5
NEG = -1e30
ADAM_LR, ADAM_B1, ADAM_B2, ADAM_EPS, ADAM_WD, ADAM_STEP = 0.001, 0.9, 0.999, 1e-08, 0.01, 10
V7X_VMEM_LIMIT_BYTES = 48 * 1024 * 1024
MESH = pl.DeviceIdType.MESH
HBM_SPEC = pl.BlockSpec(memory_space=pltpu.HBM)
VMEM_SPEC = pl.BlockSpec(memory_space=pltpu.VMEM)


def _cp(*sem):
    return pltpu.CompilerParams(dimension_semantics=sem or None, vmem_limit_bytes=V7X_VMEM_LIMIT_BYTES)


def _tile(n, cands):
    for t in cands:
        if n % t == 0:
            return t
    return n


def _sds(shape, dtype):
    return jax.ShapeDtypeStruct(tuple(shape), dtype)


def _dot(a, b, ca, cb):
    return lax.dot_general(a, b, (((ca,), (cb,)), ((), ())), preferred_element_type=F32)


def _rms(x, g):
    return x * lax.rsqrt(jnp.mean(x * x, axis=-1, keepdims=True) + EPS) * g


def rmsnorm_fwd(h, g, name):
    t, d = h.shape
    tm = _tile(t, (512, 256, 128))

    def body(h_ref, g_ref, o_ref):
        o_ref[...] = _rms(h_ref[...], g_ref[...]).astype(o_ref.dtype)

    return pl.pallas_call(
        body, name=name, grid=(t // tm,),
        in_specs=[pl.BlockSpec((tm, d), lambda i: (i, 0)), pl.BlockSpec((1, d), lambda i: (0, 0))],
        out_specs=pl.BlockSpec((tm, d), lambda i: (i, 0)),
        out_shape=_sds((t, d), BF16), compiler_params=_cp("parallel"))(h, g)


def rmsnorm_bwd(h, g, dxn, dres, name):
    t, d = h.shape
    tm = _tile(t, (512, 256, 128))

    def body(h_ref, g_ref, dxn_ref, dres_ref, dh_ref, dg_ref):
        _, vjp = jax.vjp(_rms, h_ref[...], g_ref[...])
        dh, dg = vjp(dxn_ref[...].astype(F32))
        dh_ref[...] = dres_ref[...] + dh

        @pl.when(pl.program_id(0) == 0)
        def _():
            dg_ref[...] = jnp.zeros_like(dg_ref)

        dg_ref[...] += dg

    row = pl.BlockSpec((tm, d), lambda i: (i, 0))
    vec = pl.BlockSpec((1, d), lambda i: (0, 0))
    return pl.pallas_call(
        body, name=name, grid=(t // tm,), in_specs=[row, vec, row, row], out_specs=[row, vec],
        out_shape=[_sds((t, d), F32), _sds((1, d), F32)], compiler_params=_cp("arbitrary"))(h, g, dxn, dres)


def loss_head(h, g, tgt, name):
    t, d = h.shape
    tm = _tile(t, (512, 256, 128))

    def body(h_ref, g_ref, t_ref, l_ref, dh_ref, dg_ref):
        y, vjp = jax.vjp(_rms, h_ref[...], g_ref[...])
        err = y - t_ref[...]
        dh, dg = vjp(err * (1.0 / d))
        dh_ref[...] = dh
        part = 0.5 * jnp.sum(jnp.mean(err * err, axis=-1, keepdims=True), axis=0, keepdims=True)

        @pl.when(pl.program_id(0) == 0)
        def _():
            dg_ref[...] = jnp.zeros_like(dg_ref)
            l_ref[...] = jnp.zeros_like(l_ref)

        dg_ref[...] += dg
        l_ref[...] += part

    row = pl.BlockSpec((tm, d), lambda i: (i, 0))
    vec = pl.BlockSpec((1, d), lambda i: (0, 0))
    one = pl.BlockSpec((1, 1), lambda i: (0, 0))
    return pl.pallas_call(
        body, name=name, grid=(t // tm,), in_specs=[row, vec, row], out_specs=[one, row, vec],
        out_shape=[_sds((1, 1), F32), _sds((t, d), F32), _sds((1, d), F32)], compiler_params=_cp("arbitrary"))(h, g, tgt)


def _logical(op):
    arr, lead = op if isinstance(op, tuple) else (op, None)
    planes = arr.shape[-3] if arr.ndim - (lead is not None) == 3 else 1
    return arr, lead, arr.shape[-2], arr.shape[-1], planes


def _spec(op, rows_t, cols_t, row_of, col_of):
    arr, lead, _, cols, _ = _logical(op)
    per = cols // cols_t
    lead = () if lead is None else (lead,)
    if arr.ndim - len(lead) == 2:
        return pl.BlockSpec((None,) * len(lead) + (rows_t, cols_t), lambda *g: lead + (row_of(*g), col_of(*g)))
    return pl.BlockSpec((None,) * len(lead) + (None, rows_t, cols_t),
                        lambda *g: lead + (col_of(*g) // per, row_of(*g), col_of(*g) % per))


def _arr(op):
    return op[0] if isinstance(op, tuple) else op


def norm_matmul(h, g, w, out_dtype, name):
    t, d = h.shape
    wa, layer = w
    n = wa.shape[-1]
    tm = _tile(t, (512, 256, 128))

    def body(h_ref, g_ref, w_ref, xn_ref, o_ref):
        xn = _rms(h_ref[...], g_ref[...]).astype(BF16)
        xn_ref[...] = xn
        o_ref[...] = _dot(xn, w_ref[...], 1, 0).astype(o_ref.dtype)

    return pl.pallas_call(
        body, name=name, grid=(t // tm,),
        in_specs=[pl.BlockSpec((tm, d), lambda i: (i, 0)), pl.BlockSpec((1, d), lambda i: (0, 0)),
                  pl.BlockSpec((None, d, n), lambda i: (layer, 0, 0))],
        out_specs=[pl.BlockSpec((tm, d), lambda i: (i, 0)), pl.BlockSpec((tm, n), lambda i: (i, 0))],
        out_shape=[_sds((t, d), BF16), _sds((t, n), out_dtype)], compiler_params=_cp("parallel"))(h, g, wa)


def dx_norm_bwd(dy, w, h, g, dres, name, rider=None):
    t, d = h.shape
    dy_arr, dy_lead, _, kc, kp = _logical(dy)
    w_arr, w_lead, _, wc, wp = _logical(w)
    assert kc * kp == wc * wp and dy_lead is None, name
    chunk = min(kc, wc)
    tm = _tile(t, (512, 256, 128))

    def piece(ref, planes, cols, q):
        off = q * chunk % cols
        return ref[q * chunk // cols, :, off:off + chunk] if planes > 1 else ref[:, off:off + chunk]

    def body(dy_ref, w_ref, h_ref, g_ref, dres_ref, dh_ref, dg_ref):
        dxn = None
        for q in range(kc * kp // chunk):
            p = _dot(piece(dy_ref, kp, kc, q).astype(BF16), piece(w_ref, wp, wc, q), 1, 1)
            dxn = p if dxn is None else dxn + p
        _, vjp = jax.vjp(_rms, h_ref[...], g_ref[...])
        dh, dg = vjp(dxn)
        dh_ref[...] = dres_ref[...] + dh

        @pl.when(pl.program_id(0) == 0)
        def _():
            dg_ref[...] = jnp.zeros_like(dg_ref)

        dg_ref[...] += dg

    w_lead = () if w_lead is None else (w_lead,)
    w_block = ((wp,) if wp > 1 else ()) + (d, wc)
    w_spec = pl.BlockSpec((None,) * len(w_lead) + w_block, lambda i: w_lead + (0,) * len(w_block), pipeline_mode=pl.Buffered(1))
    dy_spec = pl.BlockSpec((kp, tm, kc), lambda i: (0, i, 0)) if kp > 1 else pl.BlockSpec((tm, kc), lambda i: (i, 0))
    row = pl.BlockSpec((tm, d), lambda i: (i, 0))
    vec = pl.BlockSpec((1, d), lambda i: (0, 0))
    grid = (t // tm,)
    body, r_ops, r_in, r_shapes, r_out, r_scratch = with_rider(body, 5, 2, grid, rider)
    return pl.pallas_call(
        body, name=name, grid=grid, in_specs=[dy_spec, w_spec, row, vec, row] + r_in,
        out_specs=[row, vec] + r_out, out_shape=[_sds((t, d), F32), _sds((1, d), F32)] + r_shapes, scratch_shapes=r_scratch,
        compiler_params=_cp("arbitrary"))(dy_arr, w_arr, h, g, dres, *r_ops)


def matmul(a, b, mode, out_dtype, name, res=None, tm=None, tn=1792, tk=2816, out_planes=None, out_into=None):
    _, _, ar, ac, ap = _logical(a)
    _, _, br, bc, bp = _logical(b)
    if mode == "nn":
        m, ka, kb, n = ar, ac * ap, br, bc * bp
        n_plane, ka_plane, kb_plane = bc, ac, br
    elif mode == "nt":
        m, ka, n, kb = ar, ac * ap, br, bc * bp
        n_plane, ka_plane, kb_plane = br, ac, bc
    else:
        ka, m, kb, n = ar, ac * ap, br, bc * bp
        n_plane, ka_plane, kb_plane = bc, ar, br
    m_plane = ac if mode == "tn" else ar
    assert ka == kb, name
    k = ka
    kind, planes = out_planes or ("cols", 1)
    if kind == "cols":
        n_plane = min(n_plane, n // planes)
    tm = _tile(m_plane, ((1024, 1408, 512, 256, 128) if mode == "tn" else (512, 256, 128)) if tm is None else (tm, 1024, 512, 256, 128))
    if kind == "rows" and tm % (m // planes):
        tm = m_plane
    tn = _tile(n_plane, (tn, 1792, 1408, 1280, 1024, 896, 640, 512, 256, 128))
    tk = _tile(min(ka_plane, kb_plane), (tk, 2816, 1792, 1408, 1280, 1024, 512, 256, 128))
    nk = k // tk
    row_i, col_j, red = (lambda i, j, kk: i), (lambda i, j, kk: j), (lambda i, j, kk: kk)
    if mode == "nn":
        a_spec, b_spec, ca, cb = _spec(a, tm, tk, row_i, red), _spec(b, tk, tn, red, col_j), 1, 0
    elif mode == "nt":
        a_spec, b_spec, ca, cb = _spec(a, tm, tk, row_i, red), _spec(b, tn, tk, col_j, red), 1, 1
    else:
        a_spec, b_spec, ca, cb = _spec(a, tk, tm, red, row_i), _spec(b, tk, tn, red, col_j), 0, 0
    lead = () if out_into is None else (out_into[1],)
    if planes == 1:
        o_shape, o_block = (m, n), (tm, tn)
        o_index = lambda i, j, kk: lead + (i, j)
    elif kind == "cols":
        per = n // planes // tn
        o_shape, o_block = (planes, m, n // planes), (None, tm, tn)
        o_index = lambda i, j, kk: lead + (j // per, i, j % per)
    else:
        o_shape, o_block = (planes, m // planes, n), (tm // (m // planes), m // planes, tn)
        o_index = lambda i, j, kk: lead + (i, 0, j)
    o_spec = pl.BlockSpec((None,) * len(lead) + o_block, o_index)
    if out_into is not None:
        assert out_into[0].shape[1:] == o_shape and out_into[0].dtype == out_dtype, name
        o_shape = out_into[0].shape
    has_res = res is not None
    n_in = 2 + has_res + (out_into is not None)

    def body(*refs):
        a_ref, b_ref = refs[:2]
        rest = refs[2:2 + has_res] + refs[n_in:]
        o_ref = rest[1] if has_res else rest[0]
        p = _dot(a_ref[...].astype(BF16), b_ref[...].astype(BF16), ca, cb)
        if nk == 1:
            if has_res:
                p = p + rest[0][...]
            o_ref[...] = p.astype(o_ref.dtype).reshape(o_ref.shape)
        else:
            acc_ref = rest[-1]
            kk = pl.program_id(2)

            @pl.when(kk == 0)
            def _():
                acc_ref[...] = p

            @pl.when(kk > 0)
            def _():
                acc_ref[...] += p

            @pl.when(kk == nk - 1)
            def _():
                r = acc_ref[...]
                if has_res:
                    r = r + rest[0][...]
                o_ref[...] = r.astype(o_ref.dtype).reshape(o_ref.shape)

    operands = [_arr(a), _arr(b)] + ([res] if has_res else []) + ([out_into[0]] if out_into is not None else [])
    return pl.pallas_call(
        body, name=name, grid=(m // tm, n // tn, nk),
        in_specs=[a_spec, b_spec] + ([pl.BlockSpec((tm, tn), lambda i, j, kk: (i, j))] if has_res else [])
        + ([pl.BlockSpec(memory_space=pl.ANY)] if out_into is not None else []),
        out_specs=o_spec, out_shape=_sds(o_shape, out_dtype),
        input_output_aliases={n_in - 1: 0} if out_into is not None else {},
        scratch_shapes=[pltpu.VMEM((tm, tn), F32)] if nk > 1 else [],
        compiler_params=_cp("parallel", "parallel", "arbitrary"))(*operands)


def _swiglu(gate, up):
    return gate / (1.0 + jnp.exp(-gate)) * up


def gate_up_fwd(h, g, w, layer, name, rider=None):
    t, d = h.shape
    half = w.shape[-1]
    tm = _tile(t, (512, 256, 128))

    def body(h_ref, g_ref, wg_ref, wu_ref, hn_ref, gu_ref, act_ref):
        a = _rms(h_ref[...], g_ref[...]).astype(BF16)
        hn_ref[...] = a
        gate, up = _dot(a, wg_ref[...], 1, 0), _dot(a, wu_ref[...], 1, 0)
        gu_ref[0] = gate.astype(gu_ref.dtype)
        gu_ref[1] = up.astype(gu_ref.dtype)
        act_ref[...] = _swiglu(gate, up).astype(act_ref.dtype)

    grid = (2, t // tm)
    body, r_ops, r_in, r_shapes, r_out, r_scratch = with_rider(body, 4, 3, grid, rider)
    return pl.pallas_call(
        body, name=name, grid=grid,
        in_specs=[pl.BlockSpec((tm, d), lambda j, i: (i, 0)), pl.BlockSpec((1, d), lambda j, i: (0, 0)),
                  pl.BlockSpec((None, None, d, half), lambda j, i: (layer, j, 0, 0)),
                  pl.BlockSpec((None, None, d, half), lambda j, i: (layer, 2 + j, 0, 0))] + r_in,
        out_specs=[pl.BlockSpec((None, tm, d), lambda j, i: (j, i, 0)), pl.BlockSpec((2, tm, half), lambda j, i: (0, i, j)),
                   pl.BlockSpec((tm, half), lambda j, i: (i, j))] + r_out,
        out_shape=[_sds((2, t, d), BF16), _sds((2, t, 2 * half), BF16), _sds((t, 2 * half), BF16)] + r_shapes,
        scratch_shapes=r_scratch, compiler_params=_cp("arbitrary", "arbitrary"))(h, g, w, w, *r_ops)


def down_dx_swiglu_bwd(dh, wd, gu, name, rider=None):
    t, d = dh.shape
    w, layer = wd
    f = w.shape[-2]
    tm = _tile(t, (512, 256, 128))
    tn = _tile(f, (1408, 512, 256, 128))

    def body(dh_ref, w_ref, gu_ref, o_ref):
        dact = _dot(dh_ref[...].astype(BF16), w_ref[...], 1, 1)
        gate, up = gu_ref[0].astype(F32), gu_ref[1].astype(F32)
        sig = 1.0 / (1.0 + jnp.exp(-gate))
        silu = gate * sig
        o_ref[0] = (dact * up * (sig + silu * (1.0 - sig))).astype(o_ref.dtype)
        o_ref[1] = (dact * silu).astype(o_ref.dtype)

    planes = pl.BlockSpec((2, tm, tn), lambda j, i: (0, i, j))
    grid = (f // tn, t // tm)
    body, r_ops, r_in, r_shapes, r_out, r_scratch = with_rider(body, 3, 1, grid, rider)
    return pl.pallas_call(
        body, name=name, grid=grid,
        in_specs=[pl.BlockSpec((tm, d), lambda j, i: (i, 0)), pl.BlockSpec((None, tn, d), lambda j, i: (layer, j, 0)), planes] + r_in,
        out_specs=[planes] + r_out, out_shape=[_sds((2, t, f), BF16)] + r_shapes, scratch_shapes=r_scratch,
        compiler_params=_cp("arbitrary", "arbitrary"))(dh, w, gu, *r_ops)


def _softmax_over_keys(s, sink=None):
    m = s.max(axis=0, keepdims=True)
    if sink is not None:
        m = jnp.maximum(m, sink)
    m = lax.stop_gradient(m)
    e = jnp.exp(s - m)
    den = e.sum(axis=0, keepdims=True)
    if sink is not None:
        den = den + jnp.exp(sink - m)
    return e * (1.0 / den)


def _low_lanes():
    return lax.broadcasted_iota(jnp.int32, (1, 128), 1) < HEAD_DIM


def _stack_heads(slabs):
    low = _low_lanes()
    return jnp.concatenate([p for s in slabs for p in (jnp.where(low, s, 0.0), jnp.where(low, 0.0, s))], axis=0)


def _unstack_heads(o, n_slabs):
    low = _low_lanes()
    return [jnp.where(low, o[2 * j * WINDOW:(2 * j + 1) * WINDOW], o[(2 * j + 1) * WINDOW:(2 * j + 2) * WINDOW])
            for j in range(n_slabs)]


def _swa_group(q_slabs, k_both, v_both, sinks, mask):
    qs = _stack_heads(q_slabs).astype(BF16)
    s = jnp.where(mask, _dot(k_both.astype(BF16), qs, 1, 1) * SCALE, NEG)
    sink = jnp.concatenate([jnp.broadcast_to(v, (1, WINDOW)) for v in sinks], axis=1)
    return _unstack_heads(_dot(_softmax_over_keys(s, sink).astype(BF16), v_both.astype(BF16), 0, 0), len(q_slabs))


def _mem_pair(q_slab, k_slab, v_slab):
    s = _dot(k_slab.astype(BF16), _stack_heads([q_slab]).astype(BF16), 1, 1) * SCALE
    return _unstack_heads(_dot(_softmax_over_keys(s).astype(BF16), v_slab.astype(BF16), 0, 0), 1)[0]


def _gelu(x):
    return 0.5 * x * (1.0 + jnp.tanh(0.7978845608028654 * (x + 0.044715 * (x * x * x))))


def _gmlp_group(zu, zv, w, bcol, lg, lb, tri):
    u, v = _gelu(zu), _gelu(zv)
    mu = jnp.mean(v, axis=-1, keepdims=True)
    var = jnp.mean(jnp.square(v - mu), axis=-1, keepdims=True)
    vn = (v - mu) * lax.rsqrt(var + EPS) * lg + lb
    sv = _dot(jnp.where(tri, w, 0.0).astype(BF16), vn.astype(BF16), 1, 0) + bcol
    return u * sv


def _cols(x, width):
    return [x[:, j * width:(j + 1) * width] for j in range(x.shape[1] // width)]


def _swa_mask(has_prev):
    qi = lax.broadcasted_iota(jnp.int32, (2 * WINDOW, GROUP * WINDOW), 1) & (WINDOW - 1)
    kj = lax.broadcasted_iota(jnp.int32, (2 * WINDOW, GROUP * WINDOW), 0)
    in_prev = jnp.logical_and(jnp.logical_and(kj < WINDOW, kj > qi), has_prev)
    return jnp.logical_or(in_prev, jnp.logical_and(kj >= WINDOW, kj - WINDOW <= qi))


def _mix_a(q_slabs, k_boths, v_boths, sinks, qm_slabs, km_slabs, vm_slabs, mask):
    per = GROUP // 2
    outs = []
    for g in range(KV_HEADS):
        outs += _swa_group(q_slabs[per * g:per * (g + 1)], k_boths[g], v_boths[g], sinks[GROUP * g:GROUP * (g + 1)], mask)
    return outs + [_mem_pair(qm_slabs[j], km_slabs[j], vm_slabs[j]) for j in range(MEM_HEADS // 2)]


def _in_both_halves(prev, cur):
    cat = jnp.concatenate([prev, cur], axis=0)
    rolled = pltpu.roll(cat, HEAD_DIM, axis=1)
    low = _low_lanes()
    return [jnp.where(low, cat, rolled), jnp.where(low, rolled, cat)]


def _from_both_halves(d_boths):
    t = [d + pltpu.roll(d, HEAD_DIM, axis=1) for d in d_boths]
    return jnp.where(_low_lanes(), t[0], t[1])


def _mix_a_specs(nm, blk):
    prev = lambda n: jnp.maximum(blk(n) - 1, 0)
    return [pl.BlockSpec((WINDOW, Q_W), lambda n: (blk(n), 0)),
            pl.BlockSpec((WINDOW, KV_W), lambda n: (prev(n), Q_W // KV_W)),
            pl.BlockSpec((WINDOW, KV_W), lambda n: (blk(n), Q_W // KV_W)),
            pl.BlockSpec((WINDOW, KV_W), lambda n: (prev(n), Q_W // KV_W + 1)),
            pl.BlockSpec((WINDOW, KV_W), lambda n: (blk(n), Q_W // KV_W + 1)),
            pl.BlockSpec((WINDOW, MEM_W), lambda n: (blk(n), (Q_W + 2 * KV_W) // MEM_W)),
            pl.BlockSpec((16, 128), lambda n: (0, 0)),
            pl.BlockSpec((nm, MEM_W), lambda n: (0, 0)),
            pl.BlockSpec((nm, MEM_W), lambda n: (0, 1))]


def _mix_a_args(refs):
    q, kp, kc, vp, vc, qm, sk, km, vm = [r[...].astype(F32) for r in refs]
    return (_cols(q, 128), _in_both_halves(kp, kc), _in_both_halves(vp, vc), [sk[h:h + 1, 0:1] for h in range(Q_HEADS)],
            _cols(qm, 128), _cols(km, 128), _cols(vm, 128))


def mixer_a_fwd(proj, sk, kv, name, rider=None):
    t, nm = proj.shape[0], kv.shape[0]

    def body(*refs):
        o_ref = refs[-1]
        slabs = _mix_a(*_mix_a_args(refs[:-1]), _swa_mask(pl.program_id(0) > 0))
        o_ref[...] = jnp.concatenate(slabs, axis=1).astype(o_ref.dtype)

    grid = (t // WINDOW,)
    body, r_ops, r_in, r_shapes, r_out, r_scratch = with_rider(body, 9, 1, grid, rider)
    return pl.pallas_call(
        body, name=name, grid=grid, in_specs=_mix_a_specs(nm, lambda n: n) + r_in,
        out_specs=[pl.BlockSpec((WINDOW, Q_W + MEM_W), lambda n: (n, 0))] + r_out,
        out_shape=[_sds((t, Q_W + MEM_W), BF16)] + r_shapes, scratch_shapes=r_scratch,
        compiler_params=_cp("arbitrary"))(proj, proj, proj, proj, proj, proj, sk, kv, kv, *r_ops)


def _onehot_rows(vals, shape):
    rows = lax.broadcasted_iota(jnp.int32, shape, 0)
    out = jnp.zeros(shape, F32)
    for h, v in enumerate(vals):
        out = out + jnp.where(rows == h, jnp.broadcast_to(v, shape), 0.0)
    return out


def mixer_a_bwd(proj, dcat, sk, kv, name, rider=None):
    t, nm = proj.shape[0], kv.shape[0]
    nb = t // WINDOW
    blk = lambda i: nb - 1 - i

    def body(*refs):
        dcat_ref, dproj_ref, dsk_ref, dkv_ref, carry_ref = refs[9:]
        i = pl.program_id(0)

        @pl.when(i == 0)
        def _():
            carry_ref[...] = jnp.zeros_like(carry_ref)
            dsk_ref[...] = jnp.zeros_like(dsk_ref)
            dkv_ref[...] = jnp.zeros_like(dkv_ref)

        mask = _swa_mask(blk(i) > 0)
        _, vjp = jax.vjp(lambda *a: _mix_a(*a, mask), *_mix_a_args(refs[:9]))
        dqs, dk_boths, dv_boths, dsinks, dqms, dkms, dvms = vjp(_cols(dcat_ref[...].astype(F32), 128))
        dkv = jnp.concatenate([_from_both_halves(dk_boths), _from_both_halves(dv_boths)], axis=1)
        dkv_cur = dkv[WINDOW:] + carry_ref[...]
        carry_ref[...] = dkv[:WINDOW]
        dproj_ref[...] = jnp.concatenate(dqs + [dkv_cur] + dqms, axis=1).astype(dproj_ref.dtype)
        dsk_ref[...] += _onehot_rows(dsinks, (16, 128))
        dkv_ref[...] += jnp.concatenate(dkms + dvms, axis=1)

    width = Q_W + 2 * KV_W + MEM_W
    body, r_ops, r_in, r_shapes, r_out, r_scratch = with_rider(body, 10, 3, (nb,), rider)
    return pl.pallas_call(
        body, name=name, grid=(nb,),
        in_specs=_mix_a_specs(nm, blk) + [pl.BlockSpec((WINDOW, Q_W + MEM_W), lambda i: (blk(i), 0))] + r_in,
        out_specs=[pl.BlockSpec((WINDOW, width), lambda i: (blk(i), 0)), pl.BlockSpec((16, 128), lambda i: (0, 0)),
                   pl.BlockSpec((nm, 2 * MEM_W), lambda i: (0, 0))] + r_out,
        out_shape=[_sds((t, width), BF16), _sds((16, 128), F32), _sds((nm, 2 * MEM_W), F32)] + r_shapes,
        scratch_shapes=[pltpu.VMEM((WINDOW, 2 * KV_W), F32)] + r_scratch,
        compiler_params=_cp("arbitrary"))(proj, proj, proj, proj, proj, proj, sk, kv, kv, dcat, *r_ops)


def _mix_b(zus, zvs, ws, bcols, lgs, lbs, qms, kms, vms, tri):
    outs = [_gmlp_group(zus[g], zvs[g], ws[g], bcols[g], lgs[g], lbs[g], tri) for g in range(B_GROUPS)]
    return outs + [_mem_pair(qms[j], kms[j], vms[j]) for j in range(MEM_HEADS // 2)]


def _mix_b_specs(nm):
    return [pl.BlockSpec((WINDOW, 2 * B_W), lambda n: (n, 0)),
            pl.BlockSpec((WINDOW, MEM_W), lambda n: (n, 2 * B_W // MEM_W)),
            pl.BlockSpec((B_GROUPS, WINDOW, WINDOW), lambda n: (0, 0, 0)),
            pl.BlockSpec((WINDOW, 128), lambda n: (0, 0)),
            pl.BlockSpec((8, 128), lambda n: (0, 0)),
            pl.BlockSpec((8, 128), lambda n: (0, 0)),
            pl.BlockSpec((nm, MEM_W), lambda n: (0, 0)),
            pl.BlockSpec((nm, MEM_W), lambda n: (0, 1))]


def _mix_b_args(refs):
    z, qm, ws, bt, lg, lb, km, vm = [r[...].astype(F32) for r in refs]
    zs = _cols(z, 128)
    return (zs[:B_GROUPS], zs[B_GROUPS:], [ws[g] for g in range(B_GROUPS)], [bt[:, g:g + 1] for g in range(B_GROUPS)],
            [lg[g:g + 1, :] for g in range(B_GROUPS)], [lb[g:g + 1, :] for g in range(B_GROUPS)],
            _cols(qm, 128), _cols(km, 128), _cols(vm, 128))


def _tri():
    return lax.broadcasted_iota(jnp.int32, (WINDOW, WINDOW), 0) >= lax.broadcasted_iota(jnp.int32, (WINDOW, WINDOW), 1)


def mixer_b_fwd(proj, ws, bt, lg, lb, kv, name):
    t, nm = proj.shape[0], kv.shape[0]

    def body(*refs):
        o_ref = refs[-1]
        o_ref[...] = jnp.concatenate(_mix_b(*_mix_b_args(refs[:-1]), _tri()), axis=1).astype(o_ref.dtype)

    return pl.pallas_call(
        body, name=name, grid=(t // WINDOW,), in_specs=_mix_b_specs(nm),
        out_specs=pl.BlockSpec((WINDOW, B_W + MEM_W), lambda n: (n, 0)),
        out_shape=_sds((t, B_W + MEM_W), BF16), compiler_params=_cp("parallel"))(proj, proj, ws, bt, lg, lb, kv, kv)


def mixer_b_bwd(proj, dcat, ws, bt, lg, lb, kv, name):
    t, nm = proj.shape[0], kv.shape[0]

    def body(*refs):
        dcat_ref, dproj_ref, dws_ref, dbt_ref, dlg_ref, dlb_ref, dkv_ref = refs[8:]

        @pl.when(pl.program_id(0) == 0)
        def _():
            for r in (dws_ref, dbt_ref, dlg_ref, dlb_ref, dkv_ref):
                r[...] = jnp.zeros_like(r)

        tri = _tri()
        zus, zvs, ws, bcols, lgs, lbs, qms, kms, vms = _mix_b_args(refs[:8])
        douts = _cols(dcat_ref[...].astype(F32), 128)
        grads = []
        for g in range(B_GROUPS):
            _, vjp = jax.vjp(lambda *a: _gmlp_group(*a, tri), zus[g], zvs[g], ws[g], bcols[g], lgs[g], lbs[g])
            grads.append(vjp(douts[g]))
        dzus, dzvs, dws, dbcols, dlgs, dlbs = [list(t) for t in zip(*grads)]
        grads = []
        for j in range(MEM_HEADS // 2):
            _, vjp = jax.vjp(_mem_pair, qms[j], kms[j], vms[j])
            grads.append(vjp(douts[B_GROUPS + j]))
        dqms, dkms, dvms = [list(t) for t in zip(*grads)]
        dproj_ref[...] = jnp.concatenate(dzus + dzvs + dqms, axis=1).astype(dproj_ref.dtype)
        for g in range(B_GROUPS):
            dws_ref[g] += dws[g]
        lanes = lax.broadcasted_iota(jnp.int32, (WINDOW, 128), 1)
        dbt = jnp.zeros((WINDOW, 128), F32)
        for g in range(B_GROUPS):
            dbt = dbt + jnp.where(lanes == g, jnp.broadcast_to(dbcols[g], (WINDOW, 128)), 0.0)
        dbt_ref[...] += dbt
        dlg_ref[...] += _onehot_rows(dlgs, (8, 128))
        dlb_ref[...] += _onehot_rows(dlbs, (8, 128))
        dkv_ref[...] += jnp.concatenate(dkms + dvms, axis=1)

    width = 2 * B_W + MEM_W
    const2 = lambda n: (0, 0)
    return pl.pallas_call(
        body, name=name, grid=(t // WINDOW,),
        in_specs=_mix_b_specs(nm) + [pl.BlockSpec((WINDOW, B_W + MEM_W), lambda n: (n, 0))],
        out_specs=[pl.BlockSpec((WINDOW, width), lambda n: (n, 0)),
                   pl.BlockSpec((B_GROUPS, WINDOW, WINDOW), lambda n: (0, 0, 0)),
                   pl.BlockSpec((WINDOW, 128), const2), pl.BlockSpec((8, 128), const2), pl.BlockSpec((8, 128), const2),
                   pl.BlockSpec((nm, 2 * MEM_W), const2)],
        out_shape=[_sds((t, width), BF16), _sds((B_GROUPS, WINDOW, WINDOW), F32), _sds((WINDOW, 128), F32),
                   _sds((8, 128), F32), _sds((8, 128), F32), _sds((nm, 2 * MEM_W), F32)],
        compiler_params=_cp("arbitrary"))(proj, proj, ws, bt, lg, lb, kv, kv, dcat)


def _adamw_update(w, g, m, v):
    m2 = ADAM_B1 * m + (1.0 - ADAM_B1) * g
    v2 = ADAM_B2 * v + (1.0 - ADAM_B2) * jnp.square(g)
    m_hat = m2 / (1.0 - ADAM_B1 ** ADAM_STEP)
    v_hat = v2 / (1.0 - ADAM_B2 ** ADAM_STEP)
    return -ADAM_LR * (m_hat / (jnp.sqrt(v_hat) + ADAM_EPS) + ADAM_WD * w), m2, v2


def adamw(w, g, m, v, name):
    r, c = w.shape
    tr = _tile(r, (512, 352, 256, 128, 64, 32, 16, 8))

    def body(w_ref, g_ref, m_ref, v_ref, d_ref, nm_ref, nv_ref):
        d_ref[...], nm_ref[...], nv_ref[...] = _adamw_update(w_ref[...], g_ref[...], m_ref[...], v_ref[...])

    spec = pl.BlockSpec((tr, c), lambda i: (i, 0))
    return pl.pallas_call(
        body, name=name, grid=(r // tr,), in_specs=[spec] * 4, out_specs=[spec] * 3,
        out_shape=[_sds((r, c), F32)] * 3, compiler_params=_cp("parallel"))(w, g, m, v)


def adamw_halves(w, g_mine, g_theirs, m, v, c_arr, rows, name):
    r, c = w.shape
    tr = _tile(rows // 2, (256, 352, 128, 64, 32, 16, 8))
    per_half = rows // 2 // tr

    def body(c_ref, w_ref, gm_ref, gt_ref, m_ref, v_ref, g_ref, d_ref, nm_ref, nv_ref):
        g = jnp.where(pl.program_id(0) // per_half % 2 == c_ref[0], gm_ref[...], gt_ref[...])
        g_ref[...] = g
        d_ref[...], nm_ref[...], nv_ref[...] = _adamw_update(w_ref[...], g, m_ref[...], v_ref[...])

    spec = pl.BlockSpec((tr, c), lambda i, cr: (i, 0))
    half = pl.BlockSpec((tr, c), lambda i, cr: (i // (2 * per_half) * per_half + i % per_half, 0))
    return pl.pallas_call(
        body, name=name,
        grid_spec=pltpu.PrefetchScalarGridSpec(num_scalar_prefetch=1, grid=(r // tr,), in_specs=[spec, half, half, spec, spec],
                                               out_specs=[spec] * 4),
        out_shape=[_sds((r, c), F32)] * 4, compiler_params=_cp("parallel"))(c_arr, w, g_mine, g_theirs, m, v)


def _place():
    return lax.axis_index("x"), lax.axis_index("y"), lax.axis_index("c")


def _other_chips(x, y):
    return [(1 - x, y), (x, 1 - y), (1 - x, 1 - y)]


def _remote(src, dst, send_sems, recv_sems, k, dev):
    return pltpu.make_async_remote_copy(src_ref=src, dst_ref=dst, send_sem=send_sems.at[k], recv_sem=recv_sems.at[k],
                                        device_id=dev, device_id_type=MESH)


class Exchange:
    def __init__(self, ins, out_shapes, n_sems, start, finish):
        self.ins, self.out_shapes, self.start, self.finish = list(ins), list(out_shapes), start, finish
        self.sems = [n_sems, n_sems] if isinstance(n_sems, int) else list(n_sems)

    def scratch(self):
        return [pltpu.SemaphoreType.DMA((n,)) for n in self.sems]


def both_exchanges(a, b):
    ni, no, ns = len(a.ins), len(a.out_shapes), len(a.sems)

    def start(ins, outs, *sems):
        a.start(ins[:ni], outs[:no], *sems[:ns])
        b.start(ins[ni:], outs[no:], *sems[ns:])

    def finish(ins, outs, *sems):
        a.finish(ins[:ni], outs[:no], *sems[:ns])
        b.finish(ins[ni:], outs[no:], *sems[ns:])

    return Exchange(a.ins + b.ins, a.out_shapes + b.out_shapes, a.sems + b.sems, start, finish)


def run_exchange(ex, name):
    ni, no = len(ex.ins), len(ex.out_shapes)

    def body(*refs):
        ex.start(refs[:ni], refs[ni:ni + no], *refs[ni + no:])
        ex.finish(refs[:ni], refs[ni:ni + no], *refs[ni + no:])

    return pl.pallas_call(
        body, name=name, in_specs=[HBM_SPEC] * ni, out_specs=[HBM_SPEC] * no, out_shape=ex.out_shapes, scratch_shapes=ex.scratch(),
        compiler_params=pltpu.CompilerParams(has_side_effects=True))(*ex.ins)


def with_rider(body, n_in, n_out, grid, ex):
    if ex is None:
        return body, [], [], [], [], []
    ni, no, ns = len(ex.ins), len(ex.out_shapes), len(ex.sems)

    def riding(*refs):
        r_in, r_out, sems = refs[n_in:n_in + ni], refs[n_in + ni + n_out:n_in + ni + n_out + no], refs[-ns:]
        first = last = None
        for axis, size in enumerate(grid):
            at_first, at_last = pl.program_id(axis) == 0, pl.program_id(axis) == size - 1
            first = at_first if first is None else jnp.logical_and(first, at_first)
            last = at_last if last is None else jnp.logical_and(last, at_last)

        @pl.when(first)
        def _():
            ex.start(r_in, r_out, *sems)

        body(*refs[:n_in], *refs[n_in + ni:n_in + ni + n_out], *refs[n_in + ni + n_out + no:-ns])

        @pl.when(last)
        def _():
            ex.finish(r_in, r_out, *sems)

    return riding, ex.ins, [HBM_SPEC] * ni, ex.out_shapes, [HBM_SPEC] * no, ex.scratch()


def gather_exchange(shards):
    nw = len(shards)
    entry = [k for _, k in shards]

    def rows(ref, cc):
        return pl.ds(cc * (ref.shape[1] // 2), ref.shape[1] // 2)

    def sent(ins, outs, send_sems, recv_sems, w, j):
        x, y, c = _place()
        return _remote(ins[w].at[pl.ds(entry[w], 1), rows(ins[w], c)], outs[w].at[:, 2 * x + y, rows(ins[w], c)], send_sems, recv_sems,
                       7 * w + j, (*_other_chips(x, y)[j], c))

    def landed(ins, outs, send_sems, recv_sems, w, j, cc, to):
        x, y, c = _place()
        chip = _other_chips(x, y)[j]
        blk = outs[w].at[:, 2 * chip[0] + chip[1], rows(ins[w], cc)]
        return _remote(blk, blk, send_sems, recv_sems, 7 * w + (j if to is None else 3 + j), (x, y, c) if to is None else to)

    def own(ins, outs, send_sems, recv_sems, w):
        x, y, c = _place()
        return _remote(ins[w].at[pl.ds(entry[w], 1)], outs[w].at[:, 2 * x + y], send_sems, recv_sems, 7 * w + 6, (x, y, 1 - c))

    def start(ins, outs, send_sems, recv_sems):
        for j in range(3):
            for w in range(nw):
                sent(ins, outs, send_sems, recv_sems, w, j).start()
        for w in range(nw):
            own(ins, outs, send_sems, recv_sems, w).start()

    def finish(ins, outs, send_sems, recv_sems):
        x, y, c = _place()
        for j in range(3):
            for w in range(nw):
                landed(ins, outs, send_sems, recv_sems, w, j, c, None).wait_recv()
                landed(ins, outs, send_sems, recv_sems, w, j, c, (x, y, 1 - c)).start()
        for w in range(nw):
            own(ins, outs, send_sems, recv_sems, w).wait()
        for j in range(3):
            for w in range(nw):
                landed(ins, outs, send_sems, recv_sems, w, j, 1 - c, (x, y, c)).wait_recv()
        for j in range(3):
            for w in range(nw):
                sent(ins, outs, send_sems, recv_sems, w, j).wait_send()
                landed(ins, outs, send_sems, recv_sems, w, j, c, (x, y, 1 - c)).wait_send()

    return Exchange([s for s, _ in shards], [_sds((1, 4) + s.shape[1:], s.dtype) for s, _ in shards], 7 * nw, start, finish)


def copies_exchange(ins, out_shapes, n_sems, copies):
    def start(*refs):
        for cp in copies(*refs):
            cp.start()

    def finish(*refs):
        for cp in copies(*refs):
            cp.wait()

    return Exchange(ins, out_shapes, n_sems, start, finish)


def sibling_halves_exchange(gs):
    def copies(ins, outs, send_sems, recv_sems):
        x, y, c = _place()
        return [_remote(g.at[:, :, pl.ds((1 - c) * (g.shape[2] // 2), g.shape[2] // 2)], o, send_sems, recv_sems, w, (x, y, 1 - c))
                for w, (g, o) in enumerate(zip(ins, outs))]

    return copies_exchange(gs, [_sds(g.shape[:2] + (g.shape[2] // 2, g.shape[3]), g.dtype) for g in gs], len(gs), copies)


def chips_exchange(sbs):
    def copies(ins, outs, send_sems, recv_sems):
        x, y, c = _place()
        return [_remote(s.at[:, 2 * chip[0] + chip[1]], o.at[j], send_sems, recv_sems, 3 * w + j, (*chip, c))
                for j, chip in enumerate(_other_chips(x, y)) for w, (s, o) in enumerate(zip(ins, outs))]

    return copies_exchange(sbs, [_sds((3, s.shape[0]) + s.shape[2:], s.dtype) for s in sbs], 3 * len(sbs), copies)


def sibling_exchange(fs):
    def copies(ins, outs, send_sems, recv_sems):
        x, y, c = _place()
        return [_remote(f, o, send_sems, recv_sems, w, (x, y, 1 - c)) for w, (f, o) in enumerate(zip(ins, outs))]

    return copies_exchange(fs, [_sds(f.shape, f.dtype) for f in fs], len(fs), copies)


def _half_tile(a):
    return _tile(a, (256, 352, 176, 128, 64, 32, 16))


def chip_partial_sums(g, r1, c_arr, name):
    nl, _, a2, b = r1.shape
    ta = _half_tile(a2)
    per = a2 // ta

    def body(c_ref, g_ref, r_ref, o_ref):
        o_ref[...] = (g_ref[...] + r_ref[...]).astype(o_ref.dtype)

    blk = (None, None, ta, b)
    return pl.pallas_call(
        body, name=name,
        grid_spec=pltpu.PrefetchScalarGridSpec(
            num_scalar_prefetch=1, grid=(nl, 4, per),
            in_specs=[pl.BlockSpec(blk, lambda l, s, i, c: (l, s, c[0] * per + i, 0)), pl.BlockSpec(blk, lambda l, s, i, c: (l, s, i, 0))],
            out_specs=pl.BlockSpec(blk, lambda l, s, i, c: (l, s, i, 0))),
        out_shape=_sds(r1.shape, BF16), compiler_params=_cp("parallel", "parallel", "parallel"))(c_arr, g, r1)


def shard_total(g, r1, r2, cs_arr, name):
    nl, _, a2, b = r1.shape
    ta = _half_tile(a2)
    per = a2 // ta

    def body(cs_ref, g_ref, r1_ref, p0_ref, p1_ref, p2_ref, o_ref):
        o_ref[...] = (((g_ref[...] + r1_ref[...]) + p0_ref[...].astype(F32)) + p1_ref[...].astype(F32)) + p2_ref[...].astype(F32)

    blk4, blk3 = (None, None, ta, b), (None, ta, b)
    peer = lambda k: pl.BlockSpec((None, None, ta, b), lambda l, i, cs: (k, l, i, 0))
    return pl.pallas_call(
        body, name=name,
        grid_spec=pltpu.PrefetchScalarGridSpec(
            num_scalar_prefetch=1, grid=(nl, per),
            in_specs=[pl.BlockSpec(blk4, lambda l, i, cs: (l, cs[1], cs[0] * per + i, 0)),
                      pl.BlockSpec(blk4, lambda l, i, cs: (l, cs[1], i, 0)), peer(0), peer(1), peer(2)],
            out_specs=pl.BlockSpec(blk3, lambda l, i, cs: (l, i, 0))),
        out_shape=_sds((nl, a2, b), F32), compiler_params=_cp("parallel", "parallel"))(cs_arr, g, r1, r2, r2, r2)


def allgather_small(v, name):
    r, n = v.shape

    def body(x_ref, out_ref, send_sems, recv_sems, local_sem):
        x, y, c = _place()
        me, sibling = (x, y, c), (x, y, 1 - c)
        chips = _other_chips(x, y)

        def rows(px, py, pc):
            return out_ref.at[pl.ds((4 * px + 2 * py + pc) * r, r), :]

        def copy(k, block, to, src=None):
            return _remote(rows(*block) if src is None else src, rows(*block), send_sems, recv_sems, k, to)

        mine = pltpu.make_async_copy(x_ref, rows(*me), local_sem)
        mine.start()
        first = [copy(0, me, sibling, src=x_ref)] + [copy(1 + j, me, (*chip, c), src=x_ref) for j, chip in enumerate(chips)]
        for cp in first:
            cp.start()
        passed = [copy(4 + j, (*chip, c), sibling) for j, chip in enumerate(chips)]
        for j, chip in enumerate(chips):
            copy(1 + j, (*chip, c), me).wait_recv()
            passed[j].start()
        copy(0, sibling, me).wait_recv()
        for j, chip in enumerate(chips):
            copy(4 + j, (*chip, 1 - c), me).wait_recv()
        for cp in first + passed:
            cp.wait_send()
        mine.wait()

    return pl.pallas_call(
        body, name=name, in_specs=[VMEM_SPEC], out_specs=VMEM_SPEC, out_shape=_sds((8 * r, n), v.dtype),
        scratch_shapes=[pltpu.SemaphoreType.DMA((7,)), pltpu.SemaphoreType.DMA((7,)), pltpu.SemaphoreType.DMA],
        compiler_params=pltpu.CompilerParams(has_side_effects=True, vmem_limit_bytes=V7X_VMEM_LIMIT_BYTES))(v)


def sum_devices(v8, name):
    _, r, n = v8.shape
    tr = _tile(r, (88, 64, 32, 16, 8))

    def body(v_ref, o_ref):
        acc = v_ref[0]
        for d in range(1, 8):
            acc = acc + v_ref[d]
        o_ref[...] = acc

    return pl.pallas_call(
        body, name=name, grid=(r // tr,), in_specs=[pl.BlockSpec((8, tr, n), lambda i: (0, i, 0))],
        out_specs=pl.BlockSpec((tr, n), lambda i: (i, 0)), out_shape=_sds((r, n), F32), compiler_params=_cp("parallel"))(v8)


SHARDED = (("a_w_in", 2), ("a_w_out", 1), ("b_w_in", 2), ("b_w_out", 1), ("w_mem_kv", 1), ("w_gate_up", 2), ("w_down", 1))


def _full_from_gathered(wg, axis):
    l, _, a, b = wg.shape
    if axis == 1:
        return wg.reshape(l, 4 * a, b)
    return wg.transpose(0, 2, 1, 3).reshape(l, a, 4 * b)


def _by_shard(dw, axis):
    l, k, n = dw.shape
    if axis == 1:
        return dw.reshape(l, 4, k // 4, n)
    return dw.reshape(l, k, 4, n // 4).transpose(0, 2, 1, 3)


def _pack(arrs):
    parts = []
    for a in arrs:
        flat = a.reshape(-1)
        flat = jnp.pad(flat, (0, -flat.shape[0] % 1024))
        parts.append(flat.reshape(-1, 128))
    return jnp.concatenate(parts, axis=0)


def _unpack(buf, like):
    out, row = [], 0
    for a in like:
        size = 1
        for s in a.shape:
            size *= s
        rows = -(-size // 1024) * 8
        out.append(buf[row:row + rows].reshape(-1)[:size].reshape(a.shape))
        row += rows
    return out


def kernel(x, mem, mem_norm_g, mix_norm_g, ffn_norm_g, final_norm_g, a_w_in, a_sinks, a_w_out, b_w_in, b_w_s, b_bias_s, b_ln_g, b_ln_b, b_w_out, w_mem_kv, w_gate_up, w_down, loss_target, m_mem_norm_g, m_mix_norm_g, m_ffn_norm_g, m_final_norm_g, m_a_w_in, m_a_sinks, m_a_w_out, m_b_w_in, m_b_w_s, m_b_bias_s, m_b_ln_g, m_b_ln_b, m_b_w_out, m_w_mem_kv, m_w_gate_up, m_w_down, v_mem_norm_g, v_mix_norm_g, v_ffn_norm_g, v_final_norm_g, v_a_w_in, v_a_sinks, v_a_w_out, v_b_w_in, v_b_w_s, v_b_bias_s, v_b_ln_g, v_b_ln_b, v_b_w_out, v_w_mem_kv, v_w_gate_up, v_w_down):
    given = dict(locals())
    depth = mix_norm_g.shape[0]
    d = x.shape[-1]
    xi, yi, ci = _place()
    c_arr = jnp.stack([ci]).astype(jnp.int32)
    cs_arr = jnp.stack([ci, 2 * xi + yi]).astype(jnp.int32)

    axis_of = dict(SHARDED)
    own = {n: given[n].astype(BF16) for n, _ in SHARDED}
    MIXER, FFN = slice(0, 3), slice(3, 5)

    def layer_weights(l):
        mix = "a" if l % 2 == 0 else "b"
        return [(mix + "_w_in", l // 2), (mix + "_w_out", l // 2), ("w_mem_kv", l), ("w_gate_up", l), ("w_down", l)]

    def gather_of(l, part=slice(0, 5)):
        return gather_exchange([(own[n], k) for n, k in layer_weights(l)[part]])

    def usable(l, gathered, part=slice(0, 5)):
        return {n[2:] if n[0] in "ab" else n: (wg if n == "w_gate_up" else _full_from_gathered(wg, axis_of[n]), 0)
                for (n, _), wg in zip(layer_weights(l)[part], gathered)}

    weights = {0: usable(0, run_exchange(gather_of(0, MIXER), "gather_weights"), MIXER)}

    h = x.reshape(-1, d)
    tgt = loss_target.reshape(-1, d)
    mem2 = mem.reshape(-1, d)
    row = lambda v: v.reshape(1, -1)

    mem_n = rmsnorm_fwd(mem2, row(mem_norm_g), "mem_norm")
    saved = []
    for i in range(depth):
        j = i // 2
        wl = weights[i]
        w_in, w_out = wl["w_in"], wl["w_out"]
        kv = matmul(mem_n, wl["w_mem_kv"], "nn", BF16, "mem_kv")
        if i % 2 == 0:
            sk = jnp.pad(jnp.broadcast_to(a_sinks[j][:, None], (Q_HEADS, 128)), ((0, 16 - Q_HEADS), (0, 0)))
            xn, proj = norm_matmul(h, row(mix_norm_g[i]), w_in, BF16, "a_in")
            cat, *gathered = mixer_a_fwd(proj, sk, kv, "mixer_a", rider=gather_of(0, FFN) if i == 0 else None)
            if gathered:
                wl.update(usable(0, gathered, FFN))
            extra = (sk,)
        else:
            bt = jnp.pad(b_bias_s[j].T, ((0, 0), (0, 128 - B_GROUPS)))
            lg = jnp.pad(b_ln_g[j], ((0, 8 - B_GROUPS), (0, 0)))
            lb = jnp.pad(b_ln_b[j], ((0, 8 - B_GROUPS), (0, 0)))
            xn, proj = norm_matmul(h, row(mix_norm_g[i]), w_in, BF16, "b_in")
            cat = mixer_b_fwd(proj, b_w_s[j], bt, lg, lb, kv, "mixer_b")
            extra = (b_w_s[j], bt, lg, lb)
        h_mid = matmul(cat, w_out, "nn", F32, "mix_out", res=h)
        hn, gu, act, *gathered = gate_up_fwd(h_mid, row(ffn_norm_g[i]), *wl["w_gate_up"], "gate_up",
                                             rider=gather_of(i + 1) if i + 1 < depth else None)
        if gathered:
            weights[i + 1] = usable(i + 1, gathered)
        h_out = matmul(act, wl["w_down"], "nn", F32, "down", res=h_mid)
        saved.append((h, xn, proj, cat, h_mid, hn, gu, act, kv, extra))
        h = h_out

    loss_part, dh, d_final_g = loss_head(h, row(final_norm_g), tgt, "loss_head")
    loss = lax.psum(loss_part[0, 0], ("x", "y", "c"))

    d_mix_g, d_ffn_g = [None] * depth, [None] * depth
    d_sinks, d_ws, d_bias, d_lg, d_lb = [], [], [], [], []
    d_mem_n = jnp.zeros(mem2.shape, F32)
    totals = [None] * depth
    pending = None
    for i in reversed(range(depth)):
        h_in, xn, proj, cat, h_mid, hn, gu, act, kv, extra = saved[i]
        wl = weights[i]
        dgu, *from_sibling = down_dx_swiglu_bwd(dh, wl["w_down"], gu, "down_dx",
                                                rider=sibling_halves_exchange(pending) if pending else None)
        if pending:
            partial = [chip_partial_sums(g, r1, c_arr, "grads_chip_sum") for g, r1 in zip(pending, from_sibling)]
        dw_down = matmul(act, dh, "tn", F32, "down_dw", tm=1408, out_planes=("rows", 4))
        dw_gate_up = matmul((hn, 0), dgu, "tn", F32, "gate_up_dw", tn=1408, tk=2048, out_planes=("cols", 4))
        ffn = [dw_gate_up[None], dw_down[None]] if i == 0 else []
        riders = ([chips_exchange(partial)] if pending else []) + ([sibling_halves_exchange(ffn)] if ffn else [])
        rider = None if not riders else riders[0] if len(riders) == 1 else both_exchanges(*riders)
        dh, d_ffn_g[i], *landed = dx_norm_bwd(dgu, wl["w_gate_up"], h_mid, row(ffn_norm_g[i]), dh, "gate_up_dx", rider=rider)
        if pending:
            from_chips = landed[:len(pending)]
            totals[i + 1] = [shard_total(g, r1, r2, cs_arr, "grads_shard_total") for g, r1, r2 in zip(pending, from_sibling, from_chips)]
        if ffn:
            ffn_sibling = landed[-len(ffn):]
            ffn_partial = [chip_partial_sums(g, r1, c_arr, "grads_chip_sum") for g, r1 in zip(ffn, ffn_sibling)]
        dcat = matmul(dh, wl["w_out"], "nt", F32, "mix_out_dx")
        dw_out = matmul(cat, dh, "tn", F32, "mix_out_dw", out_planes=("rows", 4))
        if i % 2 == 0:
            dproj, dsk, dkv, *ffn_chips = mixer_a_bwd(proj, dcat, extra[0], kv, "mixer_a_bwd",
                                                      rider=chips_exchange(ffn_partial) if ffn else None)
            if ffn:
                ffn_totals = [shard_total(g, r1, r2, cs_arr, "grads_shard_total") for g, r1, r2 in zip(ffn, ffn_sibling, ffn_chips)]
            d_sinks.insert(0, dsk[:Q_HEADS, 0])
            dw_in = matmul(xn, dproj, "tn", F32, "a_in_dw")
        else:
            dproj, dws, dbt, dlg, dlb, dkv = mixer_b_bwd(proj, dcat, *extra, kv, "mixer_b_bwd")
            d_ws.insert(0, dws)
            d_bias.insert(0, dbt[:, :B_GROUPS].T)
            d_lg.insert(0, dlg[:B_GROUPS])
            d_lb.insert(0, dlb[:B_GROUPS])
            dw_in = matmul(xn, dproj, "tn", F32, "b_in_dw")
        dw_kv = matmul(mem_n, dkv, "tn", F32, "mem_kv_dw", out_planes=("rows", 4))
        d_mem_n = matmul(dkv, wl["w_mem_kv"], "nt", F32, "mem_kv_dx", res=d_mem_n)
        dh, d_mix_g[i] = dx_norm_bwd(dproj, wl["w_in"], h_in, row(mix_norm_g[i]), dh, "in_dx")
        pending = [_by_shard(dw_in[None], 2), dw_out[None], dw_kv[None]] + ([] if ffn else [dw_gate_up[None], dw_down[None]])
    grad_x = dh.reshape(x.shape)
    _, d_mem_g = rmsnorm_bwd(mem2, row(mem_norm_g), d_mem_n, jnp.zeros(mem2.shape, F32), "mem_norm_bwd")

    from_sibling = run_exchange(sibling_halves_exchange(pending), "grads_sibling_swap")
    partial = [chip_partial_sums(g, r1, c_arr, "grads_chip_sum") for g, r1 in zip(pending, from_sibling)]
    from_chips = run_exchange(chips_exchange(partial), "grads_chips_exchange")
    totals[0] = [shard_total(g, r1, r2, cs_arr, "grads_shard_total") for g, r1, r2 in zip(pending, from_sibling, from_chips)] + ffn_totals

    mine = {n: [None] * given[n].shape[0] for n, _ in SHARDED}
    for l in range(depth):
        for (n, k), tot in zip(layer_weights(l), totals[l]):
            mine[n][k] = tot
    mine = [jnp.concatenate(mine[n], axis=0) for n, _ in SHARDED]
    theirs = run_exchange(sibling_exchange(mine), "grads_sibling_totals")
    out = {}
    for (n, _), g_mine, g_theirs in zip(SHARDED, mine, theirs):
        shape = given[n].shape
        two_d = lambda a: a.reshape(-1, shape[-1])
        res = adamw_halves(two_d(given[n]), two_d(g_mine), two_d(g_theirs), two_d(given["m_" + n]), two_d(given["v_" + n]),
                           c_arr, shape[1], "adamw")
        out[n] = tuple(r.reshape(shape) for r in res)

    small = ("mem_norm_g", "mix_norm_g", "ffn_norm_g", "final_norm_g", "a_sinks", "b_w_s", "b_bias_s", "b_ln_g", "b_ln_b")
    small_g = [d_mem_g[0], jnp.concatenate(d_mix_g, axis=0), jnp.concatenate(d_ffn_g, axis=0), d_final_g[0],
               jnp.stack(d_sinks), jnp.stack(d_ws), jnp.stack(d_bias), jnp.stack(d_lg), jnp.stack(d_lb)]
    packed = _pack(small_g)
    g_small = sum_devices(allgather_small(packed, "small_allgather").reshape(8, *packed.shape), "small_sum")
    like = [given[n] for n in small]
    delta_s, new_m_s, new_v_s = adamw(_pack(like), g_small, _pack([given["m_" + n] for n in small]),
                                      _pack([given["v_" + n] for n in small]), "adamw_small")
    for n, g, dl, nm_, nv_ in zip(small, _unpack(g_small, like), _unpack(delta_s, like), _unpack(new_m_s, like), _unpack(new_v_s, like)):
        out[n] = (g, dl, nm_, nv_)

    order = ("mem_norm_g", "mix_norm_g", "ffn_norm_g", "final_norm_g", "a_w_in", "a_sinks", "a_w_out", "b_w_in", "b_w_s",
             "b_bias_s", "b_ln_g", "b_ln_b", "b_w_out", "w_mem_kv", "w_gate_up", "w_down")
    return (loss, grad_x, *[out[n][0] for n in order], *[out[n][1] for n in order],
            *[out[n][2] for n in order], *[out[n][3] for n in order])
```

```python
import jax
import jax.numpy as jnp
from jax import lax
from jax.experimental import pallas as pl
from jax.experimental.pallas import tpu as pltpu

F32, BF16 = jnp.float32, jnp.bfloat16
EPS = 1e-6
HEAD_DIM = 64
Q_HEADS, KV_HEADS, GROUP = 12, 2, 6
WINDOW = 128
MEM_HEADS = 4
B_GROUPS = 6
Q_W, KV_W, MEM_W, B_W = 768, 128, 256, 768
SCALE = HEAD_DIM ** -0.5
NEG = -1e30
ADAM_LR, ADAM_B1, ADAM_B2, ADAM_EPS, ADAM_WD, ADAM_STEP = 0.001, 0.9, 0.999, 1e-08, 0.01, 10
V7X_VMEM_LIMIT_BYTES = 48 * 1024 * 1024
MESH = pl.DeviceIdType.MESH
HBM_SPEC = pl.BlockSpec(memory_space=pltpu.HBM)
VMEM_SPEC = pl.BlockSpec(memory_space=pltpu.VMEM)


def _cp(*sem):
    return pltpu.CompilerParams(dimension_semantics=sem or None, vmem_limit_bytes=V7X_VMEM_LIMIT_BYTES)


def _tile(n, cands):
    for t in cands:
        if n % t == 0:
            return t
    return n


def _sds(shape, dtype):
    return jax.ShapeDtypeStruct(tuple(shape), dtype)


def _dot(a, b, ca, cb):
    return lax.dot_general(a, b, (((ca,), (cb,)), ((), ())), preferred_element_type=F32)


def _rms(x, g):
    return x * lax.rsqrt(jnp.mean(x * x, axis=-1, keepdims=True) + EPS) * g


def rmsnorm_fwd(h, g, name):
    t, d = h.shape
    tm = _tile(t, (512, 256, 128))

    def body(h_ref, g_ref, o_ref):
        o_ref[...] = _rms(h_ref[...], g_ref[...]).astype(o_ref.dtype)

    return pl.pallas_call(
        body, name=name, grid=(t // tm,),
        in_specs=[pl.BlockSpec((tm, d), lambda i: (i, 0)), pl.BlockSpec((1, d), lambda i: (0, 0))],
        out_specs=pl.BlockSpec((tm, d), lambda i: (i, 0)),
        out_shape=_sds((t, d), BF16), compiler_params=_cp("parallel"))(h, g)


def rmsnorm_bwd(h, g, dxn, dres, name):
    t, d = h.shape
    tm = _tile(t, (512, 256, 128))

    def body(h_ref, g_ref, dxn_ref, dres_ref, dh_ref, dg_ref):
        _, vjp = jax.vjp(_rms, h_ref[...], g_ref[...])
        dh, dg = vjp(dxn_ref[...].astype(F32))
        dh_ref[...] = dres_ref[...] + dh

        @pl.when(pl.program_id(0) == 0)
        def _():
            dg_ref[...] = jnp.zeros_like(dg_ref)

        dg_ref[...] += dg

    row = pl.BlockSpec((tm, d), lambda i: (i, 0))
    vec = pl.BlockSpec((1, d), lambda i: (0, 0))
    return pl.pallas_call(
        body, name=name, grid=(t // tm,), in_specs=[row, vec, row, row], out_specs=[row, vec],
        out_shape=[_sds((t, d), F32), _sds((1, d), F32)], compiler_params=_cp("arbitrary"))(h, g, dxn, dres)


def loss_head(h, g, tgt, name):
    t, d = h.shape
    tm = _tile(t, (512, 256, 128))

    def body(h_ref, g_ref, t_ref, l_ref, dh_ref, dg_ref):
        y, vjp = jax.vjp(_rms, h_ref[...], g_ref[...])
        err = y - t_ref[...]
        dh, dg = vjp(err * (1.0 / d))
        dh_ref[...] = dh
        part = 0.5 * jnp.sum(jnp.mean(err * err, axis=-1, keepdims=True), axis=0, keepdims=True)

        @pl.when(pl.program_id(0) == 0)
        def _():
            dg_ref[...] = jnp.zeros_like(dg_ref)
            l_ref[...] = jnp.zeros_like(l_ref)

        dg_ref[...] += dg
        l_ref[...] += part

    row = pl.BlockSpec((tm, d), lambda i: (i, 0))
    vec = pl.BlockSpec((1, d), lambda i: (0, 0))
    one = pl.BlockSpec((1, 1), lambda i: (0, 0))
    return pl.pallas_call(
        body, name=name, grid=(t // tm,), in_specs=[row, vec, row], out_specs=[one, row, vec],
        out_shape=[_sds((1, 1), F32), _sds((t, d), F32), _sds((1, d), F32)], compiler_params=_cp("arbitrary"))(h, g, tgt)


def _logical(op):
    arr, lead = op if isinstance(op, tuple) else (op, None)
    planes = arr.shape[-3] if arr.ndim - (lead is not None) == 3 else 1
    return arr, lead, arr.shape[-2], arr.shape[-1], planes


def _spec(op, rows_t, cols_t, row_of, col_of):
    arr, lead, _, cols, _ = _logical(op)
    per = cols // cols_t
    lead = () if lead is None else (lead,)
    if arr.ndim - len(lead) == 2:
        return pl.BlockSpec((None,) * len(lead) + (rows_t, cols_t), lambda *g: lead + (row_of(*g), col_of(*g)))
    return pl.BlockSpec((None,) * len(lead) + (None, rows_t, cols_t),
                        lambda *g: lead + (col_of(*g) // per, row_of(*g), col_of(*g) % per))


def _arr(op):
    return op[0] if isinstance(op, tuple) else op


def _resident_whole(w, d):
    wa, layer = w
    planes, per = wa.shape[-3], wa.shape[-1]

    def fill(w_ref, whole_ref):
        for s in range(planes):
            whole_ref[:, s * per:(s + 1) * per] = w_ref[s]

    return (pl.BlockSpec((None, planes, d, per), lambda i: (layer, 0, 0, 0), pipeline_mode=pl.Buffered(1)),
            pltpu.VMEM((d, planes * per), wa.dtype), fill)


def norm_matmul(h, g, w, out_dtype, name):
    t, d = h.shape
    w_spec, whole, fill = _resident_whole(w, d)
    n = whole.shape[1]
    tm = _tile(t, (512, 256, 128))

    def body(h_ref, g_ref, w_ref, xn_ref, o_ref, whole_ref):
        @pl.when(pl.program_id(0) == 0)
        def _():
            fill(w_ref, whole_ref)

        xn = _rms(h_ref[...], g_ref[...]).astype(BF16)
        xn_ref[...] = xn
        o_ref[...] = _dot(xn, whole_ref[...], 1, 0).astype(o_ref.dtype)

    return pl.pallas_call(
        body, name=name, grid=(t // tm,),
        in_specs=[pl.BlockSpec((tm, d), lambda i: (i, 0)), pl.BlockSpec((1, d), lambda i: (0, 0)), w_spec],
        out_specs=[pl.BlockSpec((tm, d), lambda i: (i, 0)), pl.BlockSpec((tm, n), lambda i: (i, 0))],
        out_shape=[_sds((t, d), BF16), _sds((t, n), out_dtype)], scratch_shapes=[whole],
        compiler_params=_cp("arbitrary"))(h, g, w[0])


def dx_norm_bwd(dy, w, h, g, dres, name, rider=None):
    t, d = h.shape
    dy_arr, dy_lead, _, kc, kp = _logical(dy)
    w_arr, w_lead, _, wc, wp = _logical(w)
    assert kc * kp == wc * wp and dy_lead is None, name
    chunk = min(kc, wc)
    tm = _tile(t, (512, 256, 128))

    def piece(ref, planes, cols, q):
        off = q * chunk % cols
        return ref[q * chunk // cols, :, off:off + chunk] if planes > 1 else ref[:, off:off + chunk]

    narrow = wp > 1 and wc % 128 != 0
    if narrow:
        assert kp == 1, name
        w_whole_spec, whole, fill = _resident_whole(w, d)

    def body(dy_ref, w_ref, h_ref, g_ref, dres_ref, dh_ref, dg_ref, *whole_ref):
        if narrow:
            @pl.when(pl.program_id(0) == 0)
            def _():
                fill(w_ref, whole_ref[0])

            dxn = _dot(dy_ref[...].astype(BF16), whole_ref[0][...], 1, 1)
        else:
            dxn = None
            for q in range(kc * kp // chunk):
                p = _dot(piece(dy_ref, kp, kc, q).astype(BF16), piece(w_ref, wp, wc, q), 1, 1)
                dxn = p if dxn is None else dxn + p
        _, vjp = jax.vjp(_rms, h_ref[...], g_ref[...])
        dh, dg = vjp(dxn)
        dh_ref[...] = dres_ref[...] + dh

        @pl.when(pl.program_id(0) == 0)
        def _():
            dg_ref[...] = jnp.zeros_like(dg_ref)

        dg_ref[...] += dg

    w_lead = () if w_lead is None else (w_lead,)
    w_block = ((wp,) if wp > 1 else ()) + (d, wc)
    w_spec = pl.BlockSpec((None,) * len(w_lead) + w_block, lambda i: w_lead + (0,) * len(w_block), pipeline_mode=pl.Buffered(1))
    if narrow:
        w_spec = w_whole_spec
    dy_spec = pl.BlockSpec((kp, tm, kc), lambda i: (0, i, 0)) if kp > 1 else pl.BlockSpec((tm, kc), lambda i: (i, 0))
    row = pl.BlockSpec((tm, d), lambda i: (i, 0))
    vec = pl.BlockSpec((1, d), lambda i: (0, 0))
    grid = (t // tm,)
    body, r_ops, r_in, r_shapes, r_out, r_scratch = with_rider(body, 5, 2, grid, rider)
    return pl.pallas_call(
        body, name=name, grid=grid, in_specs=[dy_spec, w_spec, row, vec, row] + r_in,
        out_specs=[row, vec] + r_out, out_shape=[_sds((t, d), F32), _sds((1, d), F32)] + r_shapes,
        scratch_shapes=([whole] if narrow else []) + r_scratch,
        compiler_params=_cp("arbitrary"))(dy_arr, w_arr, h, g, dres, *r_ops)


def matmul(a, b, mode, out_dtype, name, res=None, tm=None, tn=1792, tk=2816, out_planes=None, out_into=None):
    _, _, ar, ac, ap = _logical(a)
    _, _, br, bc, bp = _logical(b)
    if mode == "nn":
        m, ka, kb, n = ar, ac * ap, br, bc * bp
        n_plane, ka_plane, kb_plane = bc, ac, br
    elif mode == "nt":
        m, ka, n, kb = ar, ac * ap, br, bc * bp
        n_plane, ka_plane, kb_plane = br, ac, bc
    else:
        ka, m, kb, n = ar, ac * ap, br, bc * bp
        n_plane, ka_plane, kb_plane = bc, ar, br
    m_plane = ac if mode == "tn" else ar
    assert ka == kb, name
    k = ka
    kind, planes = out_planes or ("cols", 1)
    narrow = kind == "cols" and (n // planes) % 128 != 0
    if kind == "cols" and not narrow:
        n_plane = min(n_plane, n // planes)
    if narrow:
        tn = n
    tm = _tile(m_plane, ((1024, 1408, 512, 256, 128) if mode == "tn" else (512, 256, 128)) if tm is None else (tm, 1024, 512, 256, 128))
    if kind == "rows" and tm % (m // planes):
        tm = m_plane
    tn = _tile(n_plane, (tn, 1792, 1408, 1280, 1024, 896, 640, 512, 256, 128))
    tk = _tile(min(ka_plane, kb_plane), (tk, 2816, 1792, 1408, 1280, 1024, 512, 256, 128))
    nk = k // tk
    row_i, col_j, red = (lambda i, j, kk: i), (lambda i, j, kk: j), (lambda i, j, kk: kk)
    if mode == "nn":
        a_spec, b_spec, ca, cb = _spec(a, tm, tk, row_i, red), _spec(b, tk, tn, red, col_j), 1, 0
    elif mode == "nt":
        a_spec, b_spec, ca, cb = _spec(a, tm, tk, row_i, red), _spec(b, tn, tk, col_j, red), 1, 1
    else:
        a_spec, b_spec, ca, cb = _spec(a, tk, tm, red, row_i), _spec(b, tk, tn, red, col_j), 0, 0
    lead = () if out_into is None else (out_into[1],)
    if planes == 1:
        o_shape, o_block = (m, n), (tm, tn)
        o_index = lambda i, j, kk: lead + (i, j)
    elif narrow:
        o_shape, o_block = (planes, m, n // planes), (planes, tm, n // planes)
        o_index = lambda i, j, kk: lead + (0, i, 0)
    elif kind == "cols":
        per = n // planes // tn
        o_shape, o_block = (planes, m, n // planes), (None, tm, tn)
        o_index = lambda i, j, kk: lead + (j // per, i, j % per)
    else:
        o_shape, o_block = (planes, m // planes, n), (tm // (m // planes), m // planes, tn)
        o_index = lambda i, j, kk: lead + (i, 0, j)
    o_spec = pl.BlockSpec((None,) * len(lead) + o_block, o_index)
    if out_into is not None:
        assert out_into[0].shape[1:] == o_shape and out_into[0].dtype == out_dtype, name
        o_shape = out_into[0].shape
    has_res = res is not None
    n_in = 2 + has_res + (out_into is not None)

    def put(o_ref, v):
        if narrow:
            for s in range(planes):
                o_ref[s] = v[:, s * (n // planes):(s + 1) * (n // planes)].astype(o_ref.dtype)
        else:
            o_ref[...] = v.astype(o_ref.dtype).reshape(o_ref.shape)

    def body(*refs):
        a_ref, b_ref = refs[:2]
        rest = refs[2:2 + has_res] + refs[n_in:]
        o_ref = rest[1] if has_res else rest[0]
        p = _dot(a_ref[...].astype(BF16), b_ref[...].astype(BF16), ca, cb)
        if nk == 1:
            if has_res:
                p = p + rest[0][...]
            put(o_ref, p)
        else:
            acc_ref = rest[-1]
            kk = pl.program_id(2)

            @pl.when(kk == 0)
            def _():
                acc_ref[...] = p

            @pl.when(kk > 0)
            def _():
                acc_ref[...] += p

            @pl.when(kk == nk - 1)
            def _():
                r = acc_ref[...]
                if has_res:
                    r = r + rest[0][...]
                put(o_ref, r)

    operands = [_arr(a), _arr(b)] + ([res] if has_res else []) + ([out_into[0]] if out_into is not None else [])
    return pl.pallas_call(
        body, name=name, grid=(m // tm, n // tn, nk),
        in_specs=[a_spec, b_spec] + ([pl.BlockSpec((tm, tn), lambda i, j, kk: (i, j))] if has_res else [])
        + ([pl.BlockSpec(memory_space=pl.ANY)] if out_into is not None else []),
        out_specs=o_spec, out_shape=_sds(o_shape, out_dtype),
        input_output_aliases={n_in - 1: 0} if out_into is not None else {},
        scratch_shapes=[pltpu.VMEM((tm, tn), F32)] if nk > 1 else [],
        compiler_params=_cp("parallel", "parallel", "arbitrary"))(*operands)


def _swiglu(gate, up):
    return gate / (1.0 + jnp.exp(-gate)) * up


def gate_up_fwd(h, g, w, layer, name, rider=None):
    t, d = h.shape
    half = w.shape[-1]
    tm = _tile(t, (512, 256, 128))

    def body(h_ref, g_ref, wg_ref, wu_ref, hn_ref, gu_ref, act_ref):
        a = _rms(h_ref[...], g_ref[...]).astype(BF16)
        hn_ref[...] = a
        gate, up = _dot(a, wg_ref[...], 1, 0), _dot(a, wu_ref[...], 1, 0)
        gu_ref[0] = gate.astype(gu_ref.dtype)
        gu_ref[1] = up.astype(gu_ref.dtype)
        act_ref[...] = _swiglu(gate, up).astype(act_ref.dtype)

    grid = (2, t // tm)
    body, r_ops, r_in, r_shapes, r_out, r_scratch = with_rider(body, 4, 3, grid, rider)
    return pl.pallas_call(
        body, name=name, grid=grid,
        in_specs=[pl.BlockSpec((tm, d), lambda j, i: (i, 0)), pl.BlockSpec((1, d), lambda j, i: (0, 0)),
                  pl.BlockSpec((None, None, d, half), lambda j, i: (layer, j, 0, 0)),
                  pl.BlockSpec((None, None, d, half), lambda j, i: (layer, 2 + j, 0, 0))] + r_in,
        out_specs=[pl.BlockSpec((None, tm, d), lambda j, i: (j, i, 0)), pl.BlockSpec((2, tm, half), lambda j, i: (0, i, j)),
                   pl.BlockSpec((tm, half), lambda j, i: (i, j))] + r_out,
        out_shape=[_sds((2, t, d), BF16), _sds((2, t, 2 * half), BF16), _sds((t, 2 * half), BF16)] + r_shapes,
        scratch_shapes=r_scratch, compiler_params=_cp("arbitrary", "arbitrary"))(h, g, w, w, *r_ops)


def down_dx_swiglu_bwd(dh, wd, gu, name, rider=None):
    t, d = dh.shape
    w, layer = wd
    f = w.shape[-2]
    tm = _tile(t, (512, 256, 128))
    tn = _tile(f, (1408, 512, 256, 128))

    def body(dh_ref, w_ref, gu_ref, o_ref):
        dact = _dot(dh_ref[...].astype(BF16), w_ref[...], 1, 1)
        gate, up = gu_ref[0].astype(F32), gu_ref[1].astype(F32)
        sig = 1.0 / (1.0 + jnp.exp(-gate))
        silu = gate * sig
        o_ref[0] = (dact * up * (sig + silu * (1.0 - sig))).astype(o_ref.dtype)
        o_ref[1] = (dact * silu).astype(o_ref.dtype)

    planes = pl.BlockSpec((2, tm, tn), lambda j, i: (0, i, j))
    grid = (f // tn, t // tm)
    body, r_ops, r_in, r_shapes, r_out, r_scratch = with_rider(body, 3, 1, grid, rider)
    return pl.pallas_call(
        body, name=name, grid=grid,
        in_specs=[pl.BlockSpec((tm, d), lambda j, i: (i, 0)), pl.BlockSpec((None, tn, d), lambda j, i: (layer, j, 0)), planes] + r_in,
        out_specs=[planes] + r_out, out_shape=[_sds((2, t, f), BF16)] + r_shapes, scratch_shapes=r_scratch,
        compiler_params=_cp("arbitrary", "arbitrary"))(dh, w, gu, *r_ops)


def _softmax_over_keys(s, sink=None):
    m = s.max(axis=0, keepdims=True)
    if sink is not None:
        m = jnp.maximum(m, sink)
    m = lax.stop_gradient(m)
    e = jnp.exp(s - m)
    den = e.sum(axis=0, keepdims=True)
    if sink is not None:
        den = den + jnp.exp(sink - m)
    return e * (1.0 / den)


def _low_lanes():
    return lax.broadcasted_iota(jnp.int32, (1, 128), 1) < HEAD_DIM


def _stack_heads(slabs):
    low = _low_lanes()
    return jnp.concatenate([p for s in slabs for p in (jnp.where(low, s, 0.0), jnp.where(low, 0.0, s))], axis=0)


def _unstack_heads(o, n_slabs):
    low = _low_lanes()
    return [jnp.where(low, o[2 * j * WINDOW:(2 * j + 1) * WINDOW], o[(2 * j + 1) * WINDOW:(2 * j + 2) * WINDOW])
            for j in range(n_slabs)]


def _swa_group(q_slabs, k_both, v_both, sinks, mask):
    qs = _stack_heads(q_slabs).astype(BF16)
    s = jnp.where(mask, _dot(k_both.astype(BF16), qs, 1, 1) * SCALE, NEG)
    sink = jnp.concatenate([jnp.broadcast_to(v, (1, WINDOW)) for v in sinks], axis=1)
    return _unstack_heads(_dot(_softmax_over_keys(s, sink).astype(BF16), v_both.astype(BF16), 0, 0), len(q_slabs))


def _mem_pair(q_slab, k_slab, v_slab):
    s = _dot(k_slab.astype(BF16), _stack_heads([q_slab]).astype(BF16), 1, 1) * SCALE
    return _unstack_heads(_dot(_softmax_over_keys(s).astype(BF16), v_slab.astype(BF16), 0, 0), 1)[0]


def _gelu(x):
    return 0.5 * x * (1.0 + jnp.tanh(0.7978845608028654 * (x + 0.044715 * (x * x * x))))


def _gmlp_group(zu, zv, w, bcol, lg, lb, tri):
    u, v = _gelu(zu), _gelu(zv)
    mu = jnp.mean(v, axis=-1, keepdims=True)
    var = jnp.mean(jnp.square(v - mu), axis=-1, keepdims=True)
    vn = (v - mu) * lax.rsqrt(var + EPS) * lg + lb
    sv = _dot(jnp.where(tri, w, 0.0).astype(BF16), vn.astype(BF16), 1, 0) + bcol
    return u * sv


def _cols(x, width):
    return [x[:, j * width:(j + 1) * width] for j in range(x.shape[1] // width)]


def _swa_mask(has_prev):
    qi = lax.broadcasted_iota(jnp.int32, (2 * WINDOW, GROUP * WINDOW), 1) & (WINDOW - 1)
    kj = lax.broadcasted_iota(jnp.int32, (2 * WINDOW, GROUP * WINDOW), 0)
    in_prev = jnp.logical_and(jnp.logical_and(kj < WINDOW, kj > qi), has_prev)
    return jnp.logical_or(in_prev, jnp.logical_and(kj >= WINDOW, kj - WINDOW <= qi))


def _mix_a(q_slabs, k_boths, v_boths, sinks, qm_slabs, km_slabs, vm_slabs, mask):
    per = GROUP // 2
    outs = []
    for g in range(KV_HEADS):
        outs += _swa_group(q_slabs[per * g:per * (g + 1)], k_boths[g], v_boths[g], sinks[GROUP * g:GROUP * (g + 1)], mask)
    return outs + [_mem_pair(qm_slabs[j], km_slabs[j], vm_slabs[j]) for j in range(MEM_HEADS // 2)]


def _in_both_halves(prev, cur):
    cat = jnp.concatenate([prev, cur], axis=0)
    rolled = pltpu.roll(cat, HEAD_DIM, axis=1)
    low = _low_lanes()
    return [jnp.where(low, cat, rolled), jnp.where(low, rolled, cat)]


def _from_both_halves(d_boths):
    t = [d + pltpu.roll(d, HEAD_DIM, axis=1) for d in d_boths]
    return jnp.where(_low_lanes(), t[0], t[1])


def _mix_a_specs(nm, blk):
    prev = lambda n: jnp.maximum(blk(n) - 1, 0)
    return [pl.BlockSpec((WINDOW, Q_W), lambda n: (blk(n), 0)),
            pl.BlockSpec((WINDOW, KV_W), lambda n: (prev(n), Q_W // KV_W)),
            pl.BlockSpec((WINDOW, KV_W), lambda n: (blk(n), Q_W // KV_W)),
            pl.BlockSpec((WINDOW, KV_W), lambda n: (prev(n), Q_W // KV_W + 1)),
            pl.BlockSpec((WINDOW, KV_W), lambda n: (blk(n), Q_W // KV_W + 1)),
            pl.BlockSpec((WINDOW, MEM_W), lambda n: (blk(n), (Q_W + 2 * KV_W) // MEM_W)),
            pl.BlockSpec((16, 128), lambda n: (0, 0)),
            pl.BlockSpec((nm, MEM_W), lambda n: (0, 0)),
            pl.BlockSpec((nm, MEM_W), lambda n: (0, 1))]


def _mix_a_args(refs):
    q, kp, kc, vp, vc, qm, sk, km, vm = [r[...].astype(F32) for r in refs]
    return (_cols(q, 128), _in_both_halves(kp, kc), _in_both_halves(vp, vc), [sk[h:h + 1, 0:1] for h in range(Q_HEADS)],
            _cols(qm, 128), _cols(km, 128), _cols(vm, 128))


def mixer_a_fwd(proj, sk, kv, name, rider=None):
    t, nm = proj.shape[0], kv.shape[0]

    def body(*refs):
        o_ref = refs[-1]
        slabs = _mix_a(*_mix_a_args(refs[:-1]), _swa_mask(pl.program_id(0) > 0))
        o_ref[...] = jnp.concatenate(slabs, axis=1).astype(o_ref.dtype)

    grid = (t // WINDOW,)
    body, r_ops, r_in, r_shapes, r_out, r_scratch = with_rider(body, 9, 1, grid, rider)
    return pl.pallas_call(
        body, name=name, grid=grid, in_specs=_mix_a_specs(nm, lambda n: n) + r_in,
        out_specs=[pl.BlockSpec((WINDOW, Q_W + MEM_W), lambda n: (n, 0))] + r_out,
        out_shape=[_sds((t, Q_W + MEM_W), BF16)] + r_shapes, scratch_shapes=r_scratch,
        compiler_params=_cp("arbitrary"))(proj, proj, proj, proj, proj, proj, sk, kv, kv, *r_ops)


def _onehot_rows(vals, shape):
    rows = lax.broadcasted_iota(jnp.int32, shape, 0)
    out = jnp.zeros(shape, F32)
    for h, v in enumerate(vals):
        out = out + jnp.where(rows == h, jnp.broadcast_to(v, shape), 0.0)
    return out


def mixer_a_bwd(proj, dcat, sk, kv, name, rider=None):
    t, nm = proj.shape[0], kv.shape[0]
    nb = t // WINDOW
    blk = lambda i: nb - 1 - i

    def body(*refs):
        dcat_ref, dproj_ref, dsk_ref, dkv_ref, carry_ref = refs[9:]
        i = pl.program_id(0)

        @pl.when(i == 0)
        def _():
            carry_ref[...] = jnp.zeros_like(carry_ref)
            dsk_ref[...] = jnp.zeros_like(dsk_ref)
            dkv_ref[...] = jnp.zeros_like(dkv_ref)

        mask = _swa_mask(blk(i) > 0)
        _, vjp = jax.vjp(lambda *a: _mix_a(*a, mask), *_mix_a_args(refs[:9]))
        dqs, dk_boths, dv_boths, dsinks, dqms, dkms, dvms = vjp(_cols(dcat_ref[...].astype(F32), 128))
        dkv = jnp.concatenate([_from_both_halves(dk_boths), _from_both_halves(dv_boths)], axis=1)
        dkv_cur = dkv[WINDOW:] + carry_ref[...]
        carry_ref[...] = dkv[:WINDOW]
        dproj_ref[...] = jnp.concatenate(dqs + [dkv_cur] + dqms, axis=1).astype(dproj_ref.dtype)
        dsk_ref[...] += _onehot_rows(dsinks, (16, 128))
        dkv_ref[...] += jnp.concatenate(dkms + dvms, axis=1)

    width = Q_W + 2 * KV_W + MEM_W
    body, r_ops, r_in, r_shapes, r_out, r_scratch = with_rider(body, 10, 3, (nb,), rider)
    return pl.pallas_call(
        body, name=name, grid=(nb,),
        in_specs=_mix_a_specs(nm, blk) + [pl.BlockSpec((WINDOW, Q_W + MEM_W), lambda i: (blk(i), 0))] + r_in,
        out_specs=[pl.BlockSpec((WINDOW, width), lambda i: (blk(i), 0)), pl.BlockSpec((16, 128), lambda i: (0, 0)),
                   pl.BlockSpec((nm, 2 * MEM_W), lambda i: (0, 0))] + r_out,
        out_shape=[_sds((t, width), BF16), _sds((16, 128), F32), _sds((nm, 2 * MEM_W), F32)] + r_shapes,
        scratch_shapes=[pltpu.VMEM((WINDOW, 2 * KV_W), F32)] + r_scratch,
        compiler_params=_cp("arbitrary"))(proj, proj, proj, proj, proj, proj, sk, kv, kv, dcat, *r_ops)


def _mix_b(zus, zvs, ws, bcols, lgs, lbs, qms, kms, vms, tri):
    outs = [_gmlp_group(zus[g], zvs[g], ws[g], bcols[g], lgs[g], lbs[g], tri) for g in range(B_GROUPS)]
    return outs + [_mem_pair(qms[j], kms[j], vms[j]) for j in range(MEM_HEADS // 2)]


def _mix_b_specs(nm):
    return [pl.BlockSpec((WINDOW, 2 * B_W), lambda n: (n, 0)),
            pl.BlockSpec((WINDOW, MEM_W), lambda n: (n, 2 * B_W // MEM_W)),
            pl.BlockSpec((B_GROUPS, WINDOW, WINDOW), lambda n: (0, 0, 0)),
            pl.BlockSpec((WINDOW, 128), lambda n: (0, 0)),
            pl.BlockSpec((8, 128), lambda n: (0, 0)),
            pl.BlockSpec((8, 128), lambda n: (0, 0)),
            pl.BlockSpec((nm, MEM_W), lambda n: (0, 0)),
            pl.BlockSpec((nm, MEM_W), lambda n: (0, 1))]


def _mix_b_args(refs):
    z, qm, ws, bt, lg, lb, km, vm = [r[...].astype(F32) for r in refs]
    zs = _cols(z, 128)
    return (zs[:B_GROUPS], zs[B_GROUPS:], [ws[g] for g in range(B_GROUPS)], [bt[:, g:g + 1] for g in range(B_GROUPS)],
            [lg[g:g + 1, :] for g in range(B_GROUPS)], [lb[g:g + 1, :] for g in range(B_GROUPS)],
            _cols(qm, 128), _cols(km, 128), _cols(vm, 128))


def _tri():
    return lax.broadcasted_iota(jnp.int32, (WINDOW, WINDOW), 0) >= lax.broadcasted_iota(jnp.int32, (WINDOW, WINDOW), 1)


def mixer_b_fwd(proj, ws, bt, lg, lb, kv, name):
    t, nm = proj.shape[0], kv.shape[0]

    def body(*refs):
        o_ref = refs[-1]
        o_ref[...] = jnp.concatenate(_mix_b(*_mix_b_args(refs[:-1]), _tri()), axis=1).astype(o_ref.dtype)

    return pl.pallas_call(
        body, name=name, grid=(t // WINDOW,), in_specs=_mix_b_specs(nm),
        out_specs=pl.BlockSpec((WINDOW, B_W + MEM_W), lambda n: (n, 0)),
        out_shape=_sds((t, B_W + MEM_W), BF16), compiler_params=_cp("parallel"))(proj, proj, ws, bt, lg, lb, kv, kv)


def mixer_b_bwd(proj, dcat, ws, bt, lg, lb, kv, name):
    t, nm = proj.shape[0], kv.shape[0]

    def body(*refs):
        dcat_ref, dproj_ref, dws_ref, dbt_ref, dlg_ref, dlb_ref, dkv_ref = refs[8:]

        @pl.when(pl.program_id(0) == 0)
        def _():
            for r in (dws_ref, dbt_ref, dlg_ref, dlb_ref, dkv_ref):
                r[...] = jnp.zeros_like(r)

        tri = _tri()
        zus, zvs, ws, bcols, lgs, lbs, qms, kms, vms = _mix_b_args(refs[:8])
        douts = _cols(dcat_ref[...].astype(F32), 128)
        grads = []
        for g in range(B_GROUPS):
            _, vjp = jax.vjp(lambda *a: _gmlp_group(*a, tri), zus[g], zvs[g], ws[g], bcols[g], lgs[g], lbs[g])
            grads.append(vjp(douts[g]))
        dzus, dzvs, dws, dbcols, dlgs, dlbs = [list(t) for t in zip(*grads)]
        grads = []
        for j in range(MEM_HEADS // 2):
            _, vjp = jax.vjp(_mem_pair, qms[j], kms[j], vms[j])
            grads.append(vjp(douts[B_GROUPS + j]))
        dqms, dkms, dvms = [list(t) for t in zip(*grads)]
        dproj_ref[...] = jnp.concatenate(dzus + dzvs + dqms, axis=1).astype(dproj_ref.dtype)
        for g in range(B_GROUPS):
            dws_ref[g] += dws[g]
        lanes = lax.broadcasted_iota(jnp.int32, (WINDOW, 128), 1)
        dbt = jnp.zeros((WINDOW, 128), F32)
        for g in range(B_GROUPS):
            dbt = dbt + jnp.where(lanes == g, jnp.broadcast_to(dbcols[g], (WINDOW, 128)), 0.0)
        dbt_ref[...] += dbt
        dlg_ref[...] += _onehot_rows(dlgs, (8, 128))
        dlb_ref[...] += _onehot_rows(dlbs, (8, 128))
        dkv_ref[...] += jnp.concatenate(dkms + dvms, axis=1)

    width = 2 * B_W + MEM_W
    const2 = lambda n: (0, 0)
    return pl.pallas_call(
        body, name=name, grid=(t // WINDOW,),
        in_specs=_mix_b_specs(nm) + [pl.BlockSpec((WINDOW, B_W + MEM_W), lambda n: (n, 0))],
        out_specs=[pl.BlockSpec((WINDOW, width), lambda n: (n, 0)),
                   pl.BlockSpec((B_GROUPS, WINDOW, WINDOW), lambda n: (0, 0, 0)),
                   pl.BlockSpec((WINDOW, 128), const2), pl.BlockSpec((8, 128), const2), pl.BlockSpec((8, 128), const2),
                   pl.BlockSpec((nm, 2 * MEM_W), const2)],
        out_shape=[_sds((t, width), BF16), _sds((B_GROUPS, WINDOW, WINDOW), F32), _sds((WINDOW, 128), F32),
                   _sds((8, 128), F32), _sds((8, 128), F32), _sds((nm, 2 * MEM_W), F32)],
        compiler_params=_cp("arbitrary"))(proj, proj, ws, bt, lg, lb, kv, kv, dcat)


def _adamw_update(w, g, m, v):
    m2 = ADAM_B1 * m + (1.0 - ADAM_B1) * g
    v2 = ADAM_B2 * v + (1.0 - ADAM_B2) * jnp.square(g)
    m_hat = m2 / (1.0 - ADAM_B1 ** ADAM_STEP)
    v_hat = v2 / (1.0 - ADAM_B2 ** ADAM_STEP)
    return -ADAM_LR * (m_hat / (jnp.sqrt(v_hat) + ADAM_EPS) + ADAM_WD * w), m2, v2


def adamw(w, g, m, v, name):
    r, c = w.shape
    tr = _tile(r, (512, 352, 256, 128, 64, 32, 16, 8))

    def body(w_ref, g_ref, m_ref, v_ref, d_ref, nm_ref, nv_ref):
        d_ref[...], nm_ref[...], nv_ref[...] = _adamw_update(w_ref[...], g_ref[...], m_ref[...], v_ref[...])

    spec = pl.BlockSpec((tr, c), lambda i: (i, 0))
    return pl.pallas_call(
        body, name=name, grid=(r // tr,), in_specs=[spec] * 4, out_specs=[spec] * 3,
        out_shape=[_sds((r, c), F32)] * 3, compiler_params=_cp("parallel"))(w, g, m, v)


def adamw_halves(w, g_mine, g_theirs, m, v, c_arr, rows, name):
    r, c = w.shape
    tr = _tile(rows // 2, (256, 352, 128, 64, 32, 16, 8))
    per_half = rows // 2 // tr

    def body(c_ref, w_ref, gm_ref, gt_ref, m_ref, v_ref, g_ref, d_ref, nm_ref, nv_ref):
        g = jnp.where(pl.program_id(0) // per_half % 2 == c_ref[0], gm_ref[...], gt_ref[...])
        g_ref[...] = g
        d_ref[...], nm_ref[...], nv_ref[...] = _adamw_update(w_ref[...], g, m_ref[...], v_ref[...])

    spec = pl.BlockSpec((tr, c), lambda i, cr: (i, 0))
    half = pl.BlockSpec((tr, c), lambda i, cr: (i // (2 * per_half) * per_half + i % per_half, 0))
    return pl.pallas_call(
        body, name=name,
        grid_spec=pltpu.PrefetchScalarGridSpec(num_scalar_prefetch=1, grid=(r // tr,), in_specs=[spec, half, half, spec, spec],
                                               out_specs=[spec] * 4),
        out_shape=[_sds((r, c), F32)] * 4, compiler_params=_cp("parallel"))(c_arr, w, g_mine, g_theirs, m, v)


def _place():
    return lax.axis_index("x"), lax.axis_index("y"), lax.axis_index("c")


def _other_chips(x, y):
    return [(1 - x, y), (x, 1 - y), (1 - x, 1 - y)]


def _remote(src, dst, send_sems, recv_sems, k, dev):
    return pltpu.make_async_remote_copy(src_ref=src, dst_ref=dst, send_sem=send_sems.at[k], recv_sem=recv_sems.at[k],
                                        device_id=dev, device_id_type=MESH)


class Exchange:
    def __init__(self, ins, out_shapes, n_sems, start, finish):
        self.ins, self.out_shapes, self.start, self.finish = list(ins), list(out_shapes), start, finish
        self.sems = [n_sems, n_sems] if isinstance(n_sems, int) else list(n_sems)

    def scratch(self):
        return [pltpu.SemaphoreType.DMA((n,)) for n in self.sems]


def both_exchanges(a, b):
    ni, no, ns = len(a.ins), len(a.out_shapes), len(a.sems)

    def start(ins, outs, *sems):
        a.start(ins[:ni], outs[:no], *sems[:ns])
        b.start(ins[ni:], outs[no:], *sems[ns:])

    def finish(ins, outs, *sems):
        a.finish(ins[:ni], outs[:no], *sems[:ns])
        b.finish(ins[ni:], outs[no:], *sems[ns:])

    return Exchange(a.ins + b.ins, a.out_shapes + b.out_shapes, a.sems + b.sems, start, finish)


def run_exchange(ex, name):
    ni, no = len(ex.ins), len(ex.out_shapes)

    def body(*refs):
        ex.start(refs[:ni], refs[ni:ni + no], *refs[ni + no:])
        ex.finish(refs[:ni], refs[ni:ni + no], *refs[ni + no:])

    return pl.pallas_call(
        body, name=name, in_specs=[HBM_SPEC] * ni, out_specs=[HBM_SPEC] * no, out_shape=ex.out_shapes, scratch_shapes=ex.scratch(),
        compiler_params=pltpu.CompilerParams(has_side_effects=True))(*ex.ins)


def with_rider(body, n_in, n_out, grid, ex):
    if ex is None:
        return body, [], [], [], [], []
    ni, no, ns = len(ex.ins), len(ex.out_shapes), len(ex.sems)

    def riding(*refs):
        r_in, r_out, sems = refs[n_in:n_in + ni], refs[n_in + ni + n_out:n_in + ni + n_out + no], refs[-ns:]
        first = last = None
        for axis, size in enumerate(grid):
            at_first, at_last = pl.program_id(axis) == 0, pl.program_id(axis) == size - 1
            first = at_first if first is None else jnp.logical_and(first, at_first)
            last = at_last if last is None else jnp.logical_and(last, at_last)

        @pl.when(first)
        def _():
            ex.start(r_in, r_out, *sems)

        body(*refs[:n_in], *refs[n_in + ni:n_in + ni + n_out], *refs[n_in + ni + n_out + no:-ns])

        @pl.when(last)
        def _():
            ex.finish(r_in, r_out, *sems)

    return riding, ex.ins, [HBM_SPEC] * ni, ex.out_shapes, [HBM_SPEC] * no, ex.scratch()


def gather_exchange(shards):
    nw = len(shards)
    entry = [k for _, k in shards]

    def rows(ref, cc):
        return pl.ds(cc * (ref.shape[1] // 2), ref.shape[1] // 2)

    def sent(ins, outs, send_sems, recv_sems, w, j):
        x, y, c = _place()
        return _remote(ins[w].at[pl.ds(entry[w], 1), rows(ins[w], c)], outs[w].at[:, 2 * x + y, rows(ins[w], c)], send_sems, recv_sems,
                       7 * w + j, (*_other_chips(x, y)[j], c))

    def landed(ins, outs, send_sems, recv_sems, w, j, cc, to):
        x, y, c = _place()
        chip = _other_chips(x, y)[j]
        blk = outs[w].at[:, 2 * chip[0] + chip[1], rows(ins[w], cc)]
        return _remote(blk, blk, send_sems, recv_sems, 7 * w + (j if to is None else 3 + j), (x, y, c) if to is None else to)

    def own(ins, outs, send_sems, recv_sems, w):
        x, y, c = _place()
        return _remote(ins[w].at[pl.ds(entry[w], 1)], outs[w].at[:, 2 * x + y], send_sems, recv_sems, 7 * w + 6, (x, y, 1 - c))

    def start(ins, outs, send_sems, recv_sems):
        for j in range(3):
            for w in range(nw):
                sent(ins, outs, send_sems, recv_sems, w, j).start()
        for w in range(nw):
            own(ins, outs, send_sems, recv_sems, w).start()

    def finish(ins, outs, send_sems, recv_sems):
        x, y, c = _place()
        for j in range(3):
            for w in range(nw):
                landed(ins, outs, send_sems, recv_sems, w, j, c, None).wait_recv()
                landed(ins, outs, send_sems, recv_sems, w, j, c, (x, y, 1 - c)).start()
        for w in range(nw):
            own(ins, outs, send_sems, recv_sems, w).wait()
        for j in range(3):
            for w in range(nw):
                landed(ins, outs, send_sems, recv_sems, w, j, 1 - c, (x, y, c)).wait_recv()
        for j in range(3):
            for w in range(nw):
                sent(ins, outs, send_sems, recv_sems, w, j).wait_send()
                landed(ins, outs, send_sems, recv_sems, w, j, c, (x, y, 1 - c)).wait_send()

    return Exchange([s for s, _ in shards], [_sds((1, 4) + s.shape[1:], s.dtype) for s, _ in shards], 7 * nw, start, finish)


def copies_exchange(ins, out_shapes, n_sems, copies):
    def start(*refs):
        for cp in copies(*refs):
            cp.start()

    def finish(*refs):
        for cp in copies(*refs):
            cp.wait()

    return Exchange(ins, out_shapes, n_sems, start, finish)


def sibling_halves_exchange(gs):
    def copies(ins, outs, send_sems, recv_sems):
        x, y, c = _place()
        return [_remote(g.at[:, :, pl.ds((1 - c) * (g.shape[2] // 2), g.shape[2] // 2)], o, send_sems, recv_sems, w, (x, y, 1 - c))
                for w, (g, o) in enumerate(zip(ins, outs))]

    return copies_exchange(gs, [_sds(g.shape[:2] + (g.shape[2] // 2, g.shape[3]), g.dtype) for g in gs], len(gs), copies)


def chips_exchange(sbs):
    def copies(ins, outs, send_sems, recv_sems):
        x, y, c = _place()
        return [_remote(s.at[:, 2 * chip[0] + chip[1]], o.at[j], send_sems, recv_sems, 3 * w + j, (*chip, c))
                for j, chip in enumerate(_other_chips(x, y)) for w, (s, o) in enumerate(zip(ins, outs))]

    return copies_exchange(sbs, [_sds((3, s.shape[0]) + s.shape[2:], s.dtype) for s in sbs], 3 * len(sbs), copies)


def sibling_exchange(fs):
    def copies(ins, outs, send_sems, recv_sems):
        x, y, c = _place()
        return [_remote(f, o, send_sems, recv_sems, w, (x, y, 1 - c)) for w, (f, o) in enumerate(zip(ins, outs))]

    return copies_exchange(fs, [_sds(f.shape, f.dtype) for f in fs], len(fs), copies)


def _half_tile(a):
    return _tile(a, (256, 352, 176, 128, 64, 32, 16))


def chip_partial_sums(g, r1, c_arr, name):
    nl, _, a2, b = r1.shape
    ta = _half_tile(a2)
    per = a2 // ta

    def body(c_ref, g_ref, r_ref, o_ref):
        o_ref[...] = (g_ref[...] + r_ref[...]).astype(o_ref.dtype)

    blk = (None, None, ta, b)
    return pl.pallas_call(
        body, name=name,
        grid_spec=pltpu.PrefetchScalarGridSpec(
            num_scalar_prefetch=1, grid=(nl, 4, per),
            in_specs=[pl.BlockSpec(blk, lambda l, s, i, c: (l, s, c[0] * per + i, 0)), pl.BlockSpec(blk, lambda l, s, i, c: (l, s, i, 0))],
            out_specs=pl.BlockSpec(blk, lambda l, s, i, c: (l, s, i, 0))),
        out_shape=_sds(r1.shape, BF16), compiler_params=_cp("parallel", "parallel", "parallel"))(c_arr, g, r1)


def shard_total(g, r1, r2, cs_arr, name):
    nl, _, a2, b = r1.shape
    ta = _half_tile(a2)
    per = a2 // ta

    def body(cs_ref, g_ref, r1_ref, p0_ref, p1_ref, p2_ref, o_ref):
        o_ref[...] = (((g_ref[...] + r1_ref[...]) + p0_ref[...].astype(F32)) + p1_ref[...].astype(F32)) + p2_ref[...].astype(F32)

    blk4, blk3 = (None, None, ta, b), (None, ta, b)
    peer = lambda k: pl.BlockSpec((None, None, ta, b), lambda l, i, cs: (k, l, i, 0))
    return pl.pallas_call(
        body, name=name,
        grid_spec=pltpu.PrefetchScalarGridSpec(
            num_scalar_prefetch=1, grid=(nl, per),
            in_specs=[pl.BlockSpec(blk4, lambda l, i, cs: (l, cs[1], cs[0] * per + i, 0)),
                      pl.BlockSpec(blk4, lambda l, i, cs: (l, cs[1], i, 0)), peer(0), peer(1), peer(2)],
            out_specs=pl.BlockSpec(blk3, lambda l, i, cs: (l, i, 0))),
        out_shape=_sds((nl, a2, b), F32), compiler_params=_cp("parallel", "parallel"))(cs_arr, g, r1, r2, r2, r2)


def allgather_small(v, name):
    r, n = v.shape

    def body(x_ref, out_ref, send_sems, recv_sems, local_sem):
        x, y, c = _place()
        me, sibling = (x, y, c), (x, y, 1 - c)
        chips = _other_chips(x, y)

        def rows(px, py, pc):
            return out_ref.at[pl.ds((4 * px + 2 * py + pc) * r, r), :]

        def copy(k, block, to, src=None):
            return _remote(rows(*block) if src is None else src, rows(*block), send_sems, recv_sems, k, to)

        mine = pltpu.make_async_copy(x_ref, rows(*me), local_sem)
        mine.start()
        first = [copy(0, me, sibling, src=x_ref)] + [copy(1 + j, me, (*chip, c), src=x_ref) for j, chip in enumerate(chips)]
        for cp in first:
            cp.start()
        passed = [copy(4 + j, (*chip, c), sibling) for j, chip in enumerate(chips)]
        for j, chip in enumerate(chips):
            copy(1 + j, (*chip, c), me).wait_recv()
            passed[j].start()
        copy(0, sibling, me).wait_recv()
        for j, chip in enumerate(chips):
            copy(4 + j, (*chip, 1 - c), me).wait_recv()
        for cp in first + passed:
            cp.wait_send()
        mine.wait()

    return pl.pallas_call(
        body, name=name, in_specs=[VMEM_SPEC], out_specs=VMEM_SPEC, out_shape=_sds((8 * r, n), v.dtype),
        scratch_shapes=[pltpu.SemaphoreType.DMA((7,)), pltpu.SemaphoreType.DMA((7,)), pltpu.SemaphoreType.DMA],
        compiler_params=pltpu.CompilerParams(has_side_effects=True, vmem_limit_bytes=V7X_VMEM_LIMIT_BYTES))(v)


def sum_devices(v8, name):
    _, r, n = v8.shape
    tr = _tile(r, (88, 64, 32, 16, 8))

    def body(v_ref, o_ref):
        acc = v_ref[0]
        for d in range(1, 8):
            acc = acc + v_ref[d]
        o_ref[...] = acc

    return pl.pallas_call(
        body, name=name, grid=(r // tr,), in_specs=[pl.BlockSpec((8, tr, n), lambda i: (0, i, 0))],
        out_specs=pl.BlockSpec((tr, n), lambda i: (i, 0)), out_shape=_sds((r, n), F32), compiler_params=_cp("parallel"))(v8)


SHARDED = (("a_w_in", 2), ("a_w_out", 1), ("b_w_in", 2), ("b_w_out", 1), ("w_mem_kv", 1), ("w_gate_up", 2), ("w_down", 1))


def _usable(wg, axis):
    l, _, a, b = wg.shape
    return wg.reshape(l, 4 * a, b) if axis == 1 else wg


def _pack(arrs):
    parts = []
    for a in arrs:
        flat = a.reshape(-1)
        flat = jnp.pad(flat, (0, -flat.shape[0] % 1024))
        parts.append(flat.reshape(-1, 128))
    return jnp.concatenate(parts, axis=0)


def _unpack(buf, like):
    out, row = [], 0
    for a in like:
        size = 1
        for s in a.shape:
            size *= s
        rows = -(-size // 1024) * 8
        out.append(buf[row:row + rows].reshape(-1)[:size].reshape(a.shape))
        row += rows
    return out


def kernel(x, mem, mem_norm_g, mix_norm_g, ffn_norm_g, final_norm_g, a_w_in, a_sinks, a_w_out, b_w_in, b_w_s, b_bias_s, b_ln_g, b_ln_b, b_w_out, w_mem_kv, w_gate_up, w_down, loss_target, m_mem_norm_g, m_mix_norm_g, m_ffn_norm_g, m_final_norm_g, m_a_w_in, m_a_sinks, m_a_w_out, m_b_w_in, m_b_w_s, m_b_bias_s, m_b_ln_g, m_b_ln_b, m_b_w_out, m_w_mem_kv, m_w_gate_up, m_w_down, v_mem_norm_g, v_mix_norm_g, v_ffn_norm_g, v_final_norm_g, v_a_w_in, v_a_sinks, v_a_w_out, v_b_w_in, v_b_w_s, v_b_bias_s, v_b_ln_g, v_b_ln_b, v_b_w_out, v_w_mem_kv, v_w_gate_up, v_w_down):
    given = dict(locals())
    depth = mix_norm_g.shape[0]
    d = x.shape[-1]
    xi, yi, ci = _place()
    c_arr = jnp.stack([ci]).astype(jnp.int32)
    cs_arr = jnp.stack([ci, 2 * xi + yi]).astype(jnp.int32)

    axis_of = dict(SHARDED)
    own = {n: given[n].astype(BF16) for n, _ in SHARDED}
    MIXER, FFN = slice(0, 3), slice(3, 5)

    def layer_weights(l):
        mix = "a" if l % 2 == 0 else "b"
        return [(mix + "_w_in", l // 2), (mix + "_w_out", l // 2), ("w_mem_kv", l), ("w_gate_up", l), ("w_down", l)]

    def gather_of(l, part=slice(0, 5)):
        return gather_exchange([(own[n], k) for n, k in layer_weights(l)[part]])

    def usable(l, gathered, part=slice(0, 5)):
        return {n[2:] if n[0] in "ab" else n: (_usable(wg, axis_of[n]), 0) for (n, _), wg in zip(layer_weights(l)[part], gathered)}

    weights = {0: usable(0, run_exchange(gather_of(0, MIXER), "gather_weights"), MIXER)}

    h = x.reshape(-1, d)
    tgt = loss_target.reshape(-1, d)
    mem2 = mem.reshape(-1, d)
    row = lambda v: v.reshape(1, -1)

    mem_n = rmsnorm_fwd(mem2, row(mem_norm_g), "mem_norm")
    saved = []
    for i in range(depth):
        j = i // 2
        wl = weights[i]
        w_in, w_out = wl["w_in"], wl["w_out"]
        kv = matmul(mem_n, wl["w_mem_kv"], "nn", BF16, "mem_kv")
        if i % 2 == 0:
            sk = jnp.pad(jnp.broadcast_to(a_sinks[j][:, None], (Q_HEADS, 128)), ((0, 16 - Q_HEADS), (0, 0)))
            xn, proj = norm_matmul(h, row(mix_norm_g[i]), w_in, BF16, "a_in")
            cat, *gathered = mixer_a_fwd(proj, sk, kv, "mixer_a", rider=gather_of(0, FFN) if i == 0 else None)
            if gathered:
                wl.update(usable(0, gathered, FFN))
            extra = (sk,)
        else:
            bt = jnp.pad(b_bias_s[j].T, ((0, 0), (0, 128 - B_GROUPS)))
            lg = jnp.pad(b_ln_g[j], ((0, 8 - B_GROUPS), (0, 0)))
            lb = jnp.pad(b_ln_b[j], ((0, 8 - B_GROUPS), (0, 0)))
            xn, proj = norm_matmul(h, row(mix_norm_g[i]), w_in, BF16, "b_in")
            cat = mixer_b_fwd(proj, b_w_s[j], bt, lg, lb, kv, "mixer_b")
            extra = (b_w_s[j], bt, lg, lb)
        h_mid = matmul(cat, w_out, "nn", F32, "mix_out", res=h)
        hn, gu, act, *gathered = gate_up_fwd(h_mid, row(ffn_norm_g[i]), *wl["w_gate_up"], "gate_up",
                                             rider=gather_of(i + 1) if i + 1 < depth else None)
        if gathered:
            weights[i + 1] = usable(i + 1, gathered)
        h_out = matmul(act, wl["w_down"], "nn", F32, "down", res=h_mid)
        saved.append((h, xn, proj, cat, h_mid, hn, gu, act, kv, extra))
        h = h_out

    loss_part, dh, d_final_g = loss_head(h, row(final_norm_g), tgt, "loss_head")
    loss = lax.psum(loss_part[0, 0], ("x", "y", "c"))

    d_mix_g, d_ffn_g = [None] * depth, [None] * depth
    d_sinks, d_ws, d_bias, d_lg, d_lb = [], [], [], [], []
    d_mem_n = jnp.zeros(mem2.shape, F32)
    totals = [None] * depth
    pending = None
    for i in reversed(range(depth)):
        h_in, xn, proj, cat, h_mid, hn, gu, act, kv, extra = saved[i]
        wl = weights[i]
        dgu, *from_sibling = down_dx_swiglu_bwd(dh, wl["w_down"], gu, "down_dx",
                                                rider=sibling_halves_exchange(pending) if pending else None)
        if pending:
            partial = [chip_partial_sums(g, r1, c_arr, "grads_chip_sum") for g, r1 in zip(pending, from_sibling)]
        dw_down = matmul(act, dh, "tn", F32, "down_dw", tm=1408, out_planes=("rows", 4))
        dw_gate_up = matmul((hn, 0), dgu, "tn", F32, "gate_up_dw", tn=1408, tk=2048, out_planes=("cols", 4))
        ffn = [dw_gate_up[None], dw_down[None]] if i == 0 else []
        riders = ([chips_exchange(partial)] if pending else []) + ([sibling_halves_exchange(ffn)] if ffn else [])
        rider = None if not riders else riders[0] if len(riders) == 1 else both_exchanges(*riders)
        dh, d_ffn_g[i], *landed = dx_norm_bwd(dgu, wl["w_gate_up"], h_mid, row(ffn_norm_g[i]), dh, "gate_up_dx", rider=rider)
        if pending:
            from_chips = landed[:len(pending)]
            totals[i + 1] = [shard_total(g, r1, r2, cs_arr, "grads_shard_total") for g, r1, r2 in zip(pending, from_sibling, from_chips)]
        if ffn:
            ffn_sibling = landed[-len(ffn):]
            ffn_partial = [chip_partial_sums(g, r1, c_arr, "grads_chip_sum") for g, r1 in zip(ffn, ffn_sibling)]
        dcat = matmul(dh, wl["w_out"], "nt", F32, "mix_out_dx")
        dw_out = matmul(cat, dh, "tn", F32, "mix_out_dw", out_planes=("rows", 4))
        if i % 2 == 0:
            dproj, dsk, dkv, *ffn_chips = mixer_a_bwd(proj, dcat, extra[0], kv, "mixer_a_bwd",
                                                      rider=chips_exchange(ffn_partial) if ffn else None)
            if ffn:
                ffn_totals = [shard_total(g, r1, r2, cs_arr, "grads_shard_total") for g, r1, r2 in zip(ffn, ffn_sibling, ffn_chips)]
            d_sinks.insert(0, dsk[:Q_HEADS, 0])
            dw_in = matmul(xn, dproj, "tn", F32, "a_in_dw", out_planes=("cols", 4))
        else:
            dproj, dws, dbt, dlg, dlb, dkv = mixer_b_bwd(proj, dcat, *extra, kv, "mixer_b_bwd")
            d_ws.insert(0, dws)
            d_bias.insert(0, dbt[:, :B_GROUPS].T)
            d_lg.insert(0, dlg[:B_GROUPS])
            d_lb.insert(0, dlb[:B_GROUPS])
            dw_in = matmul(xn, dproj, "tn", F32, "b_in_dw", out_planes=("cols", 4))
        dw_kv = matmul(mem_n, dkv, "tn", F32, "mem_kv_dw", out_planes=("rows", 4))
        d_mem_n = matmul(dkv, wl["w_mem_kv"], "nt", F32, "mem_kv_dx", res=d_mem_n)
        dh, d_mix_g[i] = dx_norm_bwd(dproj, wl["w_in"], h_in, row(mix_norm_g[i]), dh, "in_dx")
        pending = [dw_in[None], dw_out[None], dw_kv[None]] + ([] if ffn else [dw_gate_up[None], dw_down[None]])
    grad_x = dh.reshape(x.shape)
    _, d_mem_g = rmsnorm_bwd(mem2, row(mem_norm_g), d_mem_n, jnp.zeros(mem2.shape, F32), "mem_norm_bwd")

    from_sibling = run_exchange(sibling_halves_exchange(pending), "grads_sibling_swap")
    partial = [chip_partial_sums(g, r1, c_arr, "grads_chip_sum") for g, r1 in zip(pending, from_sibling)]
    from_chips = run_exchange(chips_exchange(partial), "grads_chips_exchange")
    totals[0] = [shard_total(g, r1, r2, cs_arr, "grads_shard_total") for g, r1, r2 in zip(pending, from_sibling, from_chips)] + ffn_totals

    mine = {n: [None] * given[n].shape[0] for n, _ in SHARDED}
    for l in range(depth):
        for (n, k), tot in zip(layer_weights(l), totals[l]):
            mine[n][k] = tot
    mine = [jnp.concatenate(mine[n], axis=0) for n, _ in SHARDED]
    theirs = run_exchange(sibling_exchange(mine), "grads_sibling_totals")
    out = {}
    for (n, _), g_mine, g_theirs in zip(SHARDED, mine, theirs):
        shape = given[n].shape
        two_d = lambda a: a.reshape(-1, shape[-1])
        res = adamw_halves(two_d(given[n]), two_d(g_mine), two_d(g_theirs), two_d(given["m_" + n]), two_d(given["v_" + n]),
                           c_arr, shape[1], "adamw")
        out[n] = tuple(r.reshape(shape) for r in res)

    small = ("mem_norm_g", "mix_norm_g", "ffn_norm_g", "final_norm_g", "a_sinks", "b_w_s", "b_bias_s", "b_ln_g", "b_ln_b")
    small_g = [d_mem_g[0], jnp.concatenate(d_mix_g, axis=0), jnp.concatenate(d_ffn_g, axis=0), d_final_g[0],
               jnp.stack(d_sinks), jnp.stack(d_ws), jnp.stack(d_bias), jnp.stack(d_lg), jnp.stack(d_lb)]
    packed = _pack(small_g)
    g_small = sum_devices(allgather_small(packed, "small_allgather").reshape(8, *packed.shape), "small_sum")
    like = [given[n] for n in small]
    delta_s, new_m_s, new_v_s = adamw(_pack(like), g_small, _pack([given["m_" + n] for n in small]),
                                      _pack([given["v_" + n] for n in small]), "adamw_small")
    for n, g, dl, nm_, nv_ in zip(small, _unpack(g_small, like), _unpack(delta_s, like), _unpack(new_m_s, like), _unpack(new_v_s, like)):
        out[n] = (g, dl, nm_, nv_)

    order = ("mem_norm_g", "mix_norm_g", "ffn_norm_g", "final_norm_g", "a_w_in", "a_sinks", "a_w_out", "b_w_in", "b_w_s",
             "b_bias_s", "b_ln_g", "b_ln_b", "b_w_out", "w_mem_kv", "w_gate_up", "w_down")
    return (loss, grad_x, *[out[n][0] for n in order], *[out[n][1] for n in order],
            *[out[n][2] for n in order], *[out[n][3] for n in order])
```

```python
import jax
import jax.numpy as jnp
from jax import lax
from jax.experimental import pallas as pl
from jax.experimental.pallas import tpu as pltpu

F32, BF16 = jnp.float32, jnp.bfloat16
EPS = 1e-6
HEAD_DIM = 64
Q_HEADS, KV_HEADS, GROUP = 12, 2, 6
WINDOW = 128
MEM_HEADS = 4
B_GROUPS = 6
Q_W, KV_W, MEM_W, B_W = 768, 128, 256, 768
SCALE = HEAD_DIM ** -0.5
NEG = -1e30
ADAM_LR, ADAM_B1, ADAM_B2, ADAM_EPS, ADAM_WD, ADAM_STEP = 0.001, 0.9, 0.999, 1e-08, 0.01, 10
V7X_VMEM_LIMIT_BYTES = 48 * 1024 * 1024
MESH = pl.DeviceIdType.MESH
HBM_SPEC = pl.BlockSpec(memory_space=pltpu.HBM)
VMEM_SPEC = pl.BlockSpec(memory_space=pltpu.VMEM)


def _cp(*sem):
    return pltpu.CompilerParams(dimension_semantics=sem or None, vmem_limit_bytes=V7X_VMEM_LIMIT_BYTES)


def _tile(n, cands):
    for t in cands:
        if n % t == 0:
            return t
    return n


def _sds(shape, dtype):
    return jax.ShapeDtypeStruct(tuple(shape), dtype)


def _dot(a, b, ca, cb):
    return lax.dot_general(a, b, (((ca,), (cb,)), ((), ())), preferred_element_type=F32)


def _rms(x, g):
    return x * lax.rsqrt(jnp.mean(x * x, axis=-1, keepdims=True) + EPS) * g


def rmsnorm_fwd(h, g, name):
    t, d = h.shape
    tm = _tile(t, (512, 256, 128))

    def body(h_ref, g_ref, o_ref):
        o_ref[...] = _rms(h_ref[...], g_ref[...]).astype(o_ref.dtype)

    return pl.pallas_call(
        body, name=name, grid=(t // tm,),
        in_specs=[pl.BlockSpec((tm, d), lambda i: (i, 0)), pl.BlockSpec((1, d), lambda i: (0, 0))],
        out_specs=pl.BlockSpec((tm, d), lambda i: (i, 0)),
        out_shape=_sds((t, d), BF16), compiler_params=_cp("parallel"))(h, g)


def rmsnorm_bwd(h, g, dxn, dres, name):
    t, d = h.shape
    tm = _tile(t, (512, 256, 128))

    def body(h_ref, g_ref, dxn_ref, dres_ref, dh_ref, dg_ref):
        _, vjp = jax.vjp(_rms, h_ref[...], g_ref[...])
        dh, dg = vjp(dxn_ref[...].astype(F32))
        dh_ref[...] = dres_ref[...] + dh

        @pl.when(pl.program_id(0) == 0)
        def _():
            dg_ref[...] = jnp.zeros_like(dg_ref)

        dg_ref[...] += dg

    row = pl.BlockSpec((tm, d), lambda i: (i, 0))
    vec = pl.BlockSpec((1, d), lambda i: (0, 0))
    return pl.pallas_call(
        body, name=name, grid=(t // tm,), in_specs=[row, vec, row, row], out_specs=[row, vec],
        out_shape=[_sds((t, d), F32), _sds((1, d), F32)], compiler_params=_cp("arbitrary"))(h, g, dxn, dres)


def loss_head(h, g, tgt, name):
    t, d = h.shape
    tm = _tile(t, (512, 256, 128))

    def body(h_ref, g_ref, t_ref, l_ref, dh_ref, dg_ref):
        y, vjp = jax.vjp(_rms, h_ref[...], g_ref[...])
        err = y - t_ref[...]
        dh, dg = vjp(err * (1.0 / d))
        dh_ref[...] = dh
        part = 0.5 * jnp.sum(jnp.mean(err * err, axis=-1, keepdims=True), axis=0, keepdims=True)

        @pl.when(pl.program_id(0) == 0)
        def _():
            dg_ref[...] = jnp.zeros_like(dg_ref)
            l_ref[...] = jnp.zeros_like(l_ref)

        dg_ref[...] += dg
        l_ref[...] += part

    row = pl.BlockSpec((tm, d), lambda i: (i, 0))
    vec = pl.BlockSpec((1, d), lambda i: (0, 0))
    one = pl.BlockSpec((1, 1), lambda i: (0, 0))
    return pl.pallas_call(
        body, name=name, grid=(t // tm,), in_specs=[row, vec, row], out_specs=[one, row, vec],
        out_shape=[_sds((1, 1), F32), _sds((t, d), F32), _sds((1, d), F32)], compiler_params=_cp("arbitrary"))(h, g, tgt)


def _logical(op):
    arr, lead = op if isinstance(op, tuple) else (op, None)
    planes = arr.shape[-3] if arr.ndim - (lead is not None) == 3 else 1
    return arr, lead, arr.shape[-2], arr.shape[-1], planes


def _spec(op, rows_t, cols_t, row_of, col_of):
    arr, lead, _, cols, _ = _logical(op)
    per = cols // cols_t
    lead = () if lead is None else (lead,)
    if arr.ndim - len(lead) == 2:
        return pl.BlockSpec((None,) * len(lead) + (rows_t, cols_t), lambda *g: lead + (row_of(*g), col_of(*g)))
    return pl.BlockSpec((None,) * len(lead) + (None, rows_t, cols_t),
                        lambda *g: lead + (col_of(*g) // per, row_of(*g), col_of(*g) % per))


def _arr(op):
    return op[0] if isinstance(op, tuple) else op


def _resident_whole(w, d):
    wa, layer = w
    planes, per = wa.shape[-3], wa.shape[-1]

    def fill(w_ref, whole_ref):
        for s in range(planes):
            whole_ref[:, s * per:(s + 1) * per] = w_ref[s]

    return (pl.BlockSpec((None, planes, d, per), lambda i: (layer, 0, 0, 0), pipeline_mode=pl.Buffered(1)),
            pltpu.VMEM((d, planes * per), wa.dtype), fill)


def norm_matmul(h, g, w, out_dtype, name):
    t, d = h.shape
    w_spec, whole, fill = _resident_whole(w, d)
    n = whole.shape[1]
    tm = _tile(t, (512, 256, 128))

    def body(h_ref, g_ref, w_ref, xn_ref, o_ref, whole_ref):
        @pl.when(pl.program_id(0) == 0)
        def _():
            fill(w_ref, whole_ref)

        xn = _rms(h_ref[...], g_ref[...]).astype(BF16)
        xn_ref[...] = xn
        o_ref[...] = _dot(xn, whole_ref[...], 1, 0).astype(o_ref.dtype)

    return pl.pallas_call(
        body, name=name, grid=(t // tm,),
        in_specs=[pl.BlockSpec((tm, d), lambda i: (i, 0)), pl.BlockSpec((1, d), lambda i: (0, 0)), w_spec],
        out_specs=[pl.BlockSpec((tm, d), lambda i: (i, 0)), pl.BlockSpec((tm, n), lambda i: (i, 0))],
        out_shape=[_sds((t, d), BF16), _sds((t, n), out_dtype)], scratch_shapes=[whole],
        compiler_params=_cp("arbitrary"))(h, g, w[0])


def dx_norm_bwd(dy, w, h, g, dres, name, rider=None):
    t, d = h.shape
    dy_arr, dy_lead, _, kc, kp = _logical(dy)
    w_arr, w_lead, _, wc, wp = _logical(w)
    assert kc * kp == wc * wp and dy_lead is None, name
    chunk = min(kc, wc)
    tm = _tile(t, (512, 256, 128))

    def piece(ref, planes, cols, q):
        off = q * chunk % cols
        return ref[q * chunk // cols, :, off:off + chunk] if planes > 1 else ref[:, off:off + chunk]

    narrow = wp > 1 and wc % 128 != 0
    if narrow:
        assert kp == 1, name
        w_whole_spec, whole, fill = _resident_whole(w, d)

    def body(dy_ref, w_ref, h_ref, g_ref, dres_ref, dh_ref, dg_ref, *whole_ref):
        if narrow:
            @pl.when(pl.program_id(0) == 0)
            def _():
                fill(w_ref, whole_ref[0])

            dxn = _dot(dy_ref[...].astype(BF16), whole_ref[0][...], 1, 1)
        else:
            dxn = None
            for q in range(kc * kp // chunk):
                p = _dot(piece(dy_ref, kp, kc, q).astype(BF16), piece(w_ref, wp, wc, q), 1, 1)
                dxn = p if dxn is None else dxn + p
        _, vjp = jax.vjp(_rms, h_ref[...], g_ref[...])
        dh, dg = vjp(dxn)
        dh_ref[...] = dres_ref[...] + dh

        @pl.when(pl.program_id(0) == 0)
        def _():
            dg_ref[...] = jnp.zeros_like(dg_ref)

        dg_ref[...] += dg

    w_lead = () if w_lead is None else (w_lead,)
    w_block = ((wp,) if wp > 1 else ()) + (d, wc)
    w_spec = pl.BlockSpec((None,) * len(w_lead) + w_block, lambda i: w_lead + (0,) * len(w_block), pipeline_mode=pl.Buffered(1))
    if narrow:
        w_spec = w_whole_spec
    dy_spec = pl.BlockSpec((kp, tm, kc), lambda i: (0, i, 0)) if kp > 1 else pl.BlockSpec((tm, kc), lambda i: (i, 0))
    row = pl.BlockSpec((tm, d), lambda i: (i, 0))
    vec = pl.BlockSpec((1, d), lambda i: (0, 0))
    grid = (t // tm,)
    body, r_ops, r_in, r_shapes, r_out, r_scratch = with_rider(body, 5, 2, grid, rider)
    return pl.pallas_call(
        body, name=name, grid=grid, in_specs=[dy_spec, w_spec, row, vec, row] + r_in,
        out_specs=[row, vec] + r_out, out_shape=[_sds((t, d), F32), _sds((1, d), F32)] + r_shapes,
        scratch_shapes=([whole] if narrow else []) + r_scratch,
        compiler_params=_cp("arbitrary"))(dy_arr, w_arr, h, g, dres, *r_ops)


def matmul(a, b, mode, out_dtype, name, res=None, tm=None, tn=1792, tk=2816, out_planes=None, out_into=None, rider=None):
    _, _, ar, ac, ap = _logical(a)
    _, _, br, bc, bp = _logical(b)
    if mode == "nn":
        m, ka, kb, n = ar, ac * ap, br, bc * bp
        n_plane, ka_plane, kb_plane = bc, ac, br
    elif mode == "nt":
        m, ka, n, kb = ar, ac * ap, br, bc * bp
        n_plane, ka_plane, kb_plane = br, ac, bc
    else:
        ka, m, kb, n = ar, ac * ap, br, bc * bp
        n_plane, ka_plane, kb_plane = bc, ar, br
    m_plane = ac if mode == "tn" else ar
    assert ka == kb, name
    k = ka
    kind, planes = out_planes or ("cols", 1)
    narrow = kind == "cols" and (n // planes) % 128 != 0
    if kind == "cols" and not narrow:
        n_plane = min(n_plane, n // planes)
    if narrow:
        tn = n
    tm = _tile(m_plane, ((1024, 1408, 512, 256, 128) if mode == "tn" else (512, 256, 128)) if tm is None else (tm, 1024, 512, 256, 128))
    if kind == "rows" and tm % (m // planes):
        tm = m_plane
    tn = _tile(n_plane, (tn, 1792, 1408, 1280, 1024, 896, 640, 512, 256, 128))
    tk = _tile(min(ka_plane, kb_plane), (tk, 2816, 1792, 1408, 1280, 1024, 512, 256, 128))
    nk = k // tk
    row_i, col_j, red = (lambda i, j, kk: i), (lambda i, j, kk: j), (lambda i, j, kk: kk)
    if mode == "nn":
        a_spec, b_spec, ca, cb = _spec(a, tm, tk, row_i, red), _spec(b, tk, tn, red, col_j), 1, 0
    elif mode == "nt":
        a_spec, b_spec, ca, cb = _spec(a, tm, tk, row_i, red), _spec(b, tn, tk, col_j, red), 1, 1
    else:
        a_spec, b_spec, ca, cb = _spec(a, tk, tm, red, row_i), _spec(b, tk, tn, red, col_j), 0, 0
    lead = () if out_into is None else (out_into[1],)
    if planes == 1:
        o_shape, o_block = (m, n), (tm, tn)
        o_index = lambda i, j, kk: lead + (i, j)
    elif narrow:
        o_shape, o_block = (planes, m, n // planes), (planes, tm, n // planes)
        o_index = lambda i, j, kk: lead + (0, i, 0)
    elif kind == "cols":
        per = n // planes // tn
        o_shape, o_block = (planes, m, n // planes), (None, tm, tn)
        o_index = lambda i, j, kk: lead + (j // per, i, j % per)
    else:
        o_shape, o_block = (planes, m // planes, n), (tm // (m // planes), m // planes, tn)
        o_index = lambda i, j, kk: lead + (i, 0, j)
    o_spec = pl.BlockSpec((None,) * len(lead) + o_block, o_index)
    if out_into is not None:
        assert out_into[0].shape[1:] == o_shape and out_into[0].dtype == out_dtype, name
        o_shape = out_into[0].shape
    has_res = res is not None
    n_in = 2 + has_res + (out_into is not None)

    def put(o_ref, v):
        if narrow:
            for s in range(planes):
                o_ref[s] = v[:, s * (n // planes):(s + 1) * (n // planes)].astype(o_ref.dtype)
        else:
            o_ref[...] = v.astype(o_ref.dtype).reshape(o_ref.shape)

    def body(*refs):
        a_ref, b_ref = refs[:2]
        rest = refs[2:2 + has_res] + refs[n_in:]
        o_ref = rest[1] if has_res else rest[0]
        p = _dot(a_ref[...].astype(BF16), b_ref[...].astype(BF16), ca, cb)
        if nk == 1:
            if has_res:
                p = p + rest[0][...]
            put(o_ref, p)
        else:
            acc_ref = rest[-1]
            kk = pl.program_id(2)

            @pl.when(kk == 0)
            def _():
                acc_ref[...] = p

            @pl.when(kk > 0)
            def _():
                acc_ref[...] += p

            @pl.when(kk == nk - 1)
            def _():
                r = acc_ref[...]
                if has_res:
                    r = r + rest[0][...]
                put(o_ref, r)

    operands = [_arr(a), _arr(b)] + ([res] if has_res else []) + ([out_into[0]] if out_into is not None else [])
    grid = (m // tm, n // tn, nk)
    body, r_ops, r_in, r_shapes, r_out, r_scratch = with_rider(body, n_in, 1, grid, rider)
    out = pl.pallas_call(
        body, name=name, grid=grid,
        in_specs=[a_spec, b_spec] + ([pl.BlockSpec((tm, tn), lambda i, j, kk: (i, j))] if has_res else [])
        + ([pl.BlockSpec(memory_space=pl.ANY)] if out_into is not None else []) + r_in,
        out_specs=[o_spec] + r_out, out_shape=[_sds(o_shape, out_dtype)] + r_shapes,
        input_output_aliases={n_in - 1: 0} if out_into is not None else {},
        scratch_shapes=([pltpu.VMEM((tm, tn), F32)] if nk > 1 else []) + r_scratch,
        compiler_params=_cp(*(("arbitrary",) * 3 if rider else ("parallel", "parallel", "arbitrary"))))(*operands, *r_ops)
    return out if rider else out[0]


def _swiglu(gate, up):
    return gate / (1.0 + jnp.exp(-gate)) * up


def gate_up_fwd(h, g, w, layer, name, rider=None):
    t, d = h.shape
    half = w.shape[-1]
    tm = _tile(t, (512, 256, 128))

    def body(h_ref, g_ref, wg_ref, wu_ref, hn_ref, gu_ref, act_ref):
        a = _rms(h_ref[...], g_ref[...]).astype(BF16)
        hn_ref[...] = a
        gate, up = _dot(a, wg_ref[...], 1, 0), _dot(a, wu_ref[...], 1, 0)
        gu_ref[0] = gate.astype(gu_ref.dtype)
        gu_ref[1] = up.astype(gu_ref.dtype)
        act_ref[...] = _swiglu(gate, up).astype(act_ref.dtype)

    grid = (2, t // tm)
    body, r_ops, r_in, r_shapes, r_out, r_scratch = with_rider(body, 4, 3, grid, rider)
    return pl.pallas_call(
        body, name=name, grid=grid,
        in_specs=[pl.BlockSpec((tm, d), lambda j, i: (i, 0)), pl.BlockSpec((1, d), lambda j, i: (0, 0)),
                  pl.BlockSpec((None, None, d, half), lambda j, i: (layer, j, 0, 0)),
                  pl.BlockSpec((None, None, d, half), lambda j, i: (layer, 2 + j, 0, 0))] + r_in,
        out_specs=[pl.BlockSpec((None, tm, d), lambda j, i: (j, i, 0)), pl.BlockSpec((2, tm, half), lambda j, i: (0, i, j)),
                   pl.BlockSpec((tm, half), lambda j, i: (i, j))] + r_out,
        out_shape=[_sds((2, t, d), BF16), _sds((2, t, 2 * half), BF16), _sds((t, 2 * half), BF16)] + r_shapes,
        scratch_shapes=r_scratch, compiler_params=_cp("arbitrary", "arbitrary"))(h, g, w, w, *r_ops)


def down_dx_swiglu_bwd(dh, wd, gu, name, rider=None):
    t, d = dh.shape
    w, layer = wd
    f = w.shape[-2]
    tm = _tile(t, (512, 256, 128))
    tn = _tile(f, (1408, 512, 256, 128))

    def body(dh_ref, w_ref, gu_ref, o_ref):
        dact = _dot(dh_ref[...].astype(BF16), w_ref[...], 1, 1)
        gate, up = gu_ref[0].astype(F32), gu_ref[1].astype(F32)
        sig = 1.0 / (1.0 + jnp.exp(-gate))
        silu = gate * sig
        o_ref[0] = (dact * up * (sig + silu * (1.0 - sig))).astype(o_ref.dtype)
        o_ref[1] = (dact * silu).astype(o_ref.dtype)

    planes = pl.BlockSpec((2, tm, tn), lambda j, i: (0, i, j))
    grid = (f // tn, t // tm)
    body, r_ops, r_in, r_shapes, r_out, r_scratch = with_rider(body, 3, 1, grid, rider)
    return pl.pallas_call(
        body, name=name, grid=grid,
        in_specs=[pl.BlockSpec((tm, d), lambda j, i: (i, 0)), pl.BlockSpec((None, tn, d), lambda j, i: (layer, j, 0)), planes] + r_in,
        out_specs=[planes] + r_out, out_shape=[_sds((2, t, f), BF16)] + r_shapes, scratch_shapes=r_scratch,
        compiler_params=_cp("arbitrary", "arbitrary"))(dh, w, gu, *r_ops)


def _softmax_over_keys(s, sink=None):
    m = s.max(axis=0, keepdims=True)
    if sink is not None:
        m = jnp.maximum(m, sink)
    m = lax.stop_gradient(m)
    e = jnp.exp(s - m)
    den = e.sum(axis=0, keepdims=True)
    if sink is not None:
        den = den + jnp.exp(sink - m)
    return e * (1.0 / den)


def _low_lanes():
    return lax.broadcasted_iota(jnp.int32, (1, 128), 1) < HEAD_DIM


def _stack_heads(slabs):
    low = _low_lanes()
    return jnp.concatenate([p for s in slabs for p in (jnp.where(low, s, 0.0), jnp.where(low, 0.0, s))], axis=0)


def _unstack_heads(o, n_slabs):
    low = _low_lanes()
    return [jnp.where(low, o[2 * j * WINDOW:(2 * j + 1) * WINDOW], o[(2 * j + 1) * WINDOW:(2 * j + 2) * WINDOW])
            for j in range(n_slabs)]


def _swa_group(q_slabs, k_both, v_both, sinks, mask):
    qs = _stack_heads(q_slabs).astype(BF16)
    s = jnp.where(mask, _dot(k_both.astype(BF16), qs, 1, 1) * SCALE, NEG)
    sink = jnp.concatenate([jnp.broadcast_to(v, (1, WINDOW)) for v in sinks], axis=1)
    return _unstack_heads(_dot(_softmax_over_keys(s, sink).astype(BF16), v_both.astype(BF16), 0, 0), len(q_slabs))


def _mem_pair(q_slab, k_slab, v_slab):
    s = _dot(k_slab.astype(BF16), _stack_heads([q_slab]).astype(BF16), 1, 1) * SCALE
    return _unstack_heads(_dot(_softmax_over_keys(s).astype(BF16), v_slab.astype(BF16), 0, 0), 1)[0]


def _gelu(x):
    return 0.5 * x * (1.0 + jnp.tanh(0.7978845608028654 * (x + 0.044715 * (x * x * x))))


def _gmlp_group(zu, zv, w, bcol, lg, lb, tri):
    u, v = _gelu(zu), _gelu(zv)
    mu = jnp.mean(v, axis=-1, keepdims=True)
    var = jnp.mean(jnp.square(v - mu), axis=-1, keepdims=True)
    vn = (v - mu) * lax.rsqrt(var + EPS) * lg + lb
    sv = _dot(jnp.where(tri, w, 0.0).astype(BF16), vn.astype(BF16), 1, 0) + bcol
    return u * sv


def _cols(x, width):
    return [x[:, j * width:(j + 1) * width] for j in range(x.shape[1] // width)]


def _swa_mask(has_prev):
    qi = lax.broadcasted_iota(jnp.int32, (2 * WINDOW, GROUP * WINDOW), 1) & (WINDOW - 1)
    kj = lax.broadcasted_iota(jnp.int32, (2 * WINDOW, GROUP * WINDOW), 0)
    in_prev = jnp.logical_and(jnp.logical_and(kj < WINDOW, kj > qi), has_prev)
    return jnp.logical_or(in_prev, jnp.logical_and(kj >= WINDOW, kj - WINDOW <= qi))


def _mix_a(q_slabs, k_boths, v_boths, sinks, qm_slabs, km_slabs, vm_slabs, mask):
    per = GROUP // 2
    outs = []
    for g in range(KV_HEADS):
        outs += _swa_group(q_slabs[per * g:per * (g + 1)], k_boths[g], v_boths[g], sinks[GROUP * g:GROUP * (g + 1)], mask)
    return outs + [_mem_pair(qm_slabs[j], km_slabs[j], vm_slabs[j]) for j in range(MEM_HEADS // 2)]


def _in_both_halves(prev, cur):
    cat = jnp.concatenate([prev, cur], axis=0)
    rolled = pltpu.roll(cat, HEAD_DIM, axis=1)
    low = _low_lanes()
    return [jnp.where(low, cat, rolled), jnp.where(low, rolled, cat)]


def _from_both_halves(d_boths):
    t = [d + pltpu.roll(d, HEAD_DIM, axis=1) for d in d_boths]
    return jnp.where(_low_lanes(), t[0], t[1])


def _mix_a_specs(nm, blk):
    prev = lambda n: jnp.maximum(blk(n) - 1, 0)
    return [pl.BlockSpec((WINDOW, Q_W), lambda n: (blk(n), 0)),
            pl.BlockSpec((WINDOW, KV_W), lambda n: (prev(n), Q_W // KV_W)),
            pl.BlockSpec((WINDOW, KV_W), lambda n: (blk(n), Q_W // KV_W)),
            pl.BlockSpec((WINDOW, KV_W), lambda n: (prev(n), Q_W // KV_W + 1)),
            pl.BlockSpec((WINDOW, KV_W), lambda n: (blk(n), Q_W // KV_W + 1)),
            pl.BlockSpec((WINDOW, MEM_W), lambda n: (blk(n), (Q_W + 2 * KV_W) // MEM_W)),
            pl.BlockSpec((16, 128), lambda n: (0, 0)),
            pl.BlockSpec((nm, MEM_W), lambda n: (0, 0)),
            pl.BlockSpec((nm, MEM_W), lambda n: (0, 1))]


def _mix_a_args(refs):
    q, kp, kc, vp, vc, qm, sk, km, vm = [r[...].astype(F32) for r in refs]
    return (_cols(q, 128), _in_both_halves(kp, kc), _in_both_halves(vp, vc), [sk[h:h + 1, 0:1] for h in range(Q_HEADS)],
            _cols(qm, 128), _cols(km, 128), _cols(vm, 128))


def mixer_a_fwd(proj, sk, kv, name, rider=None):
    t, nm = proj.shape[0], kv.shape[0]

    def body(*refs):
        o_ref = refs[-1]
        slabs = _mix_a(*_mix_a_args(refs[:-1]), _swa_mask(pl.program_id(0) > 0))
        o_ref[...] = jnp.concatenate(slabs, axis=1).astype(o_ref.dtype)

    grid = (t // WINDOW,)
    body, r_ops, r_in, r_shapes, r_out, r_scratch = with_rider(body, 9, 1, grid, rider)
    return pl.pallas_call(
        body, name=name, grid=grid, in_specs=_mix_a_specs(nm, lambda n: n) + r_in,
        out_specs=[pl.BlockSpec((WINDOW, Q_W + MEM_W), lambda n: (n, 0))] + r_out,
        out_shape=[_sds((t, Q_W + MEM_W), BF16)] + r_shapes, scratch_shapes=r_scratch,
        compiler_params=_cp("arbitrary"))(proj, proj, proj, proj, proj, proj, sk, kv, kv, *r_ops)


def _onehot_rows(vals, shape):
    rows = lax.broadcasted_iota(jnp.int32, shape, 0)
    out = jnp.zeros(shape, F32)
    for h, v in enumerate(vals):
        out = out + jnp.where(rows == h, jnp.broadcast_to(v, shape), 0.0)
    return out


def mixer_a_bwd(proj, dcat, sk, kv, name, rider=None):
    t, nm = proj.shape[0], kv.shape[0]
    nb = t // WINDOW
    blk = lambda i: nb - 1 - i

    def body(*refs):
        dcat_ref, dproj_ref, dsk_ref, dkv_ref, carry_ref = refs[9:]
        i = pl.program_id(0)

        @pl.when(i == 0)
        def _():
            carry_ref[...] = jnp.zeros_like(carry_ref)
            dsk_ref[...] = jnp.zeros_like(dsk_ref)
            dkv_ref[...] = jnp.zeros_like(dkv_ref)

        mask = _swa_mask(blk(i) > 0)
        _, vjp = jax.vjp(lambda *a: _mix_a(*a, mask), *_mix_a_args(refs[:9]))
        dqs, dk_boths, dv_boths, dsinks, dqms, dkms, dvms = vjp(_cols(dcat_ref[...].astype(F32), 128))
        dkv = jnp.concatenate([_from_both_halves(dk_boths), _from_both_halves(dv_boths)], axis=1)
        dkv_cur = dkv[WINDOW:] + carry_ref[...]
        carry_ref[...] = dkv[:WINDOW]
        dproj_ref[...] = jnp.concatenate(dqs + [dkv_cur] + dqms, axis=1).astype(dproj_ref.dtype)
        dsk_ref[...] += _onehot_rows(dsinks, (16, 128))
        dkv_ref[...] += jnp.concatenate(dkms + dvms, axis=1)

    width = Q_W + 2 * KV_W + MEM_W
    body, r_ops, r_in, r_shapes, r_out, r_scratch = with_rider(body, 10, 3, (nb,), rider)
    return pl.pallas_call(
        body, name=name, grid=(nb,),
        in_specs=_mix_a_specs(nm, blk) + [pl.BlockSpec((WINDOW, Q_W + MEM_W), lambda i: (blk(i), 0))] + r_in,
        out_specs=[pl.BlockSpec((WINDOW, width), lambda i: (blk(i), 0)), pl.BlockSpec((16, 128), lambda i: (0, 0)),
                   pl.BlockSpec((nm, 2 * MEM_W), lambda i: (0, 0))] + r_out,
        out_shape=[_sds((t, width), BF16), _sds((16, 128), F32), _sds((nm, 2 * MEM_W), F32)] + r_shapes,
        scratch_shapes=[pltpu.VMEM((WINDOW, 2 * KV_W), F32)] + r_scratch,
        compiler_params=_cp("arbitrary"))(proj, proj, proj, proj, proj, proj, sk, kv, kv, dcat, *r_ops)


def _mix_b(zus, zvs, ws, bcols, lgs, lbs, qms, kms, vms, tri):
    outs = [_gmlp_group(zus[g], zvs[g], ws[g], bcols[g], lgs[g], lbs[g], tri) for g in range(B_GROUPS)]
    return outs + [_mem_pair(qms[j], kms[j], vms[j]) for j in range(MEM_HEADS // 2)]


def _mix_b_specs(nm):
    return [pl.BlockSpec((WINDOW, 2 * B_W), lambda n: (n, 0)),
            pl.BlockSpec((WINDOW, MEM_W), lambda n: (n, 2 * B_W // MEM_W)),
            pl.BlockSpec((B_GROUPS, WINDOW, WINDOW), lambda n: (0, 0, 0)),
            pl.BlockSpec((WINDOW, 128), lambda n: (0, 0)),
            pl.BlockSpec((8, 128), lambda n: (0, 0)),
            pl.BlockSpec((8, 128), lambda n: (0, 0)),
            pl.BlockSpec((nm, MEM_W), lambda n: (0, 0)),
            pl.BlockSpec((nm, MEM_W), lambda n: (0, 1))]


def _mix_b_args(refs):
    z, qm, ws, bt, lg, lb, km, vm = [r[...].astype(F32) for r in refs]
    zs = _cols(z, 128)
    return (zs[:B_GROUPS], zs[B_GROUPS:], [ws[g] for g in range(B_GROUPS)], [bt[:, g:g + 1] for g in range(B_GROUPS)],
            [lg[g:g + 1, :] for g in range(B_GROUPS)], [lb[g:g + 1, :] for g in range(B_GROUPS)],
            _cols(qm, 128), _cols(km, 128), _cols(vm, 128))


def _tri():
    return lax.broadcasted_iota(jnp.int32, (WINDOW, WINDOW), 0) >= lax.broadcasted_iota(jnp.int32, (WINDOW, WINDOW), 1)


def mixer_b_fwd(proj, ws, bt, lg, lb, kv, name):
    t, nm = proj.shape[0], kv.shape[0]

    def body(*refs):
        o_ref = refs[-1]
        o_ref[...] = jnp.concatenate(_mix_b(*_mix_b_args(refs[:-1]), _tri()), axis=1).astype(o_ref.dtype)

    return pl.pallas_call(
        body, name=name, grid=(t // WINDOW,), in_specs=_mix_b_specs(nm),
        out_specs=pl.BlockSpec((WINDOW, B_W + MEM_W), lambda n: (n, 0)),
        out_shape=_sds((t, B_W + MEM_W), BF16), compiler_params=_cp("parallel"))(proj, proj, ws, bt, lg, lb, kv, kv)


def mixer_b_bwd(proj, dcat, ws, bt, lg, lb, kv, name, rider=None):
    t, nm = proj.shape[0], kv.shape[0]

    def body(*refs):
        dcat_ref, dproj_ref, dws_ref, dbt_ref, dlg_ref, dlb_ref, dkv_ref = refs[8:]

        @pl.when(pl.program_id(0) == 0)
        def _():
            for r in (dws_ref, dbt_ref, dlg_ref, dlb_ref, dkv_ref):
                r[...] = jnp.zeros_like(r)

        tri = _tri()
        zus, zvs, ws, bcols, lgs, lbs, qms, kms, vms = _mix_b_args(refs[:8])
        douts = _cols(dcat_ref[...].astype(F32), 128)
        grads = []
        for g in range(B_GROUPS):
            _, vjp = jax.vjp(lambda *a: _gmlp_group(*a, tri), zus[g], zvs[g], ws[g], bcols[g], lgs[g], lbs[g])
            grads.append(vjp(douts[g]))
        dzus, dzvs, dws, dbcols, dlgs, dlbs = [list(t) for t in zip(*grads)]
        grads = []
        for j in range(MEM_HEADS // 2):
            _, vjp = jax.vjp(_mem_pair, qms[j], kms[j], vms[j])
            grads.append(vjp(douts[B_GROUPS + j]))
        dqms, dkms, dvms = [list(t) for t in zip(*grads)]
        dproj_ref[...] = jnp.concatenate(dzus + dzvs + dqms, axis=1).astype(dproj_ref.dtype)
        for g in range(B_GROUPS):
            dws_ref[g] += dws[g]
        lanes = lax.broadcasted_iota(jnp.int32, (WINDOW, 128), 1)
        dbt = jnp.zeros((WINDOW, 128), F32)
        for g in range(B_GROUPS):
            dbt = dbt + jnp.where(lanes == g, jnp.broadcast_to(dbcols[g], (WINDOW, 128)), 0.0)
        dbt_ref[...] += dbt
        dlg_ref[...] += _onehot_rows(dlgs, (8, 128))
        dlb_ref[...] += _onehot_rows(dlbs, (8, 128))
        dkv_ref[...] += jnp.concatenate(dkms + dvms, axis=1)

    width = 2 * B_W + MEM_W
    const2 = lambda n: (0, 0)
    grid = (t // WINDOW,)
    body, r_ops, r_in, r_shapes, r_out, r_scratch = with_rider(body, 9, 6, grid, rider)
    return pl.pallas_call(
        body, name=name, grid=grid,
        in_specs=_mix_b_specs(nm) + [pl.BlockSpec((WINDOW, B_W + MEM_W), lambda n: (n, 0))] + r_in,
        out_specs=[pl.BlockSpec((WINDOW, width), lambda n: (n, 0)),
                   pl.BlockSpec((B_GROUPS, WINDOW, WINDOW), lambda n: (0, 0, 0)),
                   pl.BlockSpec((WINDOW, 128), const2), pl.BlockSpec((8, 128), const2), pl.BlockSpec((8, 128), const2),
                   pl.BlockSpec((nm, 2 * MEM_W), const2)] + r_out,
        out_shape=[_sds((t, width), BF16), _sds((B_GROUPS, WINDOW, WINDOW), F32), _sds((WINDOW, 128), F32),
                   _sds((8, 128), F32), _sds((8, 128), F32), _sds((nm, 2 * MEM_W), F32)] + r_shapes,
        scratch_shapes=r_scratch, compiler_params=_cp("arbitrary"))(proj, proj, ws, bt, lg, lb, kv, kv, dcat, *r_ops)


def _adamw_update(w, g, m, v):
    m2 = ADAM_B1 * m + (1.0 - ADAM_B1) * g
    v2 = ADAM_B2 * v + (1.0 - ADAM_B2) * jnp.square(g)
    m_hat = m2 / (1.0 - ADAM_B1 ** ADAM_STEP)
    v_hat = v2 / (1.0 - ADAM_B2 ** ADAM_STEP)
    return -ADAM_LR * (m_hat / (jnp.sqrt(v_hat) + ADAM_EPS) + ADAM_WD * w), m2, v2


def adamw(w, g, m, v, name):
    r, c = w.shape
    tr = _tile(r, (512, 352, 256, 128, 64, 32, 16, 8))

    def body(w_ref, g_ref, m_ref, v_ref, d_ref, nm_ref, nv_ref):
        d_ref[...], nm_ref[...], nv_ref[...] = _adamw_update(w_ref[...], g_ref[...], m_ref[...], v_ref[...])

    spec = pl.BlockSpec((tr, c), lambda i: (i, 0))
    return pl.pallas_call(
        body, name=name, grid=(r // tr,), in_specs=[spec] * 4, out_specs=[spec] * 3,
        out_shape=[_sds((r, c), F32)] * 3, compiler_params=_cp("parallel"))(w, g, m, v)


def adamw_halves(w, g_mine, g_theirs, m, v, c_arr, rows, name):
    r, c = w.shape
    tr = _tile(rows // 2, (256, 352, 128, 64, 32, 16, 8))
    per_half = rows // 2 // tr

    def body(c_ref, w_ref, gm_ref, gt_ref, m_ref, v_ref, g_ref, d_ref, nm_ref, nv_ref):
        g = jnp.where(pl.program_id(0) // per_half % 2 == c_ref[0], gm_ref[...], gt_ref[...])
        g_ref[...] = g
        d_ref[...], nm_ref[...], nv_ref[...] = _adamw_update(w_ref[...], g, m_ref[...], v_ref[...])

    spec = pl.BlockSpec((tr, c), lambda i, cr: (i, 0))
    half = pl.BlockSpec((tr, c), lambda i, cr: (i // (2 * per_half) * per_half + i % per_half, 0))
    return pl.pallas_call(
        body, name=name,
        grid_spec=pltpu.PrefetchScalarGridSpec(num_scalar_prefetch=1, grid=(r // tr,), in_specs=[spec, half, half, spec, spec],
                                               out_specs=[spec] * 4),
        out_shape=[_sds((r, c), F32)] * 4, compiler_params=_cp("parallel"))(c_arr, w, g_mine, g_theirs, m, v)


def _place():
    return lax.axis_index("x"), lax.axis_index("y"), lax.axis_index("c")


def _other_chips(x, y):
    return [(1 - x, y), (x, 1 - y), (1 - x, 1 - y)]


def _remote(src, dst, send_sems, recv_sems, k, dev):
    return pltpu.make_async_remote_copy(src_ref=src, dst_ref=dst, send_sem=send_sems.at[k], recv_sem=recv_sems.at[k],
                                        device_id=dev, device_id_type=MESH)


class Exchange:
    def __init__(self, ins, out_shapes, n_sems, start, finish):
        self.ins, self.out_shapes, self.start, self.finish = list(ins), list(out_shapes), start, finish
        self.sems = [n_sems, n_sems] if isinstance(n_sems, int) else list(n_sems)

    def scratch(self):
        return [pltpu.SemaphoreType.DMA((n,)) for n in self.sems]


def both_exchanges(a, b):
    ni, no, ns = len(a.ins), len(a.out_shapes), len(a.sems)

    def start(ins, outs, *sems):
        a.start(ins[:ni], outs[:no], *sems[:ns])
        b.start(ins[ni:], outs[no:], *sems[ns:])

    def finish(ins, outs, *sems):
        a.finish(ins[:ni], outs[:no], *sems[:ns])
        b.finish(ins[ni:], outs[no:], *sems[ns:])

    return Exchange(a.ins + b.ins, a.out_shapes + b.out_shapes, a.sems + b.sems, start, finish)


def run_exchange(ex, name):
    ni, no = len(ex.ins), len(ex.out_shapes)

    def body(*refs):
        ex.start(refs[:ni], refs[ni:ni + no], *refs[ni + no:])
        ex.finish(refs[:ni], refs[ni:ni + no], *refs[ni + no:])

    return pl.pallas_call(
        body, name=name, in_specs=[HBM_SPEC] * ni, out_specs=[HBM_SPEC] * no, out_shape=ex.out_shapes, scratch_shapes=ex.scratch(),
        compiler_params=pltpu.CompilerParams(has_side_effects=True))(*ex.ins)


def with_rider(body, n_in, n_out, grid, ex):
    if ex is None:
        return body, [], [], [], [], []
    ni, no, ns = len(ex.ins), len(ex.out_shapes), len(ex.sems)

    def riding(*refs):
        r_in, r_out, sems = refs[n_in:n_in + ni], refs[n_in + ni + n_out:n_in + ni + n_out + no], refs[-ns:]
        first = last = None
        for axis, size in enumerate(grid):
            at_first, at_last = pl.program_id(axis) == 0, pl.program_id(axis) == size - 1
            first = at_first if first is None else jnp.logical_and(first, at_first)
            last = at_last if last is None else jnp.logical_and(last, at_last)

        @pl.when(first)
        def _():
            ex.start(r_in, r_out, *sems)

        body(*refs[:n_in], *refs[n_in + ni:n_in + ni + n_out], *refs[n_in + ni + n_out + no:-ns])

        @pl.when(last)
        def _():
            ex.finish(r_in, r_out, *sems)

    return riding, ex.ins, [HBM_SPEC] * ni, ex.out_shapes, [HBM_SPEC] * no, ex.scratch()


def gather_exchange(shards):
    nw = len(shards)
    entry = [k for _, k in shards]

    def rows(ref, cc):
        return pl.ds(cc * (ref.shape[1] // 2), ref.shape[1] // 2)

    def sent(ins, outs, send_sems, recv_sems, w, j):
        x, y, c = _place()
        return _remote(ins[w].at[pl.ds(entry[w], 1), rows(ins[w], c)], outs[w].at[:, 2 * x + y, rows(ins[w], c)], send_sems, recv_sems,
                       7 * w + j, (*_other_chips(x, y)[j], c))

    def landed(ins, outs, send_sems, recv_sems, w, j, cc, to):
        x, y, c = _place()
        chip = _other_chips(x, y)[j]
        blk = outs[w].at[:, 2 * chip[0] + chip[1], rows(ins[w], cc)]
        return _remote(blk, blk, send_sems, recv_sems, 7 * w + (j if to is None else 3 + j), (x, y, c) if to is None else to)

    def own(ins, outs, send_sems, recv_sems, w):
        x, y, c = _place()
        return _remote(ins[w].at[pl.ds(entry[w], 1)], outs[w].at[:, 2 * x + y], send_sems, recv_sems, 7 * w + 6, (x, y, 1 - c))

    def start(ins, outs, send_sems, recv_sems):
        for j in range(3):
            for w in range(nw):
                sent(ins, outs, send_sems, recv_sems, w, j).start()
        for w in range(nw):
            own(ins, outs, send_sems, recv_sems, w).start()

    def finish(ins, outs, send_sems, recv_sems):
        x, y, c = _place()
        for j in range(3):
            for w in range(nw):
                landed(ins, outs, send_sems, recv_sems, w, j, c, None).wait_recv()
                landed(ins, outs, send_sems, recv_sems, w, j, c, (x, y, 1 - c)).start()
        for w in range(nw):
            own(ins, outs, send_sems, recv_sems, w).wait()
        for j in range(3):
            for w in range(nw):
                landed(ins, outs, send_sems, recv_sems, w, j, 1 - c, (x, y, c)).wait_recv()
        for j in range(3):
            for w in range(nw):
                sent(ins, outs, send_sems, recv_sems, w, j).wait_send()
                landed(ins, outs, send_sems, recv_sems, w, j, c, (x, y, 1 - c)).wait_send()

    return Exchange([s for s, _ in shards], [_sds((1, 4) + s.shape[1:], s.dtype) for s, _ in shards], 7 * nw, start, finish)


def copies_exchange(ins, out_shapes, n_sems, copies):
    def start(*refs):
        for cp in copies(*refs):
            cp.start()

    def finish(*refs):
        for cp in copies(*refs):
            cp.wait()

    return Exchange(ins, out_shapes, n_sems, start, finish)


def sibling_halves_exchange(gs):
    def copies(ins, outs, send_sems, recv_sems):
        x, y, c = _place()
        return [_remote(g.at[:, :, pl.ds((1 - c) * (g.shape[2] // 2), g.shape[2] // 2)], o, send_sems, recv_sems, w, (x, y, 1 - c))
                for w, (g, o) in enumerate(zip(ins, outs))]

    return copies_exchange(gs, [_sds(g.shape[:2] + (g.shape[2] // 2, g.shape[3]), g.dtype) for g in gs], len(gs), copies)


def chips_exchange(sbs):
    def copies(ins, outs, send_sems, recv_sems):
        x, y, c = _place()
        return [_remote(s.at[:, 2 * chip[0] + chip[1]], o.at[j], send_sems, recv_sems, 3 * w + j, (*chip, c))
                for j, chip in enumerate(_other_chips(x, y)) for w, (s, o) in enumerate(zip(ins, outs))]

    return copies_exchange(sbs, [_sds((3, s.shape[0]) + s.shape[2:], s.dtype) for s in sbs], 3 * len(sbs), copies)


def sibling_exchange(fs):
    def copies(ins, outs, send_sems, recv_sems):
        x, y, c = _place()
        return [_remote(f, o, send_sems, recv_sems, w, (x, y, 1 - c)) for w, (f, o) in enumerate(zip(ins, outs))]

    return copies_exchange(fs, [_sds(f.shape, f.dtype) for f in fs], len(fs), copies)


def _half_tile(a):
    return _tile(a, (256, 352, 176, 128, 64, 32, 16))


def chip_partial_sums(g, r1, c_arr, name):
    nl, _, a2, b = r1.shape
    ta = _half_tile(a2)
    per = a2 // ta

    def body(c_ref, g_ref, r_ref, o_ref):
        o_ref[...] = (g_ref[...] + r_ref[...]).astype(o_ref.dtype)

    blk = (None, None, ta, b)
    return pl.pallas_call(
        body, name=name,
        grid_spec=pltpu.PrefetchScalarGridSpec(
            num_scalar_prefetch=1, grid=(nl, 4, per),
            in_specs=[pl.BlockSpec(blk, lambda l, s, i, c: (l, s, c[0] * per + i, 0)), pl.BlockSpec(blk, lambda l, s, i, c: (l, s, i, 0))],
            out_specs=pl.BlockSpec(blk, lambda l, s, i, c: (l, s, i, 0))),
        out_shape=_sds(r1.shape, BF16), compiler_params=_cp("parallel", "parallel", "parallel"))(c_arr, g, r1)


def shard_total(g, r1, r2, cs_arr, name):
    nl, _, a2, b = r1.shape
    ta = _half_tile(a2)
    per = a2 // ta

    def body(cs_ref, g_ref, r1_ref, p0_ref, p1_ref, p2_ref, o_ref):
        o_ref[...] = (((g_ref[...] + r1_ref[...]) + p0_ref[...].astype(F32)) + p1_ref[...].astype(F32)) + p2_ref[...].astype(F32)

    blk4, blk3 = (None, None, ta, b), (None, ta, b)
    peer = lambda k: pl.BlockSpec((None, None, ta, b), lambda l, i, cs: (k, l, i, 0))
    return pl.pallas_call(
        body, name=name,
        grid_spec=pltpu.PrefetchScalarGridSpec(
            num_scalar_prefetch=1, grid=(nl, per),
            in_specs=[pl.BlockSpec(blk4, lambda l, i, cs: (l, cs[1], cs[0] * per + i, 0)),
                      pl.BlockSpec(blk4, lambda l, i, cs: (l, cs[1], i, 0)), peer(0), peer(1), peer(2)],
            out_specs=pl.BlockSpec(blk3, lambda l, i, cs: (l, i, 0))),
        out_shape=_sds((nl, a2, b), F32), compiler_params=_cp("parallel", "parallel"))(cs_arr, g, r1, r2, r2, r2)


def allgather_small(v, name):
    r, n = v.shape

    def body(x_ref, out_ref, send_sems, recv_sems, local_sem):
        x, y, c = _place()
        me, sibling = (x, y, c), (x, y, 1 - c)
        chips = _other_chips(x, y)

        def rows(px, py, pc):
            return out_ref.at[pl.ds((4 * px + 2 * py + pc) * r, r), :]

        def copy(k, block, to, src=None):
            return _remote(rows(*block) if src is None else src, rows(*block), send_sems, recv_sems, k, to)

        mine = pltpu.make_async_copy(x_ref, rows(*me), local_sem)
        mine.start()
        first = [copy(0, me, sibling, src=x_ref)] + [copy(1 + j, me, (*chip, c), src=x_ref) for j, chip in enumerate(chips)]
        for cp in first:
            cp.start()
        passed = [copy(4 + j, (*chip, c), sibling) for j, chip in enumerate(chips)]
        for j, chip in enumerate(chips):
            copy(1 + j, (*chip, c), me).wait_recv()
            passed[j].start()
        copy(0, sibling, me).wait_recv()
        for j, chip in enumerate(chips):
            copy(4 + j, (*chip, 1 - c), me).wait_recv()
        for cp in first + passed:
            cp.wait_send()
        mine.wait()

    return pl.pallas_call(
        body, name=name, in_specs=[VMEM_SPEC], out_specs=VMEM_SPEC, out_shape=_sds((8 * r, n), v.dtype),
        scratch_shapes=[pltpu.SemaphoreType.DMA((7,)), pltpu.SemaphoreType.DMA((7,)), pltpu.SemaphoreType.DMA],
        compiler_params=pltpu.CompilerParams(has_side_effects=True, vmem_limit_bytes=V7X_VMEM_LIMIT_BYTES))(v)


def sum_devices(v8, name):
    _, r, n = v8.shape
    tr = _tile(r, (88, 64, 32, 16, 8))

    def body(v_ref, o_ref):
        acc = v_ref[0]
        for d in range(1, 8):
            acc = acc + v_ref[d]
        o_ref[...] = acc

    return pl.pallas_call(
        body, name=name, grid=(r // tr,), in_specs=[pl.BlockSpec((8, tr, n), lambda i: (0, i, 0))],
        out_specs=pl.BlockSpec((tr, n), lambda i: (i, 0)), out_shape=_sds((r, n), F32), compiler_params=_cp("parallel"))(v8)


SHARDED = (("a_w_in", 2), ("a_w_out", 1), ("b_w_in", 2), ("b_w_out", 1), ("w_mem_kv", 1), ("w_gate_up", 2), ("w_down", 1))


def _usable(wg, axis):
    l, _, a, b = wg.shape
    return wg.reshape(l, 4 * a, b) if axis == 1 else wg


def _pack(arrs):
    parts = []
    for a in arrs:
        flat = a.reshape(-1)
        flat = jnp.pad(flat, (0, -flat.shape[0] % 1024))
        parts.append(flat.reshape(-1, 128))
    return jnp.concatenate(parts, axis=0)


def _unpack(buf, like):
    out, row = [], 0
    for a in like:
        size = 1
        for s in a.shape:
            size *= s
        rows = -(-size // 1024) * 8
        out.append(buf[row:row + rows].reshape(-1)[:size].reshape(a.shape))
        row += rows
    return out


def kernel(x, mem, mem_norm_g, mix_norm_g, ffn_norm_g, final_norm_g, a_w_in, a_sinks, a_w_out, b_w_in, b_w_s, b_bias_s, b_ln_g, b_ln_b, b_w_out, w_mem_kv, w_gate_up, w_down, loss_target, m_mem_norm_g, m_mix_norm_g, m_ffn_norm_g, m_final_norm_g, m_a_w_in, m_a_sinks, m_a_w_out, m_b_w_in, m_b_w_s, m_b_bias_s, m_b_ln_g, m_b_ln_b, m_b_w_out, m_w_mem_kv, m_w_gate_up, m_w_down, v_mem_norm_g, v_mix_norm_g, v_ffn_norm_g, v_final_norm_g, v_a_w_in, v_a_sinks, v_a_w_out, v_b_w_in, v_b_w_s, v_b_bias_s, v_b_ln_g, v_b_ln_b, v_b_w_out, v_w_mem_kv, v_w_gate_up, v_w_down):
    given = dict(locals())
    depth = mix_norm_g.shape[0]
    d = x.shape[-1]
    xi, yi, ci = _place()
    c_arr = jnp.stack([ci]).astype(jnp.int32)
    cs_arr = jnp.stack([ci, 2 * xi + yi]).astype(jnp.int32)

    axis_of = dict(SHARDED)
    own = {n: given[n].astype(BF16) for n, _ in SHARDED}
    MIXER, FFN = slice(0, 3), slice(3, 5)

    def layer_weights(l):
        mix = "a" if l % 2 == 0 else "b"
        return [(mix + "_w_in", l // 2), (mix + "_w_out", l // 2), ("w_mem_kv", l), ("w_gate_up", l), ("w_down", l)]

    def gather_of(l, part=slice(0, 5)):
        return gather_exchange([(own[n], k) for n, k in layer_weights(l)[part]])

    def usable(l, gathered, part=slice(0, 5)):
        return {n[2:] if n[0] in "ab" else n: (_usable(wg, axis_of[n]), 0) for (n, _), wg in zip(layer_weights(l)[part], gathered)}

    weights = {0: usable(0, run_exchange(gather_of(0, MIXER), "gather_weights"), MIXER)}

    h = x.reshape(-1, d)
    tgt = loss_target.reshape(-1, d)
    mem2 = mem.reshape(-1, d)
    row = lambda v: v.reshape(1, -1)

    mem_n = rmsnorm_fwd(mem2, row(mem_norm_g), "mem_norm")
    saved = []
    for i in range(depth):
        j = i // 2
        wl = weights[i]
        w_in, w_out = wl["w_in"], wl["w_out"]
        kv = matmul(mem_n, wl["w_mem_kv"], "nn", BF16, "mem_kv")
        if i % 2 == 0:
            sk = jnp.pad(jnp.broadcast_to(a_sinks[j][:, None], (Q_HEADS, 128)), ((0, 16 - Q_HEADS), (0, 0)))
            xn, proj = norm_matmul(h, row(mix_norm_g[i]), w_in, BF16, "a_in")
            cat, *gathered = mixer_a_fwd(proj, sk, kv, "mixer_a", rider=gather_of(0, FFN) if i == 0 else None)
            if gathered:
                wl.update(usable(0, gathered, FFN))
            extra = (sk,)
        else:
            bt = jnp.pad(b_bias_s[j].T, ((0, 0), (0, 128 - B_GROUPS)))
            lg = jnp.pad(b_ln_g[j], ((0, 8 - B_GROUPS), (0, 0)))
            lb = jnp.pad(b_ln_b[j], ((0, 8 - B_GROUPS), (0, 0)))
            xn, proj = norm_matmul(h, row(mix_norm_g[i]), w_in, BF16, "b_in")
            cat = mixer_b_fwd(proj, b_w_s[j], bt, lg, lb, kv, "mixer_b")
            extra = (b_w_s[j], bt, lg, lb)
        h_mid = matmul(cat, w_out, "nn", F32, "mix_out", res=h)
        more = i + 1 < depth
        hn, gu, act, *gathered = gate_up_fwd(h_mid, row(ffn_norm_g[i]), *wl["w_gate_up"], "gate_up",
                                             rider=gather_of(i + 1, FFN) if more else None)
        h_out, *gathered_mixer = matmul(act, wl["w_down"], "nn", F32, "down", res=h_mid, rider=gather_of(i + 1, MIXER)) if more \
            else [matmul(act, wl["w_down"], "nn", F32, "down", res=h_mid)]
        if more:
            weights[i + 1] = {**usable(i + 1, gathered, FFN), **usable(i + 1, gathered_mixer, MIXER)}
        saved.append((h, xn, proj, cat, h_mid, hn, gu, act, kv, extra))
        h = h_out

    loss_part, dh, d_final_g = loss_head(h, row(final_norm_g), tgt, "loss_head")
    loss = lax.psum(loss_part[0, 0], ("x", "y", "c"))

    d_mix_g, d_ffn_g = [None] * depth, [None] * depth
    d_sinks, d_ws, d_bias, d_lg, d_lb = [], [], [], [], []
    d_mem_n = jnp.zeros(mem2.shape, F32)
    totals = [None] * depth
    pending = None
    for i in reversed(range(depth)):
        h_in, xn, proj, cat, h_mid, hn, gu, act, kv, extra = saved[i]
        wl = weights[i]
        dgu, *from_sibling = down_dx_swiglu_bwd(dh, wl["w_down"], gu, "down_dx",
                                                rider=sibling_halves_exchange(pending) if pending else None)
        if pending:
            partial = [chip_partial_sums(g, r1, c_arr, "grads_chip_sum") for g, r1 in zip(pending, from_sibling)]
        dw_down = matmul(act, dh, "tn", F32, "down_dw", tm=1408, out_planes=("rows", 4))
        dw_gate_up = matmul((hn, 0), dgu, "tn", F32, "gate_up_dw", tn=1408, tk=2048, out_planes=("cols", 4))
        ffn = [dw_gate_up[None], dw_down[None]] if i == 0 else []
        joined = lambda exs: None if not exs else exs[0] if len(exs) == 1 else both_exchanges(*exs)
        dh, d_ffn_g[i], *landed = dx_norm_bwd(
            dgu, wl["w_gate_up"], h_mid, row(ffn_norm_g[i]), dh, "gate_up_dx",
            rider=joined(([chips_exchange(partial[FFN])] if pending else []) + ([sibling_halves_exchange(ffn)] if ffn else [])))
        if pending:
            chips_ffn = landed[:2]
        if ffn:
            ffn_sibling = landed[-len(ffn):]
            ffn_partial = [chip_partial_sums(g, r1, c_arr, "grads_chip_sum") for g, r1 in zip(ffn, ffn_sibling)]
        dcat = matmul(dh, wl["w_out"], "nt", F32, "mix_out_dx")
        dw_out = matmul(cat, dh, "tn", F32, "mix_out_dw", out_planes=("rows", 4))
        mixer_rider = joined(([chips_exchange(partial[MIXER])] if pending else []) + ([chips_exchange(ffn_partial)] if ffn else []))
        if i % 2 == 0:
            dproj, dsk, dkv, *landed = mixer_a_bwd(proj, dcat, extra[0], kv, "mixer_a_bwd", rider=mixer_rider)
            d_sinks.insert(0, dsk[:Q_HEADS, 0])
            dw_in = matmul(xn, dproj, "tn", F32, "a_in_dw", out_planes=("cols", 4))
        else:
            dproj, dws, dbt, dlg, dlb, dkv, *landed = mixer_b_bwd(proj, dcat, *extra, kv, "mixer_b_bwd", rider=mixer_rider)
            d_ws.insert(0, dws)
            d_bias.insert(0, dbt[:, :B_GROUPS].T)
            d_lg.insert(0, dlg[:B_GROUPS])
            d_lb.insert(0, dlb[:B_GROUPS])
            dw_in = matmul(xn, dproj, "tn", F32, "b_in_dw", out_planes=("cols", 4))
        if pending:
            from_chips = landed[:3] + chips_ffn
            totals[i + 1] = [shard_total(g, r1, r2, cs_arr, "grads_shard_total") for g, r1, r2 in zip(pending, from_sibling, from_chips)]
        if ffn:
            ffn_totals = [shard_total(g, r1, r2, cs_arr, "grads_shard_total") for g, r1, r2 in zip(ffn, ffn_sibling, landed[-len(ffn):])]
        dw_kv = matmul(mem_n, dkv, "tn", F32, "mem_kv_dw", out_planes=("rows", 4))
        d_mem_n = matmul(dkv, wl["w_mem_kv"], "nt", F32, "mem_kv_dx", res=d_mem_n)
        dh, d_mix_g[i] = dx_norm_bwd(dproj, wl["w_in"], h_in, row(mix_norm_g[i]), dh, "in_dx")
        pending = [dw_in[None], dw_out[None], dw_kv[None]] + ([] if ffn else [dw_gate_up[None], dw_down[None]])
    grad_x = dh.reshape(x.shape)
    _, d_mem_g = rmsnorm_bwd(mem2, row(mem_norm_g), d_mem_n, jnp.zeros(mem2.shape, F32), "mem_norm_bwd")

    from_sibling = run_exchange(sibling_halves_exchange(pending), "grads_sibling_swap")
    partial = [chip_partial_sums(g, r1, c_arr, "grads_chip_sum") for g, r1 in zip(pending, from_sibling)]
    from_chips = run_exchange(chips_exchange(partial), "grads_chips_exchange")
    totals[0] = [shard_total(g, r1, r2, cs_arr, "grads_shard_total") for g, r1, r2 in zip(pending, from_sibling, from_chips)] + ffn_totals

    mine = {n: [None] * given[n].shape[0] for n, _ in SHARDED}
    for l in range(depth):
        for (n, k), tot in zip(layer_weights(l), totals[l]):
            mine[n][k] = tot
    mine = [jnp.concatenate(mine[n], axis=0) for n, _ in SHARDED]
    theirs = run_exchange(sibling_exchange(mine), "grads_sibling_totals")
    out = {}
    for (n, _), g_mine, g_theirs in zip(SHARDED, mine, theirs):
        shape = given[n].shape
        two_d = lambda a: a.reshape(-1, shape[-1])
        res = adamw_halves(two_d(given[n]), two_d(g_mine), two_d(g_theirs), two_d(given["m_" + n]), two_d(given["v_" + n]),
                           c_arr, shape[1], "adamw")
        out[n] = tuple(r.reshape(shape) for r in res)

    small = ("mem_norm_g", "mix_norm_g", "ffn_norm_g", "final_norm_g", "a_sinks", "b_w_s", "b_bias_s", "b_ln_g", "b_ln_b")
    small_g = [d_mem_g[0], jnp.concatenate(d_mix_g, axis=0), jnp.concatenate(d_ffn_g, axis=0), d_final_g[0],
               jnp.stack(d_sinks), jnp.stack(d_ws), jnp.stack(d_bias), jnp.stack(d_lg), jnp.stack(d_lb)]
    packed = _pack(small_g)
    g_small = sum_devices(allgather_small(packed, "small_allgather").reshape(8, *packed.shape), "small_sum")
    like = [given[n] for n in small]
    delta_s, new_m_s, new_v_s = adamw(_pack(like), g_small, _pack([given["m_" + n] for n in small]),
                                      _pack([given["v_" + n] for n in small]), "adamw_small")
    for n, g, dl, nm_, nv_ in zip(small, _unpack(g_small, like), _unpack(delta_s, like), _unpack(new_m_s, like), _unpack(new_v_s, like)):
        out[n] = (g, dl, nm_, nv_)

    order = ("mem_norm_g", "mix_norm_g", "ffn_norm_g", "final_norm_g", "a_w_in", "a_sinks", "a_w_out", "b_w_in", "b_w_s",
             "b_bias_s", "b_ln_g", "b_ln_b", "b_w_out", "w_mem_kv", "w_gate_up", "w_down")
    return (loss, grad_x, *[out[n][0] for n in order], *[out[n][1] for n in order],
            *[out[n][2] for n in order], *[out[n][3] for n in order])
```

```python
import jax
import jax.numpy as jnp
from jax import lax
from jax.experimental import pallas as pl
from jax.experimental.pallas import tpu as pltpu

F32, BF16 = jnp.float32, jnp.bfloat16
EPS = 1e-6
HEAD_DIM = 64
Q_HEADS, KV_HEADS, GROUP = 12, 2, 6
WINDOW = 128
MEM_HEADS = 4
B_GROUPS = 6
Q_W, KV_W, MEM_W, B_W = 768, 128, 256, 768
SCALE = HEAD_DIM ** -0.5
NEG = -1e30
ADAM_LR, ADAM_B1, ADAM_B2, ADAM_EPS, ADAM_WD, ADAM_STEP = 0.001, 0.9, 0.999, 1e-08, 0.01, 10
V7X_VMEM_LIMIT_BYTES = 48 * 1024 * 1024
MESH = pl.DeviceIdType.MESH
HBM_SPEC = pl.BlockSpec(memory_space=pltpu.HBM)
VMEM_SPEC = pl.BlockSpec(memory_space=pltpu.VMEM)


def _cp(*sem):
    return pltpu.CompilerParams(dimension_semantics=sem or None, vmem_limit_bytes=V7X_VMEM_LIMIT_BYTES)


def _tile(n, cands):
    for t in cands:
        if n % t == 0:
            return t
    return n


def _sds(shape, dtype):
    return jax.ShapeDtypeStruct(tuple(shape), dtype)


def _dot(a, b, ca, cb):
    return lax.dot_general(a, b, (((ca,), (cb,)), ((), ())), preferred_element_type=F32)


def _rms(x, g):
    return x * lax.rsqrt(jnp.mean(x * x, axis=-1, keepdims=True) + EPS) * g


def rmsnorm_fwd(h, g, name):
    t, d = h.shape
    tm = _tile(t, (512, 256, 128))

    def body(h_ref, g_ref, o_ref):
        o_ref[...] = _rms(h_ref[...], g_ref[...]).astype(o_ref.dtype)

    return pl.pallas_call(
        body, name=name, grid=(t // tm,),
        in_specs=[pl.BlockSpec((tm, d), lambda i: (i, 0)), pl.BlockSpec((1, d), lambda i: (0, 0))],
        out_specs=pl.BlockSpec((tm, d), lambda i: (i, 0)),
        out_shape=_sds((t, d), BF16), compiler_params=_cp("parallel"))(h, g)


def rmsnorm_bwd(h, g, dxn, dres, name):
    t, d = h.shape
    tm = _tile(t, (512, 256, 128))

    def body(h_ref, g_ref, dxn_ref, dres_ref, dh_ref, dg_ref):
        _, vjp = jax.vjp(_rms, h_ref[...], g_ref[...])
        dh, dg = vjp(dxn_ref[...].astype(F32))
        dh_ref[...] = dres_ref[...] + dh

        @pl.when(pl.program_id(0) == 0)
        def _():
            dg_ref[...] = jnp.zeros_like(dg_ref)

        dg_ref[...] += dg

    row = pl.BlockSpec((tm, d), lambda i: (i, 0))
    vec = pl.BlockSpec((1, d), lambda i: (0, 0))
    return pl.pallas_call(
        body, name=name, grid=(t // tm,), in_specs=[row, vec, row, row], out_specs=[row, vec],
        out_shape=[_sds((t, d), F32), _sds((1, d), F32)], compiler_params=_cp("arbitrary"))(h, g, dxn, dres)


def loss_head(h, g, tgt, name):
    t, d = h.shape
    tm = _tile(t, (512, 256, 128))

    def body(h_ref, g_ref, t_ref, l_ref, dh_ref, dg_ref):
        y, vjp = jax.vjp(_rms, h_ref[...], g_ref[...])
        err = y - t_ref[...]
        dh, dg = vjp(err * (1.0 / d))
        dh_ref[...] = dh
        part = 0.5 * jnp.sum(jnp.mean(err * err, axis=-1, keepdims=True), axis=0, keepdims=True)

        @pl.when(pl.program_id(0) == 0)
        def _():
            dg_ref[...] = jnp.zeros_like(dg_ref)
            l_ref[...] = jnp.zeros_like(l_ref)

        dg_ref[...] += dg
        l_ref[...] += part

    row = pl.BlockSpec((tm, d), lambda i: (i, 0))
    vec = pl.BlockSpec((1, d), lambda i: (0, 0))
    one = pl.BlockSpec((1, 1), lambda i: (0, 0))
    return pl.pallas_call(
        body, name=name, grid=(t // tm,), in_specs=[row, vec, row], out_specs=[one, row, vec],
        out_shape=[_sds((1, 1), F32), _sds((t, d), F32), _sds((1, d), F32)], compiler_params=_cp("arbitrary"))(h, g, tgt)


def _logical(op):
    arr, lead = op if isinstance(op, tuple) else (op, None)
    planes = arr.shape[-3] if arr.ndim - (lead is not None) == 3 else 1
    return arr, lead, arr.shape[-2], arr.shape[-1], planes


def _spec(op, rows_t, cols_t, row_of, col_of):
    arr, lead, _, cols, _ = _logical(op)
    per = cols // cols_t
    lead = () if lead is None else (lead,)
    if arr.ndim - len(lead) == 2:
        return pl.BlockSpec((None,) * len(lead) + (rows_t, cols_t), lambda *g: lead + (row_of(*g), col_of(*g)))
    return pl.BlockSpec((None,) * len(lead) + (None, rows_t, cols_t),
                        lambda *g: lead + (col_of(*g) // per, row_of(*g), col_of(*g) % per))


def _arr(op):
    return op[0] if isinstance(op, tuple) else op


def _resident_whole(w, d):
    wa, layer = w
    planes, per = wa.shape[-3], wa.shape[-1]

    def fill(w_ref, whole_ref):
        for s in range(planes):
            whole_ref[:, s * per:(s + 1) * per] = w_ref[s]

    return (pl.BlockSpec((None, planes, d, per), lambda i: (layer, 0, 0, 0), pipeline_mode=pl.Buffered(1)),
            pltpu.VMEM((d, planes * per), wa.dtype), fill)


def norm_matmul(h, g, w, out_dtype, name):
    t, d = h.shape
    w_spec, whole, fill = _resident_whole(w, d)
    n = whole.shape[1]
    tm = _tile(t, (512, 256, 128))

    def body(h_ref, g_ref, w_ref, xn_ref, o_ref, whole_ref):
        @pl.when(pl.program_id(0) == 0)
        def _():
            fill(w_ref, whole_ref)

        xn = _rms(h_ref[...], g_ref[...]).astype(BF16)
        xn_ref[...] = xn
        o_ref[...] = _dot(xn, whole_ref[...], 1, 0).astype(o_ref.dtype)

    return pl.pallas_call(
        body, name=name, grid=(t // tm,),
        in_specs=[pl.BlockSpec((tm, d), lambda i: (i, 0)), pl.BlockSpec((1, d), lambda i: (0, 0)), w_spec],
        out_specs=[pl.BlockSpec((tm, d), lambda i: (i, 0)), pl.BlockSpec((tm, n), lambda i: (i, 0))],
        out_shape=[_sds((t, d), BF16), _sds((t, n), out_dtype)], scratch_shapes=[whole],
        compiler_params=_cp("arbitrary"))(h, g, w[0])


def dx_norm_bwd(dy, w, h, g, dres, name, rider=None):
    t, d = h.shape
    dy_arr, dy_lead, _, kc, kp = _logical(dy)
    w_arr, w_lead, _, wc, wp = _logical(w)
    assert kc * kp == wc * wp and dy_lead is None, name
    chunk = min(kc, wc)
    tm = _tile(t, (512, 256, 128))

    def piece(ref, planes, cols, q):
        off = q * chunk % cols
        return ref[q * chunk // cols, :, off:off + chunk] if planes > 1 else ref[:, off:off + chunk]

    narrow = wp > 1 and wc % 128 != 0
    if narrow:
        assert kp == 1, name
        w_whole_spec, whole, fill = _resident_whole(w, d)

    def body(dy_ref, w_ref, h_ref, g_ref, dres_ref, dh_ref, dg_ref, *whole_ref):
        if narrow:
            @pl.when(pl.program_id(0) == 0)
            def _():
                fill(w_ref, whole_ref[0])

            dxn = _dot(dy_ref[...].astype(BF16), whole_ref[0][...], 1, 1)
        else:
            dxn = None
            for q in range(kc * kp // chunk):
                p = _dot(piece(dy_ref, kp, kc, q).astype(BF16), piece(w_ref, wp, wc, q), 1, 1)
                dxn = p if dxn is None else dxn + p
        _, vjp = jax.vjp(_rms, h_ref[...], g_ref[...])
        dh, dg = vjp(dxn)
        dh_ref[...] = dres_ref[...] + dh

        @pl.when(pl.program_id(0) == 0)
        def _():
            dg_ref[...] = jnp.zeros_like(dg_ref)

        dg_ref[...] += dg

    w_lead = () if w_lead is None else (w_lead,)
    w_block = ((wp,) if wp > 1 else ()) + (d, wc)
    w_spec = pl.BlockSpec((None,) * len(w_lead) + w_block, lambda i: w_lead + (0,) * len(w_block), pipeline_mode=pl.Buffered(1))
    if narrow:
        w_spec = w_whole_spec
    dy_spec = pl.BlockSpec((kp, tm, kc), lambda i: (0, i, 0)) if kp > 1 else pl.BlockSpec((tm, kc), lambda i: (i, 0))
    row = pl.BlockSpec((tm, d), lambda i: (i, 0))
    vec = pl.BlockSpec((1, d), lambda i: (0, 0))
    grid = (t // tm,)
    body, r_ops, r_in, r_shapes, r_out, r_scratch = with_rider(body, 5, 2, grid, rider)
    return pl.pallas_call(
        body, name=name, grid=grid, in_specs=[dy_spec, w_spec, row, vec, row] + r_in,
        out_specs=[row, vec] + r_out, out_shape=[_sds((t, d), F32), _sds((1, d), F32)] + r_shapes,
        scratch_shapes=([whole] if narrow else []) + r_scratch,
        compiler_params=_cp("arbitrary"))(dy_arr, w_arr, h, g, dres, *r_ops)


def matmul(a, b, mode, out_dtype, name, res=None, tm=None, tn=1792, tk=2816, out_planes=None, out_into=None, rider=None):
    _, _, ar, ac, ap = _logical(a)
    _, _, br, bc, bp = _logical(b)
    if mode == "nn":
        m, ka, kb, n = ar, ac * ap, br, bc * bp
        n_plane, ka_plane, kb_plane = bc, ac, br
    elif mode == "nt":
        m, ka, n, kb = ar, ac * ap, br, bc * bp
        n_plane, ka_plane, kb_plane = br, ac, bc
    else:
        ka, m, kb, n = ar, ac * ap, br, bc * bp
        n_plane, ka_plane, kb_plane = bc, ar, br
    m_plane = ac if mode == "tn" else ar
    assert ka == kb, name
    k = ka
    kind, planes = out_planes or ("cols", 1)
    narrow = kind == "cols" and (n // planes) % 128 != 0
    if kind == "cols" and not narrow:
        n_plane = min(n_plane, n // planes)
    if narrow:
        tn = n
    tm = _tile(m_plane, ((1024, 1408, 512, 256, 128) if mode == "tn" else (512, 256, 128)) if tm is None else (tm, 1024, 512, 256, 128))
    if kind == "rows" and tm % (m // planes):
        tm = m_plane
    tn = _tile(n_plane, (tn, 1792, 1408, 1280, 1024, 896, 640, 512, 256, 128))
    tk = _tile(min(ka_plane, kb_plane), (tk, 2816, 1792, 1408, 1280, 1024, 512, 256, 128))
    nk = k // tk
    row_i, col_j, red = (lambda i, j, kk: i), (lambda i, j, kk: j), (lambda i, j, kk: kk)
    if mode == "nn":
        a_spec, b_spec, ca, cb = _spec(a, tm, tk, row_i, red), _spec(b, tk, tn, red, col_j), 1, 0
    elif mode == "nt":
        a_spec, b_spec, ca, cb = _spec(a, tm, tk, row_i, red), _spec(b, tn, tk, col_j, red), 1, 1
    else:
        a_spec, b_spec, ca, cb = _spec(a, tk, tm, red, row_i), _spec(b, tk, tn, red, col_j), 0, 0
    lead = () if out_into is None else (out_into[1],)
    if planes == 1:
        o_shape, o_block = (m, n), (tm, tn)
        o_index = lambda i, j, kk: lead + (i, j)
    elif narrow:
        o_shape, o_block = (planes, m, n // planes), (planes, tm, n // planes)
        o_index = lambda i, j, kk: lead + (0, i, 0)
    elif kind == "cols":
        per = n // planes // tn
        o_shape, o_block = (planes, m, n // planes), (None, tm, tn)
        o_index = lambda i, j, kk: lead + (j // per, i, j % per)
    else:
        o_shape, o_block = (planes, m // planes, n), (tm // (m // planes), m // planes, tn)
        o_index = lambda i, j, kk: lead + (i, 0, j)
    o_spec = pl.BlockSpec((None,) * len(lead) + o_block, o_index)
    if out_into is not None:
        assert out_into[0].shape[1:] == o_shape and out_into[0].dtype == out_dtype, name
        o_shape = out_into[0].shape
    has_res = res is not None
    n_in = 2 + has_res + (out_into is not None)

    def put(o_ref, v):
        if narrow:
            for s in range(planes):
                o_ref[s] = v[:, s * (n // planes):(s + 1) * (n // planes)].astype(o_ref.dtype)
        else:
            o_ref[...] = v.astype(o_ref.dtype).reshape(o_ref.shape)

    def body(*refs):
        a_ref, b_ref = refs[:2]
        rest = refs[2:2 + has_res] + refs[n_in:]
        o_ref = rest[1] if has_res else rest[0]
        p = _dot(a_ref[...].astype(BF16), b_ref[...].astype(BF16), ca, cb)
        if nk == 1:
            if has_res:
                p = p + rest[0][...]
            put(o_ref, p)
        else:
            acc_ref = rest[-1]
            kk = pl.program_id(2)

            @pl.when(kk == 0)
            def _():
                acc_ref[...] = p

            @pl.when(kk > 0)
            def _():
                acc_ref[...] += p

            @pl.when(kk == nk - 1)
            def _():
                r = acc_ref[...]
                if has_res:
                    r = r + rest[0][...]
                put(o_ref, r)

    operands = [_arr(a), _arr(b)] + ([res] if has_res else []) + ([out_into[0]] if out_into is not None else [])
    grid = (m // tm, n // tn, nk)
    body, r_ops, r_in, r_shapes, r_out, r_scratch = with_rider(body, n_in, 1, grid, rider)
    out = pl.pallas_call(
        body, name=name, grid=grid,
        in_specs=[a_spec, b_spec] + ([pl.BlockSpec((tm, tn), lambda i, j, kk: (i, j))] if has_res else [])
        + ([pl.BlockSpec(memory_space=pl.ANY)] if out_into is not None else []) + r_in,
        out_specs=[o_spec] + r_out, out_shape=[_sds(o_shape, out_dtype)] + r_shapes,
        input_output_aliases={n_in - 1: 0} if out_into is not None else {},
        scratch_shapes=([pltpu.VMEM((tm, tn), F32)] if nk > 1 else []) + r_scratch,
        compiler_params=_cp(*(("arbitrary",) * 3 if rider else ("parallel", "parallel", "arbitrary"))))(*operands, *r_ops)
    return out if rider else out[0]


def gate_up_fwd(h, g, w, layer, name, rider=None):
    t, d = h.shape
    half = w.shape[-1]
    tm = _tile(t, (512, 256, 128))

    def body(h_ref, g_ref, wg_ref, wu_ref, hn_ref, gu_ref, act_ref):
        a = _rms(h_ref[...], g_ref[...]).astype(BF16)
        hn_ref[...] = a
        gate, up = _dot(a, wg_ref[...], 1, 0), _dot(a, wu_ref[...], 1, 0)
        sig = 1.0 / (1.0 + jnp.exp(-gate))
        silu = gate * sig
        gu_ref[0] = (up * (sig + silu * (1.0 - sig))).astype(gu_ref.dtype)
        gu_ref[1] = silu.astype(gu_ref.dtype)
        act_ref[...] = (silu * up).astype(act_ref.dtype)

    grid = (2, t // tm)
    body, r_ops, r_in, r_shapes, r_out, r_scratch = with_rider(body, 4, 3, grid, rider)
    return pl.pallas_call(
        body, name=name, grid=grid,
        in_specs=[pl.BlockSpec((tm, d), lambda j, i: (i, 0)), pl.BlockSpec((1, d), lambda j, i: (0, 0)),
                  pl.BlockSpec((None, None, d, half), lambda j, i: (layer, j, 0, 0)),
                  pl.BlockSpec((None, None, d, half), lambda j, i: (layer, 2 + j, 0, 0))] + r_in,
        out_specs=[pl.BlockSpec((None, tm, d), lambda j, i: (j, i, 0)), pl.BlockSpec((2, tm, half), lambda j, i: (0, i, j)),
                   pl.BlockSpec((tm, half), lambda j, i: (i, j))] + r_out,
        out_shape=[_sds((2, t, d), BF16), _sds((2, t, 2 * half), BF16), _sds((t, 2 * half), BF16)] + r_shapes,
        scratch_shapes=r_scratch, compiler_params=_cp("arbitrary", "arbitrary"))(h, g, w, w, *r_ops)


def down_dx_swiglu_bwd(dh, wd, gu, name, rider=None):
    t, d = dh.shape
    w, layer = wd
    f = w.shape[-2]
    tm = _tile(t, (512, 256, 128))
    tn = _tile(f, (1408, 512, 256, 128))

    def body(dh_ref, w_ref, gu_ref, o_ref):
        dact = _dot(dh_ref[...].astype(BF16), w_ref[...], 1, 1)
        o_ref[0] = (dact * gu_ref[0].astype(F32)).astype(o_ref.dtype)
        o_ref[1] = (dact * gu_ref[1].astype(F32)).astype(o_ref.dtype)

    planes = pl.BlockSpec((2, tm, tn), lambda j, i: (0, i, j))
    grid = (f // tn, t // tm)
    body, r_ops, r_in, r_shapes, r_out, r_scratch = with_rider(body, 3, 1, grid, rider)
    return pl.pallas_call(
        body, name=name, grid=grid,
        in_specs=[pl.BlockSpec((tm, d), lambda j, i: (i, 0)), pl.BlockSpec((None, tn, d), lambda j, i: (layer, j, 0)), planes] + r_in,
        out_specs=[planes] + r_out, out_shape=[_sds((2, t, f), BF16)] + r_shapes, scratch_shapes=r_scratch,
        compiler_params=_cp("arbitrary", "arbitrary"))(dh, w, gu, *r_ops)


def _softmax_over_keys(s, sink=None):
    m = s.max(axis=0, keepdims=True)
    if sink is not None:
        m = jnp.maximum(m, sink)
    m = lax.stop_gradient(m)
    e = jnp.exp(s - m)
    den = e.sum(axis=0, keepdims=True)
    if sink is not None:
        den = den + jnp.exp(sink - m)
    return e * (1.0 / den)


def _low_lanes():
    return lax.broadcasted_iota(jnp.int32, (1, 128), 1) < HEAD_DIM


def _stack_heads(slabs):
    low = _low_lanes()
    return jnp.concatenate([p for s in slabs for p in (jnp.where(low, s, 0.0), jnp.where(low, 0.0, s))], axis=0)


def _unstack_heads(o, n_slabs):
    low = _low_lanes()
    return [jnp.where(low, o[2 * j * WINDOW:(2 * j + 1) * WINDOW], o[(2 * j + 1) * WINDOW:(2 * j + 2) * WINDOW])
            for j in range(n_slabs)]


def _swa_group(q_slabs, k_both, v_both, sinks, mask):
    qs = _stack_heads(q_slabs).astype(BF16)
    s = jnp.where(mask, _dot(k_both.astype(BF16), qs, 1, 1) * SCALE, NEG)
    sink = jnp.concatenate([jnp.broadcast_to(v, (1, WINDOW)) for v in sinks], axis=1)
    return _unstack_heads(_dot(_softmax_over_keys(s, sink).astype(BF16), v_both.astype(BF16), 0, 0), len(q_slabs))


def _mem_pair(q_slab, k_slab, v_slab):
    s = _dot(k_slab.astype(BF16), _stack_heads([q_slab]).astype(BF16), 1, 1) * SCALE
    return _unstack_heads(_dot(_softmax_over_keys(s).astype(BF16), v_slab.astype(BF16), 0, 0), 1)[0]


def _gelu(x):
    return 0.5 * x * (1.0 + jnp.tanh(0.7978845608028654 * (x + 0.044715 * (x * x * x))))


def _gmlp_group(zu, zv, w, bcol, lg, lb, tri):
    u, v = _gelu(zu), _gelu(zv)
    mu = jnp.mean(v, axis=-1, keepdims=True)
    var = jnp.mean(jnp.square(v - mu), axis=-1, keepdims=True)
    vn = (v - mu) * lax.rsqrt(var + EPS) * lg + lb
    sv = _dot(jnp.where(tri, w, 0.0).astype(BF16), vn.astype(BF16), 1, 0) + bcol
    return u * sv


def _cols(x, width):
    return [x[:, j * width:(j + 1) * width] for j in range(x.shape[1] // width)]


def _swa_mask(has_prev):
    qi = lax.broadcasted_iota(jnp.int32, (2 * WINDOW, GROUP * WINDOW), 1) & (WINDOW - 1)
    kj = lax.broadcasted_iota(jnp.int32, (2 * WINDOW, GROUP * WINDOW), 0)
    in_prev = jnp.logical_and(jnp.logical_and(kj < WINDOW, kj > qi), has_prev)
    return jnp.logical_or(in_prev, jnp.logical_and(kj >= WINDOW, kj - WINDOW <= qi))


def _mix_a(q_slabs, k_boths, v_boths, sinks, qm_slabs, km_slabs, vm_slabs, mask):
    per = GROUP // 2
    outs = []
    for g in range(KV_HEADS):
        outs += _swa_group(q_slabs[per * g:per * (g + 1)], k_boths[g], v_boths[g], sinks[GROUP * g:GROUP * (g + 1)], mask)
    return outs + [_mem_pair(qm_slabs[j], km_slabs[j], vm_slabs[j]) for j in range(MEM_HEADS // 2)]


def _in_both_halves(prev, cur):
    cat = jnp.concatenate([prev, cur], axis=0)
    rolled = pltpu.roll(cat, HEAD_DIM, axis=1)
    low = _low_lanes()
    return [jnp.where(low, cat, rolled), jnp.where(low, rolled, cat)]


def _from_both_halves(d_boths):
    t = [d + pltpu.roll(d, HEAD_DIM, axis=1) for d in d_boths]
    return jnp.where(_low_lanes(), t[0], t[1])


def _mix_a_specs(nm, blk):
    prev = lambda n: jnp.maximum(blk(n) - 1, 0)
    return [pl.BlockSpec((WINDOW, Q_W), lambda n: (blk(n), 0)),
            pl.BlockSpec((WINDOW, KV_W), lambda n: (prev(n), Q_W // KV_W)),
            pl.BlockSpec((WINDOW, KV_W), lambda n: (blk(n), Q_W // KV_W)),
            pl.BlockSpec((WINDOW, KV_W), lambda n: (prev(n), Q_W // KV_W + 1)),
            pl.BlockSpec((WINDOW, KV_W), lambda n: (blk(n), Q_W // KV_W + 1)),
            pl.BlockSpec((WINDOW, MEM_W), lambda n: (blk(n), (Q_W + 2 * KV_W) // MEM_W)),
            pl.BlockSpec((16, 128), lambda n: (0, 0)),
            pl.BlockSpec((nm, MEM_W), lambda n: (0, 0)),
            pl.BlockSpec((nm, MEM_W), lambda n: (0, 1))]


def _mix_a_args(refs):
    q, kp, kc, vp, vc, qm, sk, km, vm = [r[...].astype(F32) for r in refs]
    return (_cols(q, 128), _in_both_halves(kp, kc), _in_both_halves(vp, vc), [sk[h:h + 1, 0:1] for h in range(Q_HEADS)],
            _cols(qm, 128), _cols(km, 128), _cols(vm, 128))


def mixer_a_fwd(proj, sk, kv, name, rider=None):
    t, nm = proj.shape[0], kv.shape[0]

    def body(*refs):
        o_ref = refs[-1]
        slabs = _mix_a(*_mix_a_args(refs[:-1]), _swa_mask(pl.program_id(0) > 0))
        o_ref[...] = jnp.concatenate(slabs, axis=1).astype(o_ref.dtype)

    grid = (t // WINDOW,)
    body, r_ops, r_in, r_shapes, r_out, r_scratch = with_rider(body, 9, 1, grid, rider)
    return pl.pallas_call(
        body, name=name, grid=grid, in_specs=_mix_a_specs(nm, lambda n: n) + r_in,
        out_specs=[pl.BlockSpec((WINDOW, Q_W + MEM_W), lambda n: (n, 0))] + r_out,
        out_shape=[_sds((t, Q_W + MEM_W), BF16)] + r_shapes, scratch_shapes=r_scratch,
        compiler_params=_cp("arbitrary"))(proj, proj, proj, proj, proj, proj, sk, kv, kv, *r_ops)


def _onehot_rows(vals, shape):
    rows = lax.broadcasted_iota(jnp.int32, shape, 0)
    out = jnp.zeros(shape, F32)
    for h, v in enumerate(vals):
        out = out + jnp.where(rows == h, jnp.broadcast_to(v, shape), 0.0)
    return out


def mixer_a_bwd(proj, dcat, sk, kv, name, rider=None):
    t, nm = proj.shape[0], kv.shape[0]
    nb = t // WINDOW
    blk = lambda i: nb - 1 - i

    def body(*refs):
        dcat_ref, dproj_ref, dsk_ref, dkv_ref, carry_ref = refs[9:]
        i = pl.program_id(0)

        @pl.when(i == 0)
        def _():
            carry_ref[...] = jnp.zeros_like(carry_ref)
            dsk_ref[...] = jnp.zeros_like(dsk_ref)
            dkv_ref[...] = jnp.zeros_like(dkv_ref)

        mask = _swa_mask(blk(i) > 0)
        _, vjp = jax.vjp(lambda *a: _mix_a(*a, mask), *_mix_a_args(refs[:9]))
        dqs, dk_boths, dv_boths, dsinks, dqms, dkms, dvms = vjp(_cols(dcat_ref[...].astype(F32), 128))
        dkv = jnp.concatenate([_from_both_halves(dk_boths), _from_both_halves(dv_boths)], axis=1)
        dkv_cur = dkv[WINDOW:] + carry_ref[...]
        carry_ref[...] = dkv[:WINDOW]
        dproj_ref[...] = jnp.concatenate(dqs + [dkv_cur] + dqms, axis=1).astype(dproj_ref.dtype)
        dsk_ref[...] += _onehot_rows(dsinks, (16, 128))
        dkv_ref[...] += jnp.concatenate(dkms + dvms, axis=1)

    width = Q_W + 2 * KV_W + MEM_W
    body, r_ops, r_in, r_shapes, r_out, r_scratch = with_rider(body, 10, 3, (nb,), rider)
    return pl.pallas_call(
        body, name=name, grid=(nb,),
        in_specs=_mix_a_specs(nm, blk) + [pl.BlockSpec((WINDOW, Q_W + MEM_W), lambda i: (blk(i), 0))] + r_in,
        out_specs=[pl.BlockSpec((WINDOW, width), lambda i: (blk(i), 0)), pl.BlockSpec((16, 128), lambda i: (0, 0)),
                   pl.BlockSpec((nm, 2 * MEM_W), lambda i: (0, 0))] + r_out,
        out_shape=[_sds((t, width), BF16), _sds((16, 128), F32), _sds((nm, 2 * MEM_W), F32)] + r_shapes,
        scratch_shapes=[pltpu.VMEM((WINDOW, 2 * KV_W), F32)] + r_scratch,
        compiler_params=_cp("arbitrary"))(proj, proj, proj, proj, proj, proj, sk, kv, kv, dcat, *r_ops)


def _mix_b(zus, zvs, ws, bcols, lgs, lbs, qms, kms, vms, tri):
    outs = [_gmlp_group(zus[g], zvs[g], ws[g], bcols[g], lgs[g], lbs[g], tri) for g in range(B_GROUPS)]
    return outs + [_mem_pair(qms[j], kms[j], vms[j]) for j in range(MEM_HEADS // 2)]


def _mix_b_specs(nm):
    return [pl.BlockSpec((WINDOW, 2 * B_W), lambda n: (n, 0)),
            pl.BlockSpec((WINDOW, MEM_W), lambda n: (n, 2 * B_W // MEM_W)),
            pl.BlockSpec((B_GROUPS, WINDOW, WINDOW), lambda n: (0, 0, 0)),
            pl.BlockSpec((WINDOW, 128), lambda n: (0, 0)),
            pl.BlockSpec((8, 128), lambda n: (0, 0)),
            pl.BlockSpec((8, 128), lambda n: (0, 0)),
            pl.BlockSpec((nm, MEM_W), lambda n: (0, 0)),
            pl.BlockSpec((nm, MEM_W), lambda n: (0, 1))]


def _mix_b_args(refs):
    z, qm, ws, bt, lg, lb, km, vm = [r[...].astype(F32) for r in refs]
    zs = _cols(z, 128)
    return (zs[:B_GROUPS], zs[B_GROUPS:], [ws[g] for g in range(B_GROUPS)], [bt[:, g:g + 1] for g in range(B_GROUPS)],
            [lg[g:g + 1, :] for g in range(B_GROUPS)], [lb[g:g + 1, :] for g in range(B_GROUPS)],
            _cols(qm, 128), _cols(km, 128), _cols(vm, 128))


def _tri():
    return lax.broadcasted_iota(jnp.int32, (WINDOW, WINDOW), 0) >= lax.broadcasted_iota(jnp.int32, (WINDOW, WINDOW), 1)


def mixer_b_fwd(proj, ws, bt, lg, lb, kv, name):
    t, nm = proj.shape[0], kv.shape[0]

    def body(*refs):
        o_ref = refs[-1]
        o_ref[...] = jnp.concatenate(_mix_b(*_mix_b_args(refs[:-1]), _tri()), axis=1).astype(o_ref.dtype)

    return pl.pallas_call(
        body, name=name, grid=(t // WINDOW,), in_specs=_mix_b_specs(nm),
        out_specs=pl.BlockSpec((WINDOW, B_W + MEM_W), lambda n: (n, 0)),
        out_shape=_sds((t, B_W + MEM_W), BF16), compiler_params=_cp("parallel"))(proj, proj, ws, bt, lg, lb, kv, kv)


def mixer_b_bwd(proj, dcat, ws, bt, lg, lb, kv, name, rider=None):
    t, nm = proj.shape[0], kv.shape[0]

    def body(*refs):
        dcat_ref, dproj_ref, dws_ref, dbt_ref, dlg_ref, dlb_ref, dkv_ref = refs[8:]

        @pl.when(pl.program_id(0) == 0)
        def _():
            for r in (dws_ref, dbt_ref, dlg_ref, dlb_ref, dkv_ref):
                r[...] = jnp.zeros_like(r)

        tri = _tri()
        zus, zvs, ws, bcols, lgs, lbs, qms, kms, vms = _mix_b_args(refs[:8])
        douts = _cols(dcat_ref[...].astype(F32), 128)
        grads = []
        for g in range(B_GROUPS):
            _, vjp = jax.vjp(lambda *a: _gmlp_group(*a, tri), zus[g], zvs[g], ws[g], bcols[g], lgs[g], lbs[g])
            grads.append(vjp(douts[g]))
        dzus, dzvs, dws, dbcols, dlgs, dlbs = [list(t) for t in zip(*grads)]
        grads = []
        for j in range(MEM_HEADS // 2):
            _, vjp = jax.vjp(_mem_pair, qms[j], kms[j], vms[j])
            grads.append(vjp(douts[B_GROUPS + j]))
        dqms, dkms, dvms = [list(t) for t in zip(*grads)]
        dproj_ref[...] = jnp.concatenate(dzus + dzvs + dqms, axis=1).astype(dproj_ref.dtype)
        for g in range(B_GROUPS):
            dws_ref[g] += dws[g]
        lanes = lax.broadcasted_iota(jnp.int32, (WINDOW, 128), 1)
        dbt = jnp.zeros((WINDOW, 128), F32)
        for g in range(B_GROUPS):
            dbt = dbt + jnp.where(lanes == g, jnp.broadcast_to(dbcols[g], (WINDOW, 128)), 0.0)
        dbt_ref[...] += dbt
        dlg_ref[...] += _onehot_rows(dlgs, (8, 128))
        dlb_ref[...] += _onehot_rows(dlbs, (8, 128))
        dkv_ref[...] += jnp.concatenate(dkms + dvms, axis=1)

    width = 2 * B_W + MEM_W
    const2 = lambda n: (0, 0)
    grid = (t // WINDOW,)
    body, r_ops, r_in, r_shapes, r_out, r_scratch = with_rider(body, 9, 6, grid, rider)
    return pl.pallas_call(
        body, name=name, grid=grid,
        in_specs=_mix_b_specs(nm) + [pl.BlockSpec((WINDOW, B_W + MEM_W), lambda n: (n, 0))] + r_in,
        out_specs=[pl.BlockSpec((WINDOW, width), lambda n: (n, 0)),
                   pl.BlockSpec((B_GROUPS, WINDOW, WINDOW), lambda n: (0, 0, 0)),
                   pl.BlockSpec((WINDOW, 128), const2), pl.BlockSpec((8, 128), const2), pl.BlockSpec((8, 128), const2),
                   pl.BlockSpec((nm, 2 * MEM_W), const2)] + r_out,
        out_shape=[_sds((t, width), BF16), _sds((B_GROUPS, WINDOW, WINDOW), F32), _sds((WINDOW, 128), F32),
                   _sds((8, 128), F32), _sds((8, 128), F32), _sds((nm, 2 * MEM_W), F32)] + r_shapes,
        scratch_shapes=r_scratch, compiler_params=_cp("arbitrary"))(proj, proj, ws, bt, lg, lb, kv, kv, dcat, *r_ops)


def _adamw_update(w, g, m, v):
    m2 = ADAM_B1 * m + (1.0 - ADAM_B1) * g
    v2 = ADAM_B2 * v + (1.0 - ADAM_B2) * jnp.square(g)
    m_hat = m2 / (1.0 - ADAM_B1 ** ADAM_STEP)
    v_hat = v2 / (1.0 - ADAM_B2 ** ADAM_STEP)
    return -ADAM_LR * (m_hat / (jnp.sqrt(v_hat) + ADAM_EPS) + ADAM_WD * w), m2, v2


def adamw(w, g, m, v, name):
    r, c = w.shape
    tr = _tile(r, (512, 352, 256, 128, 64, 32, 16, 8))

    def body(w_ref, g_ref, m_ref, v_ref, d_ref, nm_ref, nv_ref):
        d_ref[...], nm_ref[...], nv_ref[...] = _adamw_update(w_ref[...], g_ref[...], m_ref[...], v_ref[...])

    spec = pl.BlockSpec((tr, c), lambda i: (i, 0))
    return pl.pallas_call(
        body, name=name, grid=(r // tr,), in_specs=[spec] * 4, out_specs=[spec] * 3,
        out_shape=[_sds((r, c), F32)] * 3, compiler_params=_cp("parallel"))(w, g, m, v)


def adamw_halves(w, g_mine, g_theirs, m, v, c_arr, rows, name):
    r, c = w.shape
    tr = _tile(rows // 2, (256, 352, 224, 160, 128, 64, 32, 16, 8))
    per_half = rows // 2 // tr

    def body(c_ref, w_ref, gm_ref, gt_ref, m_ref, v_ref, g_ref, d_ref, nm_ref, nv_ref):
        g = jnp.where(pl.program_id(0) // per_half % 2 == c_ref[0], gm_ref[...], gt_ref[...])
        g_ref[...] = g
        d_ref[...], nm_ref[...], nv_ref[...] = _adamw_update(w_ref[...], g, m_ref[...], v_ref[...])

    spec = pl.BlockSpec((tr, c), lambda i, cr: (i, 0))
    half = pl.BlockSpec((tr, c), lambda i, cr: (i // (2 * per_half) * per_half + i % per_half, 0))
    return pl.pallas_call(
        body, name=name,
        grid_spec=pltpu.PrefetchScalarGridSpec(num_scalar_prefetch=1, grid=(r // tr,), in_specs=[spec, half, half, spec, spec],
                                               out_specs=[spec] * 4),
        out_shape=[_sds((r, c), F32)] * 4, compiler_params=_cp("parallel"))(c_arr, w, g_mine, g_theirs, m, v)


def _place():
    return lax.axis_index("x"), lax.axis_index("y"), lax.axis_index("c")


def _other_chips(x, y):
    return [(1 - x, y), (x, 1 - y), (1 - x, 1 - y)]


def _remote(src, dst, send_sems, recv_sems, k, dev):
    return pltpu.make_async_remote_copy(src_ref=src, dst_ref=dst, send_sem=send_sems.at[k], recv_sem=recv_sems.at[k],
                                        device_id=dev, device_id_type=MESH)


class Exchange:
    def __init__(self, ins, out_shapes, n_sems, start, finish):
        self.ins, self.out_shapes, self.start, self.finish = list(ins), list(out_shapes), start, finish
        self.sems = [n_sems, n_sems] if isinstance(n_sems, int) else list(n_sems)

    def scratch(self):
        return [pltpu.SemaphoreType.DMA((n,)) for n in self.sems]


def both_exchanges(a, b):
    ni, no, ns = len(a.ins), len(a.out_shapes), len(a.sems)

    def start(ins, outs, *sems):
        a.start(ins[:ni], outs[:no], *sems[:ns])
        b.start(ins[ni:], outs[no:], *sems[ns:])

    def finish(ins, outs, *sems):
        a.finish(ins[:ni], outs[:no], *sems[:ns])
        b.finish(ins[ni:], outs[no:], *sems[ns:])

    return Exchange(a.ins + b.ins, a.out_shapes + b.out_shapes, a.sems + b.sems, start, finish)


def run_exchange(ex, name):
    ni, no = len(ex.ins), len(ex.out_shapes)

    def body(*refs):
        ex.start(refs[:ni], refs[ni:ni + no], *refs[ni + no:])
        ex.finish(refs[:ni], refs[ni:ni + no], *refs[ni + no:])

    return pl.pallas_call(
        body, name=name, in_specs=[HBM_SPEC] * ni, out_specs=[HBM_SPEC] * no, out_shape=ex.out_shapes, scratch_shapes=ex.scratch(),
        compiler_params=pltpu.CompilerParams(has_side_effects=True))(*ex.ins)


def with_rider(body, n_in, n_out, grid, ex):
    if ex is None:
        return body, [], [], [], [], []
    ni, no, ns = len(ex.ins), len(ex.out_shapes), len(ex.sems)

    def riding(*refs):
        r_in, r_out, sems = refs[n_in:n_in + ni], refs[n_in + ni + n_out:n_in + ni + n_out + no], refs[-ns:]
        first = last = None
        for axis, size in enumerate(grid):
            at_first, at_last = pl.program_id(axis) == 0, pl.program_id(axis) == size - 1
            first = at_first if first is None else jnp.logical_and(first, at_first)
            last = at_last if last is None else jnp.logical_and(last, at_last)

        @pl.when(first)
        def _():
            ex.start(r_in, r_out, *sems)

        body(*refs[:n_in], *refs[n_in + ni:n_in + ni + n_out], *refs[n_in + ni + n_out + no:-ns])

        @pl.when(last)
        def _():
            ex.finish(r_in, r_out, *sems)

    return riding, ex.ins, [HBM_SPEC] * ni, ex.out_shapes, [HBM_SPEC] * no, ex.scratch()


def gather_exchange(shards):
    nw = len(shards)
    entry = [k for _, k in shards]

    def rows(ref, cc):
        return pl.ds(cc * (ref.shape[1] // 2), ref.shape[1] // 2)

    def sent(ins, outs, send_sems, recv_sems, w, j):
        x, y, c = _place()
        return _remote(ins[w].at[pl.ds(entry[w], 1), rows(ins[w], c)], outs[w].at[:, 2 * x + y, rows(ins[w], c)], send_sems, recv_sems,
                       7 * w + j, (*_other_chips(x, y)[j], c))

    def landed(ins, outs, send_sems, recv_sems, w, j, cc, to):
        x, y, c = _place()
        chip = _other_chips(x, y)[j]
        blk = outs[w].at[:, 2 * chip[0] + chip[1], rows(ins[w], cc)]
        return _remote(blk, blk, send_sems, recv_sems, 7 * w + (j if to is None else 3 + j), (x, y, c) if to is None else to)

    def own(ins, outs, send_sems, recv_sems, w):
        x, y, c = _place()
        return _remote(ins[w].at[pl.ds(entry[w], 1)], outs[w].at[:, 2 * x + y], send_sems, recv_sems, 7 * w + 6, (x, y, 1 - c))

    def start(ins, outs, send_sems, recv_sems):
        for j in range(3):
            for w in range(nw):
                sent(ins, outs, send_sems, recv_sems, w, j).start()
        for w in range(nw):
            own(ins, outs, send_sems, recv_sems, w).start()

    def finish(ins, outs, send_sems, recv_sems):
        x, y, c = _place()
        for j in range(3):
            for w in range(nw):
                landed(ins, outs, send_sems, recv_sems, w, j, c, None).wait_recv()
                landed(ins, outs, send_sems, recv_sems, w, j, c, (x, y, 1 - c)).start()
        for w in range(nw):
            own(ins, outs, send_sems, recv_sems, w).wait()
        for j in range(3):
            for w in range(nw):
                landed(ins, outs, send_sems, recv_sems, w, j, 1 - c, (x, y, c)).wait_recv()
        for j in range(3):
            for w in range(nw):
                sent(ins, outs, send_sems, recv_sems, w, j).wait_send()
                landed(ins, outs, send_sems, recv_sems, w, j, c, (x, y, 1 - c)).wait_send()

    return Exchange([s for s, _ in shards], [_sds((1, 4) + s.shape[1:], s.dtype) for s, _ in shards], 7 * nw, start, finish)


def copies_exchange(ins, out_shapes, n_sems, copies):
    def start(*refs):
        for cp in copies(*refs):
            cp.start()

    def finish(*refs):
        for cp in copies(*refs):
            cp.wait()

    return Exchange(ins, out_shapes, n_sems, start, finish)


def sibling_halves_exchange(gs):
    def copies(ins, outs, send_sems, recv_sems):
        x, y, c = _place()
        return [_remote(g.at[:, :, pl.ds((1 - c) * (g.shape[2] // 2), g.shape[2] // 2)], o, send_sems, recv_sems, w, (x, y, 1 - c))
                for w, (g, o) in enumerate(zip(ins, outs))]

    return copies_exchange(gs, [_sds(g.shape[:2] + (g.shape[2] // 2, g.shape[3]), g.dtype) for g in gs], len(gs), copies)


def chips_exchange(sbs):
    def copies(ins, outs, send_sems, recv_sems):
        x, y, c = _place()
        return [_remote(s.at[:, 2 * chip[0] + chip[1]], o.at[j], send_sems, recv_sems, 3 * w + j, (*chip, c))
                for j, chip in enumerate(_other_chips(x, y)) for w, (s, o) in enumerate(zip(ins, outs))]

    return copies_exchange(sbs, [_sds((3, s.shape[0]) + s.shape[2:], s.dtype) for s in sbs], 3 * len(sbs), copies)


def sibling_exchange(fs):
    def copies(ins, outs, send_sems, recv_sems):
        x, y, c = _place()
        return [_remote(f, o, send_sems, recv_sems, w, (x, y, 1 - c)) for w, (f, o) in enumerate(zip(ins, outs))]

    return copies_exchange(fs, [_sds(f.shape, f.dtype) for f in fs], len(fs), copies)


def _half_tile(a):
    return _tile(a, (256, 352, 224, 176, 160, 128, 64, 32, 16))


def chip_partial_sums(g, r1, c_arr, name):
    nl, _, a2, b = r1.shape
    ta = _half_tile(a2)
    per = a2 // ta

    def body(c_ref, g_ref, r_ref, o_ref):
        o_ref[...] = (g_ref[...] + r_ref[...]).astype(o_ref.dtype)

    blk = (None, None, ta, b)
    return pl.pallas_call(
        body, name=name,
        grid_spec=pltpu.PrefetchScalarGridSpec(
            num_scalar_prefetch=1, grid=(nl, 4, per),
            in_specs=[pl.BlockSpec(blk, lambda l, s, i, c: (l, s, c[0] * per + i, 0)), pl.BlockSpec(blk, lambda l, s, i, c: (l, s, i, 0))],
            out_specs=pl.BlockSpec(blk, lambda l, s, i, c: (l, s, i, 0))),
        out_shape=_sds(r1.shape, BF16), compiler_params=_cp("parallel", "parallel", "parallel"))(c_arr, g, r1)


def shard_total(g, r1, r2, cs_arr, name):
    nl, _, a2, b = r1.shape
    ta = _half_tile(a2)
    per = a2 // ta

    def body(cs_ref, g_ref, r1_ref, p0_ref, p1_ref, p2_ref, o_ref):
        o_ref[...] = (((g_ref[...] + r1_ref[...]) + p0_ref[...].astype(F32)) + p1_ref[...].astype(F32)) + p2_ref[...].astype(F32)

    blk4, blk3 = (None, None, ta, b), (None, ta, b)
    peer = lambda k: pl.BlockSpec((None, None, ta, b), lambda l, i, cs: (k, l, i, 0))
    return pl.pallas_call(
        body, name=name,
        grid_spec=pltpu.PrefetchScalarGridSpec(
            num_scalar_prefetch=1, grid=(nl, per),
            in_specs=[pl.BlockSpec(blk4, lambda l, i, cs: (l, cs[1], cs[0] * per + i, 0)),
                      pl.BlockSpec(blk4, lambda l, i, cs: (l, cs[1], i, 0)), peer(0), peer(1), peer(2)],
            out_specs=pl.BlockSpec(blk3, lambda l, i, cs: (l, i, 0))),
        out_shape=_sds((nl, a2, b), F32), compiler_params=_cp("parallel", "parallel"))(cs_arr, g, r1, r2, r2, r2)


def allgather_small(v, name):
    r, n = v.shape

    def body(x_ref, out_ref, send_sems, recv_sems, local_sem):
        x, y, c = _place()
        me, sibling = (x, y, c), (x, y, 1 - c)
        chips = _other_chips(x, y)

        def rows(px, py, pc):
            return out_ref.at[pl.ds((4 * px + 2 * py + pc) * r, r), :]

        def copy(k, block, to, src=None):
            return _remote(rows(*block) if src is None else src, rows(*block), send_sems, recv_sems, k, to)

        mine = pltpu.make_async_copy(x_ref, rows(*me), local_sem)
        mine.start()
        first = [copy(0, me, sibling, src=x_ref)] + [copy(1 + j, me, (*chip, c), src=x_ref) for j, chip in enumerate(chips)]
        for cp in first:
            cp.start()
        passed = [copy(4 + j, (*chip, c), sibling) for j, chip in enumerate(chips)]
        for j, chip in enumerate(chips):
            copy(1 + j, (*chip, c), me).wait_recv()
            passed[j].start()
        copy(0, sibling, me).wait_recv()
        for j, chip in enumerate(chips):
            copy(4 + j, (*chip, 1 - c), me).wait_recv()
        for cp in first + passed:
            cp.wait_send()
        mine.wait()

    return pl.pallas_call(
        body, name=name, in_specs=[VMEM_SPEC], out_specs=VMEM_SPEC, out_shape=_sds((8 * r, n), v.dtype),
        scratch_shapes=[pltpu.SemaphoreType.DMA((7,)), pltpu.SemaphoreType.DMA((7,)), pltpu.SemaphoreType.DMA],
        compiler_params=pltpu.CompilerParams(has_side_effects=True, vmem_limit_bytes=V7X_VMEM_LIMIT_BYTES))(v)


def sum_devices(v8, name):
    _, r, n = v8.shape
    tr = _tile(r, (88, 64, 32, 16, 8))

    def body(v_ref, o_ref):
        acc = v_ref[0]
        for d in range(1, 8):
            acc = acc + v_ref[d]
        o_ref[...] = acc

    return pl.pallas_call(
        body, name=name, grid=(r // tr,), in_specs=[pl.BlockSpec((8, tr, n), lambda i: (0, i, 0))],
        out_specs=pl.BlockSpec((tr, n), lambda i: (i, 0)), out_shape=_sds((r, n), F32), compiler_params=_cp("parallel"))(v8)


SHARDED = (("a_w_in", 2), ("a_w_out", 1), ("b_w_in", 2), ("b_w_out", 1), ("w_mem_kv", 1), ("w_gate_up", 2), ("w_down", 1))


def _usable(wg, axis):
    l, _, a, b = wg.shape
    return wg.reshape(l, 4 * a, b) if axis == 1 else wg


def _pack(arrs):
    parts = []
    for a in arrs:
        flat = a.reshape(-1)
        flat = jnp.pad(flat, (0, -flat.shape[0] % 1024))
        parts.append(flat.reshape(-1, 128))
    return jnp.concatenate(parts, axis=0)


def _unpack(buf, like):
    out, row = [], 0
    for a in like:
        size = 1
        for s in a.shape:
            size *= s
        rows = -(-size // 1024) * 8
        out.append(buf[row:row + rows].reshape(-1)[:size].reshape(a.shape))
        row += rows
    return out


def kernel(x, mem, mem_norm_g, mix_norm_g, ffn_norm_g, final_norm_g, a_w_in, a_sinks, a_w_out, b_w_in, b_w_s, b_bias_s, b_ln_g, b_ln_b, b_w_out, w_mem_kv, w_gate_up, w_down, loss_target, m_mem_norm_g, m_mix_norm_g, m_ffn_norm_g, m_final_norm_g, m_a_w_in, m_a_sinks, m_a_w_out, m_b_w_in, m_b_w_s, m_b_bias_s, m_b_ln_g, m_b_ln_b, m_b_w_out, m_w_mem_kv, m_w_gate_up, m_w_down, v_mem_norm_g, v_mix_norm_g, v_ffn_norm_g, v_final_norm_g, v_a_w_in, v_a_sinks, v_a_w_out, v_b_w_in, v_b_w_s, v_b_bias_s, v_b_ln_g, v_b_ln_b, v_b_w_out, v_w_mem_kv, v_w_gate_up, v_w_down):
    given = dict(locals())
    depth = mix_norm_g.shape[0]
    d = x.shape[-1]
    xi, yi, ci = _place()
    c_arr = jnp.stack([ci]).astype(jnp.int32)
    cs_arr = jnp.stack([ci, 2 * xi + yi]).astype(jnp.int32)

    axis_of = dict(SHARDED)
    own = {n: given[n].astype(BF16) for n, _ in SHARDED}
    MIXER, FFN = slice(0, 3), slice(3, 5)

    def layer_weights(l):
        mix = "a" if l % 2 == 0 else "b"
        return [(mix + "_w_in", l // 2), (mix + "_w_out", l // 2), ("w_mem_kv", l), ("w_gate_up", l), ("w_down", l)]

    def gather_of(l, part=slice(0, 5)):
        return gather_exchange([(own[n], k) for n, k in layer_weights(l)[part]])

    def usable(l, gathered, part=slice(0, 5)):
        return {n[2:] if n[0] in "ab" else n: (_usable(wg, axis_of[n]), 0) for (n, _), wg in zip(layer_weights(l)[part], gathered)}

    weights = {0: usable(0, run_exchange(gather_of(0, MIXER), "gather_weights"), MIXER)}

    h = x.reshape(-1, d)
    tgt = loss_target.reshape(-1, d)
    mem2 = mem.reshape(-1, d)
    row = lambda v: v.reshape(1, -1)

    mem_n = rmsnorm_fwd(mem2, row(mem_norm_g), "mem_norm")
    saved = []
    for i in range(depth):
        j = i // 2
        wl = weights[i]
        w_in, w_out = wl["w_in"], wl["w_out"]
        kv = matmul(mem_n, wl["w_mem_kv"], "nn", BF16, "mem_kv")
        if i % 2 == 0:
            sk = jnp.pad(jnp.broadcast_to(a_sinks[j][:, None], (Q_HEADS, 128)), ((0, 16 - Q_HEADS), (0, 0)))
            xn, proj = norm_matmul(h, row(mix_norm_g[i]), w_in, BF16, "a_in")
            cat, *gathered = mixer_a_fwd(proj, sk, kv, "mixer_a", rider=gather_of(0, FFN) if i == 0 else None)
            if gathered:
                wl.update(usable(0, gathered, FFN))
            extra = (sk,)
        else:
            bt = jnp.pad(b_bias_s[j].T, ((0, 0), (0, 128 - B_GROUPS)))
            lg = jnp.pad(b_ln_g[j], ((0, 8 - B_GROUPS), (0, 0)))
            lb = jnp.pad(b_ln_b[j], ((0, 8 - B_GROUPS), (0, 0)))
            xn, proj = norm_matmul(h, row(mix_norm_g[i]), w_in, BF16, "b_in")
            cat = mixer_b_fwd(proj, b_w_s[j], bt, lg, lb, kv, "mixer_b")
            extra = (b_w_s[j], bt, lg, lb)
        h_mid = matmul(cat, w_out, "nn", F32, "mix_out", res=h)
        more = i + 1 < depth
        hn, gu, act, *gathered = gate_up_fwd(h_mid, row(ffn_norm_g[i]), *wl["w_gate_up"], "gate_up",
                                             rider=gather_of(i + 1, FFN) if more else None)
        h_out, *gathered_mixer = matmul(act, wl["w_down"], "nn", F32, "down", res=h_mid, rider=gather_of(i + 1, MIXER)) if more \
            else [matmul(act, wl["w_down"], "nn", F32, "down", res=h_mid)]
        if more:
            weights[i + 1] = {**usable(i + 1, gathered, FFN), **usable(i + 1, gathered_mixer, MIXER)}
        saved.append((h, xn, proj, cat, h_mid, hn, gu, act, kv, extra))
        h = h_out

    loss_part, dh, d_final_g = loss_head(h, row(final_norm_g), tgt, "loss_head")
    loss = lax.psum(loss_part[0, 0], ("x", "y", "c"))

    d_mix_g, d_ffn_g = [None] * depth, [None] * depth
    d_sinks, d_ws, d_bias, d_lg, d_lb = [], [], [], [], []
    d_mem_n = jnp.zeros(mem2.shape, F32)
    totals = [None] * depth
    pending = None
    for i in reversed(range(depth)):
        h_in, xn, proj, cat, h_mid, hn, gu, act, kv, extra = saved[i]
        wl = weights[i]
        dgu, *from_sibling = down_dx_swiglu_bwd(dh, wl["w_down"], gu, "down_dx",
                                                rider=sibling_halves_exchange(pending) if pending else None)
        if pending:
            partial = [chip_partial_sums(g, r1, c_arr, "grads_chip_sum") for g, r1 in zip(pending, from_sibling)]
        dw_down = matmul(act, dh, "tn", F32, "down_dw", tm=1408, out_planes=("rows", 4))
        dw_gate_up = matmul((hn, 0), dgu, "tn", F32, "gate_up_dw", tn=1408, tk=2048, out_planes=("cols", 4))
        ffn = [dw_gate_up[None], dw_down[None]] if i == 0 else []
        joined = lambda exs: None if not exs else exs[0] if len(exs) == 1 else both_exchanges(*exs)
        dh, d_ffn_g[i], *landed = dx_norm_bwd(
            dgu, wl["w_gate_up"], h_mid, row(ffn_norm_g[i]), dh, "gate_up_dx",
            rider=joined(([chips_exchange(partial[FFN])] if pending else []) + ([sibling_halves_exchange(ffn)] if ffn else [])))
        if pending:
            chips_ffn = landed[:2]
        if ffn:
            ffn_sibling = landed[-len(ffn):]
            ffn_partial = [chip_partial_sums(g, r1, c_arr, "grads_chip_sum") for g, r1 in zip(ffn, ffn_sibling)]
        dcat = matmul(dh, wl["w_out"], "nt", F32, "mix_out_dx")
        dw_out = matmul(cat, dh, "tn", F32, "mix_out_dw", out_planes=("rows", 4))
        mixer_rider = joined(([chips_exchange(partial[MIXER])] if pending else []) + ([chips_exchange(ffn_partial)] if ffn else []))
        if i % 2 == 0:
            dproj, dsk, dkv, *landed = mixer_a_bwd(proj, dcat, extra[0], kv, "mixer_a_bwd", rider=mixer_rider)
            d_sinks.insert(0, dsk[:Q_HEADS, 0])
            dw_in = matmul(dproj, xn, "tn", F32, "a_in_dw", out_planes=("rows", 4))
        else:
            dproj, dws, dbt, dlg, dlb, dkv, *landed = mixer_b_bwd(proj, dcat, *extra, kv, "mixer_b_bwd", rider=mixer_rider)
            d_ws.insert(0, dws)
            d_bias.insert(0, dbt[:, :B_GROUPS].T)
            d_lg.insert(0, dlg[:B_GROUPS])
            d_lb.insert(0, dlb[:B_GROUPS])
            dw_in = matmul(dproj, xn, "tn", F32, "b_in_dw", out_planes=("rows", 4))
        if pending:
            from_chips = landed[:3] + chips_ffn
            totals[i + 1] = [shard_total(g, r1, r2, cs_arr, "grads_shard_total") for g, r1, r2 in zip(pending, from_sibling, from_chips)]
        if ffn:
            ffn_totals = [shard_total(g, r1, r2, cs_arr, "grads_shard_total") for g, r1, r2 in zip(ffn, ffn_sibling, landed[-len(ffn):])]
        dw_kv = matmul(mem_n, dkv, "tn", F32, "mem_kv_dw", out_planes=("rows", 4))
        d_mem_n = matmul(dkv, wl["w_mem_kv"], "nt", F32, "mem_kv_dx", res=d_mem_n)
        dh, d_mix_g[i] = dx_norm_bwd(dproj, wl["w_in"], h_in, row(mix_norm_g[i]), dh, "in_dx")
        pending = [dw_in[None], dw_out[None], dw_kv[None]] + ([] if ffn else [dw_gate_up[None], dw_down[None]])
    grad_x = dh.reshape(x.shape)
    _, d_mem_g = rmsnorm_bwd(mem2, row(mem_norm_g), d_mem_n, jnp.zeros(mem2.shape, F32), "mem_norm_bwd")

    from_sibling = run_exchange(sibling_halves_exchange(pending), "grads_sibling_swap")
    partial = [chip_partial_sums(g, r1, c_arr, "grads_chip_sum") for g, r1 in zip(pending, from_sibling)]
    from_chips = run_exchange(chips_exchange(partial), "grads_chips_exchange")
    totals[0] = [shard_total(g, r1, r2, cs_arr, "grads_shard_total") for g, r1, r2 in zip(pending, from_sibling, from_chips)] + ffn_totals

    mine = {n: [None] * given[n].shape[0] for n, _ in SHARDED}
    for l in range(depth):
        for (n, k), tot in zip(layer_weights(l), totals[l]):
            mine[n][k] = tot
    mine = [jnp.concatenate(mine[n], axis=0) for n, _ in SHARDED]
    theirs = run_exchange(sibling_exchange(mine), "grads_sibling_totals")
    out = {}
    for (n, _), g_mine, g_theirs in zip(SHARDED, mine, theirs):
        flip = (lambda a: jnp.swapaxes(a, 1, 2)) if n in ("a_w_in", "b_w_in") else (lambda a: a)
        shape = flip(given[n]).shape
        two_d = lambda a: a.reshape(-1, shape[-1])
        res = adamw_halves(two_d(flip(given[n])), two_d(g_mine), two_d(g_theirs), two_d(flip(given["m_" + n])),
                           two_d(flip(given["v_" + n])), c_arr, shape[1], "adamw")
        out[n] = tuple(flip(r.reshape(shape)) for r in res)

    small = ("mem_norm_g", "mix_norm_g", "ffn_norm_g", "final_norm_g", "a_sinks", "b_w_s", "b_bias_s", "b_ln_g", "b_ln_b")
    small_g = [d_mem_g[0], jnp.concatenate(d_mix_g, axis=0), jnp.concatenate(d_ffn_g, axis=0), d_final_g[0],
               jnp.stack(d_sinks), jnp.stack(d_ws), jnp.stack(d_bias), jnp.stack(d_lg), jnp.stack(d_lb)]
    packed = _pack(small_g)
    g_small = sum_devices(allgather_small(packed, "small_allgather").reshape(8, *packed.shape), "small_sum")
    like = [given[n] for n in small]
    delta_s, new_m_s, new_v_s = adamw(_pack(like), g_small, _pack([given["m_" + n] for n in small]),
                                      _pack([given["v_" + n] for n in small]), "adamw_small")
    for n, g, dl, nm_, nv_ in zip(small, _unpack(g_small, like), _unpack(delta_s, like), _unpack(new_m_s, like), _unpack(new_v_s, like)):
        out[n] = (g, dl, nm_, nv_)

    order = ("mem_norm_g", "mix_norm_g", "ffn_norm_g", "final_norm_g", "a_w_in", "a_sinks", "a_w_out", "b_w_in", "b_w_s",
             "b_bias_s", "b_ln_g", "b_ln_b", "b_w_out", "w_mem_kv", "w_gate_up", "w_down")
    return (loss, grad_x, *[out[n][0] for n in order], *[out[n][1] for n in order],
            *[out[n][2] for n in order], *[out[n][3] for n in order])
```

```python
import jax
import jax.numpy as jnp
from jax import lax
from jax.experimental import pallas as pl
from jax.experimental.pallas import tpu as pltpu

F32, BF16 = jnp.float32, jnp.bfloat16
EPS = 1e-6
HEAD_DIM = 64
Q_HEADS, KV_HEADS, GROUP = 12, 2, 6
WINDOW = 128
MEM_HEADS = 4
B_GROUPS = 6
Q_W, KV_W, MEM_W, B_W = 768, 128, 256, 768
SCALE = HEAD_DIM ** -0.5
NEG = -1e30
ADAM_LR, ADAM_B1, ADAM_B2, ADAM_EPS, ADAM_WD, ADAM_STEP = 0.001, 0.9, 0.999, 1e-08, 0.01, 10
V7X_VMEM_LIMIT_BYTES = 48 * 1024 * 1024
MESH = pl.DeviceIdType.MESH
HBM_SPEC = pl.BlockSpec(memory_space=pltpu.HBM)
VMEM_SPEC = pl.BlockSpec(memory_space=pltpu.VMEM)


def _cp(*sem):
    return pltpu.CompilerParams(dimension_semantics=sem or None, vmem_limit_bytes=V7X_VMEM_LIMIT_BYTES)


def _tile(n, cands):
    for t in cands:
        if n % t == 0:
            return t
    return n


def _sds(shape, dtype):
    return jax.ShapeDtypeStruct(tuple(shape), dtype)


def _dot(a, b, ca, cb):
    return lax.dot_general(a, b, (((ca,), (cb,)), ((), ())), preferred_element_type=F32)


def _rms(x, g):
    return x * lax.rsqrt(jnp.mean(x * x, axis=-1, keepdims=True) + EPS) * g


def rmsnorm_fwd(h, g, name):
    t, d = h.shape
    tm = _tile(t, (512, 256, 128))

    def body(h_ref, g_ref, o_ref):
        o_ref[...] = _rms(h_ref[...], g_ref[...]).astype(o_ref.dtype)

    return pl.pallas_call(
        body, name=name, grid=(t // tm,),
        in_specs=[pl.BlockSpec((tm, d), lambda i: (i, 0)), pl.BlockSpec((1, d), lambda i: (0, 0))],
        out_specs=pl.BlockSpec((tm, d), lambda i: (i, 0)),
        out_shape=_sds((t, d), BF16), compiler_params=_cp("parallel"))(h, g)


def rmsnorm_bwd(h, g, dxn, dres, name):
    t, d = h.shape
    tm = _tile(t, (512, 256, 128))

    def body(h_ref, g_ref, dxn_ref, dres_ref, dh_ref, dg_ref):
        _, vjp = jax.vjp(_rms, h_ref[...], g_ref[...])
        dh, dg = vjp(dxn_ref[...].astype(F32))
        dh_ref[...] = dres_ref[...] + dh

        @pl.when(pl.program_id(0) == 0)
        def _():
            dg_ref[...] = jnp.zeros_like(dg_ref)

        dg_ref[...] += dg

    row = pl.BlockSpec((tm, d), lambda i: (i, 0))
    vec = pl.BlockSpec((1, d), lambda i: (0, 0))
    return pl.pallas_call(
        body, name=name, grid=(t // tm,), in_specs=[row, vec, row, row], out_specs=[row, vec],
        out_shape=[_sds((t, d), F32), _sds((1, d), F32)], compiler_params=_cp("arbitrary"))(h, g, dxn, dres)


def loss_head(h, g, tgt, name):
    t, d = h.shape
    tm = _tile(t, (512, 256, 128))

    def body(h_ref, g_ref, t_ref, l_ref, dh_ref, dg_ref, dh16_ref):
        y, vjp = jax.vjp(_rms, h_ref[...], g_ref[...])
        err = y - t_ref[...]
        dh, dg = vjp(err * (1.0 / d))
        dh_ref[...] = dh
        dh16_ref[...] = dh.astype(BF16)
        part = 0.5 * jnp.sum(jnp.mean(err * err, axis=-1, keepdims=True), axis=0, keepdims=True)

        @pl.when(pl.program_id(0) == 0)
        def _():
            dg_ref[...] = jnp.zeros_like(dg_ref)
            l_ref[...] = jnp.zeros_like(l_ref)

        dg_ref[...] += dg
        l_ref[...] += part

    row = pl.BlockSpec((tm, d), lambda i: (i, 0))
    vec = pl.BlockSpec((1, d), lambda i: (0, 0))
    one = pl.BlockSpec((1, 1), lambda i: (0, 0))
    return pl.pallas_call(
        body, name=name, grid=(t // tm,), in_specs=[row, vec, row], out_specs=[one, row, vec, row],
        out_shape=[_sds((1, 1), F32), _sds((t, d), F32), _sds((1, d), F32), _sds((t, d), BF16)],
        compiler_params=_cp("arbitrary"))(h, g, tgt)


def _logical(op):
    arr, lead = op if isinstance(op, tuple) else (op, None)
    planes = arr.shape[-3] if arr.ndim - (lead is not None) == 3 else 1
    return arr, lead, arr.shape[-2], arr.shape[-1], planes


def _spec(op, rows_t, cols_t, row_of, col_of):
    arr, lead, _, cols, _ = _logical(op)
    per = cols // cols_t
    lead = () if lead is None else (lead,)
    if arr.ndim - len(lead) == 2:
        return pl.BlockSpec((None,) * len(lead) + (rows_t, cols_t), lambda *g: lead + (row_of(*g), col_of(*g)))
    return pl.BlockSpec((None,) * len(lead) + (None, rows_t, cols_t),
                        lambda *g: lead + (col_of(*g) // per, row_of(*g), col_of(*g) % per))


def _arr(op):
    return op[0] if isinstance(op, tuple) else op


def _resident_whole(w, d):
    wa, layer = w
    planes, per = wa.shape[-3], wa.shape[-1]

    def fill(w_ref, whole_ref):
        for s in range(planes):
            whole_ref[:, s * per:(s + 1) * per] = w_ref[s]

    return (pl.BlockSpec((None, planes, d, per), lambda i: (layer, 0, 0, 0), pipeline_mode=pl.Buffered(1)),
            pltpu.VMEM((d, planes * per), wa.dtype), fill)


def norm_matmul(h, g, w, out_dtype, name):
    t, d = h.shape
    w_spec, whole, fill = _resident_whole(w, d)
    n = whole.shape[1]
    tm = _tile(t, (512, 256, 128))

    def body(h_ref, g_ref, w_ref, xn_ref, o_ref, whole_ref):
        @pl.when(pl.program_id(0) == 0)
        def _():
            fill(w_ref, whole_ref)

        xn = _rms(h_ref[...], g_ref[...]).astype(BF16)
        xn_ref[...] = xn
        o_ref[...] = _dot(xn, whole_ref[...], 1, 0).astype(o_ref.dtype)

    return pl.pallas_call(
        body, name=name, grid=(t // tm,),
        in_specs=[pl.BlockSpec((tm, d), lambda i: (i, 0)), pl.BlockSpec((1, d), lambda i: (0, 0)), w_spec],
        out_specs=[pl.BlockSpec((tm, d), lambda i: (i, 0)), pl.BlockSpec((tm, n), lambda i: (i, 0))],
        out_shape=[_sds((t, d), BF16), _sds((t, n), out_dtype)], scratch_shapes=[whole],
        compiler_params=_cp("arbitrary"))(h, g, w[0])


def dx_norm_bwd(dy, w, h, g, dres, name, rider=None):
    t, d = h.shape
    dy_arr, dy_lead, _, kc, kp = _logical(dy)
    w_arr, w_lead, _, wc, wp = _logical(w)
    assert kc * kp == wc * wp and dy_lead is None, name
    chunk = min(kc, wc)
    tm = _tile(t, (512, 256, 128))

    def piece(ref, planes, cols, q):
        off = q * chunk % cols
        return ref[q * chunk // cols, :, off:off + chunk] if planes > 1 else ref[:, off:off + chunk]

    narrow = wp > 1 and wc % 128 != 0
    if narrow:
        assert kp == 1, name
        w_whole_spec, whole, fill = _resident_whole(w, d)

    def body(dy_ref, w_ref, h_ref, g_ref, dres_ref, dh_ref, dg_ref, dh16_ref, *whole_ref):
        if narrow:
            @pl.when(pl.program_id(0) == 0)
            def _():
                fill(w_ref, whole_ref[0])

            dxn = _dot(dy_ref[...].astype(BF16), whole_ref[0][...], 1, 1)
        else:
            dxn = None
            for q in range(kc * kp // chunk):
                p = _dot(piece(dy_ref, kp, kc, q).astype(BF16), piece(w_ref, wp, wc, q), 1, 1)
                dxn = p if dxn is None else dxn + p
        _, vjp = jax.vjp(_rms, h_ref[...], g_ref[...])
        dh, dg = vjp(dxn)
        dh = dres_ref[...] + dh
        dh_ref[...] = dh
        dh16_ref[...] = dh.astype(BF16)

        @pl.when(pl.program_id(0) == 0)
        def _():
            dg_ref[...] = jnp.zeros_like(dg_ref)

        dg_ref[...] += dg

    w_lead = () if w_lead is None else (w_lead,)
    w_block = ((wp,) if wp > 1 else ()) + (d, wc)
    w_spec = pl.BlockSpec((None,) * len(w_lead) + w_block, lambda i: w_lead + (0,) * len(w_block), pipeline_mode=pl.Buffered(1))
    if narrow:
        w_spec = w_whole_spec
    dy_spec = pl.BlockSpec((kp, tm, kc), lambda i: (0, i, 0)) if kp > 1 else pl.BlockSpec((tm, kc), lambda i: (i, 0))
    row = pl.BlockSpec((tm, d), lambda i: (i, 0))
    vec = pl.BlockSpec((1, d), lambda i: (0, 0))
    grid = (t // tm,)
    body, r_ops, r_in, r_shapes, r_out, r_scratch = with_rider(body, 5, 3, grid, rider)
    return pl.pallas_call(
        body, name=name, grid=grid, in_specs=[dy_spec, w_spec, row, vec, row] + r_in,
        out_specs=[row, vec, row] + r_out, out_shape=[_sds((t, d), F32), _sds((1, d), F32), _sds((t, d), BF16)] + r_shapes,
        scratch_shapes=([whole] if narrow else []) + r_scratch,
        compiler_params=_cp("arbitrary"))(dy_arr, w_arr, h, g, dres, *r_ops)


def matmul(a, b, mode, out_dtype, name, res=None, tm=None, tn=1792, tk=2816, out_planes=None, out_into=None, rider=None):
    _, _, ar, ac, ap = _logical(a)
    _, _, br, bc, bp = _logical(b)
    if mode == "nn":
        m, ka, kb, n = ar, ac * ap, br, bc * bp
        n_plane, ka_plane, kb_plane = bc, ac, br
    elif mode == "nt":
        m, ka, n, kb = ar, ac * ap, br, bc * bp
        n_plane, ka_plane, kb_plane = br, ac, bc
    else:
        ka, m, kb, n = ar, ac * ap, br, bc * bp
        n_plane, ka_plane, kb_plane = bc, ar, br
    m_plane = ac if mode == "tn" else ar
    assert ka == kb, name
    k = ka
    kind, planes = out_planes or ("cols", 1)
    narrow = kind == "cols" and (n // planes) % 128 != 0
    if kind == "cols" and not narrow:
        n_plane = min(n_plane, n // planes)
    if narrow:
        tn = n
    tm = _tile(m_plane, ((1024, 1408, 512, 256, 128) if mode == "tn" else (512, 256, 128)) if tm is None else (tm, 1024, 512, 256, 128))
    if kind == "rows" and tm % (m // planes):
        tm = m_plane
    tn = _tile(n_plane, (tn, 1792, 1408, 1280, 1024, 896, 640, 512, 256, 128))
    tk = _tile(min(ka_plane, kb_plane), (tk, 2816, 1792, 1408, 1280, 1024, 512, 256, 128))
    nk = k // tk
    row_i, col_j, red = (lambda i, j, kk: i), (lambda i, j, kk: j), (lambda i, j, kk: kk)
    if mode == "nn":
        a_spec, b_spec, ca, cb = _spec(a, tm, tk, row_i, red), _spec(b, tk, tn, red, col_j), 1, 0
    elif mode == "nt":
        a_spec, b_spec, ca, cb = _spec(a, tm, tk, row_i, red), _spec(b, tn, tk, col_j, red), 1, 1
    else:
        a_spec, b_spec, ca, cb = _spec(a, tk, tm, red, row_i), _spec(b, tk, tn, red, col_j), 0, 0
    lead = () if out_into is None else (out_into[1],)
    if planes == 1:
        o_shape, o_block = (m, n), (tm, tn)
        o_index = lambda i, j, kk: lead + (i, j)
    elif narrow:
        o_shape, o_block = (planes, m, n // planes), (planes, tm, n // planes)
        o_index = lambda i, j, kk: lead + (0, i, 0)
    elif kind == "cols":
        per = n // planes // tn
        o_shape, o_block = (planes, m, n // planes), (None, tm, tn)
        o_index = lambda i, j, kk: lead + (j // per, i, j % per)
    else:
        o_shape, o_block = (planes, m // planes, n), (tm // (m // planes), m // planes, tn)
        o_index = lambda i, j, kk: lead + (i, 0, j)
    o_spec = pl.BlockSpec((None,) * len(lead) + o_block, o_index)
    if out_into is not None:
        assert out_into[0].shape[1:] == o_shape and out_into[0].dtype == out_dtype, name
        o_shape = out_into[0].shape
    has_res = res is not None
    n_in = 2 + has_res + (out_into is not None)

    def put(o_ref, v):
        if narrow:
            for s in range(planes):
                o_ref[s] = v[:, s * (n // planes):(s + 1) * (n // planes)].astype(o_ref.dtype)
        else:
            o_ref[...] = v.astype(o_ref.dtype).reshape(o_ref.shape)

    def body(*refs):
        a_ref, b_ref = refs[:2]
        rest = refs[2:2 + has_res] + refs[n_in:]
        o_ref = rest[1] if has_res else rest[0]
        p = _dot(a_ref[...].astype(BF16), b_ref[...].astype(BF16), ca, cb)
        if nk == 1:
            if has_res:
                p = p + rest[0][...]
            put(o_ref, p)
        else:
            acc_ref = rest[-1]
            kk = pl.program_id(2)

            @pl.when(kk == 0)
            def _():
                acc_ref[...] = p

            @pl.when(kk > 0)
            def _():
                acc_ref[...] += p

            @pl.when(kk == nk - 1)
            def _():
                r = acc_ref[...]
                if has_res:
                    r = r + rest[0][...]
                put(o_ref, r)

    operands = [_arr(a), _arr(b)] + ([res] if has_res else []) + ([out_into[0]] if out_into is not None else [])
    grid = (m // tm, n // tn, nk)
    body, r_ops, r_in, r_shapes, r_out, r_scratch = with_rider(body, n_in, 1, grid, rider)
    out = pl.pallas_call(
        body, name=name, grid=grid,
        in_specs=[a_spec, b_spec] + ([pl.BlockSpec((tm, tn), lambda i, j, kk: (i, j))] if has_res else [])
        + ([pl.BlockSpec(memory_space=pl.ANY)] if out_into is not None else []) + r_in,
        out_specs=[o_spec] + r_out, out_shape=[_sds(o_shape, out_dtype)] + r_shapes,
        input_output_aliases={n_in - 1: 0} if out_into is not None else {},
        scratch_shapes=([pltpu.VMEM((tm, tn), F32)] if nk > 1 else []) + r_scratch,
        compiler_params=_cp(*(("arbitrary",) * 3 if rider else ("parallel", "parallel", "arbitrary"))))(*operands, *r_ops)
    return out if rider else out[0]


def gate_up_fwd(h, g, w, layer, name, rider=None):
    t, d = h.shape
    half = w.shape[-1]
    tm = _tile(t, (512, 256, 128))

    def body(h_ref, g_ref, wg_ref, wu_ref, hn_ref, gu_ref, act_ref):
        a = _rms(h_ref[...], g_ref[...]).astype(BF16)
        hn_ref[...] = a
        gate, up = _dot(a, wg_ref[...], 1, 0), _dot(a, wu_ref[...], 1, 0)
        sig = 1.0 / (1.0 + jnp.exp(-gate))
        silu = gate * sig
        gu_ref[0] = (up * (sig + silu * (1.0 - sig))).astype(gu_ref.dtype)
        gu_ref[1] = silu.astype(gu_ref.dtype)
        act_ref[...] = (silu * up).astype(act_ref.dtype)

    grid = (2, t // tm)
    body, r_ops, r_in, r_shapes, r_out, r_scratch = with_rider(body, 4, 3, grid, rider)
    return pl.pallas_call(
        body, name=name, grid=grid,
        in_specs=[pl.BlockSpec((tm, d), lambda j, i: (i, 0)), pl.BlockSpec((1, d), lambda j, i: (0, 0)),
                  pl.BlockSpec((None, None, d, half), lambda j, i: (layer, j, 0, 0)),
                  pl.BlockSpec((None, None, d, half), lambda j, i: (layer, 2 + j, 0, 0))] + r_in,
        out_specs=[pl.BlockSpec((None, tm, d), lambda j, i: (j, i, 0)), pl.BlockSpec((2, tm, half), lambda j, i: (0, i, j)),
                   pl.BlockSpec((tm, half), lambda j, i: (i, j))] + r_out,
        out_shape=[_sds((2, t, d), BF16), _sds((2, t, 2 * half), BF16), _sds((t, 2 * half), BF16)] + r_shapes,
        scratch_shapes=r_scratch, compiler_params=_cp("arbitrary", "arbitrary"))(h, g, w, w, *r_ops)


def down_dx_swiglu_bwd(dh, wd, gu, name, rider=None):
    t, d = dh.shape
    w, layer = wd
    f = w.shape[-2]
    tm = _tile(t, (512, 256, 128))
    tn = _tile(f, (1408, 512, 256, 128))

    def body(dh_ref, w_ref, gu_ref, o_ref):
        dact = _dot(dh_ref[...].astype(BF16), w_ref[...], 1, 1)
        o_ref[0] = (dact * gu_ref[0].astype(F32)).astype(o_ref.dtype)
        o_ref[1] = (dact * gu_ref[1].astype(F32)).astype(o_ref.dtype)

    planes = pl.BlockSpec((2, tm, tn), lambda j, i: (0, i, j))
    grid = (f // tn, t // tm)
    body, r_ops, r_in, r_shapes, r_out, r_scratch = with_rider(body, 3, 1, grid, rider)
    return pl.pallas_call(
        body, name=name, grid=grid,
        in_specs=[pl.BlockSpec((tm, d), lambda j, i: (i, 0)), pl.BlockSpec((None, tn, d), lambda j, i: (layer, j, 0)), planes] + r_in,
        out_specs=[planes] + r_out, out_shape=[_sds((2, t, f), BF16)] + r_shapes, scratch_shapes=r_scratch,
        compiler_params=_cp("arbitrary", "arbitrary"))(dh, w, gu, *r_ops)


def _softmax_over_keys(s, sink=None):
    m = s.max(axis=0, keepdims=True)
    if sink is not None:
        m = jnp.maximum(m, sink)
    m = lax.stop_gradient(m)
    e = jnp.exp(s - m)
    den = e.sum(axis=0, keepdims=True)
    if sink is not None:
        den = den + jnp.exp(sink - m)
    return e * (1.0 / den)


def _low_lanes():
    return lax.broadcasted_iota(jnp.int32, (1, 128), 1) < HEAD_DIM


def _stack_heads(slabs):
    low = _low_lanes()
    return jnp.concatenate([p for s in slabs for p in (jnp.where(low, s, 0.0), jnp.where(low, 0.0, s))], axis=0)


def _unstack_heads(o, n_slabs):
    low = _low_lanes()
    return [jnp.where(low, o[2 * j * WINDOW:(2 * j + 1) * WINDOW], o[(2 * j + 1) * WINDOW:(2 * j + 2) * WINDOW])
            for j in range(n_slabs)]


def _swa_group(q_slabs, k_both, v_both, sinks, mask):
    qs = _stack_heads(q_slabs).astype(BF16)
    s = jnp.where(mask, _dot(k_both.astype(BF16), qs, 1, 1) * SCALE, NEG)
    sink = jnp.concatenate([jnp.broadcast_to(v, (1, WINDOW)) for v in sinks], axis=1)
    return _unstack_heads(_dot(_softmax_over_keys(s, sink).astype(BF16), v_both.astype(BF16), 0, 0), len(q_slabs))


def _mem_pair(q_slab, k_slab, v_slab):
    s = _dot(k_slab.astype(BF16), _stack_heads([q_slab]).astype(BF16), 1, 1) * SCALE
    return _unstack_heads(_dot(_softmax_over_keys(s).astype(BF16), v_slab.astype(BF16), 0, 0), 1)[0]


def _gelu(x):
    return 0.5 * x * (1.0 + jnp.tanh(0.7978845608028654 * (x + 0.044715 * (x * x * x))))


def _gmlp_group(zu, zv, w, bcol, lg, lb, tri):
    u, v = _gelu(zu), _gelu(zv)
    mu = jnp.mean(v, axis=-1, keepdims=True)
    var = jnp.mean(jnp.square(v - mu), axis=-1, keepdims=True)
    vn = (v - mu) * lax.rsqrt(var + EPS) * lg + lb
    sv = _dot(jnp.where(tri, w, 0.0).astype(BF16), vn.astype(BF16), 1, 0) + bcol
    return u * sv


def _cols(x, width):
    return [x[:, j * width:(j + 1) * width] for j in range(x.shape[1] // width)]


def _swa_mask(has_prev):
    qi = lax.broadcasted_iota(jnp.int32, (2 * WINDOW, GROUP * WINDOW), 1) & (WINDOW - 1)
    kj = lax.broadcasted_iota(jnp.int32, (2 * WINDOW, GROUP * WINDOW), 0)
    in_prev = jnp.logical_and(jnp.logical_and(kj < WINDOW, kj > qi), has_prev)
    return jnp.logical_or(in_prev, jnp.logical_and(kj >= WINDOW, kj - WINDOW <= qi))


def _mix_a(q_slabs, k_boths, v_boths, sinks, qm_slabs, km_slabs, vm_slabs, mask):
    per = GROUP // 2
    outs = []
    for g in range(KV_HEADS):
        outs += _swa_group(q_slabs[per * g:per * (g + 1)], k_boths[g], v_boths[g], sinks[GROUP * g:GROUP * (g + 1)], mask)
    return outs + [_mem_pair(qm_slabs[j], km_slabs[j], vm_slabs[j]) for j in range(MEM_HEADS // 2)]


def _in_both_halves(prev, cur):
    cat = jnp.concatenate([prev, cur], axis=0)
    rolled = pltpu.roll(cat, HEAD_DIM, axis=1)
    low = _low_lanes()
    return [jnp.where(low, cat, rolled), jnp.where(low, rolled, cat)]


def _from_both_halves(d_boths):
    t = [d + pltpu.roll(d, HEAD_DIM, axis=1) for d in d_boths]
    return jnp.where(_low_lanes(), t[0], t[1])


def _mix_a_specs(nm, blk):
    prev = lambda n: jnp.maximum(blk(n) - 1, 0)
    return [pl.BlockSpec((WINDOW, Q_W), lambda n: (blk(n), 0)),
            pl.BlockSpec((WINDOW, KV_W), lambda n: (prev(n), Q_W // KV_W)),
            pl.BlockSpec((WINDOW, KV_W), lambda n: (blk(n), Q_W // KV_W)),
            pl.BlockSpec((WINDOW, KV_W), lambda n: (prev(n), Q_W // KV_W + 1)),
            pl.BlockSpec((WINDOW, KV_W), lambda n: (blk(n), Q_W // KV_W + 1)),
            pl.BlockSpec((WINDOW, MEM_W), lambda n: (blk(n), (Q_W + 2 * KV_W) // MEM_W)),
            pl.BlockSpec((16, 128), lambda n: (0, 0)),
            pl.BlockSpec((nm, MEM_W), lambda n: (0, 0)),
            pl.BlockSpec((nm, MEM_W), lambda n: (0, 1))]


def _mix_a_args(refs):
    q, kp, kc, vp, vc, qm, sk, km, vm = [r[...].astype(F32) for r in refs]
    return (_cols(q, 128), _in_both_halves(kp, kc), _in_both_halves(vp, vc), [sk[h:h + 1, 0:1] for h in range(Q_HEADS)],
            _cols(qm, 128), _cols(km, 128), _cols(vm, 128))


def mixer_a_fwd(proj, sk, kv, name, rider=None):
    t, nm = proj.shape[0], kv.shape[0]

    def body(*refs):
        o_ref = refs[-1]
        slabs = _mix_a(*_mix_a_args(refs[:-1]), _swa_mask(pl.program_id(0) > 0))
        o_ref[...] = jnp.concatenate(slabs, axis=1).astype(o_ref.dtype)

    grid = (t // WINDOW,)
    body, r_ops, r_in, r_shapes, r_out, r_scratch = with_rider(body, 9, 1, grid, rider)
    return pl.pallas_call(
        body, name=name, grid=grid, in_specs=_mix_a_specs(nm, lambda n: n) + r_in,
        out_specs=[pl.BlockSpec((WINDOW, Q_W + MEM_W), lambda n: (n, 0))] + r_out,
        out_shape=[_sds((t, Q_W + MEM_W), BF16)] + r_shapes, scratch_shapes=r_scratch,
        compiler_params=_cp("arbitrary"))(proj, proj, proj, proj, proj, proj, sk, kv, kv, *r_ops)


def _onehot_rows(vals, shape):
    rows = lax.broadcasted_iota(jnp.int32, shape, 0)
    out = jnp.zeros(shape, F32)
    for h, v in enumerate(vals):
        out = out + jnp.where(rows == h, jnp.broadcast_to(v, shape), 0.0)
    return out


def mixer_a_bwd(proj, dcat, sk, kv, name, rider=None):
    t, nm = proj.shape[0], kv.shape[0]
    nb = t // WINDOW
    blk = lambda i: nb - 1 - i

    def body(*refs):
        dcat_ref, dproj_ref, dsk_ref, dkv_ref, carry_ref = refs[9:]
        i = pl.program_id(0)

        @pl.when(i == 0)
        def _():
            carry_ref[...] = jnp.zeros_like(carry_ref)
            dsk_ref[...] = jnp.zeros_like(dsk_ref)
            dkv_ref[...] = jnp.zeros_like(dkv_ref)

        mask = _swa_mask(blk(i) > 0)
        _, vjp = jax.vjp(lambda *a: _mix_a(*a, mask), *_mix_a_args(refs[:9]))
        dqs, dk_boths, dv_boths, dsinks, dqms, dkms, dvms = vjp(_cols(dcat_ref[...].astype(F32), 128))
        dkv = jnp.concatenate([_from_both_halves(dk_boths), _from_both_halves(dv_boths)], axis=1)
        dkv_cur = dkv[WINDOW:] + carry_ref[...]
        carry_ref[...] = dkv[:WINDOW]
        dproj_ref[...] = jnp.concatenate(dqs + [dkv_cur] + dqms, axis=1).astype(dproj_ref.dtype)
        dsk_ref[...] += _onehot_rows(dsinks, (16, 128))
        dkv_ref[...] += jnp.concatenate(dkms + dvms, axis=1)

    width = Q_W + 2 * KV_W + MEM_W
    body, r_ops, r_in, r_shapes, r_out, r_scratch = with_rider(body, 10, 3, (nb,), rider)
    return pl.pallas_call(
        body, name=name, grid=(nb,),
        in_specs=_mix_a_specs(nm, blk) + [pl.BlockSpec((WINDOW, Q_W + MEM_W), lambda i: (blk(i), 0))] + r_in,
        out_specs=[pl.BlockSpec((WINDOW, width), lambda i: (blk(i), 0)), pl.BlockSpec((16, 128), lambda i: (0, 0)),
                   pl.BlockSpec((nm, 2 * MEM_W), lambda i: (0, 0))] + r_out,
        out_shape=[_sds((t, width), BF16), _sds((16, 128), F32), _sds((nm, 2 * MEM_W), F32)] + r_shapes,
        scratch_shapes=[pltpu.VMEM((WINDOW, 2 * KV_W), F32)] + r_scratch,
        compiler_params=_cp("arbitrary"))(proj, proj, proj, proj, proj, proj, sk, kv, kv, dcat, *r_ops)


def _mix_b(zus, zvs, ws, bcols, lgs, lbs, qms, kms, vms, tri):
    outs = [_gmlp_group(zus[g], zvs[g], ws[g], bcols[g], lgs[g], lbs[g], tri) for g in range(B_GROUPS)]
    return outs + [_mem_pair(qms[j], kms[j], vms[j]) for j in range(MEM_HEADS // 2)]


def _mix_b_specs(nm):
    return [pl.BlockSpec((WINDOW, 2 * B_W), lambda n: (n, 0)),
            pl.BlockSpec((WINDOW, MEM_W), lambda n: (n, 2 * B_W // MEM_W)),
            pl.BlockSpec((B_GROUPS, WINDOW, WINDOW), lambda n: (0, 0, 0)),
            pl.BlockSpec((WINDOW, 128), lambda n: (0, 0)),
            pl.BlockSpec((8, 128), lambda n: (0, 0)),
            pl.BlockSpec((8, 128), lambda n: (0, 0)),
            pl.BlockSpec((nm, MEM_W), lambda n: (0, 0)),
            pl.BlockSpec((nm, MEM_W), lambda n: (0, 1))]


def _mix_b_args(refs):
    z, qm, ws, bt, lg, lb, km, vm = [r[...].astype(F32) for r in refs]
    zs = _cols(z, 128)
    return (zs[:B_GROUPS], zs[B_GROUPS:], [ws[g] for g in range(B_GROUPS)], [bt[:, g:g + 1] for g in range(B_GROUPS)],
            [lg[g:g + 1, :] for g in range(B_GROUPS)], [lb[g:g + 1, :] for g in range(B_GROUPS)],
            _cols(qm, 128), _cols(km, 128), _cols(vm, 128))


def _tri():
    return lax.broadcasted_iota(jnp.int32, (WINDOW, WINDOW), 0) >= lax.broadcasted_iota(jnp.int32, (WINDOW, WINDOW), 1)


def mixer_b_fwd(proj, ws, bt, lg, lb, kv, name):
    t, nm = proj.shape[0], kv.shape[0]

    def body(*refs):
        o_ref = refs[-1]
        o_ref[...] = jnp.concatenate(_mix_b(*_mix_b_args(refs[:-1]), _tri()), axis=1).astype(o_ref.dtype)

    return pl.pallas_call(
        body, name=name, grid=(t // WINDOW,), in_specs=_mix_b_specs(nm),
        out_specs=pl.BlockSpec((WINDOW, B_W + MEM_W), lambda n: (n, 0)),
        out_shape=_sds((t, B_W + MEM_W), BF16), compiler_params=_cp("parallel"))(proj, proj, ws, bt, lg, lb, kv, kv)


def mixer_b_bwd(proj, dcat, ws, bt, lg, lb, kv, name, rider=None):
    t, nm = proj.shape[0], kv.shape[0]

    def body(*refs):
        dcat_ref, dproj_ref, dws_ref, dbt_ref, dlg_ref, dlb_ref, dkv_ref = refs[8:]

        @pl.when(pl.program_id(0) == 0)
        def _():
            for r in (dws_ref, dbt_ref, dlg_ref, dlb_ref, dkv_ref):
                r[...] = jnp.zeros_like(r)

        tri = _tri()
        zus, zvs, ws, bcols, lgs, lbs, qms, kms, vms = _mix_b_args(refs[:8])
        douts = _cols(dcat_ref[...].astype(F32), 128)
        grads = []
        for g in range(B_GROUPS):
            _, vjp = jax.vjp(lambda *a: _gmlp_group(*a, tri), zus[g], zvs[g], ws[g], bcols[g], lgs[g], lbs[g])
            grads.append(vjp(douts[g]))
        dzus, dzvs, dws, dbcols, dlgs, dlbs = [list(t) for t in zip(*grads)]
        grads = []
        for j in range(MEM_HEADS // 2):
            _, vjp = jax.vjp(_mem_pair, qms[j], kms[j], vms[j])
            grads.append(vjp(douts[B_GROUPS + j]))
        dqms, dkms, dvms = [list(t) for t in zip(*grads)]
        dproj_ref[...] = jnp.concatenate(dzus + dzvs + dqms, axis=1).astype(dproj_ref.dtype)
        for g in range(B_GROUPS):
            dws_ref[g] += dws[g]
        lanes = lax.broadcasted_iota(jnp.int32, (WINDOW, 128), 1)
        dbt = jnp.zeros((WINDOW, 128), F32)
        for g in range(B_GROUPS):
            dbt = dbt + jnp.where(lanes == g, jnp.broadcast_to(dbcols[g], (WINDOW, 128)), 0.0)
        dbt_ref[...] += dbt
        dlg_ref[...] += _onehot_rows(dlgs, (8, 128))
        dlb_ref[...] += _onehot_rows(dlbs, (8, 128))
        dkv_ref[...] += jnp.concatenate(dkms + dvms, axis=1)

    width = 2 * B_W + MEM_W
    const2 = lambda n: (0, 0)
    grid = (t // WINDOW,)
    body, r_ops, r_in, r_shapes, r_out, r_scratch = with_rider(body, 9, 6, grid, rider)
    return pl.pallas_call(
        body, name=name, grid=grid,
        in_specs=_mix_b_specs(nm) + [pl.BlockSpec((WINDOW, B_W + MEM_W), lambda n: (n, 0))] + r_in,
        out_specs=[pl.BlockSpec((WINDOW, width), lambda n: (n, 0)),
                   pl.BlockSpec((B_GROUPS, WINDOW, WINDOW), lambda n: (0, 0, 0)),
                   pl.BlockSpec((WINDOW, 128), const2), pl.BlockSpec((8, 128), const2), pl.BlockSpec((8, 128), const2),
                   pl.BlockSpec((nm, 2 * MEM_W), const2)] + r_out,
        out_shape=[_sds((t, width), BF16), _sds((B_GROUPS, WINDOW, WINDOW), F32), _sds((WINDOW, 128), F32),
                   _sds((8, 128), F32), _sds((8, 128), F32), _sds((nm, 2 * MEM_W), F32)] + r_shapes,
        scratch_shapes=r_scratch, compiler_params=_cp("arbitrary"))(proj, proj, ws, bt, lg, lb, kv, kv, dcat, *r_ops)


def _adamw_update(w, g, m, v):
    m2 = ADAM_B1 * m + (1.0 - ADAM_B1) * g
    v2 = ADAM_B2 * v + (1.0 - ADAM_B2) * jnp.square(g)
    m_hat = m2 / (1.0 - ADAM_B1 ** ADAM_STEP)
    v_hat = v2 / (1.0 - ADAM_B2 ** ADAM_STEP)
    return -ADAM_LR * (m_hat / (jnp.sqrt(v_hat) + ADAM_EPS) + ADAM_WD * w), m2, v2


def adamw(w, g, m, v, name):
    r, c = w.shape
    tr = _tile(r, (512, 352, 256, 128, 64, 32, 16, 8))

    def body(w_ref, g_ref, m_ref, v_ref, d_ref, nm_ref, nv_ref):
        d_ref[...], nm_ref[...], nv_ref[...] = _adamw_update(w_ref[...], g_ref[...], m_ref[...], v_ref[...])

    spec = pl.BlockSpec((tr, c), lambda i: (i, 0))
    return pl.pallas_call(
        body, name=name, grid=(r // tr,), in_specs=[spec] * 4, out_specs=[spec] * 3,
        out_shape=[_sds((r, c), F32)] * 3, compiler_params=_cp("parallel"))(w, g, m, v)


def adamw_halves(w, g_mine, g_theirs, m, v, c_arr, rows, name):
    r, c = w.shape
    tr = _tile(rows // 2, (256, 352, 224, 160, 128, 64, 32, 16, 8))
    per_half = rows // 2 // tr

    def body(c_ref, w_ref, gm_ref, gt_ref, m_ref, v_ref, g_ref, d_ref, nm_ref, nv_ref):
        g = jnp.where(pl.program_id(0) // per_half % 2 == c_ref[0], gm_ref[...], gt_ref[...])
        g_ref[...] = g
        d_ref[...], nm_ref[...], nv_ref[...] = _adamw_update(w_ref[...], g, m_ref[...], v_ref[...])

    spec = pl.BlockSpec((tr, c), lambda i, cr: (i, 0))
    half = pl.BlockSpec((tr, c), lambda i, cr: (i // (2 * per_half) * per_half + i % per_half, 0))
    return pl.pallas_call(
        body, name=name,
        grid_spec=pltpu.PrefetchScalarGridSpec(num_scalar_prefetch=1, grid=(r // tr,), in_specs=[spec, half, half, spec, spec],
                                               out_specs=[spec] * 4),
        out_shape=[_sds((r, c), F32)] * 4, compiler_params=_cp("parallel"))(c_arr, w, g_mine, g_theirs, m, v)


def _place():
    return lax.axis_index("x"), lax.axis_index("y"), lax.axis_index("c")


def _other_chips(x, y):
    return [(1 - x, y), (x, 1 - y), (1 - x, 1 - y)]


def _remote(src, dst, send_sems, recv_sems, k, dev):
    return pltpu.make_async_remote_copy(src_ref=src, dst_ref=dst, send_sem=send_sems.at[k], recv_sem=recv_sems.at[k],
                                        device_id=dev, device_id_type=MESH)


class Exchange:
    def __init__(self, ins, out_shapes, n_sems, start, finish):
        self.ins, self.out_shapes, self.start, self.finish = list(ins), list(out_shapes), start, finish
        self.sems = [n_sems, n_sems] if isinstance(n_sems, int) else list(n_sems)

    def scratch(self):
        return [pltpu.SemaphoreType.DMA((n,)) for n in self.sems]


def both_exchanges(a, b):
    ni, no, ns = len(a.ins), len(a.out_shapes), len(a.sems)

    def start(ins, outs, *sems):
        a.start(ins[:ni], outs[:no], *sems[:ns])
        b.start(ins[ni:], outs[no:], *sems[ns:])

    def finish(ins, outs, *sems):
        a.finish(ins[:ni], outs[:no], *sems[:ns])
        b.finish(ins[ni:], outs[no:], *sems[ns:])

    return Exchange(a.ins + b.ins, a.out_shapes + b.out_shapes, a.sems + b.sems, start, finish)


def run_exchange(ex, name):
    ni, no = len(ex.ins), len(ex.out_shapes)

    def body(*refs):
        ex.start(refs[:ni], refs[ni:ni + no], *refs[ni + no:])
        ex.finish(refs[:ni], refs[ni:ni + no], *refs[ni + no:])

    return pl.pallas_call(
        body, name=name, in_specs=[HBM_SPEC] * ni, out_specs=[HBM_SPEC] * no, out_shape=ex.out_shapes, scratch_shapes=ex.scratch(),
        compiler_params=pltpu.CompilerParams(has_side_effects=True))(*ex.ins)


def with_rider(body, n_in, n_out, grid, ex):
    if ex is None:
        return body, [], [], [], [], []
    ni, no, ns = len(ex.ins), len(ex.out_shapes), len(ex.sems)

    def riding(*refs):
        r_in, r_out, sems = refs[n_in:n_in + ni], refs[n_in + ni + n_out:n_in + ni + n_out + no], refs[-ns:]
        first = last = None
        for axis, size in enumerate(grid):
            at_first, at_last = pl.program_id(axis) == 0, pl.program_id(axis) == size - 1
            first = at_first if first is None else jnp.logical_and(first, at_first)
            last = at_last if last is None else jnp.logical_and(last, at_last)

        @pl.when(first)
        def _():
            ex.start(r_in, r_out, *sems)

        body(*refs[:n_in], *refs[n_in + ni:n_in + ni + n_out], *refs[n_in + ni + n_out + no:-ns])

        @pl.when(last)
        def _():
            ex.finish(r_in, r_out, *sems)

    return riding, ex.ins, [HBM_SPEC] * ni, ex.out_shapes, [HBM_SPEC] * no, ex.scratch()


def gather_exchange(shards):
    nw = len(shards)
    entry = [k for _, k in shards]

    def rows(ref, cc):
        return pl.ds(cc * (ref.shape[1] // 2), ref.shape[1] // 2)

    def sent(ins, outs, send_sems, recv_sems, w, j):
        x, y, c = _place()
        return _remote(ins[w].at[pl.ds(entry[w], 1), rows(ins[w], c)], outs[w].at[:, 2 * x + y, rows(ins[w], c)], send_sems, recv_sems,
                       7 * w + j, (*_other_chips(x, y)[j], c))

    def landed(ins, outs, send_sems, recv_sems, w, j, cc, to):
        x, y, c = _place()
        chip = _other_chips(x, y)[j]
        blk = outs[w].at[:, 2 * chip[0] + chip[1], rows(ins[w], cc)]
        return _remote(blk, blk, send_sems, recv_sems, 7 * w + (j if to is None else 3 + j), (x, y, c) if to is None else to)

    def own(ins, outs, send_sems, recv_sems, w):
        x, y, c = _place()
        return _remote(ins[w].at[pl.ds(entry[w], 1)], outs[w].at[:, 2 * x + y], send_sems, recv_sems, 7 * w + 6, (x, y, 1 - c))

    def start(ins, outs, send_sems, recv_sems):
        for j in range(3):
            for w in range(nw):
                sent(ins, outs, send_sems, recv_sems, w, j).start()
        for w in range(nw):
            own(ins, outs, send_sems, recv_sems, w).start()

    def finish(ins, outs, send_sems, recv_sems):
        x, y, c = _place()
        for j in range(3):
            for w in range(nw):
                landed(ins, outs, send_sems, recv_sems, w, j, c, None).wait_recv()
                landed(ins, outs, send_sems, recv_sems, w, j, c, (x, y, 1 - c)).start()
        for w in range(nw):
            own(ins, outs, send_sems, recv_sems, w).wait()
        for j in range(3):
            for w in range(nw):
                landed(ins, outs, send_sems, recv_sems, w, j, 1 - c, (x, y, c)).wait_recv()
        for j in range(3):
            for w in range(nw):
                sent(ins, outs, send_sems, recv_sems, w, j).wait_send()
                landed(ins, outs, send_sems, recv_sems, w, j, c, (x, y, 1 - c)).wait_send()

    return Exchange([s for s, _ in shards], [_sds((1, 4) + s.shape[1:], s.dtype) for s, _ in shards], 7 * nw, start, finish)


def copies_exchange(ins, out_shapes, n_sems, copies):
    def start(*refs):
        for cp in copies(*refs):
            cp.start()

    def finish(*refs):
        for cp in copies(*refs):
            cp.wait()

    return Exchange(ins, out_shapes, n_sems, start, finish)


def sibling_halves_exchange(gs):
    def copies(ins, outs, send_sems, recv_sems):
        x, y, c = _place()
        return [_remote(g.at[:, :, pl.ds((1 - c) * (g.shape[2] // 2), g.shape[2] // 2)], o, send_sems, recv_sems, w, (x, y, 1 - c))
                for w, (g, o) in enumerate(zip(ins, outs))]

    return copies_exchange(gs, [_sds(g.shape[:2] + (g.shape[2] // 2, g.shape[3]), g.dtype) for g in gs], len(gs), copies)


def chips_exchange(sbs):
    def copies(ins, outs, send_sems, recv_sems):
        x, y, c = _place()
        return [_remote(s.at[:, 2 * chip[0] + chip[1]], o.at[j], send_sems, recv_sems, 3 * w + j, (*chip, c))
                for j, chip in enumerate(_other_chips(x, y)) for w, (s, o) in enumerate(zip(ins, outs))]

    return copies_exchange(sbs, [_sds((3, s.shape[0]) + s.shape[2:], s.dtype) for s in sbs], 3 * len(sbs), copies)


def sibling_exchange(fs):
    def copies(ins, outs, send_sems, recv_sems):
        x, y, c = _place()
        return [_remote(f, o, send_sems, recv_sems, w, (x, y, 1 - c)) for w, (f, o) in enumerate(zip(ins, outs))]

    return copies_exchange(fs, [_sds(f.shape, f.dtype) for f in fs], len(fs), copies)


def _half_tile(a):
    return _tile(a, (256, 352, 224, 176, 160, 128, 64, 32, 16))


def chip_partial_sums(g, r1, c_arr, name):
    nl, _, a2, b = r1.shape
    ta = _half_tile(a2)
    per = a2 // ta

    def body(c_ref, g_ref, r_ref, o_ref):
        o_ref[...] = (g_ref[...] + r_ref[...]).astype(o_ref.dtype)

    blk = (None, None, ta, b)
    return pl.pallas_call(
        body, name=name,
        grid_spec=pltpu.PrefetchScalarGridSpec(
            num_scalar_prefetch=1, grid=(nl, 4, per),
            in_specs=[pl.BlockSpec(blk, lambda l, s, i, c: (l, s, c[0] * per + i, 0)), pl.BlockSpec(blk, lambda l, s, i, c: (l, s, i, 0))],
            out_specs=pl.BlockSpec(blk, lambda l, s, i, c: (l, s, i, 0))),
        out_shape=_sds(r1.shape, BF16), compiler_params=_cp("parallel", "parallel", "parallel"))(c_arr, g, r1)


def shard_total(g, r1, r2, cs_arr, name):
    nl, _, a2, b = r1.shape
    ta = _half_tile(a2)
    per = a2 // ta

    def body(cs_ref, g_ref, r1_ref, p0_ref, p1_ref, p2_ref, o_ref):
        o_ref[...] = (((g_ref[...] + r1_ref[...]) + p0_ref[...].astype(F32)) + p1_ref[...].astype(F32)) + p2_ref[...].astype(F32)

    blk4, blk3 = (None, None, ta, b), (None, ta, b)
    peer = lambda k: pl.BlockSpec((None, None, ta, b), lambda l, i, cs: (k, l, i, 0))
    return pl.pallas_call(
        body, name=name,
        grid_spec=pltpu.PrefetchScalarGridSpec(
            num_scalar_prefetch=1, grid=(nl, per),
            in_specs=[pl.BlockSpec(blk4, lambda l, i, cs: (l, cs[1], cs[0] * per + i, 0)),
                      pl.BlockSpec(blk4, lambda l, i, cs: (l, cs[1], i, 0)), peer(0), peer(1), peer(2)],
            out_specs=pl.BlockSpec(blk3, lambda l, i, cs: (l, i, 0))),
        out_shape=_sds((nl, a2, b), F32), compiler_params=_cp("parallel", "parallel"))(cs_arr, g, r1, r2, r2, r2)


def allgather_small(v, name):
    r, n = v.shape

    def body(x_ref, out_ref, send_sems, recv_sems, local_sem):
        x, y, c = _place()
        me, sibling = (x, y, c), (x, y, 1 - c)
        chips = _other_chips(x, y)

        def rows(px, py, pc):
            return out_ref.at[pl.ds((4 * px + 2 * py + pc) * r, r), :]

        def copy(k, block, to, src=None):
            return _remote(rows(*block) if src is None else src, rows(*block), send_sems, recv_sems, k, to)

        mine = pltpu.make_async_copy(x_ref, rows(*me), local_sem)
        mine.start()
        first = [copy(0, me, sibling, src=x_ref)] + [copy(1 + j, me, (*chip, c), src=x_ref) for j, chip in enumerate(chips)]
        for cp in first:
            cp.start()
        passed = [copy(4 + j, (*chip, c), sibling) for j, chip in enumerate(chips)]
        for j, chip in enumerate(chips):
            copy(1 + j, (*chip, c), me).wait_recv()
            passed[j].start()
        copy(0, sibling, me).wait_recv()
        for j, chip in enumerate(chips):
            copy(4 + j, (*chip, 1 - c), me).wait_recv()
        for cp in first + passed:
            cp.wait_send()
        mine.wait()

    return pl.pallas_call(
        body, name=name, in_specs=[VMEM_SPEC], out_specs=VMEM_SPEC, out_shape=_sds((8 * r, n), v.dtype),
        scratch_shapes=[pltpu.SemaphoreType.DMA((7,)), pltpu.SemaphoreType.DMA((7,)), pltpu.SemaphoreType.DMA],
        compiler_params=pltpu.CompilerParams(has_side_effects=True, vmem_limit_bytes=V7X_VMEM_LIMIT_BYTES))(v)


def sum_devices(v8, name):
    _, r, n = v8.shape
    tr = _tile(r, (88, 64, 32, 16, 8))

    def body(v_ref, o_ref):
        acc = v_ref[0]
        for d in range(1, 8):
            acc = acc + v_ref[d]
        o_ref[...] = acc

    return pl.pallas_call(
        body, name=name, grid=(r // tr,), in_specs=[pl.BlockSpec((8, tr, n), lambda i: (0, i, 0))],
        out_specs=pl.BlockSpec((tr, n), lambda i: (i, 0)), out_shape=_sds((r, n), F32), compiler_params=_cp("parallel"))(v8)


SHARDED = (("a_w_in", 2), ("a_w_out", 1), ("b_w_in", 2), ("b_w_out", 1), ("w_mem_kv", 1), ("w_gate_up", 2), ("w_down", 1))


def _usable(wg, axis):
    l, _, a, b = wg.shape
    return wg.reshape(l, 4 * a, b) if axis == 1 else wg


def _pack(arrs):
    parts = []
    for a in arrs:
        flat = a.reshape(-1)
        flat = jnp.pad(flat, (0, -flat.shape[0] % 1024))
        parts.append(flat.reshape(-1, 128))
    return jnp.concatenate(parts, axis=0)


def _unpack(buf, like):
    out, row = [], 0
    for a in like:
        size = 1
        for s in a.shape:
            size *= s
        rows = -(-size // 1024) * 8
        out.append(buf[row:row + rows].reshape(-1)[:size].reshape(a.shape))
        row += rows
    return out


def kernel(x, mem, mem_norm_g, mix_norm_g, ffn_norm_g, final_norm_g, a_w_in, a_sinks, a_w_out, b_w_in, b_w_s, b_bias_s, b_ln_g, b_ln_b, b_w_out, w_mem_kv, w_gate_up, w_down, loss_target, m_mem_norm_g, m_mix_norm_g, m_ffn_norm_g, m_final_norm_g, m_a_w_in, m_a_sinks, m_a_w_out, m_b_w_in, m_b_w_s, m_b_bias_s, m_b_ln_g, m_b_ln_b, m_b_w_out, m_w_mem_kv, m_w_gate_up, m_w_down, v_mem_norm_g, v_mix_norm_g, v_ffn_norm_g, v_final_norm_g, v_a_w_in, v_a_sinks, v_a_w_out, v_b_w_in, v_b_w_s, v_b_bias_s, v_b_ln_g, v_b_ln_b, v_b_w_out, v_w_mem_kv, v_w_gate_up, v_w_down):
    given = dict(locals())
    depth = mix_norm_g.shape[0]
    d = x.shape[-1]
    xi, yi, ci = _place()
    c_arr = jnp.stack([ci]).astype(jnp.int32)
    cs_arr = jnp.stack([ci, 2 * xi + yi]).astype(jnp.int32)

    axis_of = dict(SHARDED)
    own = {n: given[n].astype(BF16) for n, _ in SHARDED}
    MIXER, FFN = slice(0, 3), slice(3, 5)

    def layer_weights(l):
        mix = "a" if l % 2 == 0 else "b"
        return [(mix + "_w_in", l // 2), (mix + "_w_out", l // 2), ("w_mem_kv", l), ("w_gate_up", l), ("w_down", l)]

    def gather_of(l, part=slice(0, 5)):
        return gather_exchange([(own[n], k) for n, k in layer_weights(l)[part]])

    def usable(l, gathered, part=slice(0, 5)):
        return {n[2:] if n[0] in "ab" else n: (_usable(wg, axis_of[n]), 0) for (n, _), wg in zip(layer_weights(l)[part], gathered)}

    weights = {0: usable(0, run_exchange(gather_of(0, MIXER), "gather_weights"), MIXER)}

    h = x.reshape(-1, d)
    tgt = loss_target.reshape(-1, d)
    mem2 = mem.reshape(-1, d)
    row = lambda v: v.reshape(1, -1)

    mem_n = rmsnorm_fwd(mem2, row(mem_norm_g), "mem_norm")
    saved = []
    for i in range(depth):
        j = i // 2
        wl = weights[i]
        w_in, w_out = wl["w_in"], wl["w_out"]
        kv = matmul(mem_n, wl["w_mem_kv"], "nn", BF16, "mem_kv")
        if i % 2 == 0:
            sk = jnp.pad(jnp.broadcast_to(a_sinks[j][:, None], (Q_HEADS, 128)), ((0, 16 - Q_HEADS), (0, 0)))
            xn, proj = norm_matmul(h, row(mix_norm_g[i]), w_in, BF16, "a_in")
            cat, *gathered = mixer_a_fwd(proj, sk, kv, "mixer_a", rider=gather_of(0, FFN) if i == 0 else None)
            if gathered:
                wl.update(usable(0, gathered, FFN))
            extra = (sk,)
        else:
            bt = jnp.pad(b_bias_s[j].T, ((0, 0), (0, 128 - B_GROUPS)))
            lg = jnp.pad(b_ln_g[j], ((0, 8 - B_GROUPS), (0, 0)))
            lb = jnp.pad(b_ln_b[j], ((0, 8 - B_GROUPS), (0, 0)))
            xn, proj = norm_matmul(h, row(mix_norm_g[i]), w_in, BF16, "b_in")
            cat = mixer_b_fwd(proj, b_w_s[j], bt, lg, lb, kv, "mixer_b")
            extra = (b_w_s[j], bt, lg, lb)
        h_mid = matmul(cat, w_out, "nn", F32, "mix_out", res=h)
        more = i + 1 < depth
        hn, gu, act, *gathered = gate_up_fwd(h_mid, row(ffn_norm_g[i]), *wl["w_gate_up"], "gate_up",
                                             rider=gather_of(i + 1, FFN) if more else None)
        h_out, *gathered_mixer = matmul(act, wl["w_down"], "nn", F32, "down", res=h_mid, rider=gather_of(i + 1, MIXER)) if more \
            else [matmul(act, wl["w_down"], "nn", F32, "down", res=h_mid)]
        if more:
            weights[i + 1] = {**usable(i + 1, gathered, FFN), **usable(i + 1, gathered_mixer, MIXER)}
        saved.append((h, xn, proj, cat, h_mid, hn, gu, act, kv, extra))
        h = h_out

    loss_part, dh, d_final_g, dh16 = loss_head(h, row(final_norm_g), tgt, "loss_head")
    loss = lax.psum(loss_part[0, 0], ("x", "y", "c"))

    d_mix_g, d_ffn_g = [None] * depth, [None] * depth
    d_sinks, d_ws, d_bias, d_lg, d_lb = [], [], [], [], []
    d_mem_n = jnp.zeros(mem2.shape, F32)
    totals = [None] * depth
    pending = None
    for i in reversed(range(depth)):
        h_in, xn, proj, cat, h_mid, hn, gu, act, kv, extra = saved[i]
        wl = weights[i]
        dgu, *from_sibling = down_dx_swiglu_bwd(dh16, wl["w_down"], gu, "down_dx",
                                                rider=sibling_halves_exchange(pending) if pending else None)
        if pending:
            partial = [chip_partial_sums(g, r1, c_arr, "grads_chip_sum") for g, r1 in zip(pending, from_sibling)]
        dw_down = matmul(act, dh16, "tn", F32, "down_dw", tm=1408, tk=2048, out_planes=("rows", 4))
        dw_gate_up = matmul((hn, 0), dgu, "tn", F32, "gate_up_dw", tn=1408, tk=2048, out_planes=("cols", 4))
        ffn = [dw_gate_up[None], dw_down[None]] if i == 0 else []
        joined = lambda exs: None if not exs else exs[0] if len(exs) == 1 else both_exchanges(*exs)
        dh, d_ffn_g[i], dh16, *landed = dx_norm_bwd(
            dgu, wl["w_gate_up"], h_mid, row(ffn_norm_g[i]), dh, "gate_up_dx",
            rider=joined(([chips_exchange(partial[FFN])] if pending else []) + ([sibling_halves_exchange(ffn)] if ffn else [])))
        if pending:
            chips_ffn = landed[:2]
        if ffn:
            ffn_sibling = landed[-len(ffn):]
            ffn_partial = [chip_partial_sums(g, r1, c_arr, "grads_chip_sum") for g, r1 in zip(ffn, ffn_sibling)]
        dcat = matmul(dh16, wl["w_out"], "nt", F32, "mix_out_dx")
        dw_out = matmul(cat, dh16, "tn", F32, "mix_out_dw", tk=2048, out_planes=("rows", 4))
        mixer_rider = joined(([chips_exchange(partial[MIXER])] if pending else []) + ([chips_exchange(ffn_partial)] if ffn else []))
        if i % 2 == 0:
            dproj, dsk, dkv, *landed = mixer_a_bwd(proj, dcat, extra[0], kv, "mixer_a_bwd", rider=mixer_rider)
            d_sinks.insert(0, dsk[:Q_HEADS, 0])
            dw_in = matmul(dproj, xn, "tn", F32, "a_in_dw", out_planes=("rows", 4))
        else:
            dproj, dws, dbt, dlg, dlb, dkv, *landed = mixer_b_bwd(proj, dcat, *extra, kv, "mixer_b_bwd", rider=mixer_rider)
            d_ws.insert(0, dws)
            d_bias.insert(0, dbt[:, :B_GROUPS].T)
            d_lg.insert(0, dlg[:B_GROUPS])
            d_lb.insert(0, dlb[:B_GROUPS])
            dw_in = matmul(dproj, xn, "tn", F32, "b_in_dw", out_planes=("rows", 4))
        if pending:
            from_chips = landed[:3] + chips_ffn
            totals[i + 1] = [shard_total(g, r1, r2, cs_arr, "grads_shard_total") for g, r1, r2 in zip(pending, from_sibling, from_chips)]
        if ffn:
            ffn_totals = [shard_total(g, r1, r2, cs_arr, "grads_shard_total") for g, r1, r2 in zip(ffn, ffn_sibling, landed[-len(ffn):])]
        dw_kv = matmul(mem_n, dkv, "tn", F32, "mem_kv_dw", out_planes=("rows", 4))
        d_mem_n = matmul(dkv, wl["w_mem_kv"], "nt", F32, "mem_kv_dx", res=d_mem_n)
        dh, d_mix_g[i], dh16 = dx_norm_bwd(dproj, wl["w_in"], h_in, row(mix_norm_g[i]), dh, "in_dx")
        pending = [dw_in[None], dw_out[None], dw_kv[None]] + ([] if ffn else [dw_gate_up[None], dw_down[None]])
    grad_x = dh.reshape(x.shape)
    _, d_mem_g = rmsnorm_bwd(mem2, row(mem_norm_g), d_mem_n, jnp.zeros(mem2.shape, F32), "mem_norm_bwd")

    from_sibling = run_exchange(sibling_halves_exchange(pending), "grads_sibling_swap")
    partial = [chip_partial_sums(g, r1, c_arr, "grads_chip_sum") for g, r1 in zip(pending, from_sibling)]
    from_chips = run_exchange(chips_exchange(partial), "grads_chips_exchange")
    totals[0] = [shard_total(g, r1, r2, cs_arr, "grads_shard_total") for g, r1, r2 in zip(pending, from_sibling, from_chips)] + ffn_totals

    mine = {n: [None] * given[n].shape[0] for n, _ in SHARDED}
    for l in range(depth):
        for (n, k), tot in zip(layer_weights(l), totals[l]):
            mine[n][k] = tot
    mine = [jnp.concatenate(mine[n], axis=0) for n, _ in SHARDED]
    theirs = run_exchange(sibling_exchange(mine), "grads_sibling_totals")
    out = {}
    for (n, _), g_mine, g_theirs in zip(SHARDED, mine, theirs):
        flip = (lambda a: jnp.swapaxes(a, 1, 2)) if n in ("a_w_in", "b_w_in") else (lambda a: a)
        shape = flip(given[n]).shape
        two_d = lambda a: a.reshape(-1, shape[-1])
        res = adamw_halves(two_d(flip(given[n])), two_d(g_mine), two_d(g_theirs), two_d(flip(given["m_" + n])),
                           two_d(flip(given["v_" + n])), c_arr, shape[1], "adamw")
        out[n] = tuple(flip(r.reshape(shape)) for r in res)

    small = ("mem_norm_g", "mix_norm_g", "ffn_norm_g", "final_norm_g", "a_sinks", "b_w_s", "b_bias_s", "b_ln_g", "b_ln_b")
    small_g = [d_mem_g[0], jnp.concatenate(d_mix_g, axis=0), jnp.concatenate(d_ffn_g, axis=0), d_final_g[0],
               jnp.stack(d_sinks), jnp.stack(d_ws), jnp.stack(d_bias), jnp.stack(d_lg), jnp.stack(d_lb)]
    packed = _pack(small_g)
    g_small = sum_devices(allgather_small(packed, "small_allgather").reshape(8, *packed.shape), "small_sum")
    like = [given[n] for n in small]
    delta_s, new_m_s, new_v_s = adamw(_pack(like), g_small, _pack([given["m_" + n] for n in small]),
                                      _pack([given["v_" + n] for n in small]), "adamw_small")
    for n, g, dl, nm_, nv_ in zip(small, _unpack(g_small, like), _unpack(delta_s, like), _unpack(new_m_s, like), _unpack(new_v_s, like)):
        out[n] = (g, dl, nm_, nv_)

    order = ("mem_norm_g", "mix_norm_g", "ffn_norm_g", "final_norm_g", "a_w_in", "a_sinks", "a_w_out", "b_w_in", "b_w_s",
             "b_bias_s", "b_ln_g", "b_ln_b", "b_w_out", "w_mem_kv", "w_gate_up", "w_down")
    return (loss, grad_x, *[out[n][0] for n in order], *[out[n][1] for n in order],
            *[out[n][2] for n in order], *[out[n][3] for n in order])
```

```python
import jax
import jax.numpy as jnp
from jax import lax
from jax.experimental import pallas as pl
from jax.experimental.pallas import tpu as pltpu

F32, BF16 = jnp.float32, jnp.bfloat16
EPS = 1e-6
HEAD_DIM = 64
Q_HEADS, KV_HEADS, GROUP = 12, 2, 6
WINDOW = 128
MEM_HEADS = 4
B_GROUPS = 6
Q_W, KV_W, MEM_W, B_W = 768, 128, 256, 768
SCALE = HEAD_DIM ** -0.5
NEG = -1e30
ADAM_LR, ADAM_B1, ADAM_B2, ADAM_EPS, ADAM_WD, ADAM_STEP = 0.001, 0.9, 0.999, 1e-08, 0.01, 10
V7X_VMEM_LIMIT_BYTES = 48 * 1024 * 1024
MESH = pl.DeviceIdType.MESH
HBM_SPEC = pl.BlockSpec(memory_space=pltpu.HBM)


def _cp(*sem):
    return pltpu.CompilerParams(dimension_semantics=sem or None, vmem_limit_bytes=V7X_VMEM_LIMIT_BYTES)


def _tile(n, cands):
    for t in cands:
        if n % t == 0:
            return t
    return n


def _sds(shape, dtype):
    return jax.ShapeDtypeStruct(tuple(shape), dtype)


def _dot(a, b, ca, cb):
    return lax.dot_general(a, b, (((ca,), (cb,)), ((), ())), preferred_element_type=F32)


def _rms(x, g):
    return x * lax.rsqrt(jnp.mean(x * x, axis=-1, keepdims=True) + EPS) * g


def rmsnorm_fwd(h, g, name):
    t, d = h.shape
    tm = _tile(t, (512, 256, 128))

    def body(h_ref, g_ref, o_ref):
        o_ref[...] = _rms(h_ref[...], g_ref[...]).astype(o_ref.dtype)

    return pl.pallas_call(
        body, name=name, grid=(t // tm,),
        in_specs=[pl.BlockSpec((tm, d), lambda i: (i, 0)), pl.BlockSpec((1, d), lambda i: (0, 0))],
        out_specs=pl.BlockSpec((tm, d), lambda i: (i, 0)),
        out_shape=_sds((t, d), BF16), compiler_params=_cp("parallel"))(h, g)


def rmsnorm_bwd(h, g, dxn, dres, name):
    t, d = h.shape
    tm = _tile(t, (512, 256, 128))

    def body(h_ref, g_ref, dxn_ref, dres_ref, dh_ref, dg_ref):
        _, vjp = jax.vjp(_rms, h_ref[...], g_ref[...])
        dh, dg = vjp(dxn_ref[...].astype(F32))
        dh_ref[...] = dres_ref[...] + dh

        @pl.when(pl.program_id(0) == 0)
        def _():
            dg_ref[...] = jnp.zeros_like(dg_ref)

        dg_ref[...] += dg

    row = pl.BlockSpec((tm, d), lambda i: (i, 0))
    vec = pl.BlockSpec((1, d), lambda i: (0, 0))
    return pl.pallas_call(
        body, name=name, grid=(t // tm,), in_specs=[row, vec, row, row], out_specs=[row, vec],
        out_shape=[_sds((t, d), F32), _sds((1, d), F32)], compiler_params=_cp("arbitrary"))(h, g, dxn, dres)


def loss_head(h, g, tgt, name):
    t, d = h.shape
    tm = _tile(t, (512, 256, 128))

    def body(h_ref, g_ref, t_ref, l_ref, dh_ref, dg_ref, dh16_ref):
        y, vjp = jax.vjp(_rms, h_ref[...], g_ref[...])
        err = y - t_ref[...]
        dh, dg = vjp(err * (1.0 / d))
        dh_ref[...] = dh
        dh16_ref[...] = dh.astype(BF16)
        part = 0.5 * jnp.sum(jnp.mean(err * err, axis=-1, keepdims=True), axis=0, keepdims=True)

        @pl.when(pl.program_id(0) == 0)
        def _():
            dg_ref[...] = jnp.zeros_like(dg_ref)
            l_ref[...] = jnp.zeros_like(l_ref)

        dg_ref[...] += dg
        l_ref[...] += part

    row = pl.BlockSpec((tm, d), lambda i: (i, 0))
    vec = pl.BlockSpec((1, d), lambda i: (0, 0))
    one = pl.BlockSpec((1, 1), lambda i: (0, 0))
    return pl.pallas_call(
        body, name=name, grid=(t // tm,), in_specs=[row, vec, row], out_specs=[one, row, vec, row],
        out_shape=[_sds((1, 1), F32), _sds((t, d), F32), _sds((1, d), F32), _sds((t, d), BF16)],
        compiler_params=_cp("arbitrary"))(h, g, tgt)


def _logical(op):
    arr, lead = op if isinstance(op, tuple) else (op, None)
    planes = arr.shape[-3] if arr.ndim - (lead is not None) == 3 else 1
    return arr, lead, arr.shape[-2], arr.shape[-1], planes


def _spec(op, rows_t, cols_t, row_of, col_of):
    arr, lead, _, cols, _ = _logical(op)
    per = cols // cols_t
    lead = () if lead is None else (lead,)
    if arr.ndim - len(lead) == 2:
        return pl.BlockSpec((None,) * len(lead) + (rows_t, cols_t), lambda *g: lead + (row_of(*g), col_of(*g)))
    return pl.BlockSpec((None,) * len(lead) + (None, rows_t, cols_t),
                        lambda *g: lead + (col_of(*g) // per, row_of(*g), col_of(*g) % per))


def _arr(op):
    return op[0] if isinstance(op, tuple) else op


def _resident_whole(w, d):
    wa, layer = w
    planes, per = wa.shape[-3], wa.shape[-1]

    def fill(w_ref, whole_ref):
        for s in range(planes):
            whole_ref[:, s * per:(s + 1) * per] = w_ref[s]

    return (pl.BlockSpec((None, planes, d, per), lambda i: (layer, 0, 0, 0), pipeline_mode=pl.Buffered(1)),
            pltpu.VMEM((d, planes * per), wa.dtype), fill)


def norm_matmul(h, g, w, out_dtype, name):
    t, d = h.shape
    w_spec, whole, fill = _resident_whole(w, d)
    n = whole.shape[1]
    tm = _tile(t, (512, 256, 128))

    def body(h_ref, g_ref, w_ref, xn_ref, o_ref, whole_ref):
        @pl.when(pl.program_id(0) == 0)
        def _():
            fill(w_ref, whole_ref)

        xn = _rms(h_ref[...], g_ref[...]).astype(BF16)
        xn_ref[...] = xn
        o_ref[...] = _dot(xn, whole_ref[...], 1, 0).astype(o_ref.dtype)

    return pl.pallas_call(
        body, name=name, grid=(t // tm,),
        in_specs=[pl.BlockSpec((tm, d), lambda i: (i, 0)), pl.BlockSpec((1, d), lambda i: (0, 0)), w_spec],
        out_specs=[pl.BlockSpec((tm, d), lambda i: (i, 0)), pl.BlockSpec((tm, n), lambda i: (i, 0))],
        out_shape=[_sds((t, d), BF16), _sds((t, n), out_dtype)], scratch_shapes=[whole],
        compiler_params=_cp("arbitrary"))(h, g, w[0])


def dx_norm_bwd(dy, w, h, g, dres, name, rider=None):
    t, d = h.shape
    dy_arr, dy_lead, _, kc, kp = _logical(dy)
    w_arr, w_lead, _, wc, wp = _logical(w)
    assert kc * kp == wc * wp and dy_lead is None, name
    chunk = min(kc, wc)
    tm = _tile(t, (512, 256, 128))

    def piece(ref, planes, cols, q):
        off = q * chunk % cols
        return ref[q * chunk // cols, :, off:off + chunk] if planes > 1 else ref[:, off:off + chunk]

    narrow = wp > 1 and wc % 128 != 0
    if narrow:
        assert kp == 1, name
        w_whole_spec, whole, fill = _resident_whole(w, d)

    def body(dy_ref, w_ref, h_ref, g_ref, dres_ref, dh_ref, dg_ref, dh16_ref, *whole_ref):
        if narrow:
            @pl.when(pl.program_id(0) == 0)
            def _():
                fill(w_ref, whole_ref[0])

            dxn = _dot(dy_ref[...].astype(BF16), whole_ref[0][...], 1, 1)
        else:
            dxn = None
            for q in range(kc * kp // chunk):
                p = _dot(piece(dy_ref, kp, kc, q).astype(BF16), piece(w_ref, wp, wc, q), 1, 1)
                dxn = p if dxn is None else dxn + p
        _, vjp = jax.vjp(_rms, h_ref[...], g_ref[...])
        dh, dg = vjp(dxn)
        dh = dres_ref[...] + dh
        dh_ref[...] = dh
        dh16_ref[...] = dh.astype(BF16)

        @pl.when(pl.program_id(0) == 0)
        def _():
            dg_ref[...] = jnp.zeros_like(dg_ref)

        dg_ref[...] += dg

    w_lead = () if w_lead is None else (w_lead,)
    w_block = ((wp,) if wp > 1 else ()) + (d, wc)
    w_spec = pl.BlockSpec((None,) * len(w_lead) + w_block, lambda i: w_lead + (0,) * len(w_block), pipeline_mode=pl.Buffered(1))
    if narrow:
        w_spec = w_whole_spec
    dy_spec = pl.BlockSpec((kp, tm, kc), lambda i: (0, i, 0)) if kp > 1 else pl.BlockSpec((tm, kc), lambda i: (i, 0))
    row = pl.BlockSpec((tm, d), lambda i: (i, 0))
    vec = pl.BlockSpec((1, d), lambda i: (0, 0))
    grid = (t // tm,)
    body, r_ops, r_in, r_shapes, r_out, r_scratch = with_rider(body, 5, 3, grid, rider)
    return pl.pallas_call(
        body, name=name, grid=grid, in_specs=[dy_spec, w_spec, row, vec, row] + r_in,
        out_specs=[row, vec, row] + r_out, out_shape=[_sds((t, d), F32), _sds((1, d), F32), _sds((t, d), BF16)] + r_shapes,
        scratch_shapes=([whole] if narrow else []) + r_scratch,
        compiler_params=_cp("arbitrary"))(dy_arr, w_arr, h, g, dres, *r_ops)


def matmul(a, b, mode, out_dtype, name, res=None, tm=None, tn=1792, tk=2816, out_planes=None, out_into=None, rider=None):
    _, _, ar, ac, ap = _logical(a)
    _, _, br, bc, bp = _logical(b)
    if mode == "nn":
        m, ka, kb, n = ar, ac * ap, br, bc * bp
        n_plane, ka_plane, kb_plane = bc, ac, br
    elif mode == "nt":
        m, ka, n, kb = ar, ac * ap, br, bc * bp
        n_plane, ka_plane, kb_plane = br, ac, bc
    else:
        ka, m, kb, n = ar, ac * ap, br, bc * bp
        n_plane, ka_plane, kb_plane = bc, ar, br
    m_plane = ac if mode == "tn" else ar
    assert ka == kb, name
    k = ka
    kind, planes = out_planes or ("cols", 1)
    narrow = kind == "cols" and (n // planes) % 128 != 0
    if kind == "cols" and not narrow:
        n_plane = min(n_plane, n // planes)
    if narrow:
        tn = n
    tm = _tile(m_plane, ((1024, 1408, 512, 256, 128) if mode == "tn" else (512, 256, 128)) if tm is None else (tm, 1024, 512, 256, 128))
    if kind == "rows" and tm % (m // planes):
        tm = m_plane
    tn = _tile(n_plane, (tn, 1792, 1408, 1280, 1024, 896, 640, 512, 256, 128))
    tk = _tile(min(ka_plane, kb_plane), (tk, 2816, 1792, 1408, 1280, 1024, 512, 256, 128))
    nk = k // tk
    row_i, col_j, red = (lambda i, j, kk: i), (lambda i, j, kk: j), (lambda i, j, kk: kk)
    if mode == "nn":
        a_spec, b_spec, ca, cb = _spec(a, tm, tk, row_i, red), _spec(b, tk, tn, red, col_j), 1, 0
    elif mode == "nt":
        a_spec, b_spec, ca, cb = _spec(a, tm, tk, row_i, red), _spec(b, tn, tk, col_j, red), 1, 1
    else:
        a_spec, b_spec, ca, cb = _spec(a, tk, tm, red, row_i), _spec(b, tk, tn, red, col_j), 0, 0
    lead = () if out_into is None else (out_into[1],)
    if planes == 1:
        o_shape, o_block = (m, n), (tm, tn)
        o_index = lambda i, j, kk: lead + (i, j)
    elif narrow:
        o_shape, o_block = (planes, m, n // planes), (planes, tm, n // planes)
        o_index = lambda i, j, kk: lead + (0, i, 0)
    elif kind == "cols":
        per = n // planes // tn
        o_shape, o_block = (planes, m, n // planes), (None, tm, tn)
        o_index = lambda i, j, kk: lead + (j // per, i, j % per)
    else:
        o_shape, o_block = (planes, m // planes, n), (tm // (m // planes), m // planes, tn)
        o_index = lambda i, j, kk: lead + (i, 0, j)
    o_spec = pl.BlockSpec((None,) * len(lead) + o_block, o_index)
    if out_into is not None:
        assert out_into[0].shape[1:] == o_shape and out_into[0].dtype == out_dtype, name
        o_shape = out_into[0].shape
    has_res = res is not None
    n_in = 2 + has_res + (out_into is not None)

    def put(o_ref, v):
        if narrow:
            for s in range(planes):
                o_ref[s] = v[:, s * (n // planes):(s + 1) * (n // planes)].astype(o_ref.dtype)
        else:
            o_ref[...] = v.astype(o_ref.dtype).reshape(o_ref.shape)

    def body(*refs):
        a_ref, b_ref = refs[:2]
        rest = refs[2:2 + has_res] + refs[n_in:]
        o_ref = rest[1] if has_res else rest[0]
        p = _dot(a_ref[...].astype(BF16), b_ref[...].astype(BF16), ca, cb)
        if nk == 1:
            if has_res:
                p = p + rest[0][...]
            put(o_ref, p)
        else:
            acc_ref = rest[-1]
            kk = pl.program_id(2)

            @pl.when(kk == 0)
            def _():
                acc_ref[...] = p

            @pl.when(kk > 0)
            def _():
                acc_ref[...] += p

            @pl.when(kk == nk - 1)
            def _():
                r = acc_ref[...]
                if has_res:
                    r = r + rest[0][...]
                put(o_ref, r)

    operands = [_arr(a), _arr(b)] + ([res] if has_res else []) + ([out_into[0]] if out_into is not None else [])
    grid = (m // tm, n // tn, nk)
    body, r_ops, r_in, r_shapes, r_out, r_scratch = with_rider(body, n_in, 1, grid, rider)
    out = pl.pallas_call(
        body, name=name, grid=grid,
        in_specs=[a_spec, b_spec] + ([pl.BlockSpec((tm, tn), lambda i, j, kk: (i, j))] if has_res else [])
        + ([pl.BlockSpec(memory_space=pl.ANY)] if out_into is not None else []) + r_in,
        out_specs=[o_spec] + r_out, out_shape=[_sds(o_shape, out_dtype)] + r_shapes,
        input_output_aliases={n_in - 1: 0} if out_into is not None else {},
        scratch_shapes=([pltpu.VMEM((tm, tn), F32)] if nk > 1 else []) + r_scratch,
        compiler_params=_cp(*(("arbitrary",) * 3 if rider else ("parallel", "parallel", "arbitrary"))))(*operands, *r_ops)
    return out if rider else out[0]


def gate_up_fwd(h, g, w, layer, name, rider=None):
    t, d = h.shape
    half = w.shape[-1]
    tm = _tile(t, (512, 256, 128))

    def body(h_ref, g_ref, wg_ref, wu_ref, hn_ref, gu_ref, act_ref):
        a = _rms(h_ref[...], g_ref[...]).astype(BF16)
        hn_ref[...] = a
        gate, up = _dot(a, wg_ref[...], 1, 0), _dot(a, wu_ref[...], 1, 0)
        sig = 1.0 / (1.0 + jnp.exp(-gate))
        silu = gate * sig
        gu_ref[0] = (up * (sig + silu * (1.0 - sig))).astype(gu_ref.dtype)
        gu_ref[1] = silu.astype(gu_ref.dtype)
        act_ref[...] = (silu * up).astype(act_ref.dtype)

    grid = (2, t // tm)
    body, r_ops, r_in, r_shapes, r_out, r_scratch = with_rider(body, 4, 3, grid, rider)
    return pl.pallas_call(
        body, name=name, grid=grid,
        in_specs=[pl.BlockSpec((tm, d), lambda j, i: (i, 0)), pl.BlockSpec((1, d), lambda j, i: (0, 0)),
                  pl.BlockSpec((None, None, d, half), lambda j, i: (layer, j, 0, 0)),
                  pl.BlockSpec((None, None, d, half), lambda j, i: (layer, 2 + j, 0, 0))] + r_in,
        out_specs=[pl.BlockSpec((None, tm, d), lambda j, i: (j, i, 0)), pl.BlockSpec((2, tm, half), lambda j, i: (0, i, j)),
                   pl.BlockSpec((tm, half), lambda j, i: (i, j))] + r_out,
        out_shape=[_sds((2, t, d), BF16), _sds((2, t, 2 * half), BF16), _sds((t, 2 * half), BF16)] + r_shapes,
        scratch_shapes=r_scratch, compiler_params=_cp("arbitrary", "arbitrary"))(h, g, w, w, *r_ops)


def down_dx_swiglu_bwd(dh, wd, gu, name, rider=None):
    t, d = dh.shape
    w, layer = wd
    f = w.shape[-2]
    tm = _tile(t, (512, 256, 128))
    tn = _tile(f, (1408, 512, 256, 128))

    def body(dh_ref, w_ref, gu_ref, o_ref):
        dact = _dot(dh_ref[...].astype(BF16), w_ref[...], 1, 1)
        o_ref[0] = (dact * gu_ref[0].astype(F32)).astype(o_ref.dtype)
        o_ref[1] = (dact * gu_ref[1].astype(F32)).astype(o_ref.dtype)

    planes = pl.BlockSpec((2, tm, tn), lambda j, i: (0, i, j))
    grid = (f // tn, t // tm)
    body, r_ops, r_in, r_shapes, r_out, r_scratch = with_rider(body, 3, 1, grid, rider)
    return pl.pallas_call(
        body, name=name, grid=grid,
        in_specs=[pl.BlockSpec((tm, d), lambda j, i: (i, 0)), pl.BlockSpec((None, tn, d), lambda j, i: (layer, j, 0)), planes] + r_in,
        out_specs=[planes] + r_out, out_shape=[_sds((2, t, f), BF16)] + r_shapes, scratch_shapes=r_scratch,
        compiler_params=_cp("arbitrary", "arbitrary"))(dh, w, gu, *r_ops)


def _softmax_over_keys(s, sink=None):
    m = s.max(axis=0, keepdims=True)
    if sink is not None:
        m = jnp.maximum(m, sink)
    m = lax.stop_gradient(m)
    e = jnp.exp(s - m)
    den = e.sum(axis=0, keepdims=True)
    if sink is not None:
        den = den + jnp.exp(sink - m)
    return e * (1.0 / den)


def _low_lanes():
    return lax.broadcasted_iota(jnp.int32, (1, 128), 1) < HEAD_DIM


def _stack_heads(slabs):
    low = _low_lanes()
    return jnp.concatenate([p for s in slabs for p in (jnp.where(low, s, 0.0), jnp.where(low, 0.0, s))], axis=0)


def _unstack_heads(o, n_slabs):
    low = _low_lanes()
    return [jnp.where(low, o[2 * j * WINDOW:(2 * j + 1) * WINDOW], o[(2 * j + 1) * WINDOW:(2 * j + 2) * WINDOW])
            for j in range(n_slabs)]


def _swa_group(q_slabs, k_both, v_both, sinks, mask):
    qs = _stack_heads(q_slabs).astype(BF16)
    s = jnp.where(mask, _dot(k_both.astype(BF16), qs, 1, 1) * SCALE, NEG)
    sink = jnp.concatenate([jnp.broadcast_to(v, (1, WINDOW)) for v in sinks], axis=1)
    return _unstack_heads(_dot(_softmax_over_keys(s, sink).astype(BF16), v_both.astype(BF16), 0, 0), len(q_slabs))


def _mem_pair(q_slab, k_slab, v_slab):
    s = _dot(k_slab.astype(BF16), _stack_heads([q_slab]).astype(BF16), 1, 1) * SCALE
    return _unstack_heads(_dot(_softmax_over_keys(s).astype(BF16), v_slab.astype(BF16), 0, 0), 1)[0]


def _gelu(x):
    return 0.5 * x * (1.0 + jnp.tanh(0.7978845608028654 * (x + 0.044715 * (x * x * x))))


def _gmlp_group(zu, zv, w, bcol, lg, lb, tri):
    u, v = _gelu(zu), _gelu(zv)
    mu = jnp.mean(v, axis=-1, keepdims=True)
    var = jnp.mean(jnp.square(v - mu), axis=-1, keepdims=True)
    vn = (v - mu) * lax.rsqrt(var + EPS) * lg + lb
    sv = _dot(jnp.where(tri, w, 0.0).astype(BF16), vn.astype(BF16), 1, 0) + bcol
    return u * sv


def _cols(x, width):
    return [x[:, j * width:(j + 1) * width] for j in range(x.shape[1] // width)]


def _swa_mask(has_prev):
    qi = lax.broadcasted_iota(jnp.int32, (2 * WINDOW, GROUP * WINDOW), 1) & (WINDOW - 1)
    kj = lax.broadcasted_iota(jnp.int32, (2 * WINDOW, GROUP * WINDOW), 0)
    in_prev = jnp.logical_and(jnp.logical_and(kj < WINDOW, kj > qi), has_prev)
    return jnp.logical_or(in_prev, jnp.logical_and(kj >= WINDOW, kj - WINDOW <= qi))


def _mix_a(q_slabs, k_boths, v_boths, sinks, qm_slabs, km_slabs, vm_slabs, mask):
    per = GROUP // 2
    outs = []
    for g in range(KV_HEADS):
        outs += _swa_group(q_slabs[per * g:per * (g + 1)], k_boths[g], v_boths[g], sinks[GROUP * g:GROUP * (g + 1)], mask)
    return outs + [_mem_pair(qm_slabs[j], km_slabs[j], vm_slabs[j]) for j in range(MEM_HEADS // 2)]


def _in_both_halves(prev, cur):
    cat = jnp.concatenate([prev, cur], axis=0)
    rolled = pltpu.roll(cat, HEAD_DIM, axis=1)
    low = _low_lanes()
    return [jnp.where(low, cat, rolled), jnp.where(low, rolled, cat)]


def _from_both_halves(d_boths):
    t = [d + pltpu.roll(d, HEAD_DIM, axis=1) for d in d_boths]
    return jnp.where(_low_lanes(), t[0], t[1])


def _mix_a_specs(nm, blk):
    prev = lambda n: jnp.maximum(blk(n) - 1, 0)
    return [pl.BlockSpec((WINDOW, Q_W), lambda n: (blk(n), 0)),
            pl.BlockSpec((WINDOW, KV_W), lambda n: (prev(n), Q_W // KV_W)),
            pl.BlockSpec((WINDOW, KV_W), lambda n: (blk(n), Q_W // KV_W)),
            pl.BlockSpec((WINDOW, KV_W), lambda n: (prev(n), Q_W // KV_W + 1)),
            pl.BlockSpec((WINDOW, KV_W), lambda n: (blk(n), Q_W // KV_W + 1)),
            pl.BlockSpec((WINDOW, MEM_W), lambda n: (blk(n), (Q_W + 2 * KV_W) // MEM_W)),
            pl.BlockSpec((16, 128), lambda n: (0, 0)),
            pl.BlockSpec((nm, MEM_W), lambda n: (0, 0)),
            pl.BlockSpec((nm, MEM_W), lambda n: (0, 1))]


def _mix_a_args(refs):
    q, kp, kc, vp, vc, qm, sk, km, vm = [r[...].astype(F32) for r in refs]
    return (_cols(q, 128), _in_both_halves(kp, kc), _in_both_halves(vp, vc), [sk[h:h + 1, 0:1] for h in range(Q_HEADS)],
            _cols(qm, 128), _cols(km, 128), _cols(vm, 128))


def mixer_a_fwd(proj, sk, kv, name, rider=None):
    t, nm = proj.shape[0], kv.shape[0]

    def body(*refs):
        o_ref = refs[-1]
        slabs = _mix_a(*_mix_a_args(refs[:-1]), _swa_mask(pl.program_id(0) > 0))
        o_ref[...] = jnp.concatenate(slabs, axis=1).astype(o_ref.dtype)

    grid = (t // WINDOW,)
    body, r_ops, r_in, r_shapes, r_out, r_scratch = with_rider(body, 9, 1, grid, rider)
    return pl.pallas_call(
        body, name=name, grid=grid, in_specs=_mix_a_specs(nm, lambda n: n) + r_in,
        out_specs=[pl.BlockSpec((WINDOW, Q_W + MEM_W), lambda n: (n, 0))] + r_out,
        out_shape=[_sds((t, Q_W + MEM_W), BF16)] + r_shapes, scratch_shapes=r_scratch,
        compiler_params=_cp("arbitrary"))(proj, proj, proj, proj, proj, proj, sk, kv, kv, *r_ops)


def _onehot_rows(vals, shape):
    rows = lax.broadcasted_iota(jnp.int32, shape, 0)
    out = jnp.zeros(shape, F32)
    for h, v in enumerate(vals):
        out = out + jnp.where(rows == h, jnp.broadcast_to(v, shape), 0.0)
    return out


def mixer_a_bwd(proj, dcat, sk, kv, name, rider=None):
    t, nm = proj.shape[0], kv.shape[0]
    nb = t // WINDOW
    blk = lambda i: nb - 1 - i

    def body(*refs):
        dcat_ref, dproj_ref, dsk_ref, dkv_ref, carry_ref = refs[9:]
        i = pl.program_id(0)

        @pl.when(i == 0)
        def _():
            carry_ref[...] = jnp.zeros_like(carry_ref)
            dsk_ref[...] = jnp.zeros_like(dsk_ref)
            dkv_ref[...] = jnp.zeros_like(dkv_ref)

        mask = _swa_mask(blk(i) > 0)
        _, vjp = jax.vjp(lambda *a: _mix_a(*a, mask), *_mix_a_args(refs[:9]))
        dqs, dk_boths, dv_boths, dsinks, dqms, dkms, dvms = vjp(_cols(dcat_ref[...].astype(F32), 128))
        dkv = jnp.concatenate([_from_both_halves(dk_boths), _from_both_halves(dv_boths)], axis=1)
        dkv_cur = dkv[WINDOW:] + carry_ref[...]
        carry_ref[...] = dkv[:WINDOW]
        dproj_ref[...] = jnp.concatenate(dqs + [dkv_cur] + dqms, axis=1).astype(dproj_ref.dtype)
        dsk_ref[...] += _onehot_rows(dsinks, (16, 128))
        dkv_ref[...] += jnp.concatenate(dkms + dvms, axis=1)

    width = Q_W + 2 * KV_W + MEM_W
    body, r_ops, r_in, r_shapes, r_out, r_scratch = with_rider(body, 10, 3, (nb,), rider)
    return pl.pallas_call(
        body, name=name, grid=(nb,),
        in_specs=_mix_a_specs(nm, blk) + [pl.BlockSpec((WINDOW, Q_W + MEM_W), lambda i: (blk(i), 0))] + r_in,
        out_specs=[pl.BlockSpec((WINDOW, width), lambda i: (blk(i), 0)), pl.BlockSpec((16, 128), lambda i: (0, 0)),
                   pl.BlockSpec((nm, 2 * MEM_W), lambda i: (0, 0))] + r_out,
        out_shape=[_sds((t, width), BF16), _sds((16, 128), F32), _sds((nm, 2 * MEM_W), F32)] + r_shapes,
        scratch_shapes=[pltpu.VMEM((WINDOW, 2 * KV_W), F32)] + r_scratch,
        compiler_params=_cp("arbitrary"))(proj, proj, proj, proj, proj, proj, sk, kv, kv, dcat, *r_ops)


def _mix_b(zus, zvs, ws, bcols, lgs, lbs, qms, kms, vms, tri):
    outs = [_gmlp_group(zus[g], zvs[g], ws[g], bcols[g], lgs[g], lbs[g], tri) for g in range(B_GROUPS)]
    return outs + [_mem_pair(qms[j], kms[j], vms[j]) for j in range(MEM_HEADS // 2)]


def _mix_b_specs(nm):
    return [pl.BlockSpec((WINDOW, 2 * B_W), lambda n: (n, 0)),
            pl.BlockSpec((WINDOW, MEM_W), lambda n: (n, 2 * B_W // MEM_W)),
            pl.BlockSpec((B_GROUPS, WINDOW, WINDOW), lambda n: (0, 0, 0)),
            pl.BlockSpec((WINDOW, 128), lambda n: (0, 0)),
            pl.BlockSpec((8, 128), lambda n: (0, 0)),
            pl.BlockSpec((8, 128), lambda n: (0, 0)),
            pl.BlockSpec((nm, MEM_W), lambda n: (0, 0)),
            pl.BlockSpec((nm, MEM_W), lambda n: (0, 1))]


def _mix_b_args(refs):
    z, qm, ws, bt, lg, lb, km, vm = [r[...].astype(F32) for r in refs]
    zs = _cols(z, 128)
    return (zs[:B_GROUPS], zs[B_GROUPS:], [ws[g] for g in range(B_GROUPS)], [bt[:, g:g + 1] for g in range(B_GROUPS)],
            [lg[g:g + 1, :] for g in range(B_GROUPS)], [lb[g:g + 1, :] for g in range(B_GROUPS)],
            _cols(qm, 128), _cols(km, 128), _cols(vm, 128))


def _tri():
    return lax.broadcasted_iota(jnp.int32, (WINDOW, WINDOW), 0) >= lax.broadcasted_iota(jnp.int32, (WINDOW, WINDOW), 1)


def mixer_b_fwd(proj, ws, bt, lg, lb, kv, name):
    t, nm = proj.shape[0], kv.shape[0]

    def body(*refs):
        o_ref = refs[-1]
        o_ref[...] = jnp.concatenate(_mix_b(*_mix_b_args(refs[:-1]), _tri()), axis=1).astype(o_ref.dtype)

    return pl.pallas_call(
        body, name=name, grid=(t // WINDOW,), in_specs=_mix_b_specs(nm),
        out_specs=pl.BlockSpec((WINDOW, B_W + MEM_W), lambda n: (n, 0)),
        out_shape=_sds((t, B_W + MEM_W), BF16), compiler_params=_cp("parallel"))(proj, proj, ws, bt, lg, lb, kv, kv)


def mixer_b_bwd(proj, dcat, ws, bt, lg, lb, kv, name, rider=None):
    t, nm = proj.shape[0], kv.shape[0]

    def body(*refs):
        dcat_ref, dproj_ref, dws_ref, dbt_ref, dlg_ref, dlb_ref, dkv_ref = refs[8:]

        @pl.when(pl.program_id(0) == 0)
        def _():
            for r in (dws_ref, dbt_ref, dlg_ref, dlb_ref, dkv_ref):
                r[...] = jnp.zeros_like(r)

        tri = _tri()
        zus, zvs, ws, bcols, lgs, lbs, qms, kms, vms = _mix_b_args(refs[:8])
        douts = _cols(dcat_ref[...].astype(F32), 128)
        grads = []
        for g in range(B_GROUPS):
            _, vjp = jax.vjp(lambda *a: _gmlp_group(*a, tri), zus[g], zvs[g], ws[g], bcols[g], lgs[g], lbs[g])
            grads.append(vjp(douts[g]))
        dzus, dzvs, dws, dbcols, dlgs, dlbs = [list(t) for t in zip(*grads)]
        grads = []
        for j in range(MEM_HEADS // 2):
            _, vjp = jax.vjp(_mem_pair, qms[j], kms[j], vms[j])
            grads.append(vjp(douts[B_GROUPS + j]))
        dqms, dkms, dvms = [list(t) for t in zip(*grads)]
        dproj_ref[...] = jnp.concatenate(dzus + dzvs + dqms, axis=1).astype(dproj_ref.dtype)
        for g in range(B_GROUPS):
            dws_ref[g] += dws[g]
        lanes = lax.broadcasted_iota(jnp.int32, (WINDOW, 128), 1)
        dbt = jnp.zeros((WINDOW, 128), F32)
        for g in range(B_GROUPS):
            dbt = dbt + jnp.where(lanes == g, jnp.broadcast_to(dbcols[g], (WINDOW, 128)), 0.0)
        dbt_ref[...] += dbt
        dlg_ref[...] += _onehot_rows(dlgs, (8, 128))
        dlb_ref[...] += _onehot_rows(dlbs, (8, 128))
        dkv_ref[...] += jnp.concatenate(dkms + dvms, axis=1)

    width = 2 * B_W + MEM_W
    const2 = lambda n: (0, 0)
    grid = (t // WINDOW,)
    body, r_ops, r_in, r_shapes, r_out, r_scratch = with_rider(body, 9, 6, grid, rider)
    return pl.pallas_call(
        body, name=name, grid=grid,
        in_specs=_mix_b_specs(nm) + [pl.BlockSpec((WINDOW, B_W + MEM_W), lambda n: (n, 0))] + r_in,
        out_specs=[pl.BlockSpec((WINDOW, width), lambda n: (n, 0)),
                   pl.BlockSpec((B_GROUPS, WINDOW, WINDOW), lambda n: (0, 0, 0)),
                   pl.BlockSpec((WINDOW, 128), const2), pl.BlockSpec((8, 128), const2), pl.BlockSpec((8, 128), const2),
                   pl.BlockSpec((nm, 2 * MEM_W), const2)] + r_out,
        out_shape=[_sds((t, width), BF16), _sds((B_GROUPS, WINDOW, WINDOW), F32), _sds((WINDOW, 128), F32),
                   _sds((8, 128), F32), _sds((8, 128), F32), _sds((nm, 2 * MEM_W), F32)] + r_shapes,
        scratch_shapes=r_scratch, compiler_params=_cp("arbitrary"))(proj, proj, ws, bt, lg, lb, kv, kv, dcat, *r_ops)


def _adamw_update(w, g, m, v):
    m2 = ADAM_B1 * m + (1.0 - ADAM_B1) * g
    v2 = ADAM_B2 * v + (1.0 - ADAM_B2) * jnp.square(g)
    m_hat = m2 / (1.0 - ADAM_B1 ** ADAM_STEP)
    v_hat = v2 / (1.0 - ADAM_B2 ** ADAM_STEP)
    return -ADAM_LR * (m_hat / (jnp.sqrt(v_hat) + ADAM_EPS) + ADAM_WD * w), m2, v2


def adamw(w, g, m, v, name):
    r, c = w.shape
    tr = _tile(r, (512, 352, 256, 128, 64, 32, 16, 8))

    def body(w_ref, g_ref, m_ref, v_ref, d_ref, nm_ref, nv_ref):
        d_ref[...], nm_ref[...], nv_ref[...] = _adamw_update(w_ref[...], g_ref[...], m_ref[...], v_ref[...])

    spec = pl.BlockSpec((tr, c), lambda i: (i, 0))
    return pl.pallas_call(
        body, name=name, grid=(r // tr,), in_specs=[spec] * 4, out_specs=[spec] * 3,
        out_shape=[_sds((r, c), F32)] * 3, compiler_params=_cp("parallel"))(w, g, m, v)


def adamw_halves(w, g_mine, g_theirs, m, v, c_arr, rows, name, rider=None):
    r, c = w.shape
    tr = _tile(rows // 2, (256, 352, 224, 160, 128, 64, 32, 16, 8))
    per_half = rows // 2 // tr

    def body(c_ref, w_ref, gm_ref, gt_ref, m_ref, v_ref, g_ref, d_ref, nm_ref, nv_ref):
        g = jnp.where(pl.program_id(0) // per_half % 2 == c_ref[0], gm_ref[...], gt_ref[...])
        g_ref[...] = g
        d_ref[...], nm_ref[...], nv_ref[...] = _adamw_update(w_ref[...], g, m_ref[...], v_ref[...])

    spec = pl.BlockSpec((tr, c), lambda i, cr: (i, 0))
    half = pl.BlockSpec((tr, c), lambda i, cr: (i // (2 * per_half) * per_half + i % per_half, 0))
    grid = (r // tr,)
    body, r_ops, r_in, r_shapes, r_out, r_scratch = with_rider(body, 6, 4, grid, rider)
    return pl.pallas_call(
        body, name=name,
        grid_spec=pltpu.PrefetchScalarGridSpec(num_scalar_prefetch=1, grid=grid, in_specs=[spec, half, half, spec, spec] + r_in,
                                               out_specs=[spec] * 4 + r_out, scratch_shapes=r_scratch),
        out_shape=[_sds((r, c), F32)] * 4 + r_shapes,
        compiler_params=_cp("arbitrary" if rider else "parallel"))(c_arr, w, g_mine, g_theirs, m, v, *r_ops)


def _place():
    return lax.axis_index("x"), lax.axis_index("y"), lax.axis_index("c")


def _other_chips(x, y):
    return [(1 - x, y), (x, 1 - y), (1 - x, 1 - y)]


def _remote(src, dst, send_sems, recv_sems, k, dev):
    return pltpu.make_async_remote_copy(src_ref=src, dst_ref=dst, send_sem=send_sems.at[k], recv_sem=recv_sems.at[k],
                                        device_id=dev, device_id_type=MESH)


class Exchange:
    def __init__(self, ins, out_shapes, n_sems, start, finish):
        self.ins, self.out_shapes, self.start, self.finish = list(ins), list(out_shapes), start, finish
        self.sems = [n_sems, n_sems] if isinstance(n_sems, int) else list(n_sems)

    def scratch(self):
        return [pltpu.SemaphoreType.DMA((n,)) for n in self.sems]


def both_exchanges(a, b):
    ni, no, ns = len(a.ins), len(a.out_shapes), len(a.sems)

    def start(ins, outs, *sems):
        a.start(ins[:ni], outs[:no], *sems[:ns])
        b.start(ins[ni:], outs[no:], *sems[ns:])

    def finish(ins, outs, *sems):
        a.finish(ins[:ni], outs[:no], *sems[:ns])
        b.finish(ins[ni:], outs[no:], *sems[ns:])

    return Exchange(a.ins + b.ins, a.out_shapes + b.out_shapes, a.sems + b.sems, start, finish)


def run_exchange(ex, name):
    ni, no = len(ex.ins), len(ex.out_shapes)

    def body(*refs):
        ex.start(refs[:ni], refs[ni:ni + no], *refs[ni + no:])
        ex.finish(refs[:ni], refs[ni:ni + no], *refs[ni + no:])

    return pl.pallas_call(
        body, name=name, in_specs=[HBM_SPEC] * ni, out_specs=[HBM_SPEC] * no, out_shape=ex.out_shapes, scratch_shapes=ex.scratch(),
        compiler_params=pltpu.CompilerParams(has_side_effects=True))(*ex.ins)


def with_rider(body, n_in, n_out, grid, ex):
    if ex is None:
        return body, [], [], [], [], []
    ni, no, ns = len(ex.ins), len(ex.out_shapes), len(ex.sems)

    def riding(*refs):
        r_in, r_out, sems = refs[n_in:n_in + ni], refs[n_in + ni + n_out:n_in + ni + n_out + no], refs[-ns:]
        first = last = None
        for axis, size in enumerate(grid):
            at_first, at_last = pl.program_id(axis) == 0, pl.program_id(axis) == size - 1
            first = at_first if first is None else jnp.logical_and(first, at_first)
            last = at_last if last is None else jnp.logical_and(last, at_last)

        @pl.when(first)
        def _():
            ex.start(r_in, r_out, *sems)

        body(*refs[:n_in], *refs[n_in + ni:n_in + ni + n_out], *refs[n_in + ni + n_out + no:-ns])

        @pl.when(last)
        def _():
            ex.finish(r_in, r_out, *sems)

    return riding, ex.ins, [HBM_SPEC] * ni, ex.out_shapes, [HBM_SPEC] * no, ex.scratch()


def gather_exchange(shards):
    nw = len(shards)
    entry = [k for _, k in shards]

    def rows(ref, cc):
        return pl.ds(cc * (ref.shape[1] // 2), ref.shape[1] // 2)

    def sent(ins, outs, send_sems, recv_sems, w, j):
        x, y, c = _place()
        return _remote(ins[w].at[pl.ds(entry[w], 1), rows(ins[w], c)], outs[w].at[:, 2 * x + y, rows(ins[w], c)], send_sems, recv_sems,
                       7 * w + j, (*_other_chips(x, y)[j], c))

    def landed(ins, outs, send_sems, recv_sems, w, j, cc, to):
        x, y, c = _place()
        chip = _other_chips(x, y)[j]
        blk = outs[w].at[:, 2 * chip[0] + chip[1], rows(ins[w], cc)]
        return _remote(blk, blk, send_sems, recv_sems, 7 * w + (j if to is None else 3 + j), (x, y, c) if to is None else to)

    def own(ins, outs, send_sems, recv_sems, w):
        x, y, c = _place()
        return _remote(ins[w].at[pl.ds(entry[w], 1)], outs[w].at[:, 2 * x + y], send_sems, recv_sems, 7 * w + 6, (x, y, 1 - c))

    def start(ins, outs, send_sems, recv_sems):
        for j in range(3):
            for w in range(nw):
                sent(ins, outs, send_sems, recv_sems, w, j).start()
        for w in range(nw):
            own(ins, outs, send_sems, recv_sems, w).start()

    def finish(ins, outs, send_sems, recv_sems):
        x, y, c = _place()
        for j in range(3):
            for w in range(nw):
                landed(ins, outs, send_sems, recv_sems, w, j, c, None).wait_recv()
                landed(ins, outs, send_sems, recv_sems, w, j, c, (x, y, 1 - c)).start()
        for w in range(nw):
            own(ins, outs, send_sems, recv_sems, w).wait()
        for j in range(3):
            for w in range(nw):
                landed(ins, outs, send_sems, recv_sems, w, j, 1 - c, (x, y, c)).wait_recv()
        for j in range(3):
            for w in range(nw):
                sent(ins, outs, send_sems, recv_sems, w, j).wait_send()
                landed(ins, outs, send_sems, recv_sems, w, j, c, (x, y, 1 - c)).wait_send()

    return Exchange([s for s, _ in shards], [_sds((1, 4) + s.shape[1:], s.dtype) for s, _ in shards], 7 * nw, start, finish)


def copies_exchange(ins, out_shapes, n_sems, copies):
    def start(*refs):
        for cp in copies(*refs):
            cp.start()

    def finish(*refs):
        for cp in copies(*refs):
            cp.wait()

    return Exchange(ins, out_shapes, n_sems, start, finish)


def sibling_halves_exchange(gs):
    def copies(ins, outs, send_sems, recv_sems):
        x, y, c = _place()
        return [_remote(g.at[:, :, pl.ds((1 - c) * (g.shape[2] // 2), g.shape[2] // 2)], o, send_sems, recv_sems, w, (x, y, 1 - c))
                for w, (g, o) in enumerate(zip(ins, outs))]

    return copies_exchange(gs, [_sds(g.shape[:2] + (g.shape[2] // 2, g.shape[3]), g.dtype) for g in gs], len(gs), copies)


def chips_exchange(sbs):
    def copies(ins, outs, send_sems, recv_sems):
        x, y, c = _place()
        return [_remote(s.at[:, 2 * chip[0] + chip[1]], o.at[j], send_sems, recv_sems, 3 * w + j, (*chip, c))
                for j, chip in enumerate(_other_chips(x, y)) for w, (s, o) in enumerate(zip(ins, outs))]

    return copies_exchange(sbs, [_sds((3, s.shape[0]) + s.shape[2:], s.dtype) for s in sbs], 3 * len(sbs), copies)


def sibling_exchange(fs):
    def copies(ins, outs, send_sems, recv_sems):
        x, y, c = _place()
        return [_remote(f, o, send_sems, recv_sems, w, (x, y, 1 - c)) for w, (f, o) in enumerate(zip(ins, outs))]

    return copies_exchange(fs, [_sds(f.shape, f.dtype) for f in fs], len(fs), copies)


def _half_tile(a):
    return _tile(a, (256, 352, 224, 176, 160, 128, 64, 32, 16))


def chip_partial_sums(g, r1, c_arr, name):
    nl, _, a2, b = r1.shape
    ta = _half_tile(a2)
    per = a2 // ta

    def body(c_ref, g_ref, r_ref, o_ref):
        o_ref[...] = (g_ref[...] + r_ref[...]).astype(o_ref.dtype)

    blk = (None, None, ta, b)
    return pl.pallas_call(
        body, name=name,
        grid_spec=pltpu.PrefetchScalarGridSpec(
            num_scalar_prefetch=1, grid=(nl, 4, per),
            in_specs=[pl.BlockSpec(blk, lambda l, s, i, c: (l, s, c[0] * per + i, 0)), pl.BlockSpec(blk, lambda l, s, i, c: (l, s, i, 0))],
            out_specs=pl.BlockSpec(blk, lambda l, s, i, c: (l, s, i, 0))),
        out_shape=_sds(r1.shape, BF16), compiler_params=_cp("parallel", "parallel", "parallel"))(c_arr, g, r1)


def shard_total(g, r1, r2, cs_arr, name):
    nl, _, a2, b = r1.shape
    ta = _half_tile(a2)
    per = a2 // ta

    def body(cs_ref, g_ref, r1_ref, p0_ref, p1_ref, p2_ref, o_ref):
        o_ref[...] = (((g_ref[...] + r1_ref[...]) + p0_ref[...].astype(F32)) + p1_ref[...].astype(F32)) + p2_ref[...].astype(F32)

    blk4, blk3 = (None, None, ta, b), (None, ta, b)
    peer = lambda k: pl.BlockSpec((None, None, ta, b), lambda l, i, cs: (k, l, i, 0))
    return pl.pallas_call(
        body, name=name,
        grid_spec=pltpu.PrefetchScalarGridSpec(
            num_scalar_prefetch=1, grid=(nl, per),
            in_specs=[pl.BlockSpec(blk4, lambda l, i, cs: (l, cs[1], cs[0] * per + i, 0)),
                      pl.BlockSpec(blk4, lambda l, i, cs: (l, cs[1], i, 0)), peer(0), peer(1), peer(2)],
            out_specs=pl.BlockSpec(blk3, lambda l, i, cs: (l, i, 0))),
        out_shape=_sds((nl, a2, b), F32), compiler_params=_cp("parallel", "parallel"))(cs_arr, g, r1, r2, r2, r2)


def small_gather_exchange(v):
    r, n = v.shape

    def copy(ins, outs, send_sems, recv_sems, k, block, to, own=False):
        rows = outs[0].at[pl.ds((4 * block[0] + 2 * block[1] + block[2]) * r, r), :]
        return _remote(ins[0] if own else rows, rows, send_sems, recv_sems, k, to)

    def local(ins, outs, local_sem):
        x, y, c = _place()
        return pltpu.make_async_copy(ins[0], outs[0].at[pl.ds((4 * x + 2 * y + c) * r, r), :], local_sem.at[0])

    def start(ins, outs, send_sems, recv_sems, local_sem):
        x, y, c = _place()
        local(ins, outs, local_sem).start()
        copy(ins, outs, send_sems, recv_sems, 0, (x, y, c), (x, y, 1 - c), own=True).start()
        for j, chip in enumerate(_other_chips(x, y)):
            copy(ins, outs, send_sems, recv_sems, 1 + j, (x, y, c), (*chip, c), own=True).start()

    def finish(ins, outs, send_sems, recv_sems, local_sem):
        x, y, c = _place()
        chips = _other_chips(x, y)
        for j, chip in enumerate(chips):
            copy(ins, outs, send_sems, recv_sems, 1 + j, (*chip, c), (x, y, c)).wait_recv()
            copy(ins, outs, send_sems, recv_sems, 4 + j, (*chip, c), (x, y, 1 - c)).start()
        copy(ins, outs, send_sems, recv_sems, 0, (x, y, 1 - c), (x, y, c)).wait_recv()
        for j, chip in enumerate(chips):
            copy(ins, outs, send_sems, recv_sems, 4 + j, (*chip, 1 - c), (x, y, c)).wait_recv()
        copy(ins, outs, send_sems, recv_sems, 0, (x, y, c), (x, y, 1 - c), own=True).wait_send()
        for j, chip in enumerate(chips):
            copy(ins, outs, send_sems, recv_sems, 1 + j, (x, y, c), (*chip, c), own=True).wait_send()
            copy(ins, outs, send_sems, recv_sems, 4 + j, (*chip, c), (x, y, 1 - c)).wait_send()
        local(ins, outs, local_sem).wait()

    return Exchange([v], [_sds((8 * r, n), v.dtype)], [7, 7, 1], start, finish)


def sum_devices(v8, name):
    _, r, n = v8.shape
    tr = _tile(r, (88, 64, 32, 16, 8))

    def body(v_ref, o_ref):
        acc = v_ref[0]
        for d in range(1, 8):
            acc = acc + v_ref[d]
        o_ref[...] = acc

    return pl.pallas_call(
        body, name=name, grid=(r // tr,), in_specs=[pl.BlockSpec((8, tr, n), lambda i: (0, i, 0))],
        out_specs=pl.BlockSpec((tr, n), lambda i: (i, 0)), out_shape=_sds((r, n), F32), compiler_params=_cp("parallel"))(v8)


SHARDED = (("a_w_in", 2), ("a_w_out", 1), ("b_w_in", 2), ("b_w_out", 1), ("w_mem_kv", 1), ("w_gate_up", 2), ("w_down", 1))


def _usable(wg, axis):
    l, _, a, b = wg.shape
    return wg.reshape(l, 4 * a, b) if axis == 1 else wg


def _pack(arrs):
    parts = []
    for a in arrs:
        flat = a.reshape(-1)
        flat = jnp.pad(flat, (0, -flat.shape[0] % 1024))
        parts.append(flat.reshape(-1, 128))
    return jnp.concatenate(parts, axis=0)


def _unpack(buf, like):
    out, row = [], 0
    for a in like:
        size = 1
        for s in a.shape:
            size *= s
        rows = -(-size // 1024) * 8
        out.append(buf[row:row + rows].reshape(-1)[:size].reshape(a.shape))
        row += rows
    return out


def kernel(x, mem, mem_norm_g, mix_norm_g, ffn_norm_g, final_norm_g, a_w_in, a_sinks, a_w_out, b_w_in, b_w_s, b_bias_s, b_ln_g, b_ln_b, b_w_out, w_mem_kv, w_gate_up, w_down, loss_target, m_mem_norm_g, m_mix_norm_g, m_ffn_norm_g, m_final_norm_g, m_a_w_in, m_a_sinks, m_a_w_out, m_b_w_in, m_b_w_s, m_b_bias_s, m_b_ln_g, m_b_ln_b, m_b_w_out, m_w_mem_kv, m_w_gate_up, m_w_down, v_mem_norm_g, v_mix_norm_g, v_ffn_norm_g, v_final_norm_g, v_a_w_in, v_a_sinks, v_a_w_out, v_b_w_in, v_b_w_s, v_b_bias_s, v_b_ln_g, v_b_ln_b, v_b_w_out, v_w_mem_kv, v_w_gate_up, v_w_down):
    given = dict(locals())
    depth = mix_norm_g.shape[0]
    d = x.shape[-1]
    xi, yi, ci = _place()
    c_arr = jnp.stack([ci]).astype(jnp.int32)
    cs_arr = jnp.stack([ci, 2 * xi + yi]).astype(jnp.int32)

    axis_of = dict(SHARDED)
    own = {n: given[n].astype(BF16) for n, _ in SHARDED}
    MIXER, FFN = slice(0, 3), slice(3, 5)

    def layer_weights(l):
        mix = "a" if l % 2 == 0 else "b"
        return [(mix + "_w_in", l // 2), (mix + "_w_out", l // 2), ("w_mem_kv", l), ("w_gate_up", l), ("w_down", l)]

    def gather_of(l, part=slice(0, 5)):
        return gather_exchange([(own[n], k) for n, k in layer_weights(l)[part]])

    def usable(l, gathered, part=slice(0, 5)):
        return {n[2:] if n[0] in "ab" else n: (_usable(wg, axis_of[n]), 0) for (n, _), wg in zip(layer_weights(l)[part], gathered)}

    weights = {0: usable(0, run_exchange(gather_of(0, MIXER), "gather_weights"), MIXER)}

    h = x.reshape(-1, d)
    tgt = loss_target.reshape(-1, d)
    mem2 = mem.reshape(-1, d)
    row = lambda v: v.reshape(1, -1)

    mem_n = rmsnorm_fwd(mem2, row(mem_norm_g), "mem_norm")
    saved = []
    for i in range(depth):
        j = i // 2
        wl = weights[i]
        w_in, w_out = wl["w_in"], wl["w_out"]
        kv = matmul(mem_n, wl["w_mem_kv"], "nn", BF16, "mem_kv")
        if i % 2 == 0:
            sk = jnp.pad(jnp.broadcast_to(a_sinks[j][:, None], (Q_HEADS, 128)), ((0, 16 - Q_HEADS), (0, 0)))
            xn, proj = norm_matmul(h, row(mix_norm_g[i]), w_in, BF16, "a_in")
            cat, *gathered = mixer_a_fwd(proj, sk, kv, "mixer_a", rider=gather_of(0, FFN) if i == 0 else None)
            if gathered:
                wl.update(usable(0, gathered, FFN))
            extra = (sk,)
        else:
            bt = jnp.pad(b_bias_s[j].T, ((0, 0), (0, 128 - B_GROUPS)))
            lg = jnp.pad(b_ln_g[j], ((0, 8 - B_GROUPS), (0, 0)))
            lb = jnp.pad(b_ln_b[j], ((0, 8 - B_GROUPS), (0, 0)))
            xn, proj = norm_matmul(h, row(mix_norm_g[i]), w_in, BF16, "b_in")
            cat = mixer_b_fwd(proj, b_w_s[j], bt, lg, lb, kv, "mixer_b")
            extra = (b_w_s[j], bt, lg, lb)
        h_mid = matmul(cat, w_out, "nn", F32, "mix_out", res=h)
        more = i + 1 < depth
        hn, gu, act, *gathered = gate_up_fwd(h_mid, row(ffn_norm_g[i]), *wl["w_gate_up"], "gate_up",
                                             rider=gather_of(i + 1, FFN) if more else None)
        h_out, *gathered_mixer = matmul(act, wl["w_down"], "nn", F32, "down", res=h_mid, rider=gather_of(i + 1, MIXER)) if more \
            else [matmul(act, wl["w_down"], "nn", F32, "down", res=h_mid)]
        if more:
            weights[i + 1] = {**usable(i + 1, gathered, FFN), **usable(i + 1, gathered_mixer, MIXER)}
        saved.append((h, xn, proj, cat, h_mid, hn, gu, act, kv, extra))
        h = h_out

    loss_part, dh, d_final_g, dh16 = loss_head(h, row(final_norm_g), tgt, "loss_head")
    loss = lax.psum(loss_part[0, 0], ("x", "y", "c"))

    d_mix_g, d_ffn_g = [None] * depth, [None] * depth
    d_sinks, d_ws, d_bias, d_lg, d_lb = [], [], [], [], []
    d_mem_n = jnp.zeros(mem2.shape, F32)
    totals = [None] * depth
    pending = None
    for i in reversed(range(depth)):
        h_in, xn, proj, cat, h_mid, hn, gu, act, kv, extra = saved[i]
        wl = weights[i]
        dgu, *from_sibling = down_dx_swiglu_bwd(dh16, wl["w_down"], gu, "down_dx",
                                                rider=sibling_halves_exchange(pending) if pending else None)
        if pending:
            partial = [chip_partial_sums(g, r1, c_arr, "grads_chip_sum") for g, r1 in zip(pending, from_sibling)]
        dw_down = matmul(act, dh16, "tn", F32, "down_dw", tm=1408, tk=2048, out_planes=("rows", 4))
        dw_gate_up = matmul((hn, 0), dgu, "tn", F32, "gate_up_dw", tn=1408, tk=2048, out_planes=("cols", 4))
        ffn = [dw_gate_up[None], dw_down[None]] if i == 0 else []
        joined = lambda exs: None if not exs else exs[0] if len(exs) == 1 else both_exchanges(*exs)
        dh, d_ffn_g[i], dh16, *landed = dx_norm_bwd(
            dgu, wl["w_gate_up"], h_mid, row(ffn_norm_g[i]), dh, "gate_up_dx",
            rider=joined(([chips_exchange(partial[FFN])] if pending else []) + ([sibling_halves_exchange(ffn)] if ffn else [])))
        if pending:
            chips_ffn = landed[:2]
        if ffn:
            ffn_sibling = landed[-len(ffn):]
            ffn_partial = [chip_partial_sums(g, r1, c_arr, "grads_chip_sum") for g, r1 in zip(ffn, ffn_sibling)]
        dcat = matmul(dh16, wl["w_out"], "nt", F32, "mix_out_dx")
        dw_out = matmul(cat, dh16, "tn", F32, "mix_out_dw", tk=2048, out_planes=("rows", 4))
        mixer_rider = joined(([chips_exchange(partial[MIXER])] if pending else []) + ([chips_exchange(ffn_partial)] if ffn else []))
        if i % 2 == 0:
            dproj, dsk, dkv, *landed = mixer_a_bwd(proj, dcat, extra[0], kv, "mixer_a_bwd", rider=mixer_rider)
            d_sinks.insert(0, dsk[:Q_HEADS, 0])
            dw_in = matmul(dproj, xn, "tn", F32, "a_in_dw", out_planes=("rows", 4))
        else:
            dproj, dws, dbt, dlg, dlb, dkv, *landed = mixer_b_bwd(proj, dcat, *extra, kv, "mixer_b_bwd", rider=mixer_rider)
            d_ws.insert(0, dws)
            d_bias.insert(0, dbt[:, :B_GROUPS].T)
            d_lg.insert(0, dlg[:B_GROUPS])
            d_lb.insert(0, dlb[:B_GROUPS])
            dw_in = matmul(dproj, xn, "tn", F32, "b_in_dw", out_planes=("rows", 4))
        if pending:
            from_chips = landed[:3] + chips_ffn
            totals[i + 1] = [shard_total(g, r1, r2, cs_arr, "grads_shard_total") for g, r1, r2 in zip(pending, from_sibling, from_chips)]
        if ffn:
            ffn_totals = [shard_total(g, r1, r2, cs_arr, "grads_shard_total") for g, r1, r2 in zip(ffn, ffn_sibling, landed[-len(ffn):])]
        dw_kv = matmul(mem_n, dkv, "tn", F32, "mem_kv_dw", out_planes=("rows", 4))
        d_mem_n = matmul(dkv, wl["w_mem_kv"], "nt", F32, "mem_kv_dx", res=d_mem_n)
        dh, d_mix_g[i], dh16 = dx_norm_bwd(dproj, wl["w_in"], h_in, row(mix_norm_g[i]), dh, "in_dx")
        pending = [dw_in[None], dw_out[None], dw_kv[None]] + ([] if ffn else [dw_gate_up[None], dw_down[None]])
    grad_x = dh.reshape(x.shape)
    _, d_mem_g = rmsnorm_bwd(mem2, row(mem_norm_g), d_mem_n, jnp.zeros(mem2.shape, F32), "mem_norm_bwd")

    from_sibling = run_exchange(sibling_halves_exchange(pending), "grads_sibling_swap")
    partial = [chip_partial_sums(g, r1, c_arr, "grads_chip_sum") for g, r1 in zip(pending, from_sibling)]
    small = ("mem_norm_g", "mix_norm_g", "ffn_norm_g", "final_norm_g", "a_sinks", "b_w_s", "b_bias_s", "b_ln_g", "b_ln_b")
    small_g = [d_mem_g[0], jnp.concatenate(d_mix_g, axis=0), jnp.concatenate(d_ffn_g, axis=0), d_final_g[0],
               jnp.stack(d_sinks), jnp.stack(d_ws), jnp.stack(d_bias), jnp.stack(d_lg), jnp.stack(d_lb)]
    packed = _pack(small_g)
    totals[0] = [None] * 3 + ffn_totals
    out = {}

    def stacked(n):
        parts = [None] * given[n].shape[0]
        for l in range(depth):
            for (name_l, k), tot in zip(layer_weights(l), totals[l]):
                if name_l == n:
                    parts[k] = tot
        return jnp.concatenate(parts, axis=0)

    def update(n, g_mine, g_theirs, rider=None):
        flip = (lambda a: jnp.swapaxes(a, 1, 2)) if n in ("a_w_in", "b_w_in") else (lambda a: a)
        shape = flip(given[n]).shape
        two_d = lambda a: a.reshape(-1, shape[-1])
        res = adamw_halves(two_d(flip(given[n])), two_d(g_mine), two_d(g_theirs), two_d(flip(given["m_" + n])),
                           two_d(flip(given["v_" + n])), c_arr, shape[1], "adamw", rider=rider)
        out[n] = tuple(flip(r.reshape(shape)) for r in res[:4])
        return res[4:]

    ffn_names = ("w_gate_up", "w_down")
    mine = [stacked(n) for n in ffn_names]
    theirs = run_exchange(sibling_exchange(mine), "grads_sibling_totals")
    *from_chips, gathered_small = update(ffn_names[0], mine[0], theirs[0],
                                         rider=both_exchanges(chips_exchange(partial), small_gather_exchange(packed)))
    update(ffn_names[1], mine[1], theirs[1])
    totals[0][:3] = [shard_total(g, r1, r2, cs_arr, "grads_shard_total") for g, r1, r2 in zip(pending, from_sibling, from_chips)]
    mixer_names = [n for n, _ in SHARDED if n not in ffn_names]
    mine = [stacked(n) for n in mixer_names]
    theirs = run_exchange(sibling_exchange(mine), "grads_sibling_totals")
    for n, g_mine, g_theirs in zip(mixer_names, mine, theirs):
        update(n, g_mine, g_theirs)

    g_small = sum_devices(gathered_small.reshape(8, *packed.shape), "small_sum")
    like = [given[n] for n in small]
    delta_s, new_m_s, new_v_s = adamw(_pack(like), g_small, _pack([given["m_" + n] for n in small]),
                                      _pack([given["v_" + n] for n in small]), "adamw_small")
    for n, g, dl, nm_, nv_ in zip(small, _unpack(g_small, like), _unpack(delta_s, like), _unpack(new_m_s, like), _unpack(new_v_s, like)):
        out[n] = (g, dl, nm_, nv_)

    order = ("mem_norm_g", "mix_norm_g", "ffn_norm_g", "final_norm_g", "a_w_in", "a_sinks", "a_w_out", "b_w_in", "b_w_s",
             "b_bias_s", "b_ln_g", "b_ln_b", "b_w_out", "w_mem_kv", "w_gate_up", "w_down")
    return (loss, grad_x, *[out[n][0] for n in order], *[out[n][1] for n in order],
            *[out[n][2] for n in order], *[out[n][3] for n in order])
```

```python
import jax
import jax.numpy as jnp
from jax import lax
from jax.experimental import pallas as pl
from jax.experimental.pallas import tpu as pltpu

F32, BF16 = jnp.float32, jnp.bfloat16
EPS = 1e-6
HEAD_DIM = 64
Q_HEADS, KV_HEADS, GROUP = 12, 2, 6
WINDOW = 128
MEM_HEADS = 4
B_GROUPS = 6
Q_W, KV_W, MEM_W, B_W = 768, 128, 256, 768
SCALE = HEAD_DIM ** -0.5
NEG = -1e30
ADAM_LR, ADAM_B1, ADAM_B2, ADAM_EPS, ADAM_WD, ADAM_STEP = 0.001, 0.9, 0.999, 1e-08, 0.01, 10
V7X_VMEM_LIMIT_BYTES = 48 * 1024 * 1024
MESH = pl.DeviceIdType.MESH
HBM_SPEC = pl.BlockSpec(memory_space=pltpu.HBM)
VMEM_SPEC = pl.BlockSpec(memory_space=pltpu.VMEM)


def _cp(*sem):
    return pltpu.CompilerParams(dimension_semantics=sem or None, vmem_limit_bytes=V7X_VMEM_LIMIT_BYTES)


def _tile(n, cands):
    for t in cands:
        if n % t == 0:
            return t
    return n


def _sds(shape, dtype):
    return jax.ShapeDtypeStruct(tuple(shape), dtype)


def _dot(a, b, ca, cb):
    return lax.dot_general(a, b, (((ca,), (cb,)), ((), ())), preferred_element_type=F32)


def _rms(x, g):
    return x * lax.rsqrt(jnp.mean(x * x, axis=-1, keepdims=True) + EPS) * g


def rmsnorm_fwd(h, g, name):
    t, d = h.shape
    tm = _tile(t, (512, 256, 128))

    def body(h_ref, g_ref, o_ref):
        o_ref[...] = _rms(h_ref[...], g_ref[...]).astype(o_ref.dtype)

    return pl.pallas_call(
        body, name=name, grid=(t // tm,),
        in_specs=[pl.BlockSpec((tm, d), lambda i: (i, 0)), pl.BlockSpec((1, d), lambda i: (0, 0))],
        out_specs=pl.BlockSpec((tm, d), lambda i: (i, 0)),
        out_shape=_sds((t, d), BF16), compiler_params=_cp("parallel"))(h, g)


def rmsnorm_bwd(h, g, dxn, dres, name):
    t, d = h.shape
    tm = _tile(t, (512, 256, 128))

    def body(h_ref, g_ref, dxn_ref, dres_ref, dh_ref, dg_ref):
        _, vjp = jax.vjp(_rms, h_ref[...], g_ref[...])
        dh, dg = vjp(dxn_ref[...].astype(F32))
        dh_ref[...] = dres_ref[...] + dh

        @pl.when(pl.program_id(0) == 0)
        def _():
            dg_ref[...] = jnp.zeros_like(dg_ref)

        dg_ref[...] += dg

    row = pl.BlockSpec((tm, d), lambda i: (i, 0))
    vec = pl.BlockSpec((1, d), lambda i: (0, 0))
    return pl.pallas_call(
        body, name=name, grid=(t // tm,), in_specs=[row, vec, row, row], out_specs=[row, vec],
        out_shape=[_sds((t, d), F32), _sds((1, d), F32)], compiler_params=_cp("arbitrary"))(h, g, dxn, dres)


def loss_head(h, g, tgt, name):
    t, d = h.shape
    tm = _tile(t, (512, 256, 128))

    def body(h_ref, g_ref, t_ref, l_ref, dh_ref, dg_ref, dh16_ref):
        y, vjp = jax.vjp(_rms, h_ref[...], g_ref[...])
        err = y - t_ref[...]
        dh, dg = vjp(err * (1.0 / d))
        dh_ref[...] = dh
        dh16_ref[...] = dh.astype(BF16)
        part = 0.5 * jnp.sum(jnp.mean(err * err, axis=-1, keepdims=True), axis=0, keepdims=True)

        @pl.when(pl.program_id(0) == 0)
        def _():
            dg_ref[...] = jnp.zeros_like(dg_ref)
            l_ref[...] = jnp.zeros_like(l_ref)

        dg_ref[...] += dg
        l_ref[...] += part

    row = pl.BlockSpec((tm, d), lambda i: (i, 0))
    vec = pl.BlockSpec((1, d), lambda i: (0, 0))
    one = pl.BlockSpec((1, 1), lambda i: (0, 0))
    return pl.pallas_call(
        body, name=name, grid=(t // tm,), in_specs=[row, vec, row], out_specs=[one, row, vec, row],
        out_shape=[_sds((1, 1), F32), _sds((t, d), F32), _sds((1, d), F32), _sds((t, d), BF16)],
        compiler_params=_cp("arbitrary"))(h, g, tgt)


def _logical(op):
    arr, lead = op if isinstance(op, tuple) else (op, None)
    planes = arr.shape[-3] if arr.ndim - (lead is not None) == 3 else 1
    return arr, lead, arr.shape[-2], arr.shape[-1], planes


def _spec(op, rows_t, cols_t, row_of, col_of):
    arr, lead, _, cols, _ = _logical(op)
    per = cols // cols_t
    lead = () if lead is None else (lead,)
    if arr.ndim - len(lead) == 2:
        return pl.BlockSpec((None,) * len(lead) + (rows_t, cols_t), lambda *g: lead + (row_of(*g), col_of(*g)))
    return pl.BlockSpec((None,) * len(lead) + (None, rows_t, cols_t),
                        lambda *g: lead + (col_of(*g) // per, row_of(*g), col_of(*g) % per))


def _arr(op):
    return op[0] if isinstance(op, tuple) else op


def _resident_whole(w, d):
    wa, layer = w
    planes, per = wa.shape[-3], wa.shape[-1]

    def fill(w_ref, whole_ref):
        for s in range(planes):
            whole_ref[:, s * per:(s + 1) * per] = w_ref[s]

    return (pl.BlockSpec((None, planes, d, per), lambda i: (layer, 0, 0, 0), pipeline_mode=pl.Buffered(1)),
            pltpu.VMEM((d, planes * per), wa.dtype), fill)


def norm_matmul(h, g, w, out_dtype, name):
    t, d = h.shape
    w_spec, whole, fill = _resident_whole(w, d)
    n = whole.shape[1]
    tm = _tile(t, (512, 256, 128))

    def body(h_ref, g_ref, w_ref, xn_ref, o_ref, whole_ref):
        @pl.when(pl.program_id(0) == 0)
        def _():
            fill(w_ref, whole_ref)

        xn = _rms(h_ref[...], g_ref[...]).astype(BF16)
        xn_ref[...] = xn
        o_ref[...] = _dot(xn, whole_ref[...], 1, 0).astype(o_ref.dtype)

    return pl.pallas_call(
        body, name=name, grid=(t // tm,),
        in_specs=[pl.BlockSpec((tm, d), lambda i: (i, 0)), pl.BlockSpec((1, d), lambda i: (0, 0)), w_spec],
        out_specs=[pl.BlockSpec((tm, d), lambda i: (i, 0)), pl.BlockSpec((tm, n), lambda i: (i, 0))],
        out_shape=[_sds((t, d), BF16), _sds((t, n), out_dtype)], scratch_shapes=[whole],
        compiler_params=_cp("arbitrary"))(h, g, w[0])


def dx_norm_bwd(dy, w, h, g, dres, name, rider=None):
    t, d = h.shape
    dy_arr, dy_lead, _, kc, kp = _logical(dy)
    w_arr, w_lead, _, wc, wp = _logical(w)
    assert kc * kp == wc * wp and dy_lead is None, name
    chunk = min(kc, wc)
    tm = _tile(t, (512, 256, 128))

    def piece(ref, planes, cols, q):
        off = q * chunk % cols
        return ref[q * chunk // cols, :, off:off + chunk] if planes > 1 else ref[:, off:off + chunk]

    narrow = wp > 1 and wc % 128 != 0
    if narrow:
        assert kp == 1, name
        w_whole_spec, whole, fill = _resident_whole(w, d)

    def body(dy_ref, w_ref, h_ref, g_ref, dres_ref, dh_ref, dg_ref, dh16_ref, *whole_ref):
        if narrow:
            @pl.when(pl.program_id(0) == 0)
            def _():
                fill(w_ref, whole_ref[0])

            dxn = _dot(dy_ref[...].astype(BF16), whole_ref[0][...], 1, 1)
        else:
            dxn = None
            for q in range(kc * kp // chunk):
                p = _dot(piece(dy_ref, kp, kc, q).astype(BF16), piece(w_ref, wp, wc, q), 1, 1)
                dxn = p if dxn is None else dxn + p
        _, vjp = jax.vjp(_rms, h_ref[...], g_ref[...])
        dh, dg = vjp(dxn)
        dh = dres_ref[...] + dh
        dh_ref[...] = dh
        dh16_ref[...] = dh.astype(BF16)

        @pl.when(pl.program_id(0) == 0)
        def _():
            dg_ref[...] = jnp.zeros_like(dg_ref)

        dg_ref[...] += dg

    w_lead = () if w_lead is None else (w_lead,)
    w_block = ((wp,) if wp > 1 else ()) + (d, wc)
    w_spec = pl.BlockSpec((None,) * len(w_lead) + w_block, lambda i: w_lead + (0,) * len(w_block), pipeline_mode=pl.Buffered(1))
    if narrow:
        w_spec = w_whole_spec
    dy_spec = pl.BlockSpec((kp, tm, kc), lambda i: (0, i, 0)) if kp > 1 else pl.BlockSpec((tm, kc), lambda i: (i, 0))
    row = pl.BlockSpec((tm, d), lambda i: (i, 0))
    vec = pl.BlockSpec((1, d), lambda i: (0, 0))
    grid = (t // tm,)
    body, r_ops, r_in, r_shapes, r_out, r_scratch = with_rider(body, 5, 3, grid, rider)
    return pl.pallas_call(
        body, name=name, grid=grid, in_specs=[dy_spec, w_spec, row, vec, row] + r_in,
        out_specs=[row, vec, row] + r_out, out_shape=[_sds((t, d), F32), _sds((1, d), F32), _sds((t, d), BF16)] + r_shapes,
        scratch_shapes=([whole] if narrow else []) + r_scratch,
        compiler_params=_cp("arbitrary"))(dy_arr, w_arr, h, g, dres, *r_ops)


def matmul(a, b, mode, out_dtype, name, res=None, tm=None, tn=1792, tk=2816, out_planes=None, out_into=None, rider=None):
    _, _, ar, ac, ap = _logical(a)
    _, _, br, bc, bp = _logical(b)
    if mode == "nn":
        m, ka, kb, n = ar, ac * ap, br, bc * bp
        n_plane, ka_plane, kb_plane = bc, ac, br
    elif mode == "nt":
        m, ka, n, kb = ar, ac * ap, br, bc * bp
        n_plane, ka_plane, kb_plane = br, ac, bc
    else:
        ka, m, kb, n = ar, ac * ap, br, bc * bp
        n_plane, ka_plane, kb_plane = bc, ar, br
    m_plane = ac if mode == "tn" else ar
    assert ka == kb, name
    k = ka
    kind, planes = out_planes or ("cols", 1)
    narrow = kind == "cols" and (n // planes) % 128 != 0
    if kind == "cols" and not narrow:
        n_plane = min(n_plane, n // planes)
    if narrow:
        tn = n
    tm = _tile(m_plane, ((1024, 1408, 512, 256, 128) if mode == "tn" else (512, 256, 128)) if tm is None else (tm, 1024, 512, 256, 128))
    if kind == "rows" and tm % (m // planes):
        tm = m_plane
    tn = _tile(n_plane, (tn, 1792, 1408, 1280, 1024, 896, 640, 512, 256, 128))
    tk = _tile(min(ka_plane, kb_plane), (tk, 2816, 1792, 1408, 1280, 1024, 512, 256, 128))
    nk = k // tk
    row_i, col_j, red = (lambda i, j, kk: i), (lambda i, j, kk: j), (lambda i, j, kk: kk)
    if mode == "nn":
        a_spec, b_spec, ca, cb = _spec(a, tm, tk, row_i, red), _spec(b, tk, tn, red, col_j), 1, 0
    elif mode == "nt":
        a_spec, b_spec, ca, cb = _spec(a, tm, tk, row_i, red), _spec(b, tn, tk, col_j, red), 1, 1
    else:
        a_spec, b_spec, ca, cb = _spec(a, tk, tm, red, row_i), _spec(b, tk, tn, red, col_j), 0, 0
    lead = () if out_into is None else (out_into[1],)
    if planes == 1:
        o_shape, o_block = (m, n), (tm, tn)
        o_index = lambda i, j, kk: lead + (i, j)
    elif narrow:
        o_shape, o_block = (planes, m, n // planes), (planes, tm, n // planes)
        o_index = lambda i, j, kk: lead + (0, i, 0)
    elif kind == "cols":
        per = n // planes // tn
        o_shape, o_block = (planes, m, n // planes), (None, tm, tn)
        o_index = lambda i, j, kk: lead + (j // per, i, j % per)
    else:
        o_shape, o_block = (planes, m // planes, n), (tm // (m // planes), m // planes, tn)
        o_index = lambda i, j, kk: lead + (i, 0, j)
    o_spec = pl.BlockSpec((None,) * len(lead) + o_block, o_index)
    if out_into is not None:
        assert out_into[0].shape[1:] == o_shape and out_into[0].dtype == out_dtype, name
        o_shape = out_into[0].shape
    has_res = res is not None
    n_in = 2 + has_res + (out_into is not None)

    def put(o_ref, v):
        if narrow:
            for s in range(planes):
                o_ref[s] = v[:, s * (n // planes):(s + 1) * (n // planes)].astype(o_ref.dtype)
        else:
            o_ref[...] = v.astype(o_ref.dtype).reshape(o_ref.shape)

    def body(*refs):
        a_ref, b_ref = refs[:2]
        rest = refs[2:2 + has_res] + refs[n_in:]
        o_ref = rest[1] if has_res else rest[0]
        p = _dot(a_ref[...].astype(BF16), b_ref[...].astype(BF16), ca, cb)
        if nk == 1:
            if has_res:
                p = p + rest[0][...]
            put(o_ref, p)
        else:
            acc_ref = rest[-1]
            kk = pl.program_id(2)

            @pl.when(kk == 0)
            def _():
                acc_ref[...] = p

            @pl.when(kk > 0)
            def _():
                acc_ref[...] += p

            @pl.when(kk == nk - 1)
            def _():
                r = acc_ref[...]
                if has_res:
                    r = r + rest[0][...]
                put(o_ref, r)

    operands = [_arr(a), _arr(b)] + ([res] if has_res else []) + ([out_into[0]] if out_into is not None else [])
    grid = (m // tm, n // tn, nk)
    body, r_ops, r_in, r_shapes, r_out, r_scratch = with_rider(body, n_in, 1, grid, rider)
    out = pl.pallas_call(
        body, name=name, grid=grid,
        in_specs=[a_spec, b_spec] + ([pl.BlockSpec((tm, tn), lambda i, j, kk: (i, j))] if has_res else [])
        + ([pl.BlockSpec(memory_space=pl.ANY)] if out_into is not None else []) + r_in,
        out_specs=[o_spec] + r_out, out_shape=[_sds(o_shape, out_dtype)] + r_shapes,
        input_output_aliases={n_in - 1: 0} if out_into is not None else {},
        scratch_shapes=([pltpu.VMEM((tm, tn), F32)] if nk > 1 else []) + r_scratch,
        compiler_params=_cp(*(("arbitrary",) * 3 if rider else ("parallel", "parallel", "arbitrary"))))(*operands, *r_ops)
    return out if rider else out[0]


def gate_up_fwd(h, g, w, layer, name, rider=None):
    t, d = h.shape
    half = w.shape[-1]
    tm = _tile(t, (512, 256, 128))

    def body(h_ref, g_ref, wg_ref, wu_ref, hn_ref, gu_ref, act_ref):
        a = _rms(h_ref[...], g_ref[...]).astype(BF16)
        hn_ref[...] = a
        gate, up = _dot(a, wg_ref[...], 1, 0), _dot(a, wu_ref[...], 1, 0)
        sig = 1.0 / (1.0 + jnp.exp(-gate))
        silu = gate * sig
        gu_ref[0] = (up * (sig + silu * (1.0 - sig))).astype(gu_ref.dtype)
        gu_ref[1] = silu.astype(gu_ref.dtype)
        act_ref[...] = (silu * up).astype(act_ref.dtype)

    grid = (2, t // tm)
    body, r_ops, r_in, r_shapes, r_out, r_scratch = with_rider(body, 4, 3, grid, rider)
    return pl.pallas_call(
        body, name=name, grid=grid,
        in_specs=[pl.BlockSpec((tm, d), lambda j, i: (i, 0)), pl.BlockSpec((1, d), lambda j, i: (0, 0)),
                  pl.BlockSpec((None, None, d, half), lambda j, i: (layer, j, 0, 0)),
                  pl.BlockSpec((None, None, d, half), lambda j, i: (layer, 2 + j, 0, 0))] + r_in,
        out_specs=[pl.BlockSpec((None, tm, d), lambda j, i: (j, i, 0)), pl.BlockSpec((2, tm, half), lambda j, i: (0, i, j)),
                   pl.BlockSpec((tm, half), lambda j, i: (i, j))] + r_out,
        out_shape=[_sds((2, t, d), BF16), _sds((2, t, 2 * half), BF16), _sds((t, 2 * half), BF16)] + r_shapes,
        scratch_shapes=r_scratch, compiler_params=_cp("arbitrary", "arbitrary"))(h, g, w, w, *r_ops)


def down_dx_swiglu_bwd(dh, wd, gu, name, rider=None):
    t, d = dh.shape
    w, layer = wd
    f = w.shape[-2]
    tm = _tile(t, (512, 256, 128))
    tn = _tile(f, (1408, 512, 256, 128))

    def body(dh_ref, w_ref, gu_ref, o_ref):
        dact = _dot(dh_ref[...].astype(BF16), w_ref[...], 1, 1)
        o_ref[0] = (dact * gu_ref[0].astype(F32)).astype(o_ref.dtype)
        o_ref[1] = (dact * gu_ref[1].astype(F32)).astype(o_ref.dtype)

    planes = pl.BlockSpec((2, tm, tn), lambda j, i: (0, i, j))
    grid = (f // tn, t // tm)
    body, r_ops, r_in, r_shapes, r_out, r_scratch = with_rider(body, 3, 1, grid, rider)
    return pl.pallas_call(
        body, name=name, grid=grid,
        in_specs=[pl.BlockSpec((tm, d), lambda j, i: (i, 0)), pl.BlockSpec((None, tn, d), lambda j, i: (layer, j, 0)), planes] + r_in,
        out_specs=[planes] + r_out, out_shape=[_sds((2, t, f), BF16)] + r_shapes, scratch_shapes=r_scratch,
        compiler_params=_cp("arbitrary", "arbitrary"))(dh, w, gu, *r_ops)


def _softmax_over_keys(s, sink=None):
    m = s.max(axis=0, keepdims=True)
    if sink is not None:
        m = jnp.maximum(m, sink)
    m = lax.stop_gradient(m)
    e = jnp.exp(s - m)
    den = e.sum(axis=0, keepdims=True)
    if sink is not None:
        den = den + jnp.exp(sink - m)
    return e * (1.0 / den)


def _low_lanes():
    return lax.broadcasted_iota(jnp.int32, (1, 128), 1) < HEAD_DIM


def _stack_heads(slabs):
    low = _low_lanes()
    return jnp.concatenate([p for s in slabs for p in (jnp.where(low, s, 0.0), jnp.where(low, 0.0, s))], axis=0)


def _unstack_heads(o, n_slabs):
    low = _low_lanes()
    return [jnp.where(low, o[2 * j * WINDOW:(2 * j + 1) * WINDOW], o[(2 * j + 1) * WINDOW:(2 * j + 2) * WINDOW])
            for j in range(n_slabs)]


def _swa_group(q_slabs, k_both, v_both, sinks, mask):
    qs = _stack_heads(q_slabs).astype(BF16)
    s = jnp.where(mask, _dot(k_both.astype(BF16), qs, 1, 1) * SCALE, NEG)
    sink = jnp.concatenate([jnp.broadcast_to(v, (1, WINDOW)) for v in sinks], axis=1)
    return _unstack_heads(_dot(_softmax_over_keys(s, sink).astype(BF16), v_both.astype(BF16), 0, 0), len(q_slabs))


def _mem_pair(q_slab, k_slab, v_slab):
    s = _dot(k_slab.astype(BF16), _stack_heads([q_slab]).astype(BF16), 1, 1) * SCALE
    return _unstack_heads(_dot(_softmax_over_keys(s).astype(BF16), v_slab.astype(BF16), 0, 0), 1)[0]


def _gelu(x):
    return 0.5 * x * (1.0 + jnp.tanh(0.7978845608028654 * (x + 0.044715 * (x * x * x))))


def _gmlp_group(zu, zv, w, bcol, lg, lb, tri):
    u, v = _gelu(zu), _gelu(zv)
    mu = jnp.mean(v, axis=-1, keepdims=True)
    var = jnp.mean(jnp.square(v - mu), axis=-1, keepdims=True)
    vn = (v - mu) * lax.rsqrt(var + EPS) * lg + lb
    sv = _dot(jnp.where(tri, w, 0.0).astype(BF16), vn.astype(BF16), 1, 0) + bcol
    return u * sv


def _cols(x, width):
    return [x[:, j * width:(j + 1) * width] for j in range(x.shape[1] // width)]


def _swa_mask(has_prev):
    qi = lax.broadcasted_iota(jnp.int32, (2 * WINDOW, GROUP * WINDOW), 1) & (WINDOW - 1)
    kj = lax.broadcasted_iota(jnp.int32, (2 * WINDOW, GROUP * WINDOW), 0)
    in_prev = jnp.logical_and(jnp.logical_and(kj < WINDOW, kj > qi), has_prev)
    return jnp.logical_or(in_prev, jnp.logical_and(kj >= WINDOW, kj - WINDOW <= qi))


def _mix_a(q_slabs, k_boths, v_boths, sinks, qm_slabs, km_slabs, vm_slabs, mask):
    per = GROUP // 2
    outs = []
    for g in range(KV_HEADS):
        outs += _swa_group(q_slabs[per * g:per * (g + 1)], k_boths[g], v_boths[g], sinks[GROUP * g:GROUP * (g + 1)], mask)
    return outs + [_mem_pair(qm_slabs[j], km_slabs[j], vm_slabs[j]) for j in range(MEM_HEADS // 2)]


def _in_both_halves(prev, cur):
    cat = jnp.concatenate([prev, cur], axis=0)
    rolled = pltpu.roll(cat, HEAD_DIM, axis=1)
    low = _low_lanes()
    return [jnp.where(low, cat, rolled), jnp.where(low, rolled, cat)]


def _from_both_halves(d_boths):
    t = [d + pltpu.roll(d, HEAD_DIM, axis=1) for d in d_boths]
    return jnp.where(_low_lanes(), t[0], t[1])


def _mix_a_specs(nm, blk):
    prev = lambda n: jnp.maximum(blk(n) - 1, 0)
    return [pl.BlockSpec((WINDOW, Q_W), lambda n: (blk(n), 0)),
            pl.BlockSpec((WINDOW, KV_W), lambda n: (prev(n), Q_W // KV_W)),
            pl.BlockSpec((WINDOW, KV_W), lambda n: (blk(n), Q_W // KV_W)),
            pl.BlockSpec((WINDOW, KV_W), lambda n: (prev(n), Q_W // KV_W + 1)),
            pl.BlockSpec((WINDOW, KV_W), lambda n: (blk(n), Q_W // KV_W + 1)),
            pl.BlockSpec((WINDOW, MEM_W), lambda n: (blk(n), (Q_W + 2 * KV_W) // MEM_W)),
            pl.BlockSpec((16, 128), lambda n: (0, 0)),
            pl.BlockSpec((nm, MEM_W), lambda n: (0, 0)),
            pl.BlockSpec((nm, MEM_W), lambda n: (0, 1))]


def _mix_a_args(refs):
    q, kp, kc, vp, vc, qm, sk, km, vm = [r[...].astype(F32) for r in refs]
    return (_cols(q, 128), _in_both_halves(kp, kc), _in_both_halves(vp, vc), [sk[h:h + 1, 0:1] for h in range(Q_HEADS)],
            _cols(qm, 128), _cols(km, 128), _cols(vm, 128))


def mixer_a_fwd(proj, sk, kv, name, rider=None):
    t, nm = proj.shape[0], kv.shape[0]

    def body(*refs):
        o_ref = refs[-1]
        slabs = _mix_a(*_mix_a_args(refs[:-1]), _swa_mask(pl.program_id(0) > 0))
        o_ref[...] = jnp.concatenate(slabs, axis=1).astype(o_ref.dtype)

    grid = (t // WINDOW,)
    body, r_ops, r_in, r_shapes, r_out, r_scratch = with_rider(body, 9, 1, grid, rider)
    return pl.pallas_call(
        body, name=name, grid=grid, in_specs=_mix_a_specs(nm, lambda n: n) + r_in,
        out_specs=[pl.BlockSpec((WINDOW, Q_W + MEM_W), lambda n: (n, 0))] + r_out,
        out_shape=[_sds((t, Q_W + MEM_W), BF16)] + r_shapes, scratch_shapes=r_scratch,
        compiler_params=_cp("arbitrary"))(proj, proj, proj, proj, proj, proj, sk, kv, kv, *r_ops)


def _onehot_rows(vals, shape):
    rows = lax.broadcasted_iota(jnp.int32, shape, 0)
    out = jnp.zeros(shape, F32)
    for h, v in enumerate(vals):
        out = out + jnp.where(rows == h, jnp.broadcast_to(v, shape), 0.0)
    return out


def mixer_a_bwd(proj, dcat, sk, kv, name, rider=None):
    t, nm = proj.shape[0], kv.shape[0]
    nb = t // WINDOW
    blk = lambda i: nb - 1 - i

    def body(*refs):
        dcat_ref, dproj_ref, dsk_ref, dkv_ref, carry_ref = refs[9:]
        i = pl.program_id(0)

        @pl.when(i == 0)
        def _():
            carry_ref[...] = jnp.zeros_like(carry_ref)
            dsk_ref[...] = jnp.zeros_like(dsk_ref)
            dkv_ref[...] = jnp.zeros_like(dkv_ref)

        mask = _swa_mask(blk(i) > 0)
        _, vjp = jax.vjp(lambda *a: _mix_a(*a, mask), *_mix_a_args(refs[:9]))
        dqs, dk_boths, dv_boths, dsinks, dqms, dkms, dvms = vjp(_cols(dcat_ref[...].astype(F32), 128))
        dkv = jnp.concatenate([_from_both_halves(dk_boths), _from_both_halves(dv_boths)], axis=1)
        dkv_cur = dkv[WINDOW:] + carry_ref[...]
        carry_ref[...] = dkv[:WINDOW]
        dproj_ref[...] = jnp.concatenate(dqs + [dkv_cur] + dqms, axis=1).astype(dproj_ref.dtype)
        dsk_ref[...] += _onehot_rows(dsinks, (16, 128))
        dkv_ref[...] += jnp.concatenate(dkms + dvms, axis=1)

    width = Q_W + 2 * KV_W + MEM_W
    body, r_ops, r_in, r_shapes, r_out, r_scratch = with_rider(body, 10, 3, (nb,), rider)
    return pl.pallas_call(
        body, name=name, grid=(nb,),
        in_specs=_mix_a_specs(nm, blk) + [pl.BlockSpec((WINDOW, Q_W + MEM_W), lambda i: (blk(i), 0))] + r_in,
        out_specs=[pl.BlockSpec((WINDOW, width), lambda i: (blk(i), 0)), pl.BlockSpec((16, 128), lambda i: (0, 0)),
                   pl.BlockSpec((nm, 2 * MEM_W), lambda i: (0, 0))] + r_out,
        out_shape=[_sds((t, width), BF16), _sds((16, 128), F32), _sds((nm, 2 * MEM_W), F32)] + r_shapes,
        scratch_shapes=[pltpu.VMEM((WINDOW, 2 * KV_W), F32)] + r_scratch,
        compiler_params=_cp("arbitrary"))(proj, proj, proj, proj, proj, proj, sk, kv, kv, dcat, *r_ops)


def _mix_b(zus, zvs, ws, bcols, lgs, lbs, qms, kms, vms, tri):
    outs = [_gmlp_group(zus[g], zvs[g], ws[g], bcols[g], lgs[g], lbs[g], tri) for g in range(B_GROUPS)]
    return outs + [_mem_pair(qms[j], kms[j], vms[j]) for j in range(MEM_HEADS // 2)]


def _mix_b_specs(nm):
    return [pl.BlockSpec((WINDOW, 2 * B_W), lambda n: (n, 0)),
            pl.BlockSpec((WINDOW, MEM_W), lambda n: (n, 2 * B_W // MEM_W)),
            pl.BlockSpec((B_GROUPS, WINDOW, WINDOW), lambda n: (0, 0, 0)),
            pl.BlockSpec((WINDOW, 128), lambda n: (0, 0)),
            pl.BlockSpec((8, 128), lambda n: (0, 0)),
            pl.BlockSpec((8, 128), lambda n: (0, 0)),
            pl.BlockSpec((nm, MEM_W), lambda n: (0, 0)),
            pl.BlockSpec((nm, MEM_W), lambda n: (0, 1))]


def _mix_b_args(refs):
    z, qm, ws, bt, lg, lb, km, vm = [r[...].astype(F32) for r in refs]
    zs = _cols(z, 128)
    return (zs[:B_GROUPS], zs[B_GROUPS:], [ws[g] for g in range(B_GROUPS)], [bt[:, g:g + 1] for g in range(B_GROUPS)],
            [lg[g:g + 1, :] for g in range(B_GROUPS)], [lb[g:g + 1, :] for g in range(B_GROUPS)],
            _cols(qm, 128), _cols(km, 128), _cols(vm, 128))


def _tri():
    return lax.broadcasted_iota(jnp.int32, (WINDOW, WINDOW), 0) >= lax.broadcasted_iota(jnp.int32, (WINDOW, WINDOW), 1)


def mixer_b_fwd(proj, ws, bt, lg, lb, kv, name):
    t, nm = proj.shape[0], kv.shape[0]

    def body(*refs):
        o_ref = refs[-1]
        o_ref[...] = jnp.concatenate(_mix_b(*_mix_b_args(refs[:-1]), _tri()), axis=1).astype(o_ref.dtype)

    return pl.pallas_call(
        body, name=name, grid=(t // WINDOW,), in_specs=_mix_b_specs(nm),
        out_specs=pl.BlockSpec((WINDOW, B_W + MEM_W), lambda n: (n, 0)),
        out_shape=_sds((t, B_W + MEM_W), BF16), compiler_params=_cp("parallel"))(proj, proj, ws, bt, lg, lb, kv, kv)


def mixer_b_bwd(proj, dcat, ws, bt, lg, lb, kv, name, rider=None):
    t, nm = proj.shape[0], kv.shape[0]

    def body(*refs):
        dcat_ref, dproj_ref, dws_ref, dbt_ref, dlg_ref, dlb_ref, dkv_ref = refs[8:]

        @pl.when(pl.program_id(0) == 0)
        def _():
            for r in (dws_ref, dbt_ref, dlg_ref, dlb_ref, dkv_ref):
                r[...] = jnp.zeros_like(r)

        tri = _tri()
        zus, zvs, ws, bcols, lgs, lbs, qms, kms, vms = _mix_b_args(refs[:8])
        douts = _cols(dcat_ref[...].astype(F32), 128)
        grads = []
        for g in range(B_GROUPS):
            _, vjp = jax.vjp(lambda *a: _gmlp_group(*a, tri), zus[g], zvs[g], ws[g], bcols[g], lgs[g], lbs[g])
            grads.append(vjp(douts[g]))
        dzus, dzvs, dws, dbcols, dlgs, dlbs = [list(t) for t in zip(*grads)]
        grads = []
        for j in range(MEM_HEADS // 2):
            _, vjp = jax.vjp(_mem_pair, qms[j], kms[j], vms[j])
            grads.append(vjp(douts[B_GROUPS + j]))
        dqms, dkms, dvms = [list(t) for t in zip(*grads)]
        dproj_ref[...] = jnp.concatenate(dzus + dzvs + dqms, axis=1).astype(dproj_ref.dtype)
        for g in range(B_GROUPS):
            dws_ref[g] += dws[g]
        lanes = lax.broadcasted_iota(jnp.int32, (WINDOW, 128), 1)
        dbt = jnp.zeros((WINDOW, 128), F32)
        for g in range(B_GROUPS):
            dbt = dbt + jnp.where(lanes == g, jnp.broadcast_to(dbcols[g], (WINDOW, 128)), 0.0)
        dbt_ref[...] += dbt
        dlg_ref[...] += _onehot_rows(dlgs, (8, 128))
        dlb_ref[...] += _onehot_rows(dlbs, (8, 128))
        dkv_ref[...] += jnp.concatenate(dkms + dvms, axis=1)

    width = 2 * B_W + MEM_W
    const2 = lambda n: (0, 0)
    grid = (t // WINDOW,)
    body, r_ops, r_in, r_shapes, r_out, r_scratch = with_rider(body, 9, 6, grid, rider)
    return pl.pallas_call(
        body, name=name, grid=grid,
        in_specs=_mix_b_specs(nm) + [pl.BlockSpec((WINDOW, B_W + MEM_W), lambda n: (n, 0))] + r_in,
        out_specs=[pl.BlockSpec((WINDOW, width), lambda n: (n, 0)),
                   pl.BlockSpec((B_GROUPS, WINDOW, WINDOW), lambda n: (0, 0, 0)),
                   pl.BlockSpec((WINDOW, 128), const2), pl.BlockSpec((8, 128), const2), pl.BlockSpec((8, 128), const2),
                   pl.BlockSpec((nm, 2 * MEM_W), const2)] + r_out,
        out_shape=[_sds((t, width), BF16), _sds((B_GROUPS, WINDOW, WINDOW), F32), _sds((WINDOW, 128), F32),
                   _sds((8, 128), F32), _sds((8, 128), F32), _sds((nm, 2 * MEM_W), F32)] + r_shapes,
        scratch_shapes=r_scratch, compiler_params=_cp("arbitrary"))(proj, proj, ws, bt, lg, lb, kv, kv, dcat, *r_ops)


def _adamw_update(w, g, m, v):
    m2 = ADAM_B1 * m + (1.0 - ADAM_B1) * g
    v2 = ADAM_B2 * v + (1.0 - ADAM_B2) * jnp.square(g)
    m_hat = m2 / (1.0 - ADAM_B1 ** ADAM_STEP)
    v_hat = v2 / (1.0 - ADAM_B2 ** ADAM_STEP)
    return -ADAM_LR * (m_hat / (jnp.sqrt(v_hat) + ADAM_EPS) + ADAM_WD * w), m2, v2


def adamw(w, g, m, v, name):
    r, c = w.shape
    tr = _tile(r, (512, 352, 256, 128, 64, 32, 16, 8))

    def body(w_ref, g_ref, m_ref, v_ref, d_ref, nm_ref, nv_ref):
        d_ref[...], nm_ref[...], nv_ref[...] = _adamw_update(w_ref[...], g_ref[...], m_ref[...], v_ref[...])

    spec = pl.BlockSpec((tr, c), lambda i: (i, 0))
    return pl.pallas_call(
        body, name=name, grid=(r // tr,), in_specs=[spec] * 4, out_specs=[spec] * 3,
        out_shape=[_sds((r, c), F32)] * 3, compiler_params=_cp("parallel"))(w, g, m, v)


def adamw_halves(w, g_mine, g_theirs, m, v, c_arr, rows, name, rider=None):
    r, c = w.shape
    tr = _tile(rows // 2, (256, 352, 224, 160, 128, 64, 32, 16, 8))
    per_half = rows // 2 // tr

    def body(c_ref, w_ref, gm_ref, gt_ref, m_ref, v_ref, g_ref, d_ref, nm_ref, nv_ref):
        g = jnp.where(pl.program_id(0) // per_half % 2 == c_ref[0], gm_ref[...], gt_ref[...])
        g_ref[...] = g
        d_ref[...], nm_ref[...], nv_ref[...] = _adamw_update(w_ref[...], g, m_ref[...], v_ref[...])

    spec = pl.BlockSpec((tr, c), lambda i, cr: (i, 0))
    half = pl.BlockSpec((tr, c), lambda i, cr: (i // (2 * per_half) * per_half + i % per_half, 0))
    grid = (r // tr,)
    body, r_ops, r_in, r_shapes, r_out, r_scratch = with_rider(body, 6, 4, grid, rider)
    return pl.pallas_call(
        body, name=name,
        grid_spec=pltpu.PrefetchScalarGridSpec(num_scalar_prefetch=1, grid=grid, in_specs=[spec, half, half, spec, spec] + r_in,
                                               out_specs=[spec] * 4 + r_out, scratch_shapes=r_scratch),
        out_shape=[_sds((r, c), F32)] * 4 + r_shapes,
        compiler_params=_cp("arbitrary" if rider else "parallel"))(c_arr, w, g_mine, g_theirs, m, v, *r_ops)


def _place():
    return lax.axis_index("x"), lax.axis_index("y"), lax.axis_index("c")


def _other_chips(x, y):
    return [(1 - x, y), (x, 1 - y), (1 - x, 1 - y)]


def _remote(src, dst, send_sems, recv_sems, k, dev):
    return pltpu.make_async_remote_copy(src_ref=src, dst_ref=dst, send_sem=send_sems.at[k], recv_sem=recv_sems.at[k],
                                        device_id=dev, device_id_type=MESH)


class Exchange:
    def __init__(self, ins, out_shapes, n_sems, start, finish):
        self.ins, self.out_shapes, self.start, self.finish = list(ins), list(out_shapes), start, finish
        self.sems = [n_sems, n_sems] if isinstance(n_sems, int) else list(n_sems)

    def scratch(self):
        return [pltpu.SemaphoreType.DMA((n,)) for n in self.sems]


def both_exchanges(a, b):
    ni, no, ns = len(a.ins), len(a.out_shapes), len(a.sems)

    def start(ins, outs, *sems):
        a.start(ins[:ni], outs[:no], *sems[:ns])
        b.start(ins[ni:], outs[no:], *sems[ns:])

    def finish(ins, outs, *sems):
        a.finish(ins[:ni], outs[:no], *sems[:ns])
        b.finish(ins[ni:], outs[no:], *sems[ns:])

    return Exchange(a.ins + b.ins, a.out_shapes + b.out_shapes, a.sems + b.sems, start, finish)


def run_exchange(ex, name):
    ni, no = len(ex.ins), len(ex.out_shapes)

    def body(*refs):
        ex.start(refs[:ni], refs[ni:ni + no], *refs[ni + no:])
        ex.finish(refs[:ni], refs[ni:ni + no], *refs[ni + no:])

    return pl.pallas_call(
        body, name=name, in_specs=[HBM_SPEC] * ni, out_specs=[HBM_SPEC] * no, out_shape=ex.out_shapes, scratch_shapes=ex.scratch(),
        compiler_params=pltpu.CompilerParams(has_side_effects=True))(*ex.ins)


def with_rider(body, n_in, n_out, grid, ex):
    if ex is None:
        return body, [], [], [], [], []
    ni, no, ns = len(ex.ins), len(ex.out_shapes), len(ex.sems)

    def riding(*refs):
        r_in, r_out, sems = refs[n_in:n_in + ni], refs[n_in + ni + n_out:n_in + ni + n_out + no], refs[-ns:]
        first = last = None
        for axis, size in enumerate(grid):
            at_first, at_last = pl.program_id(axis) == 0, pl.program_id(axis) == size - 1
            first = at_first if first is None else jnp.logical_and(first, at_first)
            last = at_last if last is None else jnp.logical_and(last, at_last)

        @pl.when(first)
        def _():
            ex.start(r_in, r_out, *sems)

        body(*refs[:n_in], *refs[n_in + ni:n_in + ni + n_out], *refs[n_in + ni + n_out + no:-ns])

        @pl.when(last)
        def _():
            ex.finish(r_in, r_out, *sems)

    return riding, ex.ins, [HBM_SPEC] * ni, ex.out_shapes, [HBM_SPEC] * no, ex.scratch()


def gather_exchange(shards):
    nw = len(shards)
    entry = [k for _, k in shards]

    def rows(ref, cc):
        return pl.ds(cc * (ref.shape[1] // 2), ref.shape[1] // 2)

    def sent(ins, outs, send_sems, recv_sems, w, j):
        x, y, c = _place()
        return _remote(ins[w].at[pl.ds(entry[w], 1), rows(ins[w], c)], outs[w].at[:, 2 * x + y, rows(ins[w], c)], send_sems, recv_sems,
                       7 * w + j, (*_other_chips(x, y)[j], c))

    def landed(ins, outs, send_sems, recv_sems, w, j, cc, to):
        x, y, c = _place()
        chip = _other_chips(x, y)[j]
        blk = outs[w].at[:, 2 * chip[0] + chip[1], rows(ins[w], cc)]
        return _remote(blk, blk, send_sems, recv_sems, 7 * w + (j if to is None else 3 + j), (x, y, c) if to is None else to)

    def own(ins, outs, send_sems, recv_sems, w):
        x, y, c = _place()
        return _remote(ins[w].at[pl.ds(entry[w], 1)], outs[w].at[:, 2 * x + y], send_sems, recv_sems, 7 * w + 6, (x, y, 1 - c))

    def start(ins, outs, send_sems, recv_sems):
        for j in range(3):
            for w in range(nw):
                sent(ins, outs, send_sems, recv_sems, w, j).start()
        for w in range(nw):
            own(ins, outs, send_sems, recv_sems, w).start()

    def finish(ins, outs, send_sems, recv_sems):
        x, y, c = _place()
        for j in range(3):
            for w in range(nw):
                landed(ins, outs, send_sems, recv_sems, w, j, c, None).wait_recv()
                landed(ins, outs, send_sems, recv_sems, w, j, c, (x, y, 1 - c)).start()
        for w in range(nw):
            own(ins, outs, send_sems, recv_sems, w).wait()
        for j in range(3):
            for w in range(nw):
                landed(ins, outs, send_sems, recv_sems, w, j, 1 - c, (x, y, c)).wait_recv()
        for j in range(3):
            for w in range(nw):
                sent(ins, outs, send_sems, recv_sems, w, j).wait_send()
                landed(ins, outs, send_sems, recv_sems, w, j, c, (x, y, 1 - c)).wait_send()

    return Exchange([s for s, _ in shards], [_sds((1, 4) + s.shape[1:], s.dtype) for s, _ in shards], 7 * nw, start, finish)


def copies_exchange(ins, out_shapes, n_sems, copies):
    def start(*refs):
        for cp in copies(*refs):
            cp.start()

    def finish(*refs):
        for cp in copies(*refs):
            cp.wait()

    return Exchange(ins, out_shapes, n_sems, start, finish)


def sibling_halves_exchange(gs):
    def copies(ins, outs, send_sems, recv_sems):
        x, y, c = _place()
        return [_remote(g.at[:, :, pl.ds((1 - c) * (g.shape[2] // 2), g.shape[2] // 2)], o, send_sems, recv_sems, w, (x, y, 1 - c))
                for w, (g, o) in enumerate(zip(ins, outs))]

    return copies_exchange(gs, [_sds(g.shape[:2] + (g.shape[2] // 2, g.shape[3]), g.dtype) for g in gs], len(gs), copies)


def chips_exchange(sbs):
    def copies(ins, outs, send_sems, recv_sems):
        x, y, c = _place()
        return [_remote(s.at[:, 2 * chip[0] + chip[1]], o.at[j], send_sems, recv_sems, 3 * w + j, (*chip, c))
                for j, chip in enumerate(_other_chips(x, y)) for w, (s, o) in enumerate(zip(ins, outs))]

    return copies_exchange(sbs, [_sds((3, s.shape[0]) + s.shape[2:], s.dtype) for s in sbs], 3 * len(sbs), copies)


def sibling_exchange(fs):
    def copies(ins, outs, send_sems, recv_sems):
        x, y, c = _place()
        return [_remote(f, o, send_sems, recv_sems, w, (x, y, 1 - c)) for w, (f, o) in enumerate(zip(ins, outs))]

    return copies_exchange(fs, [_sds(f.shape, f.dtype) for f in fs], len(fs), copies)


def _half_tile(a):
    return _tile(a, (256, 352, 224, 176, 160, 128, 64, 32, 16))


def chip_partial_sums(g, r1, c_arr, name):
    nl, _, a2, b = r1.shape
    ta = _half_tile(a2)
    per = a2 // ta

    def body(c_ref, g_ref, r_ref, o_ref):
        o_ref[...] = (g_ref[...] + r_ref[...]).astype(o_ref.dtype)

    blk = (None, None, ta, b)
    return pl.pallas_call(
        body, name=name,
        grid_spec=pltpu.PrefetchScalarGridSpec(
            num_scalar_prefetch=1, grid=(nl, 4, per),
            in_specs=[pl.BlockSpec(blk, lambda l, s, i, c: (l, s, c[0] * per + i, 0)), pl.BlockSpec(blk, lambda l, s, i, c: (l, s, i, 0))],
            out_specs=pl.BlockSpec(blk, lambda l, s, i, c: (l, s, i, 0))),
        out_shape=_sds(r1.shape, BF16), compiler_params=_cp("parallel", "parallel", "parallel"))(c_arr, g, r1)


def shard_total(g, r1, r2, cs_arr, name):
    nl, _, a2, b = r1.shape
    ta = _half_tile(a2)
    per = a2 // ta

    def body(cs_ref, g_ref, r1_ref, p0_ref, p1_ref, p2_ref, o_ref):
        o_ref[...] = (((g_ref[...] + r1_ref[...]) + p0_ref[...].astype(F32)) + p1_ref[...].astype(F32)) + p2_ref[...].astype(F32)

    blk4, blk3 = (None, None, ta, b), (None, ta, b)
    peer = lambda k: pl.BlockSpec((None, None, ta, b), lambda l, i, cs: (k, l, i, 0))
    return pl.pallas_call(
        body, name=name,
        grid_spec=pltpu.PrefetchScalarGridSpec(
            num_scalar_prefetch=1, grid=(nl, per),
            in_specs=[pl.BlockSpec(blk4, lambda l, i, cs: (l, cs[1], cs[0] * per + i, 0)),
                      pl.BlockSpec(blk4, lambda l, i, cs: (l, cs[1], i, 0)), peer(0), peer(1), peer(2)],
            out_specs=pl.BlockSpec(blk3, lambda l, i, cs: (l, i, 0))),
        out_shape=_sds((nl, a2, b), F32), compiler_params=_cp("parallel", "parallel"))(cs_arr, g, r1, r2, r2, r2)


def allgather_small(v, name):
    r, n = v.shape

    def body(x_ref, out_ref, send_sems, recv_sems, local_sem):
        x, y, c = _place()
        me, sibling = (x, y, c), (x, y, 1 - c)
        chips = _other_chips(x, y)

        def rows(px, py, pc):
            return out_ref.at[pl.ds((4 * px + 2 * py + pc) * r, r), :]

        def copy(k, block, to, src=None):
            return _remote(rows(*block) if src is None else src, rows(*block), send_sems, recv_sems, k, to)

        mine = pltpu.make_async_copy(x_ref, rows(*me), local_sem)
        mine.start()
        first = [copy(0, me, sibling, src=x_ref)] + [copy(1 + j, me, (*chip, c), src=x_ref) for j, chip in enumerate(chips)]
        for cp in first:
            cp.start()
        passed = [copy(4 + j, (*chip, c), sibling) for j, chip in enumerate(chips)]
        for j, chip in enumerate(chips):
            copy(1 + j, (*chip, c), me).wait_recv()
            passed[j].start()
        copy(0, sibling, me).wait_recv()
        for j, chip in enumerate(chips):
            copy(4 + j, (*chip, 1 - c), me).wait_recv()
        for cp in first + passed:
            cp.wait_send()
        mine.wait()

    return pl.pallas_call(
        body, name=name, in_specs=[VMEM_SPEC], out_specs=VMEM_SPEC, out_shape=_sds((8 * r, n), v.dtype),
        scratch_shapes=[pltpu.SemaphoreType.DMA((7,)), pltpu.SemaphoreType.DMA((7,)), pltpu.SemaphoreType.DMA],
        compiler_params=pltpu.CompilerParams(has_side_effects=True, vmem_limit_bytes=V7X_VMEM_LIMIT_BYTES))(v)


def sum_devices(v8, name):
    _, r, n = v8.shape
    tr = _tile(r, (88, 64, 32, 16, 8))

    def body(v_ref, o_ref):
        acc = v_ref[0]
        for d in range(1, 8):
            acc = acc + v_ref[d]
        o_ref[...] = acc

    return pl.pallas_call(
        body, name=name, grid=(r // tr,), in_specs=[pl.BlockSpec((8, tr, n), lambda i: (0, i, 0))],
        out_specs=pl.BlockSpec((tr, n), lambda i: (i, 0)), out_shape=_sds((r, n), F32), compiler_params=_cp("parallel"))(v8)


SHARDED = (("a_w_in", 2), ("a_w_out", 1), ("b_w_in", 2), ("b_w_out", 1), ("w_mem_kv", 1), ("w_gate_up", 2), ("w_down", 1))


def _usable(wg, axis):
    l, _, a, b = wg.shape
    return wg.reshape(l, 4 * a, b) if axis == 1 else wg


def _pack(arrs):
    parts = []
    for a in arrs:
        flat = a.reshape(-1)
        flat = jnp.pad(flat, (0, -flat.shape[0] % 1024))
        parts.append(flat.reshape(-1, 128))
    return jnp.concatenate(parts, axis=0)


def _unpack(buf, like):
    out, row = [], 0
    for a in like:
        size = 1
        for s in a.shape:
            size *= s
        rows = -(-size // 1024) * 8
        out.append(buf[row:row + rows].reshape(-1)[:size].reshape(a.shape))
        row += rows
    return out


def kernel(x, mem, mem_norm_g, mix_norm_g, ffn_norm_g, final_norm_g, a_w_in, a_sinks, a_w_out, b_w_in, b_w_s, b_bias_s, b_ln_g, b_ln_b, b_w_out, w_mem_kv, w_gate_up, w_down, loss_target, m_mem_norm_g, m_mix_norm_g, m_ffn_norm_g, m_final_norm_g, m_a_w_in, m_a_sinks, m_a_w_out, m_b_w_in, m_b_w_s, m_b_bias_s, m_b_ln_g, m_b_ln_b, m_b_w_out, m_w_mem_kv, m_w_gate_up, m_w_down, v_mem_norm_g, v_mix_norm_g, v_ffn_norm_g, v_final_norm_g, v_a_w_in, v_a_sinks, v_a_w_out, v_b_w_in, v_b_w_s, v_b_bias_s, v_b_ln_g, v_b_ln_b, v_b_w_out, v_w_mem_kv, v_w_gate_up, v_w_down):
    given = dict(locals())
    depth = mix_norm_g.shape[0]
    d = x.shape[-1]
    xi, yi, ci = _place()
    c_arr = jnp.stack([ci]).astype(jnp.int32)
    cs_arr = jnp.stack([ci, 2 * xi + yi]).astype(jnp.int32)

    axis_of = dict(SHARDED)
    own = {n: given[n].astype(BF16) for n, _ in SHARDED}
    MIXER, FFN = slice(0, 3), slice(3, 5)

    def layer_weights(l):
        mix = "a" if l % 2 == 0 else "b"
        return [(mix + "_w_in", l // 2), (mix + "_w_out", l // 2), ("w_mem_kv", l), ("w_gate_up", l), ("w_down", l)]

    def gather_of(l, part=slice(0, 5)):
        return gather_exchange([(own[n], k) for n, k in layer_weights(l)[part]])

    def usable(l, gathered, part=slice(0, 5)):
        return {n[2:] if n[0] in "ab" else n: (_usable(wg, axis_of[n]), 0) for (n, _), wg in zip(layer_weights(l)[part], gathered)}

    weights = {0: usable(0, run_exchange(gather_of(0, MIXER), "gather_weights"), MIXER)}

    h = x.reshape(-1, d)
    tgt = loss_target.reshape(-1, d)
    mem2 = mem.reshape(-1, d)
    row = lambda v: v.reshape(1, -1)

    mem_n = rmsnorm_fwd(mem2, row(mem_norm_g), "mem_norm")
    saved = []
    for i in range(depth):
        j = i // 2
        wl = weights[i]
        w_in, w_out = wl["w_in"], wl["w_out"]
        kv = matmul(mem_n, wl["w_mem_kv"], "nn", BF16, "mem_kv")
        if i % 2 == 0:
            sk = jnp.pad(jnp.broadcast_to(a_sinks[j][:, None], (Q_HEADS, 128)), ((0, 16 - Q_HEADS), (0, 0)))
            xn, proj = norm_matmul(h, row(mix_norm_g[i]), w_in, BF16, "a_in")
            cat, *gathered = mixer_a_fwd(proj, sk, kv, "mixer_a", rider=gather_of(0, FFN) if i == 0 else None)
            if gathered:
                wl.update(usable(0, gathered, FFN))
            extra = (sk,)
        else:
            bt = jnp.pad(b_bias_s[j].T, ((0, 0), (0, 128 - B_GROUPS)))
            lg = jnp.pad(b_ln_g[j], ((0, 8 - B_GROUPS), (0, 0)))
            lb = jnp.pad(b_ln_b[j], ((0, 8 - B_GROUPS), (0, 0)))
            xn, proj = norm_matmul(h, row(mix_norm_g[i]), w_in, BF16, "b_in")
            cat = mixer_b_fwd(proj, b_w_s[j], bt, lg, lb, kv, "mixer_b")
            extra = (b_w_s[j], bt, lg, lb)
        h_mid = matmul(cat, w_out, "nn", F32, "mix_out", res=h)
        more = i + 1 < depth
        hn, gu, act, *gathered = gate_up_fwd(h_mid, row(ffn_norm_g[i]), *wl["w_gate_up"], "gate_up",
                                             rider=gather_of(i + 1, FFN) if more else None)
        h_out, *gathered_mixer = matmul(act, wl["w_down"], "nn", F32, "down", res=h_mid, rider=gather_of(i + 1, MIXER)) if more \
            else [matmul(act, wl["w_down"], "nn", F32, "down", res=h_mid)]
        if more:
            weights[i + 1] = {**usable(i + 1, gathered, FFN), **usable(i + 1, gathered_mixer, MIXER)}
        saved.append((h, xn, proj, cat, h_mid, hn, gu, act, kv, extra))
        h = h_out

    loss_part, dh, d_final_g, dh16 = loss_head(h, row(final_norm_g), tgt, "loss_head")
    loss = lax.psum(loss_part[0, 0], ("x", "y", "c"))

    d_mix_g, d_ffn_g = [None] * depth, [None] * depth
    d_sinks, d_ws, d_bias, d_lg, d_lb = [], [], [], [], []
    d_mem_n = jnp.zeros(mem2.shape, F32)
    totals = [None] * depth
    pending = None
    for i in reversed(range(depth)):
        h_in, xn, proj, cat, h_mid, hn, gu, act, kv, extra = saved[i]
        wl = weights[i]
        dgu, *from_sibling = down_dx_swiglu_bwd(dh16, wl["w_down"], gu, "down_dx",
                                                rider=sibling_halves_exchange(pending) if pending else None)
        if pending:
            partial = [chip_partial_sums(g, r1, c_arr, "grads_chip_sum") for g, r1 in zip(pending, from_sibling)]
        dw_down = matmul(act, dh16, "tn", F32, "down_dw", tm=1408, tk=2048, out_planes=("rows", 4))
        dw_gate_up = matmul((hn, 0), dgu, "tn", F32, "gate_up_dw", tn=1408, tk=2048, out_planes=("cols", 4))
        ffn = [dw_gate_up[None], dw_down[None]] if i == 0 else []
        joined = lambda exs: None if not exs else exs[0] if len(exs) == 1 else both_exchanges(*exs)
        dh, d_ffn_g[i], dh16, *landed = dx_norm_bwd(
            dgu, wl["w_gate_up"], h_mid, row(ffn_norm_g[i]), dh, "gate_up_dx",
            rider=joined(([chips_exchange(partial[FFN])] if pending else []) + ([sibling_halves_exchange(ffn)] if ffn else [])))
        if pending:
            chips_ffn = landed[:2]
        if ffn:
            ffn_sibling = landed[-len(ffn):]
            ffn_partial = [chip_partial_sums(g, r1, c_arr, "grads_chip_sum") for g, r1 in zip(ffn, ffn_sibling)]
        dcat = matmul(dh16, wl["w_out"], "nt", F32, "mix_out_dx")
        dw_out = matmul(cat, dh16, "tn", F32, "mix_out_dw", tk=2048, out_planes=("rows", 4))
        mixer_rider = joined(([chips_exchange(partial[MIXER])] if pending else []) + ([chips_exchange(ffn_partial)] if ffn else []))
        if i % 2 == 0:
            dproj, dsk, dkv, *landed = mixer_a_bwd(proj, dcat, extra[0], kv, "mixer_a_bwd", rider=mixer_rider)
            d_sinks.insert(0, dsk[:Q_HEADS, 0])
            dw_in = matmul(dproj, xn, "tn", F32, "a_in_dw", out_planes=("rows", 4))
        else:
            dproj, dws, dbt, dlg, dlb, dkv, *landed = mixer_b_bwd(proj, dcat, *extra, kv, "mixer_b_bwd", rider=mixer_rider)
            d_ws.insert(0, dws)
            d_bias.insert(0, dbt[:, :B_GROUPS].T)
            d_lg.insert(0, dlg[:B_GROUPS])
            d_lb.insert(0, dlb[:B_GROUPS])
            dw_in = matmul(dproj, xn, "tn", F32, "b_in_dw", out_planes=("rows", 4))
        if pending:
            from_chips = landed[:3] + chips_ffn
            totals[i + 1] = [shard_total(g, r1, r2, cs_arr, "grads_shard_total") for g, r1, r2 in zip(pending, from_sibling, from_chips)]
        if ffn:
            ffn_totals = [shard_total(g, r1, r2, cs_arr, "grads_shard_total") for g, r1, r2 in zip(ffn, ffn_sibling, landed[-len(ffn):])]
        dw_kv = matmul(mem_n, dkv, "tn", F32, "mem_kv_dw", out_planes=("rows", 4))
        d_mem_n = matmul(dkv, wl["w_mem_kv"], "nt", F32, "mem_kv_dx", res=d_mem_n)
        dh, d_mix_g[i], dh16 = dx_norm_bwd(dproj, wl["w_in"], h_in, row(mix_norm_g[i]), dh, "in_dx")
        pending = [dw_in[None], dw_out[None], dw_kv[None]] + ([] if ffn else [dw_gate_up[None], dw_down[None]])
    grad_x = dh.reshape(x.shape)
    _, d_mem_g = rmsnorm_bwd(mem2, row(mem_norm_g), d_mem_n, jnp.zeros(mem2.shape, F32), "mem_norm_bwd")

    from_sibling = run_exchange(sibling_halves_exchange(pending), "grads_sibling_swap")
    partial = [chip_partial_sums(g, r1, c_arr, "grads_chip_sum") for g, r1 in zip(pending, from_sibling)]
    totals[0] = [None] * 3 + ffn_totals
    out = {}

    def stacked(n):
        parts = [None] * given[n].shape[0]
        for l in range(depth):
            for (name_l, k), tot in zip(layer_weights(l), totals[l]):
                if name_l == n:
                    parts[k] = tot
        return jnp.concatenate(parts, axis=0)

    def update(n, g_mine, g_theirs, rider=None):
        flip = (lambda a: jnp.swapaxes(a, 1, 2)) if n in ("a_w_in", "b_w_in") else (lambda a: a)
        shape = flip(given[n]).shape
        two_d = lambda a: a.reshape(-1, shape[-1])
        res = adamw_halves(two_d(flip(given[n])), two_d(g_mine), two_d(g_theirs), two_d(flip(given["m_" + n])),
                           two_d(flip(given["v_" + n])), c_arr, shape[1], "adamw", rider=rider)
        out[n] = tuple(flip(r.reshape(shape)) for r in res[:4])
        return res[4:]

    ffn_names = ("w_gate_up", "w_down")
    mine = [stacked(n) for n in ffn_names]
    theirs = run_exchange(sibling_exchange(mine), "grads_sibling_totals")
    from_chips = update(ffn_names[0], mine[0], theirs[0], rider=chips_exchange(partial))
    update(ffn_names[1], mine[1], theirs[1])
    totals[0][:3] = [shard_total(g, r1, r2, cs_arr, "grads_shard_total") for g, r1, r2 in zip(pending, from_sibling, from_chips)]
    mixer_names = [n for n, _ in SHARDED if n not in ffn_names]
    mine = [stacked(n) for n in mixer_names]
    theirs = run_exchange(sibling_exchange(mine), "grads_sibling_totals")
    for n, g_mine, g_theirs in zip(mixer_names, mine, theirs):
        update(n, g_mine, g_theirs)

    small = ("mem_norm_g", "mix_norm_g", "ffn_norm_g", "final_norm_g", "a_sinks", "b_w_s", "b_bias_s", "b_ln_g", "b_ln_b")
    small_g = [d_mem_g[0], jnp.concatenate(d_mix_g, axis=0), jnp.concatenate(d_ffn_g, axis=0), d_final_g[0],
               jnp.stack(d_sinks), jnp.stack(d_ws), jnp.stack(d_bias), jnp.stack(d_lg), jnp.stack(d_lb)]
    packed = _pack(small_g)
    g_small = sum_devices(allgather_small(packed, "small_allgather").reshape(8, *packed.shape), "small_sum")
    like = [given[n] for n in small]
    delta_s, new_m_s, new_v_s = adamw(_pack(like), g_small, _pack([given["m_" + n] for n in small]),
                                      _pack([given["v_" + n] for n in small]), "adamw_small")
    for n, g, dl, nm_, nv_ in zip(small, _unpack(g_small, like), _unpack(delta_s, like), _unpack(new_m_s, like), _unpack(new_v_s, like)):
        out[n] = (g, dl, nm_, nv_)

    order = ("mem_norm_g", "mix_norm_g", "ffn_norm_g", "final_norm_g", "a_w_in", "a_sinks", "a_w_out", "b_w_in", "b_w_s",
             "b_bias_s", "b_ln_g", "b_ln_b", "b_w_out", "w_mem_kv", "w_gate_up", "w_down")
    return (loss, grad_x, *[out[n][0] for n in order], *[out[n][1] for n in order],
            *[out[n][2] for n in order], *[out[n][3] for n in order])
```

```python
import jax
import jax.numpy as jnp
from jax import lax
from jax.experimental import pallas as pl
from jax.experimental.pallas import tpu as pltpu

F32, BF16 = jnp.float32, jnp.bfloat16
EPS = 1e-6
HEAD_DIM = 64
Q_HEADS, KV_HEADS, GROUP = 12, 2, 6
WINDOW = 128
MEM_HEADS = 4
B_GROUPS = 6
Q_W, KV_W, MEM_W, B_W = 768, 128, 256, 768
SCALE = HEAD_DIM ** -0.5
NEG = -1e30
ADAM_LR, ADAM_B1, ADAM_B2, ADAM_EPS, ADAM_WD, ADAM_STEP = 0.001, 0.9, 0.999, 1e-08, 0.01, 10
V7X_VMEM_LIMIT_BYTES = 48 * 1024 * 1024
MESH = pl.DeviceIdType.MESH
HBM_SPEC = pl.BlockSpec(memory_space=pltpu.HBM)
VMEM_SPEC = pl.BlockSpec(memory_space=pltpu.VMEM)


def _cp(*sem):
    return pltpu.CompilerParams(dimension_semantics=sem or None, vmem_limit_bytes=V7X_VMEM_LIMIT_BYTES)


def _tile(n, cands):
    for t in cands:
        if n % t == 0:
            return t
    return n


def _sds(shape, dtype):
    return jax.ShapeDtypeStruct(tuple(shape), dtype)


def _dot(a, b, ca, cb):
    return lax.dot_general(a, b, (((ca,), (cb,)), ((), ())), preferred_element_type=F32)


def _rms(x, g):
    return x * lax.rsqrt(jnp.mean(x * x, axis=-1, keepdims=True) + EPS) * g


def rmsnorm_fwd(h, g, name):
    t, d = h.shape
    tm = _tile(t, (512, 256, 128))

    def body(h_ref, g_ref, o_ref):
        o_ref[...] = _rms(h_ref[...], g_ref[...]).astype(o_ref.dtype)

    return pl.pallas_call(
        body, name=name, grid=(t // tm,),
        in_specs=[pl.BlockSpec((tm, d), lambda i: (i, 0)), pl.BlockSpec((1, d), lambda i: (0, 0))],
        out_specs=pl.BlockSpec((tm, d), lambda i: (i, 0)),
        out_shape=_sds((t, d), BF16), compiler_params=_cp("parallel"))(h, g)


def rmsnorm_bwd(h, g, dxn, dres, name):
    t, d = h.shape
    tm = _tile(t, (512, 256, 128))

    def body(h_ref, g_ref, dxn_ref, dres_ref, dh_ref, dg_ref):
        _, vjp = jax.vjp(_rms, h_ref[...], g_ref[...])
        dh, dg = vjp(dxn_ref[...].astype(F32))
        dh_ref[...] = dres_ref[...] + dh

        @pl.when(pl.program_id(0) == 0)
        def _():
            dg_ref[...] = jnp.zeros_like(dg_ref)

        dg_ref[...] += dg

    row = pl.BlockSpec((tm, d), lambda i: (i, 0))
    vec = pl.BlockSpec((1, d), lambda i: (0, 0))
    return pl.pallas_call(
        body, name=name, grid=(t // tm,), in_specs=[row, vec, row, row], out_specs=[row, vec],
        out_shape=[_sds((t, d), F32), _sds((1, d), F32)], compiler_params=_cp("arbitrary"))(h, g, dxn, dres)


def loss_head(h, g, tgt, name):
    t, d = h.shape
    tm = _tile(t, (512, 256, 128))

    def body(h_ref, g_ref, t_ref, l_ref, dh_ref, dg_ref, dh16_ref):
        y, vjp = jax.vjp(_rms, h_ref[...], g_ref[...])
        err = y - t_ref[...]
        dh, dg = vjp(err * (1.0 / d))
        dh_ref[...] = dh
        dh16_ref[...] = dh.astype(BF16)
        part = 0.5 * jnp.sum(jnp.mean(err * err, axis=-1, keepdims=True), axis=0, keepdims=True)

        @pl.when(pl.program_id(0) == 0)
        def _():
            dg_ref[...] = jnp.zeros_like(dg_ref)
            l_ref[...] = jnp.zeros_like(l_ref)

        dg_ref[...] += dg
        l_ref[...] += part

    row = pl.BlockSpec((tm, d), lambda i: (i, 0))
    vec = pl.BlockSpec((1, d), lambda i: (0, 0))
    one = pl.BlockSpec((1, 1), lambda i: (0, 0))
    return pl.pallas_call(
        body, name=name, grid=(t // tm,), in_specs=[row, vec, row], out_specs=[one, row, vec, row],
        out_shape=[_sds((1, 1), F32), _sds((t, d), F32), _sds((1, d), F32), _sds((t, d), BF16)],
        compiler_params=_cp("arbitrary"))(h, g, tgt)


def _logical(op):
    arr, lead = op if isinstance(op, tuple) else (op, None)
    planes = arr.shape[-3] if arr.ndim - (lead is not None) == 3 else 1
    return arr, lead, arr.shape[-2], arr.shape[-1], planes


def _spec(op, rows_t, cols_t, row_of, col_of):
    arr, lead, _, cols, _ = _logical(op)
    per = cols // cols_t
    lead = () if lead is None else (lead,)
    if arr.ndim - len(lead) == 2:
        return pl.BlockSpec((None,) * len(lead) + (rows_t, cols_t), lambda *g: lead + (row_of(*g), col_of(*g)))
    return pl.BlockSpec((None,) * len(lead) + (None, rows_t, cols_t),
                        lambda *g: lead + (col_of(*g) // per, row_of(*g), col_of(*g) % per))


def _arr(op):
    return op[0] if isinstance(op, tuple) else op


def _resident_whole(w, d):
    wa, layer = w
    planes, per = wa.shape[-3], wa.shape[-1]

    def fill(w_ref, whole_ref):
        for s in range(planes):
            whole_ref[:, s * per:(s + 1) * per] = w_ref[s]

    return (pl.BlockSpec((None, planes, d, per), lambda i: (layer, 0, 0, 0), pipeline_mode=pl.Buffered(1)),
            pltpu.VMEM((d, planes * per), wa.dtype), fill)


def norm_matmul(h, g, w, out_dtype, name):
    t, d = h.shape
    w_spec, whole, fill = _resident_whole(w, d)
    n = whole.shape[1]
    tm = _tile(t, (512, 256, 128))

    def body(h_ref, g_ref, w_ref, xn_ref, o_ref, whole_ref):
        @pl.when(pl.program_id(0) == 0)
        def _():
            fill(w_ref, whole_ref)

        xn = _rms(h_ref[...], g_ref[...]).astype(BF16)
        xn_ref[...] = xn
        o_ref[...] = _dot(xn, whole_ref[...], 1, 0).astype(o_ref.dtype)

    return pl.pallas_call(
        body, name=name, grid=(t // tm,),
        in_specs=[pl.BlockSpec((tm, d), lambda i: (i, 0)), pl.BlockSpec((1, d), lambda i: (0, 0)), w_spec],
        out_specs=[pl.BlockSpec((tm, d), lambda i: (i, 0)), pl.BlockSpec((tm, n), lambda i: (i, 0))],
        out_shape=[_sds((t, d), BF16), _sds((t, n), out_dtype)], scratch_shapes=[whole],
        compiler_params=_cp("arbitrary"))(h, g, w[0])


def dx_norm_bwd(dy, w, h, g, dres, name, rider=None):
    t, d = h.shape
    dy_arr, dy_lead, _, kc, kp = _logical(dy)
    w_arr, w_lead, _, wc, wp = _logical(w)
    assert kc * kp == wc * wp and dy_lead is None, name
    chunk = min(kc, wc)
    tm = _tile(t, (512, 256, 128))

    def piece(ref, planes, cols, q):
        off = q * chunk % cols
        return ref[q * chunk // cols, :, off:off + chunk] if planes > 1 else ref[:, off:off + chunk]

    narrow = wp > 1 and wc % 128 != 0
    if narrow:
        assert kp == 1, name
        w_whole_spec, whole, fill = _resident_whole(w, d)

    def body(dy_ref, w_ref, h_ref, g_ref, dres_ref, dh_ref, dg_ref, dh16_ref, *whole_ref):
        if narrow:
            @pl.when(pl.program_id(0) == 0)
            def _():
                fill(w_ref, whole_ref[0])

            dxn = _dot(dy_ref[...].astype(BF16), whole_ref[0][...], 1, 1)
        else:
            dxn = None
            for q in range(kc * kp // chunk):
                p = _dot(piece(dy_ref, kp, kc, q).astype(BF16), piece(w_ref, wp, wc, q), 1, 1)
                dxn = p if dxn is None else dxn + p
        _, vjp = jax.vjp(_rms, h_ref[...], g_ref[...])
        dh, dg = vjp(dxn)
        dh = dres_ref[...] + dh
        dh_ref[...] = dh
        dh16_ref[...] = dh.astype(BF16)

        @pl.when(pl.program_id(0) == 0)
        def _():
            dg_ref[...] = jnp.zeros_like(dg_ref)

        dg_ref[...] += dg

    w_lead = () if w_lead is None else (w_lead,)
    w_block = ((wp,) if wp > 1 else ()) + (d, wc)
    w_spec = pl.BlockSpec((None,) * len(w_lead) + w_block, lambda i: w_lead + (0,) * len(w_block), pipeline_mode=pl.Buffered(1))
    if narrow:
        w_spec = w_whole_spec
    dy_spec = pl.BlockSpec((kp, tm, kc), lambda i: (0, i, 0)) if kp > 1 else pl.BlockSpec((tm, kc), lambda i: (i, 0))
    row = pl.BlockSpec((tm, d), lambda i: (i, 0))
    vec = pl.BlockSpec((1, d), lambda i: (0, 0))
    grid = (t // tm,)
    body, r_ops, r_in, r_shapes, r_out, r_scratch = with_rider(body, 5, 3, grid, rider)
    return pl.pallas_call(
        body, name=name, grid=grid, in_specs=[dy_spec, w_spec, row, vec, row] + r_in,
        out_specs=[row, vec, row] + r_out, out_shape=[_sds((t, d), F32), _sds((1, d), F32), _sds((t, d), BF16)] + r_shapes,
        scratch_shapes=([whole] if narrow else []) + r_scratch,
        compiler_params=_cp("arbitrary"))(dy_arr, w_arr, h, g, dres, *r_ops)


def matmul(a, b, mode, out_dtype, name, res=None, tm=None, tn=1792, tk=2816, out_planes=None, out_into=None, rider=None):
    _, _, ar, ac, ap = _logical(a)
    _, _, br, bc, bp = _logical(b)
    if mode == "nn":
        m, ka, kb, n = ar, ac * ap, br, bc * bp
        n_plane, ka_plane, kb_plane = bc, ac, br
    elif mode == "nt":
        m, ka, n, kb = ar, ac * ap, br, bc * bp
        n_plane, ka_plane, kb_plane = br, ac, bc
    else:
        ka, m, kb, n = ar, ac * ap, br, bc * bp
        n_plane, ka_plane, kb_plane = bc, ar, br
    m_plane = ac if mode == "tn" else ar
    assert ka == kb, name
    k = ka
    kind, planes = out_planes or ("cols", 1)
    narrow = kind == "cols" and (n // planes) % 128 != 0
    if kind == "cols" and not narrow:
        n_plane = min(n_plane, n // planes)
    if narrow:
        tn = n
    tm = _tile(m_plane, ((1024, 1408, 512, 256, 128) if mode == "tn" else (512, 256, 128)) if tm is None else (tm, 1024, 512, 256, 128))
    if kind == "rows" and tm % (m // planes):
        tm = m_plane
    tn = _tile(n_plane, (tn, 1792, 1408, 1280, 1024, 896, 640, 512, 256, 128))
    tk = _tile(min(ka_plane, kb_plane), (tk, 2816, 1792, 1408, 1280, 1024, 512, 256, 128))
    nk = k // tk
    row_i, col_j, red = (lambda i, j, kk: i), (lambda i, j, kk: j), (lambda i, j, kk: kk)
    if mode == "nn":
        a_spec, b_spec, ca, cb = _spec(a, tm, tk, row_i, red), _spec(b, tk, tn, red, col_j), 1, 0
    elif mode == "nt":
        a_spec, b_spec, ca, cb = _spec(a, tm, tk, row_i, red), _spec(b, tn, tk, col_j, red), 1, 1
    else:
        a_spec, b_spec, ca, cb = _spec(a, tk, tm, red, row_i), _spec(b, tk, tn, red, col_j), 0, 0
    lead = () if out_into is None else (out_into[1],)
    if planes == 1:
        o_shape, o_block = (m, n), (tm, tn)
        o_index = lambda i, j, kk: lead + (i, j)
    elif narrow:
        o_shape, o_block = (planes, m, n // planes), (planes, tm, n // planes)
        o_index = lambda i, j, kk: lead + (0, i, 0)
    elif kind == "cols":
        per = n // planes // tn
        o_shape, o_block = (planes, m, n // planes), (None, tm, tn)
        o_index = lambda i, j, kk: lead + (j // per, i, j % per)
    else:
        o_shape, o_block = (planes, m // planes, n), (tm // (m // planes), m // planes, tn)
        o_index = lambda i, j, kk: lead + (i, 0, j)
    o_spec = pl.BlockSpec((None,) * len(lead) + o_block, o_index)
    if out_into is not None:
        assert out_into[0].shape[1:] == o_shape and out_into[0].dtype == out_dtype, name
        o_shape = out_into[0].shape
    has_res = res is not None
    n_in = 2 + has_res + (out_into is not None)

    def put(o_ref, v):
        if narrow:
            for s in range(planes):
                o_ref[s] = v[:, s * (n // planes):(s + 1) * (n // planes)].astype(o_ref.dtype)
        else:
            o_ref[...] = v.astype(o_ref.dtype).reshape(o_ref.shape)

    def body(*refs):
        a_ref, b_ref = refs[:2]
        rest = refs[2:2 + has_res] + refs[n_in:]
        o_ref = rest[1] if has_res else rest[0]
        p = _dot(a_ref[...].astype(BF16), b_ref[...].astype(BF16), ca, cb)
        if nk == 1:
            if has_res:
                p = p + rest[0][...]
            put(o_ref, p)
        else:
            acc_ref = rest[-1]
            kk = pl.program_id(2)

            @pl.when(kk == 0)
            def _():
                acc_ref[...] = p

            @pl.when(kk > 0)
            def _():
                acc_ref[...] += p

            @pl.when(kk == nk - 1)
            def _():
                r = acc_ref[...]
                if has_res:
                    r = r + rest[0][...]
                put(o_ref, r)

    operands = [_arr(a), _arr(b)] + ([res] if has_res else []) + ([out_into[0]] if out_into is not None else [])
    grid = (m // tm, n // tn, nk)
    body, r_ops, r_in, r_shapes, r_out, r_scratch = with_rider(body, n_in, 1, grid, rider)
    out = pl.pallas_call(
        body, name=name, grid=grid,
        in_specs=[a_spec, b_spec] + ([pl.BlockSpec((tm, tn), lambda i, j, kk: (i, j))] if has_res else [])
        + ([pl.BlockSpec(memory_space=pl.ANY)] if out_into is not None else []) + r_in,
        out_specs=[o_spec] + r_out, out_shape=[_sds(o_shape, out_dtype)] + r_shapes,
        input_output_aliases={n_in - 1: 0} if out_into is not None else {},
        scratch_shapes=([pltpu.VMEM((tm, tn), F32)] if nk > 1 else []) + r_scratch,
        compiler_params=_cp(*(("arbitrary",) * 3 if rider else ("parallel", "parallel", "arbitrary"))))(*operands, *r_ops)
    return out if rider else out[0]


def gate_up_fwd(h, g, w, layer, name, rider=None):
    t, d = h.shape
    half = w.shape[-1]
    tm = _tile(t, (512, 256, 128))

    def body(h_ref, g_ref, wg_ref, wu_ref, hn_ref, gu_ref, act_ref):
        a = _rms(h_ref[...], g_ref[...]).astype(BF16)
        hn_ref[...] = a
        gate, up = _dot(a, wg_ref[...], 1, 0), _dot(a, wu_ref[...], 1, 0)
        sig = 1.0 / (1.0 + jnp.exp(-gate))
        silu = gate * sig
        gu_ref[0] = (up * (sig + silu * (1.0 - sig))).astype(gu_ref.dtype)
        gu_ref[1] = silu.astype(gu_ref.dtype)
        act_ref[...] = (silu * up).astype(act_ref.dtype)

    grid = (2, t // tm)
    body, r_ops, r_in, r_shapes, r_out, r_scratch = with_rider(body, 4, 3, grid, rider)
    return pl.pallas_call(
        body, name=name, grid=grid,
        in_specs=[pl.BlockSpec((tm, d), lambda j, i: (i, 0)), pl.BlockSpec((1, d), lambda j, i: (0, 0)),
                  pl.BlockSpec((None, None, d, half), lambda j, i: (layer, j, 0, 0)),
                  pl.BlockSpec((None, None, d, half), lambda j, i: (layer, 2 + j, 0, 0))] + r_in,
        out_specs=[pl.BlockSpec((None, tm, d), lambda j, i: (j, i, 0)), pl.BlockSpec((2, tm, half), lambda j, i: (0, i, j)),
                   pl.BlockSpec((tm, half), lambda j, i: (i, j))] + r_out,
        out_shape=[_sds((2, t, d), BF16), _sds((2, t, 2 * half), BF16), _sds((t, 2 * half), BF16)] + r_shapes,
        scratch_shapes=r_scratch, compiler_params=_cp("arbitrary", "arbitrary"))(h, g, w, w, *r_ops)


def down_dx_swiglu_bwd(dh, wd, gu, name, rider=None):
    t, d = dh.shape
    w, layer = wd
    f = w.shape[-2]
    tm = _tile(t, (512, 256, 128))
    tn = _tile(f, (1408, 512, 256, 128))

    def body(dh_ref, w_ref, gu_ref, o_ref):
        dact = _dot(dh_ref[...].astype(BF16), w_ref[...], 1, 1)
        o_ref[0] = (dact * gu_ref[0].astype(F32)).astype(o_ref.dtype)
        o_ref[1] = (dact * gu_ref[1].astype(F32)).astype(o_ref.dtype)

    planes = pl.BlockSpec((2, tm, tn), lambda j, i: (0, i, j))
    grid = (f // tn, t // tm)
    body, r_ops, r_in, r_shapes, r_out, r_scratch = with_rider(body, 3, 1, grid, rider)
    return pl.pallas_call(
        body, name=name, grid=grid,
        in_specs=[pl.BlockSpec((tm, d), lambda j, i: (i, 0)), pl.BlockSpec((None, tn, d), lambda j, i: (layer, j, 0)), planes] + r_in,
        out_specs=[planes] + r_out, out_shape=[_sds((2, t, f), BF16)] + r_shapes, scratch_shapes=r_scratch,
        compiler_params=_cp("arbitrary", "arbitrary"))(dh, w, gu, *r_ops)


def _softmax_over_keys(s, sink=None):
    m = s.max(axis=0, keepdims=True)
    if sink is not None:
        m = jnp.maximum(m, sink)
    m = lax.stop_gradient(m)
    e = jnp.exp(s - m)
    den = e.sum(axis=0, keepdims=True)
    if sink is not None:
        den = den + jnp.exp(sink - m)
    return e * (1.0 / den)


def _low_lanes():
    return lax.broadcasted_iota(jnp.int32, (1, 128), 1) < HEAD_DIM


def _stack_heads(slabs):
    low = _low_lanes()
    return jnp.concatenate([p for s in slabs for p in (jnp.where(low, s, 0.0), jnp.where(low, 0.0, s))], axis=0)


def _unstack_heads(o, n_slabs):
    low = _low_lanes()
    return [jnp.where(low, o[2 * j * WINDOW:(2 * j + 1) * WINDOW], o[(2 * j + 1) * WINDOW:(2 * j + 2) * WINDOW])
            for j in range(n_slabs)]


def _swa_group(q_slabs, k_both, v_both, sinks, mask):
    qs = _stack_heads(q_slabs).astype(BF16)
    s = jnp.where(mask, _dot(k_both.astype(BF16), qs, 1, 1) * SCALE, NEG)
    sink = jnp.concatenate([jnp.broadcast_to(v, (1, WINDOW)) for v in sinks], axis=1)
    return _unstack_heads(_dot(_softmax_over_keys(s, sink).astype(BF16), v_both.astype(BF16), 0, 0), len(q_slabs))


def _mem_pair(q_slab, k_slab, v_slab):
    s = _dot(k_slab.astype(BF16), _stack_heads([q_slab]).astype(BF16), 1, 1) * SCALE
    return _unstack_heads(_dot(_softmax_over_keys(s).astype(BF16), v_slab.astype(BF16), 0, 0), 1)[0]


def _gelu(x):
    return 0.5 * x * (1.0 + jnp.tanh(0.7978845608028654 * (x + 0.044715 * (x * x * x))))


def _gmlp_group(zu, zv, w, bcol, lg, lb, tri):
    u, v = _gelu(zu), _gelu(zv)
    mu = jnp.mean(v, axis=-1, keepdims=True)
    var = jnp.mean(jnp.square(v - mu), axis=-1, keepdims=True)
    vn = (v - mu) * lax.rsqrt(var + EPS) * lg + lb
    sv = _dot(jnp.where(tri, w, 0.0).astype(BF16), vn.astype(BF16), 1, 0) + bcol
    return u * sv


def _cols(x, width):
    return [x[:, j * width:(j + 1) * width] for j in range(x.shape[1] // width)]


def _swa_mask(has_prev):
    qi = lax.broadcasted_iota(jnp.int32, (2 * WINDOW, GROUP * WINDOW), 1) & (WINDOW - 1)
    kj = lax.broadcasted_iota(jnp.int32, (2 * WINDOW, GROUP * WINDOW), 0)
    in_prev = jnp.logical_and(jnp.logical_and(kj < WINDOW, kj > qi), has_prev)
    return jnp.logical_or(in_prev, jnp.logical_and(kj >= WINDOW, kj - WINDOW <= qi))


def _mix_a(q_slabs, k_boths, v_boths, sinks, qm_slabs, km_slabs, vm_slabs, mask):
    per = GROUP // 2
    outs = []
    for g in range(KV_HEADS):
        outs += _swa_group(q_slabs[per * g:per * (g + 1)], k_boths[g], v_boths[g], sinks[GROUP * g:GROUP * (g + 1)], mask)
    return outs + [_mem_pair(qm_slabs[j], km_slabs[j], vm_slabs[j]) for j in range(MEM_HEADS // 2)]


def _in_both_halves(prev, cur):
    cat = jnp.concatenate([prev, cur], axis=0)
    rolled = pltpu.roll(cat, HEAD_DIM, axis=1)
    low = _low_lanes()
    return [jnp.where(low, cat, rolled), jnp.where(low, rolled, cat)]


def _from_both_halves(d_boths):
    t = [d + pltpu.roll(d, HEAD_DIM, axis=1) for d in d_boths]
    return jnp.where(_low_lanes(), t[0], t[1])


def _mix_a_specs(nm, blk):
    prev = lambda n: jnp.maximum(blk(n) - 1, 0)
    return [pl.BlockSpec((WINDOW, Q_W), lambda n: (blk(n), 0)),
            pl.BlockSpec((WINDOW, KV_W), lambda n: (prev(n), Q_W // KV_W)),
            pl.BlockSpec((WINDOW, KV_W), lambda n: (blk(n), Q_W // KV_W)),
            pl.BlockSpec((WINDOW, KV_W), lambda n: (prev(n), Q_W // KV_W + 1)),
            pl.BlockSpec((WINDOW, KV_W), lambda n: (blk(n), Q_W // KV_W + 1)),
            pl.BlockSpec((WINDOW, MEM_W), lambda n: (blk(n), (Q_W + 2 * KV_W) // MEM_W)),
            pl.BlockSpec((16, 128), lambda n: (0, 0)),
            pl.BlockSpec((nm, MEM_W), lambda n: (0, 0)),
            pl.BlockSpec((nm, MEM_W), lambda n: (0, 1))]


def _mix_a_args(refs):
    q, kp, kc, vp, vc, qm, sk, km, vm = [r[...].astype(F32) for r in refs]
    return (_cols(q, 128), _in_both_halves(kp, kc), _in_both_halves(vp, vc), [sk[h:h + 1, 0:1] for h in range(Q_HEADS)],
            _cols(qm, 128), _cols(km, 128), _cols(vm, 128))


def mixer_a_fwd(proj, sk, kv, name, rider=None):
    t, nm = proj.shape[0], kv.shape[0]

    def body(*refs):
        o_ref = refs[-1]
        slabs = _mix_a(*_mix_a_args(refs[:-1]), _swa_mask(pl.program_id(0) > 0))
        o_ref[...] = jnp.concatenate(slabs, axis=1).astype(o_ref.dtype)

    grid = (t // WINDOW,)
    body, r_ops, r_in, r_shapes, r_out, r_scratch = with_rider(body, 9, 1, grid, rider)
    return pl.pallas_call(
        body, name=name, grid=grid, in_specs=_mix_a_specs(nm, lambda n: n) + r_in,
        out_specs=[pl.BlockSpec((WINDOW, Q_W + MEM_W), lambda n: (n, 0))] + r_out,
        out_shape=[_sds((t, Q_W + MEM_W), BF16)] + r_shapes, scratch_shapes=r_scratch,
        compiler_params=_cp("arbitrary"))(proj, proj, proj, proj, proj, proj, sk, kv, kv, *r_ops)


def _onehot_rows(vals, shape):
    rows = lax.broadcasted_iota(jnp.int32, shape, 0)
    out = jnp.zeros(shape, F32)
    for h, v in enumerate(vals):
        out = out + jnp.where(rows == h, jnp.broadcast_to(v, shape), 0.0)
    return out


def mixer_a_bwd(proj, dcat, sk, kv, name, rider=None):
    t, nm = proj.shape[0], kv.shape[0]
    nb = t // WINDOW
    blk = lambda i: nb - 1 - i

    def body(*refs):
        dcat_ref, dproj_ref, dsk_ref, dkv_ref, carry_ref = refs[9:]
        i = pl.program_id(0)

        @pl.when(i == 0)
        def _():
            carry_ref[...] = jnp.zeros_like(carry_ref)
            dsk_ref[...] = jnp.zeros_like(dsk_ref)
            dkv_ref[...] = jnp.zeros_like(dkv_ref)

        mask = _swa_mask(blk(i) > 0)
        _, vjp = jax.vjp(lambda *a: _mix_a(*a, mask), *_mix_a_args(refs[:9]))
        dqs, dk_boths, dv_boths, dsinks, dqms, dkms, dvms = vjp(_cols(dcat_ref[...].astype(F32), 128))
        dkv = jnp.concatenate([_from_both_halves(dk_boths), _from_both_halves(dv_boths)], axis=1)
        dkv_cur = dkv[WINDOW:] + carry_ref[...]
        carry_ref[...] = dkv[:WINDOW]
        dproj_ref[...] = jnp.concatenate(dqs + [dkv_cur] + dqms, axis=1).astype(dproj_ref.dtype)
        dsk_ref[...] += _onehot_rows(dsinks, (16, 128))
        dkv_ref[...] += jnp.concatenate(dkms + dvms, axis=1)

    width = Q_W + 2 * KV_W + MEM_W
    body, r_ops, r_in, r_shapes, r_out, r_scratch = with_rider(body, 10, 3, (nb,), rider)
    return pl.pallas_call(
        body, name=name, grid=(nb,),
        in_specs=_mix_a_specs(nm, blk) + [pl.BlockSpec((WINDOW, Q_W + MEM_W), lambda i: (blk(i), 0))] + r_in,
        out_specs=[pl.BlockSpec((WINDOW, width), lambda i: (blk(i), 0)), pl.BlockSpec((16, 128), lambda i: (0, 0)),
                   pl.BlockSpec((nm, 2 * MEM_W), lambda i: (0, 0))] + r_out,
        out_shape=[_sds((t, width), BF16), _sds((16, 128), F32), _sds((nm, 2 * MEM_W), F32)] + r_shapes,
        scratch_shapes=[pltpu.VMEM((WINDOW, 2 * KV_W), F32)] + r_scratch,
        compiler_params=_cp("arbitrary"))(proj, proj, proj, proj, proj, proj, sk, kv, kv, dcat, *r_ops)


def _mix_b(zus, zvs, ws, bcols, lgs, lbs, qms, kms, vms, tri):
    outs = [_gmlp_group(zus[g], zvs[g], ws[g], bcols[g], lgs[g], lbs[g], tri) for g in range(B_GROUPS)]
    return outs + [_mem_pair(qms[j], kms[j], vms[j]) for j in range(MEM_HEADS // 2)]


def _mix_b_specs(nm):
    return [pl.BlockSpec((WINDOW, 2 * B_W), lambda n: (n, 0)),
            pl.BlockSpec((WINDOW, MEM_W), lambda n: (n, 2 * B_W // MEM_W)),
            pl.BlockSpec((B_GROUPS, WINDOW, WINDOW), lambda n: (0, 0, 0)),
            pl.BlockSpec((WINDOW, 128), lambda n: (0, 0)),
            pl.BlockSpec((8, 128), lambda n: (0, 0)),
            pl.BlockSpec((8, 128), lambda n: (0, 0)),
            pl.BlockSpec((nm, MEM_W), lambda n: (0, 0)),
            pl.BlockSpec((nm, MEM_W), lambda n: (0, 1))]


def _mix_b_args(refs):
    z, qm, ws, bt, lg, lb, km, vm = [r[...].astype(F32) for r in refs]
    zs = _cols(z, 128)
    return (zs[:B_GROUPS], zs[B_GROUPS:], [ws[g] for g in range(B_GROUPS)], [bt[:, g:g + 1] for g in range(B_GROUPS)],
            [lg[g:g + 1, :] for g in range(B_GROUPS)], [lb[g:g + 1, :] for g in range(B_GROUPS)],
            _cols(qm, 128), _cols(km, 128), _cols(vm, 128))


def _tri():
    return lax.broadcasted_iota(jnp.int32, (WINDOW, WINDOW), 0) >= lax.broadcasted_iota(jnp.int32, (WINDOW, WINDOW), 1)


def mixer_b_fwd(proj, ws, bt, lg, lb, kv, name):
    t, nm = proj.shape[0], kv.shape[0]

    def body(*refs):
        o_ref = refs[-1]
        o_ref[...] = jnp.concatenate(_mix_b(*_mix_b_args(refs[:-1]), _tri()), axis=1).astype(o_ref.dtype)

    return pl.pallas_call(
        body, name=name, grid=(t // WINDOW,), in_specs=_mix_b_specs(nm),
        out_specs=pl.BlockSpec((WINDOW, B_W + MEM_W), lambda n: (n, 0)),
        out_shape=_sds((t, B_W + MEM_W), BF16), compiler_params=_cp("parallel"))(proj, proj, ws, bt, lg, lb, kv, kv)


def mixer_b_bwd(proj, dcat, ws, bt, lg, lb, kv, name, rider=None):
    t, nm = proj.shape[0], kv.shape[0]

    def body(*refs):
        dcat_ref, dproj_ref, dws_ref, dbt_ref, dlg_ref, dlb_ref, dkv_ref = refs[8:]

        @pl.when(pl.program_id(0) == 0)
        def _():
            for r in (dws_ref, dbt_ref, dlg_ref, dlb_ref, dkv_ref):
                r[...] = jnp.zeros_like(r)

        tri = _tri()
        zus, zvs, ws, bcols, lgs, lbs, qms, kms, vms = _mix_b_args(refs[:8])
        douts = _cols(dcat_ref[...].astype(F32), 128)
        grads = []
        for g in range(B_GROUPS):
            _, vjp = jax.vjp(lambda *a: _gmlp_group(*a, tri), zus[g], zvs[g], ws[g], bcols[g], lgs[g], lbs[g])
            grads.append(vjp(douts[g]))
        dzus, dzvs, dws, dbcols, dlgs, dlbs = [list(t) for t in zip(*grads)]
        grads = []
        for j in range(MEM_HEADS // 2):
            _, vjp = jax.vjp(_mem_pair, qms[j], kms[j], vms[j])
            grads.append(vjp(douts[B_GROUPS + j]))
        dqms, dkms, dvms = [list(t) for t in zip(*grads)]
        dproj_ref[...] = jnp.concatenate(dzus + dzvs + dqms, axis=1).astype(dproj_ref.dtype)
        for g in range(B_GROUPS):
            dws_ref[g] += dws[g]
        lanes = lax.broadcasted_iota(jnp.int32, (WINDOW, 128), 1)
        dbt = jnp.zeros((WINDOW, 128), F32)
        for g in range(B_GROUPS):
            dbt = dbt + jnp.where(lanes == g, jnp.broadcast_to(dbcols[g], (WINDOW, 128)), 0.0)
        dbt_ref[...] += dbt
        dlg_ref[...] += _onehot_rows(dlgs, (8, 128))
        dlb_ref[...] += _onehot_rows(dlbs, (8, 128))
        dkv_ref[...] += jnp.concatenate(dkms + dvms, axis=1)

    width = 2 * B_W + MEM_W
    const2 = lambda n: (0, 0)
    grid = (t // WINDOW,)
    body, r_ops, r_in, r_shapes, r_out, r_scratch = with_rider(body, 9, 6, grid, rider)
    return pl.pallas_call(
        body, name=name, grid=grid,
        in_specs=_mix_b_specs(nm) + [pl.BlockSpec((WINDOW, B_W + MEM_W), lambda n: (n, 0))] + r_in,
        out_specs=[pl.BlockSpec((WINDOW, width), lambda n: (n, 0)),
                   pl.BlockSpec((B_GROUPS, WINDOW, WINDOW), lambda n: (0, 0, 0)),
                   pl.BlockSpec((WINDOW, 128), const2), pl.BlockSpec((8, 128), const2), pl.BlockSpec((8, 128), const2),
                   pl.BlockSpec((nm, 2 * MEM_W), const2)] + r_out,
        out_shape=[_sds((t, width), BF16), _sds((B_GROUPS, WINDOW, WINDOW), F32), _sds((WINDOW, 128), F32),
                   _sds((8, 128), F32), _sds((8, 128), F32), _sds((nm, 2 * MEM_W), F32)] + r_shapes,
        scratch_shapes=r_scratch, compiler_params=_cp("arbitrary"))(proj, proj, ws, bt, lg, lb, kv, kv, dcat, *r_ops)


def _adamw_update(w, g, m, v):
    m2 = ADAM_B1 * m + (1.0 - ADAM_B1) * g
    v2 = ADAM_B2 * v + (1.0 - ADAM_B2) * jnp.square(g)
    m_hat = m2 / (1.0 - ADAM_B1 ** ADAM_STEP)
    v_hat = v2 / (1.0 - ADAM_B2 ** ADAM_STEP)
    return -ADAM_LR * (m_hat / (jnp.sqrt(v_hat) + ADAM_EPS) + ADAM_WD * w), m2, v2


def adamw(w, g, m, v, name):
    r, c = w.shape
    tr = _tile(r, (512, 352, 256, 128, 64, 32, 16, 8))

    def body(w_ref, g_ref, m_ref, v_ref, d_ref, nm_ref, nv_ref):
        d_ref[...], nm_ref[...], nv_ref[...] = _adamw_update(w_ref[...], g_ref[...], m_ref[...], v_ref[...])

    spec = pl.BlockSpec((tr, c), lambda i: (i, 0))
    return pl.pallas_call(
        body, name=name, grid=(r // tr,), in_specs=[spec] * 4, out_specs=[spec] * 3,
        out_shape=[_sds((r, c), F32)] * 3, compiler_params=_cp("parallel"))(w, g, m, v)


def adamw_halves(w, g_mine, g_theirs, m, v, c_arr, rows, name):
    r, c = w.shape
    tr = _tile(rows // 2, (256, 352, 224, 160, 128, 64, 32, 16, 8))
    per_half = rows // 2 // tr

    def body(c_ref, w_ref, gm_ref, gt_ref, m_ref, v_ref, g_ref, d_ref, nm_ref, nv_ref):
        g = jnp.where(pl.program_id(0) // per_half % 2 == c_ref[0], gm_ref[...], gt_ref[...])
        g_ref[...] = g
        d_ref[...], nm_ref[...], nv_ref[...] = _adamw_update(w_ref[...], g, m_ref[...], v_ref[...])

    spec = pl.BlockSpec((tr, c), lambda i, cr: (i, 0))
    half = pl.BlockSpec((tr, c), lambda i, cr: (i // (2 * per_half) * per_half + i % per_half, 0))
    return pl.pallas_call(
        body, name=name,
        grid_spec=pltpu.PrefetchScalarGridSpec(num_scalar_prefetch=1, grid=(r // tr,), in_specs=[spec, half, half, spec, spec],
                                               out_specs=[spec] * 4),
        out_shape=[_sds((r, c), F32)] * 4, compiler_params=_cp("parallel"))(c_arr, w, g_mine, g_theirs, m, v)


def _place():
    return lax.axis_index("x"), lax.axis_index("y"), lax.axis_index("c")


def _other_chips(x, y):
    return [(1 - x, y), (x, 1 - y), (1 - x, 1 - y)]


def _remote(src, dst, send_sems, recv_sems, k, dev):
    return pltpu.make_async_remote_copy(src_ref=src, dst_ref=dst, send_sem=send_sems.at[k], recv_sem=recv_sems.at[k],
                                        device_id=dev, device_id_type=MESH)


class Exchange:
    def __init__(self, ins, out_shapes, n_sems, start, finish):
        self.ins, self.out_shapes, self.start, self.finish = list(ins), list(out_shapes), start, finish
        self.sems = [n_sems, n_sems] if isinstance(n_sems, int) else list(n_sems)

    def scratch(self):
        return [pltpu.SemaphoreType.DMA((n,)) for n in self.sems]


def both_exchanges(a, b):
    ni, no, ns = len(a.ins), len(a.out_shapes), len(a.sems)

    def start(ins, outs, *sems):
        a.start(ins[:ni], outs[:no], *sems[:ns])
        b.start(ins[ni:], outs[no:], *sems[ns:])

    def finish(ins, outs, *sems):
        a.finish(ins[:ni], outs[:no], *sems[:ns])
        b.finish(ins[ni:], outs[no:], *sems[ns:])

    return Exchange(a.ins + b.ins, a.out_shapes + b.out_shapes, a.sems + b.sems, start, finish)


def run_exchange(ex, name):
    ni, no = len(ex.ins), len(ex.out_shapes)

    def body(*refs):
        ex.start(refs[:ni], refs[ni:ni + no], *refs[ni + no:])
        ex.finish(refs[:ni], refs[ni:ni + no], *refs[ni + no:])

    return pl.pallas_call(
        body, name=name, in_specs=[HBM_SPEC] * ni, out_specs=[HBM_SPEC] * no, out_shape=ex.out_shapes, scratch_shapes=ex.scratch(),
        compiler_params=pltpu.CompilerParams(has_side_effects=True))(*ex.ins)


def with_rider(body, n_in, n_out, grid, ex):
    if ex is None:
        return body, [], [], [], [], []
    ni, no, ns = len(ex.ins), len(ex.out_shapes), len(ex.sems)

    def riding(*refs):
        r_in, r_out, sems = refs[n_in:n_in + ni], refs[n_in + ni + n_out:n_in + ni + n_out + no], refs[-ns:]
        first = last = None
        for axis, size in enumerate(grid):
            at_first, at_last = pl.program_id(axis) == 0, pl.program_id(axis) == size - 1
            first = at_first if first is None else jnp.logical_and(first, at_first)
            last = at_last if last is None else jnp.logical_and(last, at_last)

        @pl.when(first)
        def _():
            ex.start(r_in, r_out, *sems)

        body(*refs[:n_in], *refs[n_in + ni:n_in + ni + n_out], *refs[n_in + ni + n_out + no:-ns])

        @pl.when(last)
        def _():
            ex.finish(r_in, r_out, *sems)

    return riding, ex.ins, [HBM_SPEC] * ni, ex.out_shapes, [HBM_SPEC] * no, ex.scratch()


def gather_exchange(shards):
    nw = len(shards)
    entry = [k for _, k in shards]

    def rows(ref, cc):
        return pl.ds(cc * (ref.shape[1] // 2), ref.shape[1] // 2)

    def sent(ins, outs, send_sems, recv_sems, w, j):
        x, y, c = _place()
        return _remote(ins[w].at[pl.ds(entry[w], 1), rows(ins[w], c)], outs[w].at[:, 2 * x + y, rows(ins[w], c)], send_sems, recv_sems,
                       7 * w + j, (*_other_chips(x, y)[j], c))

    def landed(ins, outs, send_sems, recv_sems, w, j, cc, to):
        x, y, c = _place()
        chip = _other_chips(x, y)[j]
        blk = outs[w].at[:, 2 * chip[0] + chip[1], rows(ins[w], cc)]
        return _remote(blk, blk, send_sems, recv_sems, 7 * w + (j if to is None else 3 + j), (x, y, c) if to is None else to)

    def own(ins, outs, send_sems, recv_sems, w):
        x, y, c = _place()
        return _remote(ins[w].at[pl.ds(entry[w], 1)], outs[w].at[:, 2 * x + y], send_sems, recv_sems, 7 * w + 6, (x, y, 1 - c))

    def start(ins, outs, send_sems, recv_sems):
        for j in range(3):
            for w in range(nw):
                sent(ins, outs, send_sems, recv_sems, w, j).start()
        for w in range(nw):
            own(ins, outs, send_sems, recv_sems, w).start()

    def finish(ins, outs, send_sems, recv_sems):
        x, y, c = _place()
        for j in range(3):
            for w in range(nw):
                landed(ins, outs, send_sems, recv_sems, w, j, c, None).wait_recv()
                landed(ins, outs, send_sems, recv_sems, w, j, c, (x, y, 1 - c)).start()
        for w in range(nw):
            own(ins, outs, send_sems, recv_sems, w).wait()
        for j in range(3):
            for w in range(nw):
                landed(ins, outs, send_sems, recv_sems, w, j, 1 - c, (x, y, c)).wait_recv()
        for j in range(3):
            for w in range(nw):
                sent(ins, outs, send_sems, recv_sems, w, j).wait_send()
                landed(ins, outs, send_sems, recv_sems, w, j, c, (x, y, 1 - c)).wait_send()

    return Exchange([s for s, _ in shards], [_sds((1, 4) + s.shape[1:], s.dtype) for s, _ in shards], 7 * nw, start, finish)


def copies_exchange(ins, out_shapes, n_sems, copies):
    def start(*refs):
        for cp in copies(*refs):
            cp.start()

    def finish(*refs):
        for cp in copies(*refs):
            cp.wait()

    return Exchange(ins, out_shapes, n_sems, start, finish)


def sibling_halves_exchange(gs):
    def copies(ins, outs, send_sems, recv_sems):
        x, y, c = _place()
        return [_remote(g.at[:, :, pl.ds((1 - c) * (g.shape[2] // 2), g.shape[2] // 2)], o, send_sems, recv_sems, w, (x, y, 1 - c))
                for w, (g, o) in enumerate(zip(ins, outs))]

    return copies_exchange(gs, [_sds(g.shape[:2] + (g.shape[2] // 2, g.shape[3]), g.dtype) for g in gs], len(gs), copies)


def chips_exchange(sbs):
    def copies(ins, outs, send_sems, recv_sems):
        x, y, c = _place()
        return [_remote(s.at[:, 2 * chip[0] + chip[1]], o.at[j], send_sems, recv_sems, 3 * w + j, (*chip, c))
                for j, chip in enumerate(_other_chips(x, y)) for w, (s, o) in enumerate(zip(ins, outs))]

    return copies_exchange(sbs, [_sds((3, s.shape[0]) + s.shape[2:], s.dtype) for s in sbs], 3 * len(sbs), copies)


def sibling_exchange(fs):
    def copies(ins, outs, send_sems, recv_sems):
        x, y, c = _place()
        return [_remote(f, o, send_sems, recv_sems, w, (x, y, 1 - c)) for w, (f, o) in enumerate(zip(ins, outs))]

    return copies_exchange(fs, [_sds(f.shape, f.dtype) for f in fs], len(fs), copies)


def _half_tile(a):
    return _tile(a, (256, 352, 224, 176, 160, 128, 64, 32, 16))


def chip_partial_sums(g, r1, c_arr, name):
    nl, _, a2, b = r1.shape
    ta = _half_tile(a2)
    per = a2 // ta

    def body(c_ref, g_ref, r_ref, o_ref):
        o_ref[...] = (g_ref[...] + r_ref[...]).astype(o_ref.dtype)

    blk = (None, None, ta, b)
    return pl.pallas_call(
        body, name=name,
        grid_spec=pltpu.PrefetchScalarGridSpec(
            num_scalar_prefetch=1, grid=(nl, 4, per),
            in_specs=[pl.BlockSpec(blk, lambda l, s, i, c: (l, s, c[0] * per + i, 0)), pl.BlockSpec(blk, lambda l, s, i, c: (l, s, i, 0))],
            out_specs=pl.BlockSpec(blk, lambda l, s, i, c: (l, s, i, 0))),
        out_shape=_sds(r1.shape, BF16), compiler_params=_cp("parallel", "parallel", "parallel"))(c_arr, g, r1)


def shard_total(g, r1, r2, cs_arr, name):
    nl, _, a2, b = r1.shape
    ta = _half_tile(a2)
    per = a2 // ta

    def body(cs_ref, g_ref, r1_ref, p0_ref, p1_ref, p2_ref, o_ref):
        o_ref[...] = (((g_ref[...] + r1_ref[...]) + p0_ref[...].astype(F32)) + p1_ref[...].astype(F32)) + p2_ref[...].astype(F32)

    blk4, blk3 = (None, None, ta, b), (None, ta, b)
    peer = lambda k: pl.BlockSpec((None, None, ta, b), lambda l, i, cs: (k, l, i, 0))
    return pl.pallas_call(
        body, name=name,
        grid_spec=pltpu.PrefetchScalarGridSpec(
            num_scalar_prefetch=1, grid=(nl, per),
            in_specs=[pl.BlockSpec(blk4, lambda l, i, cs: (l, cs[1], cs[0] * per + i, 0)),
                      pl.BlockSpec(blk4, lambda l, i, cs: (l, cs[1], i, 0)), peer(0), peer(1), peer(2)],
            out_specs=pl.BlockSpec(blk3, lambda l, i, cs: (l, i, 0))),
        out_shape=_sds((nl, a2, b), F32), compiler_params=_cp("parallel", "parallel"))(cs_arr, g, r1, r2, r2, r2)


def allgather_small(v, name):
    r, n = v.shape

    def body(x_ref, out_ref, send_sems, recv_sems, local_sem):
        x, y, c = _place()
        me, sibling = (x, y, c), (x, y, 1 - c)
        chips = _other_chips(x, y)

        def rows(px, py, pc):
            return out_ref.at[pl.ds((4 * px + 2 * py + pc) * r, r), :]

        def copy(k, block, to, src=None):
            return _remote(rows(*block) if src is None else src, rows(*block), send_sems, recv_sems, k, to)

        mine = pltpu.make_async_copy(x_ref, rows(*me), local_sem)
        mine.start()
        first = [copy(0, me, sibling, src=x_ref)] + [copy(1 + j, me, (*chip, c), src=x_ref) for j, chip in enumerate(chips)]
        for cp in first:
            cp.start()
        passed = [copy(4 + j, (*chip, c), sibling) for j, chip in enumerate(chips)]
        for j, chip in enumerate(chips):
            copy(1 + j, (*chip, c), me).wait_recv()
            passed[j].start()
        copy(0, sibling, me).wait_recv()
        for j, chip in enumerate(chips):
            copy(4 + j, (*chip, 1 - c), me).wait_recv()
        for cp in first + passed:
            cp.wait_send()
        mine.wait()

    return pl.pallas_call(
        body, name=name, in_specs=[VMEM_SPEC], out_specs=VMEM_SPEC, out_shape=_sds((8 * r, n), v.dtype),
        scratch_shapes=[pltpu.SemaphoreType.DMA((7,)), pltpu.SemaphoreType.DMA((7,)), pltpu.SemaphoreType.DMA],
        compiler_params=pltpu.CompilerParams(has_side_effects=True, vmem_limit_bytes=V7X_VMEM_LIMIT_BYTES))(v)


def sum_devices(v8, name):
    _, r, n = v8.shape
    tr = _tile(r, (88, 64, 32, 16, 8))

    def body(v_ref, o_ref):
        acc = v_ref[0]
        for d in range(1, 8):
            acc = acc + v_ref[d]
        o_ref[...] = acc

    return pl.pallas_call(
        body, name=name, grid=(r // tr,), in_specs=[pl.BlockSpec((8, tr, n), lambda i: (0, i, 0))],
        out_specs=pl.BlockSpec((tr, n), lambda i: (i, 0)), out_shape=_sds((r, n), F32), compiler_params=_cp("parallel"))(v8)


SHARDED = (("a_w_in", 2), ("a_w_out", 1), ("b_w_in", 2), ("b_w_out", 1), ("w_mem_kv", 1), ("w_gate_up", 2), ("w_down", 1))


def _usable(wg, axis):
    l, _, a, b = wg.shape
    return wg.reshape(l, 4 * a, b) if axis == 1 else wg


def _pack(arrs):
    parts = []
    for a in arrs:
        flat = a.reshape(-1)
        flat = jnp.pad(flat, (0, -flat.shape[0] % 1024))
        parts.append(flat.reshape(-1, 128))
    return jnp.concatenate(parts, axis=0)


def _unpack(buf, like):
    out, row = [], 0
    for a in like:
        size = 1
        for s in a.shape:
            size *= s
        rows = -(-size // 1024) * 8
        out.append(buf[row:row + rows].reshape(-1)[:size].reshape(a.shape))
        row += rows
    return out


def kernel(x, mem, mem_norm_g, mix_norm_g, ffn_norm_g, final_norm_g, a_w_in, a_sinks, a_w_out, b_w_in, b_w_s, b_bias_s, b_ln_g, b_ln_b, b_w_out, w_mem_kv, w_gate_up, w_down, loss_target, m_mem_norm_g, m_mix_norm_g, m_ffn_norm_g, m_final_norm_g, m_a_w_in, m_a_sinks, m_a_w_out, m_b_w_in, m_b_w_s, m_b_bias_s, m_b_ln_g, m_b_ln_b, m_b_w_out, m_w_mem_kv, m_w_gate_up, m_w_down, v_mem_norm_g, v_mix_norm_g, v_ffn_norm_g, v_final_norm_g, v_a_w_in, v_a_sinks, v_a_w_out, v_b_w_in, v_b_w_s, v_b_bias_s, v_b_ln_g, v_b_ln_b, v_b_w_out, v_w_mem_kv, v_w_gate_up, v_w_down):
    given = dict(locals())
    depth = mix_norm_g.shape[0]
    d = x.shape[-1]
    xi, yi, ci = _place()
    c_arr = jnp.stack([ci]).astype(jnp.int32)
    cs_arr = jnp.stack([ci, 2 * xi + yi]).astype(jnp.int32)

    axis_of = dict(SHARDED)
    own = {n: given[n].astype(BF16) for n, _ in SHARDED}
    MIXER, FFN = slice(0, 3), slice(3, 5)

    def layer_weights(l):
        mix = "a" if l % 2 == 0 else "b"
        return [(mix + "_w_in", l // 2), (mix + "_w_out", l // 2), ("w_mem_kv", l), ("w_gate_up", l), ("w_down", l)]

    def gather_of(l, part=slice(0, 5)):
        return gather_exchange([(own[n], k) for n, k in layer_weights(l)[part]])

    def usable(l, gathered, part=slice(0, 5)):
        return {n[2:] if n[0] in "ab" else n: (_usable(wg, axis_of[n]), 0) for (n, _), wg in zip(layer_weights(l)[part], gathered)}

    weights = {0: usable(0, run_exchange(gather_of(0, MIXER), "gather_weights"), MIXER)}

    h = x.reshape(-1, d)
    tgt = loss_target.reshape(-1, d)
    mem2 = mem.reshape(-1, d)
    row = lambda v: v.reshape(1, -1)

    mem_n = rmsnorm_fwd(mem2, row(mem_norm_g), "mem_norm")
    saved = []
    for i in range(depth):
        j = i // 2
        wl = weights[i]
        w_in, w_out = wl["w_in"], wl["w_out"]
        kv = matmul(mem_n, wl["w_mem_kv"], "nn", BF16, "mem_kv")
        if i % 2 == 0:
            sk = jnp.pad(jnp.broadcast_to(a_sinks[j][:, None], (Q_HEADS, 128)), ((0, 16 - Q_HEADS), (0, 0)))
            xn, proj = norm_matmul(h, row(mix_norm_g[i]), w_in, BF16, "a_in")
            cat, *gathered = mixer_a_fwd(proj, sk, kv, "mixer_a", rider=gather_of(0, FFN) if i == 0 else None)
            if gathered:
                wl.update(usable(0, gathered, FFN))
            extra = (sk,)
        else:
            bt = jnp.pad(b_bias_s[j].T, ((0, 0), (0, 128 - B_GROUPS)))
            lg = jnp.pad(b_ln_g[j], ((0, 8 - B_GROUPS), (0, 0)))
            lb = jnp.pad(b_ln_b[j], ((0, 8 - B_GROUPS), (0, 0)))
            xn, proj = norm_matmul(h, row(mix_norm_g[i]), w_in, BF16, "b_in")
            cat = mixer_b_fwd(proj, b_w_s[j], bt, lg, lb, kv, "mixer_b")
            extra = (b_w_s[j], bt, lg, lb)
        h_mid = matmul(cat, w_out, "nn", F32, "mix_out", res=h)
        more = i + 1 < depth
        hn, gu, act, *gathered = gate_up_fwd(h_mid, row(ffn_norm_g[i]), *wl["w_gate_up"], "gate_up",
                                             rider=gather_of(i + 1, FFN) if more else None)
        h_out, *gathered_mixer = matmul(act, wl["w_down"], "nn", F32, "down", res=h_mid, rider=gather_of(i + 1, MIXER)) if more \
            else [matmul(act, wl["w_down"], "nn", F32, "down", res=h_mid)]
        if more:
            weights[i + 1] = {**usable(i + 1, gathered, FFN), **usable(i + 1, gathered_mixer, MIXER)}
        saved.append((h, xn, proj, cat, h_mid, hn, gu, act, kv, extra))
        h = h_out

    loss_part, dh, d_final_g, dh16 = loss_head(h, row(final_norm_g), tgt, "loss_head")
    loss = lax.psum(loss_part[0, 0], ("x", "y", "c"))

    d_mix_g, d_ffn_g = [None] * depth, [None] * depth
    d_sinks, d_ws, d_bias, d_lg, d_lb = [], [], [], [], []
    d_mem_n = jnp.zeros(mem2.shape, F32)
    totals = [None] * depth
    pending = None
    for i in reversed(range(depth)):
        h_in, xn, proj, cat, h_mid, hn, gu, act, kv, extra = saved[i]
        wl = weights[i]
        dgu, *from_sibling = down_dx_swiglu_bwd(dh16, wl["w_down"], gu, "down_dx",
                                                rider=sibling_halves_exchange(pending) if pending else None)
        if pending:
            partial = [chip_partial_sums(g, r1, c_arr, "grads_chip_sum") for g, r1 in zip(pending, from_sibling)]
        dw_down = matmul(act, dh16, "tn", F32, "down_dw", tm=1408, tk=2048, out_planes=("rows", 4))
        dw_gate_up = matmul((hn, 0), dgu, "tn", F32, "gate_up_dw", tn=1408, tk=2048, out_planes=("cols", 4))
        ffn = [dw_gate_up[None], dw_down[None]] if i == 0 else []
        joined = lambda exs: None if not exs else exs[0] if len(exs) == 1 else both_exchanges(*exs)
        dh, d_ffn_g[i], dh16, *landed = dx_norm_bwd(
            dgu, wl["w_gate_up"], h_mid, row(ffn_norm_g[i]), dh, "gate_up_dx",
            rider=joined(([chips_exchange(partial[FFN])] if pending else []) + ([sibling_halves_exchange(ffn)] if ffn else [])))
        if pending:
            chips_ffn = landed[:2]
        if ffn:
            ffn_sibling = landed[-len(ffn):]
            ffn_partial = [chip_partial_sums(g, r1, c_arr, "grads_chip_sum") for g, r1 in zip(ffn, ffn_sibling)]
        dcat = matmul(dh16, wl["w_out"], "nt", BF16, "mix_out_dx")
        dw_out = matmul(cat, dh16, "tn", F32, "mix_out_dw", tk=2048, out_planes=("rows", 4))
        mixer_rider = joined(([chips_exchange(partial[MIXER])] if pending else []) + ([chips_exchange(ffn_partial)] if ffn else []))
        if i % 2 == 0:
            dproj, dsk, dkv, *landed = mixer_a_bwd(proj, dcat, extra[0], kv, "mixer_a_bwd", rider=mixer_rider)
            d_sinks.insert(0, dsk[:Q_HEADS, 0])
            dw_in = matmul(dproj, xn, "tn", F32, "a_in_dw", out_planes=("rows", 4))
        else:
            dproj, dws, dbt, dlg, dlb, dkv, *landed = mixer_b_bwd(proj, dcat, *extra, kv, "mixer_b_bwd", rider=mixer_rider)
            d_ws.insert(0, dws)
            d_bias.insert(0, dbt[:, :B_GROUPS].T)
            d_lg.insert(0, dlg[:B_GROUPS])
            d_lb.insert(0, dlb[:B_GROUPS])
            dw_in = matmul(dproj, xn, "tn", F32, "b_in_dw", out_planes=("rows", 4))
        if pending:
            from_chips = landed[:3] + chips_ffn
            totals[i + 1] = [shard_total(g, r1, r2, cs_arr, "grads_shard_total") for g, r1, r2 in zip(pending, from_sibling, from_chips)]
        if ffn:
            ffn_totals = [shard_total(g, r1, r2, cs_arr, "grads_shard_total") for g, r1, r2 in zip(ffn, ffn_sibling, landed[-len(ffn):])]
        dw_kv = matmul(mem_n, dkv, "tn", F32, "mem_kv_dw", out_planes=("rows", 4))
        d_mem_n = matmul(dkv, wl["w_mem_kv"], "nt", F32, "mem_kv_dx", res=d_mem_n)
        dh, d_mix_g[i], dh16 = dx_norm_bwd(dproj, wl["w_in"], h_in, row(mix_norm_g[i]), dh, "in_dx")
        pending = [dw_in[None], dw_out[None], dw_kv[None]] + ([] if ffn else [dw_gate_up[None], dw_down[None]])
    grad_x = dh.reshape(x.shape)
    _, d_mem_g = rmsnorm_bwd(mem2, row(mem_norm_g), d_mem_n, jnp.zeros(mem2.shape, F32), "mem_norm_bwd")

    from_sibling = run_exchange(sibling_halves_exchange(pending), "grads_sibling_swap")
    partial = [chip_partial_sums(g, r1, c_arr, "grads_chip_sum") for g, r1 in zip(pending, from_sibling)]
    from_chips = run_exchange(chips_exchange(partial), "grads_chips_exchange")
    totals[0] = [shard_total(g, r1, r2, cs_arr, "grads_shard_total") for g, r1, r2 in zip(pending, from_sibling, from_chips)] + ffn_totals

    mine = {n: [None] * given[n].shape[0] for n, _ in SHARDED}
    for l in range(depth):
        for (n, k), tot in zip(layer_weights(l), totals[l]):
            mine[n][k] = tot
    mine = [jnp.concatenate(mine[n], axis=0) for n, _ in SHARDED]
    theirs = run_exchange(sibling_exchange(mine), "grads_sibling_totals")
    out = {}
    for (n, _), g_mine, g_theirs in zip(SHARDED, mine, theirs):
        flip = (lambda a: jnp.swapaxes(a, 1, 2)) if n in ("a_w_in", "b_w_in") else (lambda a: a)
        shape = flip(given[n]).shape
        two_d = lambda a: a.reshape(-1, shape[-1])
        res = adamw_halves(two_d(flip(given[n])), two_d(g_mine), two_d(g_theirs), two_d(flip(given["m_" + n])),
                           two_d(flip(given["v_" + n])), c_arr, shape[1], "adamw")
        out[n] = tuple(flip(r.reshape(shape)) for r in res)

    small = ("mem_norm_g", "mix_norm_g", "ffn_norm_g", "final_norm_g", "a_sinks", "b_w_s", "b_bias_s", "b_ln_g", "b_ln_b")
    small_g = [d_mem_g[0], jnp.concatenate(d_mix_g, axis=0), jnp.concatenate(d_ffn_g, axis=0), d_final_g[0],
               jnp.stack(d_sinks), jnp.stack(d_ws), jnp.stack(d_bias), jnp.stack(d_lg), jnp.stack(d_lb)]
    packed = _pack(small_g)
    g_small = sum_devices(allgather_small(packed, "small_allgather").reshape(8, *packed.shape), "small_sum")
    like = [given[n] for n in small]
    delta_s, new_m_s, new_v_s = adamw(_pack(like), g_small, _pack([given["m_" + n] for n in small]),
                                      _pack([given["v_" + n] for n in small]), "adamw_small")
    for n, g, dl, nm_, nv_ in zip(small, _unpack(g_small, like), _unpack(delta_s, like), _unpack(new_m_s, like), _unpack(new_v_s, like)):
        out[n] = (g, dl, nm_, nv_)

    order = ("mem_norm_g", "mix_norm_g", "ffn_norm_g", "final_norm_g", "a_w_in", "a_sinks", "a_w_out", "b_w_in", "b_w_s",
             "b_bias_s", "b_ln_g", "b_ln_b", "b_w_out", "w_mem_kv", "w_gate_up", "w_down")
    return (loss, grad_x, *[out[n][0] for n in order], *[out[n][1] for n in order],
            *[out[n][2] for n in order], *[out[n][3] for n in order])
```

```python
import jax
import jax.numpy as jnp
from jax import lax
from jax.experimental import pallas as pl
from jax.experimental.pallas import tpu as pltpu

F32, BF16 = jnp.float32, jnp.bfloat16
EPS = 1e-6
HEAD_DIM = 64
Q_HEADS, KV_HEADS, GROUP = 12, 2, 6
WINDOW = 128
MEM_HEADS = 4
B_GROUPS = 6
Q_W, KV_W, MEM_W, B_W = 768, 128, 256, 768
SCALE = HEAD_DIM ** -0.5
NEG = -1e30
ADAM_LR, ADAM_B1, ADAM_B2, ADAM_EPS, ADAM_WD, ADAM_STEP = 0.001, 0.9, 0.999, 1e-08, 0.01, 10
V7X_VMEM_LIMIT_BYTES = 48 * 1024 * 1024
MESH = pl.DeviceIdType.MESH
HBM_SPEC = pl.BlockSpec(memory_space=pltpu.HBM)
VMEM_SPEC = pl.BlockSpec(memory_space=pltpu.VMEM)


def _cp(*sem):
    return pltpu.CompilerParams(dimension_semantics=sem or None, vmem_limit_bytes=V7X_VMEM_LIMIT_BYTES)


def _tile(n, cands):
    for t in cands:
        if n % t == 0:
            return t
    return n


def _sds(shape, dtype):
    return jax.ShapeDtypeStruct(tuple(shape), dtype)


def _dot(a, b, ca, cb):
    return lax.dot_general(a, b, (((ca,), (cb,)), ((), ())), preferred_element_type=F32)


def _rms(x, g):
    return x * lax.rsqrt(jnp.mean(x * x, axis=-1, keepdims=True) + EPS) * g


def rmsnorm_fwd(h, g, name):
    t, d = h.shape
    tm = _tile(t, (512, 256, 128))

    def body(h_ref, g_ref, o_ref):
        o_ref[...] = _rms(h_ref[...], g_ref[...]).astype(o_ref.dtype)

    return pl.pallas_call(
        body, name=name, grid=(t // tm,),
        in_specs=[pl.BlockSpec((tm, d), lambda i: (i, 0)), pl.BlockSpec((1, d), lambda i: (0, 0))],
        out_specs=pl.BlockSpec((tm, d), lambda i: (i, 0)),
        out_shape=_sds((t, d), BF16), compiler_params=_cp("parallel"))(h, g)


def rmsnorm_bwd(h, g, dxn, dres, name):
    t, d = h.shape
    tm = _tile(t, (512, 256, 128))

    def body(h_ref, g_ref, dxn_ref, dres_ref, dh_ref, dg_ref):
        _, vjp = jax.vjp(_rms, h_ref[...], g_ref[...])
        dh, dg = vjp(dxn_ref[...].astype(F32))
        dh_ref[...] = dres_ref[...] + dh

        @pl.when(pl.program_id(0) == 0)
        def _():
            dg_ref[...] = jnp.zeros_like(dg_ref)

        dg_ref[...] += dg

    row = pl.BlockSpec((tm, d), lambda i: (i, 0))
    vec = pl.BlockSpec((1, d), lambda i: (0, 0))
    return pl.pallas_call(
        body, name=name, grid=(t // tm,), in_specs=[row, vec, row, row], out_specs=[row, vec],
        out_shape=[_sds((t, d), F32), _sds((1, d), F32)], compiler_params=_cp("arbitrary"))(h, g, dxn, dres)


def loss_head(h, g, tgt, name):
    t, d = h.shape
    tm = _tile(t, (512, 256, 128))

    def body(h_ref, g_ref, t_ref, l_ref, dh_ref, dg_ref, dh16_ref):
        y, vjp = jax.vjp(_rms, h_ref[...], g_ref[...])
        err = y - t_ref[...]
        dh, dg = vjp(err * (1.0 / d))
        dh_ref[...] = dh
        dh16_ref[...] = dh.astype(BF16)
        part = 0.5 * jnp.sum(jnp.mean(err * err, axis=-1, keepdims=True), axis=0, keepdims=True)

        @pl.when(pl.program_id(0) == 0)
        def _():
            dg_ref[...] = jnp.zeros_like(dg_ref)
            l_ref[...] = jnp.zeros_like(l_ref)

        dg_ref[...] += dg
        l_ref[...] += part

    row = pl.BlockSpec((tm, d), lambda i: (i, 0))
    vec = pl.BlockSpec((1, d), lambda i: (0, 0))
    one = pl.BlockSpec((1, 1), lambda i: (0, 0))
    return pl.pallas_call(
        body, name=name, grid=(t // tm,), in_specs=[row, vec, row], out_specs=[one, row, vec, row],
        out_shape=[_sds((1, 1), F32), _sds((t, d), F32), _sds((1, d), F32), _sds((t, d), BF16)],
        compiler_params=_cp("arbitrary"))(h, g, tgt)


def _logical(op):
    arr, lead = op if isinstance(op, tuple) else (op, None)
    planes = arr.shape[-3] if arr.ndim - (lead is not None) == 3 else 1
    return arr, lead, arr.shape[-2], arr.shape[-1], planes


def _spec(op, rows_t, cols_t, row_of, col_of):
    arr, lead, _, cols, _ = _logical(op)
    per = cols // cols_t
    lead = () if lead is None else (lead,)
    if arr.ndim - len(lead) == 2:
        return pl.BlockSpec((None,) * len(lead) + (rows_t, cols_t), lambda *g: lead + (row_of(*g), col_of(*g)))
    return pl.BlockSpec((None,) * len(lead) + (None, rows_t, cols_t),
                        lambda *g: lead + (col_of(*g) // per, row_of(*g), col_of(*g) % per))


def _arr(op):
    return op[0] if isinstance(op, tuple) else op


def _resident_whole(w, d):
    wa, layer = w
    planes, per = wa.shape[-3], wa.shape[-1]

    def fill(w_ref, whole_ref):
        for s in range(planes):
            whole_ref[:, s * per:(s + 1) * per] = w_ref[s]

    return (pl.BlockSpec((None, planes, d, per), lambda i: (layer, 0, 0, 0), pipeline_mode=pl.Buffered(1)),
            pltpu.VMEM((d, planes * per), wa.dtype), fill)


def norm_matmul(h, g, w, out_dtype, name):
    t, d = h.shape
    w_spec, whole, fill = _resident_whole(w, d)
    n = whole.shape[1]
    tm = _tile(t, (512, 256, 128))

    def body(h_ref, g_ref, w_ref, xn_ref, o_ref, whole_ref):
        @pl.when(pl.program_id(0) == 0)
        def _():
            fill(w_ref, whole_ref)

        xn = _rms(h_ref[...], g_ref[...]).astype(BF16)
        xn_ref[...] = xn
        o_ref[...] = _dot(xn, whole_ref[...], 1, 0).astype(o_ref.dtype)

    return pl.pallas_call(
        body, name=name, grid=(t // tm,),
        in_specs=[pl.BlockSpec((tm, d), lambda i: (i, 0)), pl.BlockSpec((1, d), lambda i: (0, 0)), w_spec],
        out_specs=[pl.BlockSpec((tm, d), lambda i: (i, 0)), pl.BlockSpec((tm, n), lambda i: (i, 0))],
        out_shape=[_sds((t, d), BF16), _sds((t, n), out_dtype)], scratch_shapes=[whole],
        compiler_params=_cp("arbitrary"))(h, g, w[0])


def dx_norm_bwd(dy, w, h, g, dres, name, rider=None):
    t, d = h.shape
    dy_arr, dy_lead, _, kc, kp = _logical(dy)
    w_arr, w_lead, _, wc, wp = _logical(w)
    assert kc * kp == wc * wp and dy_lead is None, name
    chunk = min(kc, wc)
    tm = _tile(t, (512, 256, 128))

    def piece(ref, planes, cols, q):
        off = q * chunk % cols
        return ref[q * chunk // cols, :, off:off + chunk] if planes > 1 else ref[:, off:off + chunk]

    narrow = wp > 1 and wc % 128 != 0
    if narrow:
        assert kp == 1, name
        w_whole_spec, whole, fill = _resident_whole(w, d)

    def body(dy_ref, w_ref, h_ref, g_ref, dres_ref, dh_ref, dg_ref, dh16_ref, *whole_ref):
        if narrow:
            @pl.when(pl.program_id(0) == 0)
            def _():
                fill(w_ref, whole_ref[0])

            dxn = _dot(dy_ref[...].astype(BF16), whole_ref[0][...], 1, 1)
        else:
            dxn = None
            for q in range(kc * kp // chunk):
                p = _dot(piece(dy_ref, kp, kc, q).astype(BF16), piece(w_ref, wp, wc, q), 1, 1)
                dxn = p if dxn is None else dxn + p
        _, vjp = jax.vjp(_rms, h_ref[...], g_ref[...])
        dh, dg = vjp(dxn)
        dh = dres_ref[...] + dh
        dh_ref[...] = dh
        dh16_ref[...] = dh.astype(BF16)

        @pl.when(pl.program_id(0) == 0)
        def _():
            dg_ref[...] = jnp.zeros_like(dg_ref)

        dg_ref[...] += dg

    w_lead = () if w_lead is None else (w_lead,)
    w_block = ((wp,) if wp > 1 else ()) + (d, wc)
    w_spec = pl.BlockSpec((None,) * len(w_lead) + w_block, lambda i: w_lead + (0,) * len(w_block), pipeline_mode=pl.Buffered(1))
    if narrow:
        w_spec = w_whole_spec
    dy_spec = pl.BlockSpec((kp, tm, kc), lambda i: (0, i, 0)) if kp > 1 else pl.BlockSpec((tm, kc), lambda i: (i, 0))
    row = pl.BlockSpec((tm, d), lambda i: (i, 0))
    vec = pl.BlockSpec((1, d), lambda i: (0, 0))
    grid = (t // tm,)
    body, r_ops, r_in, r_shapes, r_out, r_scratch = with_rider(body, 5, 3, grid, rider)
    return pl.pallas_call(
        body, name=name, grid=grid, in_specs=[dy_spec, w_spec, row, vec, row] + r_in,
        out_specs=[row, vec, row] + r_out, out_shape=[_sds((t, d), F32), _sds((1, d), F32), _sds((t, d), BF16)] + r_shapes,
        scratch_shapes=([whole] if narrow else []) + r_scratch,
        compiler_params=_cp("arbitrary"))(dy_arr, w_arr, h, g, dres, *r_ops)


def matmul(a, b, mode, out_dtype, name, res=None, tm=None, tn=1792, tk=2816, out_planes=None, out_into=None, rider=None):
    _, _, ar, ac, ap = _logical(a)
    _, _, br, bc, bp = _logical(b)
    if mode == "nn":
        m, ka, kb, n = ar, ac * ap, br, bc * bp
        n_plane, ka_plane, kb_plane = bc, ac, br
    elif mode == "nt":
        m, ka, n, kb = ar, ac * ap, br, bc * bp
        n_plane, ka_plane, kb_plane = br, ac, bc
    else:
        ka, m, kb, n = ar, ac * ap, br, bc * bp
        n_plane, ka_plane, kb_plane = bc, ar, br
    m_plane = ac if mode == "tn" else ar
    assert ka == kb, name
    k = ka
    kind, planes = out_planes or ("cols", 1)
    narrow = kind == "cols" and (n // planes) % 128 != 0
    if kind == "cols" and not narrow:
        n_plane = min(n_plane, n // planes)
    if narrow:
        tn = n
    tm = _tile(m_plane, ((1024, 1408, 512, 256, 128) if mode == "tn" else (512, 256, 128)) if tm is None else (tm, 1024, 512, 256, 128))
    if kind == "rows" and tm % (m // planes):
        tm = m_plane
    tn = _tile(n_plane, (tn, 1792, 1408, 1280, 1024, 896, 640, 512, 256, 128))
    tk = _tile(min(ka_plane, kb_plane), (tk, 2816, 1792, 1408, 1280, 1024, 512, 256, 128))
    nk = k // tk
    row_i, col_j, red = (lambda i, j, kk: i), (lambda i, j, kk: j), (lambda i, j, kk: kk)
    if mode == "nn":
        a_spec, b_spec, ca, cb = _spec(a, tm, tk, row_i, red), _spec(b, tk, tn, red, col_j), 1, 0
    elif mode == "nt":
        a_spec, b_spec, ca, cb = _spec(a, tm, tk, row_i, red), _spec(b, tn, tk, col_j, red), 1, 1
    else:
        a_spec, b_spec, ca, cb = _spec(a, tk, tm, red, row_i), _spec(b, tk, tn, red, col_j), 0, 0
    lead = () if out_into is None else (out_into[1],)
    if planes == 1:
        o_shape, o_block = (m, n), (tm, tn)
        o_index = lambda i, j, kk: lead + (i, j)
    elif narrow:
        o_shape, o_block = (planes, m, n // planes), (planes, tm, n // planes)
        o_index = lambda i, j, kk: lead + (0, i, 0)
    elif kind == "cols":
        per = n // planes // tn
        o_shape, o_block = (planes, m, n // planes), (None, tm, tn)
        o_index = lambda i, j, kk: lead + (j // per, i, j % per)
    else:
        o_shape, o_block = (planes, m // planes, n), (tm // (m // planes), m // planes, tn)
        o_index = lambda i, j, kk: lead + (i, 0, j)
    o_spec = pl.BlockSpec((None,) * len(lead) + o_block, o_index)
    if out_into is not None:
        assert out_into[0].shape[1:] == o_shape and out_into[0].dtype == out_dtype, name
        o_shape = out_into[0].shape
    has_res = res is not None
    n_in = 2 + has_res + (out_into is not None)

    def put(o_ref, v):
        if narrow:
            for s in range(planes):
                o_ref[s] = v[:, s * (n // planes):(s + 1) * (n // planes)].astype(o_ref.dtype)
        else:
            o_ref[...] = v.astype(o_ref.dtype).reshape(o_ref.shape)

    def body(*refs):
        a_ref, b_ref = refs[:2]
        rest = refs[2:2 + has_res] + refs[n_in:]
        o_ref = rest[1] if has_res else rest[0]
        p = _dot(a_ref[...].astype(BF16), b_ref[...].astype(BF16), ca, cb)
        if nk == 1:
            if has_res:
                p = p + rest[0][...]
            put(o_ref, p)
        else:
            acc_ref = rest[-1]
            kk = pl.program_id(2)

            @pl.when(kk == 0)
            def _():
                acc_ref[...] = p

            @pl.when(kk > 0)
            def _():
                acc_ref[...] += p

            @pl.when(kk == nk - 1)
            def _():
                r = acc_ref[...]
                if has_res:
                    r = r + rest[0][...]
                put(o_ref, r)

    operands = [_arr(a), _arr(b)] + ([res] if has_res else []) + ([out_into[0]] if out_into is not None else [])
    grid = (m // tm, n // tn, nk)
    body, r_ops, r_in, r_shapes, r_out, r_scratch = with_rider(body, n_in, 1, grid, rider)
    out = pl.pallas_call(
        body, name=name, grid=grid,
        in_specs=[a_spec, b_spec] + ([pl.BlockSpec((tm, tn), lambda i, j, kk: (i, j))] if has_res else [])
        + ([pl.BlockSpec(memory_space=pl.ANY)] if out_into is not None else []) + r_in,
        out_specs=[o_spec] + r_out, out_shape=[_sds(o_shape, out_dtype)] + r_shapes,
        input_output_aliases={n_in - 1: 0} if out_into is not None else {},
        scratch_shapes=([pltpu.VMEM((tm, tn), F32)] if nk > 1 else []) + r_scratch,
        compiler_params=_cp(*(("arbitrary",) * 3 if rider else ("parallel", "parallel", "arbitrary"))))(*operands, *r_ops)
    return out if rider else out[0]


def gate_up_fwd(h, g, w, layer, name, rider=None):
    t, d = h.shape
    half = w.shape[-1]
    tm = _tile(t, (512, 256, 128))

    def body(h_ref, g_ref, wg_ref, wu_ref, hn_ref, gu_ref, act_ref):
        a = _rms(h_ref[...], g_ref[...]).astype(BF16)
        hn_ref[...] = a
        gate, up = _dot(a, wg_ref[...], 1, 0), _dot(a, wu_ref[...], 1, 0)
        sig = 1.0 / (1.0 + jnp.exp(-gate))
        silu = gate * sig
        gu_ref[0] = (up * (sig + silu * (1.0 - sig))).astype(gu_ref.dtype)
        gu_ref[1] = silu.astype(gu_ref.dtype)
        act_ref[...] = (silu * up).astype(act_ref.dtype)

    grid = (2, t // tm)
    body, r_ops, r_in, r_shapes, r_out, r_scratch = with_rider(body, 4, 3, grid, rider)
    return pl.pallas_call(
        body, name=name, grid=grid,
        in_specs=[pl.BlockSpec((tm, d), lambda j, i: (i, 0)), pl.BlockSpec((1, d), lambda j, i: (0, 0)),
                  pl.BlockSpec((None, None, d, half), lambda j, i: (layer, j, 0, 0)),
                  pl.BlockSpec((None, None, d, half), lambda j, i: (layer, 2 + j, 0, 0))] + r_in,
        out_specs=[pl.BlockSpec((None, tm, d), lambda j, i: (j, i, 0)), pl.BlockSpec((2, tm, half), lambda j, i: (0, i, j)),
                   pl.BlockSpec((tm, half), lambda j, i: (i, j))] + r_out,
        out_shape=[_sds((2, t, d), BF16), _sds((2, t, 2 * half), BF16), _sds((t, 2 * half), BF16)] + r_shapes,
        scratch_shapes=r_scratch, compiler_params=_cp("arbitrary", "arbitrary"))(h, g, w, w, *r_ops)


def down_dx_swiglu_bwd(dh, wd, gu, name, rider=None):
    t, d = dh.shape
    w, layer = wd
    f = w.shape[-2]
    tm = _tile(t, (512, 256, 128))
    tn = _tile(f, (1408, 512, 256, 128))

    def body(dh_ref, w_ref, gu_ref, o_ref):
        dh = dh_ref[...].astype(BF16)
        for j in range(f // tn):
            cols = slice(j * tn, (j + 1) * tn)
            dact = _dot(dh, w_ref[cols, :], 1, 1)
            o_ref[0, :, cols] = (dact * gu_ref[0, :, cols].astype(F32)).astype(o_ref.dtype)
            o_ref[1, :, cols] = (dact * gu_ref[1, :, cols].astype(F32)).astype(o_ref.dtype)

    planes = pl.BlockSpec((2, tm, f), lambda i: (0, i, 0))
    grid = (t // tm,)
    body, r_ops, r_in, r_shapes, r_out, r_scratch = with_rider(body, 3, 1, grid, rider)
    return pl.pallas_call(
        body, name=name, grid=grid,
        in_specs=[pl.BlockSpec((tm, d), lambda i: (i, 0)),
                  pl.BlockSpec((None, f, d), lambda i: (layer, 0, 0), pipeline_mode=pl.Buffered(1)), planes] + r_in,
        out_specs=[planes] + r_out, out_shape=[_sds((2, t, f), BF16)] + r_shapes, scratch_shapes=r_scratch,
        compiler_params=_cp("arbitrary"))(dh, w, gu, *r_ops)


def _softmax_over_keys(s, sink=None):
    m = s.max(axis=0, keepdims=True)
    if sink is not None:
        m = jnp.maximum(m, sink)
    m = lax.stop_gradient(m)
    e = jnp.exp(s - m)
    den = e.sum(axis=0, keepdims=True)
    if sink is not None:
        den = den + jnp.exp(sink - m)
    return e * (1.0 / den)


def _low_lanes():
    return lax.broadcasted_iota(jnp.int32, (1, 128), 1) < HEAD_DIM


def _stack_heads(slabs):
    low = _low_lanes()
    return jnp.concatenate([p for s in slabs for p in (jnp.where(low, s, 0.0), jnp.where(low, 0.0, s))], axis=0)


def _unstack_heads(o, n_slabs):
    low = _low_lanes()
    return [jnp.where(low, o[2 * j * WINDOW:(2 * j + 1) * WINDOW], o[(2 * j + 1) * WINDOW:(2 * j + 2) * WINDOW])
            for j in range(n_slabs)]


def _swa_group(q_slabs, k_both, v_both, sinks, mask):
    qs = _stack_heads(q_slabs).astype(BF16)
    s = jnp.where(mask, _dot(k_both.astype(BF16), qs, 1, 1) * SCALE, NEG)
    sink = jnp.concatenate([jnp.broadcast_to(v, (1, WINDOW)) for v in sinks], axis=1)
    return _unstack_heads(_dot(_softmax_over_keys(s, sink).astype(BF16), v_both.astype(BF16), 0, 0), len(q_slabs))


def _mem_pair(q_slab, k_slab, v_slab):
    s = _dot(k_slab.astype(BF16), _stack_heads([q_slab]).astype(BF16), 1, 1) * SCALE
    return _unstack_heads(_dot(_softmax_over_keys(s).astype(BF16), v_slab.astype(BF16), 0, 0), 1)[0]


def _gelu(x):
    return 0.5 * x * (1.0 + jnp.tanh(0.7978845608028654 * (x + 0.044715 * (x * x * x))))


def _gmlp_group(zu, zv, w, bcol, lg, lb, tri):
    u, v = _gelu(zu), _gelu(zv)
    mu = jnp.mean(v, axis=-1, keepdims=True)
    var = jnp.mean(jnp.square(v - mu), axis=-1, keepdims=True)
    vn = (v - mu) * lax.rsqrt(var + EPS) * lg + lb
    sv = _dot(jnp.where(tri, w, 0.0).astype(BF16), vn.astype(BF16), 1, 0) + bcol
    return u * sv


def _cols(x, width):
    return [x[:, j * width:(j + 1) * width] for j in range(x.shape[1] // width)]


def _swa_mask(has_prev):
    qi = lax.broadcasted_iota(jnp.int32, (2 * WINDOW, GROUP * WINDOW), 1) & (WINDOW - 1)
    kj = lax.broadcasted_iota(jnp.int32, (2 * WINDOW, GROUP * WINDOW), 0)
    in_prev = jnp.logical_and(jnp.logical_and(kj < WINDOW, kj > qi), has_prev)
    return jnp.logical_or(in_prev, jnp.logical_and(kj >= WINDOW, kj - WINDOW <= qi))


def _mix_a(q_slabs, k_boths, v_boths, sinks, qm_slabs, km_slabs, vm_slabs, mask):
    per = GROUP // 2
    outs = []
    for g in range(KV_HEADS):
        outs += _swa_group(q_slabs[per * g:per * (g + 1)], k_boths[g], v_boths[g], sinks[GROUP * g:GROUP * (g + 1)], mask)
    return outs + [_mem_pair(qm_slabs[j], km_slabs[j], vm_slabs[j]) for j in range(MEM_HEADS // 2)]


def _in_both_halves(prev, cur):
    cat = jnp.concatenate([prev, cur], axis=0)
    rolled = pltpu.roll(cat, HEAD_DIM, axis=1)
    low = _low_lanes()
    return [jnp.where(low, cat, rolled), jnp.where(low, rolled, cat)]


def _from_both_halves(d_boths):
    t = [d + pltpu.roll(d, HEAD_DIM, axis=1) for d in d_boths]
    return jnp.where(_low_lanes(), t[0], t[1])


def _mix_a_specs(nm, blk):
    prev = lambda n: jnp.maximum(blk(n) - 1, 0)
    return [pl.BlockSpec((WINDOW, Q_W), lambda n: (blk(n), 0)),
            pl.BlockSpec((WINDOW, KV_W), lambda n: (prev(n), Q_W // KV_W)),
            pl.BlockSpec((WINDOW, KV_W), lambda n: (blk(n), Q_W // KV_W)),
            pl.BlockSpec((WINDOW, KV_W), lambda n: (prev(n), Q_W // KV_W + 1)),
            pl.BlockSpec((WINDOW, KV_W), lambda n: (blk(n), Q_W // KV_W + 1)),
            pl.BlockSpec((WINDOW, MEM_W), lambda n: (blk(n), (Q_W + 2 * KV_W) // MEM_W)),
            pl.BlockSpec((16, 128), lambda n: (0, 0)),
            pl.BlockSpec((nm, MEM_W), lambda n: (0, 0)),
            pl.BlockSpec((nm, MEM_W), lambda n: (0, 1))]


def _mix_a_args(refs):
    q, kp, kc, vp, vc, qm, sk, km, vm = [r[...].astype(F32) for r in refs]
    return (_cols(q, 128), _in_both_halves(kp, kc), _in_both_halves(vp, vc), [sk[h:h + 1, 0:1] for h in range(Q_HEADS)],
            _cols(qm, 128), _cols(km, 128), _cols(vm, 128))


def mixer_a_fwd(proj, sk, kv, name, rider=None):
    t, nm = proj.shape[0], kv.shape[0]

    def body(*refs):
        o_ref = refs[-1]
        slabs = _mix_a(*_mix_a_args(refs[:-1]), _swa_mask(pl.program_id(0) > 0))
        o_ref[...] = jnp.concatenate(slabs, axis=1).astype(o_ref.dtype)

    grid = (t // WINDOW,)
    body, r_ops, r_in, r_shapes, r_out, r_scratch = with_rider(body, 9, 1, grid, rider)
    return pl.pallas_call(
        body, name=name, grid=grid, in_specs=_mix_a_specs(nm, lambda n: n) + r_in,
        out_specs=[pl.BlockSpec((WINDOW, Q_W + MEM_W), lambda n: (n, 0))] + r_out,
        out_shape=[_sds((t, Q_W + MEM_W), BF16)] + r_shapes, scratch_shapes=r_scratch,
        compiler_params=_cp("arbitrary"))(proj, proj, proj, proj, proj, proj, sk, kv, kv, *r_ops)


def _onehot_rows(vals, shape):
    rows = lax.broadcasted_iota(jnp.int32, shape, 0)
    out = jnp.zeros(shape, F32)
    for h, v in enumerate(vals):
        out = out + jnp.where(rows == h, jnp.broadcast_to(v, shape), 0.0)
    return out


def mixer_a_bwd(proj, dcat, sk, kv, name, rider=None):
    t, nm = proj.shape[0], kv.shape[0]
    nb = t // WINDOW
    blk = lambda i: nb - 1 - i

    def body(*refs):
        dcat_ref, dproj_ref, dsk_ref, dkv_ref, carry_ref = refs[9:]
        i = pl.program_id(0)

        @pl.when(i == 0)
        def _():
            carry_ref[...] = jnp.zeros_like(carry_ref)
            dsk_ref[...] = jnp.zeros_like(dsk_ref)
            dkv_ref[...] = jnp.zeros_like(dkv_ref)

        mask = _swa_mask(blk(i) > 0)
        _, vjp = jax.vjp(lambda *a: _mix_a(*a, mask), *_mix_a_args(refs[:9]))
        dqs, dk_boths, dv_boths, dsinks, dqms, dkms, dvms = vjp(_cols(dcat_ref[...].astype(F32), 128))
        dkv = jnp.concatenate([_from_both_halves(dk_boths), _from_both_halves(dv_boths)], axis=1)
        dkv_cur = dkv[WINDOW:] + carry_ref[...]
        carry_ref[...] = dkv[:WINDOW]
        dproj_ref[...] = jnp.concatenate(dqs + [dkv_cur] + dqms, axis=1).astype(dproj_ref.dtype)
        dsk_ref[...] += _onehot_rows(dsinks, (16, 128))
        dkv_ref[...] += jnp.concatenate(dkms + dvms, axis=1)

    width = Q_W + 2 * KV_W + MEM_W
    body, r_ops, r_in, r_shapes, r_out, r_scratch = with_rider(body, 10, 3, (nb,), rider)
    return pl.pallas_call(
        body, name=name, grid=(nb,),
        in_specs=_mix_a_specs(nm, blk) + [pl.BlockSpec((WINDOW, Q_W + MEM_W), lambda i: (blk(i), 0))] + r_in,
        out_specs=[pl.BlockSpec((WINDOW, width), lambda i: (blk(i), 0)), pl.BlockSpec((16, 128), lambda i: (0, 0)),
                   pl.BlockSpec((nm, 2 * MEM_W), lambda i: (0, 0))] + r_out,
        out_shape=[_sds((t, width), BF16), _sds((16, 128), F32), _sds((nm, 2 * MEM_W), F32)] + r_shapes,
        scratch_shapes=[pltpu.VMEM((WINDOW, 2 * KV_W), F32)] + r_scratch,
        compiler_params=_cp("arbitrary"))(proj, proj, proj, proj, proj, proj, sk, kv, kv, dcat, *r_ops)


def _mix_b(zus, zvs, ws, bcols, lgs, lbs, qms, kms, vms, tri):
    outs = [_gmlp_group(zus[g], zvs[g], ws[g], bcols[g], lgs[g], lbs[g], tri) for g in range(B_GROUPS)]
    return outs + [_mem_pair(qms[j], kms[j], vms[j]) for j in range(MEM_HEADS // 2)]


def _mix_b_specs(nm):
    return [pl.BlockSpec((WINDOW, 2 * B_W), lambda n: (n, 0)),
            pl.BlockSpec((WINDOW, MEM_W), lambda n: (n, 2 * B_W // MEM_W)),
            pl.BlockSpec((B_GROUPS, WINDOW, WINDOW), lambda n: (0, 0, 0)),
            pl.BlockSpec((WINDOW, 128), lambda n: (0, 0)),
            pl.BlockSpec((8, 128), lambda n: (0, 0)),
            pl.BlockSpec((8, 128), lambda n: (0, 0)),
            pl.BlockSpec((nm, MEM_W), lambda n: (0, 0)),
            pl.BlockSpec((nm, MEM_W), lambda n: (0, 1))]


def _mix_b_args(refs):
    z, qm, ws, bt, lg, lb, km, vm = [r[...].astype(F32) for r in refs]
    zs = _cols(z, 128)
    return (zs[:B_GROUPS], zs[B_GROUPS:], [ws[g] for g in range(B_GROUPS)], [bt[:, g:g + 1] for g in range(B_GROUPS)],
            [lg[g:g + 1, :] for g in range(B_GROUPS)], [lb[g:g + 1, :] for g in range(B_GROUPS)],
            _cols(qm, 128), _cols(km, 128), _cols(vm, 128))


def _tri():
    return lax.broadcasted_iota(jnp.int32, (WINDOW, WINDOW), 0) >= lax.broadcasted_iota(jnp.int32, (WINDOW, WINDOW), 1)


def mixer_b_fwd(proj, ws, bt, lg, lb, kv, name):
    t, nm = proj.shape[0], kv.shape[0]

    def body(*refs):
        o_ref = refs[-1]
        o_ref[...] = jnp.concatenate(_mix_b(*_mix_b_args(refs[:-1]), _tri()), axis=1).astype(o_ref.dtype)

    return pl.pallas_call(
        body, name=name, grid=(t // WINDOW,), in_specs=_mix_b_specs(nm),
        out_specs=pl.BlockSpec((WINDOW, B_W + MEM_W), lambda n: (n, 0)),
        out_shape=_sds((t, B_W + MEM_W), BF16), compiler_params=_cp("parallel"))(proj, proj, ws, bt, lg, lb, kv, kv)


def mixer_b_bwd(proj, dcat, ws, bt, lg, lb, kv, name, rider=None):
    t, nm = proj.shape[0], kv.shape[0]

    def body(*refs):
        dcat_ref, dproj_ref, dws_ref, dbt_ref, dlg_ref, dlb_ref, dkv_ref = refs[8:]

        @pl.when(pl.program_id(0) == 0)
        def _():
            for r in (dws_ref, dbt_ref, dlg_ref, dlb_ref, dkv_ref):
                r[...] = jnp.zeros_like(r)

        tri = _tri()
        zus, zvs, ws, bcols, lgs, lbs, qms, kms, vms = _mix_b_args(refs[:8])
        douts = _cols(dcat_ref[...].astype(F32), 128)
        grads = []
        for g in range(B_GROUPS):
            _, vjp = jax.vjp(lambda *a: _gmlp_group(*a, tri), zus[g], zvs[g], ws[g], bcols[g], lgs[g], lbs[g])
            grads.append(vjp(douts[g]))
        dzus, dzvs, dws, dbcols, dlgs, dlbs = [list(t) for t in zip(*grads)]
        grads = []
        for j in range(MEM_HEADS // 2):
            _, vjp = jax.vjp(_mem_pair, qms[j], kms[j], vms[j])
            grads.append(vjp(douts[B_GROUPS + j]))
        dqms, dkms, dvms = [list(t) for t in zip(*grads)]
        dproj_ref[...] = jnp.concatenate(dzus + dzvs + dqms, axis=1).astype(dproj_ref.dtype)
        for g in range(B_GROUPS):
            dws_ref[g] += dws[g]
        lanes = lax.broadcasted_iota(jnp.int32, (WINDOW, 128), 1)
        dbt = jnp.zeros((WINDOW, 128), F32)
        for g in range(B_GROUPS):
            dbt = dbt + jnp.where(lanes == g, jnp.broadcast_to(dbcols[g], (WINDOW, 128)), 0.0)
        dbt_ref[...] += dbt
        dlg_ref[...] += _onehot_rows(dlgs, (8, 128))
        dlb_ref[...] += _onehot_rows(dlbs, (8, 128))
        dkv_ref[...] += jnp.concatenate(dkms + dvms, axis=1)

    width = 2 * B_W + MEM_W
    const2 = lambda n: (0, 0)
    grid = (t // WINDOW,)
    body, r_ops, r_in, r_shapes, r_out, r_scratch = with_rider(body, 9, 6, grid, rider)
    return pl.pallas_call(
        body, name=name, grid=grid,
        in_specs=_mix_b_specs(nm) + [pl.BlockSpec((WINDOW, B_W + MEM_W), lambda n: (n, 0))] + r_in,
        out_specs=[pl.BlockSpec((WINDOW, width), lambda n: (n, 0)),
                   pl.BlockSpec((B_GROUPS, WINDOW, WINDOW), lambda n: (0, 0, 0)),
                   pl.BlockSpec((WINDOW, 128), const2), pl.BlockSpec((8, 128), const2), pl.BlockSpec((8, 128), const2),
                   pl.BlockSpec((nm, 2 * MEM_W), const2)] + r_out,
        out_shape=[_sds((t, width), BF16), _sds((B_GROUPS, WINDOW, WINDOW), F32), _sds((WINDOW, 128), F32),
                   _sds((8, 128), F32), _sds((8, 128), F32), _sds((nm, 2 * MEM_W), F32)] + r_shapes,
        scratch_shapes=r_scratch, compiler_params=_cp("arbitrary"))(proj, proj, ws, bt, lg, lb, kv, kv, dcat, *r_ops)


def _adamw_update(w, g, m, v):
    m2 = ADAM_B1 * m + (1.0 - ADAM_B1) * g
    v2 = ADAM_B2 * v + (1.0 - ADAM_B2) * jnp.square(g)
    m_hat = m2 / (1.0 - ADAM_B1 ** ADAM_STEP)
    v_hat = v2 / (1.0 - ADAM_B2 ** ADAM_STEP)
    return -ADAM_LR * (m_hat / (jnp.sqrt(v_hat) + ADAM_EPS) + ADAM_WD * w), m2, v2


def adamw(w, g, m, v, name):
    r, c = w.shape
    tr = _tile(r, (512, 352, 256, 128, 64, 32, 16, 8))

    def body(w_ref, g_ref, m_ref, v_ref, d_ref, nm_ref, nv_ref):
        d_ref[...], nm_ref[...], nv_ref[...] = _adamw_update(w_ref[...], g_ref[...], m_ref[...], v_ref[...])

    spec = pl.BlockSpec((tr, c), lambda i: (i, 0))
    return pl.pallas_call(
        body, name=name, grid=(r // tr,), in_specs=[spec] * 4, out_specs=[spec] * 3,
        out_shape=[_sds((r, c), F32)] * 3, compiler_params=_cp("parallel"))(w, g, m, v)


def adamw_halves(w, g_mine, g_theirs, m, v, c_arr, rows, name):
    r, c = w.shape
    tr = _tile(rows // 2, (256, 352, 224, 160, 128, 64, 32, 16, 8))
    per_half = rows // 2 // tr

    def body(c_ref, w_ref, gm_ref, gt_ref, m_ref, v_ref, g_ref, d_ref, nm_ref, nv_ref):
        g = jnp.where(pl.program_id(0) // per_half % 2 == c_ref[0], gm_ref[...], gt_ref[...])
        g_ref[...] = g
        d_ref[...], nm_ref[...], nv_ref[...] = _adamw_update(w_ref[...], g, m_ref[...], v_ref[...])

    spec = pl.BlockSpec((tr, c), lambda i, cr: (i, 0))
    half = pl.BlockSpec((tr, c), lambda i, cr: (i // (2 * per_half) * per_half + i % per_half, 0))
    return pl.pallas_call(
        body, name=name,
        grid_spec=pltpu.PrefetchScalarGridSpec(num_scalar_prefetch=1, grid=(r // tr,), in_specs=[spec, half, half, spec, spec],
                                               out_specs=[spec] * 4),
        out_shape=[_sds((r, c), F32)] * 4, compiler_params=_cp("parallel"))(c_arr, w, g_mine, g_theirs, m, v)


def _place():
    return lax.axis_index("x"), lax.axis_index("y"), lax.axis_index("c")


def _other_chips(x, y):
    return [(1 - x, y), (x, 1 - y), (1 - x, 1 - y)]


def _remote(src, dst, send_sems, recv_sems, k, dev):
    return pltpu.make_async_remote_copy(src_ref=src, dst_ref=dst, send_sem=send_sems.at[k], recv_sem=recv_sems.at[k],
                                        device_id=dev, device_id_type=MESH)


class Exchange:
    def __init__(self, ins, out_shapes, n_sems, start, finish):
        self.ins, self.out_shapes, self.start, self.finish = list(ins), list(out_shapes), start, finish
        self.sems = [n_sems, n_sems] if isinstance(n_sems, int) else list(n_sems)

    def scratch(self):
        return [pltpu.SemaphoreType.DMA((n,)) for n in self.sems]


def both_exchanges(a, b):
    ni, no, ns = len(a.ins), len(a.out_shapes), len(a.sems)

    def start(ins, outs, *sems):
        a.start(ins[:ni], outs[:no], *sems[:ns])
        b.start(ins[ni:], outs[no:], *sems[ns:])

    def finish(ins, outs, *sems):
        a.finish(ins[:ni], outs[:no], *sems[:ns])
        b.finish(ins[ni:], outs[no:], *sems[ns:])

    return Exchange(a.ins + b.ins, a.out_shapes + b.out_shapes, a.sems + b.sems, start, finish)


def run_exchange(ex, name):
    ni, no = len(ex.ins), len(ex.out_shapes)

    def body(*refs):
        ex.start(refs[:ni], refs[ni:ni + no], *refs[ni + no:])
        ex.finish(refs[:ni], refs[ni:ni + no], *refs[ni + no:])

    return pl.pallas_call(
        body, name=name, in_specs=[HBM_SPEC] * ni, out_specs=[HBM_SPEC] * no, out_shape=ex.out_shapes, scratch_shapes=ex.scratch(),
        compiler_params=pltpu.CompilerParams(has_side_effects=True))(*ex.ins)


def with_rider(body, n_in, n_out, grid, ex):
    if ex is None:
        return body, [], [], [], [], []
    ni, no, ns = len(ex.ins), len(ex.out_shapes), len(ex.sems)

    def riding(*refs):
        r_in, r_out, sems = refs[n_in:n_in + ni], refs[n_in + ni + n_out:n_in + ni + n_out + no], refs[-ns:]
        first = last = None
        for axis, size in enumerate(grid):
            at_first, at_last = pl.program_id(axis) == 0, pl.program_id(axis) == size - 1
            first = at_first if first is None else jnp.logical_and(first, at_first)
            last = at_last if last is None else jnp.logical_and(last, at_last)

        @pl.when(first)
        def _():
            ex.start(r_in, r_out, *sems)

        body(*refs[:n_in], *refs[n_in + ni:n_in + ni + n_out], *refs[n_in + ni + n_out + no:-ns])

        @pl.when(last)
        def _():
            ex.finish(r_in, r_out, *sems)

    return riding, ex.ins, [HBM_SPEC] * ni, ex.out_shapes, [HBM_SPEC] * no, ex.scratch()


def gather_exchange(shards):
    nw = len(shards)
    entry = [k for _, k in shards]

    def rows(ref, cc):
        return pl.ds(cc * (ref.shape[1] // 2), ref.shape[1] // 2)

    def sent(ins, outs, send_sems, recv_sems, w, j):
        x, y, c = _place()
        return _remote(ins[w].at[pl.ds(entry[w], 1), rows(ins[w], c)], outs[w].at[:, 2 * x + y, rows(ins[w], c)], send_sems, recv_sems,
                       7 * w + j, (*_other_chips(x, y)[j], c))

    def landed(ins, outs, send_sems, recv_sems, w, j, cc, to):
        x, y, c = _place()
        chip = _other_chips(x, y)[j]
        blk = outs[w].at[:, 2 * chip[0] + chip[1], rows(ins[w], cc)]
        return _remote(blk, blk, send_sems, recv_sems, 7 * w + (j if to is None else 3 + j), (x, y, c) if to is None else to)

    def own(ins, outs, send_sems, recv_sems, w):
        x, y, c = _place()
        return _remote(ins[w].at[pl.ds(entry[w], 1)], outs[w].at[:, 2 * x + y], send_sems, recv_sems, 7 * w + 6, (x, y, 1 - c))

    def start(ins, outs, send_sems, recv_sems):
        for j in range(3):
            for w in range(nw):
                sent(ins, outs, send_sems, recv_sems, w, j).start()
        for w in range(nw):
            own(ins, outs, send_sems, recv_sems, w).start()

    def finish(ins, outs, send_sems, recv_sems):
        x, y, c = _place()
        for j in range(3):
            for w in range(nw):
                landed(ins, outs, send_sems, recv_sems, w, j, c, None).wait_recv()
                landed(ins, outs, send_sems, recv_sems, w, j, c, (x, y, 1 - c)).start()
        for w in range(nw):
            own(ins, outs, send_sems, recv_sems, w).wait()
        for j in range(3):
            for w in range(nw):
                landed(ins, outs, send_sems, recv_sems, w, j, 1 - c, (x, y, c)).wait_recv()
        for j in range(3):
            for w in range(nw):
                sent(ins, outs, send_sems, recv_sems, w, j).wait_send()
                landed(ins, outs, send_sems, recv_sems, w, j, c, (x, y, 1 - c)).wait_send()

    return Exchange([s for s, _ in shards], [_sds((1, 4) + s.shape[1:], s.dtype) for s, _ in shards], 7 * nw, start, finish)


def copies_exchange(ins, out_shapes, n_sems, copies):
    def start(*refs):
        for cp in copies(*refs):
            cp.start()

    def finish(*refs):
        for cp in copies(*refs):
            cp.wait()

    return Exchange(ins, out_shapes, n_sems, start, finish)


def sibling_halves_exchange(gs):
    def copies(ins, outs, send_sems, recv_sems):
        x, y, c = _place()
        return [_remote(g.at[:, :, pl.ds((1 - c) * (g.shape[2] // 2), g.shape[2] // 2)], o, send_sems, recv_sems, w, (x, y, 1 - c))
                for w, (g, o) in enumerate(zip(ins, outs))]

    return copies_exchange(gs, [_sds(g.shape[:2] + (g.shape[2] // 2, g.shape[3]), g.dtype) for g in gs], len(gs), copies)


def chips_exchange(sbs):
    def copies(ins, outs, send_sems, recv_sems):
        x, y, c = _place()
        return [_remote(s.at[:, 2 * chip[0] + chip[1]], o.at[j], send_sems, recv_sems, 3 * w + j, (*chip, c))
                for j, chip in enumerate(_other_chips(x, y)) for w, (s, o) in enumerate(zip(ins, outs))]

    return copies_exchange(sbs, [_sds((3, s.shape[0]) + s.shape[2:], s.dtype) for s in sbs], 3 * len(sbs), copies)


def sibling_exchange(fs):
    def copies(ins, outs, send_sems, recv_sems):
        x, y, c = _place()
        return [_remote(f, o, send_sems, recv_sems, w, (x, y, 1 - c)) for w, (f, o) in enumerate(zip(ins, outs))]

    return copies_exchange(fs, [_sds(f.shape, f.dtype) for f in fs], len(fs), copies)


def _half_tile(a):
    return _tile(a, (256, 352, 224, 176, 160, 128, 64, 32, 16))


def chip_partial_sums(g, r1, c_arr, name):
    nl, _, a2, b = r1.shape
    ta = _half_tile(a2)
    per = a2 // ta

    def body(c_ref, g_ref, r_ref, o_ref):
        o_ref[...] = (g_ref[...] + r_ref[...]).astype(o_ref.dtype)

    blk = (None, None, ta, b)
    return pl.pallas_call(
        body, name=name,
        grid_spec=pltpu.PrefetchScalarGridSpec(
            num_scalar_prefetch=1, grid=(nl, 4, per),
            in_specs=[pl.BlockSpec(blk, lambda l, s, i, c: (l, s, c[0] * per + i, 0)), pl.BlockSpec(blk, lambda l, s, i, c: (l, s, i, 0))],
            out_specs=pl.BlockSpec(blk, lambda l, s, i, c: (l, s, i, 0))),
        out_shape=_sds(r1.shape, BF16), compiler_params=_cp("parallel", "parallel", "parallel"))(c_arr, g, r1)


def shard_total(g, r1, r2, cs_arr, name):
    nl, _, a2, b = r1.shape
    ta = _half_tile(a2)
    per = a2 // ta

    def body(cs_ref, g_ref, r1_ref, p0_ref, p1_ref, p2_ref, o_ref):
        o_ref[...] = (((g_ref[...] + r1_ref[...]) + p0_ref[...].astype(F32)) + p1_ref[...].astype(F32)) + p2_ref[...].astype(F32)

    blk4, blk3 = (None, None, ta, b), (None, ta, b)
    peer = lambda k: pl.BlockSpec((None, None, ta, b), lambda l, i, cs: (k, l, i, 0))
    return pl.pallas_call(
        body, name=name,
        grid_spec=pltpu.PrefetchScalarGridSpec(
            num_scalar_prefetch=1, grid=(nl, per),
            in_specs=[pl.BlockSpec(blk4, lambda l, i, cs: (l, cs[1], cs[0] * per + i, 0)),
                      pl.BlockSpec(blk4, lambda l, i, cs: (l, cs[1], i, 0)), peer(0), peer(1), peer(2)],
            out_specs=pl.BlockSpec(blk3, lambda l, i, cs: (l, i, 0))),
        out_shape=_sds((nl, a2, b), F32), compiler_params=_cp("parallel", "parallel"))(cs_arr, g, r1, r2, r2, r2)


def allgather_small(v, name):
    r, n = v.shape

    def body(x_ref, out_ref, send_sems, recv_sems, local_sem):
        x, y, c = _place()
        me, sibling = (x, y, c), (x, y, 1 - c)
        chips = _other_chips(x, y)

        def rows(px, py, pc):
            return out_ref.at[pl.ds((4 * px + 2 * py + pc) * r, r), :]

        def copy(k, block, to, src=None):
            return _remote(rows(*block) if src is None else src, rows(*block), send_sems, recv_sems, k, to)

        mine = pltpu.make_async_copy(x_ref, rows(*me), local_sem)
        mine.start()
        first = [copy(0, me, sibling, src=x_ref)] + [copy(1 + j, me, (*chip, c), src=x_ref) for j, chip in enumerate(chips)]
        for cp in first:
            cp.start()
        passed = [copy(4 + j, (*chip, c), sibling) for j, chip in enumerate(chips)]
        for j, chip in enumerate(chips):
            copy(1 + j, (*chip, c), me).wait_recv()
            passed[j].start()
        copy(0, sibling, me).wait_recv()
        for j, chip in enumerate(chips):
            copy(4 + j, (*chip, 1 - c), me).wait_recv()
        for cp in first + passed:
            cp.wait_send()
        mine.wait()

    return pl.pallas_call(
        body, name=name, in_specs=[VMEM_SPEC], out_specs=VMEM_SPEC, out_shape=_sds((8 * r, n), v.dtype),
        scratch_shapes=[pltpu.SemaphoreType.DMA((7,)), pltpu.SemaphoreType.DMA((7,)), pltpu.SemaphoreType.DMA],
        compiler_params=pltpu.CompilerParams(has_side_effects=True, vmem_limit_bytes=V7X_VMEM_LIMIT_BYTES))(v)


def sum_devices(v8, name):
    _, r, n = v8.shape
    tr = _tile(r, (88, 64, 32, 16, 8))

    def body(v_ref, o_ref):
        acc = v_ref[0]
        for d in range(1, 8):
            acc = acc + v_ref[d]
        o_ref[...] = acc

    return pl.pallas_call(
        body, name=name, grid=(r // tr,), in_specs=[pl.BlockSpec((8, tr, n), lambda i: (0, i, 0))],
        out_specs=pl.BlockSpec((tr, n), lambda i: (i, 0)), out_shape=_sds((r, n), F32), compiler_params=_cp("parallel"))(v8)


SHARDED = (("a_w_in", 2), ("a_w_out", 1), ("b_w_in", 2), ("b_w_out", 1), ("w_mem_kv", 1), ("w_gate_up", 2), ("w_down", 1))


def _usable(wg, axis):
    l, _, a, b = wg.shape
    return wg.reshape(l, 4 * a, b) if axis == 1 else wg


def _pack(arrs):
    parts = []
    for a in arrs:
        flat = a.reshape(-1)
        flat = jnp.pad(flat, (0, -flat.shape[0] % 1024))
        parts.append(flat.reshape(-1, 128))
    return jnp.concatenate(parts, axis=0)


def _unpack(buf, like):
    out, row = [], 0
    for a in like:
        size = 1
        for s in a.shape:
            size *= s
        rows = -(-size // 1024) * 8
        out.append(buf[row:row + rows].reshape(-1)[:size].reshape(a.shape))
        row += rows
    return out


def kernel(x, mem, mem_norm_g, mix_norm_g, ffn_norm_g, final_norm_g, a_w_in, a_sinks, a_w_out, b_w_in, b_w_s, b_bias_s, b_ln_g, b_ln_b, b_w_out, w_mem_kv, w_gate_up, w_down, loss_target, m_mem_norm_g, m_mix_norm_g, m_ffn_norm_g, m_final_norm_g, m_a_w_in, m_a_sinks, m_a_w_out, m_b_w_in, m_b_w_s, m_b_bias_s, m_b_ln_g, m_b_ln_b, m_b_w_out, m_w_mem_kv, m_w_gate_up, m_w_down, v_mem_norm_g, v_mix_norm_g, v_ffn_norm_g, v_final_norm_g, v_a_w_in, v_a_sinks, v_a_w_out, v_b_w_in, v_b_w_s, v_b_bias_s, v_b_ln_g, v_b_ln_b, v_b_w_out, v_w_mem_kv, v_w_gate_up, v_w_down):
    given = dict(locals())
    depth = mix_norm_g.shape[0]
    d = x.shape[-1]
    xi, yi, ci = _place()
    c_arr = jnp.stack([ci]).astype(jnp.int32)
    cs_arr = jnp.stack([ci, 2 * xi + yi]).astype(jnp.int32)

    axis_of = dict(SHARDED)
    own = {n: given[n].astype(BF16) for n, _ in SHARDED}
    MIXER, FFN = slice(0, 3), slice(3, 5)

    def layer_weights(l):
        mix = "a" if l % 2 == 0 else "b"
        return [(mix + "_w_in", l // 2), (mix + "_w_out", l // 2), ("w_mem_kv", l), ("w_gate_up", l), ("w_down", l)]

    def gather_of(l, part=slice(0, 5)):
        return gather_exchange([(own[n], k) for n, k in layer_weights(l)[part]])

    def usable(l, gathered, part=slice(0, 5)):
        return {n[2:] if n[0] in "ab" else n: (_usable(wg, axis_of[n]), 0) for (n, _), wg in zip(layer_weights(l)[part], gathered)}

    weights = {0: usable(0, run_exchange(gather_of(0, MIXER), "gather_weights"), MIXER)}

    h = x.reshape(-1, d)
    tgt = loss_target.reshape(-1, d)
    mem2 = mem.reshape(-1, d)
    row = lambda v: v.reshape(1, -1)

    mem_n = rmsnorm_fwd(mem2, row(mem_norm_g), "mem_norm")
    saved = []
    for i in range(depth):
        j = i // 2
        wl = weights[i]
        w_in, w_out = wl["w_in"], wl["w_out"]
        kv = matmul(mem_n, wl["w_mem_kv"], "nn", BF16, "mem_kv")
        if i % 2 == 0:
            sk = jnp.pad(jnp.broadcast_to(a_sinks[j][:, None], (Q_HEADS, 128)), ((0, 16 - Q_HEADS), (0, 0)))
            xn, proj = norm_matmul(h, row(mix_norm_g[i]), w_in, BF16, "a_in")
            cat, *gathered = mixer_a_fwd(proj, sk, kv, "mixer_a", rider=gather_of(0, FFN) if i == 0 else None)
            if gathered:
                wl.update(usable(0, gathered, FFN))
            extra = (sk,)
        else:
            bt = jnp.pad(b_bias_s[j].T, ((0, 0), (0, 128 - B_GROUPS)))
            lg = jnp.pad(b_ln_g[j], ((0, 8 - B_GROUPS), (0, 0)))
            lb = jnp.pad(b_ln_b[j], ((0, 8 - B_GROUPS), (0, 0)))
            xn, proj = norm_matmul(h, row(mix_norm_g[i]), w_in, BF16, "b_in")
            cat = mixer_b_fwd(proj, b_w_s[j], bt, lg, lb, kv, "mixer_b")
            extra = (b_w_s[j], bt, lg, lb)
        h_mid = matmul(cat, w_out, "nn", F32, "mix_out", res=h)
        more = i + 1 < depth
        hn, gu, act, *gathered = gate_up_fwd(h_mid, row(ffn_norm_g[i]), *wl["w_gate_up"], "gate_up",
                                             rider=gather_of(i + 1, FFN) if more else None)
        h_out, *gathered_mixer = matmul(act, wl["w_down"], "nn", F32, "down", res=h_mid, rider=gather_of(i + 1, MIXER)) if more \
            else [matmul(act, wl["w_down"], "nn", F32, "down", res=h_mid)]
        if more:
            weights[i + 1] = {**usable(i + 1, gathered, FFN), **usable(i + 1, gathered_mixer, MIXER)}
        saved.append((h, xn, proj, cat, h_mid, hn, gu, act, kv, extra))
        h = h_out

    loss_part, dh, d_final_g, dh16 = loss_head(h, row(final_norm_g), tgt, "loss_head")
    loss = lax.psum(loss_part[0, 0], ("x", "y", "c"))

    d_mix_g, d_ffn_g = [None] * depth, [None] * depth
    d_sinks, d_ws, d_bias, d_lg, d_lb = [], [], [], [], []
    d_mem_n = jnp.zeros(mem2.shape, F32)
    totals = [None] * depth
    pending = None
    for i in reversed(range(depth)):
        h_in, xn, proj, cat, h_mid, hn, gu, act, kv, extra = saved[i]
        wl = weights[i]
        dgu, *from_sibling = down_dx_swiglu_bwd(dh16, wl["w_down"], gu, "down_dx",
                                                rider=sibling_halves_exchange(pending) if pending else None)
        if pending:
            partial = [chip_partial_sums(g, r1, c_arr, "grads_chip_sum") for g, r1 in zip(pending, from_sibling)]
        dw_down = matmul(act, dh16, "tn", F32, "down_dw", tm=1408, tk=2048, out_planes=("rows", 4))
        dw_gate_up = matmul((hn, 0), dgu, "tn", F32, "gate_up_dw", tn=1408, tk=2048, out_planes=("cols", 4))
        ffn = [dw_gate_up[None], dw_down[None]] if i == 0 else []
        joined = lambda exs: None if not exs else exs[0] if len(exs) == 1 else both_exchanges(*exs)
        dh, d_ffn_g[i], dh16, *landed = dx_norm_bwd(
            dgu, wl["w_gate_up"], h_mid, row(ffn_norm_g[i]), dh, "gate_up_dx",
            rider=joined(([chips_exchange(partial[FFN])] if pending else []) + ([sibling_halves_exchange(ffn)] if ffn else [])))
        if pending:
            chips_ffn = landed[:2]
        if ffn:
            ffn_sibling = landed[-len(ffn):]
            ffn_partial = [chip_partial_sums(g, r1, c_arr, "grads_chip_sum") for g, r1 in zip(ffn, ffn_sibling)]
        dcat = matmul(dh16, wl["w_out"], "nt", F32, "mix_out_dx")
        dw_out = matmul(cat, dh16, "tn", F32, "mix_out_dw", tk=2048, out_planes=("rows", 4))
        mixer_rider = joined(([chips_exchange(partial[MIXER])] if pending else []) + ([chips_exchange(ffn_partial)] if ffn else []))
        if i % 2 == 0:
            dproj, dsk, dkv, *landed = mixer_a_bwd(proj, dcat, extra[0], kv, "mixer_a_bwd", rider=mixer_rider)
            d_sinks.insert(0, dsk[:Q_HEADS, 0])
            dw_in = matmul(dproj, xn, "tn", F32, "a_in_dw", out_planes=("rows", 4))
        else:
            dproj, dws, dbt, dlg, dlb, dkv, *landed = mixer_b_bwd(proj, dcat, *extra, kv, "mixer_b_bwd", rider=mixer_rider)
            d_ws.insert(0, dws)
            d_bias.insert(0, dbt[:, :B_GROUPS].T)
            d_lg.insert(0, dlg[:B_GROUPS])
            d_lb.insert(0, dlb[:B_GROUPS])
            dw_in = matmul(dproj, xn, "tn", F32, "b_in_dw", out_planes=("rows", 4))
        if pending:
            from_chips = landed[:3] + chips_ffn
            totals[i + 1] = [shard_total(g, r1, r2, cs_arr, "grads_shard_total") for g, r1, r2 in zip(pending, from_sibling, from_chips)]
        if ffn:
            ffn_totals = [shard_total(g, r1, r2, cs_arr, "grads_shard_total") for g, r1, r2 in zip(ffn, ffn_sibling, landed[-len(ffn):])]
        dw_kv = matmul(mem_n, dkv, "tn", F32, "mem_kv_dw", out_planes=("rows", 4))
        d_mem_n = matmul(dkv, wl["w_mem_kv"], "nt", F32, "mem_kv_dx", res=d_mem_n)
        dh, d_mix_g[i], dh16 = dx_norm_bwd(dproj, wl["w_in"], h_in, row(mix_norm_g[i]), dh, "in_dx")
        pending = [dw_in[None], dw_out[None], dw_kv[None]] + ([] if ffn else [dw_gate_up[None], dw_down[None]])
    grad_x = dh.reshape(x.shape)
    _, d_mem_g = rmsnorm_bwd(mem2, row(mem_norm_g), d_mem_n, jnp.zeros(mem2.shape, F32), "mem_norm_bwd")

    from_sibling = run_exchange(sibling_halves_exchange(pending), "grads_sibling_swap")
    partial = [chip_partial_sums(g, r1, c_arr, "grads_chip_sum") for g, r1 in zip(pending, from_sibling)]
    from_chips = run_exchange(chips_exchange(partial), "grads_chips_exchange")
    totals[0] = [shard_total(g, r1, r2, cs_arr, "grads_shard_total") for g, r1, r2 in zip(pending, from_sibling, from_chips)] + ffn_totals

    mine = {n: [None] * given[n].shape[0] for n, _ in SHARDED}
    for l in range(depth):
        for (n, k), tot in zip(layer_weights(l), totals[l]):
            mine[n][k] = tot
    mine = [jnp.concatenate(mine[n], axis=0) for n, _ in SHARDED]
    theirs = run_exchange(sibling_exchange(mine), "grads_sibling_totals")
    out = {}
    for (n, _), g_mine, g_theirs in zip(SHARDED, mine, theirs):
        flip = (lambda a: jnp.swapaxes(a, 1, 2)) if n in ("a_w_in", "b_w_in") else (lambda a: a)
        shape = flip(given[n]).shape
        two_d = lambda a: a.reshape(-1, shape[-1])
        res = adamw_halves(two_d(flip(given[n])), two_d(g_mine), two_d(g_theirs), two_d(flip(given["m_" + n])),
                           two_d(flip(given["v_" + n])), c_arr, shape[1], "adamw")
        out[n] = tuple(flip(r.reshape(shape)) for r in res)

    small = ("mem_norm_g", "mix_norm_g", "ffn_norm_g", "final_norm_g", "a_sinks", "b_w_s", "b_bias_s", "b_ln_g", "b_ln_b")
    small_g = [d_mem_g[0], jnp.concatenate(d_mix_g, axis=0), jnp.concatenate(d_ffn_g, axis=0), d_final_g[0],
               jnp.stack(d_sinks), jnp.stack(d_ws), jnp.stack(d_bias), jnp.stack(d_lg), jnp.stack(d_lb)]
    packed = _pack(small_g)
    g_small = sum_devices(allgather_small(packed, "small_allgather").reshape(8, *packed.shape), "small_sum")
    like = [given[n] for n in small]
    delta_s, new_m_s, new_v_s = adamw(_pack(like), g_small, _pack([given["m_" + n] for n in small]),
                                      _pack([given["v_" + n] for n in small]), "adamw_small")
    for n, g, dl, nm_, nv_ in zip(small, _unpack(g_small, like), _unpack(delta_s, like), _unpack(new_m_s, like), _unpack(new_v_s, like)):
        out[n] = (g, dl, nm_, nv_)

    order = ("mem_norm_g", "mix_norm_g", "ffn_norm_g", "final_norm_g", "a_w_in", "a_sinks", "a_w_out", "b_w_in", "b_w_s",
             "b_bias_s", "b_ln_g", "b_ln_b", "b_w_out", "w_mem_kv", "w_gate_up", "w_down")
    return (loss, grad_x, *[out[n][0] for n in order], *[out[n][1] for n in order],
            *[out[n][2] for n in order], *[out[n][3] for n in order])
```
